```python
import jax, jax.numpy as jnp
from jax import lax
import numpy as np

D_MODEL = 2048
BATCH = 8
SEQ = 8192
DEPTH = 2

CTX_LEN = 256
GRID_W = 64
D_MIX = D_MODEL
D_CONV = D_MIX // 2
D_RET = D_MIX - D_CONV
RET_HEADS = 8
RET_HEAD_DIM = D_RET // RET_HEADS
CONV_WIDTH = 3
CHUNK = 128
D_IN = 4 * D_CONV + 4 * D_RET
ROPE_BASE = 10000.0
RET_DECAY_OFFSET = 5.0
EPS = 1e-6

kernel_name = "hybrid_conv_retention_prefix_dit_block"


def _rmsnorm(x, w):
    xf = x.astype(jnp.float32)
    y = xf * lax.rsqrt(jnp.mean(xf * xf, axis=-1, keepdims=True) + EPS)
    return (y * w.astype(jnp.float32)).astype(x.dtype)


def _split_in(u):
    idx = [D_CONV, 2 * D_CONV, 3 * D_CONV, 4 * D_CONV,
           4 * D_CONV + D_RET, 4 * D_CONV + 2 * D_RET, 4 * D_CONV + 3 * D_RET]
    return jnp.split(u, idx, axis=-1)


def _short_conv(u, w):
    up = jnp.pad(u, ((0, 0), (1, 1), (0, 0)))
    return up[:, :-2] * w[0] + up[:, 1:-1] * w[1] + up[:, 2:] * w[2]


def _conv_branch(h, b, c, z, conv_w, norm_w):
    y = b * _short_conv(c * h, conv_w)
    return jax.nn.silu(z) * _rmsnorm(y, norm_w)


def _heads(t):
    b, l, _ = t.shape
    return t.reshape(b, l, RET_HEADS, RET_HEAD_DIM).transpose(0, 2, 1, 3)


def _rope_1d(x, pos):
    f = x.shape[-1] // 2
    inv = ROPE_BASE ** (-jnp.arange(f, dtype=jnp.float32) / f)
    ang = pos.astype(jnp.float32)[:, None] * inv[None, :]
    cos, sin = jnp.cos(ang), jnp.sin(ang)
    x1, x2 = x[..., :f], x[..., f:]
    return jnp.concatenate([x1 * cos - x2 * sin, x1 * sin + x2 * cos], axis=-1).astype(x.dtype)


def _axial_rope(x, row_pos, col_pos):
    half = x.shape[-1] // 2
    return jnp.concatenate([_rope_1d(x[..., :half], row_pos),
                            _rope_1d(x[..., half:], col_pos)], axis=-1)


def _chunk_retention(q, k, v, lg, s0):
    b, h, l, dk = q.shape
    n = l // CHUNK
    qc = q.reshape(b, h, n, CHUNK, dk)
    kc = k.reshape(b, h, n, CHUNK, dk)
    vc = v.reshape(b, h, n, CHUNK, v.shape[-1])
    pos = jnp.arange(CHUNK, dtype=jnp.float32)
    diff = pos[:, None] - pos[None, :]
    dmask = jnp.where(diff >= 0, jnp.exp(lg[:, None, None] * jnp.maximum(diff, 0.0)[None]), 0.0)
    scores = jnp.einsum('bhnid,bhnjd->bhnij', qc, kc) * dmask[None, :, None]
    intra = jnp.einsum('bhnij,bhnje->bhnie', scores, vc)
    k_decay = jnp.exp(lg[:, None] * (CHUNK - 1 - pos)[None])
    q_decay = jnp.exp(lg[:, None] * (pos + 1.0)[None])
    chunk_decay = jnp.exp(lg * CHUNK)
    chunk_kv = jnp.einsum('bhnjd,hj,bhnje->nbhde', kc, k_decay, vc)

    def step(s, kv):
        return chunk_decay[None, :, None, None] * s + kv, s

    _, s_prev = lax.scan(step, s0, chunk_kv)
    inter = jnp.einsum('bhnid,hi,nbhde->bhnie', qc, q_decay, s_prev)
    return (intra + inter).reshape(b, h, l, -1)


def _bidir_retention(q, k, v, lg_f, lg_b, s0_f, s0_b):
    o_f = _chunk_retention(q, k, v, lg_f, s0_f)
    flip = lambda t: jnp.flip(t, axis=2)
    o_b = _chunk_retention(flip(q), flip(k), flip(v), lg_b, s0_b)
    return o_f + flip(o_b)


def _context_states(k, v, lg_f, lg_b):
    lc = k.shape[2]
    t = jnp.arange(lc, dtype=jnp.float32)
    w_f = jnp.exp(lg_f[:, None] * (lc - 1.0 - t)[None])
    w_b = jnp.exp(lg_b[:, None] * t[None])
    s_f = jnp.einsum('bhtd,ht,bhte->bhde', k, w_f, v)
    s_b = jnp.einsum('bhtd,ht,bhte->bhde', k, w_b, v)
    return s_f, s_b


def _ret_out(o, z, gn_w):
    of = o.astype(jnp.float32)
    mu = jnp.mean(of, axis=-1, keepdims=True)
    var = jnp.mean(jnp.square(of - mu), axis=-1, keepdims=True)
    on = (of - mu) * lax.rsqrt(var + EPS)
    b, h, l, d = on.shape
    on = on.transpose(0, 2, 1, 3).reshape(b, l, h * d) * gn_w.astype(jnp.float32)
    return jax.nn.silu(z) * on.astype(z.dtype)


def _layer(x, ctx, c, c_ctx, norm_w, w_mod, b_mod, w_in, conv_w, conv_norm_w, ret_norm_w,
           decay_f, decay_b, w_out, row_pos, col_pos, update_ctx):
    d = D_MODEL
    lg_f = -jnp.exp(decay_f.astype(jnp.float32))
    lg_b = -jnp.exp(decay_b.astype(jnp.float32))
    k_scale = RET_HEAD_DIM ** -0.5

    shift, scale, gate = jnp.split(jax.nn.silu(c) @ w_mod + b_mod, 3, axis=-1)
    hx = _rmsnorm(x, norm_w) * (1 + scale[:, None]) + shift[:, None]
    a_h, a_b, a_c, a_z, q, k, v, r_z = _split_in(hx @ w_in)

    n_mod = 3 if update_ctx else 2
    mod_c = jax.nn.silu(c_ctx) @ w_mod[:, :n_mod * d] + b_mod[:n_mod * d]
    hc = _rmsnorm(ctx, norm_w) * (1 + mod_c[d:2 * d]) + mod_c[:d]
    if update_ctx:
        ca_h, ca_b, ca_c, ca_z, cq, ck, cv, cr_z = _split_in(hc @ w_in)
    else:
        kv0 = 4 * D_CONV + D_RET
        ck, cv = jnp.split(hc @ w_in[:, kv0:kv0 + 2 * D_RET], 2, axis=-1)
    ck_h = _heads(ck) * k_scale
    cv_h = _heads(cv)
    s_f, s_b = _context_states(ck_h, cv_h, lg_f, lg_b)

    q_h = _axial_rope(_heads(q), row_pos, col_pos)
    k_h = _axial_rope(_heads(k), row_pos, col_pos) * k_scale
    o = _bidir_retention(q_h, k_h, _heads(v), lg_f, lg_b, s_f, s_b)
    y_ret = _ret_out(o, r_z, ret_norm_w)
    y_conv = _conv_branch(a_h, a_b, a_c, a_z, conv_w, conv_norm_w)
    x = x + gate[:, None] * (jnp.concatenate([y_conv, y_ret], axis=-1) @ w_out)

    if update_ctx:
        zeros = jnp.zeros_like(s_f)
        oc = _bidir_retention(_heads(cq), ck_h, cv_h, lg_f, lg_b, zeros, zeros)
        yc_ret = _ret_out(oc, cr_z, ret_norm_w)
        yc_conv = _conv_branch(ca_h, ca_b, ca_c, ca_z, conv_w, conv_norm_w)
        ctx = ctx + mod_c[2 * d:] * (jnp.concatenate([yc_conv, yc_ret], axis=-1) @ w_out)
    return x, ctx


def _fwd_setup_inputs(seed: int = 0) -> dict:
    key = jax.random.key(seed)
    ks = jax.random.split(key, 16)
    f32 = jnp.float32
    nrm = lambda k, s: jax.random.normal(k, s, f32)
    base = jnp.log(-jnp.log1p(-(2.0 ** -(RET_DECAY_OFFSET + jnp.arange(RET_HEADS, dtype=f32)))))
    return {
        "x": nrm(ks[0], (BATCH, SEQ, D_MODEL)),
        "c": nrm(ks[1], (BATCH, D_MODEL)),
        "ctx": nrm(ks[2], (BATCH, CTX_LEN, D_MODEL)),
        "c_ctx": nrm(ks[3], (D_MODEL,)),
        "norm_w": 1.0 + 0.05 * nrm(ks[4], (DEPTH, D_MODEL)),
        "w_mod": nrm(ks[5], (DEPTH, D_MODEL, 3 * D_MODEL)) * (0.5 * D_MODEL ** -0.5),
        "b_mod": 0.02 * nrm(ks[6], (DEPTH, 3 * D_MODEL)),
        "w_in": nrm(ks[7], (DEPTH, D_MODEL, D_IN)) * D_MODEL ** -0.5,
        "conv_w": nrm(ks[8], (DEPTH, CONV_WIDTH, D_CONV)) * CONV_WIDTH ** -0.5,
        "conv_norm_w": 1.0 + 0.05 * nrm(ks[9], (DEPTH, D_CONV)),
        "ret_norm_w": 1.0 + 0.05 * nrm(ks[10], (DEPTH, D_RET)),
        "ret_decay_f": base[None] + 0.05 * nrm(ks[11], (DEPTH, RET_HEADS)),
        "ret_decay_b": base[None] + 0.05 * nrm(ks[12], (DEPTH, RET_HEADS)),
        "w_out": nrm(ks[13], (DEPTH, D_MIX, D_MODEL)) * D_MIX ** -0.5,
        "final_norm_w": 1.0 + 0.05 * nrm(ks[14], (D_MODEL,)),
    }


def _fwd_reference(x, c, ctx, c_ctx, norm_w, w_mod, b_mod, w_in, conv_w, conv_norm_w, ret_norm_w,
              ret_decay_f, ret_decay_b, w_out, final_norm_w):
    seq = x.shape[1]
    rows = seq // GRID_W
    row_pos = jnp.repeat(jnp.arange(rows), GRID_W)
    col_pos = jnp.tile(jnp.arange(GRID_W), rows)
    for layer in range(DEPTH):
        x, ctx = _layer(x, ctx, c, c_ctx, norm_w[layer], w_mod[layer], b_mod[layer], w_in[layer],
                        conv_w[layer], conv_norm_w[layer], ret_norm_w[layer],
                        ret_decay_f[layer], ret_decay_b[layer], w_out[layer],
                        row_pos, col_pos, layer < DEPTH - 1)
    return _rmsnorm(x, final_norm_w)


import jax as _jax
import jax.numpy as _jnp

TWIN_FORMAT = 'train_step'
FWD_PARAMS = ['x', 'c', 'ctx', 'c_ctx', 'norm_w', 'w_mod', 'b_mod', 'w_in', 'conv_w', 'conv_norm_w', 'ret_norm_w', 'ret_decay_f', 'ret_decay_b', 'w_out', 'final_norm_w']
TWIN_WEIGHTS = ['c_ctx', 'norm_w', 'w_mod', 'b_mod', 'w_in', 'conv_w', 'conv_norm_w', 'ret_norm_w', 'ret_decay_f', 'ret_decay_b', 'w_out', 'final_norm_w']
TWIN_DIFF_INPUT = 'x'
TWIN_INPUTS = ['x', 'c', 'ctx', 'c_ctx', 'norm_w', 'w_mod', 'b_mod', 'w_in', 'conv_w', 'conv_norm_w', 'ret_norm_w', 'ret_decay_f', 'ret_decay_b', 'w_out', 'final_norm_w', 'loss_target', 'm_c_ctx', 'm_norm_w', 'm_w_mod', 'm_b_mod', 'm_w_in', 'm_conv_w', 'm_conv_norm_w', 'm_ret_norm_w', 'm_ret_decay_f', 'm_ret_decay_b', 'm_w_out', 'm_final_norm_w', 'v_c_ctx', 'v_norm_w', 'v_w_mod', 'v_b_mod', 'v_w_in', 'v_conv_w', 'v_conv_norm_w', 'v_ret_norm_w', 'v_ret_decay_f', 'v_ret_decay_b', 'v_w_out', 'v_final_norm_w']
TWIN_OUTPUTS = ['loss', 'grad_x', 'grad_c_ctx', 'grad_norm_w', 'grad_w_mod', 'grad_b_mod', 'grad_w_in', 'grad_conv_w', 'grad_conv_norm_w', 'grad_ret_norm_w', 'grad_ret_decay_f', 'grad_ret_decay_b', 'grad_w_out', 'grad_final_norm_w', 'delta_c_ctx', 'delta_norm_w', 'delta_w_mod', 'delta_b_mod', 'delta_w_in', 'delta_conv_w', 'delta_conv_norm_w', 'delta_ret_norm_w', 'delta_ret_decay_f', 'delta_ret_decay_b', 'delta_w_out', 'delta_final_norm_w', 'new_m_c_ctx', 'new_m_norm_w', 'new_m_w_mod', 'new_m_b_mod', 'new_m_w_in', 'new_m_conv_w', 'new_m_conv_norm_w', 'new_m_ret_norm_w', 'new_m_ret_decay_f', 'new_m_ret_decay_b', 'new_m_w_out', 'new_m_final_norm_w', 'new_v_c_ctx', 'new_v_norm_w', 'new_v_w_mod', 'new_v_b_mod', 'new_v_w_in', 'new_v_conv_w', 'new_v_conv_norm_w', 'new_v_ret_norm_w', 'new_v_ret_decay_f', 'new_v_ret_decay_b', 'new_v_w_out', 'new_v_final_norm_w']
TWIN_LEAF_KINDS = {'loss': 'loss', 'grad_x': 'grad_x', 'grad_c_ctx': 'grad_w', 'grad_norm_w': 'grad_w', 'grad_w_mod': 'grad_w', 'grad_b_mod': 'grad_w', 'grad_w_in': 'grad_w', 'grad_conv_w': 'grad_w', 'grad_conv_norm_w': 'grad_w', 'grad_ret_norm_w': 'grad_w', 'grad_ret_decay_f': 'grad_w', 'grad_ret_decay_b': 'grad_w', 'grad_w_out': 'grad_w', 'grad_final_norm_w': 'grad_w', 'delta_c_ctx': 'delta_w', 'delta_norm_w': 'delta_w', 'delta_w_mod': 'delta_w', 'delta_b_mod': 'delta_w', 'delta_w_in': 'delta_w', 'delta_conv_w': 'delta_w', 'delta_conv_norm_w': 'delta_w', 'delta_ret_norm_w': 'delta_w', 'delta_ret_decay_f': 'delta_w', 'delta_ret_decay_b': 'delta_w', 'delta_w_out': 'delta_w', 'delta_final_norm_w': 'delta_w', 'new_m_c_ctx': 'new_m', 'new_m_norm_w': 'new_m', 'new_m_w_mod': 'new_m', 'new_m_b_mod': 'new_m', 'new_m_w_in': 'new_m', 'new_m_conv_w': 'new_m', 'new_m_conv_norm_w': 'new_m', 'new_m_ret_norm_w': 'new_m', 'new_m_ret_decay_f': 'new_m', 'new_m_ret_decay_b': 'new_m', 'new_m_w_out': 'new_m', 'new_m_final_norm_w': 'new_m', 'new_v_c_ctx': 'new_v', 'new_v_norm_w': 'new_v', 'new_v_w_mod': 'new_v', 'new_v_b_mod': 'new_v', 'new_v_w_in': 'new_v', 'new_v_conv_w': 'new_v', 'new_v_conv_norm_w': 'new_v', 'new_v_ret_norm_w': 'new_v', 'new_v_ret_decay_f': 'new_v', 'new_v_ret_decay_b': 'new_v', 'new_v_w_out': 'new_v', 'new_v_final_norm_w': 'new_v'}


def _forward(args):
    return _fwd_reference(*[args[k] for k in FWD_PARAMS])


def _output_shape():
    def fwd():
        inp = _fwd_setup_inputs(0)
        return _fwd_reference(*[inp[k] for k in FWD_PARAMS])
    out = _jax.eval_shape(fwd)
    return out.shape, out.dtype

N_MICROBATCH = 1
ADAM_LR = 0.001
ADAM_B1 = 0.9
ADAM_B2 = 0.999
ADAM_EPS = 1e-08
ADAM_WD = 0.01
ADAM_STEP = 10
PER_EXAMPLE_BATCH_AXIS = {'x': 0, 'c': 0, 'ctx': 0, 'loss_target': 0}
SHARED_INPUTS = []
_WEIGHT_DTYPES = {'c_ctx': _jnp.float32, 'norm_w': _jnp.float32, 'w_mod': _jnp.float32, 'b_mod': _jnp.float32, 'w_in': _jnp.float32, 'conv_w': _jnp.float32, 'conv_norm_w': _jnp.float32, 'ret_norm_w': _jnp.float32, 'ret_decay_f': _jnp.float32, 'ret_decay_b': _jnp.float32, 'w_out': _jnp.float32, 'final_norm_w': _jnp.float32}
MOMENT_SCALE = {'c_ctx': 1.897470e-02, 'norm_w': 4.822847e-02, 'w_mod': 5.198206e-02, 'b_mod': 9.891284e-02, 'w_in': 2.667359e-02, 'conv_w': 2.521049e-02, 'conv_norm_w': 2.498867e-02, 'ret_norm_w': 2.555137e-02, 'ret_decay_f': 6.979889e-02, 'ret_decay_b': 5.755478e-02, 'w_out': 2.494031e-02, 'final_norm_w': 3.195141e+01}


def _to_microbatches(a, axis):
    t = _jnp.moveaxis(a, axis, 0)
    t = t.reshape((N_MICROBATCH, t.shape[0] // N_MICROBATCH) + t.shape[1:])
    return _jnp.moveaxis(t, 1, axis + 1)


def setup_inputs(seed: int = 0) -> dict:
    inp = _fwd_setup_inputs(seed)
    key = _jax.random.fold_in(_jax.random.key(seed), 7919)
    shape, _ = _output_shape()
    out = dict(inp)
    out["loss_target"] = _jax.random.normal(_jax.random.fold_in(key, 0), shape, _jnp.float32)
    for i, name in enumerate(TWIN_WEIGHTS):
        w = inp[name].astype(_jnp.float32)
        if MOMENT_SCALE is None:
            s = _jnp.sqrt(_jnp.mean(_jnp.square(w)) + 1e-30)
        else:
            s = MOMENT_SCALE[name]
        km, kv = _jax.random.split(_jax.random.fold_in(key, i + 1))
        out[name] = w
        out["m_" + name] = s * _jax.random.normal(km, w.shape, _jnp.float32)
        out["v_" + name] = (s * s) * _jax.random.uniform(kv, w.shape, _jnp.float32, 0.5, 1.5)
    if N_MICROBATCH > 1:
        for name, axis in PER_EXAMPLE_BATCH_AXIS.items():
            out[name] = _to_microbatches(out[name], axis)
    return {'x': out['x'], 'c': out['c'], 'ctx': out['ctx'], 'c_ctx': out['c_ctx'], 'norm_w': out['norm_w'], 'w_mod': out['w_mod'], 'b_mod': out['b_mod'], 'w_in': out['w_in'], 'conv_w': out['conv_w'], 'conv_norm_w': out['conv_norm_w'], 'ret_norm_w': out['ret_norm_w'], 'ret_decay_f': out['ret_decay_f'], 'ret_decay_b': out['ret_decay_b'], 'w_out': out['w_out'], 'final_norm_w': out['final_norm_w'], 'loss_target': out['loss_target'], 'm_c_ctx': out['m_c_ctx'], 'm_norm_w': out['m_norm_w'], 'm_w_mod': out['m_w_mod'], 'm_b_mod': out['m_b_mod'], 'm_w_in': out['m_w_in'], 'm_conv_w': out['m_conv_w'], 'm_conv_norm_w': out['m_conv_norm_w'], 'm_ret_norm_w': out['m_ret_norm_w'], 'm_ret_decay_f': out['m_ret_decay_f'], 'm_ret_decay_b': out['m_ret_decay_b'], 'm_w_out': out['m_w_out'], 'm_final_norm_w': out['m_final_norm_w'], 'v_c_ctx': out['v_c_ctx'], 'v_norm_w': out['v_norm_w'], 'v_w_mod': out['v_w_mod'], 'v_b_mod': out['v_b_mod'], 'v_w_in': out['v_w_in'], 'v_conv_w': out['v_conv_w'], 'v_conv_norm_w': out['v_conv_norm_w'], 'v_ret_norm_w': out['v_ret_norm_w'], 'v_ret_decay_f': out['v_ret_decay_f'], 'v_ret_decay_b': out['v_ret_decay_b'], 'v_w_out': out['v_w_out'], 'v_final_norm_w': out['v_final_norm_w']}


def _loss(weights, diff, rest, loss_target):
    with _jax.named_scope("forward"):
        args = {**rest, TWIN_DIFF_INPUT: diff, **{k: w.astype(_WEIGHT_DTYPES[k]) for k, w in weights.items()}}
        y = _forward(args)
    with _jax.named_scope("loss_head"):
        err = _jnp.square(y.astype(_jnp.float32) - loss_target)
        return 0.5 * _jnp.sum(_jnp.mean(err, axis=-1)) if err.ndim else 0.5 * err


def _adamw(w, g, m, v):
    m = ADAM_B1 * m + (1.0 - ADAM_B1) * g
    v = ADAM_B2 * v + (1.0 - ADAM_B2) * _jnp.square(g)
    m_hat = m / (1.0 - ADAM_B1 ** ADAM_STEP)
    v_hat = v / (1.0 - ADAM_B2 ** ADAM_STEP)
    delta = -ADAM_LR * (m_hat / (_jnp.sqrt(v_hat) + ADAM_EPS) + ADAM_WD * w)
    return delta, m, v


def reference(x, c, ctx, c_ctx, norm_w, w_mod, b_mod, w_in, conv_w, conv_norm_w, ret_norm_w, ret_decay_f, ret_decay_b, w_out, final_norm_w, loss_target, m_c_ctx, m_norm_w, m_w_mod, m_b_mod, m_w_in, m_conv_w, m_conv_norm_w, m_ret_norm_w, m_ret_decay_f, m_ret_decay_b, m_w_out, m_final_norm_w, v_c_ctx, v_norm_w, v_w_mod, v_b_mod, v_w_in, v_conv_w, v_conv_norm_w, v_ret_norm_w, v_ret_decay_f, v_ret_decay_b, v_w_out, v_final_norm_w):
    given = dict(x=x, c=c, ctx=ctx, c_ctx=c_ctx, norm_w=norm_w, w_mod=w_mod, b_mod=b_mod, w_in=w_in, conv_w=conv_w, conv_norm_w=conv_norm_w, ret_norm_w=ret_norm_w, ret_decay_f=ret_decay_f, ret_decay_b=ret_decay_b, w_out=w_out, final_norm_w=final_norm_w, loss_target=loss_target, m_c_ctx=m_c_ctx, m_norm_w=m_norm_w, m_w_mod=m_w_mod, m_b_mod=m_b_mod, m_w_in=m_w_in, m_conv_w=m_conv_w, m_conv_norm_w=m_conv_norm_w, m_ret_norm_w=m_ret_norm_w, m_ret_decay_f=m_ret_decay_f, m_ret_decay_b=m_ret_decay_b, m_w_out=m_w_out, m_final_norm_w=m_final_norm_w, v_c_ctx=v_c_ctx, v_norm_w=v_norm_w, v_w_mod=v_w_mod, v_b_mod=v_b_mod, v_w_in=v_w_in, v_conv_w=v_conv_w, v_conv_norm_w=v_conv_norm_w, v_ret_norm_w=v_ret_norm_w, v_ret_decay_f=v_ret_decay_f, v_ret_decay_b=v_ret_decay_b, v_w_out=v_w_out, v_final_norm_w=v_final_norm_w)
    weights = {n: given[n] for n in TWIN_WEIGHTS}
    shared = {n: given[n] for n in SHARED_INPUTS}
    per_example = {n: given[n] for n in ['x', 'c', 'ctx']}
    grad_fn = _jax.value_and_grad(_loss, argnums=(0, 1))

    def one_microbatch(ex, loss_target):
        ex = dict(ex)
        diff = ex.pop(TWIN_DIFF_INPUT)
        return grad_fn(weights, diff, {**shared, **ex}, loss_target)

    if N_MICROBATCH == 1:
        loss, (grad_w, grad_x) = one_microbatch(per_example, given["loss_target"])
    else:
        def body(carry, xs):
            loss_sum, grad_sum = carry
            l_k, (gw_k, gx_k) = one_microbatch(xs[0], xs[1])
            with _jax.named_scope("update"):
                return (loss_sum + l_k, _jax.tree.map(_jnp.add, grad_sum, gw_k)), gx_k

        init = (_jnp.zeros((), _jnp.float32), _jax.tree.map(_jnp.zeros_like, weights))
        (loss, grad_w), grad_x = _jax.lax.scan(body, init, (per_example, given["loss_target"]))
    with _jax.named_scope("update"):
        delta_w, new_m, new_v = {}, {}, {}
        for n in TWIN_WEIGHTS:
            delta_w[n], new_m[n], new_v[n] = _adamw(weights[n], grad_w[n], given["m_" + n], given["v_" + n])
    return (loss, grad_x, *[grad_w[n] for n in TWIN_WEIGHTS], *[delta_w[n] for n in TWIN_WEIGHTS],
            *[new_m[n] for n in TWIN_WEIGHTS], *[new_v[n] for n in TWIN_WEIGHTS])
```

```python
import functools

import jax
import jax.numpy as jnp
from jax import lax
from jax.experimental import pallas as pl
from jax.experimental.pallas import tpu as pltpu

F32 = jnp.float32
BF16 = jnp.bfloat16

EPS = 1e-6
CHUNK = 128
HEAD_DIM = 128
GRID_W = 64
ROPE_BASE = 10000.0
N_DEV = 8
ADAM_LR, ADAM_B1, ADAM_B2, ADAM_EPS, ADAM_WD, ADAM_STEP = 0.001, 0.9, 0.999, 1e-08, 0.01, 10

ROW_TILE = 256
V7X_VMEM_LIMIT = 56 * 1024 * 1024
MESH_AXES = ("x", "y", "c")

NN = ((1,), (0,))
NT = ((1,), (1,))
TN = ((0,), (0,))


def _dot(a, b, dims):
    return lax.dot_general(a, b, (dims, ((), ())), preferred_element_type=F32)


def _params(sem=None):
    if sem is None:
        return pltpu.CompilerParams(vmem_limit_bytes=V7X_VMEM_LIMIT)
    return pltpu.CompilerParams(dimension_semantics=sem, vmem_limit_bytes=V7X_VMEM_LIMIT)


def _silu(z):
    return z * jax.nn.sigmoid(z)


def _dsilu(z):
    s = jax.nn.sigmoid(z)
    return s * (1.0 + z * (1.0 - s))


def _sum_all(a):
    return jnp.sum(jnp.sum(a, axis=1, keepdims=True), axis=0, keepdims=True)


def _mm_rows(t):
    return 768 if t % 768 == 0 else ROW_TILE


def _full(shape):
    n = len(shape)
    return pl.BlockSpec(shape, lambda *_: (0,) * n)


def _peers(x, y, c):
    return [(x, y, 1 - c), (1 - x, y, c), (x, 1 - y, c), (1 - x, 1 - y, c),
            (1 - x, y, 1 - c), (x, 1 - y, 1 - c), (1 - x, 1 - y, 1 - c)]


def _lin(p):
    return 4 * p[0] + 2 * p[1] + p[2]


def _all_gather(arrays, name, in_vmem):
    n_arr = len(arrays)
    space = pltpu.VMEM if in_vmem else pl.ANY

    def body(*refs):
        ins, outs = refs[:n_arr], refs[n_arr:2 * n_arr]
        send_sems, recv_sems, local_sems = refs[2 * n_arr:]
        x, y, c = lax.axis_index("x"), lax.axis_index("y"), lax.axis_index("c")
        me, sibling = (x, y, c), (x, y, 1 - c)
        chips = [(1 - x, y), (x, 1 - y), (1 - x, 1 - y)]
        every = []
        locals_ = []
        for a in range(n_arr):
            m_per = ins[a].shape[0]
            out_ref = outs[a]

            def rows(p, out_ref=out_ref, m_per=m_per):
                return out_ref.at[pl.ds(_lin(p) * m_per, m_per), :]

            def copy(k, block, to, src=None, a=a, rows=rows):
                return pltpu.make_async_remote_copy(
                    src_ref=rows(block) if src is None else src, dst_ref=rows(block),
                    send_sem=send_sems.at[a, k], recv_sem=recv_sems.at[a, k],
                    device_id=to, device_id_type=pl.DeviceIdType.MESH)

            mine = pltpu.make_async_copy(ins[a], rows(me), local_sems.at[a])
            mine.start()
            locals_.append(mine)
            first = [copy(0, me, sibling, src=ins[a])]
            first += [copy(1 + j, me, (*chip, c), src=ins[a]) for j, chip in enumerate(chips)]
            for cp in first:
                cp.start()
            every.append((copy, first))
        sends = []
        for a in range(n_arr):
            copy, first = every[a]
            passed = [copy(4 + j, (*chip, c), sibling) for j, chip in enumerate(chips)]
            for j, chip in enumerate(chips):
                copy(1 + j, (*chip, c), me).wait_recv()
                passed[j].start()
            sends += first + passed
        for a in range(n_arr):
            copy, _ = every[a]
            copy(0, sibling, me).wait_recv()
            for j, chip in enumerate(chips):
                copy(4 + j, (*chip, 1 - c), me).wait_recv()
        for cp in sends:
            cp.wait_send()
        for mine in locals_:
            mine.wait()

    outs = pl.pallas_call(
        body, name=name,
        out_shape=[jax.ShapeDtypeStruct((N_DEV * a.shape[0], a.shape[1]), a.dtype) for a in arrays],
        in_specs=[pl.BlockSpec(memory_space=space)] * n_arr,
        out_specs=[pl.BlockSpec(memory_space=space)] * n_arr,
        scratch_shapes=[pltpu.SemaphoreType.DMA((n_arr, 7)), pltpu.SemaphoreType.DMA((n_arr, 7)),
                        pltpu.SemaphoreType.DMA((n_arr,))],
        compiler_params=_params(),
    )(*arrays)
    return list(outs)


def _all_to_all(groups, name):
    arrays = [a for grp in groups for a in grp]
    n_arr = len(arrays)
    n_grp = len(groups)
    place = []
    for gi, grp in enumerate(groups):
        off = 0
        for a in grp:
            place.append((gi, off, a.shape[1]))
            off += a.shape[1]

    def body(*refs):
        ins, outs = refs[:n_arr], refs[n_arr:n_arr + n_grp]
        send_sems, recv_sems, local_sems = refs[n_arr + n_grp:]
        x, y, c = lax.axis_index("x"), lax.axis_index("y"), lax.axis_index("c")
        me = (x, y, c)
        peers = _peers(x, y, c)

        def landing(a, sender):
            gi, off, rows = place[a]
            return outs[gi].at[_lin(sender), pl.ds(off, rows), :]

        locals_ = []
        sends = []
        for a in range(n_arr):
            mine = pltpu.make_async_copy(ins[a].at[_lin(me)], landing(a, me), local_sems.at[a])
            mine.start()
            locals_.append(mine)
        for k, peer in enumerate(peers):
            for a in range(n_arr):
                cp = pltpu.make_async_remote_copy(
                    src_ref=ins[a].at[_lin(peer)], dst_ref=landing(a, me),
                    send_sem=send_sems.at[a, k], recv_sem=recv_sems.at[a, k],
                    device_id=peer, device_id_type=pl.DeviceIdType.MESH)
                cp.start()
                sends.append(cp)
        for k, peer in enumerate(peers):
            for a in range(n_arr):
                pltpu.make_async_remote_copy(
                    src_ref=ins[a].at[_lin(me)], dst_ref=landing(a, peer),
                    send_sem=send_sems.at[a, k], recv_sem=recv_sems.at[a, k],
                    device_id=peer, device_id_type=pl.DeviceIdType.MESH).wait_recv()
        for cp in sends:
            cp.wait_send()
        for mine in locals_:
            mine.wait()

    outs = pl.pallas_call(
        body, name=name,
        out_shape=[jax.ShapeDtypeStruct((N_DEV, sum(a.shape[1] for a in grp), grp[0].shape[2]), grp[0].dtype)
                   for grp in groups],
        in_specs=[pl.BlockSpec(memory_space=pl.ANY)] * n_arr,
        out_specs=[pl.BlockSpec(memory_space=pl.ANY)] * n_grp,
        scratch_shapes=[pltpu.SemaphoreType.DMA((n_arr, 7)), pltpu.SemaphoreType.DMA((n_arr, 7)),
                        pltpu.SemaphoreType.DMA((n_arr,))],
        compiler_params=_params(),
    )(*arrays)
    return list(outs)


def _mod_rows(c9, w_mod, b_sh, name):
    n = w_mod.shape[1]

    def body(c_ref, w_ref, b_ref, o_ref):
        s9 = _silu(c_ref[...]).astype(BF16)
        o_ref[...] = _dot(s9, w_ref[...].astype(BF16), NN) + b_ref[...]

    return pl.pallas_call(body, name=name, out_shape=jax.ShapeDtypeStruct((16, n), F32),
                          compiler_params=_params())(c9, w_mod, b_sh)


def _mod_grads(dm_rows, dc_rows, c9, w_mod, name):
    d, n = w_mod.shape

    def body(dm_ref, dc_ref, c_ref, w_ref, gw_ref, dc_out):
        dc = dc_ref[...]
        tot = dc[0:1]
        for j in range(1, N_DEV):
            tot = tot + dc[j:j + 1]
        row = lax.broadcasted_iota(jnp.int32, (8, n), 0)
        lower = jnp.where(row == 0, tot, 0.0)
        dmod9 = jnp.concatenate([dm_ref[...], lower], axis=0).astype(BF16)
        c9v = c_ref[...]
        s9 = _silu(c9v).astype(BF16)
        gw_ref[...] = _dot(s9, dmod9, TN)
        ds = _dot(lower.astype(BF16), w_ref[...].astype(BF16), NT)
        dc_out[...] = ds * _dsilu(c9v[8:16])

    return pl.pallas_call(body, name=name,
                          out_shape=[jax.ShapeDtypeStruct((d, n), F32), jax.ShapeDtypeStruct((8, d), F32)],
                          compiler_params=_params())(dm_rows, dc_rows, c9, w_mod)


def _decay_tables(dec, n_heads, name):
    c = CHUNK

    def body(dec_ref, dc_ref, dlf_ref, dlb_ref, qf_ref, kf_ref, qb_ref, kb_ref, cdf_ref, cdb_ref, lg_ref):
        h = pl.program_id(0)
        d = dec_ref[...]
        lane = lax.broadcasted_iota(jnp.int32, d.shape, 1)
        lg = -jnp.exp(jnp.sum(jnp.where(lane == h, d, 0.0), axis=1, keepdims=True))
        lgf, lgb = lg[0:1], lg[1:2]
        i = lax.broadcasted_iota(jnp.int32, (c, c), 0).astype(F32)
        j = lax.broadcasted_iota(jnp.int32, (c, c), 1).astype(F32)
        diff = i - j
        d_f = jnp.where(diff >= 0, jnp.exp(lgf * jnp.maximum(diff, 0.0)), 0.0)
        d_b = jnp.where(diff <= 0, jnp.exp(lgb * jnp.maximum(-diff, 0.0)), 0.0)
        dc_ref[...] = d_f + d_b
        dlf_ref[...] = diff * d_f
        dlb_ref[...] = -diff * d_b
        pos = lax.broadcasted_iota(jnp.int32, (c, HEAD_DIM), 0).astype(F32)
        qf_ref[...] = jnp.exp(lgf * (pos + 1.0))
        kf_ref[...] = jnp.exp(lgf * (c - 1.0 - pos))
        qb_ref[...] = jnp.exp(lgb * (c - pos))
        kb_ref[...] = jnp.exp(lgb * pos)
        ones = jnp.ones((8, HEAD_DIM), F32)
        cdf_ref[...] = jnp.exp(lgf * float(c)) * ones
        cdb_ref[...] = jnp.exp(lgb * float(c)) * ones

        @pl.when(h == 0)
        def _():
            lg_ref[...] = jnp.zeros_like(lg_ref)

        row8 = lax.broadcasted_iota(jnp.int32, (8, HEAD_DIM), 0)
        lane8 = lax.broadcasted_iota(jnp.int32, (8, HEAD_DIM), 1)
        lg_ref[...] += (jnp.where((row8 == 0) & (lane8 == h), lgf, 0.0)
                        + jnp.where((row8 == 1) & (lane8 == h), lgb, 0.0))

    def per_head(*tail):
        return pl.BlockSpec((None,) + tail, lambda h: (h,) + (0,) * len(tail))

    shapes = [(c, c)] * 3 + [(c, HEAD_DIM)] * 4 + [(8, HEAD_DIM)] * 2
    return pl.pallas_call(
        body, name=name, grid=(n_heads,),
        in_specs=[_full(dec.shape)],
        out_specs=[per_head(*s) for s in shapes] + [_full((8, HEAD_DIM))],
        out_shape=[jax.ShapeDtypeStruct((n_heads,) + s, F32) for s in shapes]
        + [jax.ShapeDtypeStruct((8, HEAD_DIM), F32)],
        compiler_params=_params(("arbitrary",)),
    )(dec)


def _prenorm(xt, nw, mod, n_lat, name):
    t, d = xt.shape
    nxb = n_lat // ROW_TILE

    def body(x_ref, nw_ref, mod_ref, o_ref):
        ctx = pl.program_id(0) >= nxb
        m = mod_ref[...]
        shift = jnp.where(ctx, m[3:4], m[0:1])
        scale = jnp.where(ctx, m[4:5], m[1:2])
        x = x_ref[...]
        r = lax.rsqrt(jnp.mean(x * x, axis=-1, keepdims=True) + EPS)
        o_ref[...] = ((x * r) * nw_ref[...] * (1.0 + scale) + shift).astype(BF16)

    row = pl.BlockSpec((ROW_TILE, d), lambda i: (i, 0))
    return pl.pallas_call(body, name=name, grid=(t // ROW_TILE,),
                          in_specs=[row, _full((1, d)), _full((8, d))], out_specs=row,
                          out_shape=jax.ShapeDtypeStruct((t, d), BF16),
                          compiler_params=_params(("parallel",)))(xt, nw, mod)


def _in_proj(hx, wg, name):
    t, d = hx.shape
    n_seg, _, s = wg.shape
    tm = _mm_rows(t)

    def body(a_ref, w_ref, o_ref):
        o_ref[...] = _dot(a_ref[...], w_ref[...], NN)

    return pl.pallas_call(
        body, name=name, grid=(n_seg, t // tm),
        in_specs=[pl.BlockSpec((tm, d), lambda g, i: (i, 0)), pl.BlockSpec((None, d, s), lambda g, i: (g, 0, 0))],
        out_specs=pl.BlockSpec((None, tm, s), lambda g, i: (g, i, 0)),
        out_shape=jax.ShapeDtypeStruct((n_seg, t, s), F32),
        compiler_params=_params(("parallel", "parallel")))(hx, wg)


def _rope_fwd(v, cos, sa, sb):
    return v * cos + pltpu.roll(v, 96, 1) * sa + pltpu.roll(v, 32, 1) * sb


def _rope_bwd(g, cos, sa, sb):
    return g * cos + pltpu.roll(g * sa, 32, 1) + pltpu.roll(g * sb, 96, 1)


def _rope_qkv(u, cos, sa, sb, n_heads, name):
    _, t, s = u.shape
    k_scale = HEAD_DIM ** -0.5

    def body(q_ref, k_ref, v_ref, cos_ref, sa_ref, sb_ref, qo, ko, vo):
        co, a, b = cos_ref[...], sa_ref[...], sb_ref[...]
        for h in range(n_heads):
            sl = pl.ds(h * HEAD_DIM, HEAD_DIM)
            qo[:, sl] = _rope_fwd(q_ref[:, sl], co, a, b).astype(BF16)
            ko[:, sl] = (_rope_fwd(k_ref[:, sl], co, a, b) * k_scale).astype(BF16)
        vo[...] = v_ref[...].astype(BF16)

    def seg(g):
        return pl.BlockSpec((None, ROW_TILE, s), lambda i: (g, i, 0))

    tab = pl.BlockSpec((ROW_TILE, HEAD_DIM), lambda i: (i, 0))
    row = pl.BlockSpec((ROW_TILE, s), lambda i: (i, 0))
    return pl.pallas_call(body, name=name, grid=(t // ROW_TILE,),
                          in_specs=[seg(4), seg(5), seg(6), tab, tab, tab], out_specs=[row, row, row],
                          out_shape=[jax.ShapeDtypeStruct((t, s), BF16)] * 3,
                          compiler_params=_params(("parallel",)))(u, u, u, cos, sa, sb)


def _f_order(step, nx, ncc):
    return jnp.where(step < ncc, nx + step, step - ncc)


def _state_sweep(kr, vb, tabs, n_heads, nx, ncc, name):
    t, s = kr.shape
    nc = nx + ncc
    c = CHUNK

    def body(kf_ref, vf_ref, kb_ref, vb_ref, kft, kbt, cdf, cdb, sf_out, sb_out, sf, sb):
        @pl.when(pl.program_id(0) == 0)
        def _():
            sf[...] = jnp.zeros_like(sf)
            sb[...] = jnp.zeros_like(sb)

        for h in range(n_heads):
            sl = pl.ds(h * HEAD_DIM, HEAD_DIM)
            sf_out[h] = sf[h].astype(BF16)
            sb_out[h] = sb[h].astype(BF16)
            kd = (kf_ref[:, sl].astype(F32) * kft[h]).astype(BF16)
            sf[h] = cdf[h][0:1, :] * sf[h] + _dot(kd, vf_ref[:, sl], TN)
            kd = (kb_ref[:, sl].astype(F32) * kbt[h]).astype(BF16)
            sb[h] = cdb[h][0:1, :] * sb[h] + _dot(kd, vb_ref[:, sl], TN)

    fwd = pl.BlockSpec((c, s), lambda i: (_f_order(i, nx, ncc), 0))
    bwd = pl.BlockSpec((c, s), lambda i: (nc - 1 - i, 0))
    st = (None, n_heads, HEAD_DIM, HEAD_DIM)
    return pl.pallas_call(
        body, name=name, grid=(nc,),
        in_specs=[fwd, fwd, bwd, bwd, _full((n_heads, c, HEAD_DIM)), _full((n_heads, c, HEAD_DIM)),
                  _full((n_heads, 8, HEAD_DIM)), _full((n_heads, 8, HEAD_DIM))],
        out_specs=[pl.BlockSpec(st, lambda i: (_f_order(i, nx, ncc), 0, 0, 0)),
                   pl.BlockSpec(st, lambda i: (nc - 1 - i, 0, 0, 0))],
        out_shape=[jax.ShapeDtypeStruct((nc, n_heads, HEAD_DIM, HEAD_DIM), BF16)] * 2,
        scratch_shapes=[pltpu.VMEM((n_heads, HEAD_DIM, HEAD_DIM), F32)] * 2,
        compiler_params=_params(("arbitrary",)),
    )(kr, vb, kr, vb, tabs["kf"], tabs["kb"], tabs["cdf"], tabs["cdb"])


def _halo_specs(s, n8):
    per = CHUNK // 8

    def prev(g):
        return pl.BlockSpec((None, 8, s), lambda i: (g, jnp.maximum(i * per - 1, 0), 0))

    def nxt(g):
        return pl.BlockSpec((None, 8, s), lambda i: (g, jnp.minimum((i + 1) * per, n8 - 1), 0))

    return prev, nxt


def _shifted(a, before, after, has_prev, has_next):
    rows = a.shape[0]
    rowi = lax.broadcasted_iota(jnp.int32, a.shape, 0)
    am = jnp.where(rowi == 0, jnp.where(has_prev, before, 0.0), pltpu.roll(a, 1, 0))
    ap = jnp.where(rowi == rows - 1, jnp.where(has_next, after, 0.0), pltpu.roll(a, rows - 1, 0))
    return am, ap


def _neighbours(i, nx, nc):
    return (i != 0) & (i != nx), (i != nx - 1) & (i != nc - 1)


def _mix_fwd(u, qr, kr, vb, sf, sb, tabs, conv_w, cnw, gnw, n_heads, nx, ncc, name):
    _, t, s = u.shape
    nc = nx + ncc
    c = CHUNK

    def body(h_ref, b_ref, c_ref, z_ref, rz_ref, hp_ref, hn_ref, cp_ref, cn_ref, q_ref, k_ref, v_ref,
             sf_ref, sb_ref, dc_ref, qft, qbt, w_ref, cnw_ref, gnw_ref, y_ref, o_ref):
        i = pl.program_id(0)
        has_prev, has_next = _neighbours(i, nx, nc)
        a = c_ref[...] * h_ref[...]
        am, ap = _shifted(a, cp_ref[7:8] * hp_ref[7:8], cn_ref[0:1] * hn_ref[0:1], has_prev, has_next)
        w = w_ref[...]
        y0 = w[0:1] * am + w[1:2] * a + w[2:3] * ap
        yb = b_ref[...] * y0
        r = lax.rsqrt(jnp.mean(yb * yb, axis=-1, keepdims=True) + EPS)
        y_ref[:, pl.ds(0, s)] = (_silu(z_ref[...]) * ((yb * r) * cnw_ref[...])).astype(BF16)
        for h in range(n_heads):
            sl = pl.ds(h * HEAD_DIM, HEAD_DIM)
            q, k, v = q_ref[:, sl], k_ref[:, sl], v_ref[:, sl]
            p = (_dot(q, k, NT) * dc_ref[h]).astype(BF16)
            o = _dot(p, v, NN)
            qf = q.astype(F32)
            o += _dot((qf * qft[h]).astype(BF16), sf_ref[h], NN)
            o += _dot((qf * qbt[h]).astype(BF16), sb_ref[h], NN)
            o_ref[:, sl] = o
            mu = jnp.mean(o, axis=-1, keepdims=True)
            var = jnp.mean(jnp.square(o - mu), axis=-1, keepdims=True)
            on = (o - mu) * lax.rsqrt(var + EPS)
            y_ref[:, pl.ds(s + h * HEAD_DIM, HEAD_DIM)] = (
                _silu(rz_ref[:, sl]) * (on * gnw_ref[:, sl])).astype(BF16)

    def seg(g):
        return pl.BlockSpec((None, c, s), lambda i: (g, i, 0))

    prev, nxt = _halo_specs(s, t // 8)
    row = pl.BlockSpec((c, s), lambda i: (i, 0))
    st = pl.BlockSpec((None, n_heads, HEAD_DIM, HEAD_DIM), lambda i: (i, 0, 0, 0))
    return pl.pallas_call(
        body, name=name, grid=(nc,),
        in_specs=[seg(0), seg(1), seg(2), seg(3), seg(7), prev(0), nxt(0), prev(2), nxt(2), row, row, row,
                  st, st, _full((n_heads, c, c)), _full((n_heads, c, HEAD_DIM)), _full((n_heads, c, HEAD_DIM)),
                  _full((3, s)), _full((1, s)), _full((1, s))],
        out_specs=[pl.BlockSpec((c, 2 * s), lambda i: (i, 0)), row],
        out_shape=[jax.ShapeDtypeStruct((t, 2 * s), BF16), jax.ShapeDtypeStruct((t, s), F32)],
        compiler_params=_params(("parallel",)),
    )(u, u, u, u, u, u, u, u, u, qr, kr, vb, sf, sb, tabs["dc"], tabs["qf"], tabs["qb"], conv_w, cnw, gnw)


def _row_gate(mod_ref, row0, rows, n_lat, col):
    rowi = row0 + lax.broadcasted_iota(jnp.int32, (rows, 1), 0)
    return jnp.where(rowi >= n_lat, mod_ref[5:6, col], mod_ref[2:3, col])


def _out_proj(ycat, w_out, xt, mod, n_lat, name):
    t, d = xt.shape
    tm = _mm_rows(t)
    tn = min(d, 1024)

    def body(a_ref, w_ref, x_ref, mod_ref, m_ref, xo_ref):
        m = _dot(a_ref[...], w_ref[...], NN)
        m_ref[...] = m
        gate = _row_gate(mod_ref, pl.program_id(1) * tm, tm, n_lat, slice(None))
        xo_ref[...] = x_ref[...] + gate * m

    blk = pl.BlockSpec((tm, tn), lambda j, i: (i, j))
    return pl.pallas_call(
        body, name=name, grid=(d // tn, t // tm),
        in_specs=[pl.BlockSpec((tm, d), lambda j, i: (i, 0)), pl.BlockSpec((d, tn), lambda j, i: (0, j)), blk,
                  pl.BlockSpec((8, tn), lambda j, i: (0, j))],
        out_specs=[blk, blk], out_shape=[jax.ShapeDtypeStruct((t, d), F32)] * 2,
        compiler_params=_params(("parallel", "parallel")))(ycat, w_out, xt, mod)


def _loss_head(x2, tgt, fnw, n_lat, name):
    t, d = x2.shape
    nxb = n_lat // ROW_TILE

    def body(x_ref, t_ref, w_ref, dx_ref, loss_ref, dw_ref):
        i = pl.program_id(0)

        @pl.when(i == 0)
        def _():
            loss_ref[...] = jnp.zeros_like(loss_ref)
            dw_ref[...] = jnp.zeros_like(dw_ref)

        @pl.when(i < nxb)
        def _():
            x = x_ref[...]
            w = w_ref[...]
            r = lax.rsqrt(jnp.mean(x * x, axis=-1, keepdims=True) + EPS)
            xn = x * r
            e = xn * w - t_ref[...]
            loss_ref[...] += 0.5 * jnp.sum(jnp.mean(e * e, axis=-1, keepdims=True), axis=0, keepdims=True)
            dy = e * (1.0 / d)
            dw_ref[0:1, :] += jnp.sum(dy * xn, axis=0, keepdims=True)
            dxn = dy * w
            dx_ref[...] = r * (dxn - xn * jnp.mean(dxn * xn, axis=-1, keepdims=True))

        @pl.when(i >= nxb)
        def _():
            dx_ref[...] = jnp.zeros_like(dx_ref)

    row = pl.BlockSpec((ROW_TILE, d), lambda i: (i, 0))
    return pl.pallas_call(
        body, name=name, grid=(t // ROW_TILE,),
        in_specs=[row, pl.BlockSpec((ROW_TILE, d), lambda i: (jnp.minimum(i, nxb - 1), 0)), _full((1, d))],
        out_specs=[row, _full((8, HEAD_DIM)), _full((8, d))],
        out_shape=[jax.ShapeDtypeStruct((t, d), F32), jax.ShapeDtypeStruct((8, HEAD_DIM), F32),
                   jax.ShapeDtypeStruct((8, d), F32)],
        compiler_params=_params(("arbitrary",)))(x2, tgt, fnw)


def _gate_bwd(dxo, m, mod, n_lat, name):
    t, d = dxo.shape
    nxb = n_lat // ROW_TILE

    def body(dx_ref, m_ref, mod_ref, dm_ref, acc_ref):
        i = pl.program_id(0)

        @pl.when(i == 0)
        def _():
            acc_ref[...] = jnp.zeros_like(acc_ref)

        dx = dx_ref[...]
        dg = jnp.sum(dx * m_ref[...], axis=0, keepdims=True)

        @pl.when(i < nxb)
        def _():
            dm_ref[...] = (dx * mod_ref[2:3, :]).astype(BF16)
            acc_ref[2:3, :] += dg

        @pl.when(i >= nxb)
        def _():
            dm_ref[...] = (dx * mod_ref[5:6, :]).astype(BF16)
            acc_ref[5:6, :] += dg

    row = pl.BlockSpec((ROW_TILE, d), lambda i: (i, 0))
    return pl.pallas_call(body, name=name, grid=(t // ROW_TILE,),
                          in_specs=[row, row, _full((8, d))], out_specs=[row, _full((8, d))],
                          out_shape=[jax.ShapeDtypeStruct((t, d), BF16), jax.ShapeDtypeStruct((8, d), F32)],
                          compiler_params=_params(("arbitrary",)))(dxo, m, mod)


def _matmul_nt(a, w, name):
    t, k = a.shape
    n = w.shape[0]
    tm = _mm_rows(t)
    tn = min(n, 1024)

    def body(a_ref, w_ref, o_ref):
        o_ref[...] = _dot(a_ref[...], w_ref[...], NT)

    return pl.pallas_call(
        body, name=name, grid=(n // tn, t // tm),
        in_specs=[pl.BlockSpec((tm, k), lambda j, i: (i, 0)), pl.BlockSpec((tn, k), lambda j, i: (j, 0))],
        out_specs=pl.BlockSpec((tm, tn), lambda j, i: (i, j)),
        out_shape=jax.ShapeDtypeStruct((t, n), F32),
        compiler_params=_params(("parallel", "parallel")))(a, w)


def _weight_grad(a, b, name):
    t, m = a.shape
    n_g, _, n = b.shape
    bt = _mm_rows(t)
    bm = min(m, 1024)
    nt = t // bt

    def body(a_ref, b_ref, o_ref, acc):
        k = pl.program_id(2)

        @pl.when(k == 0)
        def _():
            acc[...] = jnp.zeros_like(acc)

        acc[...] += _dot(a_ref[...], b_ref[...], TN)

        @pl.when(k == nt - 1)
        def _():
            o_ref[...] = acc[...].astype(o_ref.dtype)

    return pl.pallas_call(
        body, name=name, grid=(n_g, m // bm, nt),
        in_specs=[pl.BlockSpec((bt, bm), lambda g, i, k: (k, i)), pl.BlockSpec((None, bt, n), lambda g, i, k: (g, k, 0))],
        out_specs=pl.BlockSpec((None, bm, n), lambda g, i, k: (g, i, 0)),
        out_shape=jax.ShapeDtypeStruct((n_g, m, n), BF16),
        scratch_shapes=[pltpu.VMEM((bm, n), F32)],
        compiler_params=_params(("parallel", "parallel", "arbitrary")))(a, b)


def _mix_bwd_a(dycat, u, o, conv_w, cnw, gnw, n_heads, nx, ncc, name):
    _, t, s = u.shape
    nc = nx + ncc
    c = CHUNK

    def body(dy_ref, h_ref, b_ref, c_ref, z_ref, rz_ref, hp_ref, hn_ref, cp_ref, cn_ref, o_ref, w_ref,
             cnw_ref, gnw_ref, g_ref, dz_ref, db_ref, drz_ref, do_ref, acc_ref):
        i = pl.program_id(0)

        @pl.when(i == 0)
        def _():
            acc_ref[...] = jnp.zeros_like(acc_ref)

        has_prev, has_next = _neighbours(i, nx, nc)
        a = c_ref[...] * h_ref[...]
        am, ap = _shifted(a, cp_ref[7:8] * hp_ref[7:8], cn_ref[0:1] * hn_ref[0:1], has_prev, has_next)
        w = w_ref[...]
        y0 = w[0:1] * am + w[1:2] * a + w[2:3] * ap
        bb = b_ref[...]
        yb = bb * y0
        r = lax.rsqrt(jnp.mean(yb * yb, axis=-1, keepdims=True) + EPS)
        ynn = yb * r
        z = z_ref[...]
        dyc = dy_ref[:, pl.ds(0, s)]
        cw = cnw_ref[...]
        dz_ref[...] = (dyc * (ynn * cw) * _dsilu(z)).astype(BF16)
        dyn = dyc * _silu(z)
        acc_ref[0:1, :] += jnp.sum(dyn * ynn, axis=0, keepdims=True)
        dynn = dyn * cw
        dyb = r * (dynn - ynn * jnp.mean(dynn * ynn, axis=-1, keepdims=True))
        db_ref[...] = (dyb * y0).astype(BF16)
        g_ref[...] = dyb * bb
        for h in range(n_heads):
            sl = pl.ds(h * HEAD_DIM, HEAD_DIM)
            ov = o_ref[:, sl]
            mu = jnp.mean(ov, axis=-1, keepdims=True)
            var = jnp.mean(jnp.square(ov - mu), axis=-1, keepdims=True)
            rs = lax.rsqrt(var + EPS)
            on = (ov - mu) * rs
            dyr = dy_ref[:, pl.ds(s + h * HEAD_DIM, HEAD_DIM)]
            rz = rz_ref[:, sl]
            gw = gnw_ref[:, sl]
            drz_ref[:, sl] = (dyr * (on * gw) * _dsilu(rz)).astype(BF16)
            dyg = dyr * _silu(rz)
            acc_ref[1:2, sl] += jnp.sum(dyg * on, axis=0, keepdims=True)
            don = dyg * gw
            do = rs * (don - jnp.mean(don, axis=-1, keepdims=True)
                       - on * jnp.mean(don * on, axis=-1, keepdims=True))
            do_ref[:, sl] = do.astype(BF16)

    def seg(g):
        return pl.BlockSpec((None, c, s), lambda i: (g, i, 0))

    prev, nxt = _halo_specs(s, t // 8)
    row = pl.BlockSpec((c, s), lambda i: (i, 0))
    return pl.pallas_call(
        body, name=name, grid=(nc,),
        in_specs=[pl.BlockSpec((c, 2 * s), lambda i: (i, 0)), seg(0), seg(1), seg(2), seg(3), seg(7),
                  prev(0), nxt(0), prev(2), nxt(2), row, _full((3, s)), _full((1, s)), _full((1, s))],
        out_specs=[row, row, row, row, row, _full((8, s))],
        out_shape=[jax.ShapeDtypeStruct((t, s), F32)] + [jax.ShapeDtypeStruct((t, s), BF16)] * 4
        + [jax.ShapeDtypeStruct((8, s), F32)],
        compiler_params=_params(("arbitrary",)),
    )(dycat, u, u, u, u, u, u, u, u, u, o, conv_w, cnw, gnw)


def _grad_state_sweep(qr, do, tabs, n_heads, nx, ncc, name):
    t, s = qr.shape
    nc = nx + ncc
    c = CHUNK

    def body(qf_ref, df_ref, qb_ref, db_ref, qft, qbt, cdf, cdb, gf_out, gb_out, gf, gb):
        @pl.when(pl.program_id(0) == 0)
        def _():
            gf[...] = jnp.zeros_like(gf)
            gb[...] = jnp.zeros_like(gb)

        for h in range(n_heads):
            sl = pl.ds(h * HEAD_DIM, HEAD_DIM)
            gf_out[h] = gf[h].astype(BF16)
            gb_out[h] = gb[h].astype(BF16)
            qd = (qf_ref[:, sl].astype(F32) * qft[h]).astype(BF16)
            gf[h] = cdf[h][0:1, :] * gf[h] + _dot(qd, df_ref[:, sl], TN)
            qd = (qb_ref[:, sl].astype(F32) * qbt[h]).astype(BF16)
            gb[h] = cdb[h][0:1, :] * gb[h] + _dot(qd, db_ref[:, sl], TN)

    fwd = pl.BlockSpec((c, s), lambda i: (_f_order(nc - 1 - i, nx, ncc), 0))
    bwd = pl.BlockSpec((c, s), lambda i: (i, 0))
    st = (None, n_heads, HEAD_DIM, HEAD_DIM)
    return pl.pallas_call(
        body, name=name, grid=(nc,),
        in_specs=[fwd, fwd, bwd, bwd, _full((n_heads, c, HEAD_DIM)), _full((n_heads, c, HEAD_DIM)),
                  _full((n_heads, 8, HEAD_DIM)), _full((n_heads, 8, HEAD_DIM))],
        out_specs=[pl.BlockSpec(st, lambda i: (_f_order(nc - 1 - i, nx, ncc), 0, 0, 0)),
                   pl.BlockSpec(st, lambda i: (i, 0, 0, 0))],
        out_shape=[jax.ShapeDtypeStruct((nc, n_heads, HEAD_DIM, HEAD_DIM), BF16)] * 2,
        scratch_shapes=[pltpu.VMEM((n_heads, HEAD_DIM, HEAD_DIM), F32)] * 2,
        compiler_params=_params(("arbitrary",)),
    )(qr, do, qr, do, tabs["qf"], tabs["qb"], tabs["cdf"], tabs["cdb"])


def _mix_bwd_b(u, g, dz, db, drz, qr, kr, vb, do, sf, sb, gf, gb, tabs, cos, sa, sb_tab, conv_w,
               n_heads, nx, ncc, name):
    _, t, s = u.shape
    nc = nx + ncc
    c = CHUNK
    k_scale = HEAD_DIM ** -0.5

    def body(h_ref, c_ref, g_ref, gp_ref, gn_ref, dz_ref, db_ref, drz_ref, q_ref, k_ref, v_ref, do_ref,
             sf_ref, sb_ref, gf_ref, gb_ref, dc_t, dlf_t, dlb_t, qft, kft, qbt, kbt, cdf, cdb, lg_ref,
             cos_ref, sa_ref, sb_ref2, w_ref, du_ref, dw_ref, dlg_ref):
        i = pl.program_id(0)

        @pl.when(i == 0)
        def _():
            dw_ref[...] = jnp.zeros_like(dw_ref)
            dlg_ref[...] = jnp.zeros_like(dlg_ref)

        has_prev, has_next = _neighbours(i, nx, nc)
        gv = g_ref[...]
        gm, gp = _shifted(gv, gp_ref[7:8], gn_ref[0:1], has_prev, has_next)
        w = w_ref[...]
        da = w[0:1] * gp + w[1:2] * gv + w[2:3] * gm
        hh, cc = h_ref[...], c_ref[...]
        du_ref[0] = (da * cc).astype(BF16)
        du_ref[2] = (da * hh).astype(BF16)
        a = cc * hh
        dw_ref[0:1, :] += jnp.sum(a * gp, axis=0, keepdims=True)
        dw_ref[1:2, :] += jnp.sum(a * gv, axis=0, keepdims=True)
        dw_ref[2:3, :] += jnp.sum(a * gm, axis=0, keepdims=True)
        du_ref[1] = db_ref[...]
        du_ref[3] = dz_ref[...]
        du_ref[7] = drz_ref[...]

        co, ra, rb = cos_ref[...], sa_ref[...], sb_ref2[...]
        pos = lax.broadcasted_iota(jnp.int32, (c, HEAD_DIM), 0).astype(F32)
        row8 = lax.broadcasted_iota(jnp.int32, (8, HEAD_DIM), 0)
        lane8 = lax.broadcasted_iota(jnp.int32, (8, HEAD_DIM), 1)
        dlg = jnp.zeros((8, HEAD_DIM), F32)
        for h in range(n_heads):
            sl = pl.ds(h * HEAD_DIM, HEAD_DIM)
            q, k, v, do = q_ref[:, sl], k_ref[:, sl], v_ref[:, sl], do_ref[:, sl]
            qf, kf, dof = q.astype(F32), k.astype(F32), do.astype(F32)
            s_f, s_b, g_f, g_b = sf_ref[h], sb_ref[h], gf_ref[h], gb_ref[h]
            p = _dot(q, k, NT)
            pd = _dot(do, v, NT)
            pdd = (pd * dc_t[h]).astype(BF16)
            dq = _dot(pdd, k, NN)
            dk = _dot(pdd, q, TN)
            dv = _dot((p * dc_t[h]).astype(BF16), do, TN)
            dq_f = _dot((dof * qft[h]).astype(BF16), s_f, NT)
            dq_b = _dot((dof * qbt[h]).astype(BF16), s_b, NT)
            dk_f = _dot(v, g_f, NT) * kft[h]
            dk_b = _dot(v, g_b, NT) * kbt[h]
            dv += _dot((kf * kft[h]).astype(BF16), g_f, NN) + _dot((kf * kbt[h]).astype(BF16), g_b, NN)
            ppd = p * pd
            cd_f, cd_b = cdf[h][0:1, :], cdb[h][0:1, :]
            t_f = _sum_all(dlf_t[h] * ppd + (pos + 1.0) * qf * dq_f + (c - 1.0 - pos) * kf * dk_f
                           + float(c) * (cd_f * (g_f.astype(F32) * s_f.astype(F32))))
            t_b = _sum_all(dlb_t[h] * ppd + (c - pos) * qf * dq_b + pos * kf * dk_b
                           + float(c) * (cd_b * (g_b.astype(F32) * s_b.astype(F32))))
            dlg += jnp.where((row8 == 0) & (lane8 == h), t_f, 0.0) + jnp.where((row8 == 1) & (lane8 == h), t_b, 0.0)
            du_ref[4, :, sl] = _rope_bwd(dq + dq_f + dq_b, co, ra, rb).astype(BF16)
            du_ref[5, :, sl] = (_rope_bwd(dk + dk_f + dk_b, co, ra, rb) * k_scale).astype(BF16)
            du_ref[6, :, sl] = dv.astype(BF16)
        dlg_ref[...] += dlg

        @pl.when(i == nc - 1)
        def _():
            dlg_ref[...] = dlg_ref[...] * lg_ref[...]

    def seg(gi):
        return pl.BlockSpec((None, c, s), lambda i: (gi, i, 0))

    per = c // 8
    n8 = t // 8
    row = pl.BlockSpec((c, s), lambda i: (i, 0))
    st = pl.BlockSpec((None, n_heads, HEAD_DIM, HEAD_DIM), lambda i: (i, 0, 0, 0))
    tab = pl.BlockSpec((c, HEAD_DIM), lambda i: (i, 0))
    hc = _full((n_heads, c, HEAD_DIM))
    cc_ = _full((n_heads, c, c))
    h8 = _full((n_heads, 8, HEAD_DIM))
    return pl.pallas_call(
        body, name=name, grid=(nc,),
        in_specs=[seg(0), seg(2), row,
                  pl.BlockSpec((8, s), lambda i: (jnp.maximum(i * per - 1, 0), 0)),
                  pl.BlockSpec((8, s), lambda i: (jnp.minimum((i + 1) * per, n8 - 1), 0)),
                  row, row, row, row, row, row, row, st, st, st, st, cc_, cc_, cc_, hc, hc, hc, hc, h8, h8,
                  _full((8, HEAD_DIM)), tab, tab, tab, _full((3, s))],
        out_specs=[pl.BlockSpec((8, c, s), lambda i: (0, i, 0)), _full((8, s)), _full((8, HEAD_DIM))],
        out_shape=[jax.ShapeDtypeStruct((8, t, s), BF16), jax.ShapeDtypeStruct((8, s), F32),
                   jax.ShapeDtypeStruct((8, HEAD_DIM), F32)],
        compiler_params=_params(("arbitrary",)),
    )(u, u, g, g, g, dz, db, drz, qr, kr, vb, do, sf, sb, gf, gb, tabs["dc"], tabs["dlf"], tabs["dlb"],
      tabs["qf"], tabs["kf"], tabs["qb"], tabs["kb"], tabs["cdf"], tabs["cdb"], tabs["lg"], cos, sa, sb_tab, conv_w)


def _in_proj_bwd(du, wg, name):
    n_seg, t, s = du.shape
    d = wg.shape[1]
    tm = _mm_rows(t)

    def body(a_ref, w_ref, o_ref):
        g = pl.program_id(1)
        part = _dot(a_ref[...], w_ref[...], NT)

        @pl.when(g == 0)
        def _():
            o_ref[...] = part

        @pl.when(g > 0)
        def _():
            o_ref[...] += part

    return pl.pallas_call(
        body, name=name, grid=(t // tm, n_seg),
        in_specs=[pl.BlockSpec((None, tm, s), lambda i, g: (g, i, 0)), pl.BlockSpec((None, d, s), lambda i, g: (g, 0, 0))],
        out_specs=pl.BlockSpec((tm, d), lambda i, g: (i, 0)),
        out_shape=jax.ShapeDtypeStruct((t, d), F32),
        compiler_params=_params(("parallel", "arbitrary")))(du, wg)


def _prenorm_bwd(dhx, xt, dxo, nw, mod, n_lat, name):
    t, d = xt.shape
    nxb = n_lat // ROW_TILE

    def body(dh_ref, x_ref, dxo_ref, nw_ref, mod_ref, dx_ref, acc_ref):
        i = pl.program_id(0)

        @pl.when(i == 0)
        def _():
            acc_ref[...] = jnp.zeros_like(acc_ref)

        ctx = i >= nxb
        m = mod_ref[...]
        scale1 = 1.0 + jnp.where(ctx, m[4:5], m[1:2])
        x = x_ref[...]
        nw_v = nw_ref[...]
        r = lax.rsqrt(jnp.mean(x * x, axis=-1, keepdims=True) + EPS)
        xn = x * r
        dh = dh_ref[...]
        dshift = jnp.sum(dh, axis=0, keepdims=True)
        dscale = jnp.sum(dh * (xn * nw_v), axis=0, keepdims=True)
        acc_ref[6:7, :] += jnp.sum(dh * scale1 * xn, axis=0, keepdims=True)

        @pl.when(i < nxb)
        def _():
            acc_ref[0:1, :] += dshift
            acc_ref[1:2, :] += dscale

        @pl.when(i >= nxb)
        def _():
            acc_ref[3:4, :] += dshift
            acc_ref[4:5, :] += dscale

        dxn = dh * (nw_v * scale1)
        dx_ref[...] = dxo_ref[...] + r * (dxn - xn * jnp.mean(dxn * xn, axis=-1, keepdims=True))

    row = pl.BlockSpec((ROW_TILE, d), lambda i: (i, 0))
    return pl.pallas_call(body, name=name, grid=(t // ROW_TILE,),
                          in_specs=[row, row, row, _full((1, d)), _full((8, d))], out_specs=[row, _full((8, d))],
                          out_shape=[jax.ShapeDtypeStruct((t, d), F32), jax.ShapeDtypeStruct((8, d), F32)],
                          compiler_params=_params(("arbitrary",)))(dhx, xt, dxo, nw, mod)


def _adamw(g, w, m, v):
    m = ADAM_B1 * m + (1.0 - ADAM_B1) * g
    v = ADAM_B2 * v + (1.0 - ADAM_B2) * jnp.square(g)
    m_hat = m / (1.0 - ADAM_B1 ** ADAM_STEP)
    v_hat = v / (1.0 - ADAM_B2 ** ADAM_STEP)
    delta = -ADAM_LR * (m_hat / (jnp.sqrt(v_hat) + ADAM_EPS) + ADAM_WD * w)
    return delta, m, v


def _sum_adamw(parts, w, m, v, name):
    n_p, r, n = parts.shape
    part_block_bytes = 4 * 1024 * 1024
    br = 8
    for cand in (512, 256, 128, 64, 32, 16):
        if r % cand == 0 and n_p * cand * n * parts.dtype.itemsize <= part_block_bytes:
            br = cand
            break

    def body(p_ref, w_ref, m_ref, v_ref, g_out, d_out, m_out, v_out):
        g = p_ref[0].astype(F32)
        for j in range(1, n_p):
            g = g + p_ref[j].astype(F32)
        g_out[...] = g
        d_out[...], m_out[...], v_out[...] = _adamw(g, w_ref[...], m_ref[...], v_ref[...])

    row = pl.BlockSpec((br, n), lambda i: (i, 0))
    return pl.pallas_call(body, name=name, grid=(r // br,),
                          in_specs=[pl.BlockSpec((n_p, br, n), lambda i: (0, i, 0)), row, row, row],
                          out_specs=[row] * 4, out_shape=[jax.ShapeDtypeStruct((r, n), F32)] * 4,
                          compiler_params=_params(("parallel",)))(parts, w, m, v)


def _rope_tables(n_lat, n_ctx):
    f = HEAD_DIM // 4
    rows = n_lat // GRID_W
    row_pos = jnp.repeat(jnp.arange(rows), GRID_W).astype(F32)
    col_pos = jnp.tile(jnp.arange(GRID_W), rows).astype(F32)
    inv = ROPE_BASE ** (-jnp.arange(f, dtype=F32) / f)
    ang_r = row_pos[:, None] * inv[None, :]
    ang_c = col_pos[:, None] * inv[None, :]
    zero = jnp.zeros_like(ang_r)
    cos = jnp.concatenate([jnp.cos(ang_r), jnp.cos(ang_r), jnp.cos(ang_c), jnp.cos(ang_c)], axis=-1)
    sa = jnp.concatenate([-jnp.sin(ang_r), zero, -jnp.sin(ang_c), zero], axis=-1)
    sb = jnp.concatenate([zero, jnp.sin(ang_r), zero, jnp.sin(ang_c)], axis=-1)
    pad = jnp.zeros((n_ctx, HEAD_DIM), F32)
    return (jnp.concatenate([cos, pad + 1.0], axis=0), jnp.concatenate([sa, pad], axis=0),
            jnp.concatenate([sb, pad], axis=0))


def _pad_rows(a, rows):
    if a.shape[0] == rows:
        return a
    return jnp.concatenate([a, jnp.zeros((rows - a.shape[0],) + a.shape[1:], a.dtype)], axis=0)


def _pad_cols(a, cols):
    if a.shape[1] == cols:
        return a
    return jnp.concatenate([a, jnp.zeros((a.shape[0], cols - a.shape[1]), a.dtype)], axis=1)


def kernel(x, c, ctx, c_ctx, norm_w, w_mod, b_mod, w_in, conv_w, conv_norm_w, ret_norm_w, ret_decay_f, ret_decay_b, w_out, final_norm_w, loss_target, m_c_ctx, m_norm_w, m_w_mod, m_b_mod, m_w_in, m_conv_w, m_conv_norm_w, m_ret_norm_w, m_ret_decay_f, m_ret_decay_b, m_w_out, m_final_norm_w, v_c_ctx, v_norm_w, v_w_mod, v_b_mod, v_w_in, v_conv_w, v_conv_norm_w, v_ret_norm_w, v_ret_decay_f, v_ret_decay_b, v_w_out, v_final_norm_w):
    depth = norm_w.shape[0]
    n_lat, d = x.shape[1], x.shape[2]
    n_ctx = ctx.shape[1]
    s = d // 2
    n_heads = ret_decay_f.shape[1]
    nx, ncc = n_lat // CHUNK, n_ctx // CHUNK
    n_mod = w_mod.shape[2]
    n_cw = conv_w.shape[2]
    r_out = w_out.shape[1]
    assert s == n_heads * HEAD_DIM and w_in.shape[2] == s and N_DEV * r_out == d
    assert n_lat % ROW_TILE == 0 and n_ctx % ROW_TILE == 0 and 3 * depth * n_cw <= d and d >= 3 * n_mod // 3
    me = 4 * lax.axis_index("x") + 2 * lax.axis_index("y") + lax.axis_index("c")

    first = jnp.concatenate([c.reshape(1, d), _pad_cols(conv_w.reshape(1, -1), d), jnp.zeros((6, d), F32)], axis=0)
    (first_g,) = _all_gather([first], "gather_cond", True)
    first_g = first_g.reshape(N_DEV, 8, d)
    c_all = first_g[:, 0, :]
    conv_full = first_g[:, 1, :3 * depth * n_cw].reshape(N_DEV, depth, 3, n_cw)
    conv_full = conv_full.transpose(1, 2, 0, 3).reshape(depth, 3, N_DEV * n_cw)
    c9 = jnp.concatenate([c_all, c_ctx.reshape(1, d), jnp.zeros((7, d), F32)], axis=0)

    b_sh = lax.dynamic_slice(b_mod, (0, me * n_mod), (depth, n_mod))
    mod_sh = jnp.concatenate([_mod_rows(c9, w_mod[l], b_sh[l:l + 1], f"mod_rows_l{l}") for l in range(depth)], axis=0)
    (mod_g,) = _all_gather([mod_sh], "gather_mod", True)
    mod_g = mod_g.reshape(N_DEV, depth, 16, n_mod)
    mods = []
    for l in range(depth):
        mine = lax.dynamic_index_in_dim(mod_g[:, l], me, axis=1, keepdims=False).reshape(3, d)
        cx = mod_g[:, l, 8, :].reshape(3, d)
        mods.append(jnp.concatenate([mine, cx, jnp.zeros((2, d), F32)], axis=0))

    shards = []
    for l in range(depth):
        shards += [w_in[l].astype(BF16), w_out[l].astype(BF16)]
    gathered = _all_gather(shards, "gather_weights", False)
    w_in_g = [gathered[2 * l].reshape(N_DEV, d, s) for l in range(depth)]
    w_out_g = [gathered[2 * l + 1] for l in range(depth)]

    cos, sa, sb_tab = _rope_tables(n_lat, n_ctx)
    xt = jnp.concatenate([x[0], ctx[0]], axis=0)

    saved = []
    for l in range(depth):
        names = ["dc", "dlf", "dlb", "qf", "kf", "qb", "kb", "cdf", "cdb", "lg"]
        dec = jnp.stack([ret_decay_f[l], ret_decay_b[l]], axis=0)
        tabs = dict(zip(names, _decay_tables(dec, n_heads, f"decay_tables_l{l}")))
        hx = _prenorm(xt, norm_w[l:l + 1], mods[l], n_lat, f"prenorm_l{l}")
        u = _in_proj(hx, w_in_g[l], f"in_proj_l{l}")
        qr, kr, vb = _rope_qkv(u, cos, sa, sb_tab, n_heads, f"rope_l{l}")
        sf, sb = _state_sweep(kr, vb, tabs, n_heads, nx, ncc, f"state_sweep_l{l}")
        ycat, o = _mix_fwd(u, qr, kr, vb, sf, sb, tabs, conv_full[l], conv_norm_w[l:l + 1], ret_norm_w[l:l + 1],
                           n_heads, nx, ncc, f"mix_fwd_l{l}")
        m_res, x_new = _out_proj(ycat, w_out_g[l], xt, mods[l], n_lat, f"out_proj_l{l}")
        saved.append(dict(tabs=tabs, xt=xt, hx=hx, u=u, qr=qr, kr=kr, vb=vb, sf=sf, sb=sb, ycat=ycat, o=o, m=m_res))
        xt = x_new

    dxt, loss_blk, dfnw = _loss_head(xt, loss_target[0], final_norm_w.reshape(1, d), n_lat, "loss_head")
    loss = lax.psum(loss_blk[0, 0], MESH_AXES)

    dmod_x, dmod_c, dnw, dcnw, dgnw, dconv, ddec, dwin, dwout = [], [], [], [], [], [], [], [], []
    for l in reversed(range(depth)):
        sv = saved[l]
        dm, gate_acc = _gate_bwd(dxt, sv["m"], mods[l], n_lat, f"gate_bwd_l{l}")
        dycat = _matmul_nt(dm, w_out_g[l], f"out_proj_bwd_l{l}")
        dwout.append(_weight_grad(sv["ycat"], dm.reshape(1, *dm.shape), f"w_out_grad_l{l}")[0])
        g, dz, db, drz, do, norm_acc = _mix_bwd_a(dycat, sv["u"], sv["o"], conv_full[l], conv_norm_w[l:l + 1],
                                                   ret_norm_w[l:l + 1], n_heads, nx, ncc, f"mix_bwd_a_l{l}")
        gf, gb = _grad_state_sweep(sv["qr"], do, sv["tabs"], n_heads, nx, ncc, f"grad_state_sweep_l{l}")
        du, conv_acc, dlg = _mix_bwd_b(sv["u"], g, dz, db, drz, sv["qr"], sv["kr"], sv["vb"], do, sv["sf"], sv["sb"],
                                       gf, gb, sv["tabs"], cos, sa, sb_tab, conv_full[l], n_heads, nx, ncc,
                                       f"mix_bwd_b_l{l}")
        dwin.append(_weight_grad(sv["hx"], du, f"w_in_grad_l{l}"))
        dhx = _in_proj_bwd(du, w_in_g[l], f"in_proj_bwd_l{l}")
        dxt, pre_acc = _prenorm_bwd(dhx, sv["xt"], dxt, norm_w[l:l + 1], mods[l], n_lat, f"prenorm_bwd_l{l}")
        dmod_x.append(jnp.concatenate([pre_acc[0], pre_acc[1], gate_acc[2]]))
        dmod_c.append(jnp.concatenate([pre_acc[3], pre_acc[4], gate_acc[5]]))
        dnw.append(pre_acc[6])
        dcnw.append(norm_acc[0])
        dgnw.append(norm_acc[1])
        dconv.append(conv_acc[0:3])
        ddec.append(dlg[0:2, :n_heads])
    for lst in (dmod_x, dmod_c, dnw, dcnw, dgnw, dconv, ddec, dwin, dwout):
        lst.reverse()
    grad_x = dxt[:n_lat].reshape(1, n_lat, d)

    rows = []
    for l in range(depth):
        rows += [dmod_x[l], dmod_c[l]]
    (dmod_g,) = _all_gather([_pad_rows(jnp.stack(rows, axis=0), 8)], "gather_dmod", True)
    dmod_g = dmod_g.reshape(N_DEV, 8, 3 * d)
    mine_cols = lax.dynamic_slice(dmod_g, (0, 0, me * n_mod), (N_DEV, 8, n_mod))
    g_wmod, dcc = [], jnp.zeros((d,), F32)
    for l in range(depth):
        gw, dc_part = _mod_grads(mine_cols[:, 2 * l], mine_cols[:, 2 * l + 1], c9, w_mod[l], f"mod_grads_l{l}")
        g_wmod.append(gw)
        dcc = dcc + dc_part[0]

    n_small = 16
    small = jnp.concatenate([
        jnp.stack(dnw, axis=0),
        jnp.concatenate(dcnw).reshape(1, -1),
        jnp.concatenate(dgnw).reshape(1, -1),
        dfnw[0:1],
        dcc.reshape(1, d),
        jnp.stack(dconv, axis=0).reshape(-1, d),
        _pad_cols(jnp.stack(ddec, axis=0).reshape(1, -1), d),
    ], axis=0)
    assert depth * s == d and small.shape[0] <= n_small
    n_rows = small.shape[0]
    (small_g,) = _all_gather([_pad_rows(small, n_small)], "gather_small", True)
    small_g = small_g.reshape(N_DEV, n_small, d)

    def pack_small(nw_, cn_, gn_, fn_, cc_, df_, db_):
        return _pad_rows(jnp.concatenate([
            nw_, cn_.reshape(1, -1), gn_.reshape(1, -1), fn_.reshape(1, d), cc_.reshape(1, d),
            jnp.zeros((n_rows - depth - 5, d), F32),
            _pad_cols(jnp.stack([df_, db_], axis=1).reshape(1, -1), d)], axis=0), n_small)

    w_s = pack_small(norm_w, conv_norm_w, ret_norm_w, final_norm_w, c_ctx, ret_decay_f, ret_decay_b)
    m_s = pack_small(m_norm_w, m_conv_norm_w, m_ret_norm_w, m_final_norm_w, m_c_ctx, m_ret_decay_f, m_ret_decay_b)
    v_s = pack_small(v_norm_w, v_conv_norm_w, v_ret_norm_w, v_final_norm_w, v_c_ctx, v_ret_decay_f, v_ret_decay_b)
    small_out = _sum_adamw(small_g, w_s, m_s, v_s, "adamw_small")

    def unpack_small(a):
        nw_ = a[0:depth]
        cn_ = a[depth].reshape(depth, s)
        gn_ = a[depth + 1].reshape(depth, s)
        fn_ = a[depth + 2]
        cc_ = a[depth + 3]
        dd = a[n_rows - 1, :depth * 2 * n_heads].reshape(depth, 2, n_heads)
        return dict(c_ctx=cc_, norm_w=nw_, conv_norm_w=cn_, ret_norm_w=gn_, ret_decay_f=dd[:, 0], ret_decay_b=dd[:, 1],
                    final_norm_w=fn_)

    res = {}
    for kind, arr in zip(("grad", "delta", "m", "v"), small_out):
        for k_, val in unpack_small(arr).items():
            res[(kind, k_)] = val

    bm_parts = jnp.concatenate([dmod_g[:, 0:2 * depth:2].reshape(N_DEV, depth, 3 * d),
                                dmod_g[:, 1:2 * depth:2].reshape(N_DEV, depth, 3 * d)], axis=0)
    bm_parts = jnp.concatenate([bm_parts, jnp.zeros((2 * N_DEV, 8 - depth, 3 * d), F32)], axis=1)
    pad8 = lambda a: _pad_rows(a, 8)
    bm_out = _sum_adamw(bm_parts, pad8(b_mod), pad8(m_b_mod), pad8(v_b_mod), "adamw_b_mod")
    for kind, arr in zip(("grad", "delta", "m", "v"), bm_out):
        res[(kind, "b_mod")] = arr[:depth]

    conv_rows = small_g[:, depth + 4:depth + 4 + 3 * depth * s // d].reshape(N_DEV, depth * 3, s)
    conv_mine = lax.dynamic_slice(conv_rows, (0, 0, me * n_cw), (N_DEV, depth * 3, n_cw))
    conv_mine = jnp.concatenate([conv_mine, jnp.zeros((N_DEV, 8 - depth * 3, n_cw), F32)], axis=1)
    cw2 = lambda a: _pad_rows(a.reshape(depth * 3, n_cw), 8)
    cw_out = _sum_adamw(conv_mine, cw2(conv_w), cw2(m_conv_w), cw2(v_conv_w), "adamw_conv_w")
    for kind, arr in zip(("grad", "delta", "m", "v"), cw_out):
        res[(kind, "conv_w")] = arr[:depth * 3].reshape(depth, 3, n_cw)

    wm_out = _sum_adamw(jnp.stack(g_wmod, axis=0).reshape(1, depth * d, n_mod), w_mod.reshape(depth * d, n_mod),
                        m_w_mod.reshape(depth * d, n_mod), v_w_mod.reshape(depth * d, n_mod), "adamw_w_mod")
    for kind, arr in zip(("grad", "delta", "m", "v"), wm_out):
        res[(kind, "w_mod")] = arr.reshape(depth, d, n_mod)

    dwout_blk = [g_.reshape(N_DEV, r_out, d) for g_ in dwout]
    win_parts, wout_parts = _all_to_all([dwin, dwout_blk], "grad_exchange")
    wi_out = _sum_adamw(win_parts, w_in.reshape(depth * d, s), m_w_in.reshape(depth * d, s),
                        v_w_in.reshape(depth * d, s), "adamw_w_in")
    for kind, arr in zip(("grad", "delta", "m", "v"), wi_out):
        res[(kind, "w_in")] = arr.reshape(depth, d, s)
    wo_out = _sum_adamw(wout_parts, w_out.reshape(depth * r_out, d), m_w_out.reshape(depth * r_out, d),
                        v_w_out.reshape(depth * r_out, d), "adamw_w_out")
    for kind, arr in zip(("grad", "delta", "m", "v"), wo_out):
        res[(kind, "w_out")] = arr.reshape(depth, r_out, d)

    order = ["c_ctx", "norm_w", "w_mod", "b_mod", "w_in", "conv_w", "conv_norm_w", "ret_norm_w", "ret_decay_f",
             "ret_decay_b", "w_out", "final_norm_w"]
    outs = [loss, grad_x]
    for kind in ("grad", "delta", "m", "v"):
        outs += [res[(kind, k_)] for k_ in order]
    return tuple(outs)
```

```python
import functools

import jax
import jax.numpy as jnp
from jax import lax
from jax.experimental import pallas as pl
from jax.experimental.pallas import tpu as pltpu

F32 = jnp.float32
BF16 = jnp.bfloat16

EPS = 1e-6
CHUNK = 128
HEAD_DIM = 128
GRID_W = 64
ROPE_BASE = 10000.0
N_DEV = 8
ADAM_LR, ADAM_B1, ADAM_B2, ADAM_EPS, ADAM_WD, ADAM_STEP = 0.001, 0.9, 0.999, 1e-08, 0.01, 10

ROW_TILE = 256
V7X_VMEM_LIMIT = 56 * 1024 * 1024
MESH_AXES = ("x", "y", "c")

NN = ((1,), (0,))
NT = ((1,), (1,))
TN = ((0,), (0,))


def _dot(a, b, dims):
    return lax.dot_general(a, b, (dims, ((), ())), preferred_element_type=F32)


def _params(sem=None):
    if sem is None:
        return pltpu.CompilerParams(vmem_limit_bytes=V7X_VMEM_LIMIT)
    return pltpu.CompilerParams(dimension_semantics=sem, vmem_limit_bytes=V7X_VMEM_LIMIT)


def _silu(z):
    return z * jax.nn.sigmoid(z)


def _dsilu(z):
    s = jax.nn.sigmoid(z)
    return s * (1.0 + z * (1.0 - s))


def _sum_all(a):
    return jnp.sum(jnp.sum(a, axis=1, keepdims=True), axis=0, keepdims=True)


def _mm_rows(t):
    return 768 if t % 768 == 0 else ROW_TILE


def _full(shape):
    n = len(shape)
    return pl.BlockSpec(shape, lambda *_: (0,) * n)


def _peers(x, y, c):
    return [(x, y, 1 - c), (1 - x, y, c), (x, 1 - y, c), (1 - x, 1 - y, c),
            (1 - x, y, 1 - c), (x, 1 - y, 1 - c), (1 - x, 1 - y, 1 - c)]


def _lin(p):
    return 4 * p[0] + 2 * p[1] + p[2]


def _all_gather(arrays, name, in_vmem):
    n_arr = len(arrays)
    space = pltpu.VMEM if in_vmem else pl.ANY

    def body(*refs):
        ins, outs = refs[:n_arr], refs[n_arr:2 * n_arr]
        send_sems, recv_sems, local_sems = refs[2 * n_arr:]
        x, y, c = lax.axis_index("x"), lax.axis_index("y"), lax.axis_index("c")
        me, sibling = (x, y, c), (x, y, 1 - c)
        chips = [(1 - x, y), (x, 1 - y), (1 - x, 1 - y)]
        every = []
        locals_ = []
        for a in range(n_arr):
            m_per = ins[a].shape[0]
            out_ref = outs[a]

            def rows(p, out_ref=out_ref, m_per=m_per):
                return out_ref.at[pl.ds(_lin(p) * m_per, m_per), :]

            def copy(k, block, to, src=None, a=a, rows=rows):
                return pltpu.make_async_remote_copy(
                    src_ref=rows(block) if src is None else src, dst_ref=rows(block),
                    send_sem=send_sems.at[a, k], recv_sem=recv_sems.at[a, k],
                    device_id=to, device_id_type=pl.DeviceIdType.MESH)

            mine = pltpu.make_async_copy(ins[a], rows(me), local_sems.at[a])
            mine.start()
            locals_.append(mine)
            first = [copy(0, me, sibling, src=ins[a])]
            first += [copy(1 + j, me, (*chip, c), src=ins[a]) for j, chip in enumerate(chips)]
            for cp in first:
                cp.start()
            every.append((copy, first))
        sends = []
        for a in range(n_arr):
            copy, first = every[a]
            passed = [copy(4 + j, (*chip, c), sibling) for j, chip in enumerate(chips)]
            for j, chip in enumerate(chips):
                copy(1 + j, (*chip, c), me).wait_recv()
                passed[j].start()
            sends += first + passed
        for a in range(n_arr):
            copy, _ = every[a]
            copy(0, sibling, me).wait_recv()
            for j, chip in enumerate(chips):
                copy(4 + j, (*chip, 1 - c), me).wait_recv()
        for cp in sends:
            cp.wait_send()
        for mine in locals_:
            mine.wait()

    outs = pl.pallas_call(
        body, name=name,
        out_shape=[jax.ShapeDtypeStruct((N_DEV * a.shape[0], a.shape[1]), a.dtype) for a in arrays],
        in_specs=[pl.BlockSpec(memory_space=space)] * n_arr,
        out_specs=[pl.BlockSpec(memory_space=space)] * n_arr,
        scratch_shapes=[pltpu.SemaphoreType.DMA((n_arr, 7)), pltpu.SemaphoreType.DMA((n_arr, 7)),
                        pltpu.SemaphoreType.DMA((n_arr,))],
        compiler_params=_params(),
    )(*arrays)
    return list(outs)


def _all_to_all(groups, name):
    arrays = [a for grp in groups for a in grp]
    n_arr = len(arrays)
    n_grp = len(groups)
    place = []
    for gi, grp in enumerate(groups):
        off = 0
        for a in grp:
            place.append((gi, off, a.shape[1]))
            off += a.shape[1]

    def body(*refs):
        ins, outs = refs[:n_arr], refs[n_arr:n_arr + n_grp]
        send_sems, recv_sems, local_sems = refs[n_arr + n_grp:]
        x, y, c = lax.axis_index("x"), lax.axis_index("y"), lax.axis_index("c")
        me = (x, y, c)
        peers = _peers(x, y, c)

        def landing(a, sender):
            gi, off, rows = place[a]
            return outs[gi].at[_lin(sender), pl.ds(off, rows), :]

        locals_ = []
        sends = []
        for a in range(n_arr):
            mine = pltpu.make_async_copy(ins[a].at[_lin(me)], landing(a, me), local_sems.at[a])
            mine.start()
            locals_.append(mine)
        for k, peer in enumerate(peers):
            for a in range(n_arr):
                cp = pltpu.make_async_remote_copy(
                    src_ref=ins[a].at[_lin(peer)], dst_ref=landing(a, me),
                    send_sem=send_sems.at[a, k], recv_sem=recv_sems.at[a, k],
                    device_id=peer, device_id_type=pl.DeviceIdType.MESH)
                cp.start()
                sends.append(cp)
        for k, peer in enumerate(peers):
            for a in range(n_arr):
                pltpu.make_async_remote_copy(
                    src_ref=ins[a].at[_lin(me)], dst_ref=landing(a, peer),
                    send_sem=send_sems.at[a, k], recv_sem=recv_sems.at[a, k],
                    device_id=peer, device_id_type=pl.DeviceIdType.MESH).wait_recv()
        for cp in sends:
            cp.wait_send()
        for mine in locals_:
            mine.wait()

    outs = pl.pallas_call(
        body, name=name,
        out_shape=[jax.ShapeDtypeStruct((N_DEV, sum(a.shape[1] for a in grp), grp[0].shape[2]), grp[0].dtype)
                   for grp in groups],
        in_specs=[pl.BlockSpec(memory_space=pl.ANY)] * n_arr,
        out_specs=[pl.BlockSpec(memory_space=pl.ANY)] * n_grp,
        scratch_shapes=[pltpu.SemaphoreType.DMA((n_arr, 7)), pltpu.SemaphoreType.DMA((n_arr, 7)),
                        pltpu.SemaphoreType.DMA((n_arr,))],
        compiler_params=_params(),
    )(*arrays)
    return list(outs)


_HBM = pl.BlockSpec(memory_space=pltpu.HBM)
_SEM = pl.BlockSpec(memory_space=pltpu.SEMAPHORE)
_DATAFLOW = pltpu.SideEffectType.DATAFLOW_SIDE_EFFECTING


def _push_copies(src_refs, land_refs, send_sems, recv_sems, scatter):
    x, y, c = lax.axis_index("x"), lax.axis_index("y"), lax.axis_index("c")
    me = (x, y, c)
    out, back = [], []
    for k, peer in enumerate(_peers(x, y, c)):
        for a, (src, land) in enumerate(zip(src_refs, land_refs)):
            sems = dict(send_sem=send_sems.at[7 * a + k], recv_sem=recv_sems.at[7 * a + k],
                        device_id=peer, device_id_type=pl.DeviceIdType.MESH)
            mine = src.at[_lin(peer)] if scatter else src
            out.append(pltpu.make_async_remote_copy(src_ref=mine, dst_ref=land.at[_lin(me)], **sems))
            back.append(pltpu.make_async_remote_copy(src_ref=mine, dst_ref=land.at[_lin(peer)], **sems))
    return out, back


def _push_start(srcs, lands, scatter, name, after=()):
    n = len(srcs)
    n_in = 2 * n + len(after)

    def body(*refs):
        send_sems, recv_sems = refs[n_in], refs[n_in + 1]
        out, _ = _push_copies(refs[:n], refs[n:2 * n], send_sems, recv_sems, scatter)
        for cp in out:
            cp.start()
        token = refs[-1]
        token[...] = jnp.zeros_like(token)

    both = list(srcs) + list(lands)
    res = pl.pallas_call(
        body, name=name,
        out_shape=[pltpu.SemaphoreType.DMA((7 * n,)), pltpu.SemaphoreType.DMA((7 * n,))]
        + [pltpu.HBM(a.shape, a.dtype) for a in both] + [jax.ShapeDtypeStruct((8, 128), F32)],
        in_specs=[_HBM] * (2 * n) + [pl.BlockSpec(memory_space=pl.ANY)] * len(after),
        out_specs=[_SEM, _SEM] + [_HBM] * (2 * n) + [pl.BlockSpec(memory_space=pltpu.VMEM)],
        input_output_aliases={i: 2 + i for i in range(2 * n)},
        compiler_params=pltpu.CompilerParams(has_side_effects=_DATAFLOW),
    )(*[pltpu.with_memory_space_constraint(a, pltpu.HBM) for a in both], *after)
    return res[0], res[1], list(res[2:2 + n]), list(res[2 + n:2 + 2 * n]), res[-1]


def _push_wait(send_sems, recv_sems, srcs, lands, scatter, after, name):
    n = len(srcs)

    def body(*refs):
        out, back = _push_copies(refs[:n], refs[n:2 * n], refs[2 * n], refs[2 * n + 1], scatter)
        for cp in out:
            cp.wait_send()
        for cp in back:
            cp.wait_recv()

    both = list(srcs) + list(lands)
    res = pl.pallas_call(
        body, name=name,
        out_shape=[pltpu.HBM(a.shape, a.dtype) for a in both],
        in_specs=[_HBM] * (2 * n) + [_SEM, _SEM, pl.BlockSpec(memory_space=pl.ANY)],
        out_specs=[_HBM] * (2 * n),
        input_output_aliases={i: i for i in range(2 * n)},
        compiler_params=pltpu.CompilerParams(has_side_effects=_DATAFLOW),
    )(*both, send_sems, recv_sems, after)
    return list(res[n:])


def _landing(own, me):
    zone = lax.empty((N_DEV,) + own.shape, own.dtype)
    return lax.dynamic_update_slice(zone, own[None], (me,) + (0,) * own.ndim)


def _mod_rows(c9, w_mod, b_sh, name):
    n = w_mod.shape[1]

    def body(c_ref, w_ref, b_ref, o_ref):
        s9 = _silu(c_ref[...]).astype(BF16)
        o_ref[...] = _dot(s9, w_ref[...].astype(BF16), NN) + b_ref[...]

    return pl.pallas_call(body, name=name, out_shape=jax.ShapeDtypeStruct((16, n), F32),
                          compiler_params=_params())(c9, w_mod, b_sh)


def _mod_grads(dm_rows, dc_rows, c9, w_mod, name):
    d, n = w_mod.shape

    def body(dm_ref, dc_ref, c_ref, w_ref, gw_ref, dc_out):
        dc = dc_ref[...]
        tot = dc[0:1]
        for j in range(1, N_DEV):
            tot = tot + dc[j:j + 1]
        row = lax.broadcasted_iota(jnp.int32, (8, n), 0)
        lower = jnp.where(row == 0, tot, 0.0)
        dmod9 = jnp.concatenate([dm_ref[...], lower], axis=0).astype(BF16)
        c9v = c_ref[...]
        s9 = _silu(c9v).astype(BF16)
        gw_ref[...] = _dot(s9, dmod9, TN)
        ds = _dot(lower.astype(BF16), w_ref[...].astype(BF16), NT)
        dc_out[...] = ds * _dsilu(c9v[8:16])

    return pl.pallas_call(body, name=name,
                          out_shape=[jax.ShapeDtypeStruct((d, n), F32), jax.ShapeDtypeStruct((8, d), F32)],
                          compiler_params=_params())(dm_rows, dc_rows, c9, w_mod)


def _decay_tables(dec, n_heads, name):
    c = CHUNK

    def body(dec_ref, dc_ref, dlf_ref, dlb_ref, qf_ref, kf_ref, qb_ref, kb_ref, cdf_ref, cdb_ref, lg_ref):
        h = pl.program_id(0)
        d = dec_ref[...]
        lane = lax.broadcasted_iota(jnp.int32, d.shape, 1)
        lg = -jnp.exp(jnp.sum(jnp.where(lane == h, d, 0.0), axis=1, keepdims=True))
        lgf, lgb = lg[0:1], lg[1:2]
        i = lax.broadcasted_iota(jnp.int32, (c, c), 0).astype(F32)
        j = lax.broadcasted_iota(jnp.int32, (c, c), 1).astype(F32)
        diff = i - j
        d_f = jnp.where(diff >= 0, jnp.exp(lgf * jnp.maximum(diff, 0.0)), 0.0)
        d_b = jnp.where(diff <= 0, jnp.exp(lgb * jnp.maximum(-diff, 0.0)), 0.0)
        dc_ref[...] = d_f + d_b
        dlf_ref[...] = diff * d_f
        dlb_ref[...] = -diff * d_b
        pos = lax.broadcasted_iota(jnp.int32, (c, HEAD_DIM), 0).astype(F32)
        qf_ref[...] = jnp.exp(lgf * (pos + 1.0))
        kf_ref[...] = jnp.exp(lgf * (c - 1.0 - pos))
        qb_ref[...] = jnp.exp(lgb * (c - pos))
        kb_ref[...] = jnp.exp(lgb * pos)
        ones = jnp.ones((8, HEAD_DIM), F32)
        cdf_ref[...] = jnp.exp(lgf * float(c)) * ones
        cdb_ref[...] = jnp.exp(lgb * float(c)) * ones

        @pl.when(h == 0)
        def _():
            lg_ref[...] = jnp.zeros_like(lg_ref)

        row8 = lax.broadcasted_iota(jnp.int32, (8, HEAD_DIM), 0)
        lane8 = lax.broadcasted_iota(jnp.int32, (8, HEAD_DIM), 1)
        lg_ref[...] += (jnp.where((row8 == 0) & (lane8 == h), lgf, 0.0)
                        + jnp.where((row8 == 1) & (lane8 == h), lgb, 0.0))

    def per_head(*tail):
        return pl.BlockSpec((None,) + tail, lambda h: (h,) + (0,) * len(tail))

    shapes = [(c, c)] * 3 + [(c, HEAD_DIM)] * 4 + [(8, HEAD_DIM)] * 2
    return pl.pallas_call(
        body, name=name, grid=(n_heads,),
        in_specs=[_full(dec.shape)],
        out_specs=[per_head(*s) for s in shapes] + [_full((8, HEAD_DIM))],
        out_shape=[jax.ShapeDtypeStruct((n_heads,) + s, F32) for s in shapes]
        + [jax.ShapeDtypeStruct((8, HEAD_DIM), F32)],
        compiler_params=_params(("arbitrary",)),
    )(dec)


def _prenorm(xt, nw, mod, n_lat, name, after=()):
    t, d = xt.shape
    nxb = n_lat // ROW_TILE

    def body(x_ref, nw_ref, mod_ref, *rest):
        o_ref = rest[-1]
        ctx = pl.program_id(0) >= nxb
        m = mod_ref[...]
        shift = jnp.where(ctx, m[3:4], m[0:1])
        scale = jnp.where(ctx, m[4:5], m[1:2])
        x = x_ref[...]
        r = lax.rsqrt(jnp.mean(x * x, axis=-1, keepdims=True) + EPS)
        o_ref[...] = ((x * r) * nw_ref[...] * (1.0 + scale) + shift).astype(BF16)

    row = pl.BlockSpec((ROW_TILE, d), lambda i: (i, 0))
    return pl.pallas_call(body, name=name, grid=(t // ROW_TILE,),
                          in_specs=[row, _full((1, d)), _full((8, d))] + [pl.BlockSpec(memory_space=pl.ANY)] * len(after),
                          out_specs=row, out_shape=jax.ShapeDtypeStruct((t, d), BF16),
                          compiler_params=_params(("parallel",)))(xt, nw, mod, *after)


def _in_proj(hx, wg, name):
    t, d = hx.shape
    n_seg, _, s = wg.shape
    tm = _mm_rows(t)

    def body(a_ref, w_ref, o_ref):
        o_ref[...] = _dot(a_ref[...], w_ref[...], NN)

    return pl.pallas_call(
        body, name=name, grid=(n_seg, t // tm),
        in_specs=[pl.BlockSpec((tm, d), lambda g, i: (i, 0)), pl.BlockSpec((None, d, s), lambda g, i: (g, 0, 0))],
        out_specs=pl.BlockSpec((None, tm, s), lambda g, i: (g, i, 0)),
        out_shape=jax.ShapeDtypeStruct((n_seg, t, s), F32),
        compiler_params=_params(("parallel", "parallel")))(hx, wg)


def _rope_fwd(v, cos, sa, sb):
    return v * cos + pltpu.roll(v, 96, 1) * sa + pltpu.roll(v, 32, 1) * sb


def _rope_bwd(g, cos, sa, sb):
    return g * cos + pltpu.roll(g * sa, 32, 1) + pltpu.roll(g * sb, 96, 1)


def _rope_qkv(u, cos, sa, sb, n_heads, name):
    _, t, s = u.shape
    k_scale = HEAD_DIM ** -0.5

    def body(q_ref, k_ref, v_ref, cos_ref, sa_ref, sb_ref, qo, ko, vo):
        co, a, b = cos_ref[...], sa_ref[...], sb_ref[...]
        for h in range(n_heads):
            sl = pl.ds(h * HEAD_DIM, HEAD_DIM)
            qo[:, sl] = _rope_fwd(q_ref[:, sl], co, a, b).astype(BF16)
            ko[:, sl] = (_rope_fwd(k_ref[:, sl], co, a, b) * k_scale).astype(BF16)
        vo[...] = v_ref[...].astype(BF16)

    def seg(g):
        return pl.BlockSpec((None, ROW_TILE, s), lambda i: (g, i, 0))

    tab = pl.BlockSpec((ROW_TILE, HEAD_DIM), lambda i: (i, 0))
    row = pl.BlockSpec((ROW_TILE, s), lambda i: (i, 0))
    return pl.pallas_call(body, name=name, grid=(t // ROW_TILE,),
                          in_specs=[seg(4), seg(5), seg(6), tab, tab, tab], out_specs=[row, row, row],
                          out_shape=[jax.ShapeDtypeStruct((t, s), BF16)] * 3,
                          compiler_params=_params(("parallel",)))(u, u, u, cos, sa, sb)


def _f_order(step, nx, ncc):
    return jnp.where(step < ncc, nx + step, step - ncc)


def _state_sweep(kr, vb, tabs, n_heads, nx, ncc, name):
    t, s = kr.shape
    nc = nx + ncc
    c = CHUNK

    def body(kf_ref, vf_ref, kb_ref, vb_ref, kft, kbt, cdf, cdb, sf_out, sb_out, sf, sb):
        @pl.when(pl.program_id(0) == 0)
        def _():
            sf[...] = jnp.zeros_like(sf)
            sb[...] = jnp.zeros_like(sb)

        for h in range(n_heads):
            sl = pl.ds(h * HEAD_DIM, HEAD_DIM)
            sf_out[h] = sf[h].astype(BF16)
            sb_out[h] = sb[h].astype(BF16)
            kd = (kf_ref[:, sl].astype(F32) * kft[h]).astype(BF16)
            sf[h] = cdf[h][0:1, :] * sf[h] + _dot(kd, vf_ref[:, sl], TN)
            kd = (kb_ref[:, sl].astype(F32) * kbt[h]).astype(BF16)
            sb[h] = cdb[h][0:1, :] * sb[h] + _dot(kd, vb_ref[:, sl], TN)

    fwd = pl.BlockSpec((c, s), lambda i: (_f_order(i, nx, ncc), 0))
    bwd = pl.BlockSpec((c, s), lambda i: (nc - 1 - i, 0))
    st = (None, n_heads, HEAD_DIM, HEAD_DIM)
    return pl.pallas_call(
        body, name=name, grid=(nc,),
        in_specs=[fwd, fwd, bwd, bwd, _full((n_heads, c, HEAD_DIM)), _full((n_heads, c, HEAD_DIM)),
                  _full((n_heads, 8, HEAD_DIM)), _full((n_heads, 8, HEAD_DIM))],
        out_specs=[pl.BlockSpec(st, lambda i: (_f_order(i, nx, ncc), 0, 0, 0)),
                   pl.BlockSpec(st, lambda i: (nc - 1 - i, 0, 0, 0))],
        out_shape=[jax.ShapeDtypeStruct((nc, n_heads, HEAD_DIM, HEAD_DIM), BF16)] * 2,
        scratch_shapes=[pltpu.VMEM((n_heads, HEAD_DIM, HEAD_DIM), F32)] * 2,
        compiler_params=_params(("arbitrary",)),
    )(kr, vb, kr, vb, tabs["kf"], tabs["kb"], tabs["cdf"], tabs["cdb"])


def _halo_specs(s, n8):
    per = CHUNK // 8

    def prev(g):
        return pl.BlockSpec((None, 8, s), lambda i: (g, jnp.maximum(i * per - 1, 0), 0))

    def nxt(g):
        return pl.BlockSpec((None, 8, s), lambda i: (g, jnp.minimum((i + 1) * per, n8 - 1), 0))

    return prev, nxt


def _shifted(a, before, after, has_prev, has_next):
    rows = a.shape[0]
    rowi = lax.broadcasted_iota(jnp.int32, a.shape, 0)
    am = jnp.where(rowi == 0, jnp.where(has_prev, before, 0.0), pltpu.roll(a, 1, 0))
    ap = jnp.where(rowi == rows - 1, jnp.where(has_next, after, 0.0), pltpu.roll(a, rows - 1, 0))
    return am, ap


def _neighbours(i, nx, nc):
    return (i != 0) & (i != nx), (i != nx - 1) & (i != nc - 1)


def _mix_fwd(u, qr, kr, vb, sf, sb, tabs, conv_w, cnw, gnw, n_heads, nx, ncc, name):
    _, t, s = u.shape
    nc = nx + ncc
    c = CHUNK

    def body(h_ref, b_ref, c_ref, z_ref, rz_ref, hp_ref, hn_ref, cp_ref, cn_ref, q_ref, k_ref, v_ref,
             sf_ref, sb_ref, dc_ref, qft, qbt, w_ref, cnw_ref, gnw_ref, y_ref, o_ref):
        i = pl.program_id(0)
        has_prev, has_next = _neighbours(i, nx, nc)
        a = c_ref[...] * h_ref[...]
        am, ap = _shifted(a, cp_ref[7:8] * hp_ref[7:8], cn_ref[0:1] * hn_ref[0:1], has_prev, has_next)
        w = w_ref[...]
        y0 = w[0:1] * am + w[1:2] * a + w[2:3] * ap
        yb = b_ref[...] * y0
        r = lax.rsqrt(jnp.mean(yb * yb, axis=-1, keepdims=True) + EPS)
        y_ref[:, pl.ds(0, s)] = (_silu(z_ref[...]) * ((yb * r) * cnw_ref[...])).astype(BF16)
        for h in range(n_heads):
            sl = pl.ds(h * HEAD_DIM, HEAD_DIM)
            q, k, v = q_ref[:, sl], k_ref[:, sl], v_ref[:, sl]
            p = (_dot(q, k, NT) * dc_ref[h]).astype(BF16)
            o = _dot(p, v, NN)
            qf = q.astype(F32)
            o += _dot((qf * qft[h]).astype(BF16), sf_ref[h], NN)
            o += _dot((qf * qbt[h]).astype(BF16), sb_ref[h], NN)
            o_ref[:, sl] = o
            mu = jnp.mean(o, axis=-1, keepdims=True)
            var = jnp.mean(jnp.square(o - mu), axis=-1, keepdims=True)
            on = (o - mu) * lax.rsqrt(var + EPS)
            y_ref[:, pl.ds(s + h * HEAD_DIM, HEAD_DIM)] = (
                _silu(rz_ref[:, sl]) * (on * gnw_ref[:, sl])).astype(BF16)

    def seg(g):
        return pl.BlockSpec((None, c, s), lambda i: (g, i, 0))

    prev, nxt = _halo_specs(s, t // 8)
    row = pl.BlockSpec((c, s), lambda i: (i, 0))
    st = pl.BlockSpec((None, n_heads, HEAD_DIM, HEAD_DIM), lambda i: (i, 0, 0, 0))
    return pl.pallas_call(
        body, name=name, grid=(nc,),
        in_specs=[seg(0), seg(1), seg(2), seg(3), seg(7), prev(0), nxt(0), prev(2), nxt(2), row, row, row,
                  st, st, _full((n_heads, c, c)), _full((n_heads, c, HEAD_DIM)), _full((n_heads, c, HEAD_DIM)),
                  _full((3, s)), _full((1, s)), _full((1, s))],
        out_specs=[pl.BlockSpec((c, 2 * s), lambda i: (i, 0)), row],
        out_shape=[jax.ShapeDtypeStruct((t, 2 * s), BF16), jax.ShapeDtypeStruct((t, s), F32)],
        compiler_params=_params(("parallel",)),
    )(u, u, u, u, u, u, u, u, u, qr, kr, vb, sf, sb, tabs["dc"], tabs["qf"], tabs["qb"], conv_w, cnw, gnw)


def _row_gate(mod_ref, row0, rows, n_lat, col):
    rowi = row0 + lax.broadcasted_iota(jnp.int32, (rows, 1), 0)
    return jnp.where(rowi >= n_lat, mod_ref[5:6, col], mod_ref[2:3, col])


def _out_proj(ycat, w_out, xt, mod, n_lat, name):
    t, d = xt.shape
    tm = _mm_rows(t)
    tn = min(d, 1024)

    def body(a_ref, w_ref, x_ref, mod_ref, m_ref, xo_ref):
        m = _dot(a_ref[...], w_ref[...], NN)
        m_ref[...] = m
        gate = _row_gate(mod_ref, pl.program_id(1) * tm, tm, n_lat, slice(None))
        xo_ref[...] = x_ref[...] + gate * m

    blk = pl.BlockSpec((tm, tn), lambda j, i: (i, j))
    return pl.pallas_call(
        body, name=name, grid=(d // tn, t // tm),
        in_specs=[pl.BlockSpec((tm, d), lambda j, i: (i, 0)), pl.BlockSpec((d, tn), lambda j, i: (0, j)), blk,
                  pl.BlockSpec((8, tn), lambda j, i: (0, j))],
        out_specs=[blk, blk], out_shape=[jax.ShapeDtypeStruct((t, d), F32)] * 2,
        compiler_params=_params(("parallel", "parallel")))(ycat, w_out, xt, mod)


def _loss_head(x2, tgt, fnw, n_lat, name):
    t, d = x2.shape
    nxb = n_lat // ROW_TILE

    def body(x_ref, t_ref, w_ref, dx_ref, loss_ref, dw_ref):
        i = pl.program_id(0)

        @pl.when(i == 0)
        def _():
            loss_ref[...] = jnp.zeros_like(loss_ref)
            dw_ref[...] = jnp.zeros_like(dw_ref)

        @pl.when(i < nxb)
        def _():
            x = x_ref[...]
            w = w_ref[...]
            r = lax.rsqrt(jnp.mean(x * x, axis=-1, keepdims=True) + EPS)
            xn = x * r
            e = xn * w - t_ref[...]
            loss_ref[...] += 0.5 * jnp.sum(jnp.mean(e * e, axis=-1, keepdims=True), axis=0, keepdims=True)
            dy = e * (1.0 / d)
            dw_ref[0:1, :] += jnp.sum(dy * xn, axis=0, keepdims=True)
            dxn = dy * w
            dx_ref[...] = r * (dxn - xn * jnp.mean(dxn * xn, axis=-1, keepdims=True))

        @pl.when(i >= nxb)
        def _():
            dx_ref[...] = jnp.zeros_like(dx_ref)

    row = pl.BlockSpec((ROW_TILE, d), lambda i: (i, 0))
    return pl.pallas_call(
        body, name=name, grid=(t // ROW_TILE,),
        in_specs=[row, pl.BlockSpec((ROW_TILE, d), lambda i: (jnp.minimum(i, nxb - 1), 0)), _full((1, d))],
        out_specs=[row, _full((8, HEAD_DIM)), _full((8, d))],
        out_shape=[jax.ShapeDtypeStruct((t, d), F32), jax.ShapeDtypeStruct((8, HEAD_DIM), F32),
                   jax.ShapeDtypeStruct((8, d), F32)],
        compiler_params=_params(("arbitrary",)))(x2, tgt, fnw)


def _gate_bwd(dxo, m, mod, n_lat, name):
    t, d = dxo.shape
    nxb = n_lat // ROW_TILE

    def body(dx_ref, m_ref, mod_ref, dm_ref, acc_ref):
        i = pl.program_id(0)

        @pl.when(i == 0)
        def _():
            acc_ref[...] = jnp.zeros_like(acc_ref)

        dx = dx_ref[...]
        dg = jnp.sum(dx * m_ref[...], axis=0, keepdims=True)

        @pl.when(i < nxb)
        def _():
            dm_ref[...] = (dx * mod_ref[2:3, :]).astype(BF16)
            acc_ref[2:3, :] += dg

        @pl.when(i >= nxb)
        def _():
            dm_ref[...] = (dx * mod_ref[5:6, :]).astype(BF16)
            acc_ref[5:6, :] += dg

    row = pl.BlockSpec((ROW_TILE, d), lambda i: (i, 0))
    return pl.pallas_call(body, name=name, grid=(t // ROW_TILE,),
                          in_specs=[row, row, _full((8, d))], out_specs=[row, _full((8, d))],
                          out_shape=[jax.ShapeDtypeStruct((t, d), BF16), jax.ShapeDtypeStruct((8, d), F32)],
                          compiler_params=_params(("arbitrary",)))(dxo, m, mod)


def _matmul_nt(a, w, name):
    t, k = a.shape
    n = w.shape[0]
    tm = _mm_rows(t)
    tn = min(n, 1024)

    def body(a_ref, w_ref, o_ref):
        o_ref[...] = _dot(a_ref[...], w_ref[...], NT)

    return pl.pallas_call(
        body, name=name, grid=(n // tn, t // tm),
        in_specs=[pl.BlockSpec((tm, k), lambda j, i: (i, 0)), pl.BlockSpec((tn, k), lambda j, i: (j, 0))],
        out_specs=pl.BlockSpec((tm, tn), lambda j, i: (i, j)),
        out_shape=jax.ShapeDtypeStruct((t, n), F32),
        compiler_params=_params(("parallel", "parallel")))(a, w)


def _weight_grad(a, b, name):
    t, m = a.shape
    n_g, _, n = b.shape
    bt = _mm_rows(t)
    bm = min(m, 1024)
    nt = t // bt

    def body(a_ref, b_ref, o_ref, acc):
        k = pl.program_id(2)

        @pl.when(k == 0)
        def _():
            acc[...] = jnp.zeros_like(acc)

        acc[...] += _dot(a_ref[...], b_ref[...], TN)

        @pl.when(k == nt - 1)
        def _():
            o_ref[...] = acc[...].astype(o_ref.dtype)

    return pl.pallas_call(
        body, name=name, grid=(n_g, m // bm, nt),
        in_specs=[pl.BlockSpec((bt, bm), lambda g, i, k: (k, i)), pl.BlockSpec((None, bt, n), lambda g, i, k: (g, k, 0))],
        out_specs=pl.BlockSpec((None, bm, n), lambda g, i, k: (g, i, 0)),
        out_shape=jax.ShapeDtypeStruct((n_g, m, n), BF16),
        scratch_shapes=[pltpu.VMEM((bm, n), F32)],
        compiler_params=_params(("parallel", "parallel", "arbitrary")))(a, b)


def _mix_bwd_a(dycat, u, o, conv_w, cnw, gnw, n_heads, nx, ncc, name):
    _, t, s = u.shape
    nc = nx + ncc
    c = CHUNK

    def body(dy_ref, h_ref, b_ref, c_ref, z_ref, rz_ref, hp_ref, hn_ref, cp_ref, cn_ref, o_ref, w_ref,
             cnw_ref, gnw_ref, g_ref, dz_ref, db_ref, drz_ref, do_ref, acc_ref):
        i = pl.program_id(0)

        @pl.when(i == 0)
        def _():
            acc_ref[...] = jnp.zeros_like(acc_ref)

        has_prev, has_next = _neighbours(i, nx, nc)
        a = c_ref[...] * h_ref[...]
        am, ap = _shifted(a, cp_ref[7:8] * hp_ref[7:8], cn_ref[0:1] * hn_ref[0:1], has_prev, has_next)
        w = w_ref[...]
        y0 = w[0:1] * am + w[1:2] * a + w[2:3] * ap
        bb = b_ref[...]
        yb = bb * y0
        r = lax.rsqrt(jnp.mean(yb * yb, axis=-1, keepdims=True) + EPS)
        ynn = yb * r
        z = z_ref[...]
        dyc = dy_ref[:, pl.ds(0, s)]
        cw = cnw_ref[...]
        dz_ref[...] = (dyc * (ynn * cw) * _dsilu(z)).astype(BF16)
        dyn = dyc * _silu(z)
        acc_ref[0:1, :] += jnp.sum(dyn * ynn, axis=0, keepdims=True)
        dynn = dyn * cw
        dyb = r * (dynn - ynn * jnp.mean(dynn * ynn, axis=-1, keepdims=True))
        db_ref[...] = (dyb * y0).astype(BF16)
        g_ref[...] = dyb * bb
        for h in range(n_heads):
            sl = pl.ds(h * HEAD_DIM, HEAD_DIM)
            ov = o_ref[:, sl]
            mu = jnp.mean(ov, axis=-1, keepdims=True)
            var = jnp.mean(jnp.square(ov - mu), axis=-1, keepdims=True)
            rs = lax.rsqrt(var + EPS)
            on = (ov - mu) * rs
            dyr = dy_ref[:, pl.ds(s + h * HEAD_DIM, HEAD_DIM)]
            rz = rz_ref[:, sl]
            gw = gnw_ref[:, sl]
            drz_ref[:, sl] = (dyr * (on * gw) * _dsilu(rz)).astype(BF16)
            dyg = dyr * _silu(rz)
            acc_ref[1:2, sl] += jnp.sum(dyg * on, axis=0, keepdims=True)
            don = dyg * gw
            do = rs * (don - jnp.mean(don, axis=-1, keepdims=True)
                       - on * jnp.mean(don * on, axis=-1, keepdims=True))
            do_ref[:, sl] = do.astype(BF16)

    def seg(g):
        return pl.BlockSpec((None, c, s), lambda i: (g, i, 0))

    prev, nxt = _halo_specs(s, t // 8)
    row = pl.BlockSpec((c, s), lambda i: (i, 0))
    return pl.pallas_call(
        body, name=name, grid=(nc,),
        in_specs=[pl.BlockSpec((c, 2 * s), lambda i: (i, 0)), seg(0), seg(1), seg(2), seg(3), seg(7),
                  prev(0), nxt(0), prev(2), nxt(2), row, _full((3, s)), _full((1, s)), _full((1, s))],
        out_specs=[row, row, row, row, row, _full((8, s))],
        out_shape=[jax.ShapeDtypeStruct((t, s), F32)] + [jax.ShapeDtypeStruct((t, s), BF16)] * 4
        + [jax.ShapeDtypeStruct((8, s), F32)],
        compiler_params=_params(("arbitrary",)),
    )(dycat, u, u, u, u, u, u, u, u, u, o, conv_w, cnw, gnw)


def _grad_state_sweep(qr, do, tabs, n_heads, nx, ncc, name):
    t, s = qr.shape
    nc = nx + ncc
    c = CHUNK

    def body(qf_ref, df_ref, qb_ref, db_ref, qft, qbt, cdf, cdb, gf_out, gb_out, gf, gb):
        @pl.when(pl.program_id(0) == 0)
        def _():
            gf[...] = jnp.zeros_like(gf)
            gb[...] = jnp.zeros_like(gb)

        for h in range(n_heads):
            sl = pl.ds(h * HEAD_DIM, HEAD_DIM)
            gf_out[h] = gf[h].astype(BF16)
            gb_out[h] = gb[h].astype(BF16)
            qd = (qf_ref[:, sl].astype(F32) * qft[h]).astype(BF16)
            gf[h] = cdf[h][0:1, :] * gf[h] + _dot(qd, df_ref[:, sl], TN)
            qd = (qb_ref[:, sl].astype(F32) * qbt[h]).astype(BF16)
            gb[h] = cdb[h][0:1, :] * gb[h] + _dot(qd, db_ref[:, sl], TN)

    fwd = pl.BlockSpec((c, s), lambda i: (_f_order(nc - 1 - i, nx, ncc), 0))
    bwd = pl.BlockSpec((c, s), lambda i: (i, 0))
    st = (None, n_heads, HEAD_DIM, HEAD_DIM)
    return pl.pallas_call(
        body, name=name, grid=(nc,),
        in_specs=[fwd, fwd, bwd, bwd, _full((n_heads, c, HEAD_DIM)), _full((n_heads, c, HEAD_DIM)),
                  _full((n_heads, 8, HEAD_DIM)), _full((n_heads, 8, HEAD_DIM))],
        out_specs=[pl.BlockSpec(st, lambda i: (_f_order(nc - 1 - i, nx, ncc), 0, 0, 0)),
                   pl.BlockSpec(st, lambda i: (i, 0, 0, 0))],
        out_shape=[jax.ShapeDtypeStruct((nc, n_heads, HEAD_DIM, HEAD_DIM), BF16)] * 2,
        scratch_shapes=[pltpu.VMEM((n_heads, HEAD_DIM, HEAD_DIM), F32)] * 2,
        compiler_params=_params(("arbitrary",)),
    )(qr, do, qr, do, tabs["qf"], tabs["qb"], tabs["cdf"], tabs["cdb"])


def _mix_bwd_b(u, g, dz, db, drz, qr, kr, vb, do, sf, sb, gf, gb, tabs, cos, sa, sb_tab, conv_w,
               n_heads, nx, ncc, name):
    _, t, s = u.shape
    nc = nx + ncc
    c = CHUNK
    k_scale = HEAD_DIM ** -0.5

    def body(h_ref, c_ref, g_ref, gp_ref, gn_ref, dz_ref, db_ref, drz_ref, q_ref, k_ref, v_ref, do_ref,
             sf_ref, sb_ref, gf_ref, gb_ref, dc_t, dlf_t, dlb_t, qft, kft, qbt, kbt, cdf, cdb, lg_ref,
             cos_ref, sa_ref, sb_ref2, w_ref, du_ref, dw_ref, dlg_ref):
        i = pl.program_id(0)

        @pl.when(i == 0)
        def _():
            dw_ref[...] = jnp.zeros_like(dw_ref)
            dlg_ref[...] = jnp.zeros_like(dlg_ref)

        has_prev, has_next = _neighbours(i, nx, nc)
        gv = g_ref[...]
        gm, gp = _shifted(gv, gp_ref[7:8], gn_ref[0:1], has_prev, has_next)
        w = w_ref[...]
        da = w[0:1] * gp + w[1:2] * gv + w[2:3] * gm
        hh, cc = h_ref[...], c_ref[...]
        du_ref[0] = (da * cc).astype(BF16)
        du_ref[2] = (da * hh).astype(BF16)
        a = cc * hh
        dw_ref[0:1, :] += jnp.sum(a * gp, axis=0, keepdims=True)
        dw_ref[1:2, :] += jnp.sum(a * gv, axis=0, keepdims=True)
        dw_ref[2:3, :] += jnp.sum(a * gm, axis=0, keepdims=True)
        du_ref[1] = db_ref[...]
        du_ref[3] = dz_ref[...]
        du_ref[7] = drz_ref[...]

        co, ra, rb = cos_ref[...], sa_ref[...], sb_ref2[...]
        pos = lax.broadcasted_iota(jnp.int32, (c, HEAD_DIM), 0).astype(F32)
        row8 = lax.broadcasted_iota(jnp.int32, (8, HEAD_DIM), 0)
        lane8 = lax.broadcasted_iota(jnp.int32, (8, HEAD_DIM), 1)
        dlg = jnp.zeros((8, HEAD_DIM), F32)
        for h in range(n_heads):
            sl = pl.ds(h * HEAD_DIM, HEAD_DIM)
            q, k, v, do = q_ref[:, sl], k_ref[:, sl], v_ref[:, sl], do_ref[:, sl]
            qf, kf, dof = q.astype(F32), k.astype(F32), do.astype(F32)
            s_f, s_b, g_f, g_b = sf_ref[h], sb_ref[h], gf_ref[h], gb_ref[h]
            p = _dot(q, k, NT)
            pd = _dot(do, v, NT)
            pdd = (pd * dc_t[h]).astype(BF16)
            dq = _dot(pdd, k, NN)
            dk = _dot(pdd, q, TN)
            dv = _dot((p * dc_t[h]).astype(BF16), do, TN)
            dq_f = _dot((dof * qft[h]).astype(BF16), s_f, NT)
            dq_b = _dot((dof * qbt[h]).astype(BF16), s_b, NT)
            dk_f = _dot(v, g_f, NT) * kft[h]
            dk_b = _dot(v, g_b, NT) * kbt[h]
            dv += _dot((kf * kft[h]).astype(BF16), g_f, NN) + _dot((kf * kbt[h]).astype(BF16), g_b, NN)
            ppd = p * pd
            cd_f, cd_b = cdf[h][0:1, :], cdb[h][0:1, :]
            t_f = _sum_all(dlf_t[h] * ppd + (pos + 1.0) * qf * dq_f + (c - 1.0 - pos) * kf * dk_f
                           + float(c) * (cd_f * (g_f.astype(F32) * s_f.astype(F32))))
            t_b = _sum_all(dlb_t[h] * ppd + (c - pos) * qf * dq_b + pos * kf * dk_b
                           + float(c) * (cd_b * (g_b.astype(F32) * s_b.astype(F32))))
            dlg += jnp.where((row8 == 0) & (lane8 == h), t_f, 0.0) + jnp.where((row8 == 1) & (lane8 == h), t_b, 0.0)
            du_ref[4, :, sl] = _rope_bwd(dq + dq_f + dq_b, co, ra, rb).astype(BF16)
            du_ref[5, :, sl] = (_rope_bwd(dk + dk_f + dk_b, co, ra, rb) * k_scale).astype(BF16)
            du_ref[6, :, sl] = dv.astype(BF16)
        dlg_ref[...] += dlg

        @pl.when(i == nc - 1)
        def _():
            dlg_ref[...] = dlg_ref[...] * lg_ref[...]

    def seg(gi):
        return pl.BlockSpec((None, c, s), lambda i: (gi, i, 0))

    per = c // 8
    n8 = t // 8
    row = pl.BlockSpec((c, s), lambda i: (i, 0))
    st = pl.BlockSpec((None, n_heads, HEAD_DIM, HEAD_DIM), lambda i: (i, 0, 0, 0))
    tab = pl.BlockSpec((c, HEAD_DIM), lambda i: (i, 0))
    hc = _full((n_heads, c, HEAD_DIM))
    cc_ = _full((n_heads, c, c))
    h8 = _full((n_heads, 8, HEAD_DIM))
    return pl.pallas_call(
        body, name=name, grid=(nc,),
        in_specs=[seg(0), seg(2), row,
                  pl.BlockSpec((8, s), lambda i: (jnp.maximum(i * per - 1, 0), 0)),
                  pl.BlockSpec((8, s), lambda i: (jnp.minimum((i + 1) * per, n8 - 1), 0)),
                  row, row, row, row, row, row, row, st, st, st, st, cc_, cc_, cc_, hc, hc, hc, hc, h8, h8,
                  _full((8, HEAD_DIM)), tab, tab, tab, _full((3, s))],
        out_specs=[pl.BlockSpec((8, c, s), lambda i: (0, i, 0)), _full((8, s)), _full((8, HEAD_DIM))],
        out_shape=[jax.ShapeDtypeStruct((8, t, s), BF16), jax.ShapeDtypeStruct((8, s), F32),
                   jax.ShapeDtypeStruct((8, HEAD_DIM), F32)],
        compiler_params=_params(("arbitrary",)),
    )(u, u, g, g, g, dz, db, drz, qr, kr, vb, do, sf, sb, gf, gb, tabs["dc"], tabs["dlf"], tabs["dlb"],
      tabs["qf"], tabs["kf"], tabs["qb"], tabs["kb"], tabs["cdf"], tabs["cdb"], tabs["lg"], cos, sa, sb_tab, conv_w)


def _in_proj_bwd(du, wg, name, after=()):
    n_seg, t, s = du.shape
    d = wg.shape[1]
    tm = _mm_rows(t)

    def body(a_ref, w_ref, *rest):
        o_ref = rest[-1]
        g = pl.program_id(1)
        part = _dot(a_ref[...], w_ref[...], NT)

        @pl.when(g == 0)
        def _():
            o_ref[...] = part

        @pl.when(g > 0)
        def _():
            o_ref[...] += part

    return pl.pallas_call(
        body, name=name, grid=(t // tm, n_seg),
        in_specs=[pl.BlockSpec((None, tm, s), lambda i, g: (g, i, 0)), pl.BlockSpec((None, d, s), lambda i, g: (g, 0, 0))]
        + [pl.BlockSpec(memory_space=pl.ANY)] * len(after),
        out_specs=pl.BlockSpec((tm, d), lambda i, g: (i, 0)),
        out_shape=jax.ShapeDtypeStruct((t, d), F32),
        compiler_params=_params(("parallel", "arbitrary")))(du, wg, *after)


def _prenorm_bwd(dhx, xt, dxo, nw, mod, n_lat, name):
    t, d = xt.shape
    nxb = n_lat // ROW_TILE

    def body(dh_ref, x_ref, dxo_ref, nw_ref, mod_ref, dx_ref, acc_ref):
        i = pl.program_id(0)

        @pl.when(i == 0)
        def _():
            acc_ref[...] = jnp.zeros_like(acc_ref)

        ctx = i >= nxb
        m = mod_ref[...]
        scale1 = 1.0 + jnp.where(ctx, m[4:5], m[1:2])
        x = x_ref[...]
        nw_v = nw_ref[...]
        r = lax.rsqrt(jnp.mean(x * x, axis=-1, keepdims=True) + EPS)
        xn = x * r
        dh = dh_ref[...]
        dshift = jnp.sum(dh, axis=0, keepdims=True)
        dscale = jnp.sum(dh * (xn * nw_v), axis=0, keepdims=True)
        acc_ref[6:7, :] += jnp.sum(dh * scale1 * xn, axis=0, keepdims=True)

        @pl.when(i < nxb)
        def _():
            acc_ref[0:1, :] += dshift
            acc_ref[1:2, :] += dscale

        @pl.when(i >= nxb)
        def _():
            acc_ref[3:4, :] += dshift
            acc_ref[4:5, :] += dscale

        dxn = dh * (nw_v * scale1)
        dx_ref[...] = dxo_ref[...] + r * (dxn - xn * jnp.mean(dxn * xn, axis=-1, keepdims=True))

    row = pl.BlockSpec((ROW_TILE, d), lambda i: (i, 0))
    return pl.pallas_call(body, name=name, grid=(t // ROW_TILE,),
                          in_specs=[row, row, row, _full((1, d)), _full((8, d))], out_specs=[row, _full((8, d))],
                          out_shape=[jax.ShapeDtypeStruct((t, d), F32), jax.ShapeDtypeStruct((8, d), F32)],
                          compiler_params=_params(("arbitrary",)))(dhx, xt, dxo, nw, mod)


def _adamw(g, w, m, v):
    m = ADAM_B1 * m + (1.0 - ADAM_B1) * g
    v = ADAM_B2 * v + (1.0 - ADAM_B2) * jnp.square(g)
    m_hat = m / (1.0 - ADAM_B1 ** ADAM_STEP)
    v_hat = v / (1.0 - ADAM_B2 ** ADAM_STEP)
    delta = -ADAM_LR * (m_hat / (jnp.sqrt(v_hat) + ADAM_EPS) + ADAM_WD * w)
    return delta, m, v


def _sum_adamw(parts, w, m, v, name, row0=0, into=None):
    n_p, r, n = parts.shape
    r_all = w.shape[0]
    part_block_bytes = 4 * 1024 * 1024
    br = 8
    for cand in (512, 256, 128, 64, 32, 16):
        if r % cand == 0 and row0 % cand == 0 and n_p * cand * n * parts.dtype.itemsize <= part_block_bytes:
            br = cand
            break
    blk0 = row0 // br

    def body(p_ref, w_ref, m_ref, v_ref, *rest):
        g_out, d_out, m_out, v_out = rest[-4:]
        g = p_ref[0].astype(F32)
        for j in range(1, n_p):
            g = g + p_ref[j].astype(F32)
        g_out[...] = g
        d_out[...], m_out[...], v_out[...] = _adamw(g, w_ref[...], m_ref[...], v_ref[...])

    row = pl.BlockSpec((br, n), lambda i: (i + blk0, 0))
    kept = [] if into is None else list(into)
    return pl.pallas_call(body, name=name, grid=(r // br,),
                          in_specs=[pl.BlockSpec((n_p, br, n), lambda i: (0, i, 0)), row, row, row]
                          + [pl.BlockSpec(memory_space=pl.ANY)] * len(kept),
                          out_specs=[row] * 4, out_shape=[jax.ShapeDtypeStruct((r_all, n), F32)] * 4,
                          input_output_aliases={4 + j: j for j in range(len(kept))},
                          compiler_params=_params(("parallel",)))(parts, w, m, v, *kept)


def _rope_tables(n_lat, n_ctx):
    f = HEAD_DIM // 4
    rows = n_lat // GRID_W
    row_pos = jnp.repeat(jnp.arange(rows), GRID_W).astype(F32)
    col_pos = jnp.tile(jnp.arange(GRID_W), rows).astype(F32)
    inv = ROPE_BASE ** (-jnp.arange(f, dtype=F32) / f)
    ang_r = row_pos[:, None] * inv[None, :]
    ang_c = col_pos[:, None] * inv[None, :]
    zero = jnp.zeros_like(ang_r)
    cos = jnp.concatenate([jnp.cos(ang_r), jnp.cos(ang_r), jnp.cos(ang_c), jnp.cos(ang_c)], axis=-1)
    sa = jnp.concatenate([-jnp.sin(ang_r), zero, -jnp.sin(ang_c), zero], axis=-1)
    sb = jnp.concatenate([zero, jnp.sin(ang_r), zero, jnp.sin(ang_c)], axis=-1)
    pad = jnp.zeros((n_ctx, HEAD_DIM), F32)
    return (jnp.concatenate([cos, pad + 1.0], axis=0), jnp.concatenate([sa, pad], axis=0),
            jnp.concatenate([sb, pad], axis=0))


def _pad_rows(a, rows):
    return jnp.pad(a, [(0, rows - a.shape[0])] + [(0, 0)] * (a.ndim - 1))


def _pad_cols(a, cols):
    return jnp.pad(a, [(0, 0), (0, cols - a.shape[1])])


def kernel(x, c, ctx, c_ctx, norm_w, w_mod, b_mod, w_in, conv_w, conv_norm_w, ret_norm_w, ret_decay_f, ret_decay_b, w_out, final_norm_w, loss_target, m_c_ctx, m_norm_w, m_w_mod, m_b_mod, m_w_in, m_conv_w, m_conv_norm_w, m_ret_norm_w, m_ret_decay_f, m_ret_decay_b, m_w_out, m_final_norm_w, v_c_ctx, v_norm_w, v_w_mod, v_b_mod, v_w_in, v_conv_w, v_conv_norm_w, v_ret_norm_w, v_ret_decay_f, v_ret_decay_b, v_w_out, v_final_norm_w):
    depth = norm_w.shape[0]
    n_lat, d = x.shape[1], x.shape[2]
    n_ctx = ctx.shape[1]
    s = d // 2
    n_heads = ret_decay_f.shape[1]
    nx, ncc = n_lat // CHUNK, n_ctx // CHUNK
    n_mod = w_mod.shape[2]
    n_cw = conv_w.shape[2]
    r_out = w_out.shape[1]
    assert s == n_heads * HEAD_DIM and w_in.shape[2] == s and N_DEV * r_out == d
    assert n_lat % ROW_TILE == 0 and n_ctx % ROW_TILE == 0 and 3 * depth * n_cw <= d and d >= 3 * n_mod // 3
    me = 4 * lax.axis_index("x") + 2 * lax.axis_index("y") + lax.axis_index("c")

    first = jnp.concatenate([c.reshape(1, d), _pad_cols(conv_w.reshape(1, -1), d), jnp.zeros((6, d), F32)], axis=0)
    (first_g,) = _all_gather([first], "gather_cond", True)
    first_g = first_g.reshape(N_DEV, 8, d)
    c_all = first_g[:, 0, :]
    conv_full = first_g[:, 1, :3 * depth * n_cw].reshape(N_DEV, depth, 3, n_cw)
    conv_full = conv_full.transpose(1, 2, 0, 3).reshape(depth, 3, N_DEV * n_cw)
    c9 = jnp.concatenate([c_all, c_ctx.reshape(1, d), jnp.zeros((7, d), F32)], axis=0)

    b_sh = lax.dynamic_slice(b_mod, (0, me * n_mod), (depth, n_mod))
    mod_sh = jnp.concatenate([_mod_rows(c9, w_mod[l], b_sh[l:l + 1], f"mod_rows_l{l}") for l in range(depth)], axis=0)
    (mod_g,) = _all_gather([mod_sh], "gather_mod", True)
    mod_g = mod_g.reshape(N_DEV, depth, 16, n_mod)
    mods = []
    for l in range(depth):
        mine = lax.dynamic_index_in_dim(mod_g[:, l], me, axis=1, keepdims=False).reshape(3, d)
        cx = mod_g[:, l, 8, :].reshape(3, d)
        mods.append(jnp.concatenate([mine, cx, jnp.zeros((2, d), F32)], axis=0))

    w_in_bf = [w_in[l].astype(BF16) for l in range(depth)]
    w_out_bf = [w_out[l].astype(BF16) for l in range(depth)]
    (w_in0,) = _all_gather([w_in_bf[0]], "gather_w_in0", False)
    w_in_g = [w_in0.reshape(N_DEV, d, s)] + [None] * (depth - 1)
    w_out_g = [None] * depth
    pending, tokens = [], []
    for l in range(depth):
        srcs = [w_out_bf[l]] + ([w_in_bf[l]] if l > 0 else [])
        lands = [_landing(a, me) for a in srcs]
        started = _push_start(srcs, lands, False, f"weights_start_l{l}", after=[w_in0] if l == 0 else tokens[-1:])
        pending.append(started[:4])
        tokens.append(started[4])

    cos, sa, sb_tab = _rope_tables(n_lat, n_ctx)
    xt = jnp.concatenate([x[0], ctx[0]], axis=0)

    saved = []
    for l in range(depth):
        names = ["dc", "dlf", "dlb", "qf", "kf", "qb", "kb", "cdf", "cdb", "lg"]
        dec = jnp.stack([ret_decay_f[l], ret_decay_b[l]], axis=0)
        tabs = dict(zip(names, _decay_tables(dec, n_heads, f"decay_tables_l{l}")))
        if l > 0:
            landed = _push_wait(*pending[l], False, xt, f"weights_wait_l{l}")
            w_out_g[l], w_in_g[l] = landed[0].reshape(d, d), landed[1]
        hx = _prenorm(xt, norm_w[l:l + 1], mods[l], n_lat, f"prenorm_l{l}", after=tokens if l == 0 else ())
        u = _in_proj(hx, w_in_g[l], f"in_proj_l{l}")
        qr, kr, vb = _rope_qkv(u, cos, sa, sb_tab, n_heads, f"rope_l{l}")
        sf, sb = _state_sweep(kr, vb, tabs, n_heads, nx, ncc, f"state_sweep_l{l}")
        ycat, o = _mix_fwd(u, qr, kr, vb, sf, sb, tabs, conv_full[l], conv_norm_w[l:l + 1], ret_norm_w[l:l + 1],
                           n_heads, nx, ncc, f"mix_fwd_l{l}")
        if l == 0:
            (landed,) = _push_wait(*pending[0], False, ycat, "weights_wait_l0")
            w_out_g[0] = landed.reshape(d, d)
        m_res, x_new = _out_proj(ycat, w_out_g[l], xt, mods[l], n_lat, f"out_proj_l{l}")
        saved.append(dict(tabs=tabs, xt=xt, hx=hx, u=u, qr=qr, kr=kr, vb=vb, sf=sf, sb=sb, ycat=ycat, o=o, m=m_res))
        xt = x_new

    dxt, loss_blk, dfnw = _loss_head(xt, loss_target[0], final_norm_w.reshape(1, d), n_lat, "loss_head")
    loss = lax.psum(loss_blk[0, 0], MESH_AXES)

    dmod_x, dmod_c, dnw, dcnw, dgnw, dconv, ddec, dwin, dwout = [], [], [], [], [], [], [], [], []
    for l in reversed(range(depth)):
        sv = saved[l]
        dm, gate_acc = _gate_bwd(dxt, sv["m"], mods[l], n_lat, f"gate_bwd_l{l}")
        dycat = _matmul_nt(dm, w_out_g[l], f"out_proj_bwd_l{l}")
        dwout.append(_weight_grad(sv["ycat"], dm.reshape(1, *dm.shape), f"w_out_grad_l{l}")[0])
        g, dz, db, drz, do, norm_acc = _mix_bwd_a(dycat, sv["u"], sv["o"], conv_full[l], conv_norm_w[l:l + 1],
                                                   ret_norm_w[l:l + 1], n_heads, nx, ncc, f"mix_bwd_a_l{l}")
        gf, gb = _grad_state_sweep(sv["qr"], do, sv["tabs"], n_heads, nx, ncc, f"grad_state_sweep_l{l}")
        du, conv_acc, dlg = _mix_bwd_b(sv["u"], g, dz, db, drz, sv["qr"], sv["kr"], sv["vb"], do, sv["sf"], sv["sb"],
                                       gf, gb, sv["tabs"], cos, sa, sb_tab, conv_full[l], n_heads, nx, ncc,
                                       f"mix_bwd_b_l{l}")
        srcs = [_weight_grad(sv["hx"], du, f"w_in_grad_l{l}"), dwout[-1].reshape(N_DEV, r_out, d)]
        lands = [_landing(lax.dynamic_index_in_dim(a, me, axis=0, keepdims=False), me) for a in srcs]
        started = _push_start(srcs, lands, True, f"grads_start_l{l}")
        dwin.append(started[:4])
        dhx = _in_proj_bwd(du, w_in_g[l], f"in_proj_bwd_l{l}", after=started[4:])
        dxt, pre_acc = _prenorm_bwd(dhx, sv["xt"], dxt, norm_w[l:l + 1], mods[l], n_lat, f"prenorm_bwd_l{l}")
        dmod_x.append(jnp.concatenate([pre_acc[0], pre_acc[1], gate_acc[2]]))
        dmod_c.append(jnp.concatenate([pre_acc[3], pre_acc[4], gate_acc[5]]))
        dnw.append(pre_acc[6])
        dcnw.append(norm_acc[0])
        dgnw.append(norm_acc[1])
        dconv.append(conv_acc[0:3])
        ddec.append(dlg[0:2, :n_heads])
    for lst in (dmod_x, dmod_c, dnw, dcnw, dgnw, dconv, ddec, dwin, dwout):
        lst.reverse()
    grad_x = dxt[:n_lat].reshape(1, n_lat, d)

    rows = []
    for l in range(depth):
        rows += [dmod_x[l], dmod_c[l]]
    (dmod_g,) = _all_gather([_pad_rows(jnp.stack(rows, axis=0), 8)], "gather_dmod", True)
    dmod_g = dmod_g.reshape(N_DEV, 8, 3 * d)
    mine_cols = lax.dynamic_slice(dmod_g, (0, 0, me * n_mod), (N_DEV, 8, n_mod))
    g_wmod, dcc = [], jnp.zeros((d,), F32)
    for l in range(depth):
        gw, dc_part = _mod_grads(mine_cols[:, 2 * l], mine_cols[:, 2 * l + 1], c9, w_mod[l], f"mod_grads_l{l}")
        g_wmod.append(gw)
        dcc = dcc + dc_part[0]

    n_small = 16
    small = jnp.concatenate([
        jnp.stack(dnw, axis=0),
        jnp.concatenate(dcnw).reshape(1, -1),
        jnp.concatenate(dgnw).reshape(1, -1),
        dfnw[0:1],
        dcc.reshape(1, d),
        jnp.stack(dconv, axis=0).reshape(-1, d),
        _pad_cols(jnp.stack(ddec, axis=0).reshape(1, -1), d),
    ], axis=0)
    assert depth * s == d and small.shape[0] <= n_small
    n_rows = small.shape[0]
    (small_g,) = _all_gather([_pad_rows(small, n_small)], "gather_small", True)
    small_g = small_g.reshape(N_DEV, n_small, d)

    def pack_small(nw_, cn_, gn_, fn_, cc_, df_, db_):
        return _pad_rows(jnp.concatenate([
            nw_, cn_.reshape(1, -1), gn_.reshape(1, -1), fn_.reshape(1, d), cc_.reshape(1, d),
            jnp.zeros((n_rows - depth - 5, d), F32),
            _pad_cols(jnp.stack([df_, db_], axis=1).reshape(1, -1), d)], axis=0), n_small)

    w_s = pack_small(norm_w, conv_norm_w, ret_norm_w, final_norm_w, c_ctx, ret_decay_f, ret_decay_b)
    m_s = pack_small(m_norm_w, m_conv_norm_w, m_ret_norm_w, m_final_norm_w, m_c_ctx, m_ret_decay_f, m_ret_decay_b)
    v_s = pack_small(v_norm_w, v_conv_norm_w, v_ret_norm_w, v_final_norm_w, v_c_ctx, v_ret_decay_f, v_ret_decay_b)
    small_out = _sum_adamw(small_g, w_s, m_s, v_s, "adamw_small")

    def unpack_small(a):
        nw_ = a[0:depth]
        cn_ = a[depth].reshape(depth, s)
        gn_ = a[depth + 1].reshape(depth, s)
        fn_ = a[depth + 2]
        cc_ = a[depth + 3]
        dd = a[n_rows - 1, :depth * 2 * n_heads].reshape(depth, 2, n_heads)
        return dict(c_ctx=cc_, norm_w=nw_, conv_norm_w=cn_, ret_norm_w=gn_, ret_decay_f=dd[:, 0], ret_decay_b=dd[:, 1],
                    final_norm_w=fn_)

    res = {}
    for kind, arr in zip(("grad", "delta", "m", "v"), small_out):
        for k_, val in unpack_small(arr).items():
            res[(kind, k_)] = val

    bm_parts = jnp.concatenate([dmod_g[:, 0:2 * depth:2].reshape(N_DEV, depth, 3 * d),
                                dmod_g[:, 1:2 * depth:2].reshape(N_DEV, depth, 3 * d)], axis=0)
    bm_parts = jnp.concatenate([bm_parts, jnp.zeros((2 * N_DEV, 8 - depth, 3 * d), F32)], axis=1)
    pad8 = lambda a: _pad_rows(a, 8)
    bm_out = _sum_adamw(bm_parts, pad8(b_mod), pad8(m_b_mod), pad8(v_b_mod), "adamw_b_mod")
    for kind, arr in zip(("grad", "delta", "m", "v"), bm_out):
        res[(kind, "b_mod")] = arr[:depth]

    conv_rows = small_g[:, depth + 4:depth + 4 + 3 * depth * s // d].reshape(N_DEV, depth * 3, s)
    conv_mine = lax.dynamic_slice(conv_rows, (0, 0, me * n_cw), (N_DEV, depth * 3, n_cw))
    conv_mine = jnp.concatenate([conv_mine, jnp.zeros((N_DEV, 8 - depth * 3, n_cw), F32)], axis=1)
    cw2 = lambda a: _pad_rows(a.reshape(depth * 3, n_cw), 8)
    cw_out = _sum_adamw(conv_mine, cw2(conv_w), cw2(m_conv_w), cw2(v_conv_w), "adamw_conv_w")
    for kind, arr in zip(("grad", "delta", "m", "v"), cw_out):
        res[(kind, "conv_w")] = arr[:depth * 3].reshape(depth, 3, n_cw)

    wm_out = _sum_adamw(jnp.stack(g_wmod, axis=0).reshape(1, depth * d, n_mod), w_mod.reshape(depth * d, n_mod),
                        m_w_mod.reshape(depth * d, n_mod), v_w_mod.reshape(depth * d, n_mod), "adamw_w_mod")
    for kind, arr in zip(("grad", "delta", "m", "v"), wm_out):
        res[(kind, "w_mod")] = arr.reshape(depth, d, n_mod)

    wi_out = wo_out = None
    after = wm_out[0]
    for l in reversed(range(depth)):
        win_parts, wout_parts = _push_wait(*dwin[l], True, after, f"grads_wait_l{l}")
        wi_out = _sum_adamw(win_parts, w_in.reshape(depth * d, s), m_w_in.reshape(depth * d, s),
                            v_w_in.reshape(depth * d, s), f"adamw_w_in_l{l}", row0=l * d, into=wi_out)
        wo_out = _sum_adamw(wout_parts, w_out.reshape(depth * r_out, d), m_w_out.reshape(depth * r_out, d),
                            v_w_out.reshape(depth * r_out, d), f"adamw_w_out_l{l}", row0=l * r_out, into=wo_out)
        after = wo_out[0]
    for kind, arr in zip(("grad", "delta", "m", "v"), wi_out):
        res[(kind, "w_in")] = arr.reshape(depth, d, s)
    for kind, arr in zip(("grad", "delta", "m", "v"), wo_out):
        res[(kind, "w_out")] = arr.reshape(depth, r_out, d)

    order = ["c_ctx", "norm_w", "w_mod", "b_mod", "w_in", "conv_w", "conv_norm_w", "ret_norm_w", "ret_decay_f",
             "ret_decay_b", "w_out", "final_norm_w"]
    outs = [loss, grad_x]
    for kind in ("grad", "delta", "m", "v"):
        outs += [res[(kind, k_)] for k_ in order]
    return tuple(outs)
```

```python
import functools

import jax
import jax.numpy as jnp
from jax import lax
from jax.experimental import pallas as pl
from jax.experimental.pallas import tpu as pltpu

F32 = jnp.float32
BF16 = jnp.bfloat16

EPS = 1e-6
CHUNK = 128
HEAD_DIM = 128
GRID_W = 64
ROPE_BASE = 10000.0
N_DEV = 8
ADAM_LR, ADAM_B1, ADAM_B2, ADAM_EPS, ADAM_WD, ADAM_STEP = 0.001, 0.9, 0.999, 1e-08, 0.01, 10

ROW_TILE = 256
V7X_VMEM_LIMIT = 56 * 1024 * 1024
MESH_AXES = ("x", "y", "c")

NN = ((1,), (0,))
NT = ((1,), (1,))
TN = ((0,), (0,))


def _dot(a, b, dims):
    return lax.dot_general(a, b, (dims, ((), ())), preferred_element_type=F32)


def _params(sem=None):
    if sem is None:
        return pltpu.CompilerParams(vmem_limit_bytes=V7X_VMEM_LIMIT)
    return pltpu.CompilerParams(dimension_semantics=sem, vmem_limit_bytes=V7X_VMEM_LIMIT)


def _silu(z):
    return z * jax.nn.sigmoid(z)


def _dsilu(z):
    s = jax.nn.sigmoid(z)
    return s * (1.0 + z * (1.0 - s))


def _sum_all(a):
    return jnp.sum(jnp.sum(a, axis=1, keepdims=True), axis=0, keepdims=True)


def _mm_rows(t):
    return 768 if t % 768 == 0 else ROW_TILE


def _rows_or(t, rows):
    return rows if t % rows == 0 else _mm_rows(t)


def _tiles(layer, t, d):
    if layer == 0:
        return dict(in_tm=_mm_rows(t), bwd_tm=_mm_rows(t), wg_bm=min(d, 1024), out_tm=_mm_rows(t), out_tn=min(d, 1024))
    return dict(in_tm=_rows_or(t, 1408), bwd_tm=_rows_or(t, 1056), wg_bm=d, out_tm=ROW_TILE, out_tn=d)


def _full(shape):
    n = len(shape)
    return pl.BlockSpec(shape, lambda *_: (0,) * n)


def _peers(x, y, c):
    return [(x, y, 1 - c), (1 - x, y, c), (x, 1 - y, c), (1 - x, 1 - y, c),
            (1 - x, y, 1 - c), (x, 1 - y, 1 - c), (1 - x, 1 - y, 1 - c)]


def _lin(p):
    return 4 * p[0] + 2 * p[1] + p[2]


def _all_gather(arrays, name, in_vmem):
    n_arr = len(arrays)
    space = pltpu.VMEM if in_vmem else pl.ANY

    def body(*refs):
        ins, outs = refs[:n_arr], refs[n_arr:2 * n_arr]
        send_sems, recv_sems, local_sems = refs[2 * n_arr:]
        x, y, c = lax.axis_index("x"), lax.axis_index("y"), lax.axis_index("c")
        me, sibling = (x, y, c), (x, y, 1 - c)
        chips = [(1 - x, y), (x, 1 - y), (1 - x, 1 - y)]
        every = []
        locals_ = []
        for a in range(n_arr):
            m_per = ins[a].shape[0]
            out_ref = outs[a]

            def rows(p, out_ref=out_ref, m_per=m_per):
                return out_ref.at[pl.ds(_lin(p) * m_per, m_per), :]

            def copy(k, block, to, src=None, a=a, rows=rows):
                return pltpu.make_async_remote_copy(
                    src_ref=rows(block) if src is None else src, dst_ref=rows(block),
                    send_sem=send_sems.at[a, k], recv_sem=recv_sems.at[a, k],
                    device_id=to, device_id_type=pl.DeviceIdType.MESH)

            mine = pltpu.make_async_copy(ins[a], rows(me), local_sems.at[a])
            mine.start()
            locals_.append(mine)
            first = [copy(0, me, sibling, src=ins[a])]
            first += [copy(1 + j, me, (*chip, c), src=ins[a]) for j, chip in enumerate(chips)]
            for cp in first:
                cp.start()
            every.append((copy, first))
        sends = []
        for a in range(n_arr):
            copy, first = every[a]
            passed = [copy(4 + j, (*chip, c), sibling) for j, chip in enumerate(chips)]
            for j, chip in enumerate(chips):
                copy(1 + j, (*chip, c), me).wait_recv()
                passed[j].start()
            sends += first + passed
        for a in range(n_arr):
            copy, _ = every[a]
            copy(0, sibling, me).wait_recv()
            for j, chip in enumerate(chips):
                copy(4 + j, (*chip, 1 - c), me).wait_recv()
        for cp in sends:
            cp.wait_send()
        for mine in locals_:
            mine.wait()

    outs = pl.pallas_call(
        body, name=name,
        out_shape=[jax.ShapeDtypeStruct((N_DEV * a.shape[0], a.shape[1]), a.dtype) for a in arrays],
        in_specs=[pl.BlockSpec(memory_space=space)] * n_arr,
        out_specs=[pl.BlockSpec(memory_space=space)] * n_arr,
        scratch_shapes=[pltpu.SemaphoreType.DMA((n_arr, 7)), pltpu.SemaphoreType.DMA((n_arr, 7)),
                        pltpu.SemaphoreType.DMA((n_arr,))],
        compiler_params=_params(),
    )(*arrays)
    return list(outs)


_HBM = pl.BlockSpec(memory_space=pltpu.HBM)
_SEM = pl.BlockSpec(memory_space=pltpu.SEMAPHORE)
_DATAFLOW = pltpu.SideEffectType.DATAFLOW_SIDE_EFFECTING


def _push_copies(src_refs, land_refs, send_sems, recv_sems, scatter):
    x, y, c = lax.axis_index("x"), lax.axis_index("y"), lax.axis_index("c")
    me = (x, y, c)
    out, back = [], []
    for k, peer in enumerate(_peers(x, y, c)):
        for a, (src, land) in enumerate(zip(src_refs, land_refs)):
            sems = dict(send_sem=send_sems.at[7 * a + k], recv_sem=recv_sems.at[7 * a + k],
                        device_id=peer, device_id_type=pl.DeviceIdType.MESH)
            mine = src.at[_lin(peer)] if scatter else src
            out.append(pltpu.make_async_remote_copy(src_ref=mine, dst_ref=land.at[_lin(me)], **sems))
            back.append(pltpu.make_async_remote_copy(src_ref=mine, dst_ref=land.at[_lin(peer)], **sems))
    return out, back


def _push_start(srcs, lands, scatter, name, after=()):
    n = len(srcs)
    n_in = 2 * n + len(after)

    def body(*refs):
        send_sems, recv_sems = refs[n_in], refs[n_in + 1]
        out, _ = _push_copies(refs[:n], refs[n:2 * n], send_sems, recv_sems, scatter)
        for cp in out:
            cp.start()
        token = refs[-1]
        token[...] = jnp.zeros_like(token)

    both = list(srcs) + list(lands)
    res = pl.pallas_call(
        body, name=name,
        out_shape=[pltpu.SemaphoreType.DMA((7 * n,)), pltpu.SemaphoreType.DMA((7 * n,))]
        + [pltpu.HBM(a.shape, a.dtype) for a in both] + [jax.ShapeDtypeStruct((8, 128), F32)],
        in_specs=[_HBM] * (2 * n) + [pl.BlockSpec(memory_space=pl.ANY)] * len(after),
        out_specs=[_SEM, _SEM] + [_HBM] * (2 * n) + [pl.BlockSpec(memory_space=pltpu.VMEM)],
        input_output_aliases={i: 2 + i for i in range(2 * n)},
        compiler_params=pltpu.CompilerParams(has_side_effects=_DATAFLOW),
    )(*[pltpu.with_memory_space_constraint(a, pltpu.HBM) for a in both], *after)
    return res[0], res[1], list(res[2:2 + n]), list(res[2 + n:2 + 2 * n]), res[-1]


def _push_wait(send_sems, recv_sems, srcs, lands, scatter, after, name):
    n = len(srcs)

    def body(*refs):
        out, back = _push_copies(refs[:n], refs[n:2 * n], refs[2 * n], refs[2 * n + 1], scatter)
        for cp in out:
            cp.wait_send()
        for cp in back:
            cp.wait_recv()

    both = list(srcs) + list(lands)
    res = pl.pallas_call(
        body, name=name,
        out_shape=[pltpu.HBM(a.shape, a.dtype) for a in both],
        in_specs=[_HBM] * (2 * n) + [_SEM, _SEM, pl.BlockSpec(memory_space=pl.ANY)],
        out_specs=[_HBM] * (2 * n),
        input_output_aliases={i: i for i in range(2 * n)},
        compiler_params=pltpu.CompilerParams(has_side_effects=_DATAFLOW),
    )(*both, send_sems, recv_sems, after)
    return list(res[n:])


def _landing(own, me):
    zone = lax.empty((N_DEV,) + own.shape, own.dtype)
    return lax.dynamic_update_slice(zone, own[None], (me,) + (0,) * own.ndim)


def _mod_rows(c9, w_mod, b_sh, name):
    n = w_mod.shape[1]

    def body(c_ref, w_ref, b_ref, o_ref):
        s9 = _silu(c_ref[...]).astype(BF16)
        o_ref[...] = _dot(s9, w_ref[...].astype(BF16), NN) + b_ref[...]

    return pl.pallas_call(body, name=name, out_shape=jax.ShapeDtypeStruct((16, n), F32),
                          compiler_params=_params())(c9, w_mod, b_sh)


def _mod_grads(dm_rows, dc_rows, c9, w_mod, name):
    d, n = w_mod.shape

    def body(dm_ref, dc_ref, c_ref, w_ref, gw_ref, dc_out):
        dc = dc_ref[...]
        tot = dc[0:1]
        for j in range(1, N_DEV):
            tot = tot + dc[j:j + 1]
        row = lax.broadcasted_iota(jnp.int32, (8, n), 0)
        lower = jnp.where(row == 0, tot, 0.0)
        dmod9 = jnp.concatenate([dm_ref[...], lower], axis=0).astype(BF16)
        c9v = c_ref[...]
        s9 = _silu(c9v).astype(BF16)
        gw_ref[...] = _dot(s9, dmod9, TN)
        ds = _dot(lower.astype(BF16), w_ref[...].astype(BF16), NT)
        dc_out[...] = ds * _dsilu(c9v[8:16])

    return pl.pallas_call(body, name=name,
                          out_shape=[jax.ShapeDtypeStruct((d, n), F32), jax.ShapeDtypeStruct((8, d), F32)],
                          compiler_params=_params())(dm_rows, dc_rows, c9, w_mod)


def _decay_tables(dec, n_heads, name):
    c = CHUNK

    def body(dec_ref, dc_ref, dlf_ref, dlb_ref, qf_ref, kf_ref, qb_ref, kb_ref, cdf_ref, cdb_ref, lg_ref):
        h = pl.program_id(0)
        d = dec_ref[...]
        lane = lax.broadcasted_iota(jnp.int32, d.shape, 1)
        lg = -jnp.exp(jnp.sum(jnp.where(lane == h, d, 0.0), axis=1, keepdims=True))
        lgf, lgb = lg[0:1], lg[1:2]
        i = lax.broadcasted_iota(jnp.int32, (c, c), 0).astype(F32)
        j = lax.broadcasted_iota(jnp.int32, (c, c), 1).astype(F32)
        diff = i - j
        d_f = jnp.where(diff >= 0, jnp.exp(lgf * jnp.maximum(diff, 0.0)), 0.0)
        d_b = jnp.where(diff <= 0, jnp.exp(lgb * jnp.maximum(-diff, 0.0)), 0.0)
        dc_ref[...] = d_f + d_b
        dlf_ref[...] = diff * d_f
        dlb_ref[...] = -diff * d_b
        pos = lax.broadcasted_iota(jnp.int32, (c, HEAD_DIM), 0).astype(F32)
        qf_ref[...] = jnp.exp(lgf * (pos + 1.0))
        kf_ref[...] = jnp.exp(lgf * (c - 1.0 - pos))
        qb_ref[...] = jnp.exp(lgb * (c - pos))
        kb_ref[...] = jnp.exp(lgb * pos)
        ones = jnp.ones((8, HEAD_DIM), F32)
        cdf_ref[...] = jnp.exp(lgf * float(c)) * ones
        cdb_ref[...] = jnp.exp(lgb * float(c)) * ones

        @pl.when(h == 0)
        def _():
            lg_ref[...] = jnp.zeros_like(lg_ref)

        row8 = lax.broadcasted_iota(jnp.int32, (8, HEAD_DIM), 0)
        lane8 = lax.broadcasted_iota(jnp.int32, (8, HEAD_DIM), 1)
        lg_ref[...] += (jnp.where((row8 == 0) & (lane8 == h), lgf, 0.0)
                        + jnp.where((row8 == 1) & (lane8 == h), lgb, 0.0))

    def per_head(*tail):
        return pl.BlockSpec((None,) + tail, lambda h: (h,) + (0,) * len(tail))

    shapes = [(c, c)] * 3 + [(c, HEAD_DIM)] * 4 + [(8, HEAD_DIM)] * 2
    return pl.pallas_call(
        body, name=name, grid=(n_heads,),
        in_specs=[_full(dec.shape)],
        out_specs=[per_head(*s) for s in shapes] + [_full((8, HEAD_DIM))],
        out_shape=[jax.ShapeDtypeStruct((n_heads,) + s, F32) for s in shapes]
        + [jax.ShapeDtypeStruct((8, HEAD_DIM), F32)],
        compiler_params=_params(("arbitrary",)),
    )(dec)


def _modulate(x, nw, shift, scale):
    r = lax.rsqrt(jnp.mean(x * x, axis=-1, keepdims=True) + EPS)
    return ((x * r) * nw * (1.0 + scale) + shift).astype(BF16)


def _prenorm(xt, nw, mod, n_lat, name):
    t, d = xt.shape
    nxb = n_lat // ROW_TILE

    def body(x_ref, nw_ref, mod_ref, o_ref):
        ctx = pl.program_id(0) >= nxb
        m = mod_ref[...]
        o_ref[...] = _modulate(x_ref[...], nw_ref[...], jnp.where(ctx, m[3:4], m[0:1]), jnp.where(ctx, m[4:5], m[1:2]))

    row = pl.BlockSpec((ROW_TILE, d), lambda i: (i, 0))
    return pl.pallas_call(body, name=name, grid=(t // ROW_TILE,),
                          in_specs=[row, _full((1, d)), _full((8, d))],
                          out_specs=row, out_shape=jax.ShapeDtypeStruct((t, d), BF16),
                          compiler_params=_params(("parallel",)))(xt, nw, mod)


def _prenorm_first(x, ctx, nw, mod, name, after=()):
    n_lat, d = x.shape
    t = n_lat + ctx.shape[0]
    nxb = n_lat // ROW_TILE

    def body(x_ref, c_ref, nw_ref, mod_ref, *rest):
        o_ref, xt_ref = rest[-2:]
        m = mod_ref[...]
        nw_v = nw_ref[...]

        @pl.when(pl.program_id(0) < nxb)
        def _():
            xv = x_ref[...]
            xt_ref[...] = xv
            o_ref[...] = _modulate(xv, nw_v, m[0:1], m[1:2])

        @pl.when(pl.program_id(0) >= nxb)
        def _():
            xv = c_ref[...]
            xt_ref[...] = xv
            o_ref[...] = _modulate(xv, nw_v, m[3:4], m[4:5])

    row = pl.BlockSpec((ROW_TILE, d), lambda i: (i, 0))
    return pl.pallas_call(
        body, name=name, grid=(t // ROW_TILE,),
        in_specs=[pl.BlockSpec((ROW_TILE, d), lambda i: (jnp.minimum(i, nxb - 1), 0)),
                  pl.BlockSpec((ROW_TILE, d), lambda i: (jnp.maximum(i - nxb, 0), 0)), _full((1, d)), _full((8, d))]
        + [pl.BlockSpec(memory_space=pl.ANY)] * len(after),
        out_specs=[row, row], out_shape=[jax.ShapeDtypeStruct((t, d), BF16), jax.ShapeDtypeStruct((t, d), F32)],
        compiler_params=_params(("parallel",)))(x, ctx, nw, mod, *after)


def _rope_fwd(v, cos, sa, sb):
    return v * cos + pltpu.roll(v, 96, 1) * sa + pltpu.roll(v, 32, 1) * sb


def _rope_bwd(g, cos, sa, sb):
    return g * cos + pltpu.roll(g * sa, 32, 1) + pltpu.roll(g * sb, 96, 1)


N_PLAIN = 5


def _in_proj(hx, wg, cos, sa, sb, n_heads, tm, name):
    t, d = hx.shape
    n_seg, _, s = wg.shape
    nb = t // tm
    k_scale = HEAD_DIM ** -0.5

    def body(a_ref, w_ref, cos_ref, sa_ref, sb_ref, u_ref, qkv_ref):
        g = pl.program_id(0)
        acc = _dot(a_ref[...], w_ref[...], NN)

        @pl.when(g < N_PLAIN)
        def _():
            u_ref[...] = acc

        @pl.when(g == N_PLAIN + 2)
        def _():
            qkv_ref[...] = acc.astype(BF16)

        for which, scale in ((N_PLAIN, 1.0), (N_PLAIN + 1, k_scale)):
            @pl.when(g == which)
            def _(scale=scale):
                co, a, b = cos_ref[...], sa_ref[...], sb_ref[...]
                for h in range(n_heads):
                    sl = slice(h * HEAD_DIM, (h + 1) * HEAD_DIM)
                    qkv_ref[:, sl] = (_rope_fwd(acc[:, sl], co, a, b) * scale).astype(BF16)

    def w_seg(g):
        return jnp.where(g < N_PLAIN - 1, g, jnp.where(g == N_PLAIN - 1, n_seg - 1, g - 1))

    tab = pl.BlockSpec((tm, HEAD_DIM), lambda g, i: (i, 0))
    return pl.pallas_call(
        body, name=name, grid=(n_seg, nb),
        in_specs=[pl.BlockSpec((tm, d), lambda g, i: (i, 0)), pl.BlockSpec((None, d, s), lambda g, i: (w_seg(g), 0, 0)),
                  tab, tab, tab],
        out_specs=[pl.BlockSpec((None, tm, s), lambda g, i: (jnp.minimum(g, N_PLAIN - 1), jnp.where(g < N_PLAIN, i, nb - 1), 0)),
                   pl.BlockSpec((None, tm, s), lambda g, i: (jnp.maximum(g - N_PLAIN, 0), jnp.where(g < N_PLAIN, 0, i), 0))],
        out_shape=[jax.ShapeDtypeStruct((N_PLAIN, t, s), F32), jax.ShapeDtypeStruct((3, t, s), BF16)],
        compiler_params=_params(("arbitrary", "arbitrary")))(hx, wg, cos, sa, sb)


def _f_order(step, nx, ncc):
    return jnp.where(step < ncc, nx + step, step - ncc)


def _state_sweep(qkv, tabs, n_heads, nx, ncc, name):
    _, t, s = qkv.shape
    nc = nx + ncc
    c = CHUNK

    def body(kf_ref, vf_ref, kb_ref, vb_ref, kft, kbt, cdf, cdb, sf_out, sb_out, sf, sb):
        @pl.when(pl.program_id(0) == 0)
        def _():
            sf[...] = jnp.zeros_like(sf)
            sb[...] = jnp.zeros_like(sb)

        for h in range(n_heads):
            sl = pl.ds(h * HEAD_DIM, HEAD_DIM)
            sf_out[h] = sf[h].astype(BF16)
            sb_out[h] = sb[h].astype(BF16)
            kd = (kf_ref[:, sl].astype(F32) * kft[h]).astype(BF16)
            sf[h] = cdf[h][0:1, :] * sf[h] + _dot(kd, vf_ref[:, sl], TN)
            kd = (kb_ref[:, sl].astype(F32) * kbt[h]).astype(BF16)
            sb[h] = cdb[h][0:1, :] * sb[h] + _dot(kd, vb_ref[:, sl], TN)

    def fwd(j):
        return pl.BlockSpec((None, c, s), lambda i: (j, _f_order(i, nx, ncc), 0))

    def bwd(j):
        return pl.BlockSpec((None, c, s), lambda i: (j, nc - 1 - i, 0))

    st = (None, n_heads, HEAD_DIM, HEAD_DIM)
    return pl.pallas_call(
        body, name=name, grid=(nc,),
        in_specs=[fwd(1), fwd(2), bwd(1), bwd(2), _full((n_heads, c, HEAD_DIM)), _full((n_heads, c, HEAD_DIM)),
                  _full((n_heads, 8, HEAD_DIM)), _full((n_heads, 8, HEAD_DIM))],
        out_specs=[pl.BlockSpec(st, lambda i: (_f_order(i, nx, ncc), 0, 0, 0)),
                   pl.BlockSpec(st, lambda i: (nc - 1 - i, 0, 0, 0))],
        out_shape=[jax.ShapeDtypeStruct((nc, n_heads, HEAD_DIM, HEAD_DIM), BF16)] * 2,
        scratch_shapes=[pltpu.VMEM((n_heads, HEAD_DIM, HEAD_DIM), F32)] * 2,
        compiler_params=_params(("arbitrary",)),
    )(qkv, qkv, qkv, qkv, tabs["kf"], tabs["kb"], tabs["cdf"], tabs["cdb"])


def _halo_specs(s, n8):
    per = CHUNK // 8

    def prev(g):
        return pl.BlockSpec((None, 8, s), lambda i: (g, jnp.maximum(i * per - 1, 0), 0))

    def nxt(g):
        return pl.BlockSpec((None, 8, s), lambda i: (g, jnp.minimum((i + 1) * per, n8 - 1), 0))

    return prev, nxt


def _shifted(a, before, after, has_prev, has_next):
    rows = a.shape[0]
    rowi = lax.broadcasted_iota(jnp.int32, a.shape, 0)
    am = jnp.where(rowi == 0, jnp.where(has_prev, before, 0.0), pltpu.roll(a, 1, 0))
    ap = jnp.where(rowi == rows - 1, jnp.where(has_next, after, 0.0), pltpu.roll(a, rows - 1, 0))
    return am, ap


def _neighbours(i, nx, nc):
    return (i != 0) & (i != nx), (i != nx - 1) & (i != nc - 1)


def _mix_fwd(u, qkv, sf, sb, tabs, conv_w, cnw, gnw, n_heads, nx, ncc, name):
    _, t, s = u.shape
    nc = nx + ncc
    c = CHUNK

    def body(h_ref, b_ref, c_ref, z_ref, rz_ref, hp_ref, hn_ref, cp_ref, cn_ref, q_ref, k_ref, v_ref,
             sf_ref, sb_ref, dc_ref, qft, qbt, w_ref, cnw_ref, gnw_ref, y_ref, o_ref):
        i = pl.program_id(0)
        has_prev, has_next = _neighbours(i, nx, nc)
        a = c_ref[...] * h_ref[...]
        am, ap = _shifted(a, cp_ref[7:8] * hp_ref[7:8], cn_ref[0:1] * hn_ref[0:1], has_prev, has_next)
        w = w_ref[...]
        y0 = w[0:1] * am + w[1:2] * a + w[2:3] * ap
        yb = b_ref[...] * y0
        r = lax.rsqrt(jnp.mean(yb * yb, axis=-1, keepdims=True) + EPS)
        y_ref[:, pl.ds(0, s)] = (_silu(z_ref[...]) * ((yb * r) * cnw_ref[...])).astype(BF16)
        for h in range(n_heads):
            sl = pl.ds(h * HEAD_DIM, HEAD_DIM)
            q, k, v = q_ref[:, sl], k_ref[:, sl], v_ref[:, sl]
            p = (_dot(q, k, NT) * dc_ref[h]).astype(BF16)
            o = _dot(p, v, NN)
            qf = q.astype(F32)
            o += _dot((qf * qft[h]).astype(BF16), sf_ref[h], NN)
            o += _dot((qf * qbt[h]).astype(BF16), sb_ref[h], NN)
            o_ref[:, sl] = o
            mu = jnp.mean(o, axis=-1, keepdims=True)
            var = jnp.mean(jnp.square(o - mu), axis=-1, keepdims=True)
            on = (o - mu) * lax.rsqrt(var + EPS)
            y_ref[:, pl.ds(s + h * HEAD_DIM, HEAD_DIM)] = (
                _silu(rz_ref[:, sl]) * (on * gnw_ref[:, sl])).astype(BF16)

    def seg(g):
        return pl.BlockSpec((None, c, s), lambda i: (g, i, 0))

    prev, nxt = _halo_specs(s, t // 8)
    row = pl.BlockSpec((c, s), lambda i: (i, 0))
    st = pl.BlockSpec((None, n_heads, HEAD_DIM, HEAD_DIM), lambda i: (i, 0, 0, 0))
    return pl.pallas_call(
        body, name=name, grid=(nc,),
        in_specs=[seg(0), seg(1), seg(2), seg(3), seg(4), prev(0), nxt(0), prev(2), nxt(2), seg(0), seg(1), seg(2),
                  st, st, _full((n_heads, c, c)), _full((n_heads, c, HEAD_DIM)), _full((n_heads, c, HEAD_DIM)),
                  _full((3, s)), _full((1, s)), _full((1, s))],
        out_specs=[pl.BlockSpec((c, 2 * s), lambda i: (i, 0)), row],
        out_shape=[jax.ShapeDtypeStruct((t, 2 * s), BF16), jax.ShapeDtypeStruct((t, s), F32)],
        compiler_params=_params(("parallel",)),
    )(u, u, u, u, u, u, u, u, u, qkv, qkv, qkv, sf, sb, tabs["dc"], tabs["qf"], tabs["qb"], conv_w, cnw, gnw)


def _row_gate(mod_ref, row0, rows, n_lat, col):
    rowi = row0 + lax.broadcasted_iota(jnp.int32, (rows, 1), 0)
    return jnp.where(rowi >= n_lat, mod_ref[5:6, col], mod_ref[2:3, col])


def _out_proj(ycat, w_out, xt, mod, n_lat, tm, tn, name):
    t, d = xt.shape

    def body(a_ref, w_ref, x_ref, mod_ref, m_ref, xo_ref):
        m = _dot(a_ref[...], w_ref[...], NN)
        m_ref[...] = m
        gate = _row_gate(mod_ref, pl.program_id(1) * tm, tm, n_lat, slice(None))
        xo_ref[...] = x_ref[...] + gate * m

    blk = pl.BlockSpec((tm, tn), lambda j, i: (i, j))
    return pl.pallas_call(
        body, name=name, grid=(d // tn, t // tm),
        in_specs=[pl.BlockSpec((tm, d), lambda j, i: (i, 0)), pl.BlockSpec((d, tn), lambda j, i: (0, j)), blk,
                  pl.BlockSpec((8, tn), lambda j, i: (0, j))],
        out_specs=[blk, blk], out_shape=[jax.ShapeDtypeStruct((t, d), F32)] * 2,
        compiler_params=_params(("parallel", "parallel")))(ycat, w_out, xt, mod)


def _gate_bwd_rows(dx, m_ref, mod_ref, dm_ref, gacc_ref, ctx):
    g_row = 5 if ctx else 2
    dm_ref[...] = (dx * mod_ref[g_row:g_row + 1, :]).astype(BF16)
    gacc_ref[g_row:g_row + 1, :] += jnp.sum(dx * m_ref[...], axis=0, keepdims=True)


def _loss_head(x2, tgt, fnw, m, mod, n_lat, name):
    t, d = x2.shape
    nxb = n_lat // ROW_TILE

    def body(x_ref, t_ref, w_ref, m_ref, mod_ref, dx_ref, dm_ref, loss_ref, dw_ref, gacc_ref):
        i = pl.program_id(0)

        @pl.when(i == 0)
        def _():
            loss_ref[...] = jnp.zeros_like(loss_ref)
            dw_ref[...] = jnp.zeros_like(dw_ref)
            gacc_ref[...] = jnp.zeros_like(gacc_ref)

        @pl.when(i < nxb)
        def _():
            x = x_ref[...]
            w = w_ref[...]
            r = lax.rsqrt(jnp.mean(x * x, axis=-1, keepdims=True) + EPS)
            xn = x * r
            e = xn * w - t_ref[...]
            loss_ref[...] += 0.5 * jnp.sum(jnp.mean(e * e, axis=-1, keepdims=True), axis=0, keepdims=True)
            dy = e * (1.0 / d)
            dw_ref[0:1, :] += jnp.sum(dy * xn, axis=0, keepdims=True)
            dxn = dy * w
            dx = r * (dxn - xn * jnp.mean(dxn * xn, axis=-1, keepdims=True))
            dx_ref[...] = dx
            _gate_bwd_rows(dx, m_ref, mod_ref, dm_ref, gacc_ref, False)

        @pl.when(i >= nxb)
        def _():
            dx_ref[...] = jnp.zeros_like(dx_ref)
            dm_ref[...] = jnp.zeros_like(dm_ref)

    row = pl.BlockSpec((ROW_TILE, d), lambda i: (i, 0))
    return pl.pallas_call(
        body, name=name, grid=(t // ROW_TILE,),
        in_specs=[row, pl.BlockSpec((ROW_TILE, d), lambda i: (jnp.minimum(i, nxb - 1), 0)), _full((1, d)), row,
                  _full((8, d))],
        out_specs=[row, row, _full((8, HEAD_DIM)), _full((8, d)), _full((8, d))],
        out_shape=[jax.ShapeDtypeStruct((t, d), F32), jax.ShapeDtypeStruct((t, d), BF16),
                   jax.ShapeDtypeStruct((8, HEAD_DIM), F32), jax.ShapeDtypeStruct((8, d), F32),
                   jax.ShapeDtypeStruct((8, d), F32)],
        compiler_params=_params(("arbitrary",)))(x2, tgt, fnw, m, mod)


def _matmul_nt(a, w, name):
    t, k = a.shape
    n = w.shape[0]
    tm = _mm_rows(t)
    tn = min(n, 1024)

    def body(a_ref, w_ref, o_ref):
        o_ref[...] = _dot(a_ref[...], w_ref[...], NT)

    return pl.pallas_call(
        body, name=name, grid=(n // tn, t // tm),
        in_specs=[pl.BlockSpec((tm, k), lambda j, i: (i, 0)), pl.BlockSpec((tn, k), lambda j, i: (j, 0))],
        out_specs=pl.BlockSpec((tm, tn), lambda j, i: (i, j)),
        out_shape=jax.ShapeDtypeStruct((t, n), F32),
        compiler_params=_params(("parallel", "parallel")))(a, w)


def _weight_grad(a, b, bm, name):
    t, m = a.shape
    n_g, _, n = b.shape
    bt = _mm_rows(t)
    nt = t // bt

    def body(a_ref, b_ref, o_ref, acc):
        k = pl.program_id(2)

        @pl.when(k == 0)
        def _():
            acc[...] = jnp.zeros_like(acc)

        acc[...] += _dot(a_ref[...], b_ref[...], TN)

        @pl.when(k == nt - 1)
        def _():
            o_ref[...] = acc[...].astype(o_ref.dtype)

    return pl.pallas_call(
        body, name=name, grid=(n_g, m // bm, nt),
        in_specs=[pl.BlockSpec((bt, bm), lambda g, i, k: (k, i)), pl.BlockSpec((None, bt, n), lambda g, i, k: (g, k, 0))],
        out_specs=pl.BlockSpec((None, bm, n), lambda g, i, k: (g, i, 0)),
        out_shape=jax.ShapeDtypeStruct((n_g, m, n), BF16),
        scratch_shapes=[pltpu.VMEM((bm, n), F32)],
        compiler_params=_params(("parallel", "parallel", "arbitrary")))(a, b)


def _mix_bwd_a(dycat, u, o, conv_w, cnw, gnw, n_heads, nx, ncc, name):
    _, t, s = u.shape
    nc = nx + ncc
    c = CHUNK

    def body(dy_ref, h_ref, b_ref, c_ref, z_ref, rz_ref, hp_ref, hn_ref, cp_ref, cn_ref, o_ref, w_ref,
             cnw_ref, gnw_ref, g_ref, dz_ref, db_ref, drz_ref, do_ref, acc_ref):
        i = pl.program_id(0)

        @pl.when(i == 0)
        def _():
            acc_ref[...] = jnp.zeros_like(acc_ref)

        has_prev, has_next = _neighbours(i, nx, nc)
        a = c_ref[...] * h_ref[...]
        am, ap = _shifted(a, cp_ref[7:8] * hp_ref[7:8], cn_ref[0:1] * hn_ref[0:1], has_prev, has_next)
        w = w_ref[...]
        y0 = w[0:1] * am + w[1:2] * a + w[2:3] * ap
        bb = b_ref[...]
        yb = bb * y0
        r = lax.rsqrt(jnp.mean(yb * yb, axis=-1, keepdims=True) + EPS)
        ynn = yb * r
        z = z_ref[...]
        dyc = dy_ref[:, pl.ds(0, s)]
        cw = cnw_ref[...]
        dz_ref[...] = (dyc * (ynn * cw) * _dsilu(z)).astype(BF16)
        dyn = dyc * _silu(z)
        acc_ref[0:1, :] += jnp.sum(dyn * ynn, axis=0, keepdims=True)
        dynn = dyn * cw
        dyb = r * (dynn - ynn * jnp.mean(dynn * ynn, axis=-1, keepdims=True))
        db_ref[...] = (dyb * y0).astype(BF16)
        g_ref[...] = dyb * bb
        for h in range(n_heads):
            sl = pl.ds(h * HEAD_DIM, HEAD_DIM)
            ov = o_ref[:, sl]
            mu = jnp.mean(ov, axis=-1, keepdims=True)
            var = jnp.mean(jnp.square(ov - mu), axis=-1, keepdims=True)
            rs = lax.rsqrt(var + EPS)
            on = (ov - mu) * rs
            dyr = dy_ref[:, pl.ds(s + h * HEAD_DIM, HEAD_DIM)]
            rz = rz_ref[:, sl]
            gw = gnw_ref[:, sl]
            drz_ref[:, sl] = (dyr * (on * gw) * _dsilu(rz)).astype(BF16)
            dyg = dyr * _silu(rz)
            acc_ref[1:2, sl] += jnp.sum(dyg * on, axis=0, keepdims=True)
            don = dyg * gw
            do = rs * (don - jnp.mean(don, axis=-1, keepdims=True)
                       - on * jnp.mean(don * on, axis=-1, keepdims=True))
            do_ref[:, sl] = do.astype(BF16)

    def seg(g):
        return pl.BlockSpec((None, c, s), lambda i: (g, i, 0))

    prev, nxt = _halo_specs(s, t // 8)
    row = pl.BlockSpec((c, s), lambda i: (i, 0))
    return pl.pallas_call(
        body, name=name, grid=(nc,),
        in_specs=[pl.BlockSpec((c, 2 * s), lambda i: (i, 0)), seg(0), seg(1), seg(2), seg(3), seg(4),
                  prev(0), nxt(0), prev(2), nxt(2), row, _full((3, s)), _full((1, s)), _full((1, s))],
        out_specs=[row, row, row, row, row, _full((8, s))],
        out_shape=[jax.ShapeDtypeStruct((t, s), F32)] + [jax.ShapeDtypeStruct((t, s), BF16)] * 4
        + [jax.ShapeDtypeStruct((8, s), F32)],
        compiler_params=_params(("arbitrary",)),
    )(dycat, u, u, u, u, u, u, u, u, u, o, conv_w, cnw, gnw)


def _grad_state_sweep(qkv, do, tabs, n_heads, nx, ncc, name):
    t, s = do.shape
    nc = nx + ncc
    c = CHUNK

    def body(qf_ref, df_ref, qb_ref, db_ref, qft, qbt, cdf, cdb, gf_out, gb_out, gf, gb):
        @pl.when(pl.program_id(0) == 0)
        def _():
            gf[...] = jnp.zeros_like(gf)
            gb[...] = jnp.zeros_like(gb)

        for h in range(n_heads):
            sl = pl.ds(h * HEAD_DIM, HEAD_DIM)
            gf_out[h] = gf[h].astype(BF16)
            gb_out[h] = gb[h].astype(BF16)
            qd = (qf_ref[:, sl].astype(F32) * qft[h]).astype(BF16)
            gf[h] = cdf[h][0:1, :] * gf[h] + _dot(qd, df_ref[:, sl], TN)
            qd = (qb_ref[:, sl].astype(F32) * qbt[h]).astype(BF16)
            gb[h] = cdb[h][0:1, :] * gb[h] + _dot(qd, db_ref[:, sl], TN)

    fwd = pl.BlockSpec((c, s), lambda i: (_f_order(nc - 1 - i, nx, ncc), 0))
    bwd = pl.BlockSpec((c, s), lambda i: (i, 0))
    q_fwd = pl.BlockSpec((None, c, s), lambda i: (0, _f_order(nc - 1 - i, nx, ncc), 0))
    q_bwd = pl.BlockSpec((None, c, s), lambda i: (0, i, 0))
    st = (None, n_heads, HEAD_DIM, HEAD_DIM)
    return pl.pallas_call(
        body, name=name, grid=(nc,),
        in_specs=[q_fwd, fwd, q_bwd, bwd, _full((n_heads, c, HEAD_DIM)), _full((n_heads, c, HEAD_DIM)),
                  _full((n_heads, 8, HEAD_DIM)), _full((n_heads, 8, HEAD_DIM))],
        out_specs=[pl.BlockSpec(st, lambda i: (_f_order(nc - 1 - i, nx, ncc), 0, 0, 0)),
                   pl.BlockSpec(st, lambda i: (i, 0, 0, 0))],
        out_shape=[jax.ShapeDtypeStruct((nc, n_heads, HEAD_DIM, HEAD_DIM), BF16)] * 2,
        scratch_shapes=[pltpu.VMEM((n_heads, HEAD_DIM, HEAD_DIM), F32)] * 2,
        compiler_params=_params(("arbitrary",)),
    )(qkv, do, qkv, do, tabs["qf"], tabs["qb"], tabs["cdf"], tabs["cdb"])


def _mix_bwd_b(u, g, dz, db, drz, qkv, do, sf, sb, gf, gb, tabs, cos, sa, sb_tab, conv_w,
               n_heads, nx, ncc, name):
    _, t, s = u.shape
    nc = nx + ncc
    c = CHUNK
    k_scale = HEAD_DIM ** -0.5

    def body(h_ref, c_ref, g_ref, gp_ref, gn_ref, dz_ref, db_ref, drz_ref, q_ref, k_ref, v_ref, do_ref,
             sf_ref, sb_ref, gf_ref, gb_ref, dc_t, dlf_t, dlb_t, qft, kft, qbt, kbt, cdf, cdb, lg_ref,
             cos_ref, sa_ref, sb_ref2, w_ref, du_ref, dw_ref, dlg_ref):
        i = pl.program_id(0)

        @pl.when(i == 0)
        def _():
            dw_ref[...] = jnp.zeros_like(dw_ref)
            dlg_ref[...] = jnp.zeros_like(dlg_ref)

        has_prev, has_next = _neighbours(i, nx, nc)
        gv = g_ref[...]
        gm, gp = _shifted(gv, gp_ref[7:8], gn_ref[0:1], has_prev, has_next)
        w = w_ref[...]
        da = w[0:1] * gp + w[1:2] * gv + w[2:3] * gm
        hh, cc = h_ref[...], c_ref[...]
        du_ref[0] = (da * cc).astype(BF16)
        du_ref[2] = (da * hh).astype(BF16)
        a = cc * hh
        dw_ref[0:1, :] += jnp.sum(a * gp, axis=0, keepdims=True)
        dw_ref[1:2, :] += jnp.sum(a * gv, axis=0, keepdims=True)
        dw_ref[2:3, :] += jnp.sum(a * gm, axis=0, keepdims=True)
        du_ref[1] = db_ref[...]
        du_ref[3] = dz_ref[...]
        du_ref[7] = drz_ref[...]

        co, ra, rb = cos_ref[...], sa_ref[...], sb_ref2[...]
        pos = lax.broadcasted_iota(jnp.int32, (c, HEAD_DIM), 0).astype(F32)
        row8 = lax.broadcasted_iota(jnp.int32, (8, HEAD_DIM), 0)
        lane8 = lax.broadcasted_iota(jnp.int32, (8, HEAD_DIM), 1)
        dlg = jnp.zeros((8, HEAD_DIM), F32)
        for h in range(n_heads):
            sl = pl.ds(h * HEAD_DIM, HEAD_DIM)
            q, k, v, do = q_ref[:, sl], k_ref[:, sl], v_ref[:, sl], do_ref[:, sl]
            qf, kf, dof = q.astype(F32), k.astype(F32), do.astype(F32)
            s_f, s_b, g_f, g_b = sf_ref[h], sb_ref[h], gf_ref[h], gb_ref[h]
            p = _dot(q, k, NT)
            pd = _dot(do, v, NT)
            pdd = (pd * dc_t[h]).astype(BF16)
            dq = _dot(pdd, k, NN)
            dk = _dot(pdd, q, TN)
            dv = _dot((p * dc_t[h]).astype(BF16), do, TN)
            dq_f = _dot((dof * qft[h]).astype(BF16), s_f, NT)
            dq_b = _dot((dof * qbt[h]).astype(BF16), s_b, NT)
            dk_f = _dot(v, g_f, NT) * kft[h]
            dk_b = _dot(v, g_b, NT) * kbt[h]
            dv += _dot((kf * kft[h]).astype(BF16), g_f, NN) + _dot((kf * kbt[h]).astype(BF16), g_b, NN)
            ppd = p * pd
            cd_f, cd_b = cdf[h][0:1, :], cdb[h][0:1, :]
            t_f = _sum_all(dlf_t[h] * ppd + (pos + 1.0) * qf * dq_f + (c - 1.0 - pos) * kf * dk_f
                           + float(c) * (cd_f * (g_f.astype(F32) * s_f.astype(F32))))
            t_b = _sum_all(dlb_t[h] * ppd + (c - pos) * qf * dq_b + pos * kf * dk_b
                           + float(c) * (cd_b * (g_b.astype(F32) * s_b.astype(F32))))
            dlg += jnp.where((row8 == 0) & (lane8 == h), t_f, 0.0) + jnp.where((row8 == 1) & (lane8 == h), t_b, 0.0)
            du_ref[4, :, sl] = _rope_bwd(dq + dq_f + dq_b, co, ra, rb).astype(BF16)
            du_ref[5, :, sl] = (_rope_bwd(dk + dk_f + dk_b, co, ra, rb) * k_scale).astype(BF16)
            du_ref[6, :, sl] = dv.astype(BF16)
        dlg_ref[...] += dlg

        @pl.when(i == nc - 1)
        def _():
            dlg_ref[...] = dlg_ref[...] * lg_ref[...]

    def seg(gi):
        return pl.BlockSpec((None, c, s), lambda i: (gi, i, 0))

    per = c // 8
    n8 = t // 8
    row = pl.BlockSpec((c, s), lambda i: (i, 0))
    st = pl.BlockSpec((None, n_heads, HEAD_DIM, HEAD_DIM), lambda i: (i, 0, 0, 0))
    tab = pl.BlockSpec((c, HEAD_DIM), lambda i: (i, 0))
    hc = _full((n_heads, c, HEAD_DIM))
    cc_ = _full((n_heads, c, c))
    h8 = _full((n_heads, 8, HEAD_DIM))
    return pl.pallas_call(
        body, name=name, grid=(nc,),
        in_specs=[seg(0), seg(2), row,
                  pl.BlockSpec((8, s), lambda i: (jnp.maximum(i * per - 1, 0), 0)),
                  pl.BlockSpec((8, s), lambda i: (jnp.minimum((i + 1) * per, n8 - 1), 0)),
                  row, row, row, seg(0), seg(1), seg(2), row, st, st, st, st, cc_, cc_, cc_, hc, hc, hc, hc, h8, h8,
                  _full((8, HEAD_DIM)), tab, tab, tab, _full((3, s))],
        out_specs=[pl.BlockSpec((8, c, s), lambda i: (0, i, 0)), _full((8, s)), _full((8, HEAD_DIM))],
        out_shape=[jax.ShapeDtypeStruct((8, t, s), BF16), jax.ShapeDtypeStruct((8, s), F32),
                   jax.ShapeDtypeStruct((8, HEAD_DIM), F32)],
        compiler_params=_params(("arbitrary",)),
    )(u, u, g, g, g, dz, db, drz, qkv, qkv, qkv, do, sf, sb, gf, gb, tabs["dc"], tabs["dlf"], tabs["dlb"],
      tabs["qf"], tabs["kf"], tabs["qb"], tabs["kb"], tabs["cdf"], tabs["cdb"], tabs["lg"], cos, sa, sb_tab, conv_w)


def _in_proj_bwd(du, wg, tm, name, after=()):
    n_seg, t, s = du.shape
    d = wg.shape[1]

    def body(a_ref, w_ref, *rest):
        o_ref = rest[-1]
        g = pl.program_id(1)
        part = _dot(a_ref[...], w_ref[...], NT)

        @pl.when(g == 0)
        def _():
            o_ref[...] = part

        @pl.when(g > 0)
        def _():
            o_ref[...] += part

    return pl.pallas_call(
        body, name=name, grid=(t // tm, n_seg),
        in_specs=[pl.BlockSpec((None, tm, s), lambda i, g: (g, i, 0)), pl.BlockSpec((None, d, s), lambda i, g: (g, 0, 0))]
        + [pl.BlockSpec(memory_space=pl.ANY)] * len(after),
        out_specs=pl.BlockSpec((tm, d), lambda i, g: (i, 0)),
        out_shape=jax.ShapeDtypeStruct((t, d), F32),
        compiler_params=_params(("parallel", "arbitrary")))(du, wg, *after)


def _prenorm_bwd(dhx, xt, dxo, nw, mod, n_lat, below, name):
    t, d = xt.shape
    nxb = n_lat // ROW_TILE
    first = below is None

    def body(dh_ref, x_ref, dxo_ref, nw_ref, mod_ref, *rest):
        if first:
            dx_ref, acc_ref = rest
        else:
            m_ref, modb_ref, dx_ref, acc_ref, dm_ref, gacc_ref = rest
        i = pl.program_id(0)

        @pl.when(i == 0)
        def _():
            acc_ref[...] = jnp.zeros_like(acc_ref)
            if not first:
                gacc_ref[...] = jnp.zeros_like(gacc_ref)

        ctx = i >= nxb
        m = mod_ref[...]
        scale1 = 1.0 + jnp.where(ctx, m[4:5], m[1:2])
        x = x_ref[...]
        nw_v = nw_ref[...]
        r = lax.rsqrt(jnp.mean(x * x, axis=-1, keepdims=True) + EPS)
        xn = x * r
        dh = dh_ref[...]
        dshift = jnp.sum(dh, axis=0, keepdims=True)
        dscale = jnp.sum(dh * (xn * nw_v), axis=0, keepdims=True)
        acc_ref[6:7, :] += jnp.sum(dh * scale1 * xn, axis=0, keepdims=True)
        dxn = dh * (nw_v * scale1)
        dx = dxo_ref[...] + r * (dxn - xn * jnp.mean(dxn * xn, axis=-1, keepdims=True))

        @pl.when(i < nxb)
        def _():
            acc_ref[0:1, :] += dshift
            acc_ref[1:2, :] += dscale
            dx_ref[...] = dx
            if not first:
                _gate_bwd_rows(dx, m_ref, modb_ref, dm_ref, gacc_ref, False)

        @pl.when(i >= nxb)
        def _():
            acc_ref[3:4, :] += dshift
            acc_ref[4:5, :] += dscale
            if not first:
                dx_ref[...] = dx
                _gate_bwd_rows(dx, m_ref, modb_ref, dm_ref, gacc_ref, True)

    row = pl.BlockSpec((ROW_TILE, d), lambda i: (i, 0))
    acc = _full((8, d))
    if first:
        extra_in, extra = [], []
        dx_spec = pl.BlockSpec((ROW_TILE, d), lambda i: (jnp.minimum(i, nxb - 1), 0))
        dx_shape = jax.ShapeDtypeStruct((n_lat, d), F32)
        extra_out, extra_shape = [], []
    else:
        extra_in, extra = [row, acc], list(below)
        dx_spec, dx_shape = row, jax.ShapeDtypeStruct((t, d), F32)
        extra_out, extra_shape = [row, acc], [jax.ShapeDtypeStruct((t, d), BF16), jax.ShapeDtypeStruct((8, d), F32)]
    return pl.pallas_call(body, name=name, grid=(t // ROW_TILE,),
                          in_specs=[row, row, row, _full((1, d)), acc] + extra_in,
                          out_specs=[dx_spec, acc] + extra_out,
                          out_shape=[dx_shape, jax.ShapeDtypeStruct((8, d), F32)] + extra_shape,
                          compiler_params=_params(("arbitrary",)))(dhx, xt, dxo, nw, mod, *extra)


def _adamw(g, w, m, v):
    m = ADAM_B1 * m + (1.0 - ADAM_B1) * g
    v = ADAM_B2 * v + (1.0 - ADAM_B2) * jnp.square(g)
    m_hat = m / (1.0 - ADAM_B1 ** ADAM_STEP)
    v_hat = v / (1.0 - ADAM_B2 ** ADAM_STEP)
    delta = -ADAM_LR * (m_hat / (jnp.sqrt(v_hat) + ADAM_EPS) + ADAM_WD * w)
    return delta, m, v


def _sum_adamw(parts, w, m, v, name, row0=0, into=None):
    n_p, r, n = parts.shape
    r_all = w.shape[0]
    part_block_bytes = 4 * 1024 * 1024
    br = 8
    for cand in (512, 256, 128, 64, 32, 16):
        if r % cand == 0 and row0 % cand == 0 and n_p * cand * n * parts.dtype.itemsize <= part_block_bytes:
            br = cand
            break
    blk0 = row0 // br

    def body(p_ref, w_ref, m_ref, v_ref, *rest):
        g_out, d_out, m_out, v_out = rest[-4:]
        g = p_ref[0].astype(F32)
        for j in range(1, n_p):
            g = g + p_ref[j].astype(F32)
        g_out[...] = g
        d_out[...], m_out[...], v_out[...] = _adamw(g, w_ref[...], m_ref[...], v_ref[...])

    row = pl.BlockSpec((br, n), lambda i: (i + blk0, 0))
    kept = [] if into is None else list(into)
    return pl.pallas_call(body, name=name, grid=(r // br,),
                          in_specs=[pl.BlockSpec((n_p, br, n), lambda i: (0, i, 0)), row, row, row]
                          + [pl.BlockSpec(memory_space=pl.ANY)] * len(kept),
                          out_specs=[row] * 4, out_shape=[jax.ShapeDtypeStruct((r_all, n), F32)] * 4,
                          input_output_aliases={4 + j: j for j in range(len(kept))},
                          compiler_params=_params(("parallel",)))(parts, w, m, v, *kept)


def _rope_tables(n_lat, n_ctx):
    f = HEAD_DIM // 4
    rows = n_lat // GRID_W
    inv = ROPE_BASE ** (-jnp.arange(f, dtype=F32) / f)
    ang_r = jnp.arange(rows).astype(F32)[:, None] * inv[None, :]
    ang_c = jnp.arange(GRID_W).astype(F32)[:, None] * inv[None, :]

    def by_row(a):
        return jnp.broadcast_to(a[:, None, :], (rows, GRID_W, f)).reshape(n_lat, f)

    def by_col(a):
        return jnp.broadcast_to(a[None, :, :], (rows, GRID_W, f)).reshape(n_lat, f)

    cr, sr, cc, sc = by_row(jnp.cos(ang_r)), by_row(jnp.sin(ang_r)), by_col(jnp.cos(ang_c)), by_col(jnp.sin(ang_c))
    zero = jnp.zeros_like(cr)
    cos = jnp.concatenate([cr, cr, cc, cc], axis=-1)
    sa = jnp.concatenate([-sr, zero, -sc, zero], axis=-1)
    sb = jnp.concatenate([zero, sr, zero, sc], axis=-1)
    pad = jnp.zeros((n_ctx, HEAD_DIM), F32)
    return (jnp.concatenate([cos, pad + 1.0], axis=0), jnp.concatenate([sa, pad], axis=0),
            jnp.concatenate([sb, pad], axis=0))


def _pad_rows(a, rows):
    return jnp.pad(a, [(0, rows - a.shape[0])] + [(0, 0)] * (a.ndim - 1))


def _pad_cols(a, cols):
    return jnp.pad(a, [(0, 0), (0, cols - a.shape[1])])


def kernel(x, c, ctx, c_ctx, norm_w, w_mod, b_mod, w_in, conv_w, conv_norm_w, ret_norm_w, ret_decay_f, ret_decay_b, w_out, final_norm_w, loss_target, m_c_ctx, m_norm_w, m_w_mod, m_b_mod, m_w_in, m_conv_w, m_conv_norm_w, m_ret_norm_w, m_ret_decay_f, m_ret_decay_b, m_w_out, m_final_norm_w, v_c_ctx, v_norm_w, v_w_mod, v_b_mod, v_w_in, v_conv_w, v_conv_norm_w, v_ret_norm_w, v_ret_decay_f, v_ret_decay_b, v_w_out, v_final_norm_w):
    depth = norm_w.shape[0]
    n_lat, d = x.shape[1], x.shape[2]
    n_ctx = ctx.shape[1]
    s = d // 2
    n_heads = ret_decay_f.shape[1]
    nx, ncc = n_lat // CHUNK, n_ctx // CHUNK
    n_mod = w_mod.shape[2]
    n_cw = conv_w.shape[2]
    r_out = w_out.shape[1]
    assert s == n_heads * HEAD_DIM and w_in.shape[2] == s and N_DEV * r_out == d
    assert n_lat % ROW_TILE == 0 and n_ctx % ROW_TILE == 0 and 3 * depth * n_cw <= d and d >= 3 * n_mod // 3
    me = 4 * lax.axis_index("x") + 2 * lax.axis_index("y") + lax.axis_index("c")

    first = jnp.concatenate([c.reshape(1, d), _pad_cols(conv_w.reshape(1, -1), d), jnp.zeros((6, d), F32)], axis=0)
    (first_g,) = _all_gather([first], "gather_cond", True)
    first_g = first_g.reshape(N_DEV, 8, d)
    c_all = first_g[:, 0, :]
    conv_full = first_g[:, 1, :3 * depth * n_cw].reshape(N_DEV, depth, 3, n_cw)
    conv_full = conv_full.transpose(1, 2, 0, 3).reshape(depth, 3, N_DEV * n_cw)
    c9 = jnp.concatenate([c_all, c_ctx.reshape(1, d), jnp.zeros((7, d), F32)], axis=0)

    b_sh = lax.dynamic_slice(b_mod, (0, me * n_mod), (depth, n_mod))
    mod_sh = jnp.concatenate([_mod_rows(c9, w_mod[l], b_sh[l:l + 1], f"mod_rows_l{l}") for l in range(depth)], axis=0)
    (mod_g,) = _all_gather([mod_sh], "gather_mod", True)
    mod_g = mod_g.reshape(N_DEV, depth, 16, n_mod)
    mods = []
    for l in range(depth):
        mine = lax.dynamic_index_in_dim(mod_g[:, l], me, axis=1, keepdims=False).reshape(3, d)
        cx = mod_g[:, l, 8, :].reshape(3, d)
        mods.append(jnp.concatenate([mine, cx, jnp.zeros((2, d), F32)], axis=0))

    w_in_bf = [w_in[l].astype(BF16) for l in range(depth)]
    w_out_bf = [w_out[l].astype(BF16) for l in range(depth)]
    (w_in0,) = _all_gather([w_in_bf[0]], "gather_w_in0", False)
    w_in_g = [w_in0.reshape(N_DEV, d, s)] + [None] * (depth - 1)
    w_out_g = [None] * depth
    pending, tokens = [], []
    for l in range(depth):
        srcs = [w_out_bf[l]] + ([w_in_bf[l]] if l > 0 else [])
        lands = [_landing(a, me) for a in srcs]
        started = _push_start(srcs, lands, False, f"weights_start_l{l}", after=[w_in0] if l == 0 else tokens[-1:])
        pending.append(started[:4])
        tokens.append(started[4])

    cos, sa, sb_tab = _rope_tables(n_lat, n_ctx)
    t_all = n_lat + n_ctx

    saved = []
    xt = None
    for l in range(depth):
        tiles = _tiles(l, t_all, d)
        names = ["dc", "dlf", "dlb", "qf", "kf", "qb", "kb", "cdf", "cdb", "lg"]
        dec = jnp.stack([ret_decay_f[l], ret_decay_b[l]], axis=0)
        tabs = dict(zip(names, _decay_tables(dec, n_heads, f"decay_tables_l{l}")))
        if l == 0:
            hx, xt = _prenorm_first(x[0], ctx[0], norm_w[0:1], mods[0], "prenorm_l0", after=tokens)
        else:
            landed = _push_wait(*pending[l], False, xt, f"weights_wait_l{l}")
            w_out_g[l], w_in_g[l] = landed[0].reshape(d, d), landed[1]
            hx = _prenorm(xt, norm_w[l:l + 1], mods[l], n_lat, f"prenorm_l{l}")
        u, qkv = _in_proj(hx, w_in_g[l], cos, sa, sb_tab, n_heads, tiles["in_tm"], f"in_proj_l{l}")
        sf, sb = _state_sweep(qkv, tabs, n_heads, nx, ncc, f"state_sweep_l{l}")
        ycat, o = _mix_fwd(u, qkv, sf, sb, tabs, conv_full[l], conv_norm_w[l:l + 1], ret_norm_w[l:l + 1],
                           n_heads, nx, ncc, f"mix_fwd_l{l}")
        if l == 0:
            (landed,) = _push_wait(*pending[0], False, ycat, "weights_wait_l0")
            w_out_g[0] = landed.reshape(d, d)
        m_res, x_new = _out_proj(ycat, w_out_g[l], xt, mods[l], n_lat, tiles["out_tm"], tiles["out_tn"], f"out_proj_l{l}")
        saved.append(dict(tabs=tabs, xt=xt, hx=hx, u=u, qkv=qkv, sf=sf, sb=sb, ycat=ycat, o=o, m=m_res, tiles=tiles))
        xt = x_new

    top = saved[-1]
    dxt, dm, loss_blk, dfnw, gate_acc = _loss_head(xt, loss_target[0], final_norm_w.reshape(1, d), top["m"],
                                                   mods[depth - 1], n_lat, "loss_head")
    loss = lax.psum(loss_blk[0, 0], MESH_AXES)

    dmod_x, dmod_c, dnw, dcnw, dgnw, dconv, ddec, dwin, dwout = [], [], [], [], [], [], [], [], []
    for l in reversed(range(depth)):
        sv = saved[l]
        tiles = sv["tiles"]
        dycat = _matmul_nt(dm, w_out_g[l], f"out_proj_bwd_l{l}")
        dwout.append(_weight_grad(sv["ycat"], dm.reshape(1, *dm.shape), min(d, 1024), f"w_out_grad_l{l}")[0])
        g, dz, db, drz, do, norm_acc = _mix_bwd_a(dycat, sv["u"], sv["o"], conv_full[l], conv_norm_w[l:l + 1],
                                                   ret_norm_w[l:l + 1], n_heads, nx, ncc, f"mix_bwd_a_l{l}")
        gf, gb = _grad_state_sweep(sv["qkv"], do, sv["tabs"], n_heads, nx, ncc, f"grad_state_sweep_l{l}")
        du, conv_acc, dlg = _mix_bwd_b(sv["u"], g, dz, db, drz, sv["qkv"], do, sv["sf"], sv["sb"],
                                       gf, gb, sv["tabs"], cos, sa, sb_tab, conv_full[l], n_heads, nx, ncc,
                                       f"mix_bwd_b_l{l}")
        srcs = [_weight_grad(sv["hx"], du, tiles["wg_bm"], f"w_in_grad_l{l}"), dwout[-1].reshape(N_DEV, r_out, d)]
        lands = [_landing(lax.dynamic_index_in_dim(a, me, axis=0, keepdims=False), me) for a in srcs]
        started = _push_start(srcs, lands, True, f"grads_start_l{l}")
        dwin.append(started[:4])
        dhx = _in_proj_bwd(du, w_in_g[l], tiles["bwd_tm"], f"in_proj_bwd_l{l}", after=started[4:])
        this_gate = gate_acc
        if l > 0:
            below = (saved[l - 1]["m"], mods[l - 1])
            dxt, pre_acc, dm, gate_acc = _prenorm_bwd(dhx, sv["xt"], dxt, norm_w[l:l + 1], mods[l], n_lat, below,
                                                      f"prenorm_bwd_l{l}")
        else:
            dxt, pre_acc = _prenorm_bwd(dhx, sv["xt"], dxt, norm_w[l:l + 1], mods[l], n_lat, None, f"prenorm_bwd_l{l}")
        gate_acc_l = this_gate
        dmod_x.append(jnp.concatenate([pre_acc[0], pre_acc[1], gate_acc_l[2]]))
        dmod_c.append(jnp.concatenate([pre_acc[3], pre_acc[4], gate_acc_l[5]]))
        dnw.append(pre_acc[6])
        dcnw.append(norm_acc[0])
        dgnw.append(norm_acc[1])
        dconv.append(conv_acc[0:3])
        ddec.append(dlg[0:2, :n_heads])
    for lst in (dmod_x, dmod_c, dnw, dcnw, dgnw, dconv, ddec, dwin, dwout):
        lst.reverse()
    grad_x = dxt.reshape(1, n_lat, d)

    rows = []
    for l in range(depth):
        rows += [dmod_x[l], dmod_c[l]]
    (dmod_g,) = _all_gather([_pad_rows(jnp.stack(rows, axis=0), 8)], "gather_dmod", True)
    dmod_g = dmod_g.reshape(N_DEV, 8, 3 * d)
    mine_cols = lax.dynamic_slice(dmod_g, (0, 0, me * n_mod), (N_DEV, 8, n_mod))
    g_wmod, dcc = [], jnp.zeros((d,), F32)
    for l in range(depth):
        gw, dc_part = _mod_grads(mine_cols[:, 2 * l], mine_cols[:, 2 * l + 1], c9, w_mod[l], f"mod_grads_l{l}")
        g_wmod.append(gw)
        dcc = dcc + dc_part[0]

    n_small = 16
    small = jnp.concatenate([
        jnp.stack(dnw, axis=0),
        jnp.concatenate(dcnw).reshape(1, -1),
        jnp.concatenate(dgnw).reshape(1, -1),
        dfnw[0:1],
        dcc.reshape(1, d),
        jnp.stack(dconv, axis=0).reshape(-1, d),
        _pad_cols(jnp.stack(ddec, axis=0).reshape(1, -1), d),
    ], axis=0)
    assert depth * s == d and small.shape[0] <= n_small
    n_rows = small.shape[0]
    (small_g,) = _all_gather([_pad_rows(small, n_small)], "gather_small", True)
    small_g = small_g.reshape(N_DEV, n_small, d)

    def pack_small(nw_, cn_, gn_, fn_, cc_, df_, db_):
        return _pad_rows(jnp.concatenate([
            nw_, cn_.reshape(1, -1), gn_.reshape(1, -1), fn_.reshape(1, d), cc_.reshape(1, d),
            jnp.zeros((n_rows - depth - 5, d), F32),
            _pad_cols(jnp.stack([df_, db_], axis=1).reshape(1, -1), d)], axis=0), n_small)

    w_s = pack_small(norm_w, conv_norm_w, ret_norm_w, final_norm_w, c_ctx, ret_decay_f, ret_decay_b)
    m_s = pack_small(m_norm_w, m_conv_norm_w, m_ret_norm_w, m_final_norm_w, m_c_ctx, m_ret_decay_f, m_ret_decay_b)
    v_s = pack_small(v_norm_w, v_conv_norm_w, v_ret_norm_w, v_final_norm_w, v_c_ctx, v_ret_decay_f, v_ret_decay_b)
    small_out = _sum_adamw(small_g, w_s, m_s, v_s, "adamw_small")

    def unpack_small(a):
        nw_ = a[0:depth]
        cn_ = a[depth].reshape(depth, s)
        gn_ = a[depth + 1].reshape(depth, s)
        fn_ = a[depth + 2]
        cc_ = a[depth + 3]
        dd = a[n_rows - 1, :depth * 2 * n_heads].reshape(depth, 2, n_heads)
        return dict(c_ctx=cc_, norm_w=nw_, conv_norm_w=cn_, ret_norm_w=gn_, ret_decay_f=dd[:, 0], ret_decay_b=dd[:, 1],
                    final_norm_w=fn_)

    res = {}
    for kind, arr in zip(("grad", "delta", "m", "v"), small_out):
        for k_, val in unpack_small(arr).items():
            res[(kind, k_)] = val

    bm_parts = jnp.concatenate([dmod_g[:, 0:2 * depth:2].reshape(N_DEV, depth, 3 * d),
                                dmod_g[:, 1:2 * depth:2].reshape(N_DEV, depth, 3 * d)], axis=0)
    bm_parts = jnp.concatenate([bm_parts, jnp.zeros((2 * N_DEV, 8 - depth, 3 * d), F32)], axis=1)
    pad8 = lambda a: _pad_rows(a, 8)
    bm_out = _sum_adamw(bm_parts, pad8(b_mod), pad8(m_b_mod), pad8(v_b_mod), "adamw_b_mod")
    for kind, arr in zip(("grad", "delta", "m", "v"), bm_out):
        res[(kind, "b_mod")] = arr[:depth]

    conv_rows = small_g[:, depth + 4:depth + 4 + 3 * depth * s // d].reshape(N_DEV, depth * 3, s)
    conv_mine = lax.dynamic_slice(conv_rows, (0, 0, me * n_cw), (N_DEV, depth * 3, n_cw))
    conv_mine = jnp.concatenate([conv_mine, jnp.zeros((N_DEV, 8 - depth * 3, n_cw), F32)], axis=1)
    cw2 = lambda a: _pad_rows(a.reshape(depth * 3, n_cw), 8)
    cw_out = _sum_adamw(conv_mine, cw2(conv_w), cw2(m_conv_w), cw2(v_conv_w), "adamw_conv_w")
    for kind, arr in zip(("grad", "delta", "m", "v"), cw_out):
        res[(kind, "conv_w")] = arr[:depth * 3].reshape(depth, 3, n_cw)

    wm_out = _sum_adamw(jnp.stack(g_wmod, axis=0).reshape(1, depth * d, n_mod), w_mod.reshape(depth * d, n_mod),
                        m_w_mod.reshape(depth * d, n_mod), v_w_mod.reshape(depth * d, n_mod), "adamw_w_mod")
    for kind, arr in zip(("grad", "delta", "m", "v"), wm_out):
        res[(kind, "w_mod")] = arr.reshape(depth, d, n_mod)

    wi_out = wo_out = None
    after = wm_out[0]
    for l in reversed(range(depth)):
        win_parts, wout_parts = _push_wait(*dwin[l], True, after, f"grads_wait_l{l}")
        wi_out = _sum_adamw(win_parts, w_in.reshape(depth * d, s), m_w_in.reshape(depth * d, s),
                            v_w_in.reshape(depth * d, s), f"adamw_w_in_l{l}", row0=l * d, into=wi_out)
        wo_out = _sum_adamw(wout_parts, w_out.reshape(depth * r_out, d), m_w_out.reshape(depth * r_out, d),
                            v_w_out.reshape(depth * r_out, d), f"adamw_w_out_l{l}", row0=l * r_out, into=wo_out)
        after = wo_out[0]
    for kind, arr in zip(("grad", "delta", "m", "v"), wi_out):
        res[(kind, "w_in")] = arr.reshape(depth, d, s)
    for kind, arr in zip(("grad", "delta", "m", "v"), wo_out):
        res[(kind, "w_out")] = arr.reshape(depth, r_out, d)

    order = ["c_ctx", "norm_w", "w_mod", "b_mod", "w_in", "conv_w", "conv_norm_w", "ret_norm_w", "ret_decay_f",
             "ret_decay_b", "w_out", "final_norm_w"]
    outs = [loss, grad_x]
    for kind in ("grad", "delta", "m", "v"):
        outs += [res[(kind, k_)] for k_ in order]
    return tuple(outs)
```

```python
import functools

import jax
import jax.numpy as jnp
from jax import lax
from jax.experimental import pallas as pl
from jax.experimental.pallas import tpu as pltpu

F32 = jnp.float32
BF16 = jnp.bfloat16

EPS = 1e-6
CHUNK = 128
HEAD_DIM = 128
GRID_W = 64
ROPE_BASE = 10000.0
N_DEV = 8
ADAM_LR, ADAM_B1, ADAM_B2, ADAM_EPS, ADAM_WD, ADAM_STEP = 0.001, 0.9, 0.999, 1e-08, 0.01, 10

ROW_TILE = 256
V7X_VMEM_LIMIT = 56 * 1024 * 1024
MESH_AXES = ("x", "y", "c")

NN = ((1,), (0,))
NT = ((1,), (1,))
TN = ((0,), (0,))


def _dot(a, b, dims):
    return lax.dot_general(a, b, (dims, ((), ())), preferred_element_type=F32)


def _params(sem=None):
    if sem is None:
        return pltpu.CompilerParams(vmem_limit_bytes=V7X_VMEM_LIMIT)
    return pltpu.CompilerParams(dimension_semantics=sem, vmem_limit_bytes=V7X_VMEM_LIMIT)


def _silu(z):
    return z * jax.nn.sigmoid(z)


def _dsilu(z):
    s = jax.nn.sigmoid(z)
    return s * (1.0 + z * (1.0 - s))


def _sum_all(a):
    return jnp.sum(jnp.sum(a, axis=1, keepdims=True), axis=0, keepdims=True)


def _mm_rows(t):
    return 768 if t % 768 == 0 else ROW_TILE


def _rows_or(t, rows):
    return rows if t % rows == 0 else _mm_rows(t)


def _tiles(layer, t, d):
    tiles = dict(in_tm=_rows_or(t, 1408), bwd_tm=_rows_or(t, 1056), wg_bm=d, out_tm=_mm_rows(t), out_tn=min(d, 1024),
                 wo_bm=min(d, 1024), ob_tn=min(d, 1024))
    if layer > 0:
        tiles.update(wo_bm=d, ob_tn=d)
    return tiles


def _full(shape):
    n = len(shape)
    return pl.BlockSpec(shape, lambda *_: (0,) * n)


def _peers(x, y, c):
    return [(x, y, 1 - c), (1 - x, y, c), (x, 1 - y, c), (1 - x, 1 - y, c),
            (1 - x, y, 1 - c), (x, 1 - y, 1 - c), (1 - x, 1 - y, 1 - c)]


def _lin(p):
    return 4 * p[0] + 2 * p[1] + p[2]


def _all_gather(arrays, name, in_vmem):
    n_arr = len(arrays)
    space = pltpu.VMEM if in_vmem else pl.ANY

    def body(*refs):
        ins, outs = refs[:n_arr], refs[n_arr:2 * n_arr]
        send_sems, recv_sems, local_sems = refs[2 * n_arr:]
        x, y, c = lax.axis_index("x"), lax.axis_index("y"), lax.axis_index("c")
        me, sibling = (x, y, c), (x, y, 1 - c)
        chips = [(1 - x, y), (x, 1 - y), (1 - x, 1 - y)]
        every = []
        locals_ = []
        for a in range(n_arr):
            m_per = ins[a].shape[0]
            out_ref = outs[a]

            def rows(p, out_ref=out_ref, m_per=m_per):
                return out_ref.at[pl.ds(_lin(p) * m_per, m_per), :]

            def copy(k, block, to, src=None, a=a, rows=rows):
                return pltpu.make_async_remote_copy(
                    src_ref=rows(block) if src is None else src, dst_ref=rows(block),
                    send_sem=send_sems.at[a, k], recv_sem=recv_sems.at[a, k],
                    device_id=to, device_id_type=pl.DeviceIdType.MESH)

            mine = pltpu.make_async_copy(ins[a], rows(me), local_sems.at[a])
            mine.start()
            locals_.append(mine)
            first = [copy(0, me, sibling, src=ins[a])]
            first += [copy(1 + j, me, (*chip, c), src=ins[a]) for j, chip in enumerate(chips)]
            for cp in first:
                cp.start()
            every.append((copy, first))
        sends = []
        for a in range(n_arr):
            copy, first = every[a]
            passed = [copy(4 + j, (*chip, c), sibling) for j, chip in enumerate(chips)]
            for j, chip in enumerate(chips):
                copy(1 + j, (*chip, c), me).wait_recv()
                passed[j].start()
            sends += first + passed
        for a in range(n_arr):
            copy, _ = every[a]
            copy(0, sibling, me).wait_recv()
            for j, chip in enumerate(chips):
                copy(4 + j, (*chip, 1 - c), me).wait_recv()
        for cp in sends:
            cp.wait_send()
        for mine in locals_:
            mine.wait()

    outs = pl.pallas_call(
        body, name=name,
        out_shape=[jax.ShapeDtypeStruct((N_DEV * a.shape[0], a.shape[1]), a.dtype) for a in arrays],
        in_specs=[pl.BlockSpec(memory_space=space)] * n_arr,
        out_specs=[pl.BlockSpec(memory_space=space)] * n_arr,
        scratch_shapes=[pltpu.SemaphoreType.DMA((n_arr, 7)), pltpu.SemaphoreType.DMA((n_arr, 7)),
                        pltpu.SemaphoreType.DMA((n_arr,))],
        compiler_params=_params(),
    )(*arrays)
    return list(outs)


_HBM = pl.BlockSpec(memory_space=pltpu.HBM)
_SEM = pl.BlockSpec(memory_space=pltpu.SEMAPHORE)
_DATAFLOW = pltpu.SideEffectType.DATAFLOW_SIDE_EFFECTING


PUSH_COPIES = {"scatter": 7, "gather": 7, "near": 4, "relay": 3}


def _push_copies(src_refs, land_refs, send_sems, recv_sems, mode):
    x, y, c = lax.axis_index("x"), lax.axis_index("y"), lax.axis_index("c")
    me, sibling = (x, y, c), (x, y, 1 - c)
    n_k = PUSH_COPIES[mode]
    out, back = [], []
    if mode == "relay":
        for k, chip in enumerate([(1 - x, y), (x, 1 - y), (1 - x, 1 - y)]):
            for a, land in enumerate(land_refs):
                sems = dict(send_sem=send_sems.at[n_k * a + k], recv_sem=recv_sems.at[n_k * a + k],
                            device_id=sibling, device_id_type=pl.DeviceIdType.MESH)
                mine = land.at[_lin((*chip, c))]
                out.append(pltpu.make_async_remote_copy(src_ref=mine, dst_ref=mine, **sems))
                back.append(pltpu.make_async_remote_copy(src_ref=mine, dst_ref=land.at[_lin((*chip, 1 - c))], **sems))
        return out, back
    for k, peer in enumerate(_peers(x, y, c)[:n_k]):
        for a, (src, land) in enumerate(zip(src_refs, land_refs)):
            sems = dict(send_sem=send_sems.at[n_k * a + k], recv_sem=recv_sems.at[n_k * a + k],
                        device_id=peer, device_id_type=pl.DeviceIdType.MESH)
            mine = src.at[_lin(peer)] if mode == "scatter" else src
            out.append(pltpu.make_async_remote_copy(src_ref=mine, dst_ref=land.at[_lin(me)], **sems))
            back.append(pltpu.make_async_remote_copy(src_ref=mine, dst_ref=land.at[_lin(peer)], **sems))
    return out, back


def _push_start(srcs, lands, mode, name, after=()):
    n_src, n = len(srcs), len(lands)
    n_buf = n_src + n
    n_in = n_buf + len(after)
    n_sem = PUSH_COPIES[mode] * n

    def body(*refs):
        send_sems, recv_sems = refs[n_in], refs[n_in + 1]
        out, _ = _push_copies(refs[:n_src], refs[n_src:n_buf], send_sems, recv_sems, mode)
        for cp in out:
            cp.start()
        token = refs[-1]
        token[...] = jnp.zeros_like(token)

    both = list(srcs) + list(lands)
    res = pl.pallas_call(
        body, name=name,
        out_shape=[pltpu.SemaphoreType.DMA((n_sem,)), pltpu.SemaphoreType.DMA((n_sem,))]
        + [pltpu.HBM(a.shape, a.dtype) for a in both] + [jax.ShapeDtypeStruct((8, 128), F32)],
        in_specs=[_HBM] * n_buf + [pl.BlockSpec(memory_space=pl.ANY)] * len(after),
        out_specs=[_SEM, _SEM] + [_HBM] * n_buf + [pl.BlockSpec(memory_space=pltpu.VMEM)],
        input_output_aliases={i: 2 + i for i in range(n_buf)},
        compiler_params=pltpu.CompilerParams(has_side_effects=_DATAFLOW),
    )(*[pltpu.with_memory_space_constraint(a, pltpu.HBM) for a in both], *after)
    return res[0], res[1], list(res[2:2 + n_src]), list(res[2 + n_src:2 + n_buf]), res[-1]


def _push_wait(send_sems, recv_sems, srcs, lands, mode, after, name):
    n_src, n = len(srcs), len(lands)
    n_buf = n_src + n

    def body(*refs):
        out, back = _push_copies(refs[:n_src], refs[n_src:n_buf], refs[n_buf], refs[n_buf + 1], mode)
        for cp in out:
            cp.wait_send()
        for cp in back:
            cp.wait_recv()

    both = list(srcs) + list(lands)
    res = pl.pallas_call(
        body, name=name,
        out_shape=[pltpu.HBM(a.shape, a.dtype) for a in both],
        in_specs=[_HBM] * n_buf + [_SEM, _SEM, pl.BlockSpec(memory_space=pl.ANY)],
        out_specs=[_HBM] * n_buf,
        input_output_aliases={i: i for i in range(n_buf)},
        compiler_params=pltpu.CompilerParams(has_side_effects=_DATAFLOW),
    )(*both, send_sems, recv_sems, after)
    return list(res[n_src:])


def _landing(own, me):
    zone = lax.empty((N_DEV,) + own.shape, own.dtype)
    return lax.dynamic_update_slice(zone, own[None], (me,) + (0,) * own.ndim)


def _mod_rows(c9, w_mod, b_sh, name):
    n = w_mod.shape[1]

    def body(c_ref, w_ref, b_ref, o_ref):
        s9 = _silu(c_ref[...]).astype(BF16)
        o_ref[...] = _dot(s9, w_ref[...].astype(BF16), NN) + b_ref[...]

    return pl.pallas_call(body, name=name, out_shape=jax.ShapeDtypeStruct((16, n), F32),
                          compiler_params=_params())(c9, w_mod, b_sh)


def _mod_grads(dm_rows, dc_rows, c9, w_mod, name):
    d, n = w_mod.shape

    def body(dm_ref, dc_ref, c_ref, w_ref, gw_ref, dc_out):
        dc = dc_ref[...]
        tot = dc[0:1]
        for j in range(1, N_DEV):
            tot = tot + dc[j:j + 1]
        row = lax.broadcasted_iota(jnp.int32, (8, n), 0)
        lower = jnp.where(row == 0, tot, 0.0)
        dmod9 = jnp.concatenate([dm_ref[...], lower], axis=0).astype(BF16)
        c9v = c_ref[...]
        s9 = _silu(c9v).astype(BF16)
        gw_ref[...] = _dot(s9, dmod9, TN)
        ds = _dot(lower.astype(BF16), w_ref[...].astype(BF16), NT)
        dc_out[...] = ds * _dsilu(c9v[8:16])

    return pl.pallas_call(body, name=name,
                          out_shape=[jax.ShapeDtypeStruct((d, n), F32), jax.ShapeDtypeStruct((8, d), F32)],
                          compiler_params=_params())(dm_rows, dc_rows, c9, w_mod)


def _decay_tables(dec, n_heads, name):
    c = CHUNK

    def body(dec_ref, dc_ref, dlf_ref, dlb_ref, qf_ref, kf_ref, qb_ref, kb_ref, cdf_ref, cdb_ref, lg_ref):
        h = pl.program_id(0)
        d = dec_ref[...]
        lane = lax.broadcasted_iota(jnp.int32, d.shape, 1)
        lg = -jnp.exp(jnp.sum(jnp.where(lane == h, d, 0.0), axis=1, keepdims=True))
        lgf, lgb = lg[0:1], lg[1:2]
        i = lax.broadcasted_iota(jnp.int32, (c, c), 0).astype(F32)
        j = lax.broadcasted_iota(jnp.int32, (c, c), 1).astype(F32)
        diff = i - j
        d_f = jnp.where(diff >= 0, jnp.exp(lgf * jnp.maximum(diff, 0.0)), 0.0)
        d_b = jnp.where(diff <= 0, jnp.exp(lgb * jnp.maximum(-diff, 0.0)), 0.0)
        dc_ref[...] = d_f + d_b
        dlf_ref[...] = diff * d_f
        dlb_ref[...] = -diff * d_b
        pos = lax.broadcasted_iota(jnp.int32, (c, HEAD_DIM), 0).astype(F32)
        qf_ref[...] = jnp.exp(lgf * (pos + 1.0))
        kf_ref[...] = jnp.exp(lgf * (c - 1.0 - pos))
        qb_ref[...] = jnp.exp(lgb * (c - pos))
        kb_ref[...] = jnp.exp(lgb * pos)
        ones = jnp.ones((8, HEAD_DIM), F32)
        cdf_ref[...] = jnp.exp(lgf * float(c)) * ones
        cdb_ref[...] = jnp.exp(lgb * float(c)) * ones

        @pl.when(h == 0)
        def _():
            lg_ref[...] = jnp.zeros_like(lg_ref)

        row8 = lax.broadcasted_iota(jnp.int32, (8, HEAD_DIM), 0)
        lane8 = lax.broadcasted_iota(jnp.int32, (8, HEAD_DIM), 1)
        lg_ref[...] += (jnp.where((row8 == 0) & (lane8 == h), lgf, 0.0)
                        + jnp.where((row8 == 1) & (lane8 == h), lgb, 0.0))

    def per_head(*tail):
        return pl.BlockSpec((None,) + tail, lambda h: (h,) + (0,) * len(tail))

    shapes = [(c, c)] * 3 + [(c, HEAD_DIM)] * 4 + [(8, HEAD_DIM)] * 2
    return pl.pallas_call(
        body, name=name, grid=(n_heads,),
        in_specs=[_full(dec.shape)],
        out_specs=[per_head(*s) for s in shapes] + [_full((8, HEAD_DIM))],
        out_shape=[jax.ShapeDtypeStruct((n_heads,) + s, F32) for s in shapes]
        + [jax.ShapeDtypeStruct((8, HEAD_DIM), F32)],
        compiler_params=_params(("arbitrary",)),
    )(dec)


def _modulate(x, nw, shift, scale):
    r = lax.rsqrt(jnp.mean(x * x, axis=-1, keepdims=True) + EPS)
    return ((x * r) * nw * (1.0 + scale) + shift).astype(BF16)


def _prenorm(xt, nw, mod, n_lat, name):
    t, d = xt.shape
    nxb = n_lat // ROW_TILE

    def body(x_ref, nw_ref, mod_ref, o_ref):
        ctx = pl.program_id(0) >= nxb
        m = mod_ref[...]
        o_ref[...] = _modulate(x_ref[...], nw_ref[...], jnp.where(ctx, m[3:4], m[0:1]), jnp.where(ctx, m[4:5], m[1:2]))

    row = pl.BlockSpec((ROW_TILE, d), lambda i: (i, 0))
    return pl.pallas_call(body, name=name, grid=(t // ROW_TILE,),
                          in_specs=[row, _full((1, d)), _full((8, d))],
                          out_specs=row, out_shape=jax.ShapeDtypeStruct((t, d), BF16),
                          compiler_params=_params(("parallel",)))(xt, nw, mod)


def _prenorm_first(x, ctx, nw, mod, name, after=()):
    n_lat, d = x.shape
    t = n_lat + ctx.shape[0]
    nxb = n_lat // ROW_TILE

    def body(x_ref, c_ref, nw_ref, mod_ref, *rest):
        o_ref, xt_ref = rest[-2:]
        m = mod_ref[...]
        nw_v = nw_ref[...]

        @pl.when(pl.program_id(0) < nxb)
        def _():
            xv = x_ref[...]
            xt_ref[...] = xv
            o_ref[...] = _modulate(xv, nw_v, m[0:1], m[1:2])

        @pl.when(pl.program_id(0) >= nxb)
        def _():
            xv = c_ref[...]
            xt_ref[...] = xv
            o_ref[...] = _modulate(xv, nw_v, m[3:4], m[4:5])

    row = pl.BlockSpec((ROW_TILE, d), lambda i: (i, 0))
    return pl.pallas_call(
        body, name=name, grid=(t // ROW_TILE,),
        in_specs=[pl.BlockSpec((ROW_TILE, d), lambda i: (jnp.minimum(i, nxb - 1), 0)),
                  pl.BlockSpec((ROW_TILE, d), lambda i: (jnp.maximum(i - nxb, 0), 0)), _full((1, d)), _full((8, d))]
        + [pl.BlockSpec(memory_space=pl.ANY)] * len(after),
        out_specs=[row, row], out_shape=[jax.ShapeDtypeStruct((t, d), BF16), jax.ShapeDtypeStruct((t, d), F32)],
        compiler_params=_params(("parallel",)))(x, ctx, nw, mod, *after)


def _rope_fwd(v, cos, sa, sb):
    return v * cos + pltpu.roll(v, 96, 1) * sa + pltpu.roll(v, 32, 1) * sb


def _rope_bwd(g, cos, sa, sb):
    return g * cos + pltpu.roll(g * sa, 32, 1) + pltpu.roll(g * sb, 96, 1)


N_PLAIN = 5


def _in_proj(hx, wg, cos, sa, sb, n_heads, tm, name):
    t, d = hx.shape
    n_seg, _, s = wg.shape
    nb = t // tm
    k_scale = HEAD_DIM ** -0.5

    def body(a_ref, w_ref, cos_ref, sa_ref, sb_ref, u_ref, qkv_ref):
        g = pl.program_id(0)
        acc = _dot(a_ref[...], w_ref[...], NN)

        @pl.when(g < N_PLAIN)
        def _():
            u_ref[...] = acc

        @pl.when(g == N_PLAIN + 2)
        def _():
            qkv_ref[...] = acc.astype(BF16)

        for which, scale in ((N_PLAIN, 1.0), (N_PLAIN + 1, k_scale)):
            @pl.when(g == which)
            def _(scale=scale):
                co, a, b = cos_ref[...], sa_ref[...], sb_ref[...]
                for h in range(n_heads):
                    sl = slice(h * HEAD_DIM, (h + 1) * HEAD_DIM)
                    qkv_ref[:, sl] = (_rope_fwd(acc[:, sl], co, a, b) * scale).astype(BF16)

    def w_seg(g):
        return jnp.where(g < N_PLAIN - 1, g, jnp.where(g == N_PLAIN - 1, n_seg - 1, g - 1))

    tab = pl.BlockSpec((tm, HEAD_DIM), lambda g, i: (i, 0))
    return pl.pallas_call(
        body, name=name, grid=(n_seg, nb),
        in_specs=[pl.BlockSpec((tm, d), lambda g, i: (i, 0)), pl.BlockSpec((None, d, s), lambda g, i: (w_seg(g), 0, 0)),
                  tab, tab, tab],
        out_specs=[pl.BlockSpec((None, tm, s), lambda g, i: (jnp.minimum(g, N_PLAIN - 1), jnp.where(g < N_PLAIN, i, nb - 1), 0)),
                   pl.BlockSpec((None, tm, s), lambda g, i: (jnp.maximum(g - N_PLAIN, 0), jnp.where(g < N_PLAIN, 0, i), 0))],
        out_shape=[jax.ShapeDtypeStruct((N_PLAIN, t, s), F32), jax.ShapeDtypeStruct((3, t, s), BF16)],
        compiler_params=_params(("arbitrary", "arbitrary")))(hx, wg, cos, sa, sb)


def _f_order(step, nx, ncc):
    return jnp.where(step < ncc, nx + step, step - ncc)


def _state_sweep(qkv, tabs, n_heads, nx, ncc, name):
    _, t, s = qkv.shape
    nc = nx + ncc
    c = CHUNK

    def body(kf_ref, vf_ref, kb_ref, vb_ref, kft, kbt, cdf, cdb, sf_out, sb_out, sf, sb):
        @pl.when(pl.program_id(0) == 0)
        def _():
            sf[...] = jnp.zeros_like(sf)
            sb[...] = jnp.zeros_like(sb)

        for h in range(n_heads):
            sl = pl.ds(h * HEAD_DIM, HEAD_DIM)
            sf_out[h] = sf[h].astype(BF16)
            sb_out[h] = sb[h].astype(BF16)
            kd = (kf_ref[:, sl].astype(F32) * kft[h]).astype(BF16)
            sf[h] = cdf[h][0:1, :] * sf[h] + _dot(kd, vf_ref[:, sl], TN)
            kd = (kb_ref[:, sl].astype(F32) * kbt[h]).astype(BF16)
            sb[h] = cdb[h][0:1, :] * sb[h] + _dot(kd, vb_ref[:, sl], TN)

    def fwd(j):
        return pl.BlockSpec((None, c, s), lambda i: (j, _f_order(i, nx, ncc), 0))

    def bwd(j):
        return pl.BlockSpec((None, c, s), lambda i: (j, nc - 1 - i, 0))

    st = (None, n_heads, HEAD_DIM, HEAD_DIM)
    return pl.pallas_call(
        body, name=name, grid=(nc,),
        in_specs=[fwd(1), fwd(2), bwd(1), bwd(2), _full((n_heads, c, HEAD_DIM)), _full((n_heads, c, HEAD_DIM)),
                  _full((n_heads, 8, HEAD_DIM)), _full((n_heads, 8, HEAD_DIM))],
        out_specs=[pl.BlockSpec(st, lambda i: (_f_order(i, nx, ncc), 0, 0, 0)),
                   pl.BlockSpec(st, lambda i: (nc - 1 - i, 0, 0, 0))],
        out_shape=[jax.ShapeDtypeStruct((nc, n_heads, HEAD_DIM, HEAD_DIM), BF16)] * 2,
        scratch_shapes=[pltpu.VMEM((n_heads, HEAD_DIM, HEAD_DIM), F32)] * 2,
        compiler_params=_params(("arbitrary",)),
    )(qkv, qkv, qkv, qkv, tabs["kf"], tabs["kb"], tabs["cdf"], tabs["cdb"])


def _halo_specs(s, n8):
    per = CHUNK // 8

    def prev(g):
        return pl.BlockSpec((None, 8, s), lambda i: (g, jnp.maximum(i * per - 1, 0), 0))

    def nxt(g):
        return pl.BlockSpec((None, 8, s), lambda i: (g, jnp.minimum((i + 1) * per, n8 - 1), 0))

    return prev, nxt


def _shifted(a, before, after, has_prev, has_next):
    rows = a.shape[0]
    rowi = lax.broadcasted_iota(jnp.int32, a.shape, 0)
    am = jnp.where(rowi == 0, jnp.where(has_prev, before, 0.0), pltpu.roll(a, 1, 0))
    ap = jnp.where(rowi == rows - 1, jnp.where(has_next, after, 0.0), pltpu.roll(a, rows - 1, 0))
    return am, ap


def _neighbours(i, nx, nc):
    return (i != 0) & (i != nx), (i != nx - 1) & (i != nc - 1)


def _mix_fwd(u, qkv, sf, sb, tabs, conv_w, cnw, gnw, n_heads, nx, ncc, name):
    _, t, s = u.shape
    nc = nx + ncc
    c = CHUNK

    def body(h_ref, b_ref, c_ref, z_ref, rz_ref, hp_ref, hn_ref, cp_ref, cn_ref, q_ref, k_ref, v_ref,
             sf_ref, sb_ref, dc_ref, qft, qbt, w_ref, cnw_ref, gnw_ref, y_ref, o_ref):
        i = pl.program_id(0)
        has_prev, has_next = _neighbours(i, nx, nc)
        a = c_ref[...] * h_ref[...]
        am, ap = _shifted(a, cp_ref[7:8] * hp_ref[7:8], cn_ref[0:1] * hn_ref[0:1], has_prev, has_next)
        w = w_ref[...]
        y0 = w[0:1] * am + w[1:2] * a + w[2:3] * ap
        yb = b_ref[...] * y0
        r = lax.rsqrt(jnp.mean(yb * yb, axis=-1, keepdims=True) + EPS)
        y_ref[:, pl.ds(0, s)] = (_silu(z_ref[...]) * ((yb * r) * cnw_ref[...])).astype(BF16)
        for h in range(n_heads):
            sl = pl.ds(h * HEAD_DIM, HEAD_DIM)
            q, k, v = q_ref[:, sl], k_ref[:, sl], v_ref[:, sl]
            p = (_dot(q, k, NT) * dc_ref[h]).astype(BF16)
            o = _dot(p, v, NN)
            qf = q.astype(F32)
            o += _dot((qf * qft[h]).astype(BF16), sf_ref[h], NN)
            o += _dot((qf * qbt[h]).astype(BF16), sb_ref[h], NN)
            o_ref[:, sl] = o
            mu = jnp.mean(o, axis=-1, keepdims=True)
            var = jnp.mean(jnp.square(o - mu), axis=-1, keepdims=True)
            on = (o - mu) * lax.rsqrt(var + EPS)
            y_ref[:, pl.ds(s + h * HEAD_DIM, HEAD_DIM)] = (
                _silu(rz_ref[:, sl]) * (on * gnw_ref[:, sl])).astype(BF16)

    def seg(g):
        return pl.BlockSpec((None, c, s), lambda i: (g, i, 0))

    prev, nxt = _halo_specs(s, t // 8)
    row = pl.BlockSpec((c, s), lambda i: (i, 0))
    st = pl.BlockSpec((None, n_heads, HEAD_DIM, HEAD_DIM), lambda i: (i, 0, 0, 0))
    return pl.pallas_call(
        body, name=name, grid=(nc,),
        in_specs=[seg(0), seg(1), seg(2), seg(3), seg(4), prev(0), nxt(0), prev(2), nxt(2), seg(0), seg(1), seg(2),
                  st, st, _full((n_heads, c, c)), _full((n_heads, c, HEAD_DIM)), _full((n_heads, c, HEAD_DIM)),
                  _full((3, s)), _full((1, s)), _full((1, s))],
        out_specs=[pl.BlockSpec((c, 2 * s), lambda i: (i, 0)), row],
        out_shape=[jax.ShapeDtypeStruct((t, 2 * s), BF16), jax.ShapeDtypeStruct((t, s), F32)],
        compiler_params=_params(("parallel",)),
    )(u, u, u, u, u, u, u, u, u, qkv, qkv, qkv, sf, sb, tabs["dc"], tabs["qf"], tabs["qb"], conv_w, cnw, gnw)


def _row_gate(mod_ref, row0, rows, n_lat, col):
    rowi = row0 + lax.broadcasted_iota(jnp.int32, (rows, 1), 0)
    return jnp.where(rowi >= n_lat, mod_ref[5:6, col], mod_ref[2:3, col])


def _out_proj(ycat, w_out, xt, mod, n_lat, tm, tn, name):
    t, d = xt.shape

    def body(a_ref, w_ref, x_ref, mod_ref, m_ref, xo_ref):
        m = _dot(a_ref[...], w_ref[...], NN)
        m_ref[...] = m
        gate = _row_gate(mod_ref, pl.program_id(1) * tm, tm, n_lat, slice(None))
        xo_ref[...] = x_ref[...] + gate * m

    blk = pl.BlockSpec((tm, tn), lambda j, i: (i, j))
    return pl.pallas_call(
        body, name=name, grid=(d // tn, t // tm),
        in_specs=[pl.BlockSpec((tm, d), lambda j, i: (i, 0)), pl.BlockSpec((d, tn), lambda j, i: (0, j)), blk,
                  pl.BlockSpec((8, tn), lambda j, i: (0, j))],
        out_specs=[blk, blk], out_shape=[jax.ShapeDtypeStruct((t, d), F32)] * 2,
        compiler_params=_params(("parallel", "parallel")))(ycat, w_out, xt, mod)


def _gate_bwd_rows(dx, m_ref, mod_ref, dm_ref, gacc_ref, ctx):
    g_row = 5 if ctx else 2
    dm_ref[...] = (dx * mod_ref[g_row:g_row + 1, :]).astype(BF16)
    gacc_ref[g_row:g_row + 1, :] += jnp.sum(dx * m_ref[...], axis=0, keepdims=True)


def _loss_head(x2, tgt, fnw, m, mod, n_lat, name):
    t, d = x2.shape
    nxb = n_lat // ROW_TILE

    def body(x_ref, t_ref, w_ref, m_ref, mod_ref, dx_ref, dm_ref, loss_ref, dw_ref, gacc_ref):
        i = pl.program_id(0)

        @pl.when(i == 0)
        def _():
            loss_ref[...] = jnp.zeros_like(loss_ref)
            dw_ref[...] = jnp.zeros_like(dw_ref)
            gacc_ref[...] = jnp.zeros_like(gacc_ref)

        @pl.when(i < nxb)
        def _():
            x = x_ref[...]
            w = w_ref[...]
            r = lax.rsqrt(jnp.mean(x * x, axis=-1, keepdims=True) + EPS)
            xn = x * r
            e = xn * w - t_ref[...]
            loss_ref[...] += 0.5 * jnp.sum(jnp.mean(e * e, axis=-1, keepdims=True), axis=0, keepdims=True)
            dy = e * (1.0 / d)
            dw_ref[0:1, :] += jnp.sum(dy * xn, axis=0, keepdims=True)
            dxn = dy * w
            dx = r * (dxn - xn * jnp.mean(dxn * xn, axis=-1, keepdims=True))
            dx_ref[...] = dx
            _gate_bwd_rows(dx, m_ref, mod_ref, dm_ref, gacc_ref, False)

        @pl.when(i >= nxb)
        def _():
            dx_ref[...] = jnp.zeros_like(dx_ref)
            dm_ref[...] = jnp.zeros_like(dm_ref)

    row = pl.BlockSpec((ROW_TILE, d), lambda i: (i, 0))
    return pl.pallas_call(
        body, name=name, grid=(t // ROW_TILE,),
        in_specs=[row, pl.BlockSpec((ROW_TILE, d), lambda i: (jnp.minimum(i, nxb - 1), 0)), _full((1, d)), row,
                  _full((8, d))],
        out_specs=[row, row, _full((8, HEAD_DIM)), _full((8, d)), _full((8, d))],
        out_shape=[jax.ShapeDtypeStruct((t, d), F32), jax.ShapeDtypeStruct((t, d), BF16),
                   jax.ShapeDtypeStruct((8, HEAD_DIM), F32), jax.ShapeDtypeStruct((8, d), F32),
                   jax.ShapeDtypeStruct((8, d), F32)],
        compiler_params=_params(("arbitrary",)))(x2, tgt, fnw, m, mod)


def _matmul_nt(a, w, tn, name):
    t, k = a.shape
    n = w.shape[0]
    tm = _mm_rows(t)

    def body(a_ref, w_ref, o_ref):
        o_ref[...] = _dot(a_ref[...], w_ref[...], NT)

    return pl.pallas_call(
        body, name=name, grid=(n // tn, t // tm),
        in_specs=[pl.BlockSpec((tm, k), lambda j, i: (i, 0)), pl.BlockSpec((tn, k), lambda j, i: (j, 0))],
        out_specs=pl.BlockSpec((tm, tn), lambda j, i: (i, j)),
        out_shape=jax.ShapeDtypeStruct((t, n), F32),
        compiler_params=_params(("parallel", "parallel")))(a, w)


def _weight_grad(a, b, bm, name):
    t, m = a.shape
    n_g, _, n = b.shape
    bt = _mm_rows(t)
    nt = t // bt

    def body(a_ref, b_ref, o_ref, acc):
        k = pl.program_id(2)

        @pl.when(k == 0)
        def _():
            acc[...] = jnp.zeros_like(acc)

        acc[...] += _dot(a_ref[...], b_ref[...], TN)

        @pl.when(k == nt - 1)
        def _():
            o_ref[...] = acc[...].astype(o_ref.dtype)

    return pl.pallas_call(
        body, name=name, grid=(n_g, m // bm, nt),
        in_specs=[pl.BlockSpec((bt, bm), lambda g, i, k: (k, i)), pl.BlockSpec((None, bt, n), lambda g, i, k: (g, k, 0))],
        out_specs=pl.BlockSpec((None, bm, n), lambda g, i, k: (g, i, 0)),
        out_shape=jax.ShapeDtypeStruct((n_g, m, n), BF16),
        scratch_shapes=[pltpu.VMEM((bm, n), F32)],
        compiler_params=_params(("parallel", "parallel", "arbitrary")))(a, b)


def _mix_bwd_a(dycat, u, o, conv_w, cnw, gnw, n_heads, nx, ncc, name):
    _, t, s = u.shape
    nc = nx + ncc
    c = CHUNK

    def body(dy_ref, h_ref, b_ref, c_ref, z_ref, rz_ref, hp_ref, hn_ref, cp_ref, cn_ref, o_ref, w_ref,
             cnw_ref, gnw_ref, g_ref, dz_ref, db_ref, drz_ref, do_ref, acc_ref):
        i = pl.program_id(0)

        @pl.when(i == 0)
        def _():
            acc_ref[...] = jnp.zeros_like(acc_ref)

        has_prev, has_next = _neighbours(i, nx, nc)
        a = c_ref[...] * h_ref[...]
        am, ap = _shifted(a, cp_ref[7:8] * hp_ref[7:8], cn_ref[0:1] * hn_ref[0:1], has_prev, has_next)
        w = w_ref[...]
        y0 = w[0:1] * am + w[1:2] * a + w[2:3] * ap
        bb = b_ref[...]
        yb = bb * y0
        r = lax.rsqrt(jnp.mean(yb * yb, axis=-1, keepdims=True) + EPS)
        ynn = yb * r
        z = z_ref[...]
        dyc = dy_ref[:, pl.ds(0, s)]
        cw = cnw_ref[...]
        dz_ref[...] = (dyc * (ynn * cw) * _dsilu(z)).astype(BF16)
        dyn = dyc * _silu(z)
        acc_ref[0:1, :] += jnp.sum(dyn * ynn, axis=0, keepdims=True)
        dynn = dyn * cw
        dyb = r * (dynn - ynn * jnp.mean(dynn * ynn, axis=-1, keepdims=True))
        db_ref[...] = (dyb * y0).astype(BF16)
        g_ref[...] = dyb * bb
        for h in range(n_heads):
            sl = pl.ds(h * HEAD_DIM, HEAD_DIM)
            ov = o_ref[:, sl]
            mu = jnp.mean(ov, axis=-1, keepdims=True)
            var = jnp.mean(jnp.square(ov - mu), axis=-1, keepdims=True)
            rs = lax.rsqrt(var + EPS)
            on = (ov - mu) * rs
            dyr = dy_ref[:, pl.ds(s + h * HEAD_DIM, HEAD_DIM)]
            rz = rz_ref[:, sl]
            gw = gnw_ref[:, sl]
            drz_ref[:, sl] = (dyr * (on * gw) * _dsilu(rz)).astype(BF16)
            dyg = dyr * _silu(rz)
            acc_ref[1:2, sl] += jnp.sum(dyg * on, axis=0, keepdims=True)
            don = dyg * gw
            do = rs * (don - jnp.mean(don, axis=-1, keepdims=True)
                       - on * jnp.mean(don * on, axis=-1, keepdims=True))
            do_ref[:, sl] = do.astype(BF16)

    def seg(g):
        return pl.BlockSpec((None, c, s), lambda i: (g, i, 0))

    prev, nxt = _halo_specs(s, t // 8)
    row = pl.BlockSpec((c, s), lambda i: (i, 0))
    return pl.pallas_call(
        body, name=name, grid=(nc,),
        in_specs=[pl.BlockSpec((c, 2 * s), lambda i: (i, 0)), seg(0), seg(1), seg(2), seg(3), seg(4),
                  prev(0), nxt(0), prev(2), nxt(2), row, _full((3, s)), _full((1, s)), _full((1, s))],
        out_specs=[row, row, row, row, row, _full((8, s))],
        out_shape=[jax.ShapeDtypeStruct((t, s), F32)] + [jax.ShapeDtypeStruct((t, s), BF16)] * 4
        + [jax.ShapeDtypeStruct((8, s), F32)],
        compiler_params=_params(("arbitrary",)),
    )(dycat, u, u, u, u, u, u, u, u, u, o, conv_w, cnw, gnw)


def _grad_state_sweep(qkv, do, tabs, n_heads, nx, ncc, name):
    t, s = do.shape
    nc = nx + ncc
    c = CHUNK

    def body(qf_ref, df_ref, qb_ref, db_ref, qft, qbt, cdf, cdb, gf_out, gb_out, gf, gb):
        @pl.when(pl.program_id(0) == 0)
        def _():
            gf[...] = jnp.zeros_like(gf)
            gb[...] = jnp.zeros_like(gb)

        for h in range(n_heads):
            sl = pl.ds(h * HEAD_DIM, HEAD_DIM)
            gf_out[h] = gf[h].astype(BF16)
            gb_out[h] = gb[h].astype(BF16)
            qd = (qf_ref[:, sl].astype(F32) * qft[h]).astype(BF16)
            gf[h] = cdf[h][0:1, :] * gf[h] + _dot(qd, df_ref[:, sl], TN)
            qd = (qb_ref[:, sl].astype(F32) * qbt[h]).astype(BF16)
            gb[h] = cdb[h][0:1, :] * gb[h] + _dot(qd, db_ref[:, sl], TN)

    fwd = pl.BlockSpec((c, s), lambda i: (_f_order(nc - 1 - i, nx, ncc), 0))
    bwd = pl.BlockSpec((c, s), lambda i: (i, 0))
    q_fwd = pl.BlockSpec((None, c, s), lambda i: (0, _f_order(nc - 1 - i, nx, ncc), 0))
    q_bwd = pl.BlockSpec((None, c, s), lambda i: (0, i, 0))
    st = (None, n_heads, HEAD_DIM, HEAD_DIM)
    return pl.pallas_call(
        body, name=name, grid=(nc,),
        in_specs=[q_fwd, fwd, q_bwd, bwd, _full((n_heads, c, HEAD_DIM)), _full((n_heads, c, HEAD_DIM)),
                  _full((n_heads, 8, HEAD_DIM)), _full((n_heads, 8, HEAD_DIM))],
        out_specs=[pl.BlockSpec(st, lambda i: (_f_order(nc - 1 - i, nx, ncc), 0, 0, 0)),
                   pl.BlockSpec(st, lambda i: (i, 0, 0, 0))],
        out_shape=[jax.ShapeDtypeStruct((nc, n_heads, HEAD_DIM, HEAD_DIM), BF16)] * 2,
        scratch_shapes=[pltpu.VMEM((n_heads, HEAD_DIM, HEAD_DIM), F32)] * 2,
        compiler_params=_params(("arbitrary",)),
    )(qkv, do, qkv, do, tabs["qf"], tabs["qb"], tabs["cdf"], tabs["cdb"])


def _mix_bwd_b(u, g, dz, db, drz, qkv, do, sf, sb, gf, gb, tabs, cos, sa, sb_tab, conv_w,
               n_heads, nx, ncc, name):
    _, t, s = u.shape
    nc = nx + ncc
    c = CHUNK
    k_scale = HEAD_DIM ** -0.5

    def body(h_ref, c_ref, g_ref, gp_ref, gn_ref, dz_ref, db_ref, drz_ref, q_ref, k_ref, v_ref, do_ref,
             sf_ref, sb_ref, gf_ref, gb_ref, dc_t, dlf_t, dlb_t, qft, kft, qbt, kbt, cdf, cdb, lg_ref,
             cos_ref, sa_ref, sb_ref2, w_ref, du_ref, dw_ref, dlg_ref):
        i = pl.program_id(0)

        @pl.when(i == 0)
        def _():
            dw_ref[...] = jnp.zeros_like(dw_ref)
            dlg_ref[...] = jnp.zeros_like(dlg_ref)

        has_prev, has_next = _neighbours(i, nx, nc)
        gv = g_ref[...]
        gm, gp = _shifted(gv, gp_ref[7:8], gn_ref[0:1], has_prev, has_next)
        w = w_ref[...]
        da = w[0:1] * gp + w[1:2] * gv + w[2:3] * gm
        hh, cc = h_ref[...], c_ref[...]
        du_ref[0] = (da * cc).astype(BF16)
        du_ref[2] = (da * hh).astype(BF16)
        a = cc * hh
        dw_ref[0:1, :] += jnp.sum(a * gp, axis=0, keepdims=True)
        dw_ref[1:2, :] += jnp.sum(a * gv, axis=0, keepdims=True)
        dw_ref[2:3, :] += jnp.sum(a * gm, axis=0, keepdims=True)
        du_ref[1] = db_ref[...]
        du_ref[3] = dz_ref[...]
        du_ref[7] = drz_ref[...]

        co, ra, rb = cos_ref[...], sa_ref[...], sb_ref2[...]
        pos = lax.broadcasted_iota(jnp.int32, (c, HEAD_DIM), 0).astype(F32)
        row8 = lax.broadcasted_iota(jnp.int32, (8, HEAD_DIM), 0)
        lane8 = lax.broadcasted_iota(jnp.int32, (8, HEAD_DIM), 1)
        dlg = jnp.zeros((8, HEAD_DIM), F32)
        for h in range(n_heads):
            sl = pl.ds(h * HEAD_DIM, HEAD_DIM)
            q, k, v, do = q_ref[:, sl], k_ref[:, sl], v_ref[:, sl], do_ref[:, sl]
            qf, kf, dof = q.astype(F32), k.astype(F32), do.astype(F32)
            s_f, s_b, g_f, g_b = sf_ref[h], sb_ref[h], gf_ref[h], gb_ref[h]
            p = _dot(q, k, NT)
            pd = _dot(do, v, NT)
            pdd = (pd * dc_t[h]).astype(BF16)
            dq = _dot(pdd, k, NN)
            dk = _dot(pdd, q, TN)
            dv = _dot((p * dc_t[h]).astype(BF16), do, TN)
            dq_f = _dot((dof * qft[h]).astype(BF16), s_f, NT)
            dq_b = _dot((dof * qbt[h]).astype(BF16), s_b, NT)
            dk_f = _dot(v, g_f, NT) * kft[h]
            dk_b = _dot(v, g_b, NT) * kbt[h]
            dv += _dot((kf * kft[h]).astype(BF16), g_f, NN) + _dot((kf * kbt[h]).astype(BF16), g_b, NN)
            ppd = p * pd
            cd_f, cd_b = cdf[h][0:1, :], cdb[h][0:1, :]
            t_f = _sum_all(dlf_t[h] * ppd + (pos + 1.0) * qf * dq_f + (c - 1.0 - pos) * kf * dk_f
                           + float(c) * (cd_f * (g_f.astype(F32) * s_f.astype(F32))))
            t_b = _sum_all(dlb_t[h] * ppd + (c - pos) * qf * dq_b + pos * kf * dk_b
                           + float(c) * (cd_b * (g_b.astype(F32) * s_b.astype(F32))))
            dlg += jnp.where((row8 == 0) & (lane8 == h), t_f, 0.0) + jnp.where((row8 == 1) & (lane8 == h), t_b, 0.0)
            du_ref[4, :, sl] = _rope_bwd(dq + dq_f + dq_b, co, ra, rb).astype(BF16)
            du_ref[5, :, sl] = (_rope_bwd(dk + dk_f + dk_b, co, ra, rb) * k_scale).astype(BF16)
            du_ref[6, :, sl] = dv.astype(BF16)
        dlg_ref[...] += dlg

        @pl.when(i == nc - 1)
        def _():
            dlg_ref[...] = dlg_ref[...] * lg_ref[...]

    def seg(gi):
        return pl.BlockSpec((None, c, s), lambda i: (gi, i, 0))

    per = c // 8
    n8 = t // 8
    row = pl.BlockSpec((c, s), lambda i: (i, 0))
    st = pl.BlockSpec((None, n_heads, HEAD_DIM, HEAD_DIM), lambda i: (i, 0, 0, 0))
    tab = pl.BlockSpec((c, HEAD_DIM), lambda i: (i, 0))
    hc = _full((n_heads, c, HEAD_DIM))
    cc_ = _full((n_heads, c, c))
    h8 = _full((n_heads, 8, HEAD_DIM))
    return pl.pallas_call(
        body, name=name, grid=(nc,),
        in_specs=[seg(0), seg(2), row,
                  pl.BlockSpec((8, s), lambda i: (jnp.maximum(i * per - 1, 0), 0)),
                  pl.BlockSpec((8, s), lambda i: (jnp.minimum((i + 1) * per, n8 - 1), 0)),
                  row, row, row, seg(0), seg(1), seg(2), row, st, st, st, st, cc_, cc_, cc_, hc, hc, hc, hc, h8, h8,
                  _full((8, HEAD_DIM)), tab, tab, tab, _full((3, s))],
        out_specs=[pl.BlockSpec((8, c, s), lambda i: (0, i, 0)), _full((8, s)), _full((8, HEAD_DIM))],
        out_shape=[jax.ShapeDtypeStruct((8, t, s), BF16), jax.ShapeDtypeStruct((8, s), F32),
                   jax.ShapeDtypeStruct((8, HEAD_DIM), F32)],
        compiler_params=_params(("arbitrary",)),
    )(u, u, g, g, g, dz, db, drz, qkv, qkv, qkv, do, sf, sb, gf, gb, tabs["dc"], tabs["dlf"], tabs["dlb"],
      tabs["qf"], tabs["kf"], tabs["qb"], tabs["kb"], tabs["cdf"], tabs["cdb"], tabs["lg"], cos, sa, sb_tab, conv_w)


def _in_proj_bwd(du, wg, tm, name, after=()):
    n_seg, t, s = du.shape
    d = wg.shape[1]

    def body(a_ref, w_ref, *rest):
        o_ref = rest[-1]
        g = pl.program_id(1)
        part = _dot(a_ref[...], w_ref[...], NT)

        @pl.when(g == 0)
        def _():
            o_ref[...] = part

        @pl.when(g > 0)
        def _():
            o_ref[...] += part

    return pl.pallas_call(
        body, name=name, grid=(t // tm, n_seg),
        in_specs=[pl.BlockSpec((None, tm, s), lambda i, g: (g, i, 0)), pl.BlockSpec((None, d, s), lambda i, g: (g, 0, 0))]
        + [pl.BlockSpec(memory_space=pl.ANY)] * len(after),
        out_specs=pl.BlockSpec((tm, d), lambda i, g: (i, 0)),
        out_shape=jax.ShapeDtypeStruct((t, d), F32),
        compiler_params=_params(("parallel", "arbitrary")))(du, wg, *after)


def _prenorm_bwd(dhx, xt, dxo, nw, mod, n_lat, below, name):
    t, d = xt.shape
    nxb = n_lat // ROW_TILE
    first = below is None

    def body(dh_ref, x_ref, dxo_ref, nw_ref, mod_ref, *rest):
        if first:
            dx_ref, acc_ref = rest
        else:
            m_ref, modb_ref, dx_ref, acc_ref, dm_ref, gacc_ref = rest
        i = pl.program_id(0)

        @pl.when(i == 0)
        def _():
            acc_ref[...] = jnp.zeros_like(acc_ref)
            if not first:
                gacc_ref[...] = jnp.zeros_like(gacc_ref)

        ctx = i >= nxb
        m = mod_ref[...]
        scale1 = 1.0 + jnp.where(ctx, m[4:5], m[1:2])
        x = x_ref[...]
        nw_v = nw_ref[...]
        r = lax.rsqrt(jnp.mean(x * x, axis=-1, keepdims=True) + EPS)
        xn = x * r
        dh = dh_ref[...]
        dshift = jnp.sum(dh, axis=0, keepdims=True)
        dscale = jnp.sum(dh * (xn * nw_v), axis=0, keepdims=True)
        acc_ref[6:7, :] += jnp.sum(dh * scale1 * xn, axis=0, keepdims=True)
        dxn = dh * (nw_v * scale1)
        dx = dxo_ref[...] + r * (dxn - xn * jnp.mean(dxn * xn, axis=-1, keepdims=True))

        @pl.when(i < nxb)
        def _():
            acc_ref[0:1, :] += dshift
            acc_ref[1:2, :] += dscale
            dx_ref[...] = dx
            if not first:
                _gate_bwd_rows(dx, m_ref, modb_ref, dm_ref, gacc_ref, False)

        @pl.when(i >= nxb)
        def _():
            acc_ref[3:4, :] += dshift
            acc_ref[4:5, :] += dscale
            if not first:
                dx_ref[...] = dx
                _gate_bwd_rows(dx, m_ref, modb_ref, dm_ref, gacc_ref, True)

    row = pl.BlockSpec((ROW_TILE, d), lambda i: (i, 0))
    acc = _full((8, d))
    if first:
        extra_in, extra = [], []
        dx_spec = pl.BlockSpec((ROW_TILE, d), lambda i: (jnp.minimum(i, nxb - 1), 0))
        dx_shape = jax.ShapeDtypeStruct((n_lat, d), F32)
        extra_out, extra_shape = [], []
    else:
        extra_in, extra = [row, acc], list(below)
        dx_spec, dx_shape = row, jax.ShapeDtypeStruct((t, d), F32)
        extra_out, extra_shape = [row, acc], [jax.ShapeDtypeStruct((t, d), BF16), jax.ShapeDtypeStruct((8, d), F32)]
    return pl.pallas_call(body, name=name, grid=(t // ROW_TILE,),
                          in_specs=[row, row, row, _full((1, d)), acc] + extra_in,
                          out_specs=[dx_spec, acc] + extra_out,
                          out_shape=[dx_shape, jax.ShapeDtypeStruct((8, d), F32)] + extra_shape,
                          compiler_params=_params(("arbitrary",)))(dhx, xt, dxo, nw, mod, *extra)


def _adamw(g, w, m, v):
    m = ADAM_B1 * m + (1.0 - ADAM_B1) * g
    v = ADAM_B2 * v + (1.0 - ADAM_B2) * jnp.square(g)
    m_hat = m / (1.0 - ADAM_B1 ** ADAM_STEP)
    v_hat = v / (1.0 - ADAM_B2 ** ADAM_STEP)
    delta = -ADAM_LR * (m_hat / (jnp.sqrt(v_hat) + ADAM_EPS) + ADAM_WD * w)
    return delta, m, v


def _sum_adamw(parts, w, m, v, name, row0=0, into=None):
    n_p, r, n = parts.shape
    r_all = w.shape[0]
    part_block_bytes = 4 * 1024 * 1024
    br = 8
    for cand in (512, 256, 128, 64, 32, 16):
        if r % cand == 0 and row0 % cand == 0 and n_p * cand * n * parts.dtype.itemsize <= part_block_bytes:
            br = cand
            break
    blk0 = row0 // br

    def body(p_ref, w_ref, m_ref, v_ref, *rest):
        g_out, d_out, m_out, v_out = rest[-4:]
        g = p_ref[0].astype(F32)
        for j in range(1, n_p):
            g = g + p_ref[j].astype(F32)
        g_out[...] = g
        d_out[...], m_out[...], v_out[...] = _adamw(g, w_ref[...], m_ref[...], v_ref[...])

    row = pl.BlockSpec((br, n), lambda i: (i + blk0, 0))
    kept = [] if into is None else list(into)
    return pl.pallas_call(body, name=name, grid=(r // br,),
                          in_specs=[pl.BlockSpec((n_p, br, n), lambda i: (0, i, 0)), row, row, row]
                          + [pl.BlockSpec(memory_space=pl.ANY)] * len(kept),
                          out_specs=[row] * 4, out_shape=[jax.ShapeDtypeStruct((r_all, n), F32)] * 4,
                          input_output_aliases={4 + j: j for j in range(len(kept))},
                          compiler_params=_params(("parallel",)))(parts, w, m, v, *kept)


def _rope_tables(n_lat, n_ctx):
    f = HEAD_DIM // 4
    rows = n_lat // GRID_W
    inv = ROPE_BASE ** (-jnp.arange(f, dtype=F32) / f)
    ang_r = jnp.arange(rows).astype(F32)[:, None] * inv[None, :]
    ang_c = jnp.arange(GRID_W).astype(F32)[:, None] * inv[None, :]

    def by_row(a):
        return jnp.broadcast_to(a[:, None, :], (rows, GRID_W, f)).reshape(n_lat, f)

    def by_col(a):
        return jnp.broadcast_to(a[None, :, :], (rows, GRID_W, f)).reshape(n_lat, f)

    cr, sr, cc, sc = by_row(jnp.cos(ang_r)), by_row(jnp.sin(ang_r)), by_col(jnp.cos(ang_c)), by_col(jnp.sin(ang_c))
    zero = jnp.zeros_like(cr)
    cos = jnp.concatenate([cr, cr, cc, cc], axis=-1)
    sa = jnp.concatenate([-sr, zero, -sc, zero], axis=-1)
    sb = jnp.concatenate([zero, sr, zero, sc], axis=-1)
    pad = jnp.zeros((n_ctx, HEAD_DIM), F32)
    return (jnp.concatenate([cos, pad + 1.0], axis=0), jnp.concatenate([sa, pad], axis=0),
            jnp.concatenate([sb, pad], axis=0))


def _pad_rows(a, rows):
    return jnp.pad(a, [(0, rows - a.shape[0])] + [(0, 0)] * (a.ndim - 1))


def _pad_cols(a, cols):
    return jnp.pad(a, [(0, 0), (0, cols - a.shape[1])])


def kernel(x, c, ctx, c_ctx, norm_w, w_mod, b_mod, w_in, conv_w, conv_norm_w, ret_norm_w, ret_decay_f, ret_decay_b, w_out, final_norm_w, loss_target, m_c_ctx, m_norm_w, m_w_mod, m_b_mod, m_w_in, m_conv_w, m_conv_norm_w, m_ret_norm_w, m_ret_decay_f, m_ret_decay_b, m_w_out, m_final_norm_w, v_c_ctx, v_norm_w, v_w_mod, v_b_mod, v_w_in, v_conv_w, v_conv_norm_w, v_ret_norm_w, v_ret_decay_f, v_ret_decay_b, v_w_out, v_final_norm_w):
    depth = norm_w.shape[0]
    n_lat, d = x.shape[1], x.shape[2]
    n_ctx = ctx.shape[1]
    s = d // 2
    n_heads = ret_decay_f.shape[1]
    nx, ncc = n_lat // CHUNK, n_ctx // CHUNK
    n_mod = w_mod.shape[2]
    n_cw = conv_w.shape[2]
    r_out = w_out.shape[1]
    assert s == n_heads * HEAD_DIM and w_in.shape[2] == s and N_DEV * r_out == d
    assert n_lat % ROW_TILE == 0 and n_ctx % ROW_TILE == 0 and 3 * depth * n_cw <= d and d >= 3 * n_mod // 3
    me = 4 * lax.axis_index("x") + 2 * lax.axis_index("y") + lax.axis_index("c")

    w_in_bf = [w_in[l].astype(BF16) for l in range(depth)]
    w_out_bf = [w_out[l].astype(BF16) for l in range(depth)]
    near = _push_start([w_in_bf[0]], [_landing(w_in_bf[0], me)], "near", "w_in0_start")

    first = jnp.concatenate([c.reshape(1, d), _pad_cols(conv_w.reshape(1, -1), d), jnp.zeros((6, d), F32)], axis=0)
    first = first + near[4][0:1, 0:1]
    (first_g,) = _all_gather([first], "gather_cond", True)
    first_g = first_g.reshape(N_DEV, 8, d)
    c_all = first_g[:, 0, :]
    conv_full = first_g[:, 1, :3 * depth * n_cw].reshape(N_DEV, depth, 3, n_cw)
    conv_full = conv_full.transpose(1, 2, 0, 3).reshape(depth, 3, N_DEV * n_cw)
    c9 = jnp.concatenate([c_all, c_ctx.reshape(1, d), jnp.zeros((7, d), F32)], axis=0)

    b_sh = lax.dynamic_slice(b_mod, (0, me * n_mod), (depth, n_mod))
    mod_sh = jnp.concatenate([_mod_rows(c9, w_mod[l], b_sh[l:l + 1], f"mod_rows_l{l}") for l in range(depth)], axis=0)
    (mod_g,) = _all_gather([mod_sh], "gather_mod", True)
    mod_g = mod_g.reshape(N_DEV, depth, 16, n_mod)
    mods = []
    for l in range(depth):
        mine = lax.dynamic_index_in_dim(mod_g[:, l], me, axis=1, keepdims=False).reshape(3, d)
        cx = mod_g[:, l, 8, :].reshape(3, d)
        mods.append(jnp.concatenate([mine, cx, jnp.zeros((2, d), F32)], axis=0))

    (w_in0_near,) = _push_wait(*near[:4], "near", mod_g, "w_in0_wait")
    relay = _push_start([], [w_in0_near], "relay", "w_in0_relay_start")
    w_in_g = [None] * depth
    w_out_g = [None] * depth
    pending, tokens = [], [relay[4]]
    for l in range(depth):
        srcs = [w_out_bf[l]] + ([w_in_bf[l]] if l > 0 else [])
        lands = [_landing(a, me) for a in srcs]
        started = _push_start(srcs, lands, "gather", f"weights_start_l{l}", after=tokens[-1:])
        pending.append(started[:4])
        tokens.append(started[4])

    cos, sa, sb_tab = _rope_tables(n_lat, n_ctx)
    t_all = n_lat + n_ctx

    saved = []
    xt = None
    for l in range(depth):
        tiles = _tiles(l, t_all, d)
        names = ["dc", "dlf", "dlb", "qf", "kf", "qb", "kb", "cdf", "cdb", "lg"]
        dec = jnp.stack([ret_decay_f[l], ret_decay_b[l]], axis=0)
        tabs = dict(zip(names, _decay_tables(dec, n_heads, f"decay_tables_l{l}")))
        if l == 0:
            hx, xt = _prenorm_first(x[0], ctx[0], norm_w[0:1], mods[0], "prenorm_l0", after=tokens)
            (w_in_g[0],) = _push_wait(*relay[:4], "relay", hx, "w_in0_relay_wait")
        else:
            landed = _push_wait(*pending[l], "gather", xt, f"weights_wait_l{l}")
            w_out_g[l], w_in_g[l] = landed[0].reshape(d, d), landed[1]
            hx = _prenorm(xt, norm_w[l:l + 1], mods[l], n_lat, f"prenorm_l{l}")
        u, qkv = _in_proj(hx, w_in_g[l], cos, sa, sb_tab, n_heads, tiles["in_tm"], f"in_proj_l{l}")
        sf, sb = _state_sweep(qkv, tabs, n_heads, nx, ncc, f"state_sweep_l{l}")
        ycat, o = _mix_fwd(u, qkv, sf, sb, tabs, conv_full[l], conv_norm_w[l:l + 1], ret_norm_w[l:l + 1],
                           n_heads, nx, ncc, f"mix_fwd_l{l}")
        if l == 0:
            (landed,) = _push_wait(*pending[0], "gather", ycat, "weights_wait_l0")
            w_out_g[0] = landed.reshape(d, d)
        m_res, x_new = _out_proj(ycat, w_out_g[l], xt, mods[l], n_lat, tiles["out_tm"], tiles["out_tn"], f"out_proj_l{l}")
        saved.append(dict(tabs=tabs, xt=xt, hx=hx, u=u, qkv=qkv, sf=sf, sb=sb, ycat=ycat, o=o, m=m_res, tiles=tiles))
        xt = x_new

    top = saved[-1]
    dxt, dm, loss_blk, dfnw, gate_acc = _loss_head(xt, loss_target[0], final_norm_w.reshape(1, d), top["m"],
                                                   mods[depth - 1], n_lat, "loss_head")
    loss = lax.psum(loss_blk[0, 0], MESH_AXES)

    dmod_x, dmod_c, dnw, dcnw, dgnw, dconv, ddec, dwin, dwout = [], [], [], [], [], [], [], [], []
    for l in reversed(range(depth)):
        sv = saved[l]
        tiles = sv["tiles"]
        dycat = _matmul_nt(dm, w_out_g[l], tiles["ob_tn"], f"out_proj_bwd_l{l}")
        dwout.append(_weight_grad(sv["ycat"], dm.reshape(1, *dm.shape), tiles["wo_bm"], f"w_out_grad_l{l}")[0])
        g, dz, db, drz, do, norm_acc = _mix_bwd_a(dycat, sv["u"], sv["o"], conv_full[l], conv_norm_w[l:l + 1],
                                                   ret_norm_w[l:l + 1], n_heads, nx, ncc, f"mix_bwd_a_l{l}")
        gf, gb = _grad_state_sweep(sv["qkv"], do, sv["tabs"], n_heads, nx, ncc, f"grad_state_sweep_l{l}")
        du, conv_acc, dlg = _mix_bwd_b(sv["u"], g, dz, db, drz, sv["qkv"], do, sv["sf"], sv["sb"],
                                       gf, gb, sv["tabs"], cos, sa, sb_tab, conv_full[l], n_heads, nx, ncc,
                                       f"mix_bwd_b_l{l}")
        srcs = [_weight_grad(sv["hx"], du, tiles["wg_bm"], f"w_in_grad_l{l}"), dwout[-1].reshape(N_DEV, r_out, d)]
        lands = [_landing(lax.dynamic_index_in_dim(a, me, axis=0, keepdims=False), me) for a in srcs]
        started = _push_start(srcs, lands, "scatter", f"grads_start_l{l}")
        dwin.append(started[:4])
        dhx = _in_proj_bwd(du, w_in_g[l], tiles["bwd_tm"], f"in_proj_bwd_l{l}", after=started[4:])
        this_gate = gate_acc
        if l > 0:
            below = (saved[l - 1]["m"], mods[l - 1])
            dxt, pre_acc, dm, gate_acc = _prenorm_bwd(dhx, sv["xt"], dxt, norm_w[l:l + 1], mods[l], n_lat, below,
                                                      f"prenorm_bwd_l{l}")
        else:
            dxt, pre_acc = _prenorm_bwd(dhx, sv["xt"], dxt, norm_w[l:l + 1], mods[l], n_lat, None, f"prenorm_bwd_l{l}")
        gate_acc_l = this_gate
        dmod_x.append(jnp.concatenate([pre_acc[0], pre_acc[1], gate_acc_l[2]]))
        dmod_c.append(jnp.concatenate([pre_acc[3], pre_acc[4], gate_acc_l[5]]))
        dnw.append(pre_acc[6])
        dcnw.append(norm_acc[0])
        dgnw.append(norm_acc[1])
        dconv.append(conv_acc[0:3])
        ddec.append(dlg[0:2, :n_heads])
    for lst in (dmod_x, dmod_c, dnw, dcnw, dgnw, dconv, ddec, dwin, dwout):
        lst.reverse()
    grad_x = dxt.reshape(1, n_lat, d)

    rows = []
    for l in range(depth):
        rows += [dmod_x[l], dmod_c[l]]
    (dmod_g,) = _all_gather([_pad_rows(jnp.stack(rows, axis=0), 8)], "gather_dmod", True)
    dmod_g = dmod_g.reshape(N_DEV, 8, 3 * d)
    mine_cols = lax.dynamic_slice(dmod_g, (0, 0, me * n_mod), (N_DEV, 8, n_mod))
    g_wmod, dcc = [], jnp.zeros((d,), F32)
    for l in range(depth):
        gw, dc_part = _mod_grads(mine_cols[:, 2 * l], mine_cols[:, 2 * l + 1], c9, w_mod[l], f"mod_grads_l{l}")
        g_wmod.append(gw)
        dcc = dcc + dc_part[0]

    n_small = 16
    small = jnp.concatenate([
        jnp.stack(dnw, axis=0),
        jnp.concatenate(dcnw).reshape(1, -1),
        jnp.concatenate(dgnw).reshape(1, -1),
        dfnw[0:1],
        dcc.reshape(1, d),
        jnp.stack(dconv, axis=0).reshape(-1, d),
        _pad_cols(jnp.stack(ddec, axis=0).reshape(1, -1), d),
    ], axis=0)
    assert depth * s == d and small.shape[0] <= n_small
    n_rows = small.shape[0]
    (small_g,) = _all_gather([_pad_rows(small, n_small)], "gather_small", True)
    small_g = small_g.reshape(N_DEV, n_small, d)

    def pack_small(nw_, cn_, gn_, fn_, cc_, df_, db_):
        return _pad_rows(jnp.concatenate([
            nw_, cn_.reshape(1, -1), gn_.reshape(1, -1), fn_.reshape(1, d), cc_.reshape(1, d),
            jnp.zeros((n_rows - depth - 5, d), F32),
            _pad_cols(jnp.stack([df_, db_], axis=1).reshape(1, -1), d)], axis=0), n_small)

    w_s = pack_small(norm_w, conv_norm_w, ret_norm_w, final_norm_w, c_ctx, ret_decay_f, ret_decay_b)
    m_s = pack_small(m_norm_w, m_conv_norm_w, m_ret_norm_w, m_final_norm_w, m_c_ctx, m_ret_decay_f, m_ret_decay_b)
    v_s = pack_small(v_norm_w, v_conv_norm_w, v_ret_norm_w, v_final_norm_w, v_c_ctx, v_ret_decay_f, v_ret_decay_b)
    small_out = _sum_adamw(small_g, w_s, m_s, v_s, "adamw_small")

    def unpack_small(a):
        nw_ = a[0:depth]
        cn_ = a[depth].reshape(depth, s)
        gn_ = a[depth + 1].reshape(depth, s)
        fn_ = a[depth + 2]
        cc_ = a[depth + 3]
        dd = a[n_rows - 1, :depth * 2 * n_heads].reshape(depth, 2, n_heads)
        return dict(c_ctx=cc_, norm_w=nw_, conv_norm_w=cn_, ret_norm_w=gn_, ret_decay_f=dd[:, 0], ret_decay_b=dd[:, 1],
                    final_norm_w=fn_)

    res = {}
    for kind, arr in zip(("grad", "delta", "m", "v"), small_out):
        for k_, val in unpack_small(arr).items():
            res[(kind, k_)] = val

    bm_parts = jnp.concatenate([dmod_g[:, 0:2 * depth:2].reshape(N_DEV, depth, 3 * d),
                                dmod_g[:, 1:2 * depth:2].reshape(N_DEV, depth, 3 * d)], axis=0)
    bm_parts = jnp.concatenate([bm_parts, jnp.zeros((2 * N_DEV, 8 - depth, 3 * d), F32)], axis=1)
    pad8 = lambda a: _pad_rows(a, 8)
    bm_out = _sum_adamw(bm_parts, pad8(b_mod), pad8(m_b_mod), pad8(v_b_mod), "adamw_b_mod")
    for kind, arr in zip(("grad", "delta", "m", "v"), bm_out):
        res[(kind, "b_mod")] = arr[:depth]

    conv_rows = small_g[:, depth + 4:depth + 4 + 3 * depth * s // d].reshape(N_DEV, depth * 3, s)
    conv_mine = lax.dynamic_slice(conv_rows, (0, 0, me * n_cw), (N_DEV, depth * 3, n_cw))
    conv_mine = jnp.concatenate([conv_mine, jnp.zeros((N_DEV, 8 - depth * 3, n_cw), F32)], axis=1)
    cw2 = lambda a: _pad_rows(a.reshape(depth * 3, n_cw), 8)
    cw_out = _sum_adamw(conv_mine, cw2(conv_w), cw2(m_conv_w), cw2(v_conv_w), "adamw_conv_w")
    for kind, arr in zip(("grad", "delta", "m", "v"), cw_out):
        res[(kind, "conv_w")] = arr[:depth * 3].reshape(depth, 3, n_cw)

    wm_out = _sum_adamw(jnp.stack(g_wmod, axis=0).reshape(1, depth * d, n_mod), w_mod.reshape(depth * d, n_mod),
                        m_w_mod.reshape(depth * d, n_mod), v_w_mod.reshape(depth * d, n_mod), "adamw_w_mod")
    for kind, arr in zip(("grad", "delta", "m", "v"), wm_out):
        res[(kind, "w_mod")] = arr.reshape(depth, d, n_mod)

    wi_out = wo_out = None
    after = wm_out[0]
    for l in reversed(range(depth)):
        win_parts, wout_parts = _push_wait(*dwin[l], "scatter", after, f"grads_wait_l{l}")
        wi_out = _sum_adamw(win_parts, w_in.reshape(depth * d, s), m_w_in.reshape(depth * d, s),
                            v_w_in.reshape(depth * d, s), f"adamw_w_in_l{l}", row0=l * d, into=wi_out)
        wo_out = _sum_adamw(wout_parts, w_out.reshape(depth * r_out, d), m_w_out.reshape(depth * r_out, d),
                            v_w_out.reshape(depth * r_out, d), f"adamw_w_out_l{l}", row0=l * r_out, into=wo_out)
        after = wo_out[0]
    for kind, arr in zip(("grad", "delta", "m", "v"), wi_out):
        res[(kind, "w_in")] = arr.reshape(depth, d, s)
    for kind, arr in zip(("grad", "delta", "m", "v"), wo_out):
        res[(kind, "w_out")] = arr.reshape(depth, r_out, d)

    order = ["c_ctx", "norm_w", "w_mod", "b_mod", "w_in", "conv_w", "conv_norm_w", "ret_norm_w", "ret_decay_f",
             "ret_decay_b", "w_out", "final_norm_w"]
    outs = [loss, grad_x]
    for kind in ("grad", "delta", "m", "v"):
        outs += [res[(kind, k_)] for k_ in order]
    return tuple(outs)
```

```python
import functools

import jax
import jax.numpy as jnp
from jax import lax
from jax.experimental import pallas as pl
from jax.experimental.pallas import tpu as pltpu

F32 = jnp.float32
BF16 = jnp.bfloat16

EPS = 1e-6
CHUNK = 128
HEAD_DIM = 128
GRID_W = 64
ROPE_BASE = 10000.0
N_DEV = 8
ADAM_LR, ADAM_B1, ADAM_B2, ADAM_EPS, ADAM_WD, ADAM_STEP = 0.001, 0.9, 0.999, 1e-08, 0.01, 10

ROW_TILE = 256
V7X_VMEM_LIMIT = 56 * 1024 * 1024
MESH_AXES = ("x", "y", "c")

NN = ((1,), (0,))
NT = ((1,), (1,))
TN = ((0,), (0,))


def _dot(a, b, dims):
    return lax.dot_general(a, b, (dims, ((), ())), preferred_element_type=F32)


def _params(sem=None):
    if sem is None:
        return pltpu.CompilerParams(vmem_limit_bytes=V7X_VMEM_LIMIT)
    return pltpu.CompilerParams(dimension_semantics=sem, vmem_limit_bytes=V7X_VMEM_LIMIT)


def _silu(z):
    return z * jax.nn.sigmoid(z)


def _dsilu(z):
    s = jax.nn.sigmoid(z)
    return s * (1.0 + z * (1.0 - s))


def _sum_all(a):
    return jnp.sum(jnp.sum(a, axis=1, keepdims=True), axis=0, keepdims=True)


def _mm_rows(t):
    return 768 if t % 768 == 0 else ROW_TILE


def _rows_or(t, rows):
    return rows if t % rows == 0 else _mm_rows(t)


def _tiles(layer, t, d):
    tiles = dict(in_tm=_rows_or(t, 1408), bwd_tm=_rows_or(t, 1056), bwd_gs=1, wg_bm=d, out_tm=_mm_rows(t),
                 out_tn=min(d, 1024), wo_bm=d, ob_tn=d)
    if layer > 0:
        tiles.update(bwd_tm=_mm_rows(t), bwd_gs=2)
    return tiles


def _full(shape):
    n = len(shape)
    return pl.BlockSpec(shape, lambda *_: (0,) * n)


def _peers(x, y, c):
    return [(x, y, 1 - c), (1 - x, y, c), (x, 1 - y, c), (1 - x, 1 - y, c),
            (1 - x, y, 1 - c), (x, 1 - y, 1 - c), (1 - x, 1 - y, 1 - c)]


def _lin(p):
    return 4 * p[0] + 2 * p[1] + p[2]


def _all_gather(arrays, name, in_vmem):
    n_arr = len(arrays)
    space = pltpu.VMEM if in_vmem else pl.ANY

    def body(*refs):
        ins, outs = refs[:n_arr], refs[n_arr:2 * n_arr]
        send_sems, recv_sems, local_sems = refs[2 * n_arr:]
        x, y, c = lax.axis_index("x"), lax.axis_index("y"), lax.axis_index("c")
        me, sibling = (x, y, c), (x, y, 1 - c)
        chips = [(1 - x, y), (x, 1 - y), (1 - x, 1 - y)]
        every = []
        locals_ = []
        for a in range(n_arr):
            m_per = ins[a].shape[0]
            out_ref = outs[a]

            def rows(p, out_ref=out_ref, m_per=m_per):
                return out_ref.at[pl.ds(_lin(p) * m_per, m_per), :]

            def copy(k, block, to, src=None, a=a, rows=rows):
                return pltpu.make_async_remote_copy(
                    src_ref=rows(block) if src is None else src, dst_ref=rows(block),
                    send_sem=send_sems.at[a, k], recv_sem=recv_sems.at[a, k],
                    device_id=to, device_id_type=pl.DeviceIdType.MESH)

            mine = pltpu.make_async_copy(ins[a], rows(me), local_sems.at[a])
            mine.start()
            locals_.append(mine)
            first = [copy(0, me, sibling, src=ins[a])]
            first += [copy(1 + j, me, (*chip, c), src=ins[a]) for j, chip in enumerate(chips)]
            for cp in first:
                cp.start()
            every.append((copy, first))
        sends = []
        for a in range(n_arr):
            copy, first = every[a]
            passed = [copy(4 + j, (*chip, c), sibling) for j, chip in enumerate(chips)]
            for j, chip in enumerate(chips):
                copy(1 + j, (*chip, c), me).wait_recv()
                passed[j].start()
            sends += first + passed
        for a in range(n_arr):
            copy, _ = every[a]
            copy(0, sibling, me).wait_recv()
            for j, chip in enumerate(chips):
                copy(4 + j, (*chip, 1 - c), me).wait_recv()
        for cp in sends:
            cp.wait_send()
        for mine in locals_:
            mine.wait()

    outs = pl.pallas_call(
        body, name=name,
        out_shape=[jax.ShapeDtypeStruct((N_DEV * a.shape[0], a.shape[1]), a.dtype) for a in arrays],
        in_specs=[pl.BlockSpec(memory_space=space)] * n_arr,
        out_specs=[pl.BlockSpec(memory_space=space)] * n_arr,
        scratch_shapes=[pltpu.SemaphoreType.DMA((n_arr, 7)), pltpu.SemaphoreType.DMA((n_arr, 7)),
                        pltpu.SemaphoreType.DMA((n_arr,))],
        compiler_params=_params(),
    )(*arrays)
    return list(outs)


_HBM = pl.BlockSpec(memory_space=pltpu.HBM)
_SEM = pl.BlockSpec(memory_space=pltpu.SEMAPHORE)
_DATAFLOW = pltpu.SideEffectType.DATAFLOW_SIDE_EFFECTING


PUSH_COPIES = {"scatter": 7, "gather": 7, "near": 4, "relay": 3}


def _push_copies(src_refs, land_refs, send_sems, recv_sems, mode):
    x, y, c = lax.axis_index("x"), lax.axis_index("y"), lax.axis_index("c")
    me, sibling = (x, y, c), (x, y, 1 - c)
    n_k = PUSH_COPIES[mode]
    out, back = [], []
    if mode == "relay":
        for k, chip in enumerate([(1 - x, y), (x, 1 - y), (1 - x, 1 - y)]):
            for a, land in enumerate(land_refs):
                sems = dict(send_sem=send_sems.at[n_k * a + k], recv_sem=recv_sems.at[n_k * a + k],
                            device_id=sibling, device_id_type=pl.DeviceIdType.MESH)
                mine = land.at[_lin((*chip, c))]
                out.append(pltpu.make_async_remote_copy(src_ref=mine, dst_ref=mine, **sems))
                back.append(pltpu.make_async_remote_copy(src_ref=mine, dst_ref=land.at[_lin((*chip, 1 - c))], **sems))
        return out, back
    for k, peer in enumerate(_peers(x, y, c)[:n_k]):
        for a, (src, land) in enumerate(zip(src_refs, land_refs)):
            sems = dict(send_sem=send_sems.at[n_k * a + k], recv_sem=recv_sems.at[n_k * a + k],
                        device_id=peer, device_id_type=pl.DeviceIdType.MESH)
            mine = src.at[_lin(peer)] if mode == "scatter" else src
            out.append(pltpu.make_async_remote_copy(src_ref=mine, dst_ref=land.at[_lin(me)], **sems))
            back.append(pltpu.make_async_remote_copy(src_ref=mine, dst_ref=land.at[_lin(peer)], **sems))
    return out, back


def _push_start(srcs, lands, mode, name, after=()):
    n_src, n = len(srcs), len(lands)
    n_buf = n_src + n
    n_in = n_buf + len(after)
    n_sem = PUSH_COPIES[mode] * n

    def body(*refs):
        send_sems, recv_sems = refs[n_in], refs[n_in + 1]
        out, _ = _push_copies(refs[:n_src], refs[n_src:n_buf], send_sems, recv_sems, mode)
        for cp in out:
            cp.start()
        token = refs[-1]
        token[...] = jnp.zeros_like(token)

    both = list(srcs) + list(lands)
    res = pl.pallas_call(
        body, name=name,
        out_shape=[pltpu.SemaphoreType.DMA((n_sem,)), pltpu.SemaphoreType.DMA((n_sem,))]
        + [pltpu.HBM(a.shape, a.dtype) for a in both] + [jax.ShapeDtypeStruct((8, 128), F32)],
        in_specs=[_HBM] * n_buf + [pl.BlockSpec(memory_space=pl.ANY)] * len(after),
        out_specs=[_SEM, _SEM] + [_HBM] * n_buf + [pl.BlockSpec(memory_space=pltpu.VMEM)],
        input_output_aliases={i: 2 + i for i in range(n_buf)},
        compiler_params=pltpu.CompilerParams(has_side_effects=_DATAFLOW),
    )(*[pltpu.with_memory_space_constraint(a, pltpu.HBM) for a in both], *after)
    return res[0], res[1], list(res[2:2 + n_src]), list(res[2 + n_src:2 + n_buf]), res[-1]


def _push_wait(send_sems, recv_sems, srcs, lands, mode, after, name):
    n_src, n = len(srcs), len(lands)
    n_buf = n_src + n

    def body(*refs):
        out, back = _push_copies(refs[:n_src], refs[n_src:n_buf], refs[n_buf], refs[n_buf + 1], mode)
        for cp in out:
            cp.wait_send()
        for cp in back:
            cp.wait_recv()

    both = list(srcs) + list(lands)
    res = pl.pallas_call(
        body, name=name,
        out_shape=[pltpu.HBM(a.shape, a.dtype) for a in both],
        in_specs=[_HBM] * n_buf + [_SEM, _SEM, pl.BlockSpec(memory_space=pl.ANY)],
        out_specs=[_HBM] * n_buf,
        input_output_aliases={i: i for i in range(n_buf)},
        compiler_params=pltpu.CompilerParams(has_side_effects=_DATAFLOW),
    )(*both, send_sems, recv_sems, after)
    return list(res[n_src:])


def _landing(own, me):
    zone = lax.empty((N_DEV,) + own.shape, own.dtype)
    return lax.dynamic_update_slice(zone, own[None], (me,) + (0,) * own.ndim)


def _mod_rows(c9, w_mod, b_sh, name):
    n = w_mod.shape[1]

    def body(c_ref, w_ref, b_ref, o_ref):
        s9 = _silu(c_ref[...]).astype(BF16)
        o_ref[...] = _dot(s9, w_ref[...].astype(BF16), NN) + b_ref[...]

    return pl.pallas_call(body, name=name, out_shape=jax.ShapeDtypeStruct((16, n), F32),
                          compiler_params=_params())(c9, w_mod, b_sh)


def _mod_grads(dm_rows, dc_rows, c9, w_mod, name):
    d, n = w_mod.shape

    def body(dm_ref, dc_ref, c_ref, w_ref, gw_ref, dc_out):
        dc = dc_ref[...]
        tot = dc[0:1]
        for j in range(1, N_DEV):
            tot = tot + dc[j:j + 1]
        row = lax.broadcasted_iota(jnp.int32, (8, n), 0)
        lower = jnp.where(row == 0, tot, 0.0)
        dmod9 = jnp.concatenate([dm_ref[...], lower], axis=0).astype(BF16)
        c9v = c_ref[...]
        s9 = _silu(c9v).astype(BF16)
        gw_ref[...] = _dot(s9, dmod9, TN)
        ds = _dot(lower.astype(BF16), w_ref[...].astype(BF16), NT)
        dc_out[...] = ds * _dsilu(c9v[8:16])

    return pl.pallas_call(body, name=name,
                          out_shape=[jax.ShapeDtypeStruct((d, n), F32), jax.ShapeDtypeStruct((8, d), F32)],
                          compiler_params=_params())(dm_rows, dc_rows, c9, w_mod)


def _decay_tables(dec, n_heads, name):
    c = CHUNK

    def body(dec_ref, dc_ref, dlf_ref, dlb_ref, qf_ref, kf_ref, qb_ref, kb_ref, cdf_ref, cdb_ref, lg_ref):
        h = pl.program_id(0)
        d = dec_ref[...]
        lane = lax.broadcasted_iota(jnp.int32, d.shape, 1)
        lg = -jnp.exp(jnp.sum(jnp.where(lane == h, d, 0.0), axis=1, keepdims=True))
        lgf, lgb = lg[0:1], lg[1:2]
        i = lax.broadcasted_iota(jnp.int32, (c, c), 0).astype(F32)
        j = lax.broadcasted_iota(jnp.int32, (c, c), 1).astype(F32)
        diff = i - j
        d_f = jnp.where(diff >= 0, jnp.exp(lgf * jnp.maximum(diff, 0.0)), 0.0)
        d_b = jnp.where(diff <= 0, jnp.exp(lgb * jnp.maximum(-diff, 0.0)), 0.0)
        dc_ref[...] = d_f + d_b
        dlf_ref[...] = diff * d_f
        dlb_ref[...] = -diff * d_b
        pos = lax.broadcasted_iota(jnp.int32, (c, HEAD_DIM), 0).astype(F32)
        qf_ref[...] = jnp.exp(lgf * (pos + 1.0))
        kf_ref[...] = jnp.exp(lgf * (c - 1.0 - pos))
        qb_ref[...] = jnp.exp(lgb * (c - pos))
        kb_ref[...] = jnp.exp(lgb * pos)
        ones = jnp.ones((8, HEAD_DIM), F32)
        cdf_ref[...] = jnp.exp(lgf * float(c)) * ones
        cdb_ref[...] = jnp.exp(lgb * float(c)) * ones

        @pl.when(h == 0)
        def _():
            lg_ref[...] = jnp.zeros_like(lg_ref)

        row8 = lax.broadcasted_iota(jnp.int32, (8, HEAD_DIM), 0)
        lane8 = lax.broadcasted_iota(jnp.int32, (8, HEAD_DIM), 1)
        lg_ref[...] += (jnp.where((row8 == 0) & (lane8 == h), lgf, 0.0)
                        + jnp.where((row8 == 1) & (lane8 == h), lgb, 0.0))

    def per_head(*tail):
        return pl.BlockSpec((None,) + tail, lambda h: (h,) + (0,) * len(tail))

    shapes = [(c, c)] * 3 + [(c, HEAD_DIM)] * 4 + [(8, HEAD_DIM)] * 2
    return pl.pallas_call(
        body, name=name, grid=(n_heads,),
        in_specs=[_full(dec.shape)],
        out_specs=[per_head(*s) for s in shapes] + [_full((8, HEAD_DIM))],
        out_shape=[jax.ShapeDtypeStruct((n_heads,) + s, F32) for s in shapes]
        + [jax.ShapeDtypeStruct((8, HEAD_DIM), F32)],
        compiler_params=_params(("arbitrary",)),
    )(dec)


def _modulate(x, nw, shift, scale):
    r = lax.rsqrt(jnp.mean(x * x, axis=-1, keepdims=True) + EPS)
    return ((x * r) * nw * (1.0 + scale) + shift).astype(BF16)


def _prenorm(xt, nw, mod, n_lat, name):
    t, d = xt.shape
    nxb = n_lat // ROW_TILE

    def body(x_ref, nw_ref, mod_ref, o_ref):
        ctx = pl.program_id(0) >= nxb
        m = mod_ref[...]
        o_ref[...] = _modulate(x_ref[...], nw_ref[...], jnp.where(ctx, m[3:4], m[0:1]), jnp.where(ctx, m[4:5], m[1:2]))

    row = pl.BlockSpec((ROW_TILE, d), lambda i: (i, 0))
    return pl.pallas_call(body, name=name, grid=(t // ROW_TILE,),
                          in_specs=[row, _full((1, d)), _full((8, d))],
                          out_specs=row, out_shape=jax.ShapeDtypeStruct((t, d), BF16),
                          compiler_params=_params(("parallel",)))(xt, nw, mod)


def _prenorm_first(x, ctx, nw, mod, name, after=()):
    n_lat, d = x.shape
    t = n_lat + ctx.shape[0]
    nxb = n_lat // ROW_TILE

    def body(x_ref, c_ref, nw_ref, mod_ref, *rest):
        o_ref, xt_ref = rest[-2:]
        m = mod_ref[...]
        nw_v = nw_ref[...]

        @pl.when(pl.program_id(0) < nxb)
        def _():
            xv = x_ref[...]
            xt_ref[...] = xv
            o_ref[...] = _modulate(xv, nw_v, m[0:1], m[1:2])

        @pl.when(pl.program_id(0) >= nxb)
        def _():
            xv = c_ref[...]
            xt_ref[...] = xv
            o_ref[...] = _modulate(xv, nw_v, m[3:4], m[4:5])

    row = pl.BlockSpec((ROW_TILE, d), lambda i: (i, 0))
    return pl.pallas_call(
        body, name=name, grid=(t // ROW_TILE,),
        in_specs=[pl.BlockSpec((ROW_TILE, d), lambda i: (jnp.minimum(i, nxb - 1), 0)),
                  pl.BlockSpec((ROW_TILE, d), lambda i: (jnp.maximum(i - nxb, 0), 0)), _full((1, d)), _full((8, d))]
        + [pl.BlockSpec(memory_space=pl.ANY)] * len(after),
        out_specs=[row, row], out_shape=[jax.ShapeDtypeStruct((t, d), BF16), jax.ShapeDtypeStruct((t, d), F32)],
        compiler_params=_params(("parallel",)))(x, ctx, nw, mod, *after)


def _rope_fwd(v, cos, sa, sb):
    return v * cos + pltpu.roll(v, 96, 1) * sa + pltpu.roll(v, 32, 1) * sb


def _rope_bwd(g, cos, sa, sb):
    return g * cos + pltpu.roll(g * sa, 32, 1) + pltpu.roll(g * sb, 96, 1)


N_PLAIN = 5


def _in_proj(hx, wg, cos, sa, sb, n_heads, tm, name, after=()):
    t, d = hx.shape
    n_seg, _, s = wg.shape
    nb = t // tm
    k_scale = HEAD_DIM ** -0.5

    def body(a_ref, w_ref, cos_ref, sa_ref, sb_ref, *rest):
        u_ref, qkv_ref = rest[-2:]
        g = pl.program_id(0)
        acc = _dot(a_ref[...], w_ref[...], NN)

        @pl.when(g < N_PLAIN)
        def _():
            u_ref[...] = acc

        @pl.when(g == N_PLAIN + 2)
        def _():
            qkv_ref[...] = acc.astype(BF16)

        for which, scale in ((N_PLAIN, 1.0), (N_PLAIN + 1, k_scale)):
            @pl.when(g == which)
            def _(scale=scale):
                co, a, b = cos_ref[...], sa_ref[...], sb_ref[...]
                for h in range(n_heads):
                    sl = slice(h * HEAD_DIM, (h + 1) * HEAD_DIM)
                    qkv_ref[:, sl] = (_rope_fwd(acc[:, sl], co, a, b) * scale).astype(BF16)

    def w_seg(g):
        return jnp.where(g < N_PLAIN - 1, g, jnp.where(g == N_PLAIN - 1, n_seg - 1, g - 1))

    tab = pl.BlockSpec((tm, HEAD_DIM), lambda g, i: (i, 0))
    return pl.pallas_call(
        body, name=name, grid=(n_seg, nb),
        in_specs=[pl.BlockSpec((tm, d), lambda g, i: (i, 0)), pl.BlockSpec((None, d, s), lambda g, i: (w_seg(g), 0, 0)),
                  tab, tab, tab] + [pl.BlockSpec(memory_space=pl.ANY)] * len(after),
        out_specs=[pl.BlockSpec((None, tm, s), lambda g, i: (jnp.minimum(g, N_PLAIN - 1), jnp.where(g < N_PLAIN, i, nb - 1), 0)),
                   pl.BlockSpec((None, tm, s), lambda g, i: (jnp.maximum(g - N_PLAIN, 0), jnp.where(g < N_PLAIN, 0, i), 0))],
        out_shape=[jax.ShapeDtypeStruct((N_PLAIN, t, s), F32), jax.ShapeDtypeStruct((3, t, s), BF16)],
        compiler_params=_params(("arbitrary", "arbitrary")))(hx, wg, cos, sa, sb, *after)


def _pair_sweep(xs, ys, tab_f, tab_b, cdf, cdb, n_heads, nx, ncc, reverse, name):
    t, s = xs[0].shape[-2:]
    nc = nx + ncc
    c = CHUNK
    n_pair = nc // 2
    assert nx % 2 == 0 and ncc % 2 == 0

    def f_pair(i):
        step = n_pair - 1 - i if reverse else i
        return jnp.where(step < ncc // 2, nx // 2 + step, step - ncc // 2)

    def b_pair(i):
        return i if reverse else n_pair - 1 - i

    f_subs = (1, 0) if reverse else (0, 1)
    b_subs = (0, 1) if reverse else (1, 0)

    def body(xf_ref, yf_ref, xb_ref, yb_ref, tf, tb, cdf_ref, cdb_ref, sf_out, sb_out, sf, sb):
        @pl.when(pl.program_id(0) == 0)
        def _():
            sf[...] = jnp.zeros_like(sf)
            sb[...] = jnp.zeros_like(sb)

        for step in range(2):
            for x_ref, y_ref, tab, cd, out, st, sub in ((xf_ref, yf_ref, tf, cdf_ref, sf_out, sf, f_subs[step]),
                                                        (xb_ref, yb_ref, tb, cdb_ref, sb_out, sb, b_subs[step])):
                rows = pl.ds(sub * c, c)
                for h in range(n_heads):
                    sl = pl.ds(h * HEAD_DIM, HEAD_DIM)
                    out[sub, h] = st[h].astype(BF16)
                    xd = (x_ref[rows, sl].astype(F32) * tab[h]).astype(BF16)
                    st[h] = cd[h][0:1, :] * st[h] + _dot(xd, y_ref[rows, sl], TN)

    def spec(arr, pair):
        lead = arr[1]
        if lead is None:
            return pl.BlockSpec((2 * c, s), lambda i: (pair(i), 0))
        return pl.BlockSpec((None, 2 * c, s), lambda i: (lead, pair(i), 0))

    st_blk = (2, n_heads, HEAD_DIM, HEAD_DIM)
    return pl.pallas_call(
        body, name=name, grid=(n_pair,),
        in_specs=[spec(xs, f_pair), spec(ys, f_pair), spec(xs, b_pair), spec(ys, b_pair),
                  _full((n_heads, c, HEAD_DIM)), _full((n_heads, c, HEAD_DIM)),
                  _full((n_heads, 8, HEAD_DIM)), _full((n_heads, 8, HEAD_DIM))],
        out_specs=[pl.BlockSpec(st_blk, lambda i: (f_pair(i), 0, 0, 0)), pl.BlockSpec(st_blk, lambda i: (b_pair(i), 0, 0, 0))],
        out_shape=[jax.ShapeDtypeStruct((nc, n_heads, HEAD_DIM, HEAD_DIM), BF16)] * 2,
        scratch_shapes=[pltpu.VMEM((n_heads, HEAD_DIM, HEAD_DIM), F32)] * 2,
        compiler_params=_params(("arbitrary",)),
    )(xs[0], ys[0], xs[0], ys[0], tab_f, tab_b, cdf, cdb)


def _state_sweep(qkv, tabs, n_heads, nx, ncc, name):
    return _pair_sweep((qkv, 1), (qkv, 2), tabs["kf"], tabs["kb"], tabs["cdf"], tabs["cdb"], n_heads, nx, ncc, False, name)


def _halo_specs(s, n8):
    per = CHUNK // 8

    def prev(g):
        return pl.BlockSpec((None, 8, s), lambda i: (g, jnp.maximum(i * per - 1, 0), 0))

    def nxt(g):
        return pl.BlockSpec((None, 8, s), lambda i: (g, jnp.minimum((i + 1) * per, n8 - 1), 0))

    return prev, nxt


def _shifted(a, before, after, has_prev, has_next):
    rows = a.shape[0]
    rowi = lax.broadcasted_iota(jnp.int32, a.shape, 0)
    am = jnp.where(rowi == 0, jnp.where(has_prev, before, 0.0), pltpu.roll(a, 1, 0))
    ap = jnp.where(rowi == rows - 1, jnp.where(has_next, after, 0.0), pltpu.roll(a, rows - 1, 0))
    return am, ap


def _neighbours(i, nx, nc):
    return (i != 0) & (i != nx), (i != nx - 1) & (i != nc - 1)


def _mix_fwd(u, qkv, sf, sb, tabs, conv_w, cnw, gnw, n_heads, nx, ncc, name):
    _, t, s = u.shape
    nc = nx + ncc
    c = CHUNK

    def body(h_ref, b_ref, c_ref, z_ref, rz_ref, hp_ref, hn_ref, cp_ref, cn_ref, q_ref, k_ref, v_ref,
             sf_ref, sb_ref, dc_ref, qft, qbt, w_ref, cnw_ref, gnw_ref, y_ref, o_ref):
        i = pl.program_id(0)
        has_prev, has_next = _neighbours(i, nx, nc)
        a = c_ref[...] * h_ref[...]
        am, ap = _shifted(a, cp_ref[7:8] * hp_ref[7:8], cn_ref[0:1] * hn_ref[0:1], has_prev, has_next)
        w = w_ref[...]
        y0 = w[0:1] * am + w[1:2] * a + w[2:3] * ap
        yb = b_ref[...] * y0
        r = lax.rsqrt(jnp.mean(yb * yb, axis=-1, keepdims=True) + EPS)
        y_ref[:, pl.ds(0, s)] = (_silu(z_ref[...]) * ((yb * r) * cnw_ref[...])).astype(BF16)
        for h in range(n_heads):
            sl = pl.ds(h * HEAD_DIM, HEAD_DIM)
            q, k, v = q_ref[:, sl], k_ref[:, sl], v_ref[:, sl]
            p = (_dot(q, k, NT) * dc_ref[h]).astype(BF16)
            o = _dot(p, v, NN)
            qf = q.astype(F32)
            o += _dot((qf * qft[h]).astype(BF16), sf_ref[h], NN)
            o += _dot((qf * qbt[h]).astype(BF16), sb_ref[h], NN)
            o_ref[:, sl] = o
            mu = jnp.mean(o, axis=-1, keepdims=True)
            var = jnp.mean(jnp.square(o - mu), axis=-1, keepdims=True)
            on = (o - mu) * lax.rsqrt(var + EPS)
            y_ref[:, pl.ds(s + h * HEAD_DIM, HEAD_DIM)] = (
                _silu(rz_ref[:, sl]) * (on * gnw_ref[:, sl])).astype(BF16)

    def seg(g):
        return pl.BlockSpec((None, c, s), lambda i: (g, i, 0))

    prev, nxt = _halo_specs(s, t // 8)
    row = pl.BlockSpec((c, s), lambda i: (i, 0))
    st = pl.BlockSpec((None, n_heads, HEAD_DIM, HEAD_DIM), lambda i: (i, 0, 0, 0))
    return pl.pallas_call(
        body, name=name, grid=(nc,),
        in_specs=[seg(0), seg(1), seg(2), seg(3), seg(4), prev(0), nxt(0), prev(2), nxt(2), seg(0), seg(1), seg(2),
                  st, st, _full((n_heads, c, c)), _full((n_heads, c, HEAD_DIM)), _full((n_heads, c, HEAD_DIM)),
                  _full((3, s)), _full((1, s)), _full((1, s))],
        out_specs=[pl.BlockSpec((c, 2 * s), lambda i: (i, 0)), row],
        out_shape=[jax.ShapeDtypeStruct((t, 2 * s), BF16), jax.ShapeDtypeStruct((t, s), F32)],
        compiler_params=_params(("parallel",)),
    )(u, u, u, u, u, u, u, u, u, qkv, qkv, qkv, sf, sb, tabs["dc"], tabs["qf"], tabs["qb"], conv_w, cnw, gnw)


def _row_gate(mod_ref, row0, rows, n_lat, col):
    rowi = row0 + lax.broadcasted_iota(jnp.int32, (rows, 1), 0)
    return jnp.where(rowi >= n_lat, mod_ref[5:6, col], mod_ref[2:3, col])


def _out_proj(ycat, w_out, xt, mod, n_lat, tm, tn, name):
    t, d = xt.shape

    def body(a_ref, w_ref, x_ref, mod_ref, m_ref, xo_ref):
        m = _dot(a_ref[...], w_ref[...], NN)
        m_ref[...] = m
        gate = _row_gate(mod_ref, pl.program_id(1) * tm, tm, n_lat, slice(None))
        xo_ref[...] = x_ref[...] + gate * m

    blk = pl.BlockSpec((tm, tn), lambda j, i: (i, j))
    return pl.pallas_call(
        body, name=name, grid=(d // tn, t // tm),
        in_specs=[pl.BlockSpec((tm, d), lambda j, i: (i, 0)), pl.BlockSpec((d, tn), lambda j, i: (0, j)), blk,
                  pl.BlockSpec((8, tn), lambda j, i: (0, j))],
        out_specs=[blk, blk], out_shape=[jax.ShapeDtypeStruct((t, d), F32)] * 2,
        compiler_params=_params(("parallel", "parallel")))(ycat, w_out, xt, mod)


def _gate_bwd_rows(dx, m_ref, mod_ref, dm_ref, gacc_ref, ctx):
    g_row = 5 if ctx else 2
    dm_ref[...] = (dx * mod_ref[g_row:g_row + 1, :]).astype(BF16)
    gacc_ref[g_row:g_row + 1, :] += jnp.sum(dx * m_ref[...], axis=0, keepdims=True)


def _loss_head(x2, tgt, fnw, m, mod, n_lat, name):
    t, d = x2.shape
    nxb = n_lat // ROW_TILE

    def body(x_ref, t_ref, w_ref, m_ref, mod_ref, dx_ref, dm_ref, loss_ref, dw_ref, gacc_ref):
        i = pl.program_id(0)

        @pl.when(i == 0)
        def _():
            loss_ref[...] = jnp.zeros_like(loss_ref)
            dw_ref[...] = jnp.zeros_like(dw_ref)
            gacc_ref[...] = jnp.zeros_like(gacc_ref)

        @pl.when(i < nxb)
        def _():
            x = x_ref[...]
            w = w_ref[...]
            r = lax.rsqrt(jnp.mean(x * x, axis=-1, keepdims=True) + EPS)
            xn = x * r
            e = xn * w - t_ref[...]
            loss_ref[...] += 0.5 * jnp.sum(jnp.mean(e * e, axis=-1, keepdims=True), axis=0, keepdims=True)
            dy = e * (1.0 / d)
            dw_ref[0:1, :] += jnp.sum(dy * xn, axis=0, keepdims=True)
            dxn = dy * w
            dx = r * (dxn - xn * jnp.mean(dxn * xn, axis=-1, keepdims=True))
            dx_ref[...] = dx
            _gate_bwd_rows(dx, m_ref, mod_ref, dm_ref, gacc_ref, False)

        @pl.when(i >= nxb)
        def _():
            dx_ref[...] = jnp.zeros_like(dx_ref)
            dm_ref[...] = jnp.zeros_like(dm_ref)

    row = pl.BlockSpec((ROW_TILE, d), lambda i: (i, 0))
    return pl.pallas_call(
        body, name=name, grid=(t // ROW_TILE,),
        in_specs=[row, pl.BlockSpec((ROW_TILE, d), lambda i: (jnp.minimum(i, nxb - 1), 0)), _full((1, d)), row,
                  _full((8, d))],
        out_specs=[row, row, _full((8, HEAD_DIM)), _full((8, d)), _full((8, d))],
        out_shape=[jax.ShapeDtypeStruct((t, d), F32), jax.ShapeDtypeStruct((t, d), BF16),
                   jax.ShapeDtypeStruct((8, HEAD_DIM), F32), jax.ShapeDtypeStruct((8, d), F32),
                   jax.ShapeDtypeStruct((8, d), F32)],
        compiler_params=_params(("arbitrary",)))(x2, tgt, fnw, m, mod)


def _matmul_nt(a, w, tn, name):
    t, k = a.shape
    n = w.shape[0]
    tm = _mm_rows(t)

    def body(a_ref, w_ref, o_ref):
        o_ref[...] = _dot(a_ref[...], w_ref[...], NT)

    return pl.pallas_call(
        body, name=name, grid=(n // tn, t // tm),
        in_specs=[pl.BlockSpec((tm, k), lambda j, i: (i, 0)), pl.BlockSpec((tn, k), lambda j, i: (j, 0))],
        out_specs=pl.BlockSpec((tm, tn), lambda j, i: (i, j)),
        out_shape=jax.ShapeDtypeStruct((t, n), F32),
        compiler_params=_params(("parallel", "parallel")))(a, w)


def _weight_grad(a, b, bm, name):
    t, m = a.shape
    n_g, _, n = b.shape
    bt = _mm_rows(t)
    nt = t // bt

    def body(a_ref, b_ref, o_ref, acc):
        k = pl.program_id(2)

        @pl.when(k == 0)
        def _():
            acc[...] = jnp.zeros_like(acc)

        acc[...] += _dot(a_ref[...], b_ref[...], TN)

        @pl.when(k == nt - 1)
        def _():
            o_ref[...] = acc[...].astype(o_ref.dtype)

    return pl.pallas_call(
        body, name=name, grid=(n_g, m // bm, nt),
        in_specs=[pl.BlockSpec((bt, bm), lambda g, i, k: (k, i)), pl.BlockSpec((None, bt, n), lambda g, i, k: (g, k, 0))],
        out_specs=pl.BlockSpec((None, bm, n), lambda g, i, k: (g, i, 0)),
        out_shape=jax.ShapeDtypeStruct((n_g, m, n), BF16),
        scratch_shapes=[pltpu.VMEM((bm, n), F32)],
        compiler_params=_params(("parallel", "parallel", "arbitrary")))(a, b)


def _mix_bwd_a(dycat, u, o, conv_w, cnw, gnw, n_heads, nx, ncc, name):
    _, t, s = u.shape
    nc = nx + ncc
    c = CHUNK

    def body(dy_ref, h_ref, b_ref, c_ref, z_ref, rz_ref, hp_ref, hn_ref, cp_ref, cn_ref, o_ref, w_ref,
             cnw_ref, gnw_ref, g_ref, dz_ref, db_ref, drz_ref, do_ref, acc_ref):
        i = pl.program_id(0)

        @pl.when(i == 0)
        def _():
            acc_ref[...] = jnp.zeros_like(acc_ref)

        has_prev, has_next = _neighbours(i, nx, nc)
        a = c_ref[...] * h_ref[...]
        am, ap = _shifted(a, cp_ref[7:8] * hp_ref[7:8], cn_ref[0:1] * hn_ref[0:1], has_prev, has_next)
        w = w_ref[...]
        y0 = w[0:1] * am + w[1:2] * a + w[2:3] * ap
        bb = b_ref[...]
        yb = bb * y0
        r = lax.rsqrt(jnp.mean(yb * yb, axis=-1, keepdims=True) + EPS)
        ynn = yb * r
        z = z_ref[...]
        dyc = dy_ref[:, pl.ds(0, s)]
        cw = cnw_ref[...]
        dz_ref[...] = (dyc * (ynn * cw) * _dsilu(z)).astype(BF16)
        dyn = dyc * _silu(z)
        acc_ref[0:1, :] += jnp.sum(dyn * ynn, axis=0, keepdims=True)
        dynn = dyn * cw
        dyb = r * (dynn - ynn * jnp.mean(dynn * ynn, axis=-1, keepdims=True))
        db_ref[...] = (dyb * y0).astype(BF16)
        g_ref[...] = dyb * bb
        for h in range(n_heads):
            sl = pl.ds(h * HEAD_DIM, HEAD_DIM)
            ov = o_ref[:, sl]
            mu = jnp.mean(ov, axis=-1, keepdims=True)
            var = jnp.mean(jnp.square(ov - mu), axis=-1, keepdims=True)
            rs = lax.rsqrt(var + EPS)
            on = (ov - mu) * rs
            dyr = dy_ref[:, pl.ds(s + h * HEAD_DIM, HEAD_DIM)]
            rz = rz_ref[:, sl]
            gw = gnw_ref[:, sl]
            drz_ref[:, sl] = (dyr * (on * gw) * _dsilu(rz)).astype(BF16)
            dyg = dyr * _silu(rz)
            acc_ref[1:2, sl] += jnp.sum(dyg * on, axis=0, keepdims=True)
            don = dyg * gw
            do = rs * (don - jnp.mean(don, axis=-1, keepdims=True)
                       - on * jnp.mean(don * on, axis=-1, keepdims=True))
            do_ref[:, sl] = do.astype(BF16)

    def seg(g):
        return pl.BlockSpec((None, c, s), lambda i: (g, i, 0))

    prev, nxt = _halo_specs(s, t // 8)
    row = pl.BlockSpec((c, s), lambda i: (i, 0))
    return pl.pallas_call(
        body, name=name, grid=(nc,),
        in_specs=[pl.BlockSpec((c, 2 * s), lambda i: (i, 0)), seg(0), seg(1), seg(2), seg(3), seg(4),
                  prev(0), nxt(0), prev(2), nxt(2), row, _full((3, s)), _full((1, s)), _full((1, s))],
        out_specs=[row, row, row, row, row, _full((8, s))],
        out_shape=[jax.ShapeDtypeStruct((t, s), F32)] + [jax.ShapeDtypeStruct((t, s), BF16)] * 4
        + [jax.ShapeDtypeStruct((8, s), F32)],
        compiler_params=_params(("arbitrary",)),
    )(dycat, u, u, u, u, u, u, u, u, u, o, conv_w, cnw, gnw)


def _grad_state_sweep(qkv, do, tabs, n_heads, nx, ncc, name):
    return _pair_sweep((qkv, 0), (do, None), tabs["qf"], tabs["qb"], tabs["cdf"], tabs["cdb"], n_heads, nx, ncc, True, name)


def _mix_bwd_b(u, g, dz, db, drz, qkv, do, sf, sb, gf, gb, tabs, cos, sa, sb_tab, conv_w,
               n_heads, nx, ncc, name):
    _, t, s = u.shape
    nc = nx + ncc
    c = CHUNK
    k_scale = HEAD_DIM ** -0.5

    def body(h_ref, c_ref, g_ref, gp_ref, gn_ref, dz_ref, db_ref, drz_ref, q_ref, k_ref, v_ref, do_ref,
             sf_ref, sb_ref, gf_ref, gb_ref, dc_t, dlf_t, dlb_t, qft, kft, qbt, kbt, cdf, cdb, lg_ref,
             cos_ref, sa_ref, sb_ref2, w_ref, du_ref, dw_ref, dlg_ref):
        i = pl.program_id(0)

        @pl.when(i == 0)
        def _():
            dw_ref[...] = jnp.zeros_like(dw_ref)
            dlg_ref[...] = jnp.zeros_like(dlg_ref)

        has_prev, has_next = _neighbours(i, nx, nc)
        gv = g_ref[...]
        gm, gp = _shifted(gv, gp_ref[7:8], gn_ref[0:1], has_prev, has_next)
        w = w_ref[...]
        da = w[0:1] * gp + w[1:2] * gv + w[2:3] * gm
        hh, cc = h_ref[...], c_ref[...]
        du_ref[0] = (da * cc).astype(BF16)
        du_ref[2] = (da * hh).astype(BF16)
        a = cc * hh
        dw_ref[0:1, :] += jnp.sum(a * gp, axis=0, keepdims=True)
        dw_ref[1:2, :] += jnp.sum(a * gv, axis=0, keepdims=True)
        dw_ref[2:3, :] += jnp.sum(a * gm, axis=0, keepdims=True)
        du_ref[1] = db_ref[...]
        du_ref[3] = dz_ref[...]
        du_ref[7] = drz_ref[...]

        co, ra, rb = cos_ref[...], sa_ref[...], sb_ref2[...]
        pos = lax.broadcasted_iota(jnp.int32, (c, HEAD_DIM), 0).astype(F32)
        row8 = lax.broadcasted_iota(jnp.int32, (8, HEAD_DIM), 0)
        lane8 = lax.broadcasted_iota(jnp.int32, (8, HEAD_DIM), 1)
        dlg = jnp.zeros((8, HEAD_DIM), F32)
        for h in range(n_heads):
            sl = pl.ds(h * HEAD_DIM, HEAD_DIM)
            q, k, v, do = q_ref[:, sl], k_ref[:, sl], v_ref[:, sl], do_ref[:, sl]
            qf, kf, dof = q.astype(F32), k.astype(F32), do.astype(F32)
            s_f, s_b, g_f, g_b = sf_ref[h], sb_ref[h], gf_ref[h], gb_ref[h]
            p = _dot(q, k, NT)
            pd = _dot(do, v, NT)
            pdd = (pd * dc_t[h]).astype(BF16)
            dq = _dot(pdd, k, NN)
            dk = _dot(pdd, q, TN)
            dv = _dot((p * dc_t[h]).astype(BF16), do, TN)
            dq_f = _dot((dof * qft[h]).astype(BF16), s_f, NT)
            dq_b = _dot((dof * qbt[h]).astype(BF16), s_b, NT)
            dk_f = _dot(v, g_f, NT) * kft[h]
            dk_b = _dot(v, g_b, NT) * kbt[h]
            dv += _dot((kf * kft[h]).astype(BF16), g_f, NN) + _dot((kf * kbt[h]).astype(BF16), g_b, NN)
            ppd = p * pd
            cd_f, cd_b = cdf[h][0:1, :], cdb[h][0:1, :]
            t_f = _sum_all(dlf_t[h] * ppd + (pos + 1.0) * qf * dq_f + (c - 1.0 - pos) * kf * dk_f
                           + float(c) * (cd_f * (g_f.astype(F32) * s_f.astype(F32))))
            t_b = _sum_all(dlb_t[h] * ppd + (c - pos) * qf * dq_b + pos * kf * dk_b
                           + float(c) * (cd_b * (g_b.astype(F32) * s_b.astype(F32))))
            dlg += jnp.where((row8 == 0) & (lane8 == h), t_f, 0.0) + jnp.where((row8 == 1) & (lane8 == h), t_b, 0.0)
            du_ref[4, :, sl] = _rope_bwd(dq + dq_f + dq_b, co, ra, rb).astype(BF16)
            du_ref[5, :, sl] = (_rope_bwd(dk + dk_f + dk_b, co, ra, rb) * k_scale).astype(BF16)
            du_ref[6, :, sl] = dv.astype(BF16)
        dlg_ref[...] += dlg

        @pl.when(i == nc - 1)
        def _():
            dlg_ref[...] = dlg_ref[...] * lg_ref[...]

    def seg(gi):
        return pl.BlockSpec((None, c, s), lambda i: (gi, i, 0))

    per = c // 8
    n8 = t // 8
    row = pl.BlockSpec((c, s), lambda i: (i, 0))
    st = pl.BlockSpec((None, n_heads, HEAD_DIM, HEAD_DIM), lambda i: (i, 0, 0, 0))
    tab = pl.BlockSpec((c, HEAD_DIM), lambda i: (i, 0))
    hc = _full((n_heads, c, HEAD_DIM))
    cc_ = _full((n_heads, c, c))
    h8 = _full((n_heads, 8, HEAD_DIM))
    return pl.pallas_call(
        body, name=name, grid=(nc,),
        in_specs=[seg(0), seg(2), row,
                  pl.BlockSpec((8, s), lambda i: (jnp.maximum(i * per - 1, 0), 0)),
                  pl.BlockSpec((8, s), lambda i: (jnp.minimum((i + 1) * per, n8 - 1), 0)),
                  row, row, row, seg(0), seg(1), seg(2), row, st, st, st, st, cc_, cc_, cc_, hc, hc, hc, hc, h8, h8,
                  _full((8, HEAD_DIM)), tab, tab, tab, _full((3, s))],
        out_specs=[pl.BlockSpec((8, c, s), lambda i: (0, i, 0)), _full((8, s)), _full((8, HEAD_DIM))],
        out_shape=[jax.ShapeDtypeStruct((8, t, s), BF16), jax.ShapeDtypeStruct((8, s), F32),
                   jax.ShapeDtypeStruct((8, HEAD_DIM), F32)],
        compiler_params=_params(("arbitrary",)),
    )(u, u, g, g, g, dz, db, drz, qkv, qkv, qkv, do, sf, sb, gf, gb, tabs["dc"], tabs["dlf"], tabs["dlb"],
      tabs["qf"], tabs["kf"], tabs["qb"], tabs["kb"], tabs["cdf"], tabs["cdb"], tabs["lg"], cos, sa, sb_tab, conv_w)


def _in_proj_bwd(du, wg, tm, gs, name, after=()):
    n_seg, t, s = du.shape
    d = wg.shape[1]

    def body(a_ref, w_ref, *rest):
        o_ref = rest[-1]
        g = pl.program_id(1)
        part = _dot(a_ref[0], w_ref[0], NT)
        for j in range(1, gs):
            part += _dot(a_ref[j], w_ref[j], NT)

        @pl.when(g == 0)
        def _():
            o_ref[...] = part

        @pl.when(g > 0)
        def _():
            o_ref[...] += part

    return pl.pallas_call(
        body, name=name, grid=(t // tm, n_seg // gs),
        in_specs=[pl.BlockSpec((gs, tm, s), lambda i, g: (g, i, 0)), pl.BlockSpec((gs, d, s), lambda i, g: (g, 0, 0))]
        + [pl.BlockSpec(memory_space=pl.ANY)] * len(after),
        out_specs=pl.BlockSpec((tm, d), lambda i, g: (i, 0)),
        out_shape=jax.ShapeDtypeStruct((t, d), F32),
        compiler_params=_params(("parallel", "arbitrary")))(du, wg, *after)


def _prenorm_bwd(dhx, xt, dxo, nw, mod, n_lat, below, name):
    t, d = xt.shape
    nxb = n_lat // ROW_TILE
    first = below is None

    def body(dh_ref, x_ref, dxo_ref, nw_ref, mod_ref, *rest):
        if first:
            dx_ref, acc_ref = rest
        else:
            m_ref, modb_ref, dx_ref, acc_ref, dm_ref, gacc_ref = rest
        i = pl.program_id(0)

        @pl.when(i == 0)
        def _():
            acc_ref[...] = jnp.zeros_like(acc_ref)
            if not first:
                gacc_ref[...] = jnp.zeros_like(gacc_ref)

        ctx = i >= nxb
        m = mod_ref[...]
        scale1 = 1.0 + jnp.where(ctx, m[4:5], m[1:2])
        x = x_ref[...]
        nw_v = nw_ref[...]
        r = lax.rsqrt(jnp.mean(x * x, axis=-1, keepdims=True) + EPS)
        xn = x * r
        dh = dh_ref[...]
        dshift = jnp.sum(dh, axis=0, keepdims=True)
        dscale = jnp.sum(dh * (xn * nw_v), axis=0, keepdims=True)
        acc_ref[6:7, :] += jnp.sum(dh * scale1 * xn, axis=0, keepdims=True)
        dxn = dh * (nw_v * scale1)
        dx = dxo_ref[...] + r * (dxn - xn * jnp.mean(dxn * xn, axis=-1, keepdims=True))

        @pl.when(i < nxb)
        def _():
            acc_ref[0:1, :] += dshift
            acc_ref[1:2, :] += dscale
            dx_ref[...] = dx
            if not first:
                _gate_bwd_rows(dx, m_ref, modb_ref, dm_ref, gacc_ref, False)

        @pl.when(i >= nxb)
        def _():
            acc_ref[3:4, :] += dshift
            acc_ref[4:5, :] += dscale
            if not first:
                dx_ref[...] = dx
                _gate_bwd_rows(dx, m_ref, modb_ref, dm_ref, gacc_ref, True)

    row = pl.BlockSpec((ROW_TILE, d), lambda i: (i, 0))
    acc = _full((8, d))
    if first:
        extra_in, extra = [], []
        dx_spec = pl.BlockSpec((ROW_TILE, d), lambda i: (jnp.minimum(i, nxb - 1), 0))
        dx_shape = jax.ShapeDtypeStruct((n_lat, d), F32)
        extra_out, extra_shape = [], []
    else:
        extra_in, extra = [row, acc], list(below)
        dx_spec, dx_shape = row, jax.ShapeDtypeStruct((t, d), F32)
        extra_out, extra_shape = [row, acc], [jax.ShapeDtypeStruct((t, d), BF16), jax.ShapeDtypeStruct((8, d), F32)]
    return pl.pallas_call(body, name=name, grid=(t // ROW_TILE,),
                          in_specs=[row, row, row, _full((1, d)), acc] + extra_in,
                          out_specs=[dx_spec, acc] + extra_out,
                          out_shape=[dx_shape, jax.ShapeDtypeStruct((8, d), F32)] + extra_shape,
                          compiler_params=_params(("arbitrary",)))(dhx, xt, dxo, nw, mod, *extra)


def _adamw(g, w, m, v):
    m = ADAM_B1 * m + (1.0 - ADAM_B1) * g
    v = ADAM_B2 * v + (1.0 - ADAM_B2) * jnp.square(g)
    m_hat = m / (1.0 - ADAM_B1 ** ADAM_STEP)
    v_hat = v / (1.0 - ADAM_B2 ** ADAM_STEP)
    delta = -ADAM_LR * (m_hat / (jnp.sqrt(v_hat) + ADAM_EPS) + ADAM_WD * w)
    return delta, m, v


def _sum_adamw(parts, w, m, v, name, row0=0, into=None):
    n_p, r, n = parts.shape
    r_all = w.shape[0]
    part_block_bytes = 4 * 1024 * 1024
    br = 8
    for cand in (512, 256, 128, 64, 32, 16):
        if r % cand == 0 and row0 % cand == 0 and n_p * cand * n * parts.dtype.itemsize <= part_block_bytes:
            br = cand
            break
    blk0 = row0 // br

    def body(p_ref, w_ref, m_ref, v_ref, *rest):
        g_out, d_out, m_out, v_out = rest[-4:]
        g = p_ref[0].astype(F32)
        for j in range(1, n_p):
            g = g + p_ref[j].astype(F32)
        g_out[...] = g
        d_out[...], m_out[...], v_out[...] = _adamw(g, w_ref[...], m_ref[...], v_ref[...])

    row = pl.BlockSpec((br, n), lambda i: (i + blk0, 0))
    kept = [] if into is None else list(into)
    return pl.pallas_call(body, name=name, grid=(r // br,),
                          in_specs=[pl.BlockSpec((n_p, br, n), lambda i: (0, i, 0)), row, row, row]
                          + [pl.BlockSpec(memory_space=pl.ANY)] * len(kept),
                          out_specs=[row] * 4, out_shape=[jax.ShapeDtypeStruct((r_all, n), F32)] * 4,
                          input_output_aliases={4 + j: j for j in range(len(kept))},
                          compiler_params=_params(("parallel",)))(parts, w, m, v, *kept)


def _rope_tables(n_lat, n_ctx):
    f = HEAD_DIM // 4
    rows = n_lat // GRID_W
    inv = ROPE_BASE ** (-jnp.arange(f, dtype=F32) / f)
    ang_r = jnp.arange(rows).astype(F32)[:, None] * inv[None, :]
    ang_c = jnp.arange(GRID_W).astype(F32)[:, None] * inv[None, :]

    def by_row(a):
        return jnp.broadcast_to(a[:, None, :], (rows, GRID_W, f)).reshape(n_lat, f)

    def by_col(a):
        return jnp.broadcast_to(a[None, :, :], (rows, GRID_W, f)).reshape(n_lat, f)

    cr, sr, cc, sc = by_row(jnp.cos(ang_r)), by_row(jnp.sin(ang_r)), by_col(jnp.cos(ang_c)), by_col(jnp.sin(ang_c))
    zero = jnp.zeros_like(cr)
    cos = jnp.concatenate([cr, cr, cc, cc], axis=-1)
    sa = jnp.concatenate([-sr, zero, -sc, zero], axis=-1)
    sb = jnp.concatenate([zero, sr, zero, sc], axis=-1)
    pad = jnp.zeros((n_ctx, HEAD_DIM), F32)
    return (jnp.concatenate([cos, pad + 1.0], axis=0), jnp.concatenate([sa, pad], axis=0),
            jnp.concatenate([sb, pad], axis=0))


def _pad_rows(a, rows):
    return jnp.pad(a, [(0, rows - a.shape[0])] + [(0, 0)] * (a.ndim - 1))


def _pad_cols(a, cols):
    return jnp.pad(a, [(0, 0), (0, cols - a.shape[1])])


def kernel(x, c, ctx, c_ctx, norm_w, w_mod, b_mod, w_in, conv_w, conv_norm_w, ret_norm_w, ret_decay_f, ret_decay_b, w_out, final_norm_w, loss_target, m_c_ctx, m_norm_w, m_w_mod, m_b_mod, m_w_in, m_conv_w, m_conv_norm_w, m_ret_norm_w, m_ret_decay_f, m_ret_decay_b, m_w_out, m_final_norm_w, v_c_ctx, v_norm_w, v_w_mod, v_b_mod, v_w_in, v_conv_w, v_conv_norm_w, v_ret_norm_w, v_ret_decay_f, v_ret_decay_b, v_w_out, v_final_norm_w):
    depth = norm_w.shape[0]
    n_lat, d = x.shape[1], x.shape[2]
    n_ctx = ctx.shape[1]
    s = d // 2
    n_heads = ret_decay_f.shape[1]
    nx, ncc = n_lat // CHUNK, n_ctx // CHUNK
    n_mod = w_mod.shape[2]
    n_cw = conv_w.shape[2]
    r_out = w_out.shape[1]
    assert s == n_heads * HEAD_DIM and w_in.shape[2] == s and N_DEV * r_out == d
    assert n_lat % ROW_TILE == 0 and n_ctx % ROW_TILE == 0 and 3 * depth * n_cw <= d and d >= 3 * n_mod // 3
    me = 4 * lax.axis_index("x") + 2 * lax.axis_index("y") + lax.axis_index("c")

    w_in_bf = [w_in[l].astype(BF16) for l in range(depth)]
    w_out_bf = [w_out[l].astype(BF16) for l in range(depth)]

    first = jnp.concatenate([c.reshape(1, d), _pad_cols(conv_w.reshape(1, -1), d), jnp.zeros((6, d), F32)], axis=0)
    (first_g,) = _all_gather([first], "gather_cond", True)
    first_g = first_g.reshape(N_DEV, 8, d)
    c_all = first_g[:, 0, :]
    conv_full = first_g[:, 1, :3 * depth * n_cw].reshape(N_DEV, depth, 3, n_cw)
    conv_full = conv_full.transpose(1, 2, 0, 3).reshape(depth, 3, N_DEV * n_cw)
    c9 = jnp.concatenate([c_all, c_ctx.reshape(1, d), jnp.zeros((7, d), F32)], axis=0)

    b_sh = lax.dynamic_slice(b_mod, (0, me * n_mod), (depth, n_mod))
    mod_sh = jnp.concatenate([_mod_rows(c9, w_mod[l], b_sh[l:l + 1], f"mod_rows_l{l}") for l in range(depth)], axis=0)
    (mod_g,) = _all_gather([mod_sh], "gather_mod", True)
    mod_g = mod_g.reshape(N_DEV, depth, 16, n_mod)
    mods = []
    for l in range(depth):
        mine = lax.dynamic_index_in_dim(mod_g[:, l], me, axis=1, keepdims=False).reshape(3, d)
        cx = mod_g[:, l, 8, :].reshape(3, d)
        mods.append(jnp.concatenate([mine, cx, jnp.zeros((2, d), F32)], axis=0))

    near = _push_start([w_in_bf[0]], [_landing(w_in_bf[0], me)], "near", "w_in0_start", after=[mod_g])
    w_in_g = [None] * depth
    w_out_g = [None] * depth
    pending, tokens = [], []

    cos, sa, sb_tab = _rope_tables(n_lat, n_ctx)
    t_all = n_lat + n_ctx

    saved = []
    xt = None
    for l in range(depth):
        tiles = _tiles(l, t_all, d)
        names = ["dc", "dlf", "dlb", "qf", "kf", "qb", "kb", "cdf", "cdb", "lg"]
        dec = jnp.stack([ret_decay_f[l], ret_decay_b[l]], axis=0)
        tabs = dict(zip(names, _decay_tables(dec, n_heads, f"decay_tables_l{l}")))
        if l == 0:
            hx, xt = _prenorm_first(x[0], ctx[0], norm_w[0:1], mods[0], "prenorm_l0", after=near[4:])
            (w_in0_near,) = _push_wait(*near[:4], "near", hx, "w_in0_wait")
            relay = _push_start([], [w_in0_near], "relay", "w_in0_relay_start")
            (w_in_g[0],) = _push_wait(*relay[:4], "relay", relay[4], "w_in0_relay_wait")
            tokens = [w_in_g[0]]
            for k in range(depth):
                srcs = [w_out_bf[k]] + ([w_in_bf[k]] if k > 0 else [])
                started = _push_start(srcs, [_landing(a, me) for a in srcs], "gather", f"weights_start_l{k}",
                                      after=tokens[-1:])
                pending.append(started[:4])
                tokens.append(started[4])
        else:
            landed = _push_wait(*pending[l], "gather", xt, f"weights_wait_l{l}")
            w_out_g[l], w_in_g[l] = landed[0].reshape(d, d), landed[1]
            hx = _prenorm(xt, norm_w[l:l + 1], mods[l], n_lat, f"prenorm_l{l}")
        u, qkv = _in_proj(hx, w_in_g[l], cos, sa, sb_tab, n_heads, tiles["in_tm"], f"in_proj_l{l}",
                          after=tokens[1:] if l == 0 else ())
        sf, sb = _state_sweep(qkv, tabs, n_heads, nx, ncc, f"state_sweep_l{l}")
        ycat, o = _mix_fwd(u, qkv, sf, sb, tabs, conv_full[l], conv_norm_w[l:l + 1], ret_norm_w[l:l + 1],
                           n_heads, nx, ncc, f"mix_fwd_l{l}")
        if l == 0:
            (landed,) = _push_wait(*pending[0], "gather", ycat, "weights_wait_l0")
            w_out_g[0] = landed.reshape(d, d)
        m_res, x_new = _out_proj(ycat, w_out_g[l], xt, mods[l], n_lat, tiles["out_tm"], tiles["out_tn"], f"out_proj_l{l}")
        saved.append(dict(tabs=tabs, xt=xt, hx=hx, u=u, qkv=qkv, sf=sf, sb=sb, ycat=ycat, o=o, m=m_res, tiles=tiles))
        xt = x_new

    top = saved[-1]
    dxt, dm, loss_blk, dfnw, gate_acc = _loss_head(xt, loss_target[0], final_norm_w.reshape(1, d), top["m"],
                                                   mods[depth - 1], n_lat, "loss_head")
    loss = lax.psum(loss_blk[0, 0], MESH_AXES)

    dmod_x, dmod_c, dnw, dcnw, dgnw, dconv, ddec, dwin, dwout = [], [], [], [], [], [], [], [], []
    for l in reversed(range(depth)):
        sv = saved[l]
        tiles = sv["tiles"]
        dycat = _matmul_nt(dm, w_out_g[l], tiles["ob_tn"], f"out_proj_bwd_l{l}")
        dwout.append(_weight_grad(sv["ycat"], dm.reshape(1, *dm.shape), tiles["wo_bm"], f"w_out_grad_l{l}")[0])
        g, dz, db, drz, do, norm_acc = _mix_bwd_a(dycat, sv["u"], sv["o"], conv_full[l], conv_norm_w[l:l + 1],
                                                   ret_norm_w[l:l + 1], n_heads, nx, ncc, f"mix_bwd_a_l{l}")
        gf, gb = _grad_state_sweep(sv["qkv"], do, sv["tabs"], n_heads, nx, ncc, f"grad_state_sweep_l{l}")
        du, conv_acc, dlg = _mix_bwd_b(sv["u"], g, dz, db, drz, sv["qkv"], do, sv["sf"], sv["sb"],
                                       gf, gb, sv["tabs"], cos, sa, sb_tab, conv_full[l], n_heads, nx, ncc,
                                       f"mix_bwd_b_l{l}")
        srcs = [_weight_grad(sv["hx"], du, tiles["wg_bm"], f"w_in_grad_l{l}"), dwout[-1].reshape(N_DEV, r_out, d)]
        lands = [_landing(lax.dynamic_index_in_dim(a, me, axis=0, keepdims=False), me) for a in srcs]
        started = _push_start(srcs, lands, "scatter", f"grads_start_l{l}")
        dwin.append(started[:4])
        dhx = _in_proj_bwd(du, w_in_g[l], tiles["bwd_tm"], tiles["bwd_gs"], f"in_proj_bwd_l{l}", after=started[4:])
        this_gate = gate_acc
        if l > 0:
            below = (saved[l - 1]["m"], mods[l - 1])
            dxt, pre_acc, dm, gate_acc = _prenorm_bwd(dhx, sv["xt"], dxt, norm_w[l:l + 1], mods[l], n_lat, below,
                                                      f"prenorm_bwd_l{l}")
        else:
            dxt, pre_acc = _prenorm_bwd(dhx, sv["xt"], dxt, norm_w[l:l + 1], mods[l], n_lat, None, f"prenorm_bwd_l{l}")
        gate_acc_l = this_gate
        dmod_x.append(jnp.concatenate([pre_acc[0], pre_acc[1], gate_acc_l[2]]))
        dmod_c.append(jnp.concatenate([pre_acc[3], pre_acc[4], gate_acc_l[5]]))
        dnw.append(pre_acc[6])
        dcnw.append(norm_acc[0])
        dgnw.append(norm_acc[1])
        dconv.append(conv_acc[0:3])
        ddec.append(dlg[0:2, :n_heads])
    for lst in (dmod_x, dmod_c, dnw, dcnw, dgnw, dconv, ddec, dwin, dwout):
        lst.reverse()
    grad_x = dxt.reshape(1, n_lat, d)

    rows = []
    for l in range(depth):
        rows += [dmod_x[l], dmod_c[l]]
    (dmod_g,) = _all_gather([_pad_rows(jnp.stack(rows, axis=0), 8)], "gather_dmod", True)
    dmod_g = dmod_g.reshape(N_DEV, 8, 3 * d)
    mine_cols = lax.dynamic_slice(dmod_g, (0, 0, me * n_mod), (N_DEV, 8, n_mod))
    g_wmod, dcc = [], jnp.zeros((d,), F32)
    for l in range(depth):
        gw, dc_part = _mod_grads(mine_cols[:, 2 * l], mine_cols[:, 2 * l + 1], c9, w_mod[l], f"mod_grads_l{l}")
        g_wmod.append(gw)
        dcc = dcc + dc_part[0]

    n_small = 16
    small = jnp.concatenate([
        jnp.stack(dnw, axis=0),
        jnp.concatenate(dcnw).reshape(1, -1),
        jnp.concatenate(dgnw).reshape(1, -1),
        dfnw[0:1],
        dcc.reshape(1, d),
        jnp.stack(dconv, axis=0).reshape(-1, d),
        _pad_cols(jnp.stack(ddec, axis=0).reshape(1, -1), d),
    ], axis=0)
    assert depth * s == d and small.shape[0] <= n_small
    n_rows = small.shape[0]
    (small_g,) = _all_gather([_pad_rows(small, n_small)], "gather_small", True)
    small_g = small_g.reshape(N_DEV, n_small, d)

    def pack_small(nw_, cn_, gn_, fn_, cc_, df_, db_):
        return _pad_rows(jnp.concatenate([
            nw_, cn_.reshape(1, -1), gn_.reshape(1, -1), fn_.reshape(1, d), cc_.reshape(1, d),
            jnp.zeros((n_rows - depth - 5, d), F32),
            _pad_cols(jnp.stack([df_, db_], axis=1).reshape(1, -1), d)], axis=0), n_small)

    w_s = pack_small(norm_w, conv_norm_w, ret_norm_w, final_norm_w, c_ctx, ret_decay_f, ret_decay_b)
    m_s = pack_small(m_norm_w, m_conv_norm_w, m_ret_norm_w, m_final_norm_w, m_c_ctx, m_ret_decay_f, m_ret_decay_b)
    v_s = pack_small(v_norm_w, v_conv_norm_w, v_ret_norm_w, v_final_norm_w, v_c_ctx, v_ret_decay_f, v_ret_decay_b)
    small_out = _sum_adamw(small_g, w_s, m_s, v_s, "adamw_small")

    def unpack_small(a):
        nw_ = a[0:depth]
        cn_ = a[depth].reshape(depth, s)
        gn_ = a[depth + 1].reshape(depth, s)
        fn_ = a[depth + 2]
        cc_ = a[depth + 3]
        dd = a[n_rows - 1, :depth * 2 * n_heads].reshape(depth, 2, n_heads)
        return dict(c_ctx=cc_, norm_w=nw_, conv_norm_w=cn_, ret_norm_w=gn_, ret_decay_f=dd[:, 0], ret_decay_b=dd[:, 1],
                    final_norm_w=fn_)

    res = {}
    for kind, arr in zip(("grad", "delta", "m", "v"), small_out):
        for k_, val in unpack_small(arr).items():
            res[(kind, k_)] = val

    bm_parts = jnp.concatenate([dmod_g[:, 0:2 * depth:2].reshape(N_DEV, depth, 3 * d),
                                dmod_g[:, 1:2 * depth:2].reshape(N_DEV, depth, 3 * d)], axis=0)
    bm_parts = jnp.concatenate([bm_parts, jnp.zeros((2 * N_DEV, 8 - depth, 3 * d), F32)], axis=1)
    pad8 = lambda a: _pad_rows(a, 8)
    bm_out = _sum_adamw(bm_parts, pad8(b_mod), pad8(m_b_mod), pad8(v_b_mod), "adamw_b_mod")
    for kind, arr in zip(("grad", "delta", "m", "v"), bm_out):
        res[(kind, "b_mod")] = arr[:depth]

    conv_rows = small_g[:, depth + 4:depth + 4 + 3 * depth * s // d].reshape(N_DEV, depth * 3, s)
    conv_mine = lax.dynamic_slice(conv_rows, (0, 0, me * n_cw), (N_DEV, depth * 3, n_cw))
    conv_mine = jnp.concatenate([conv_mine, jnp.zeros((N_DEV, 8 - depth * 3, n_cw), F32)], axis=1)
    cw2 = lambda a: _pad_rows(a.reshape(depth * 3, n_cw), 8)
    cw_out = _sum_adamw(conv_mine, cw2(conv_w), cw2(m_conv_w), cw2(v_conv_w), "adamw_conv_w")
    for kind, arr in zip(("grad", "delta", "m", "v"), cw_out):
        res[(kind, "conv_w")] = arr[:depth * 3].reshape(depth, 3, n_cw)

    wm_out = _sum_adamw(jnp.stack(g_wmod, axis=0).reshape(1, depth * d, n_mod), w_mod.reshape(depth * d, n_mod),
                        m_w_mod.reshape(depth * d, n_mod), v_w_mod.reshape(depth * d, n_mod), "adamw_w_mod")
    for kind, arr in zip(("grad", "delta", "m", "v"), wm_out):
        res[(kind, "w_mod")] = arr.reshape(depth, d, n_mod)

    wi_out = wo_out = None
    after = wm_out[0]
    for l in reversed(range(depth)):
        win_parts, wout_parts = _push_wait(*dwin[l], "scatter", after, f"grads_wait_l{l}")
        wi_out = _sum_adamw(win_parts, w_in.reshape(depth * d, s), m_w_in.reshape(depth * d, s),
                            v_w_in.reshape(depth * d, s), f"adamw_w_in_l{l}", row0=l * d, into=wi_out)
        wo_out = _sum_adamw(wout_parts, w_out.reshape(depth * r_out, d), m_w_out.reshape(depth * r_out, d),
                            v_w_out.reshape(depth * r_out, d), f"adamw_w_out_l{l}", row0=l * r_out, into=wo_out)
        after = wo_out[0]
    for kind, arr in zip(("grad", "delta", "m", "v"), wi_out):
        res[(kind, "w_in")] = arr.reshape(depth, d, s)
    for kind, arr in zip(("grad", "delta", "m", "v"), wo_out):
        res[(kind, "w_out")] = arr.reshape(depth, r_out, d)

    order = ["c_ctx", "norm_w", "w_mod", "b_mod", "w_in", "conv_w", "conv_norm_w", "ret_norm_w", "ret_decay_f",
             "ret_decay_b", "w_out", "final_norm_w"]
    outs = [loss, grad_x]
    for kind in ("grad", "delta", "m", "v"):
        outs += [res[(kind, k_)] for k_ in order]
    return tuple(outs)
```

```python
import functools

import jax
import jax.numpy as jnp
from jax import lax
from jax.experimental import pallas as pl
from jax.experimental.pallas import tpu as pltpu

F32 = jnp.float32
BF16 = jnp.bfloat16

EPS = 1e-6
CHUNK = 128
HEAD_DIM = 128
GRID_W = 64
ROPE_BASE = 10000.0
N_DEV = 8
ADAM_LR, ADAM_B1, ADAM_B2, ADAM_EPS, ADAM_WD, ADAM_STEP = 0.001, 0.9, 0.999, 1e-08, 0.01, 10

ROW_TILE = 256
V7X_VMEM_LIMIT = 56 * 1024 * 1024
MESH_AXES = ("x", "y", "c")

NN = ((1,), (0,))
NT = ((1,), (1,))
TN = ((0,), (0,))


def _dot(a, b, dims):
    return lax.dot_general(a, b, (dims, ((), ())), preferred_element_type=F32)


def _params(sem=None):
    if sem is None:
        return pltpu.CompilerParams(vmem_limit_bytes=V7X_VMEM_LIMIT)
    return pltpu.CompilerParams(dimension_semantics=sem, vmem_limit_bytes=V7X_VMEM_LIMIT)


def _silu(z):
    return z * jax.nn.sigmoid(z)


def _dsilu(z):
    s = jax.nn.sigmoid(z)
    return s * (1.0 + z * (1.0 - s))


def _sum_all(a):
    return jnp.sum(jnp.sum(a, axis=1, keepdims=True), axis=0, keepdims=True)


def _mm_rows(t):
    return 768 if t % 768 == 0 else ROW_TILE


def _rows_or(t, rows):
    return rows if t % rows == 0 else _mm_rows(t)


def _tiles(layer, t, d):
    return dict(in_tm=_rows_or(t, 1408), bwd_tm=_mm_rows(t), bwd_gs=2, wg_bm=d, out_tm=_mm_rows(t),
                out_tn=min(d, 1024), wo_bm=d, ob_tn=d)


def _full(shape):
    n = len(shape)
    return pl.BlockSpec(shape, lambda *_: (0,) * n)


def _peers(x, y, c):
    return [(x, y, 1 - c), (1 - x, y, c), (x, 1 - y, c), (1 - x, 1 - y, c),
            (1 - x, y, 1 - c), (x, 1 - y, 1 - c), (1 - x, 1 - y, 1 - c)]


def _lin(p):
    return 4 * p[0] + 2 * p[1] + p[2]


def _all_gather(arrays, name, in_vmem):
    n_arr = len(arrays)
    space = pltpu.VMEM if in_vmem else pl.ANY

    def body(*refs):
        ins, outs = refs[:n_arr], refs[n_arr:2 * n_arr]
        send_sems, recv_sems, local_sems = refs[2 * n_arr:]
        x, y, c = lax.axis_index("x"), lax.axis_index("y"), lax.axis_index("c")
        me, sibling = (x, y, c), (x, y, 1 - c)
        chips = [(1 - x, y), (x, 1 - y), (1 - x, 1 - y)]
        every = []
        locals_ = []
        for a in range(n_arr):
            m_per = ins[a].shape[0]
            out_ref = outs[a]

            def rows(p, out_ref=out_ref, m_per=m_per):
                return out_ref.at[pl.ds(_lin(p) * m_per, m_per), :]

            def copy(k, block, to, src=None, a=a, rows=rows):
                return pltpu.make_async_remote_copy(
                    src_ref=rows(block) if src is None else src, dst_ref=rows(block),
                    send_sem=send_sems.at[a, k], recv_sem=recv_sems.at[a, k],
                    device_id=to, device_id_type=pl.DeviceIdType.MESH)

            mine = pltpu.make_async_copy(ins[a], rows(me), local_sems.at[a])
            mine.start()
            locals_.append(mine)
            first = [copy(0, me, sibling, src=ins[a])]
            first += [copy(1 + j, me, (*chip, c), src=ins[a]) for j, chip in enumerate(chips)]
            for cp in first:
                cp.start()
            every.append((copy, first))
        sends = []
        for a in range(n_arr):
            copy, first = every[a]
            passed = [copy(4 + j, (*chip, c), sibling) for j, chip in enumerate(chips)]
            for j, chip in enumerate(chips):
                copy(1 + j, (*chip, c), me).wait_recv()
                passed[j].start()
            sends += first + passed
        for a in range(n_arr):
            copy, _ = every[a]
            copy(0, sibling, me).wait_recv()
            for j, chip in enumerate(chips):
                copy(4 + j, (*chip, 1 - c), me).wait_recv()
        for cp in sends:
            cp.wait_send()
        for mine in locals_:
            mine.wait()

    outs = pl.pallas_call(
        body, name=name,
        out_shape=[jax.ShapeDtypeStruct((N_DEV * a.shape[0], a.shape[1]), a.dtype) for a in arrays],
        in_specs=[pl.BlockSpec(memory_space=space)] * n_arr,
        out_specs=[pl.BlockSpec(memory_space=space)] * n_arr,
        scratch_shapes=[pltpu.SemaphoreType.DMA((n_arr, 7)), pltpu.SemaphoreType.DMA((n_arr, 7)),
                        pltpu.SemaphoreType.DMA((n_arr,))],
        compiler_params=_params(),
    )(*arrays)
    return list(outs)


_HBM = pl.BlockSpec(memory_space=pltpu.HBM)
_SEM = pl.BlockSpec(memory_space=pltpu.SEMAPHORE)
_DATAFLOW = pltpu.SideEffectType.DATAFLOW_SIDE_EFFECTING


PUSH_COPIES = {"scatter": 7, "gather": 7, "near": 4, "relay": 3}


def _push_copies(src_refs, land_refs, send_sems, recv_sems, mode):
    x, y, c = lax.axis_index("x"), lax.axis_index("y"), lax.axis_index("c")
    me, sibling = (x, y, c), (x, y, 1 - c)
    n_k = PUSH_COPIES[mode]
    out, back = [], []
    if mode == "relay":
        for k, chip in enumerate([(1 - x, y), (x, 1 - y), (1 - x, 1 - y)]):
            for a, land in enumerate(land_refs):
                sems = dict(send_sem=send_sems.at[n_k * a + k], recv_sem=recv_sems.at[n_k * a + k],
                            device_id=sibling, device_id_type=pl.DeviceIdType.MESH)
                mine = land.at[_lin((*chip, c))]
                out.append(pltpu.make_async_remote_copy(src_ref=mine, dst_ref=mine, **sems))
                back.append(pltpu.make_async_remote_copy(src_ref=mine, dst_ref=land.at[_lin((*chip, 1 - c))], **sems))
        return out, back
    for k, peer in enumerate(_peers(x, y, c)[:n_k]):
        for a, (src, land) in enumerate(zip(src_refs, land_refs)):
            sems = dict(send_sem=send_sems.at[n_k * a + k], recv_sem=recv_sems.at[n_k * a + k],
                        device_id=peer, device_id_type=pl.DeviceIdType.MESH)
            mine = src.at[_lin(peer)] if mode == "scatter" else src
            out.append(pltpu.make_async_remote_copy(src_ref=mine, dst_ref=land.at[_lin(me)], **sems))
            back.append(pltpu.make_async_remote_copy(src_ref=mine, dst_ref=land.at[_lin(peer)], **sems))
    return out, back


def _push_start(srcs, lands, mode, name, after=()):
    n_src, n = len(srcs), len(lands)
    n_buf = n_src + n
    n_in = n_buf + len(after)
    n_sem = PUSH_COPIES[mode] * n

    def body(*refs):
        send_sems, recv_sems = refs[n_in], refs[n_in + 1]
        out, _ = _push_copies(refs[:n_src], refs[n_src:n_buf], send_sems, recv_sems, mode)
        for cp in out:
            cp.start()
        token = refs[-1]
        token[...] = jnp.zeros_like(token)

    both = list(srcs) + list(lands)
    res = pl.pallas_call(
        body, name=name,
        out_shape=[pltpu.SemaphoreType.DMA((n_sem,)), pltpu.SemaphoreType.DMA((n_sem,))]
        + [pltpu.HBM(a.shape, a.dtype) for a in both] + [jax.ShapeDtypeStruct((8, 128), F32)],
        in_specs=[_HBM] * n_buf + [pl.BlockSpec(memory_space=pl.ANY)] * len(after),
        out_specs=[_SEM, _SEM] + [_HBM] * n_buf + [pl.BlockSpec(memory_space=pltpu.VMEM)],
        input_output_aliases={i: 2 + i for i in range(n_buf)},
        compiler_params=pltpu.CompilerParams(has_side_effects=_DATAFLOW),
    )(*[pltpu.with_memory_space_constraint(a, pltpu.HBM) for a in both], *after)
    return res[0], res[1], list(res[2:2 + n_src]), list(res[2 + n_src:2 + n_buf]), res[-1]


def _push_wait(send_sems, recv_sems, srcs, lands, mode, after, name):
    n_src, n = len(srcs), len(lands)
    n_buf = n_src + n

    def body(*refs):
        out, back = _push_copies(refs[:n_src], refs[n_src:n_buf], refs[n_buf], refs[n_buf + 1], mode)
        for cp in out:
            cp.wait_send()
        for cp in back:
            cp.wait_recv()

    both = list(srcs) + list(lands)
    res = pl.pallas_call(
        body, name=name,
        out_shape=[pltpu.HBM(a.shape, a.dtype) for a in both],
        in_specs=[_HBM] * n_buf + [_SEM, _SEM, pl.BlockSpec(memory_space=pl.ANY)],
        out_specs=[_HBM] * n_buf,
        input_output_aliases={i: i for i in range(n_buf)},
        compiler_params=pltpu.CompilerParams(has_side_effects=_DATAFLOW),
    )(*both, send_sems, recv_sems, after)
    return list(res[n_src:])


def _landing(own, me):
    zone = lax.empty((N_DEV,) + own.shape, own.dtype)
    return lax.dynamic_update_slice(zone, own[None], (me,) + (0,) * own.ndim)


def _mod_rows(c9, w_mod, b_sh, name):
    n = w_mod.shape[1]

    def body(c_ref, w_ref, b_ref, o_ref):
        s9 = _silu(c_ref[...]).astype(BF16)
        o_ref[...] = _dot(s9, w_ref[...].astype(BF16), NN) + b_ref[...]

    return pl.pallas_call(body, name=name, out_shape=jax.ShapeDtypeStruct((16, n), F32),
                          compiler_params=_params())(c9, w_mod, b_sh)


def _mod_grads(dm_rows, dc_rows, c9, w_mod, name):
    d, n = w_mod.shape

    def body(dm_ref, dc_ref, c_ref, w_ref, gw_ref, dc_out):
        dc = dc_ref[...]
        tot = dc[0:1]
        for j in range(1, N_DEV):
            tot = tot + dc[j:j + 1]
        row = lax.broadcasted_iota(jnp.int32, (8, n), 0)
        lower = jnp.where(row == 0, tot, 0.0)
        dmod9 = jnp.concatenate([dm_ref[...], lower], axis=0).astype(BF16)
        c9v = c_ref[...]
        s9 = _silu(c9v).astype(BF16)
        gw_ref[...] = _dot(s9, dmod9, TN)
        ds = _dot(lower.astype(BF16), w_ref[...].astype(BF16), NT)
        dc_out[...] = ds * _dsilu(c9v[8:16])

    return pl.pallas_call(body, name=name,
                          out_shape=[jax.ShapeDtypeStruct((d, n), F32), jax.ShapeDtypeStruct((8, d), F32)],
                          compiler_params=_params())(dm_rows, dc_rows, c9, w_mod)


def _decay_tables(dec, n_heads, name):
    c = CHUNK

    def body(dec_ref, dc_ref, dlf_ref, dlb_ref, qf_ref, kf_ref, qb_ref, kb_ref, cdf_ref, cdb_ref, lg_ref):
        h = pl.program_id(0)
        d = dec_ref[...]
        lane = lax.broadcasted_iota(jnp.int32, d.shape, 1)
        lg = -jnp.exp(jnp.sum(jnp.where(lane == h, d, 0.0), axis=1, keepdims=True))
        lgf, lgb = lg[0:1], lg[1:2]
        i = lax.broadcasted_iota(jnp.int32, (c, c), 0).astype(F32)
        j = lax.broadcasted_iota(jnp.int32, (c, c), 1).astype(F32)
        diff = i - j
        d_f = jnp.where(diff >= 0, jnp.exp(lgf * jnp.maximum(diff, 0.0)), 0.0)
        d_b = jnp.where(diff <= 0, jnp.exp(lgb * jnp.maximum(-diff, 0.0)), 0.0)
        dc_ref[...] = d_f + d_b
        dlf_ref[...] = diff * d_f
        dlb_ref[...] = -diff * d_b
        pos = lax.broadcasted_iota(jnp.int32, (c, HEAD_DIM), 0).astype(F32)
        qf_ref[...] = jnp.exp(lgf * (pos + 1.0))
        kf_ref[...] = jnp.exp(lgf * (c - 1.0 - pos))
        qb_ref[...] = jnp.exp(lgb * (c - pos))
        kb_ref[...] = jnp.exp(lgb * pos)
        ones = jnp.ones((8, HEAD_DIM), F32)
        cdf_ref[...] = jnp.exp(lgf * float(c)) * ones
        cdb_ref[...] = jnp.exp(lgb * float(c)) * ones

        @pl.when(h == 0)
        def _():
            lg_ref[...] = jnp.zeros_like(lg_ref)

        row8 = lax.broadcasted_iota(jnp.int32, (8, HEAD_DIM), 0)
        lane8 = lax.broadcasted_iota(jnp.int32, (8, HEAD_DIM), 1)
        lg_ref[...] += (jnp.where((row8 == 0) & (lane8 == h), lgf, 0.0)
                        + jnp.where((row8 == 1) & (lane8 == h), lgb, 0.0))

    def per_head(*tail):
        return pl.BlockSpec((None,) + tail, lambda h: (h,) + (0,) * len(tail))

    shapes = [(c, c)] * 3 + [(c, HEAD_DIM)] * 4 + [(8, HEAD_DIM)] * 2
    return pl.pallas_call(
        body, name=name, grid=(n_heads,),
        in_specs=[_full(dec.shape)],
        out_specs=[per_head(*s) for s in shapes] + [_full((8, HEAD_DIM))],
        out_shape=[jax.ShapeDtypeStruct((n_heads,) + s, F32) for s in shapes]
        + [jax.ShapeDtypeStruct((8, HEAD_DIM), F32)],
        compiler_params=_params(("arbitrary",)),
    )(dec)


def _modulate(x, nw, shift, scale):
    r = lax.rsqrt(jnp.mean(x * x, axis=-1, keepdims=True) + EPS)
    return ((x * r) * nw * (1.0 + scale) + shift).astype(BF16)


def _prenorm(xt, nw, mod, n_lat, name):
    t, d = xt.shape
    nxb = n_lat // ROW_TILE

    def body(x_ref, nw_ref, mod_ref, o_ref):
        ctx = pl.program_id(0) >= nxb
        m = mod_ref[...]
        o_ref[...] = _modulate(x_ref[...], nw_ref[...], jnp.where(ctx, m[3:4], m[0:1]), jnp.where(ctx, m[4:5], m[1:2]))

    row = pl.BlockSpec((ROW_TILE, d), lambda i: (i, 0))
    return pl.pallas_call(body, name=name, grid=(t // ROW_TILE,),
                          in_specs=[row, _full((1, d)), _full((8, d))],
                          out_specs=row, out_shape=jax.ShapeDtypeStruct((t, d), BF16),
                          compiler_params=_params(("parallel",)))(xt, nw, mod)


def _prenorm_first(x, ctx, nw, mod, name, after=()):
    n_lat, d = x.shape
    t = n_lat + ctx.shape[0]
    nxb = n_lat // ROW_TILE

    def body(x_ref, c_ref, nw_ref, mod_ref, *rest):
        o_ref, xt_ref = rest[-2:]
        m = mod_ref[...]
        nw_v = nw_ref[...]

        @pl.when(pl.program_id(0) < nxb)
        def _():
            xv = x_ref[...]
            xt_ref[...] = xv
            o_ref[...] = _modulate(xv, nw_v, m[0:1], m[1:2])

        @pl.when(pl.program_id(0) >= nxb)
        def _():
            xv = c_ref[...]
            xt_ref[...] = xv
            o_ref[...] = _modulate(xv, nw_v, m[3:4], m[4:5])

    row = pl.BlockSpec((ROW_TILE, d), lambda i: (i, 0))
    return pl.pallas_call(
        body, name=name, grid=(t // ROW_TILE,),
        in_specs=[pl.BlockSpec((ROW_TILE, d), lambda i: (jnp.minimum(i, nxb - 1), 0)),
                  pl.BlockSpec((ROW_TILE, d), lambda i: (jnp.maximum(i - nxb, 0), 0)), _full((1, d)), _full((8, d))]
        + [pl.BlockSpec(memory_space=pl.ANY)] * len(after),
        out_specs=[row, row], out_shape=[jax.ShapeDtypeStruct((t, d), BF16), jax.ShapeDtypeStruct((t, d), F32)],
        compiler_params=_params(("parallel",)))(x, ctx, nw, mod, *after)


def _rope_fwd(v, cos, sa, sb):
    return v * cos + pltpu.roll(v, 96, 1) * sa + pltpu.roll(v, 32, 1) * sb


def _rope_bwd(g, cos, sa, sb):
    return g * cos + pltpu.roll(g * sa, 32, 1) + pltpu.roll(g * sb, 96, 1)


N_PLAIN = 5


def _in_proj(hx, wg, cos, sa, sb, n_heads, tm, name, after=()):
    t, d = hx.shape
    n_seg, _, s = wg.shape
    nb = t // tm
    k_scale = HEAD_DIM ** -0.5

    def body(a_ref, w_ref, cos_ref, sa_ref, sb_ref, *rest):
        u_ref, qkv_ref = rest[-2:]
        g = pl.program_id(0)
        acc = _dot(a_ref[...], w_ref[...], NN)

        @pl.when(g < N_PLAIN)
        def _():
            u_ref[...] = acc

        @pl.when(g == N_PLAIN + 2)
        def _():
            qkv_ref[...] = acc.astype(BF16)

        for which, scale in ((N_PLAIN, 1.0), (N_PLAIN + 1, k_scale)):
            @pl.when(g == which)
            def _(scale=scale):
                co, a, b = cos_ref[...], sa_ref[...], sb_ref[...]
                for h in range(n_heads):
                    sl = slice(h * HEAD_DIM, (h + 1) * HEAD_DIM)
                    qkv_ref[:, sl] = (_rope_fwd(acc[:, sl], co, a, b) * scale).astype(BF16)

    def w_seg(g):
        return jnp.where(g < N_PLAIN - 1, g, jnp.where(g == N_PLAIN - 1, n_seg - 1, g - 1))

    tab = pl.BlockSpec((tm, HEAD_DIM), lambda g, i: (i, 0))
    return pl.pallas_call(
        body, name=name, grid=(n_seg, nb),
        in_specs=[pl.BlockSpec((tm, d), lambda g, i: (i, 0)), pl.BlockSpec((None, d, s), lambda g, i: (w_seg(g), 0, 0)),
                  tab, tab, tab] + [pl.BlockSpec(memory_space=pl.ANY)] * len(after),
        out_specs=[pl.BlockSpec((None, tm, s), lambda g, i: (jnp.minimum(g, N_PLAIN - 1), jnp.where(g < N_PLAIN, i, nb - 1), 0)),
                   pl.BlockSpec((None, tm, s), lambda g, i: (jnp.maximum(g - N_PLAIN, 0), jnp.where(g < N_PLAIN, 0, i), 0))],
        out_shape=[jax.ShapeDtypeStruct((N_PLAIN, t, s), F32), jax.ShapeDtypeStruct((3, t, s), BF16)],
        compiler_params=_params(("arbitrary", "arbitrary")))(hx, wg, cos, sa, sb, *after)


def _pair_sweep(xs, ys, tab_f, tab_b, cdf, cdb, n_heads, nx, ncc, reverse, name):
    t, s = xs[0].shape[-2:]
    nc = nx + ncc
    c = CHUNK
    n_pair = nc // 2
    assert nx % 2 == 0 and ncc % 2 == 0

    def f_pair(i):
        step = n_pair - 1 - i if reverse else i
        return jnp.where(step < ncc // 2, nx // 2 + step, step - ncc // 2)

    def b_pair(i):
        return i if reverse else n_pair - 1 - i

    f_subs = (1, 0) if reverse else (0, 1)
    b_subs = (0, 1) if reverse else (1, 0)

    def body(xf_ref, yf_ref, xb_ref, yb_ref, tf, tb, cdf_ref, cdb_ref, sf_out, sb_out, sf, sb):
        @pl.when(pl.program_id(0) == 0)
        def _():
            sf[...] = jnp.zeros_like(sf)
            sb[...] = jnp.zeros_like(sb)

        for step in range(2):
            for x_ref, y_ref, tab, cd, out, st, sub in ((xf_ref, yf_ref, tf, cdf_ref, sf_out, sf, f_subs[step]),
                                                        (xb_ref, yb_ref, tb, cdb_ref, sb_out, sb, b_subs[step])):
                rows = pl.ds(sub * c, c)
                for h in range(n_heads):
                    sl = pl.ds(h * HEAD_DIM, HEAD_DIM)
                    out[sub, h] = st[h].astype(BF16)
                    xd = (x_ref[rows, sl].astype(F32) * tab[h]).astype(BF16)
                    st[h] = cd[h][0:1, :] * st[h] + _dot(xd, y_ref[rows, sl], TN)

    def spec(arr, pair):
        lead = arr[1]
        if lead is None:
            return pl.BlockSpec((2 * c, s), lambda i: (pair(i), 0))
        return pl.BlockSpec((None, 2 * c, s), lambda i: (lead, pair(i), 0))

    st_blk = (2, n_heads, HEAD_DIM, HEAD_DIM)
    return pl.pallas_call(
        body, name=name, grid=(n_pair,),
        in_specs=[spec(xs, f_pair), spec(ys, f_pair), spec(xs, b_pair), spec(ys, b_pair),
                  _full((n_heads, c, HEAD_DIM)), _full((n_heads, c, HEAD_DIM)),
                  _full((n_heads, 8, HEAD_DIM)), _full((n_heads, 8, HEAD_DIM))],
        out_specs=[pl.BlockSpec(st_blk, lambda i: (f_pair(i), 0, 0, 0)), pl.BlockSpec(st_blk, lambda i: (b_pair(i), 0, 0, 0))],
        out_shape=[jax.ShapeDtypeStruct((nc, n_heads, HEAD_DIM, HEAD_DIM), BF16)] * 2,
        scratch_shapes=[pltpu.VMEM((n_heads, HEAD_DIM, HEAD_DIM), F32)] * 2,
        compiler_params=_params(("arbitrary",)),
    )(xs[0], ys[0], xs[0], ys[0], tab_f, tab_b, cdf, cdb)


def _state_sweep(qkv, tabs, n_heads, nx, ncc, name):
    return _pair_sweep((qkv, 1), (qkv, 2), tabs["kf"], tabs["kb"], tabs["cdf"], tabs["cdb"], n_heads, nx, ncc, False, name)


def _halo_specs(s, n8):
    per = CHUNK // 8

    def prev(g):
        return pl.BlockSpec((None, 8, s), lambda i: (g, jnp.maximum(i * per - 1, 0), 0))

    def nxt(g):
        return pl.BlockSpec((None, 8, s), lambda i: (g, jnp.minimum((i + 1) * per, n8 - 1), 0))

    return prev, nxt


def _shifted(a, before, after, has_prev, has_next):
    rows = a.shape[0]
    rowi = lax.broadcasted_iota(jnp.int32, a.shape, 0)
    am = jnp.where(rowi == 0, jnp.where(has_prev, before, 0.0), pltpu.roll(a, 1, 0))
    ap = jnp.where(rowi == rows - 1, jnp.where(has_next, after, 0.0), pltpu.roll(a, rows - 1, 0))
    return am, ap


def _neighbours(i, nx, nc):
    return (i != 0) & (i != nx), (i != nx - 1) & (i != nc - 1)


def _mix_fwd(u, qkv, sf, sb, tabs, conv_w, cnw, gnw, n_heads, nx, ncc, name):
    _, t, s = u.shape
    nc = nx + ncc
    c = CHUNK

    def body(h_ref, b_ref, c_ref, z_ref, rz_ref, hp_ref, hn_ref, cp_ref, cn_ref, q_ref, k_ref, v_ref,
             sf_ref, sb_ref, dc_ref, qft, qbt, w_ref, cnw_ref, gnw_ref, y_ref, o_ref):
        i = pl.program_id(0)
        has_prev, has_next = _neighbours(i, nx, nc)
        a = c_ref[...] * h_ref[...]
        am, ap = _shifted(a, cp_ref[7:8] * hp_ref[7:8], cn_ref[0:1] * hn_ref[0:1], has_prev, has_next)
        w = w_ref[...]
        y0 = w[0:1] * am + w[1:2] * a + w[2:3] * ap
        yb = b_ref[...] * y0
        r = lax.rsqrt(jnp.mean(yb * yb, axis=-1, keepdims=True) + EPS)
        y_ref[:, pl.ds(0, s)] = (_silu(z_ref[...]) * ((yb * r) * cnw_ref[...])).astype(BF16)
        for h in range(n_heads):
            sl = pl.ds(h * HEAD_DIM, HEAD_DIM)
            q, k, v = q_ref[:, sl], k_ref[:, sl], v_ref[:, sl]
            p = (_dot(q, k, NT) * dc_ref[h]).astype(BF16)
            o = _dot(p, v, NN)
            qf = q.astype(F32)
            o += _dot((qf * qft[h]).astype(BF16), sf_ref[h], NN)
            o += _dot((qf * qbt[h]).astype(BF16), sb_ref[h], NN)
            o_ref[:, sl] = o
            mu = jnp.mean(o, axis=-1, keepdims=True)
            var = jnp.mean(jnp.square(o - mu), axis=-1, keepdims=True)
            on = (o - mu) * lax.rsqrt(var + EPS)
            y_ref[:, pl.ds(s + h * HEAD_DIM, HEAD_DIM)] = (
                _silu(rz_ref[:, sl]) * (on * gnw_ref[:, sl])).astype(BF16)

    def seg(g):
        return pl.BlockSpec((None, c, s), lambda i: (g, i, 0))

    prev, nxt = _halo_specs(s, t // 8)
    row = pl.BlockSpec((c, s), lambda i: (i, 0))
    st = pl.BlockSpec((None, n_heads, HEAD_DIM, HEAD_DIM), lambda i: (i, 0, 0, 0))
    return pl.pallas_call(
        body, name=name, grid=(nc,),
        in_specs=[seg(0), seg(1), seg(2), seg(3), seg(4), prev(0), nxt(0), prev(2), nxt(2), seg(0), seg(1), seg(2),
                  st, st, _full((n_heads, c, c)), _full((n_heads, c, HEAD_DIM)), _full((n_heads, c, HEAD_DIM)),
                  _full((3, s)), _full((1, s)), _full((1, s))],
        out_specs=[pl.BlockSpec((c, 2 * s), lambda i: (i, 0)), row],
        out_shape=[jax.ShapeDtypeStruct((t, 2 * s), BF16), jax.ShapeDtypeStruct((t, s), F32)],
        compiler_params=_params(("parallel",)),
    )(u, u, u, u, u, u, u, u, u, qkv, qkv, qkv, sf, sb, tabs["dc"], tabs["qf"], tabs["qb"], conv_w, cnw, gnw)


def _row_gate(mod_ref, row0, rows, n_lat, col):
    rowi = row0 + lax.broadcasted_iota(jnp.int32, (rows, 1), 0)
    return jnp.where(rowi >= n_lat, mod_ref[5:6, col], mod_ref[2:3, col])


def _out_proj(ycat, w_out, xt, mod, n_lat, tm, tn, name):
    t, d = xt.shape

    def body(a_ref, w_ref, x_ref, mod_ref, m_ref, xo_ref):
        m = _dot(a_ref[...], w_ref[...], NN)
        m_ref[...] = m
        gate = _row_gate(mod_ref, pl.program_id(1) * tm, tm, n_lat, slice(None))
        xo_ref[...] = x_ref[...] + gate * m

    blk = pl.BlockSpec((tm, tn), lambda j, i: (i, j))
    return pl.pallas_call(
        body, name=name, grid=(d // tn, t // tm),
        in_specs=[pl.BlockSpec((tm, d), lambda j, i: (i, 0)), pl.BlockSpec((d, tn), lambda j, i: (0, j)), blk,
                  pl.BlockSpec((8, tn), lambda j, i: (0, j))],
        out_specs=[blk, blk], out_shape=[jax.ShapeDtypeStruct((t, d), F32)] * 2,
        compiler_params=_params(("parallel", "parallel")))(ycat, w_out, xt, mod)


def _out_proj_loss(ycat, w_out, xt, mod, tgt, fnw, n_lat, name):
    t, d = xt.shape
    nb = t // ROW_TILE
    nxb = n_lat // ROW_TILE

    def body(a_ref, w_ref, x_ref, mod_ref, t_ref, fw_ref, dx_ref, dm_ref, loss_ref, dw_ref, gacc_ref, xs, ms):
        i = pl.program_id(0)

        @pl.when(i == 0)
        def _():
            xs[...] = jnp.zeros_like(xs)
            ms[...] = jnp.zeros_like(ms)
            loss_ref[...] = jnp.zeros_like(loss_ref)
            dw_ref[...] = jnp.zeros_like(dw_ref)
            gacc_ref[...] = jnp.zeros_like(gacc_ref)

        slot = i % 2
        mv = mod_ref[...]
        m = _dot(a_ref[...], w_ref[...], NN)
        gate = jnp.where(jnp.minimum(i, nb - 1) >= nxb, mv[5:6], mv[2:3])
        x_prev, m_prev = xs[1 - slot], ms[1 - slot]
        xs[slot] = x_ref[...] + gate * m
        ms[slot] = m

        valid = (i >= 1) & (i - 1 < nxb)
        w = fw_ref[...]
        r = lax.rsqrt(jnp.mean(x_prev * x_prev, axis=-1, keepdims=True) + EPS)
        xn = x_prev * r
        e = xn * w - t_ref[...]
        loss = 0.5 * jnp.sum(jnp.mean(e * e, axis=-1, keepdims=True), axis=0, keepdims=True)
        loss_ref[...] += jnp.where(valid, loss, 0.0)
        dy = e * (1.0 / d)
        dw_ref[0:1, :] += jnp.where(valid, jnp.sum(dy * xn, axis=0, keepdims=True), 0.0)
        dxn = dy * w
        dx = jnp.where(valid, r * (dxn - xn * jnp.mean(dxn * xn, axis=-1, keepdims=True)), 0.0)
        dx_ref[...] = dx
        dm_ref[...] = (dx * mv[2:3]).astype(BF16)
        gacc_ref[2:3, :] += jnp.sum(dx * m_prev, axis=0, keepdims=True)

    cur = pl.BlockSpec((ROW_TILE, d), lambda i: (jnp.minimum(i, nb - 1), 0))
    prev = pl.BlockSpec((ROW_TILE, d), lambda i: (jnp.maximum(i - 1, 0), 0))
    return pl.pallas_call(
        body, name=name, grid=(nb + 1,),
        in_specs=[cur, _full((d, d)), cur, _full((8, d)),
                  pl.BlockSpec((ROW_TILE, d), lambda i: (jnp.clip(i - 1, 0, nxb - 1), 0)), _full((1, d))],
        out_specs=[prev, prev, _full((8, HEAD_DIM)), _full((8, d)), _full((8, d))],
        out_shape=[jax.ShapeDtypeStruct((t, d), F32), jax.ShapeDtypeStruct((t, d), BF16),
                   jax.ShapeDtypeStruct((8, HEAD_DIM), F32), jax.ShapeDtypeStruct((8, d), F32),
                   jax.ShapeDtypeStruct((8, d), F32)],
        scratch_shapes=[pltpu.VMEM((2, ROW_TILE, d), F32), pltpu.VMEM((2, ROW_TILE, d), F32)],
        compiler_params=_params(("arbitrary",)))(ycat, w_out, xt, mod, tgt, fnw)


def _matmul_nt(a, w, tn, name, after=()):
    t, k = a.shape
    n = w.shape[0]
    tm = _mm_rows(t)

    def body(a_ref, w_ref, *rest):
        rest[-1][...] = _dot(a_ref[...], w_ref[...], NT)

    return pl.pallas_call(
        body, name=name, grid=(n // tn, t // tm),
        in_specs=[pl.BlockSpec((tm, k), lambda j, i: (i, 0)), pl.BlockSpec((tn, k), lambda j, i: (j, 0))]
        + [pl.BlockSpec(memory_space=pl.ANY)] * len(after),
        out_specs=pl.BlockSpec((tm, tn), lambda j, i: (i, j)),
        out_shape=jax.ShapeDtypeStruct((t, n), F32),
        compiler_params=_params(("parallel", "parallel")))(a, w, *after)


def _weight_grad(a, b, bm, name):
    t, m = a.shape
    n_g, _, n = b.shape
    bt = _mm_rows(t)
    nt = t // bt

    def body(a_ref, b_ref, o_ref, acc):
        k = pl.program_id(2)

        @pl.when(k == 0)
        def _():
            acc[...] = jnp.zeros_like(acc)

        acc[...] += _dot(a_ref[...], b_ref[...], TN)

        @pl.when(k == nt - 1)
        def _():
            o_ref[...] = acc[...].astype(o_ref.dtype)

    return pl.pallas_call(
        body, name=name, grid=(n_g, m // bm, nt),
        in_specs=[pl.BlockSpec((bt, bm), lambda g, i, k: (k, i)), pl.BlockSpec((None, bt, n), lambda g, i, k: (g, k, 0))],
        out_specs=pl.BlockSpec((None, bm, n), lambda g, i, k: (g, i, 0)),
        out_shape=jax.ShapeDtypeStruct((n_g, m, n), BF16),
        scratch_shapes=[pltpu.VMEM((bm, n), F32)],
        compiler_params=_params(("parallel", "parallel", "arbitrary")))(a, b)


def _weight_grad_beside_prenorm_bwd(a, b, dhx, xt, dxo, nw, mod, below, n_lat, name):
    t, m = a.shape
    n_g, _, n = b.shape
    d = xt.shape[1]
    bt = _mm_rows(t)
    nt = t // bt
    rows = t // (n_g * nt)
    assert rows * n_g * nt == t and rows % 8 == 0

    def body(a_ref, b_ref, dh_ref, x_ref, dxo_ref, nw_ref, mod_ref, m_ref, modb_ref,
             o_ref, dx_ref, acc_ref, dm_ref, gacc_ref, acc):
        g, k = pl.program_id(0), pl.program_id(1)
        step = g * nt + k

        @pl.when(step == 0)
        def _():
            acc_ref[...] = jnp.zeros_like(acc_ref)
            gacc_ref[...] = jnp.zeros_like(gacc_ref)

        @pl.when(k == 0)
        def _():
            acc[...] = jnp.zeros_like(acc)

        acc[...] += _dot(a_ref[...], b_ref[...], TN)

        rowi = step * rows + lax.broadcasted_iota(jnp.int32, (rows, 1), 0)
        ctx = rowi >= n_lat
        w_lat = jnp.where(ctx, 0.0, 1.0)
        w_ctx = 1.0 - w_lat
        mv = mod_ref[...]
        scale1 = 1.0 + jnp.where(ctx, mv[4:5], mv[1:2])
        x = x_ref[...]
        nw_v = nw_ref[...]
        r = lax.rsqrt(jnp.mean(x * x, axis=-1, keepdims=True) + EPS)
        xn = x * r
        dh = dh_ref[...]
        dsc = dh * (xn * nw_v)
        acc_ref[0:1, :] += jnp.sum(dh * w_lat, axis=0, keepdims=True)
        acc_ref[1:2, :] += jnp.sum(dsc * w_lat, axis=0, keepdims=True)
        acc_ref[3:4, :] += jnp.sum(dh * w_ctx, axis=0, keepdims=True)
        acc_ref[4:5, :] += jnp.sum(dsc * w_ctx, axis=0, keepdims=True)
        acc_ref[6:7, :] += jnp.sum(dh * scale1 * xn, axis=0, keepdims=True)
        dxn = dh * (nw_v * scale1)
        dx = dxo_ref[...] + r * (dxn - xn * jnp.mean(dxn * xn, axis=-1, keepdims=True))
        dx_ref[...] = dx
        mb = modb_ref[...]
        dm_ref[...] = (dx * jnp.where(ctx, mb[5:6], mb[2:3])).astype(BF16)
        dg = dx * m_ref[...]
        gacc_ref[2:3, :] += jnp.sum(dg * w_lat, axis=0, keepdims=True)
        gacc_ref[5:6, :] += jnp.sum(dg * w_ctx, axis=0, keepdims=True)

        @pl.when(k == nt - 1)
        def _():
            o_ref[...] = acc[...].astype(o_ref.dtype)

    side = pl.BlockSpec((rows, d), lambda g, k: (g * nt + k, 0))
    acc8 = _full((8, d))
    return pl.pallas_call(
        body, name=name, grid=(n_g, nt),
        in_specs=[pl.BlockSpec((bt, m), lambda g, k: (k, 0)), pl.BlockSpec((None, bt, n), lambda g, k: (g, k, 0)),
                  side, side, side, _full((1, d)), acc8, side, acc8],
        out_specs=[pl.BlockSpec((None, m, n), lambda g, k: (g, 0, 0)), side, acc8, side, acc8],
        out_shape=[jax.ShapeDtypeStruct((n_g, m, n), BF16), jax.ShapeDtypeStruct((t, d), F32),
                   jax.ShapeDtypeStruct((8, d), F32), jax.ShapeDtypeStruct((t, d), BF16),
                   jax.ShapeDtypeStruct((8, d), F32)],
        scratch_shapes=[pltpu.VMEM((m, n), F32)],
        compiler_params=_params(("arbitrary", "arbitrary")))(a, b, dhx, xt, dxo, nw, mod, *below)


def _mix_bwd_a(dycat, u, o, conv_w, cnw, gnw, n_heads, nx, ncc, name):
    _, t, s = u.shape
    nc = nx + ncc
    c = CHUNK

    def body(dy_ref, h_ref, b_ref, c_ref, z_ref, rz_ref, hp_ref, hn_ref, cp_ref, cn_ref, o_ref, w_ref,
             cnw_ref, gnw_ref, g_ref, dz_ref, db_ref, drz_ref, do_ref, acc_ref):
        i = pl.program_id(0)

        @pl.when(i == 0)
        def _():
            acc_ref[...] = jnp.zeros_like(acc_ref)

        has_prev, has_next = _neighbours(i, nx, nc)
        a = c_ref[...] * h_ref[...]
        am, ap = _shifted(a, cp_ref[7:8] * hp_ref[7:8], cn_ref[0:1] * hn_ref[0:1], has_prev, has_next)
        w = w_ref[...]
        y0 = w[0:1] * am + w[1:2] * a + w[2:3] * ap
        bb = b_ref[...]
        yb = bb * y0
        r = lax.rsqrt(jnp.mean(yb * yb, axis=-1, keepdims=True) + EPS)
        ynn = yb * r
        z = z_ref[...]
        dyc = dy_ref[:, pl.ds(0, s)]
        cw = cnw_ref[...]
        dz_ref[...] = (dyc * (ynn * cw) * _dsilu(z)).astype(BF16)
        dyn = dyc * _silu(z)
        acc_ref[0:1, :] += jnp.sum(dyn * ynn, axis=0, keepdims=True)
        dynn = dyn * cw
        dyb = r * (dynn - ynn * jnp.mean(dynn * ynn, axis=-1, keepdims=True))
        db_ref[...] = (dyb * y0).astype(BF16)
        g_ref[...] = dyb * bb
        for h in range(n_heads):
            sl = pl.ds(h * HEAD_DIM, HEAD_DIM)
            ov = o_ref[:, sl]
            mu = jnp.mean(ov, axis=-1, keepdims=True)
            var = jnp.mean(jnp.square(ov - mu), axis=-1, keepdims=True)
            rs = lax.rsqrt(var + EPS)
            on = (ov - mu) * rs
            dyr = dy_ref[:, pl.ds(s + h * HEAD_DIM, HEAD_DIM)]
            rz = rz_ref[:, sl]
            gw = gnw_ref[:, sl]
            drz_ref[:, sl] = (dyr * (on * gw) * _dsilu(rz)).astype(BF16)
            dyg = dyr * _silu(rz)
            acc_ref[1:2, sl] += jnp.sum(dyg * on, axis=0, keepdims=True)
            don = dyg * gw
            do = rs * (don - jnp.mean(don, axis=-1, keepdims=True)
                       - on * jnp.mean(don * on, axis=-1, keepdims=True))
            do_ref[:, sl] = do.astype(BF16)

    def seg(g):
        return pl.BlockSpec((None, c, s), lambda i: (g, i, 0))

    prev, nxt = _halo_specs(s, t // 8)
    row = pl.BlockSpec((c, s), lambda i: (i, 0))
    return pl.pallas_call(
        body, name=name, grid=(nc,),
        in_specs=[pl.BlockSpec((c, 2 * s), lambda i: (i, 0)), seg(0), seg(1), seg(2), seg(3), seg(4),
                  prev(0), nxt(0), prev(2), nxt(2), row, _full((3, s)), _full((1, s)), _full((1, s))],
        out_specs=[row, row, row, row, row, _full((8, s))],
        out_shape=[jax.ShapeDtypeStruct((t, s), F32)] + [jax.ShapeDtypeStruct((t, s), BF16)] * 4
        + [jax.ShapeDtypeStruct((8, s), F32)],
        compiler_params=_params(("arbitrary",)),
    )(dycat, u, u, u, u, u, u, u, u, u, o, conv_w, cnw, gnw)


def _grad_state_sweep(qkv, do, tabs, n_heads, nx, ncc, name):
    return _pair_sweep((qkv, 0), (do, None), tabs["qf"], tabs["qb"], tabs["cdf"], tabs["cdb"], n_heads, nx, ncc, True, name)


def _mix_bwd_b(u, g, dz, db, drz, qkv, do, sf, sb, gf, gb, tabs, cos, sa, sb_tab, conv_w,
               n_heads, nx, ncc, name):
    _, t, s = u.shape
    nc = nx + ncc
    c = CHUNK
    k_scale = HEAD_DIM ** -0.5

    def body(h_ref, c_ref, g_ref, gp_ref, gn_ref, dz_ref, db_ref, drz_ref, q_ref, k_ref, v_ref, do_ref,
             sf_ref, sb_ref, gf_ref, gb_ref, dc_t, dlf_t, dlb_t, qft, kft, qbt, kbt, cdf, cdb, lg_ref,
             cos_ref, sa_ref, sb_ref2, w_ref, du_ref, dw_ref, dlg_ref):
        i = pl.program_id(0)

        @pl.when(i == 0)
        def _():
            dw_ref[...] = jnp.zeros_like(dw_ref)
            dlg_ref[...] = jnp.zeros_like(dlg_ref)

        has_prev, has_next = _neighbours(i, nx, nc)
        gv = g_ref[...]
        gm, gp = _shifted(gv, gp_ref[7:8], gn_ref[0:1], has_prev, has_next)
        w = w_ref[...]
        da = w[0:1] * gp + w[1:2] * gv + w[2:3] * gm
        hh, cc = h_ref[...], c_ref[...]
        du_ref[0] = (da * cc).astype(BF16)
        du_ref[2] = (da * hh).astype(BF16)
        a = cc * hh
        dw_ref[0:1, :] += jnp.sum(a * gp, axis=0, keepdims=True)
        dw_ref[1:2, :] += jnp.sum(a * gv, axis=0, keepdims=True)
        dw_ref[2:3, :] += jnp.sum(a * gm, axis=0, keepdims=True)
        du_ref[1] = db_ref[...]
        du_ref[3] = dz_ref[...]
        du_ref[7] = drz_ref[...]

        co, ra, rb = cos_ref[...], sa_ref[...], sb_ref2[...]
        pos = lax.broadcasted_iota(jnp.int32, (c, HEAD_DIM), 0).astype(F32)
        row8 = lax.broadcasted_iota(jnp.int32, (8, HEAD_DIM), 0)
        lane8 = lax.broadcasted_iota(jnp.int32, (8, HEAD_DIM), 1)
        dlg = jnp.zeros((8, HEAD_DIM), F32)
        for h in range(n_heads):
            sl = pl.ds(h * HEAD_DIM, HEAD_DIM)
            q, k, v, do = q_ref[:, sl], k_ref[:, sl], v_ref[:, sl], do_ref[:, sl]
            qf, kf, dof = q.astype(F32), k.astype(F32), do.astype(F32)
            s_f, s_b, g_f, g_b = sf_ref[h], sb_ref[h], gf_ref[h], gb_ref[h]
            p = _dot(q, k, NT)
            pd = _dot(do, v, NT)
            pdd = (pd * dc_t[h]).astype(BF16)
            dq = _dot(pdd, k, NN)
            dk = _dot(pdd, q, TN)
            dv = _dot((p * dc_t[h]).astype(BF16), do, TN)
            dq_f = _dot((dof * qft[h]).astype(BF16), s_f, NT)
            dq_b = _dot((dof * qbt[h]).astype(BF16), s_b, NT)
            dk_f = _dot(v, g_f, NT) * kft[h]
            dk_b = _dot(v, g_b, NT) * kbt[h]
            dv += _dot((kf * kft[h]).astype(BF16), g_f, NN) + _dot((kf * kbt[h]).astype(BF16), g_b, NN)
            ppd = p * pd
            cd_f, cd_b = cdf[h][0:1, :], cdb[h][0:1, :]
            t_f = _sum_all(dlf_t[h] * ppd + (pos + 1.0) * qf * dq_f + (c - 1.0 - pos) * kf * dk_f
                           + float(c) * (cd_f * (g_f.astype(F32) * s_f.astype(F32))))
            t_b = _sum_all(dlb_t[h] * ppd + (c - pos) * qf * dq_b + pos * kf * dk_b
                           + float(c) * (cd_b * (g_b.astype(F32) * s_b.astype(F32))))
            dlg += jnp.where((row8 == 0) & (lane8 == h), t_f, 0.0) + jnp.where((row8 == 1) & (lane8 == h), t_b, 0.0)
            du_ref[4, :, sl] = _rope_bwd(dq + dq_f + dq_b, co, ra, rb).astype(BF16)
            du_ref[5, :, sl] = (_rope_bwd(dk + dk_f + dk_b, co, ra, rb) * k_scale).astype(BF16)
            du_ref[6, :, sl] = dv.astype(BF16)
        dlg_ref[...] += dlg

        @pl.when(i == nc - 1)
        def _():
            dlg_ref[...] = dlg_ref[...] * lg_ref[...]

    def seg(gi):
        return pl.BlockSpec((None, c, s), lambda i: (gi, i, 0))

    per = c // 8
    n8 = t // 8
    row = pl.BlockSpec((c, s), lambda i: (i, 0))
    st = pl.BlockSpec((None, n_heads, HEAD_DIM, HEAD_DIM), lambda i: (i, 0, 0, 0))
    tab = pl.BlockSpec((c, HEAD_DIM), lambda i: (i, 0))
    hc = _full((n_heads, c, HEAD_DIM))
    cc_ = _full((n_heads, c, c))
    h8 = _full((n_heads, 8, HEAD_DIM))
    return pl.pallas_call(
        body, name=name, grid=(nc,),
        in_specs=[seg(0), seg(2), row,
                  pl.BlockSpec((8, s), lambda i: (jnp.maximum(i * per - 1, 0), 0)),
                  pl.BlockSpec((8, s), lambda i: (jnp.minimum((i + 1) * per, n8 - 1), 0)),
                  row, row, row, seg(0), seg(1), seg(2), row, st, st, st, st, cc_, cc_, cc_, hc, hc, hc, hc, h8, h8,
                  _full((8, HEAD_DIM)), tab, tab, tab, _full((3, s))],
        out_specs=[pl.BlockSpec((8, c, s), lambda i: (0, i, 0)), _full((8, s)), _full((8, HEAD_DIM))],
        out_shape=[jax.ShapeDtypeStruct((8, t, s), BF16), jax.ShapeDtypeStruct((8, s), F32),
                   jax.ShapeDtypeStruct((8, HEAD_DIM), F32)],
        compiler_params=_params(("arbitrary",)),
    )(u, u, g, g, g, dz, db, drz, qkv, qkv, qkv, do, sf, sb, gf, gb, tabs["dc"], tabs["dlf"], tabs["dlb"],
      tabs["qf"], tabs["kf"], tabs["qb"], tabs["kb"], tabs["cdf"], tabs["cdb"], tabs["lg"], cos, sa, sb_tab, conv_w)


def _in_proj_bwd(du, wg, tm, gs, name, after=()):
    n_seg, t, s = du.shape
    d = wg.shape[1]

    def body(a_ref, w_ref, *rest):
        o_ref = rest[-1]
        g = pl.program_id(1)
        part = _dot(a_ref[0], w_ref[0], NT)
        for j in range(1, gs):
            part += _dot(a_ref[j], w_ref[j], NT)

        @pl.when(g == 0)
        def _():
            o_ref[...] = part

        @pl.when(g > 0)
        def _():
            o_ref[...] += part

    return pl.pallas_call(
        body, name=name, grid=(t // tm, n_seg // gs),
        in_specs=[pl.BlockSpec((gs, tm, s), lambda i, g: (g, i, 0)), pl.BlockSpec((gs, d, s), lambda i, g: (g, 0, 0))]
        + [pl.BlockSpec(memory_space=pl.ANY)] * len(after),
        out_specs=pl.BlockSpec((tm, d), lambda i, g: (i, 0)),
        out_shape=jax.ShapeDtypeStruct((t, d), F32),
        compiler_params=_params(("parallel", "arbitrary")))(du, wg, *after)


def _prenorm_bwd_first(dhx, xt, dxo, nw, mod, n_lat, name):
    t, d = xt.shape
    nxb = n_lat // ROW_TILE

    def body(dh_ref, x_ref, dxo_ref, nw_ref, mod_ref, dx_ref, acc_ref):
        i = pl.program_id(0)

        @pl.when(i == 0)
        def _():
            acc_ref[...] = jnp.zeros_like(acc_ref)

        ctx = i >= nxb
        m = mod_ref[...]
        scale1 = 1.0 + jnp.where(ctx, m[4:5], m[1:2])
        x = x_ref[...]
        nw_v = nw_ref[...]
        r = lax.rsqrt(jnp.mean(x * x, axis=-1, keepdims=True) + EPS)
        xn = x * r
        dh = dh_ref[...]
        dshift = jnp.sum(dh, axis=0, keepdims=True)
        dscale = jnp.sum(dh * (xn * nw_v), axis=0, keepdims=True)
        acc_ref[6:7, :] += jnp.sum(dh * scale1 * xn, axis=0, keepdims=True)
        dxn = dh * (nw_v * scale1)
        dx = dxo_ref[...] + r * (dxn - xn * jnp.mean(dxn * xn, axis=-1, keepdims=True))

        @pl.when(i < nxb)
        def _():
            acc_ref[0:1, :] += dshift
            acc_ref[1:2, :] += dscale
            dx_ref[...] = dx

        @pl.when(i >= nxb)
        def _():
            acc_ref[3:4, :] += dshift
            acc_ref[4:5, :] += dscale

    row = pl.BlockSpec((ROW_TILE, d), lambda i: (i, 0))
    acc = _full((8, d))
    return pl.pallas_call(body, name=name, grid=(t // ROW_TILE,),
                          in_specs=[row, row, row, _full((1, d)), acc],
                          out_specs=[pl.BlockSpec((ROW_TILE, d), lambda i: (jnp.minimum(i, nxb - 1), 0)), acc],
                          out_shape=[jax.ShapeDtypeStruct((n_lat, d), F32), jax.ShapeDtypeStruct((8, d), F32)],
                          compiler_params=_params(("arbitrary",)))(dhx, xt, dxo, nw, mod)


def _adamw(g, w, m, v):
    m = ADAM_B1 * m + (1.0 - ADAM_B1) * g
    v = ADAM_B2 * v + (1.0 - ADAM_B2) * jnp.square(g)
    m_hat = m / (1.0 - ADAM_B1 ** ADAM_STEP)
    v_hat = v / (1.0 - ADAM_B2 ** ADAM_STEP)
    delta = -ADAM_LR * (m_hat / (jnp.sqrt(v_hat) + ADAM_EPS) + ADAM_WD * w)
    return delta, m, v


def _sum_adamw(parts, w, m, v, name, row0=0, into=None):
    n_p, r, n = parts.shape
    r_all = w.shape[0]
    part_block_bytes = 4 * 1024 * 1024
    br = 8
    for cand in (512, 256, 128, 64, 32, 16):
        if r % cand == 0 and row0 % cand == 0 and n_p * cand * n * parts.dtype.itemsize <= part_block_bytes:
            br = cand
            break
    blk0 = row0 // br

    def body(p_ref, w_ref, m_ref, v_ref, *rest):
        g_out, d_out, m_out, v_out = rest[-4:]
        g = p_ref[0].astype(F32)
        for j in range(1, n_p):
            g = g + p_ref[j].astype(F32)
        g_out[...] = g
        d_out[...], m_out[...], v_out[...] = _adamw(g, w_ref[...], m_ref[...], v_ref[...])

    row = pl.BlockSpec((br, n), lambda i: (i + blk0, 0))
    kept = [] if into is None else list(into)
    return pl.pallas_call(body, name=name, grid=(r // br,),
                          in_specs=[pl.BlockSpec((n_p, br, n), lambda i: (0, i, 0)), row, row, row]
                          + [pl.BlockSpec(memory_space=pl.ANY)] * len(kept),
                          out_specs=[row] * 4, out_shape=[jax.ShapeDtypeStruct((r_all, n), F32)] * 4,
                          input_output_aliases={4 + j: j for j in range(len(kept))},
                          compiler_params=_params(("parallel",)))(parts, w, m, v, *kept)


def _rope_tables(n_lat, n_ctx):
    f = HEAD_DIM // 4
    rows = n_lat // GRID_W
    inv = ROPE_BASE ** (-jnp.arange(f, dtype=F32) / f)
    ang_r = jnp.arange(rows).astype(F32)[:, None] * inv[None, :]
    ang_c = jnp.arange(GRID_W).astype(F32)[:, None] * inv[None, :]

    def by_row(a):
        return jnp.broadcast_to(a[:, None, :], (rows, GRID_W, f)).reshape(n_lat, f)

    def by_col(a):
        return jnp.broadcast_to(a[None, :, :], (rows, GRID_W, f)).reshape(n_lat, f)

    cr, sr, cc, sc = by_row(jnp.cos(ang_r)), by_row(jnp.sin(ang_r)), by_col(jnp.cos(ang_c)), by_col(jnp.sin(ang_c))
    zero = jnp.zeros_like(cr)
    cos = jnp.concatenate([cr, cr, cc, cc], axis=-1)
    sa = jnp.concatenate([-sr, zero, -sc, zero], axis=-1)
    sb = jnp.concatenate([zero, sr, zero, sc], axis=-1)
    pad = jnp.zeros((n_ctx, HEAD_DIM), F32)
    return (jnp.concatenate([cos, pad + 1.0], axis=0), jnp.concatenate([sa, pad], axis=0),
            jnp.concatenate([sb, pad], axis=0))


def _pad_rows(a, rows):
    return jnp.pad(a, [(0, rows - a.shape[0])] + [(0, 0)] * (a.ndim - 1))


def _pad_cols(a, cols):
    return jnp.pad(a, [(0, 0), (0, cols - a.shape[1])])


def kernel(x, c, ctx, c_ctx, norm_w, w_mod, b_mod, w_in, conv_w, conv_norm_w, ret_norm_w, ret_decay_f, ret_decay_b, w_out, final_norm_w, loss_target, m_c_ctx, m_norm_w, m_w_mod, m_b_mod, m_w_in, m_conv_w, m_conv_norm_w, m_ret_norm_w, m_ret_decay_f, m_ret_decay_b, m_w_out, m_final_norm_w, v_c_ctx, v_norm_w, v_w_mod, v_b_mod, v_w_in, v_conv_w, v_conv_norm_w, v_ret_norm_w, v_ret_decay_f, v_ret_decay_b, v_w_out, v_final_norm_w):
    depth = norm_w.shape[0]
    n_lat, d = x.shape[1], x.shape[2]
    n_ctx = ctx.shape[1]
    s = d // 2
    n_heads = ret_decay_f.shape[1]
    nx, ncc = n_lat // CHUNK, n_ctx // CHUNK
    n_mod = w_mod.shape[2]
    n_cw = conv_w.shape[2]
    r_out = w_out.shape[1]
    assert s == n_heads * HEAD_DIM and w_in.shape[2] == s and N_DEV * r_out == d
    assert n_lat % ROW_TILE == 0 and n_ctx % ROW_TILE == 0 and 3 * depth * n_cw <= d and d >= 3 * n_mod // 3
    me = 4 * lax.axis_index("x") + 2 * lax.axis_index("y") + lax.axis_index("c")

    w_in_bf = [w_in[l].astype(BF16) for l in range(depth)]
    w_out_bf = [w_out[l].astype(BF16) for l in range(depth)]

    first = jnp.concatenate([c.reshape(1, d), _pad_cols(conv_w.reshape(1, -1), d), jnp.zeros((6, d), F32)], axis=0)
    (first_g,) = _all_gather([first], "gather_cond", True)
    first_g = first_g.reshape(N_DEV, 8, d)
    c_all = first_g[:, 0, :]
    conv_full = first_g[:, 1, :3 * depth * n_cw].reshape(N_DEV, depth, 3, n_cw)
    conv_full = conv_full.transpose(1, 2, 0, 3).reshape(depth, 3, N_DEV * n_cw)
    c9 = jnp.concatenate([c_all, c_ctx.reshape(1, d), jnp.zeros((7, d), F32)], axis=0)

    b_sh = lax.dynamic_slice(b_mod, (0, me * n_mod), (depth, n_mod))
    mod_sh = jnp.concatenate([_mod_rows(c9, w_mod[l], b_sh[l:l + 1], f"mod_rows_l{l}") for l in range(depth)], axis=0)
    (mod_g,) = _all_gather([mod_sh], "gather_mod", True)
    mod_g = mod_g.reshape(N_DEV, depth, 16, n_mod)
    mods = []
    for l in range(depth):
        mine = lax.dynamic_index_in_dim(mod_g[:, l], me, axis=1, keepdims=False).reshape(3, d)
        cx = mod_g[:, l, 8, :].reshape(3, d)
        mods.append(jnp.concatenate([mine, cx, jnp.zeros((2, d), F32)], axis=0))

    near = _push_start([w_in_bf[0]], [_landing(w_in_bf[0], me)], "near", "w_in0_start", after=[mod_g])
    w_in_g = [None] * depth
    w_out_g = [None] * depth
    pending, tokens = [], []

    cos, sa, sb_tab = _rope_tables(n_lat, n_ctx)
    t_all = n_lat + n_ctx

    saved = []
    xt = None
    for l in range(depth):
        tiles = _tiles(l, t_all, d)
        names = ["dc", "dlf", "dlb", "qf", "kf", "qb", "kb", "cdf", "cdb", "lg"]
        dec = jnp.stack([ret_decay_f[l], ret_decay_b[l]], axis=0)
        tabs = dict(zip(names, _decay_tables(dec, n_heads, f"decay_tables_l{l}")))
        if l == 0:
            hx, xt = _prenorm_first(x[0], ctx[0], norm_w[0:1], mods[0], "prenorm_l0", after=near[4:])
            (w_in0_near,) = _push_wait(*near[:4], "near", hx, "w_in0_wait")
            relay = _push_start([], [w_in0_near], "relay", "w_in0_relay_start")
            (w_in_g[0],) = _push_wait(*relay[:4], "relay", relay[4], "w_in0_relay_wait")
            tokens = [w_in_g[0]]
            for k in range(depth):
                srcs = [w_out_bf[k]] + ([w_in_bf[k]] if k > 0 else [])
                started = _push_start(srcs, [_landing(a, me) for a in srcs], "gather", f"weights_start_l{k}",
                                      after=tokens[-1:])
                pending.append(started[:4])
                tokens.append(started[4])
        else:
            landed = _push_wait(*pending[l], "gather", xt, f"weights_wait_l{l}")
            w_out_g[l], w_in_g[l] = landed[0].reshape(d, d), landed[1]
            hx = _prenorm(xt, norm_w[l:l + 1], mods[l], n_lat, f"prenorm_l{l}")
        u, qkv = _in_proj(hx, w_in_g[l], cos, sa, sb_tab, n_heads, tiles["in_tm"], f"in_proj_l{l}",
                          after=tokens[1:] if l == 0 else ())
        sf, sb = _state_sweep(qkv, tabs, n_heads, nx, ncc, f"state_sweep_l{l}")
        ycat, o = _mix_fwd(u, qkv, sf, sb, tabs, conv_full[l], conv_norm_w[l:l + 1], ret_norm_w[l:l + 1],
                           n_heads, nx, ncc, f"mix_fwd_l{l}")
        if l == 0:
            (landed,) = _push_wait(*pending[0], "gather", ycat, "weights_wait_l0")
            w_out_g[0] = landed.reshape(d, d)
        m_res = x_new = None
        if l < depth - 1:
            m_res, x_new = _out_proj(ycat, w_out_g[l], xt, mods[l], n_lat, tiles["out_tm"], tiles["out_tn"],
                                     f"out_proj_l{l}")
        else:
            dxt, dm, loss_blk, dfnw, gate_acc = _out_proj_loss(ycat, w_out_g[l], xt, mods[l], loss_target[0],
                                                               final_norm_w.reshape(1, d), n_lat, f"out_proj_loss_l{l}")
        saved.append(dict(tabs=tabs, xt=xt, hx=hx, u=u, qkv=qkv, sf=sf, sb=sb, ycat=ycat, o=o, m=m_res, tiles=tiles))
        xt = x_new

    loss = lax.psum(loss_blk[0, 0], MESH_AXES)

    dmod_x, dmod_c, dnw, dcnw, dgnw, dconv, ddec, dwin, dwout = [], [], [], [], [], [], [], [], []
    started_token = ()
    for l in reversed(range(depth)):
        sv = saved[l]
        tiles = sv["tiles"]
        dycat = _matmul_nt(dm, w_out_g[l], tiles["ob_tn"], f"out_proj_bwd_l{l}", after=started_token)
        dwout.append(_weight_grad(sv["ycat"], dm.reshape(1, *dm.shape), tiles["wo_bm"], f"w_out_grad_l{l}")[0])
        g, dz, db, drz, do, norm_acc = _mix_bwd_a(dycat, sv["u"], sv["o"], conv_full[l], conv_norm_w[l:l + 1],
                                                   ret_norm_w[l:l + 1], n_heads, nx, ncc, f"mix_bwd_a_l{l}")
        gf, gb = _grad_state_sweep(sv["qkv"], do, sv["tabs"], n_heads, nx, ncc, f"grad_state_sweep_l{l}")
        du, conv_acc, dlg = _mix_bwd_b(sv["u"], g, dz, db, drz, sv["qkv"], do, sv["sf"], sv["sb"],
                                       gf, gb, sv["tabs"], cos, sa, sb_tab, conv_full[l], n_heads, nx, ncc,
                                       f"mix_bwd_b_l{l}")
        gate_acc_l = gate_acc
        if l > 0:
            dhx = _in_proj_bwd(du, w_in_g[l], tiles["bwd_tm"], tiles["bwd_gs"], f"in_proj_bwd_l{l}")
            below = (saved[l - 1]["m"], mods[l - 1])
            dwin_l, dxt, pre_acc, dm, gate_acc = _weight_grad_beside_prenorm_bwd(
                sv["hx"], du, dhx, sv["xt"], dxt, norm_w[l:l + 1], mods[l], below, n_lat, f"w_in_grad_l{l}")
        else:
            dwin_l = _weight_grad(sv["hx"], du, tiles["wg_bm"], f"w_in_grad_l{l}")
        srcs = [dwin_l, dwout[-1].reshape(N_DEV, r_out, d)]
        lands = [_landing(lax.dynamic_index_in_dim(a, me, axis=0, keepdims=False), me) for a in srcs]
        started = _push_start(srcs, lands, "scatter", f"grads_start_l{l}")
        dwin.append(started[:4])
        started_token = started[4:]
        if l == 0:
            dhx = _in_proj_bwd(du, w_in_g[l], tiles["bwd_tm"], tiles["bwd_gs"], f"in_proj_bwd_l{l}", after=started[4:])
            dxt, pre_acc = _prenorm_bwd_first(dhx, sv["xt"], dxt, norm_w[l:l + 1], mods[l], n_lat, f"prenorm_bwd_l{l}")
        dmod_x.append(jnp.concatenate([pre_acc[0], pre_acc[1], gate_acc_l[2]]))
        dmod_c.append(jnp.concatenate([pre_acc[3], pre_acc[4], gate_acc_l[5]]))
        dnw.append(pre_acc[6])
        dcnw.append(norm_acc[0])
        dgnw.append(norm_acc[1])
        dconv.append(conv_acc[0:3])
        ddec.append(dlg[0:2, :n_heads])
    for lst in (dmod_x, dmod_c, dnw, dcnw, dgnw, dconv, ddec, dwin, dwout):
        lst.reverse()
    grad_x = dxt.reshape(1, n_lat, d)

    rows = []
    for l in range(depth):
        rows += [dmod_x[l], dmod_c[l]]
    (dmod_g,) = _all_gather([_pad_rows(jnp.stack(rows, axis=0), 8)], "gather_dmod", True)
    dmod_g = dmod_g.reshape(N_DEV, 8, 3 * d)
    mine_cols = lax.dynamic_slice(dmod_g, (0, 0, me * n_mod), (N_DEV, 8, n_mod))
    g_wmod, dcc = [], jnp.zeros((d,), F32)
    for l in range(depth):
        gw, dc_part = _mod_grads(mine_cols[:, 2 * l], mine_cols[:, 2 * l + 1], c9, w_mod[l], f"mod_grads_l{l}")
        g_wmod.append(gw)
        dcc = dcc + dc_part[0]

    n_small = 16
    small = jnp.concatenate([
        jnp.stack(dnw, axis=0),
        jnp.concatenate(dcnw).reshape(1, -1),
        jnp.concatenate(dgnw).reshape(1, -1),
        dfnw[0:1],
        dcc.reshape(1, d),
        jnp.stack(dconv, axis=0).reshape(-1, d),
        _pad_cols(jnp.stack(ddec, axis=0).reshape(1, -1), d),
    ], axis=0)
    assert depth * s == d and small.shape[0] <= n_small
    n_rows = small.shape[0]
    (small_g,) = _all_gather([_pad_rows(small, n_small)], "gather_small", True)
    small_g = small_g.reshape(N_DEV, n_small, d)

    def pack_small(nw_, cn_, gn_, fn_, cc_, df_, db_):
        return _pad_rows(jnp.concatenate([
            nw_, cn_.reshape(1, -1), gn_.reshape(1, -1), fn_.reshape(1, d), cc_.reshape(1, d),
            jnp.zeros((n_rows - depth - 5, d), F32),
            _pad_cols(jnp.stack([df_, db_], axis=1).reshape(1, -1), d)], axis=0), n_small)

    w_s = pack_small(norm_w, conv_norm_w, ret_norm_w, final_norm_w, c_ctx, ret_decay_f, ret_decay_b)
    m_s = pack_small(m_norm_w, m_conv_norm_w, m_ret_norm_w, m_final_norm_w, m_c_ctx, m_ret_decay_f, m_ret_decay_b)
    v_s = pack_small(v_norm_w, v_conv_norm_w, v_ret_norm_w, v_final_norm_w, v_c_ctx, v_ret_decay_f, v_ret_decay_b)
    small_out = _sum_adamw(small_g, w_s, m_s, v_s, "adamw_small")

    def unpack_small(a):
        nw_ = a[0:depth]
        cn_ = a[depth].reshape(depth, s)
        gn_ = a[depth + 1].reshape(depth, s)
        fn_ = a[depth + 2]
        cc_ = a[depth + 3]
        dd = a[n_rows - 1, :depth * 2 * n_heads].reshape(depth, 2, n_heads)
        return dict(c_ctx=cc_, norm_w=nw_, conv_norm_w=cn_, ret_norm_w=gn_, ret_decay_f=dd[:, 0], ret_decay_b=dd[:, 1],
                    final_norm_w=fn_)

    res = {}
    for kind, arr in zip(("grad", "delta", "m", "v"), small_out):
        for k_, val in unpack_small(arr).items():
            res[(kind, k_)] = val

    bm_parts = jnp.concatenate([dmod_g[:, 0:2 * depth:2].reshape(N_DEV, depth, 3 * d),
                                dmod_g[:, 1:2 * depth:2].reshape(N_DEV, depth, 3 * d)], axis=0)
    bm_parts = jnp.concatenate([bm_parts, jnp.zeros((2 * N_DEV, 8 - depth, 3 * d), F32)], axis=1)
    pad8 = lambda a: _pad_rows(a, 8)
    bm_out = _sum_adamw(bm_parts, pad8(b_mod), pad8(m_b_mod), pad8(v_b_mod), "adamw_b_mod")
    for kind, arr in zip(("grad", "delta", "m", "v"), bm_out):
        res[(kind, "b_mod")] = arr[:depth]

    conv_rows = small_g[:, depth + 4:depth + 4 + 3 * depth * s // d].reshape(N_DEV, depth * 3, s)
    conv_mine = lax.dynamic_slice(conv_rows, (0, 0, me * n_cw), (N_DEV, depth * 3, n_cw))
    conv_mine = jnp.concatenate([conv_mine, jnp.zeros((N_DEV, 8 - depth * 3, n_cw), F32)], axis=1)
    cw2 = lambda a: _pad_rows(a.reshape(depth * 3, n_cw), 8)
    cw_out = _sum_adamw(conv_mine, cw2(conv_w), cw2(m_conv_w), cw2(v_conv_w), "adamw_conv_w")
    for kind, arr in zip(("grad", "delta", "m", "v"), cw_out):
        res[(kind, "conv_w")] = arr[:depth * 3].reshape(depth, 3, n_cw)

    wm_out = _sum_adamw(jnp.stack(g_wmod, axis=0).reshape(1, depth * d, n_mod), w_mod.reshape(depth * d, n_mod),
                        m_w_mod.reshape(depth * d, n_mod), v_w_mod.reshape(depth * d, n_mod), "adamw_w_mod")
    for kind, arr in zip(("grad", "delta", "m", "v"), wm_out):
        res[(kind, "w_mod")] = arr.reshape(depth, d, n_mod)

    wi_out = wo_out = None
    after = wm_out[0]
    for l in reversed(range(depth)):
        win_parts, wout_parts = _push_wait(*dwin[l], "scatter", after, f"grads_wait_l{l}")
        wi_out = _sum_adamw(win_parts, w_in.reshape(depth * d, s), m_w_in.reshape(depth * d, s),
                            v_w_in.reshape(depth * d, s), f"adamw_w_in_l{l}", row0=l * d, into=wi_out)
        wo_out = _sum_adamw(wout_parts, w_out.reshape(depth * r_out, d), m_w_out.reshape(depth * r_out, d),
                            v_w_out.reshape(depth * r_out, d), f"adamw_w_out_l{l}", row0=l * r_out, into=wo_out)
        after = wo_out[0]
    for kind, arr in zip(("grad", "delta", "m", "v"), wi_out):
        res[(kind, "w_in")] = arr.reshape(depth, d, s)
    for kind, arr in zip(("grad", "delta", "m", "v"), wo_out):
        res[(kind, "w_out")] = arr.reshape(depth, r_out, d)

    order = ["c_ctx", "norm_w", "w_mod", "b_mod", "w_in", "conv_w", "conv_norm_w", "ret_norm_w", "ret_decay_f",
             "ret_decay_b", "w_out", "final_norm_w"]
    outs = [loss, grad_x]
    for kind in ("grad", "delta", "m", "v"):
        outs += [res[(kind, k_)] for k_ in order]
    return tuple(outs)
```

```python
import functools

import jax
import jax.numpy as jnp
from jax import lax
from jax.experimental import pallas as pl
from jax.experimental.pallas import tpu as pltpu

F32 = jnp.float32
BF16 = jnp.bfloat16

EPS = 1e-6
CHUNK = 128
HEAD_DIM = 128
GRID_W = 64
ROPE_BASE = 10000.0
N_DEV = 8
ADAM_LR, ADAM_B1, ADAM_B2, ADAM_EPS, ADAM_WD, ADAM_STEP = 0.001, 0.9, 0.999, 1e-08, 0.01, 10

ROW_TILE = 256
V7X_VMEM_LIMIT = 56 * 1024 * 1024
MESH_AXES = ("x", "y", "c")

NN = ((1,), (0,))
NT = ((1,), (1,))
TN = ((0,), (0,))


def _dot(a, b, dims):
    return lax.dot_general(a, b, (dims, ((), ())), preferred_element_type=F32)


def _params(sem=None):
    if sem is None:
        return pltpu.CompilerParams(vmem_limit_bytes=V7X_VMEM_LIMIT)
    return pltpu.CompilerParams(dimension_semantics=sem, vmem_limit_bytes=V7X_VMEM_LIMIT)


def _silu(z):
    return z * jax.nn.sigmoid(z)


def _dsilu(z):
    s = jax.nn.sigmoid(z)
    return s * (1.0 + z * (1.0 - s))


def _silu_and_slope(z):
    s = jax.nn.sigmoid(z)
    return z * s, s * (1.0 + z * (1.0 - s))


def _sum_all(a):
    return jnp.sum(jnp.sum(a, axis=1, keepdims=True), axis=0, keepdims=True)


def _mm_rows(t):
    return 768 if t % 768 == 0 else ROW_TILE


def _rows_or(t, rows):
    return rows if t % rows == 0 else _mm_rows(t)


def _tiles(layer, t, d):
    return dict(in_tm=_rows_or(t, 1408), bwd_tm=_mm_rows(t), bwd_gs=2, wg_bm=d, wg_bt=_rows_or(t, 1408),
                out_tm=_mm_rows(t), out_tn=min(d, 1024), wo_bm=d, ob_tn=d)


def _full(shape):
    n = len(shape)
    return pl.BlockSpec(shape, lambda *_: (0,) * n)


def _peers(x, y, c):
    return [(x, y, 1 - c), (1 - x, y, c), (x, 1 - y, c), (1 - x, 1 - y, c),
            (1 - x, y, 1 - c), (x, 1 - y, 1 - c), (1 - x, 1 - y, 1 - c)]


def _lin(p):
    return 4 * p[0] + 2 * p[1] + p[2]


def _all_gather(arrays, name, in_vmem):
    n_arr = len(arrays)
    space = pltpu.VMEM if in_vmem else pl.ANY

    def body(*refs):
        ins, outs = refs[:n_arr], refs[n_arr:2 * n_arr]
        send_sems, recv_sems, local_sems = refs[2 * n_arr:]
        x, y, c = lax.axis_index("x"), lax.axis_index("y"), lax.axis_index("c")
        me, sibling = (x, y, c), (x, y, 1 - c)
        chips = [(1 - x, y), (x, 1 - y), (1 - x, 1 - y)]
        every = []
        locals_ = []
        for a in range(n_arr):
            m_per = ins[a].shape[0]
            out_ref = outs[a]

            def rows(p, out_ref=out_ref, m_per=m_per):
                return out_ref.at[pl.ds(_lin(p) * m_per, m_per), :]

            def copy(k, block, to, src=None, a=a, rows=rows):
                return pltpu.make_async_remote_copy(
                    src_ref=rows(block) if src is None else src, dst_ref=rows(block),
                    send_sem=send_sems.at[a, k], recv_sem=recv_sems.at[a, k],
                    device_id=to, device_id_type=pl.DeviceIdType.MESH)

            mine = pltpu.make_async_copy(ins[a], rows(me), local_sems.at[a])
            mine.start()
            locals_.append(mine)
            first = [copy(0, me, sibling, src=ins[a])]
            first += [copy(1 + j, me, (*chip, c), src=ins[a]) for j, chip in enumerate(chips)]
            for cp in first:
                cp.start()
            every.append((copy, first))
        sends = []
        for a in range(n_arr):
            copy, first = every[a]
            passed = [copy(4 + j, (*chip, c), sibling) for j, chip in enumerate(chips)]
            for j, chip in enumerate(chips):
                copy(1 + j, (*chip, c), me).wait_recv()
                passed[j].start()
            sends += first + passed
        for a in range(n_arr):
            copy, _ = every[a]
            copy(0, sibling, me).wait_recv()
            for j, chip in enumerate(chips):
                copy(4 + j, (*chip, 1 - c), me).wait_recv()
        for cp in sends:
            cp.wait_send()
        for mine in locals_:
            mine.wait()

    outs = pl.pallas_call(
        body, name=name,
        out_shape=[jax.ShapeDtypeStruct((N_DEV * a.shape[0], a.shape[1]), a.dtype) for a in arrays],
        in_specs=[pl.BlockSpec(memory_space=space)] * n_arr,
        out_specs=[pl.BlockSpec(memory_space=space)] * n_arr,
        scratch_shapes=[pltpu.SemaphoreType.DMA((n_arr, 7)), pltpu.SemaphoreType.DMA((n_arr, 7)),
                        pltpu.SemaphoreType.DMA((n_arr,))],
        compiler_params=_params(),
    )(*arrays)
    return list(outs)


_HBM = pl.BlockSpec(memory_space=pltpu.HBM)
_SEM = pl.BlockSpec(memory_space=pltpu.SEMAPHORE)
_DATAFLOW = pltpu.SideEffectType.DATAFLOW_SIDE_EFFECTING


PUSH_COPIES = {"scatter": 7, "gather": 7, "near": 4, "relay": 3}


def _push_copies(src_refs, land_refs, send_sems, recv_sems, mode):
    x, y, c = lax.axis_index("x"), lax.axis_index("y"), lax.axis_index("c")
    me, sibling = (x, y, c), (x, y, 1 - c)
    n_k = PUSH_COPIES[mode]
    out, back = [], []
    if mode == "relay":
        for k, chip in enumerate([(1 - x, y), (x, 1 - y), (1 - x, 1 - y)]):
            for a, land in enumerate(land_refs):
                sems = dict(send_sem=send_sems.at[n_k * a + k], recv_sem=recv_sems.at[n_k * a + k],
                            device_id=sibling, device_id_type=pl.DeviceIdType.MESH)
                mine = land.at[_lin((*chip, c))]
                out.append(pltpu.make_async_remote_copy(src_ref=mine, dst_ref=mine, **sems))
                back.append(pltpu.make_async_remote_copy(src_ref=mine, dst_ref=land.at[_lin((*chip, 1 - c))], **sems))
        return out, back
    for k, peer in enumerate(_peers(x, y, c)[:n_k]):
        for a, (src, land) in enumerate(zip(src_refs, land_refs)):
            sems = dict(send_sem=send_sems.at[n_k * a + k], recv_sem=recv_sems.at[n_k * a + k],
                        device_id=peer, device_id_type=pl.DeviceIdType.MESH)
            mine = src.at[_lin(peer)] if mode == "scatter" else src
            out.append(pltpu.make_async_remote_copy(src_ref=mine, dst_ref=land.at[_lin(me)], **sems))
            back.append(pltpu.make_async_remote_copy(src_ref=mine, dst_ref=land.at[_lin(peer)], **sems))
    return out, back


def _push_start(srcs, lands, mode, name, after=()):
    n_src, n = len(srcs), len(lands)
    n_buf = n_src + n
    n_in = n_buf + len(after)
    n_sem = PUSH_COPIES[mode] * n

    def body(*refs):
        send_sems, recv_sems = refs[n_in], refs[n_in + 1]
        out, _ = _push_copies(refs[:n_src], refs[n_src:n_buf], send_sems, recv_sems, mode)
        for cp in out:
            cp.start()
        token = refs[-1]
        token[...] = jnp.zeros_like(token)

    both = list(srcs) + list(lands)
    res = pl.pallas_call(
        body, name=name,
        out_shape=[pltpu.SemaphoreType.DMA((n_sem,)), pltpu.SemaphoreType.DMA((n_sem,))]
        + [pltpu.HBM(a.shape, a.dtype) for a in both] + [jax.ShapeDtypeStruct((8, 128), F32)],
        in_specs=[_HBM] * n_buf + [pl.BlockSpec(memory_space=pl.ANY)] * len(after),
        out_specs=[_SEM, _SEM] + [_HBM] * n_buf + [pl.BlockSpec(memory_space=pltpu.VMEM)],
        input_output_aliases={i: 2 + i for i in range(n_buf)},
        compiler_params=pltpu.CompilerParams(has_side_effects=_DATAFLOW),
    )(*[pltpu.with_memory_space_constraint(a, pltpu.HBM) for a in both], *after)
    return res[0], res[1], list(res[2:2 + n_src]), list(res[2 + n_src:2 + n_buf]), res[-1]


def _push_wait(send_sems, recv_sems, srcs, lands, mode, after, name):
    n_src, n = len(srcs), len(lands)
    n_buf = n_src + n

    def body(*refs):
        out, back = _push_copies(refs[:n_src], refs[n_src:n_buf], refs[n_buf], refs[n_buf + 1], mode)
        for cp in out:
            cp.wait_send()
        for cp in back:
            cp.wait_recv()

    both = list(srcs) + list(lands)
    res = pl.pallas_call(
        body, name=name,
        out_shape=[pltpu.HBM(a.shape, a.dtype) for a in both],
        in_specs=[_HBM] * n_buf + [_SEM, _SEM, pl.BlockSpec(memory_space=pl.ANY)],
        out_specs=[_HBM] * n_buf,
        input_output_aliases={i: i for i in range(n_buf)},
        compiler_params=pltpu.CompilerParams(has_side_effects=_DATAFLOW),
    )(*both, send_sems, recv_sems, after)
    return list(res[n_src:])


def _landing(own, me):
    zone = lax.empty((N_DEV,) + own.shape, own.dtype)
    return lax.dynamic_update_slice(zone, own[None], (me,) + (0,) * own.ndim)


def _mod_rows(c9, w_mod, b_sh, name):
    n = w_mod.shape[1]

    def body(c_ref, w_ref, b_ref, o_ref):
        s9 = _silu(c_ref[...]).astype(BF16)
        o_ref[...] = _dot(s9, w_ref[...].astype(BF16), NN) + b_ref[...]

    return pl.pallas_call(body, name=name, out_shape=jax.ShapeDtypeStruct((16, n), F32),
                          compiler_params=_params())(c9, w_mod, b_sh)


def _mod_grads(dm_rows, dc_rows, c9, w_mod, name):
    d, n = w_mod.shape

    def body(dm_ref, dc_ref, c_ref, w_ref, gw_ref, dc_out):
        dc = dc_ref[...]
        tot = dc[0:1]
        for j in range(1, N_DEV):
            tot = tot + dc[j:j + 1]
        row = lax.broadcasted_iota(jnp.int32, (8, n), 0)
        lower = jnp.where(row == 0, tot, 0.0)
        dmod9 = jnp.concatenate([dm_ref[...], lower], axis=0).astype(BF16)
        c9v = c_ref[...]
        s9 = _silu(c9v).astype(BF16)
        gw_ref[...] = _dot(s9, dmod9, TN)
        ds = _dot(lower.astype(BF16), w_ref[...].astype(BF16), NT)
        dc_out[...] = ds * _dsilu(c9v[8:16])

    return pl.pallas_call(body, name=name,
                          out_shape=[jax.ShapeDtypeStruct((d, n), F32), jax.ShapeDtypeStruct((8, d), F32)],
                          compiler_params=_params())(dm_rows, dc_rows, c9, w_mod)


def _decay_tables(dec, n_heads, name):
    c = CHUNK

    def body(dec_ref, dc_ref, dlf_ref, dlb_ref, qf_ref, kf_ref, qb_ref, kb_ref, cdf_ref, cdb_ref, lg_ref):
        h = pl.program_id(0)
        d = dec_ref[...]
        lane = lax.broadcasted_iota(jnp.int32, d.shape, 1)
        lg = -jnp.exp(jnp.sum(jnp.where(lane == h, d, 0.0), axis=1, keepdims=True))
        lgf, lgb = lg[0:1], lg[1:2]
        i = lax.broadcasted_iota(jnp.int32, (c, c), 0).astype(F32)
        j = lax.broadcasted_iota(jnp.int32, (c, c), 1).astype(F32)
        diff = i - j
        d_f = jnp.where(diff >= 0, jnp.exp(lgf * jnp.maximum(diff, 0.0)), 0.0)
        d_b = jnp.where(diff <= 0, jnp.exp(lgb * jnp.maximum(-diff, 0.0)), 0.0)
        dc_ref[...] = d_f + d_b
        dlf_ref[...] = diff * d_f
        dlb_ref[...] = -diff * d_b
        pos = lax.broadcasted_iota(jnp.int32, (c, HEAD_DIM), 0).astype(F32)
        qf_ref[...] = jnp.exp(lgf * (pos + 1.0))
        kf_ref[...] = jnp.exp(lgf * (c - 1.0 - pos))
        qb_ref[...] = jnp.exp(lgb * (c - pos))
        kb_ref[...] = jnp.exp(lgb * pos)
        ones = jnp.ones((8, HEAD_DIM), F32)
        cdf_ref[...] = jnp.exp(lgf * float(c)) * ones
        cdb_ref[...] = jnp.exp(lgb * float(c)) * ones

        @pl.when(h == 0)
        def _():
            lg_ref[...] = jnp.zeros_like(lg_ref)

        row8 = lax.broadcasted_iota(jnp.int32, (8, HEAD_DIM), 0)
        lane8 = lax.broadcasted_iota(jnp.int32, (8, HEAD_DIM), 1)
        lg_ref[...] += (jnp.where((row8 == 0) & (lane8 == h), lgf, 0.0)
                        + jnp.where((row8 == 1) & (lane8 == h), lgb, 0.0))

    def per_head(*tail):
        return pl.BlockSpec((None,) + tail, lambda h: (h,) + (0,) * len(tail))

    shapes = [(c, c)] * 3 + [(c, HEAD_DIM)] * 4 + [(8, HEAD_DIM)] * 2
    return pl.pallas_call(
        body, name=name, grid=(n_heads,),
        in_specs=[_full(dec.shape)],
        out_specs=[per_head(*s) for s in shapes] + [_full((8, HEAD_DIM))],
        out_shape=[jax.ShapeDtypeStruct((n_heads,) + s, F32) for s in shapes]
        + [jax.ShapeDtypeStruct((8, HEAD_DIM), F32)],
        compiler_params=_params(("arbitrary",)),
    )(dec)


def _modulate(x, nw, shift, scale):
    r = lax.rsqrt(jnp.mean(x * x, axis=-1, keepdims=True) + EPS)
    return ((x * r) * nw * (1.0 + scale) + shift).astype(BF16)


def _prenorm(xt, nw, mod, n_lat, name):
    t, d = xt.shape
    nxb = n_lat // ROW_TILE

    def body(x_ref, nw_ref, mod_ref, o_ref):
        ctx = pl.program_id(0) >= nxb
        m = mod_ref[...]
        o_ref[...] = _modulate(x_ref[...], nw_ref[...], jnp.where(ctx, m[3:4], m[0:1]), jnp.where(ctx, m[4:5], m[1:2]))

    row = pl.BlockSpec((ROW_TILE, d), lambda i: (i, 0))
    return pl.pallas_call(body, name=name, grid=(t // ROW_TILE,),
                          in_specs=[row, _full((1, d)), _full((8, d))],
                          out_specs=row, out_shape=jax.ShapeDtypeStruct((t, d), BF16),
                          compiler_params=_params(("parallel",)))(xt, nw, mod)


def _prenorm_first(x, ctx, nw, mod, name, after=()):
    n_lat, d = x.shape
    t = n_lat + ctx.shape[0]
    nxb = n_lat // ROW_TILE

    def body(x_ref, c_ref, nw_ref, mod_ref, *rest):
        o_ref, xt_ref = rest[-2:]
        m = mod_ref[...]
        nw_v = nw_ref[...]

        @pl.when(pl.program_id(0) < nxb)
        def _():
            xv = x_ref[...]
            xt_ref[...] = xv
            o_ref[...] = _modulate(xv, nw_v, m[0:1], m[1:2])

        @pl.when(pl.program_id(0) >= nxb)
        def _():
            xv = c_ref[...]
            xt_ref[...] = xv
            o_ref[...] = _modulate(xv, nw_v, m[3:4], m[4:5])

    row = pl.BlockSpec((ROW_TILE, d), lambda i: (i, 0))
    return pl.pallas_call(
        body, name=name, grid=(t // ROW_TILE,),
        in_specs=[pl.BlockSpec((ROW_TILE, d), lambda i: (jnp.minimum(i, nxb - 1), 0)),
                  pl.BlockSpec((ROW_TILE, d), lambda i: (jnp.maximum(i - nxb, 0), 0)), _full((1, d)), _full((8, d))]
        + [pl.BlockSpec(memory_space=pl.ANY)] * len(after),
        out_specs=[row, row], out_shape=[jax.ShapeDtypeStruct((t, d), BF16), jax.ShapeDtypeStruct((t, d), F32)],
        compiler_params=_params(("parallel",)))(x, ctx, nw, mod, *after)


def _rope_fwd(v, cos, sa, sb):
    return v * cos + pltpu.roll(v, 96, 1) * sa + pltpu.roll(v, 32, 1) * sb


def _rope_bwd(g, cos, sa, sb):
    return g * cos + pltpu.roll(g * sa, 32, 1) + pltpu.roll(g * sb, 96, 1)


N_PLAIN = 5


def _in_proj(hx, wg, cos, sa, sb, s, part, tm, name, after=(), into=None):
    t, d = hx.shape
    n_seg, _, n = wg.shape
    nb = t // tm
    k_scale = HEAD_DIM ** -0.5
    kept = [] if into is None else list(into)

    def body(a_ref, w_ref, cos_ref, sa_ref, sb_ref, *rest):
        u_ref, qkv_ref = rest[-2:]
        g = pl.program_id(0)
        acc = _dot(a_ref[...], w_ref[...], NN)

        @pl.when(g < N_PLAIN)
        def _():
            u_ref[...] = acc

        @pl.when(g == N_PLAIN + 2)
        def _():
            qkv_ref[...] = acc.astype(BF16)

        for which, scale in ((N_PLAIN, 1.0), (N_PLAIN + 1, k_scale)):
            @pl.when(g == which)
            def _(scale=scale):
                co, a, b = cos_ref[...], sa_ref[...], sb_ref[...]
                for h in range(n // HEAD_DIM):
                    sl = slice(h * HEAD_DIM, (h + 1) * HEAD_DIM)
                    qkv_ref[:, sl] = (_rope_fwd(acc[:, sl], co, a, b) * scale).astype(BF16)

    def w_seg(g):
        return jnp.where(g < N_PLAIN - 1, g, jnp.where(g == N_PLAIN - 1, n_seg - 1, g - 1))

    tab = pl.BlockSpec((tm, HEAD_DIM), lambda g, i: (i, 0))
    hbm = pl.BlockSpec(memory_space=pl.ANY)
    return pl.pallas_call(
        body, name=name, grid=(n_seg, nb),
        in_specs=[pl.BlockSpec((tm, d), lambda g, i: (i, 0)), pl.BlockSpec((None, d, n), lambda g, i: (w_seg(g), 0, 0)),
                  tab, tab, tab] + [hbm] * (len(after) + len(kept)),
        out_specs=[pl.BlockSpec((None, tm, n), lambda g, i: (jnp.minimum(g, N_PLAIN - 1), jnp.where(g < N_PLAIN, i, nb - 1), part)),
                   pl.BlockSpec((None, tm, n), lambda g, i: (jnp.maximum(g - N_PLAIN, 0), jnp.where(g < N_PLAIN, 0, i), part))],
        out_shape=[jax.ShapeDtypeStruct((N_PLAIN, t, s), F32), jax.ShapeDtypeStruct((3, t, s), BF16)],
        input_output_aliases={5 + len(after) + j: j for j in range(len(kept))},
        compiler_params=_params(("arbitrary", "arbitrary")))(hx, wg, cos, sa, sb, *after, *kept)


def _pair_sweep(xs, ys, tab_f, tab_b, cdf, cdb, n_heads, nx, ncc, reverse, name):
    t, s = xs[0].shape[-2:]
    nc = nx + ncc
    c = CHUNK
    n_pair = nc // 2
    assert nx % 2 == 0 and ncc % 2 == 0

    def f_pair(i):
        step = n_pair - 1 - i if reverse else i
        return jnp.where(step < ncc // 2, nx // 2 + step, step - ncc // 2)

    def b_pair(i):
        return i if reverse else n_pair - 1 - i

    f_subs = (1, 0) if reverse else (0, 1)
    b_subs = (0, 1) if reverse else (1, 0)

    def body(xf_ref, yf_ref, xb_ref, yb_ref, tf, tb, cdf_ref, cdb_ref, sf_out, sb_out, sf, sb):
        @pl.when(pl.program_id(0) == 0)
        def _():
            sf[...] = jnp.zeros_like(sf)
            sb[...] = jnp.zeros_like(sb)

        for step in range(2):
            for x_ref, y_ref, tab, cd, out, st, sub in ((xf_ref, yf_ref, tf, cdf_ref, sf_out, sf, f_subs[step]),
                                                        (xb_ref, yb_ref, tb, cdb_ref, sb_out, sb, b_subs[step])):
                rows = pl.ds(sub * c, c)
                for h in range(n_heads):
                    sl = pl.ds(h * HEAD_DIM, HEAD_DIM)
                    out[sub, h] = st[h].astype(BF16)
                    xd = (x_ref[rows, sl].astype(F32) * tab[h]).astype(BF16)
                    st[h] = cd[h][0:1, :] * st[h] + _dot(xd, y_ref[rows, sl], TN)

    def spec(arr, pair):
        lead = arr[1]
        if lead is None:
            return pl.BlockSpec((2 * c, s), lambda i: (pair(i), 0))
        return pl.BlockSpec((None, 2 * c, s), lambda i: (lead, pair(i), 0))

    st_blk = (2, n_heads, HEAD_DIM, HEAD_DIM)
    return pl.pallas_call(
        body, name=name, grid=(n_pair,),
        in_specs=[spec(xs, f_pair), spec(ys, f_pair), spec(xs, b_pair), spec(ys, b_pair),
                  _full((n_heads, c, HEAD_DIM)), _full((n_heads, c, HEAD_DIM)),
                  _full((n_heads, 8, HEAD_DIM)), _full((n_heads, 8, HEAD_DIM))],
        out_specs=[pl.BlockSpec(st_blk, lambda i: (f_pair(i), 0, 0, 0)), pl.BlockSpec(st_blk, lambda i: (b_pair(i), 0, 0, 0))],
        out_shape=[jax.ShapeDtypeStruct((nc, n_heads, HEAD_DIM, HEAD_DIM), BF16)] * 2,
        scratch_shapes=[pltpu.VMEM((n_heads, HEAD_DIM, HEAD_DIM), F32)] * 2,
        compiler_params=_params(("arbitrary",)),
    )(xs[0], ys[0], xs[0], ys[0], tab_f, tab_b, cdf, cdb)


def _state_sweep(qkv, tabs, n_heads, nx, ncc, name):
    return _pair_sweep((qkv, 1), (qkv, 2), tabs["kf"], tabs["kb"], tabs["cdf"], tabs["cdb"], n_heads, nx, ncc, False, name)


def _halo_specs(s, n8):
    per = CHUNK // 8

    def prev(g):
        return pl.BlockSpec((None, 8, s), lambda i: (g, jnp.maximum(i * per - 1, 0), 0))

    def nxt(g):
        return pl.BlockSpec((None, 8, s), lambda i: (g, jnp.minimum((i + 1) * per, n8 - 1), 0))

    return prev, nxt


def _shifted(a, before, after, has_prev, has_next):
    rows = a.shape[0]
    rowi = lax.broadcasted_iota(jnp.int32, a.shape, 0)
    am = jnp.where(rowi == 0, jnp.where(has_prev, before, 0.0), pltpu.roll(a, 1, 0))
    ap = jnp.where(rowi == rows - 1, jnp.where(has_next, after, 0.0), pltpu.roll(a, rows - 1, 0))
    return am, ap


def _neighbours(i, nx, nc):
    return (i != 0) & (i != nx), (i != nx - 1) & (i != nc - 1)


def _mix_fwd(u, qkv, sf, sb, tabs, conv_w, cnw, gnw, n_heads, nx, ncc, name):
    _, t, s = u.shape
    nc = nx + ncc
    c = CHUNK

    def body(h_ref, b_ref, c_ref, z_ref, rz_ref, hp_ref, hn_ref, cp_ref, cn_ref, q_ref, k_ref, v_ref,
             sf_ref, sb_ref, dc_ref, qft, qbt, w_ref, cnw_ref, gnw_ref, y_ref, o_ref):
        i = pl.program_id(0)
        has_prev, has_next = _neighbours(i, nx, nc)
        a = c_ref[...] * h_ref[...]
        am, ap = _shifted(a, cp_ref[7:8] * hp_ref[7:8], cn_ref[0:1] * hn_ref[0:1], has_prev, has_next)
        w = w_ref[...]
        y0 = w[0:1] * am + w[1:2] * a + w[2:3] * ap
        yb = b_ref[...] * y0
        r = lax.rsqrt(jnp.mean(yb * yb, axis=-1, keepdims=True) + EPS)
        y_ref[:, pl.ds(0, s)] = (_silu(z_ref[...]) * ((yb * r) * cnw_ref[...])).astype(BF16)
        for h in range(n_heads):
            sl = pl.ds(h * HEAD_DIM, HEAD_DIM)
            q, k, v = q_ref[:, sl], k_ref[:, sl], v_ref[:, sl]
            p = (_dot(q, k, NT) * dc_ref[h]).astype(BF16)
            o = _dot(p, v, NN)
            qf = q.astype(F32)
            o += _dot((qf * qft[h]).astype(BF16), sf_ref[h], NN)
            o += _dot((qf * qbt[h]).astype(BF16), sb_ref[h], NN)
            o_ref[:, sl] = o
            mu = jnp.mean(o, axis=-1, keepdims=True)
            var = jnp.mean(jnp.square(o - mu), axis=-1, keepdims=True)
            on = (o - mu) * lax.rsqrt(var + EPS)
            y_ref[:, pl.ds(s + h * HEAD_DIM, HEAD_DIM)] = (
                _silu(rz_ref[:, sl]) * (on * gnw_ref[:, sl])).astype(BF16)

    def seg(g):
        return pl.BlockSpec((None, c, s), lambda i: (g, i, 0))

    prev, nxt = _halo_specs(s, t // 8)
    row = pl.BlockSpec((c, s), lambda i: (i, 0))
    st = pl.BlockSpec((None, n_heads, HEAD_DIM, HEAD_DIM), lambda i: (i, 0, 0, 0))
    return pl.pallas_call(
        body, name=name, grid=(nc,),
        in_specs=[seg(0), seg(1), seg(2), seg(3), seg(4), prev(0), nxt(0), prev(2), nxt(2), seg(0), seg(1), seg(2),
                  st, st, _full((n_heads, c, c)), _full((n_heads, c, HEAD_DIM)), _full((n_heads, c, HEAD_DIM)),
                  _full((3, s)), _full((1, s)), _full((1, s))],
        out_specs=[pl.BlockSpec((c, 2 * s), lambda i: (i, 0)), row],
        out_shape=[jax.ShapeDtypeStruct((t, 2 * s), BF16), jax.ShapeDtypeStruct((t, s), F32)],
        compiler_params=_params(("parallel",)),
    )(u, u, u, u, u, u, u, u, u, qkv, qkv, qkv, sf, sb, tabs["dc"], tabs["qf"], tabs["qb"], conv_w, cnw, gnw)


def _row_gate(mod_ref, row0, rows, n_lat, col):
    rowi = row0 + lax.broadcasted_iota(jnp.int32, (rows, 1), 0)
    return jnp.where(rowi >= n_lat, mod_ref[5:6, col], mod_ref[2:3, col])


def _out_proj(ycat, w_out, xt, mod, n_lat, tm, tn, name):
    t, d = xt.shape

    def body(a_ref, w_ref, x_ref, mod_ref, m_ref, xo_ref):
        m = _dot(a_ref[...], w_ref[...], NN)
        m_ref[...] = m
        gate = _row_gate(mod_ref, pl.program_id(1) * tm, tm, n_lat, slice(None))
        xo_ref[...] = x_ref[...] + gate * m

    blk = pl.BlockSpec((tm, tn), lambda j, i: (i, j))
    return pl.pallas_call(
        body, name=name, grid=(d // tn, t // tm),
        in_specs=[pl.BlockSpec((tm, d), lambda j, i: (i, 0)), pl.BlockSpec((d, tn), lambda j, i: (0, j)), blk,
                  pl.BlockSpec((8, tn), lambda j, i: (0, j))],
        out_specs=[blk, blk], out_shape=[jax.ShapeDtypeStruct((t, d), F32)] * 2,
        compiler_params=_params(("parallel", "parallel")))(ycat, w_out, xt, mod)


def _out_proj_loss(ycat, w_out, xt, mod, tgt, fnw, n_lat, name):
    t, d = xt.shape
    nb = t // ROW_TILE
    nxb = n_lat // ROW_TILE

    def body(a_ref, w_ref, x_ref, mod_ref, t_ref, fw_ref, dx_ref, dm_ref, loss_ref, dw_ref, gacc_ref, xs, ms):
        i = pl.program_id(0)

        @pl.when(i == 0)
        def _():
            xs[...] = jnp.zeros_like(xs)
            ms[...] = jnp.zeros_like(ms)
            loss_ref[...] = jnp.zeros_like(loss_ref)
            dw_ref[...] = jnp.zeros_like(dw_ref)
            gacc_ref[...] = jnp.zeros_like(gacc_ref)

        def step(cur, prev):
            mv = mod_ref[...]
            x_prev, m_prev = xs[prev], ms[prev]
            valid = (i >= 1) & (i - 1 < nxb)
            w = fw_ref[...]
            r = lax.rsqrt(jnp.mean(x_prev * x_prev, axis=-1, keepdims=True) + EPS)
            xn = x_prev * r
            e = xn * w - t_ref[...]
            loss = 0.5 * jnp.sum(jnp.mean(e * e, axis=-1, keepdims=True), axis=0, keepdims=True)
            loss_ref[...] += jnp.where(valid, loss, 0.0)
            dy = e * (1.0 / d)
            dw_ref[0:1, :] += jnp.where(valid, jnp.sum(dy * xn, axis=0, keepdims=True), 0.0)
            dxn = dy * w
            dx = jnp.where(valid, r * (dxn - xn * jnp.mean(dxn * xn, axis=-1, keepdims=True)), 0.0)
            dx_ref[...] = dx
            dm_ref[...] = (dx * mv[2:3]).astype(BF16)
            gacc_ref[2:3, :] += jnp.sum(dx * m_prev, axis=0, keepdims=True)

            m = _dot(a_ref[...], w_ref[...], NN)
            gate = jnp.where(jnp.minimum(i, nb - 1) >= nxb, mv[5:6], mv[2:3])
            xs[cur] = x_ref[...] + gate * m
            ms[cur] = m

        @pl.when(i % 2 == 0)
        def _():
            step(0, 1)

        @pl.when(i % 2 == 1)
        def _():
            step(1, 0)

    cur = pl.BlockSpec((ROW_TILE, d), lambda i: (jnp.minimum(i, nb - 1), 0))
    prev = pl.BlockSpec((ROW_TILE, d), lambda i: (jnp.maximum(i - 1, 0), 0))
    return pl.pallas_call(
        body, name=name, grid=(nb + 1,),
        in_specs=[cur, _full((d, d)), cur, _full((8, d)),
                  pl.BlockSpec((ROW_TILE, d), lambda i: (jnp.clip(i - 1, 0, nxb - 1), 0)), _full((1, d))],
        out_specs=[prev, prev, _full((8, HEAD_DIM)), _full((8, d)), _full((8, d))],
        out_shape=[jax.ShapeDtypeStruct((t, d), F32), jax.ShapeDtypeStruct((t, d), BF16),
                   jax.ShapeDtypeStruct((8, HEAD_DIM), F32), jax.ShapeDtypeStruct((8, d), F32),
                   jax.ShapeDtypeStruct((8, d), F32)],
        scratch_shapes=[pltpu.VMEM((2, ROW_TILE, d), F32), pltpu.VMEM((2, ROW_TILE, d), F32)],
        compiler_params=_params(("arbitrary",)))(ycat, w_out, xt, mod, tgt, fnw)


def _matmul_nt(a, w, tn, name, after=()):
    t, k = a.shape
    n = w.shape[0]
    tm = _mm_rows(t)

    def body(a_ref, w_ref, *rest):
        rest[-1][...] = _dot(a_ref[...], w_ref[...], NT)

    return pl.pallas_call(
        body, name=name, grid=(n // tn, t // tm),
        in_specs=[pl.BlockSpec((tm, k), lambda j, i: (i, 0)), pl.BlockSpec((tn, k), lambda j, i: (j, 0))]
        + [pl.BlockSpec(memory_space=pl.ANY)] * len(after),
        out_specs=pl.BlockSpec((tm, tn), lambda j, i: (i, j)),
        out_shape=jax.ShapeDtypeStruct((t, n), F32),
        compiler_params=_params(("parallel", "parallel")))(a, w, *after)


def _weight_grad(a, b, bm, bt, name):
    t, m = a.shape
    n_g, _, n = b.shape
    nt = t // bt

    def body(a_ref, b_ref, o_ref, acc):
        k = pl.program_id(2)

        @pl.when(k == 0)
        def _():
            acc[...] = jnp.zeros_like(acc)

        acc[...] += _dot(a_ref[...], b_ref[...], TN)

        @pl.when(k == nt - 1)
        def _():
            o_ref[...] = acc[...].astype(o_ref.dtype)

    return pl.pallas_call(
        body, name=name, grid=(n_g, m // bm, nt),
        in_specs=[pl.BlockSpec((bt, bm), lambda g, i, k: (k, i)), pl.BlockSpec((None, bt, n), lambda g, i, k: (g, k, 0))],
        out_specs=pl.BlockSpec((None, bm, n), lambda g, i, k: (g, i, 0)),
        out_shape=jax.ShapeDtypeStruct((n_g, m, n), BF16),
        scratch_shapes=[pltpu.VMEM((bm, n), F32)],
        compiler_params=_params(("parallel", "parallel", "arbitrary")))(a, b)


def _weight_grad_beside_prenorm_bwd(a, b, dhx, xt, dxo, nw, mod, below, n_lat, name):
    t, m = a.shape
    n_g, _, n = b.shape
    d = xt.shape[1]
    bt = _mm_rows(t)
    nt = t // bt
    rows = t // (n_g * nt)
    n_piece = 4 if rows % 32 == 0 and m % 4 == 0 else 1
    rows_p, m_p = rows // n_piece, m // n_piece
    assert rows * n_g * nt == t and rows_p % 8 == 0

    def body(a_ref, b_ref, dh_ref, x_ref, dxo_ref, nw_ref, mod_ref, m_ref, modb_ref,
             o_ref, dx_ref, acc_ref, dm_ref, gacc_ref, acc):
        g, k = pl.program_id(0), pl.program_id(1)
        step = g * nt + k

        @pl.when(step == 0)
        def _():
            acc_ref[...] = jnp.zeros_like(acc_ref)
            gacc_ref[...] = jnp.zeros_like(gacc_ref)

        @pl.when(k == 0)
        def _():
            acc[...] = jnp.zeros_like(acc)

        mv, mb, nw_v = mod_ref[...], modb_ref[...], nw_ref[...]
        for p in range(n_piece):
            rs = pl.ds(p * rows_p, rows_p)
            rowi = step * rows + p * rows_p + lax.broadcasted_iota(jnp.int32, (rows_p, 1), 0)
            ctx = rowi >= n_lat
            w_lat = jnp.where(ctx, 0.0, 1.0)
            w_ctx = 1.0 - w_lat
            scale1 = 1.0 + jnp.where(ctx, mv[4:5], mv[1:2])
            x = x_ref[rs, :]
            r = lax.rsqrt(jnp.mean(x * x, axis=-1, keepdims=True) + EPS)
            xn = x * r
            dh = dh_ref[rs, :]
            dsc = dh * (xn * nw_v)
            acc_ref[0:1, :] += jnp.sum(dh * w_lat, axis=0, keepdims=True)
            acc_ref[1:2, :] += jnp.sum(dsc * w_lat, axis=0, keepdims=True)
            acc_ref[3:4, :] += jnp.sum(dh * w_ctx, axis=0, keepdims=True)
            acc_ref[4:5, :] += jnp.sum(dsc * w_ctx, axis=0, keepdims=True)
            acc_ref[6:7, :] += jnp.sum(dh * scale1 * xn, axis=0, keepdims=True)
            dxn = dh * (nw_v * scale1)
            dx = dxo_ref[rs, :] + r * (dxn - xn * jnp.mean(dxn * xn, axis=-1, keepdims=True))
            dx_ref[rs, :] = dx
            dm_ref[rs, :] = (dx * jnp.where(ctx, mb[5:6], mb[2:3])).astype(BF16)
            dg = dx * m_ref[rs, :]
            gacc_ref[2:3, :] += jnp.sum(dg * w_lat, axis=0, keepdims=True)
            gacc_ref[5:6, :] += jnp.sum(dg * w_ctx, axis=0, keepdims=True)

            ms_ = pl.ds(p * m_p, m_p)
            acc[ms_, :] += _dot(a_ref[:, ms_], b_ref[...], TN)

        @pl.when(k == nt - 1)
        def _():
            o_ref[...] = acc[...].astype(o_ref.dtype)

    side = pl.BlockSpec((rows, d), lambda g, k: (g * nt + k, 0))
    acc8 = _full((8, d))
    return pl.pallas_call(
        body, name=name, grid=(n_g, nt),
        in_specs=[pl.BlockSpec((bt, m), lambda g, k: (k, 0)), pl.BlockSpec((None, bt, n), lambda g, k: (g, k, 0)),
                  side, side, side, _full((1, d)), acc8, side, acc8],
        out_specs=[pl.BlockSpec((None, m, n), lambda g, k: (g, 0, 0)), side, acc8, side, acc8],
        out_shape=[jax.ShapeDtypeStruct((n_g, m, n), BF16), jax.ShapeDtypeStruct((t, d), F32),
                   jax.ShapeDtypeStruct((8, d), F32), jax.ShapeDtypeStruct((t, d), BF16),
                   jax.ShapeDtypeStruct((8, d), F32)],
        scratch_shapes=[pltpu.VMEM((m, n), F32)],
        compiler_params=_params(("arbitrary", "arbitrary")))(a, b, dhx, xt, dxo, nw, mod, *below)


def _mix_bwd_a(dycat, u, o, conv_w, cnw, gnw, n_heads, nx, ncc, name):
    _, t, s = u.shape
    nc = nx + ncc
    c = CHUNK

    def body(dy_ref, h_ref, b_ref, c_ref, z_ref, rz_ref, hp_ref, hn_ref, cp_ref, cn_ref, o_ref, w_ref,
             cnw_ref, gnw_ref, g_ref, dz_ref, db_ref, drz_ref, do_ref, acc_ref):
        i = pl.program_id(0)

        @pl.when(i == 0)
        def _():
            acc_ref[...] = jnp.zeros_like(acc_ref)

        has_prev, has_next = _neighbours(i, nx, nc)
        a = c_ref[...] * h_ref[...]
        am, ap = _shifted(a, cp_ref[7:8] * hp_ref[7:8], cn_ref[0:1] * hn_ref[0:1], has_prev, has_next)
        w = w_ref[...]
        y0 = w[0:1] * am + w[1:2] * a + w[2:3] * ap
        bb = b_ref[...]
        yb = bb * y0
        r = lax.rsqrt(jnp.mean(yb * yb, axis=-1, keepdims=True) + EPS)
        ynn = yb * r
        z = z_ref[...]
        dyc = dy_ref[:, pl.ds(0, s)]
        cw = cnw_ref[...]
        sz, dsz = _silu_and_slope(z)
        dz_ref[...] = (dyc * (ynn * cw) * dsz).astype(BF16)
        dyn = dyc * sz
        acc_ref[0:1, :] += jnp.sum(dyn * ynn, axis=0, keepdims=True)
        dynn = dyn * cw
        dyb = r * (dynn - ynn * jnp.mean(dynn * ynn, axis=-1, keepdims=True))
        db_ref[...] = (dyb * y0).astype(BF16)
        g_ref[...] = dyb * bb
        for h in range(n_heads):
            sl = pl.ds(h * HEAD_DIM, HEAD_DIM)
            ov = o_ref[:, sl]
            mu = jnp.mean(ov, axis=-1, keepdims=True)
            var = jnp.mean(jnp.square(ov - mu), axis=-1, keepdims=True)
            rs = lax.rsqrt(var + EPS)
            on = (ov - mu) * rs
            dyr = dy_ref[:, pl.ds(s + h * HEAD_DIM, HEAD_DIM)]
            rz = rz_ref[:, sl]
            gw = gnw_ref[:, sl]
            srz, dsrz = _silu_and_slope(rz)
            drz_ref[:, sl] = (dyr * (on * gw) * dsrz).astype(BF16)
            dyg = dyr * srz
            acc_ref[1:2, sl] += jnp.sum(dyg * on, axis=0, keepdims=True)
            don = dyg * gw
            do = rs * (don - jnp.mean(don, axis=-1, keepdims=True)
                       - on * jnp.mean(don * on, axis=-1, keepdims=True))
            do_ref[:, sl] = do.astype(BF16)

    def seg(g):
        return pl.BlockSpec((None, c, s), lambda i: (g, i, 0))

    prev, nxt = _halo_specs(s, t // 8)
    row = pl.BlockSpec((c, s), lambda i: (i, 0))
    return pl.pallas_call(
        body, name=name, grid=(nc,),
        in_specs=[pl.BlockSpec((c, 2 * s), lambda i: (i, 0)), seg(0), seg(1), seg(2), seg(3), seg(4),
                  prev(0), nxt(0), prev(2), nxt(2), row, _full((3, s)), _full((1, s)), _full((1, s))],
        out_specs=[row, row, row, row, row, _full((8, s))],
        out_shape=[jax.ShapeDtypeStruct((t, s), F32)] + [jax.ShapeDtypeStruct((t, s), BF16)] * 4
        + [jax.ShapeDtypeStruct((8, s), F32)],
        compiler_params=_params(("arbitrary",)),
    )(dycat, u, u, u, u, u, u, u, u, u, o, conv_w, cnw, gnw)


def _grad_state_sweep(qkv, do, tabs, n_heads, nx, ncc, name):
    return _pair_sweep((qkv, 0), (do, None), tabs["qf"], tabs["qb"], tabs["cdf"], tabs["cdb"], n_heads, nx, ncc, True, name)


def _mix_bwd_b(u, g, dz, db, drz, qkv, do, sf, sb, gf, gb, tabs, cos, sa, sb_tab, conv_w,
               n_heads, nx, ncc, name):
    _, t, s = u.shape
    nc = nx + ncc
    c = CHUNK
    k_scale = HEAD_DIM ** -0.5

    def body(h_ref, c_ref, g_ref, gp_ref, gn_ref, dz_ref, db_ref, drz_ref, q_ref, k_ref, v_ref, do_ref,
             sf_ref, sb_ref, gf_ref, gb_ref, dc_t, dlf_t, dlb_t, qft, kft, qbt, kbt, cdf, cdb, lg_ref,
             cos_ref, sa_ref, sb_ref2, w_ref, du_ref, dw_ref, dlg_ref):
        i = pl.program_id(0)

        @pl.when(i == 0)
        def _():
            dw_ref[...] = jnp.zeros_like(dw_ref)
            dlg_ref[...] = jnp.zeros_like(dlg_ref)

        has_prev, has_next = _neighbours(i, nx, nc)
        gv = g_ref[...]
        gm, gp = _shifted(gv, gp_ref[7:8], gn_ref[0:1], has_prev, has_next)
        w = w_ref[...]
        da = w[0:1] * gp + w[1:2] * gv + w[2:3] * gm
        hh, cc = h_ref[...], c_ref[...]
        du_ref[0] = (da * cc).astype(BF16)
        du_ref[2] = (da * hh).astype(BF16)
        a = cc * hh
        dw_ref[0:1, :] += jnp.sum(a * gp, axis=0, keepdims=True)
        dw_ref[1:2, :] += jnp.sum(a * gv, axis=0, keepdims=True)
        dw_ref[2:3, :] += jnp.sum(a * gm, axis=0, keepdims=True)
        du_ref[1] = db_ref[...]
        du_ref[3] = dz_ref[...]
        du_ref[7] = drz_ref[...]

        co, ra, rb = cos_ref[...], sa_ref[...], sb_ref2[...]
        pos = lax.broadcasted_iota(jnp.int32, (c, HEAD_DIM), 0).astype(F32)
        row8 = lax.broadcasted_iota(jnp.int32, (8, HEAD_DIM), 0)
        lane8 = lax.broadcasted_iota(jnp.int32, (8, HEAD_DIM), 1)
        dlg = jnp.zeros((8, HEAD_DIM), F32)
        for h in range(n_heads):
            sl = pl.ds(h * HEAD_DIM, HEAD_DIM)
            q, k, v, do = q_ref[:, sl], k_ref[:, sl], v_ref[:, sl], do_ref[:, sl]
            qf, kf, dof = q.astype(F32), k.astype(F32), do.astype(F32)
            s_f, s_b, g_f, g_b = sf_ref[h], sb_ref[h], gf_ref[h], gb_ref[h]
            p = _dot(q, k, NT)
            pd = _dot(do, v, NT)
            pdd = (pd * dc_t[h]).astype(BF16)
            dq = _dot(pdd, k, NN)
            dk = _dot(pdd, q, TN)
            dv = _dot((p * dc_t[h]).astype(BF16), do, TN)
            dq_f = _dot((dof * qft[h]).astype(BF16), s_f, NT)
            dq_b = _dot((dof * qbt[h]).astype(BF16), s_b, NT)
            dk_f = _dot(v, g_f, NT) * kft[h]
            dk_b = _dot(v, g_b, NT) * kbt[h]
            dv += _dot((kf * kft[h]).astype(BF16), g_f, NN) + _dot((kf * kbt[h]).astype(BF16), g_b, NN)
            ppd = p * pd
            cd_f, cd_b = cdf[h][0:1, :], cdb[h][0:1, :]
            t_f = _sum_all(dlf_t[h] * ppd + (pos + 1.0) * qf * dq_f + (c - 1.0 - pos) * kf * dk_f
                           + float(c) * (cd_f * (g_f.astype(F32) * s_f.astype(F32))))
            t_b = _sum_all(dlb_t[h] * ppd + (c - pos) * qf * dq_b + pos * kf * dk_b
                           + float(c) * (cd_b * (g_b.astype(F32) * s_b.astype(F32))))
            dlg += jnp.where((row8 == 0) & (lane8 == h), t_f, 0.0) + jnp.where((row8 == 1) & (lane8 == h), t_b, 0.0)
            du_ref[4, :, sl] = _rope_bwd(dq + dq_f + dq_b, co, ra, rb).astype(BF16)
            du_ref[5, :, sl] = (_rope_bwd(dk + dk_f + dk_b, co, ra, rb) * k_scale).astype(BF16)
            du_ref[6, :, sl] = dv.astype(BF16)
        dlg_ref[...] += dlg

        @pl.when(i == nc - 1)
        def _():
            dlg_ref[...] = dlg_ref[...] * lg_ref[...]

    def seg(gi):
        return pl.BlockSpec((None, c, s), lambda i: (gi, i, 0))

    per = c // 8
    n8 = t // 8
    row = pl.BlockSpec((c, s), lambda i: (i, 0))
    st = pl.BlockSpec((None, n_heads, HEAD_DIM, HEAD_DIM), lambda i: (i, 0, 0, 0))
    tab = pl.BlockSpec((c, HEAD_DIM), lambda i: (i, 0))
    hc = _full((n_heads, c, HEAD_DIM))
    cc_ = _full((n_heads, c, c))
    h8 = _full((n_heads, 8, HEAD_DIM))
    return pl.pallas_call(
        body, name=name, grid=(nc,),
        in_specs=[seg(0), seg(2), row,
                  pl.BlockSpec((8, s), lambda i: (jnp.maximum(i * per - 1, 0), 0)),
                  pl.BlockSpec((8, s), lambda i: (jnp.minimum((i + 1) * per, n8 - 1), 0)),
                  row, row, row, seg(0), seg(1), seg(2), row, st, st, st, st, cc_, cc_, cc_, hc, hc, hc, hc, h8, h8,
                  _full((8, HEAD_DIM)), tab, tab, tab, _full((3, s))],
        out_specs=[pl.BlockSpec((8, c, s), lambda i: (0, i, 0)), _full((8, s)), _full((8, HEAD_DIM))],
        out_shape=[jax.ShapeDtypeStruct((8, t, s), BF16), jax.ShapeDtypeStruct((8, s), F32),
                   jax.ShapeDtypeStruct((8, HEAD_DIM), F32)],
        compiler_params=_params(("arbitrary",)),
    )(u, u, g, g, g, dz, db, drz, qkv, qkv, qkv, do, sf, sb, gf, gb, tabs["dc"], tabs["dlf"], tabs["dlb"],
      tabs["qf"], tabs["kf"], tabs["qb"], tabs["kb"], tabs["cdf"], tabs["cdb"], tabs["lg"], cos, sa, sb_tab, conv_w)


def _in_proj_bwd(du, wg, tm, gs, name, after=()):
    n_seg, t, s = du.shape
    d = wg.shape[1]

    def body(a_ref, w_ref, *rest):
        o_ref = rest[-1]
        g = pl.program_id(1)
        part = _dot(a_ref[0], w_ref[0], NT)
        for j in range(1, gs):
            part += _dot(a_ref[j], w_ref[j], NT)

        @pl.when(g == 0)
        def _():
            o_ref[...] = part

        @pl.when(g > 0)
        def _():
            o_ref[...] += part

    return pl.pallas_call(
        body, name=name, grid=(t // tm, n_seg // gs),
        in_specs=[pl.BlockSpec((gs, tm, s), lambda i, g: (g, i, 0)), pl.BlockSpec((gs, d, s), lambda i, g: (g, 0, 0))]
        + [pl.BlockSpec(memory_space=pl.ANY)] * len(after),
        out_specs=pl.BlockSpec((tm, d), lambda i, g: (i, 0)),
        out_shape=jax.ShapeDtypeStruct((t, d), F32),
        compiler_params=_params(("parallel", "arbitrary")))(du, wg, *after)


def _prenorm_bwd_first(dhx, xt, dxo, nw, mod, n_lat, name):
    t, d = xt.shape
    nxb = n_lat // ROW_TILE

    def body(dh_ref, x_ref, dxo_ref, nw_ref, mod_ref, dx_ref, acc_ref):
        i = pl.program_id(0)

        @pl.when(i == 0)
        def _():
            acc_ref[...] = jnp.zeros_like(acc_ref)

        ctx = i >= nxb
        m = mod_ref[...]
        scale1 = 1.0 + jnp.where(ctx, m[4:5], m[1:2])
        x = x_ref[...]
        nw_v = nw_ref[...]
        r = lax.rsqrt(jnp.mean(x * x, axis=-1, keepdims=True) + EPS)
        xn = x * r
        dh = dh_ref[...]
        dshift = jnp.sum(dh, axis=0, keepdims=True)
        dscale = jnp.sum(dh * (xn * nw_v), axis=0, keepdims=True)
        acc_ref[6:7, :] += jnp.sum(dh * scale1 * xn, axis=0, keepdims=True)
        dxn = dh * (nw_v * scale1)
        dx = dxo_ref[...] + r * (dxn - xn * jnp.mean(dxn * xn, axis=-1, keepdims=True))

        @pl.when(i < nxb)
        def _():
            acc_ref[0:1, :] += dshift
            acc_ref[1:2, :] += dscale
            dx_ref[...] = dx

        @pl.when(i >= nxb)
        def _():
            acc_ref[3:4, :] += dshift
            acc_ref[4:5, :] += dscale

    row = pl.BlockSpec((ROW_TILE, d), lambda i: (i, 0))
    acc = _full((8, d))
    return pl.pallas_call(body, name=name, grid=(t // ROW_TILE,),
                          in_specs=[row, row, row, _full((1, d)), acc],
                          out_specs=[pl.BlockSpec((ROW_TILE, d), lambda i: (jnp.minimum(i, nxb - 1), 0)), acc],
                          out_shape=[jax.ShapeDtypeStruct((n_lat, d), F32), jax.ShapeDtypeStruct((8, d), F32)],
                          compiler_params=_params(("arbitrary",)))(dhx, xt, dxo, nw, mod)


def _adamw(g, w, m, v):
    m = ADAM_B1 * m + (1.0 - ADAM_B1) * g
    v = ADAM_B2 * v + (1.0 - ADAM_B2) * jnp.square(g)
    m_hat = m / (1.0 - ADAM_B1 ** ADAM_STEP)
    v_hat = v / (1.0 - ADAM_B2 ** ADAM_STEP)
    delta = -ADAM_LR * (m_hat / (jnp.sqrt(v_hat) + ADAM_EPS) + ADAM_WD * w)
    return delta, m, v


def _sum_adamw(parts, w, m, v, name, row0=0, into=None):
    n_p, r, n = parts.shape
    r_all = w.shape[0]
    part_block_bytes = 4 * 1024 * 1024
    br = 8
    for cand in (512, 256, 128, 64, 32, 16):
        if r % cand == 0 and row0 % cand == 0 and n_p * cand * n * parts.dtype.itemsize <= part_block_bytes:
            br = cand
            break
    blk0 = row0 // br

    def body(p_ref, w_ref, m_ref, v_ref, *rest):
        g_out, d_out, m_out, v_out = rest[-4:]
        g = p_ref[0].astype(F32)
        for j in range(1, n_p):
            g = g + p_ref[j].astype(F32)
        g_out[...] = g
        d_out[...], m_out[...], v_out[...] = _adamw(g, w_ref[...], m_ref[...], v_ref[...])

    row = pl.BlockSpec((br, n), lambda i: (i + blk0, 0))
    kept = [] if into is None else list(into)
    return pl.pallas_call(body, name=name, grid=(r // br,),
                          in_specs=[pl.BlockSpec((n_p, br, n), lambda i: (0, i, 0)), row, row, row]
                          + [pl.BlockSpec(memory_space=pl.ANY)] * len(kept),
                          out_specs=[row] * 4, out_shape=[jax.ShapeDtypeStruct((r_all, n), F32)] * 4,
                          input_output_aliases={4 + j: j for j in range(len(kept))},
                          compiler_params=_params(("parallel",)))(parts, w, m, v, *kept)


def _rope_tables(n_lat, n_ctx):
    f = HEAD_DIM // 4
    rows = n_lat // GRID_W
    inv = ROPE_BASE ** (-jnp.arange(f, dtype=F32) / f)
    ang_r = jnp.arange(rows).astype(F32)[:, None] * inv[None, :]
    ang_c = jnp.arange(GRID_W).astype(F32)[:, None] * inv[None, :]

    def by_row(a):
        return jnp.broadcast_to(a[:, None, :], (rows, GRID_W, f)).reshape(n_lat, f)

    def by_col(a):
        return jnp.broadcast_to(a[None, :, :], (rows, GRID_W, f)).reshape(n_lat, f)

    cr, sr, cc, sc = by_row(jnp.cos(ang_r)), by_row(jnp.sin(ang_r)), by_col(jnp.cos(ang_c)), by_col(jnp.sin(ang_c))
    zero = jnp.zeros_like(cr)
    cos = jnp.concatenate([cr, cr, cc, cc], axis=-1)
    sa = jnp.concatenate([-sr, zero, -sc, zero], axis=-1)
    sb = jnp.concatenate([zero, sr, zero, sc], axis=-1)
    pad = jnp.zeros((n_ctx, HEAD_DIM), F32)
    return (jnp.concatenate([cos, pad + 1.0], axis=0), jnp.concatenate([sa, pad], axis=0),
            jnp.concatenate([sb, pad], axis=0))


def _pad_rows(a, rows):
    return jnp.pad(a, [(0, rows - a.shape[0])] + [(0, 0)] * (a.ndim - 1))


def _pad_cols(a, cols):
    return jnp.pad(a, [(0, 0), (0, cols - a.shape[1])])


def kernel(x, c, ctx, c_ctx, norm_w, w_mod, b_mod, w_in, conv_w, conv_norm_w, ret_norm_w, ret_decay_f, ret_decay_b, w_out, final_norm_w, loss_target, m_c_ctx, m_norm_w, m_w_mod, m_b_mod, m_w_in, m_conv_w, m_conv_norm_w, m_ret_norm_w, m_ret_decay_f, m_ret_decay_b, m_w_out, m_final_norm_w, v_c_ctx, v_norm_w, v_w_mod, v_b_mod, v_w_in, v_conv_w, v_conv_norm_w, v_ret_norm_w, v_ret_decay_f, v_ret_decay_b, v_w_out, v_final_norm_w):
    depth = norm_w.shape[0]
    n_lat, d = x.shape[1], x.shape[2]
    n_ctx = ctx.shape[1]
    s = d // 2
    n_heads = ret_decay_f.shape[1]
    nx, ncc = n_lat // CHUNK, n_ctx // CHUNK
    n_mod = w_mod.shape[2]
    n_cw = conv_w.shape[2]
    r_out = w_out.shape[1]
    assert s == n_heads * HEAD_DIM and w_in.shape[2] == s and N_DEV * r_out == d
    assert n_lat % ROW_TILE == 0 and n_ctx % ROW_TILE == 0 and 3 * depth * n_cw <= d and d >= 3 * n_mod // 3
    me = 4 * lax.axis_index("x") + 2 * lax.axis_index("y") + lax.axis_index("c")

    w_in_bf = [w_in[l].astype(BF16) for l in range(depth)]
    w_out_bf = [w_out[l].astype(BF16) for l in range(depth)]

    first = jnp.concatenate([c.reshape(1, d), _pad_cols(conv_w.reshape(1, -1), d), jnp.zeros((6, d), F32)], axis=0)
    (first_g,) = _all_gather([first], "gather_cond", True)
    first_g = first_g.reshape(N_DEV, 8, d)
    c_all = first_g[:, 0, :]
    conv_full = first_g[:, 1, :3 * depth * n_cw].reshape(N_DEV, depth, 3, n_cw)
    conv_full = conv_full.transpose(1, 2, 0, 3).reshape(depth, 3, N_DEV * n_cw)
    c9 = jnp.concatenate([c_all, c_ctx.reshape(1, d), jnp.zeros((7, d), F32)], axis=0)

    b_sh = lax.dynamic_slice(b_mod, (0, me * n_mod), (depth, n_mod))
    mod_sh = jnp.concatenate([_mod_rows(c9, w_mod[l], b_sh[l:l + 1], f"mod_rows_l{l}") for l in range(depth)], axis=0)
    (mod_g,) = _all_gather([mod_sh], "gather_mod", True)
    mod_g = mod_g.reshape(N_DEV, depth, 16, n_mod)
    mods = []
    for l in range(depth):
        mine = lax.dynamic_index_in_dim(mod_g[:, l], me, axis=1, keepdims=False).reshape(3, d)
        cx = mod_g[:, l, 8, :].reshape(3, d)
        mods.append(jnp.concatenate([mine, cx, jnp.zeros((2, d), F32)], axis=0))

    halves = [w_in_bf[0][:, :s // 2], w_in_bf[0][:, s // 2:]]
    near, order = [], [mod_g]
    for j, part in enumerate(halves):
        near.append(_push_start([part], [_landing(part, me)], "near", f"w_in0_start_{j}", after=order))
        order = near[-1][4:]
    w_in_g = [None] * depth
    w_out_g = [None] * depth
    pending, tokens = [], []

    cos, sa, sb_tab = _rope_tables(n_lat, n_ctx)
    t_all = n_lat + n_ctx

    saved = []
    xt = None
    for l in range(depth):
        tiles = _tiles(l, t_all, d)
        names = ["dc", "dlf", "dlb", "qf", "kf", "qb", "kb", "cdf", "cdb", "lg"]
        dec = jnp.stack([ret_decay_f[l], ret_decay_b[l]], axis=0)
        tabs = dict(zip(names, _decay_tables(dec, n_heads, f"decay_tables_l{l}")))
        if l == 0:
            hx, xt = _prenorm_first(x[0], ctx[0], norm_w[0:1], mods[0], "prenorm_l0", after=order)
            gathered, out, after = [], None, hx
            for j in range(2):
                (landed,) = _push_wait(*near[j][:4], "near", after, f"w_in0_wait_{j}")
                relay = _push_start([], [landed], "relay", f"w_in0_relay_start_{j}")
                (landed,) = _push_wait(*relay[:4], "relay", relay[4], f"w_in0_relay_wait_{j}")
                gathered.append(landed)
                tokens = [landed]
                if j == 1:
                    for k in range(depth):
                        srcs = [w_out_bf[k]] + ([w_in_bf[k]] if k > 0 else [])
                        started = _push_start(srcs, [_landing(a, me) for a in srcs], "gather", f"weights_start_l{k}",
                                              after=tokens[-1:])
                        pending.append(started[:4])
                        tokens.append(started[4])
                out = _in_proj(hx, landed, cos, sa, sb_tab, s, j, tiles["in_tm"], f"in_proj_l0_{j}",
                               after=tokens[1:], into=out)
                after = out[0]
            u, qkv = out
            w_in_g[0] = jnp.concatenate(gathered, axis=-1)
        else:
            landed = _push_wait(*pending[l], "gather", xt, f"weights_wait_l{l}")
            w_out_g[l], w_in_g[l] = landed[0].reshape(d, d), landed[1]
            hx = _prenorm(xt, norm_w[l:l + 1], mods[l], n_lat, f"prenorm_l{l}")
            u, qkv = _in_proj(hx, w_in_g[l], cos, sa, sb_tab, s, 0, tiles["in_tm"], f"in_proj_l{l}")
        sf, sb = _state_sweep(qkv, tabs, n_heads, nx, ncc, f"state_sweep_l{l}")
        ycat, o = _mix_fwd(u, qkv, sf, sb, tabs, conv_full[l], conv_norm_w[l:l + 1], ret_norm_w[l:l + 1],
                           n_heads, nx, ncc, f"mix_fwd_l{l}")
        if l == 0:
            (landed,) = _push_wait(*pending[0], "gather", ycat, "weights_wait_l0")
            w_out_g[0] = landed.reshape(d, d)
        m_res = x_new = None
        if l < depth - 1:
            m_res, x_new = _out_proj(ycat, w_out_g[l], xt, mods[l], n_lat, tiles["out_tm"], tiles["out_tn"],
                                     f"out_proj_l{l}")
        else:
            dxt, dm, loss_blk, dfnw, gate_acc = _out_proj_loss(ycat, w_out_g[l], xt, mods[l], loss_target[0],
                                                               final_norm_w.reshape(1, d), n_lat, f"out_proj_loss_l{l}")
        saved.append(dict(tabs=tabs, xt=xt, hx=hx, u=u, qkv=qkv, sf=sf, sb=sb, ycat=ycat, o=o, m=m_res, tiles=tiles))
        xt = x_new

    loss = lax.psum(loss_blk[0, 0], MESH_AXES)

    dmod_x, dmod_c, dnw, dcnw, dgnw, dconv, ddec, dwin, dwout = [], [], [], [], [], [], [], [], []
    started_token = ()
    for l in reversed(range(depth)):
        sv = saved[l]
        tiles = sv["tiles"]
        dycat = _matmul_nt(dm, w_out_g[l], tiles["ob_tn"], f"out_proj_bwd_l{l}", after=started_token)
        dwout.append(_weight_grad(sv["ycat"], dm.reshape(1, *dm.shape), tiles["wo_bm"], _mm_rows(t_all),
                                  f"w_out_grad_l{l}")[0])
        g, dz, db, drz, do, norm_acc = _mix_bwd_a(dycat, sv["u"], sv["o"], conv_full[l], conv_norm_w[l:l + 1],
                                                   ret_norm_w[l:l + 1], n_heads, nx, ncc, f"mix_bwd_a_l{l}")
        gf, gb = _grad_state_sweep(sv["qkv"], do, sv["tabs"], n_heads, nx, ncc, f"grad_state_sweep_l{l}")
        du, conv_acc, dlg = _mix_bwd_b(sv["u"], g, dz, db, drz, sv["qkv"], do, sv["sf"], sv["sb"],
                                       gf, gb, sv["tabs"], cos, sa, sb_tab, conv_full[l], n_heads, nx, ncc,
                                       f"mix_bwd_b_l{l}")
        gate_acc_l = gate_acc
        if l > 0:
            dhx = _in_proj_bwd(du, w_in_g[l], tiles["bwd_tm"], tiles["bwd_gs"], f"in_proj_bwd_l{l}")
            below = (saved[l - 1]["m"], mods[l - 1])
            dwin_l, dxt, pre_acc, dm, gate_acc = _weight_grad_beside_prenorm_bwd(
                sv["hx"], du, dhx, sv["xt"], dxt, norm_w[l:l + 1], mods[l], below, n_lat, f"w_in_grad_l{l}")
        else:
            dwin_l = _weight_grad(sv["hx"], du, tiles["wg_bm"], tiles["wg_bt"], f"w_in_grad_l{l}")
        srcs = [dwin_l, dwout[-1].reshape(N_DEV, r_out, d)]
        lands = [_landing(lax.dynamic_index_in_dim(a, me, axis=0, keepdims=False), me) for a in srcs]
        started = _push_start(srcs, lands, "scatter", f"grads_start_l{l}")
        dwin.append(started[:4])
        started_token = started[4:]
        if l == 0:
            dhx = _in_proj_bwd(du, w_in_g[l], tiles["bwd_tm"], tiles["bwd_gs"], f"in_proj_bwd_l{l}", after=started[4:])
            dxt, pre_acc = _prenorm_bwd_first(dhx, sv["xt"], dxt, norm_w[l:l + 1], mods[l], n_lat, f"prenorm_bwd_l{l}")
        dmod_x.append(jnp.concatenate([pre_acc[0], pre_acc[1], gate_acc_l[2]]))
        dmod_c.append(jnp.concatenate([pre_acc[3], pre_acc[4], gate_acc_l[5]]))
        dnw.append(pre_acc[6])
        dcnw.append(norm_acc[0])
        dgnw.append(norm_acc[1])
        dconv.append(conv_acc[0:3])
        ddec.append(dlg[0:2, :n_heads])
    for lst in (dmod_x, dmod_c, dnw, dcnw, dgnw, dconv, ddec, dwin, dwout):
        lst.reverse()
    grad_x = dxt.reshape(1, n_lat, d)

    rows = []
    for l in range(depth):
        rows += [dmod_x[l], dmod_c[l]]
    (dmod_g,) = _all_gather([_pad_rows(jnp.stack(rows, axis=0), 8)], "gather_dmod", True)
    dmod_g = dmod_g.reshape(N_DEV, 8, 3 * d)
    mine_cols = lax.dynamic_slice(dmod_g, (0, 0, me * n_mod), (N_DEV, 8, n_mod))
    g_wmod, dcc = [], jnp.zeros((d,), F32)
    for l in range(depth):
        gw, dc_part = _mod_grads(mine_cols[:, 2 * l], mine_cols[:, 2 * l + 1], c9, w_mod[l], f"mod_grads_l{l}")
        g_wmod.append(gw)
        dcc = dcc + dc_part[0]

    n_small = 16
    small = jnp.concatenate([
        jnp.stack(dnw, axis=0),
        jnp.concatenate(dcnw).reshape(1, -1),
        jnp.concatenate(dgnw).reshape(1, -1),
        dfnw[0:1],
        dcc.reshape(1, d),
        jnp.stack(dconv, axis=0).reshape(-1, d),
        _pad_cols(jnp.stack(ddec, axis=0).reshape(1, -1), d),
    ], axis=0)
    assert depth * s == d and small.shape[0] <= n_small
    n_rows = small.shape[0]
    (small_g,) = _all_gather([_pad_rows(small, n_small)], "gather_small", True)
    small_g = small_g.reshape(N_DEV, n_small, d)

    def pack_small(nw_, cn_, gn_, fn_, cc_, df_, db_):
        return _pad_rows(jnp.concatenate([
            nw_, cn_.reshape(1, -1), gn_.reshape(1, -1), fn_.reshape(1, d), cc_.reshape(1, d),
            jnp.zeros((n_rows - depth - 5, d), F32),
            _pad_cols(jnp.stack([df_, db_], axis=1).reshape(1, -1), d)], axis=0), n_small)

    w_s = pack_small(norm_w, conv_norm_w, ret_norm_w, final_norm_w, c_ctx, ret_decay_f, ret_decay_b)
    m_s = pack_small(m_norm_w, m_conv_norm_w, m_ret_norm_w, m_final_norm_w, m_c_ctx, m_ret_decay_f, m_ret_decay_b)
    v_s = pack_small(v_norm_w, v_conv_norm_w, v_ret_norm_w, v_final_norm_w, v_c_ctx, v_ret_decay_f, v_ret_decay_b)
    small_out = _sum_adamw(small_g, w_s, m_s, v_s, "adamw_small")

    def unpack_small(a):
        nw_ = a[0:depth]
        cn_ = a[depth].reshape(depth, s)
        gn_ = a[depth + 1].reshape(depth, s)
        fn_ = a[depth + 2]
        cc_ = a[depth + 3]
        dd = a[n_rows - 1, :depth * 2 * n_heads].reshape(depth, 2, n_heads)
        return dict(c_ctx=cc_, norm_w=nw_, conv_norm_w=cn_, ret_norm_w=gn_, ret_decay_f=dd[:, 0], ret_decay_b=dd[:, 1],
                    final_norm_w=fn_)

    res = {}
    for kind, arr in zip(("grad", "delta", "m", "v"), small_out):
        for k_, val in unpack_small(arr).items():
            res[(kind, k_)] = val

    bm_parts = jnp.concatenate([dmod_g[:, 0:2 * depth:2].reshape(N_DEV, depth, 3 * d),
                                dmod_g[:, 1:2 * depth:2].reshape(N_DEV, depth, 3 * d)], axis=0)
    bm_parts = jnp.concatenate([bm_parts, jnp.zeros((2 * N_DEV, 8 - depth, 3 * d), F32)], axis=1)
    pad8 = lambda a: _pad_rows(a, 8)
    bm_out = _sum_adamw(bm_parts, pad8(b_mod), pad8(m_b_mod), pad8(v_b_mod), "adamw_b_mod")
    for kind, arr in zip(("grad", "delta", "m", "v"), bm_out):
        res[(kind, "b_mod")] = arr[:depth]

    conv_rows = small_g[:, depth + 4:depth + 4 + 3 * depth * s // d].reshape(N_DEV, depth * 3, s)
    conv_mine = lax.dynamic_slice(conv_rows, (0, 0, me * n_cw), (N_DEV, depth * 3, n_cw))
    conv_mine = jnp.concatenate([conv_mine, jnp.zeros((N_DEV, 8 - depth * 3, n_cw), F32)], axis=1)
    cw2 = lambda a: _pad_rows(a.reshape(depth * 3, n_cw), 8)
    cw_out = _sum_adamw(conv_mine, cw2(conv_w), cw2(m_conv_w), cw2(v_conv_w), "adamw_conv_w")
    for kind, arr in zip(("grad", "delta", "m", "v"), cw_out):
        res[(kind, "conv_w")] = arr[:depth * 3].reshape(depth, 3, n_cw)

    wm_out = _sum_adamw(jnp.stack(g_wmod, axis=0).reshape(1, depth * d, n_mod), w_mod.reshape(depth * d, n_mod),
                        m_w_mod.reshape(depth * d, n_mod), v_w_mod.reshape(depth * d, n_mod), "adamw_w_mod")
    for kind, arr in zip(("grad", "delta", "m", "v"), wm_out):
        res[(kind, "w_mod")] = arr.reshape(depth, d, n_mod)

    wi_out = wo_out = None
    after = wm_out[0]
    for l in reversed(range(depth)):
        win_parts, wout_parts = _push_wait(*dwin[l], "scatter", after, f"grads_wait_l{l}")
        wi_out = _sum_adamw(win_parts, w_in.reshape(depth * d, s), m_w_in.reshape(depth * d, s),
                            v_w_in.reshape(depth * d, s), f"adamw_w_in_l{l}", row0=l * d, into=wi_out)
        wo_out = _sum_adamw(wout_parts, w_out.reshape(depth * r_out, d), m_w_out.reshape(depth * r_out, d),
                            v_w_out.reshape(depth * r_out, d), f"adamw_w_out_l{l}", row0=l * r_out, into=wo_out)
        after = wo_out[0]
    for kind, arr in zip(("grad", "delta", "m", "v"), wi_out):
        res[(kind, "w_in")] = arr.reshape(depth, d, s)
    for kind, arr in zip(("grad", "delta", "m", "v"), wo_out):
        res[(kind, "w_out")] = arr.reshape(depth, r_out, d)

    order = ["c_ctx", "norm_w", "w_mod", "b_mod", "w_in", "conv_w", "conv_norm_w", "ret_norm_w", "ret_decay_f",
             "ret_decay_b", "w_out", "final_norm_w"]
    outs = [loss, grad_x]
    for kind in ("grad", "delta", "m", "v"):
        outs += [res[(kind, k_)] for k_ in order]
    return tuple(outs)
```

```python
import functools

import jax
import jax.numpy as jnp
from jax import lax
from jax.experimental import pallas as pl
from jax.experimental.pallas import tpu as pltpu

F32 = jnp.float32
BF16 = jnp.bfloat16

EPS = 1e-6
CHUNK = 128
HEAD_DIM = 128
GRID_W = 64
ROPE_BASE = 10000.0
N_DEV = 8
ADAM_LR, ADAM_B1, ADAM_B2, ADAM_EPS, ADAM_WD, ADAM_STEP = 0.001, 0.9, 0.999, 1e-08, 0.01, 10

ROW_TILE = 256
V7X_VMEM_LIMIT = 56 * 1024 * 1024
MESH_AXES = ("x", "y", "c")

NN = ((1,), (0,))
NT = ((1,), (1,))
TN = ((0,), (0,))


def _dot(a, b, dims):
    return lax.dot_general(a, b, (dims, ((), ())), preferred_element_type=F32)


def _params(sem=None):
    if sem is None:
        return pltpu.CompilerParams(vmem_limit_bytes=V7X_VMEM_LIMIT)
    return pltpu.CompilerParams(dimension_semantics=sem, vmem_limit_bytes=V7X_VMEM_LIMIT)


def _silu(z):
    return z * jax.nn.sigmoid(z)


def _dsilu(z):
    s = jax.nn.sigmoid(z)
    return s * (1.0 + z * (1.0 - s))


def _silu_and_slope(z):
    s = jax.nn.sigmoid(z)
    return z * s, s * (1.0 + z * (1.0 - s))


def _sum_all(a):
    return jnp.sum(jnp.sum(a, axis=1, keepdims=True), axis=0, keepdims=True)


def _mm_rows(t):
    return 768 if t % 768 == 0 else ROW_TILE


def _rows_or(t, rows):
    return rows if t % rows == 0 else _mm_rows(t)


def _tiles(layer, t, d):
    return dict(in_tm=_rows_or(t, 1408), bwd_tm=_mm_rows(t), bwd_gs=2, wg_bm=d, wg_bt=_mm_rows(t),
                out_tm=_mm_rows(t), out_tn=min(d, 1024), wo_bm=d, ob_tn=d)


def _full(shape):
    n = len(shape)
    return pl.BlockSpec(shape, lambda *_: (0,) * n)


def _peers(x, y, c):
    return [(x, y, 1 - c), (1 - x, y, c), (x, 1 - y, c), (1 - x, 1 - y, c),
            (1 - x, y, 1 - c), (x, 1 - y, 1 - c), (1 - x, 1 - y, 1 - c)]


def _lin(p):
    return 4 * p[0] + 2 * p[1] + p[2]


def _all_gather(arrays, name, in_vmem):
    n_arr = len(arrays)
    space = pltpu.VMEM if in_vmem else pl.ANY

    def body(*refs):
        ins, outs = refs[:n_arr], refs[n_arr:2 * n_arr]
        send_sems, recv_sems, local_sems = refs[2 * n_arr:]
        x, y, c = lax.axis_index("x"), lax.axis_index("y"), lax.axis_index("c")
        me, sibling = (x, y, c), (x, y, 1 - c)
        chips = [(1 - x, y), (x, 1 - y), (1 - x, 1 - y)]
        every = []
        locals_ = []
        for a in range(n_arr):
            m_per = ins[a].shape[0]
            out_ref = outs[a]

            def rows(p, out_ref=out_ref, m_per=m_per):
                return out_ref.at[pl.ds(_lin(p) * m_per, m_per), :]

            def copy(k, block, to, src=None, a=a, rows=rows):
                return pltpu.make_async_remote_copy(
                    src_ref=rows(block) if src is None else src, dst_ref=rows(block),
                    send_sem=send_sems.at[a, k], recv_sem=recv_sems.at[a, k],
                    device_id=to, device_id_type=pl.DeviceIdType.MESH)

            mine = pltpu.make_async_copy(ins[a], rows(me), local_sems.at[a])
            mine.start()
            locals_.append(mine)
            first = [copy(0, me, sibling, src=ins[a])]
            first += [copy(1 + j, me, (*chip, c), src=ins[a]) for j, chip in enumerate(chips)]
            for cp in first:
                cp.start()
            every.append((copy, first))
        sends = []
        for a in range(n_arr):
            copy, first = every[a]
            passed = [copy(4 + j, (*chip, c), sibling) for j, chip in enumerate(chips)]
            for j, chip in enumerate(chips):
                copy(1 + j, (*chip, c), me).wait_recv()
                passed[j].start()
            sends += first + passed
        for a in range(n_arr):
            copy, _ = every[a]
            copy(0, sibling, me).wait_recv()
            for j, chip in enumerate(chips):
                copy(4 + j, (*chip, 1 - c), me).wait_recv()
        for cp in sends:
            cp.wait_send()
        for mine in locals_:
            mine.wait()

    outs = pl.pallas_call(
        body, name=name,
        out_shape=[jax.ShapeDtypeStruct((N_DEV * a.shape[0], a.shape[1]), a.dtype) for a in arrays],
        in_specs=[pl.BlockSpec(memory_space=space)] * n_arr,
        out_specs=[pl.BlockSpec(memory_space=space)] * n_arr,
        scratch_shapes=[pltpu.SemaphoreType.DMA((n_arr, 7)), pltpu.SemaphoreType.DMA((n_arr, 7)),
                        pltpu.SemaphoreType.DMA((n_arr,))],
        compiler_params=_params(),
    )(*arrays)
    return list(outs)


_HBM = pl.BlockSpec(memory_space=pltpu.HBM)
_SEM = pl.BlockSpec(memory_space=pltpu.SEMAPHORE)
_DATAFLOW = pltpu.SideEffectType.DATAFLOW_SIDE_EFFECTING


PUSH_COPIES = {"scatter": 7, "gather": 7, "near": 4, "relay": 3}


def _push_copies(src_refs, land_refs, send_sems, recv_sems, mode):
    x, y, c = lax.axis_index("x"), lax.axis_index("y"), lax.axis_index("c")
    me, sibling = (x, y, c), (x, y, 1 - c)
    n_k = PUSH_COPIES[mode]
    out, back = [], []
    if mode == "relay":
        for k, chip in enumerate([(1 - x, y), (x, 1 - y), (1 - x, 1 - y)]):
            for a, land in enumerate(land_refs):
                sems = dict(send_sem=send_sems.at[n_k * a + k], recv_sem=recv_sems.at[n_k * a + k],
                            device_id=sibling, device_id_type=pl.DeviceIdType.MESH)
                mine = land.at[_lin((*chip, c))]
                out.append(pltpu.make_async_remote_copy(src_ref=mine, dst_ref=mine, **sems))
                back.append(pltpu.make_async_remote_copy(src_ref=mine, dst_ref=land.at[_lin((*chip, 1 - c))], **sems))
        return out, back
    for k, peer in enumerate(_peers(x, y, c)[:n_k]):
        for a, (src, land) in enumerate(zip(src_refs, land_refs)):
            sems = dict(send_sem=send_sems.at[n_k * a + k], recv_sem=recv_sems.at[n_k * a + k],
                        device_id=peer, device_id_type=pl.DeviceIdType.MESH)
            mine = src.at[_lin(peer)] if mode == "scatter" else src
            out.append(pltpu.make_async_remote_copy(src_ref=mine, dst_ref=land.at[_lin(me)], **sems))
            back.append(pltpu.make_async_remote_copy(src_ref=mine, dst_ref=land.at[_lin(peer)], **sems))
    return out, back


def _push_start(srcs, lands, mode, name, after=()):
    n_src, n = len(srcs), len(lands)
    n_buf = n_src + n
    n_in = n_buf + len(after)
    n_sem = PUSH_COPIES[mode] * n

    def body(*refs):
        send_sems, recv_sems = refs[n_in], refs[n_in + 1]
        out, _ = _push_copies(refs[:n_src], refs[n_src:n_buf], send_sems, recv_sems, mode)
        for cp in out:
            cp.start()
        token = refs[-1]
        token[...] = jnp.zeros_like(token)

    both = list(srcs) + list(lands)
    res = pl.pallas_call(
        body, name=name,
        out_shape=[pltpu.SemaphoreType.DMA((n_sem,)), pltpu.SemaphoreType.DMA((n_sem,))]
        + [pltpu.HBM(a.shape, a.dtype) for a in both] + [jax.ShapeDtypeStruct((8, 128), F32)],
        in_specs=[_HBM] * n_buf + [pl.BlockSpec(memory_space=pl.ANY)] * len(after),
        out_specs=[_SEM, _SEM] + [_HBM] * n_buf + [pl.BlockSpec(memory_space=pltpu.VMEM)],
        input_output_aliases={i: 2 + i for i in range(n_buf)},
        compiler_params=pltpu.CompilerParams(has_side_effects=_DATAFLOW),
    )(*[pltpu.with_memory_space_constraint(a, pltpu.HBM) for a in both], *after)
    return res[0], res[1], list(res[2:2 + n_src]), list(res[2 + n_src:2 + n_buf]), res[-1]


def _push_wait(send_sems, recv_sems, srcs, lands, mode, after, name):
    n_src, n = len(srcs), len(lands)
    n_buf = n_src + n

    def body(*refs):
        out, back = _push_copies(refs[:n_src], refs[n_src:n_buf], refs[n_buf], refs[n_buf + 1], mode)
        for cp in out:
            cp.wait_send()
        for cp in back:
            cp.wait_recv()

    both = list(srcs) + list(lands)
    res = pl.pallas_call(
        body, name=name,
        out_shape=[pltpu.HBM(a.shape, a.dtype) for a in both],
        in_specs=[_HBM] * n_buf + [_SEM, _SEM, pl.BlockSpec(memory_space=pl.ANY)],
        out_specs=[_HBM] * n_buf,
        input_output_aliases={i: i for i in range(n_buf)},
        compiler_params=pltpu.CompilerParams(has_side_effects=_DATAFLOW),
    )(*both, send_sems, recv_sems, after)
    return list(res[n_src:])


def _landing(own, me):
    zone = lax.empty((N_DEV,) + own.shape, own.dtype)
    return lax.dynamic_update_slice(zone, own[None], (me,) + (0,) * own.ndim)


def _mod_rows(c9, w_mod, b_sh, name):
    n = w_mod.shape[1]

    def body(c_ref, w_ref, b_ref, o_ref):
        s9 = _silu(c_ref[...]).astype(BF16)
        o_ref[...] = _dot(s9, w_ref[...].astype(BF16), NN) + b_ref[...]

    return pl.pallas_call(body, name=name, out_shape=jax.ShapeDtypeStruct((16, n), F32),
                          compiler_params=_params())(c9, w_mod, b_sh)


def _mod_grads(dm_rows, dc_rows, c9, w_mod, name):
    d, n = w_mod.shape

    def body(dm_ref, dc_ref, c_ref, w_ref, gw_ref, dc_out):
        dc = dc_ref[...]
        tot = dc[0:1]
        for j in range(1, N_DEV):
            tot = tot + dc[j:j + 1]
        row = lax.broadcasted_iota(jnp.int32, (8, n), 0)
        lower = jnp.where(row == 0, tot, 0.0)
        dmod9 = jnp.concatenate([dm_ref[...], lower], axis=0).astype(BF16)
        c9v = c_ref[...]
        s9 = _silu(c9v).astype(BF16)
        gw_ref[...] = _dot(s9, dmod9, TN)
        ds = _dot(lower.astype(BF16), w_ref[...].astype(BF16), NT)
        dc_out[...] = ds * _dsilu(c9v[8:16])

    return pl.pallas_call(body, name=name,
                          out_shape=[jax.ShapeDtypeStruct((d, n), F32), jax.ShapeDtypeStruct((8, d), F32)],
                          compiler_params=_params())(dm_rows, dc_rows, c9, w_mod)


def _decay_tables(dec, n_heads, name):
    c = CHUNK

    def body(dec_ref, dc_ref, dlf_ref, dlb_ref, qf_ref, kf_ref, qb_ref, kb_ref, cdf_ref, cdb_ref, lg_ref):
        h = pl.program_id(0)
        d = dec_ref[...]
        lane = lax.broadcasted_iota(jnp.int32, d.shape, 1)
        lg = -jnp.exp(jnp.sum(jnp.where(lane == h, d, 0.0), axis=1, keepdims=True))
        lgf, lgb = lg[0:1], lg[1:2]
        i = lax.broadcasted_iota(jnp.int32, (c, c), 0).astype(F32)
        j = lax.broadcasted_iota(jnp.int32, (c, c), 1).astype(F32)
        diff = i - j
        d_f = jnp.where(diff >= 0, jnp.exp(lgf * jnp.maximum(diff, 0.0)), 0.0)
        d_b = jnp.where(diff <= 0, jnp.exp(lgb * jnp.maximum(-diff, 0.0)), 0.0)
        dc_ref[...] = d_f + d_b
        dlf_ref[...] = diff * d_f
        dlb_ref[...] = -diff * d_b
        pos = lax.broadcasted_iota(jnp.int32, (c, HEAD_DIM), 0).astype(F32)
        qf_ref[...] = jnp.exp(lgf * (pos + 1.0))
        kf_ref[...] = jnp.exp(lgf * (c - 1.0 - pos))
        qb_ref[...] = jnp.exp(lgb * (c - pos))
        kb_ref[...] = jnp.exp(lgb * pos)
        ones = jnp.ones((8, HEAD_DIM), F32)
        cdf_ref[...] = jnp.exp(lgf * float(c)) * ones
        cdb_ref[...] = jnp.exp(lgb * float(c)) * ones

        @pl.when(h == 0)
        def _():
            lg_ref[...] = jnp.zeros_like(lg_ref)

        row8 = lax.broadcasted_iota(jnp.int32, (8, HEAD_DIM), 0)
        lane8 = lax.broadcasted_iota(jnp.int32, (8, HEAD_DIM), 1)
        lg_ref[...] += (jnp.where((row8 == 0) & (lane8 == h), lgf, 0.0)
                        + jnp.where((row8 == 1) & (lane8 == h), lgb, 0.0))

    def per_head(*tail):
        return pl.BlockSpec((None,) + tail, lambda h: (h,) + (0,) * len(tail))

    shapes = [(c, c)] * 3 + [(c, HEAD_DIM)] * 4 + [(8, HEAD_DIM)] * 2
    return pl.pallas_call(
        body, name=name, grid=(n_heads,),
        in_specs=[_full(dec.shape)],
        out_specs=[per_head(*s) for s in shapes] + [_full((8, HEAD_DIM))],
        out_shape=[jax.ShapeDtypeStruct((n_heads,) + s, F32) for s in shapes]
        + [jax.ShapeDtypeStruct((8, HEAD_DIM), F32)],
        compiler_params=_params(("arbitrary",)),
    )(dec)


def _modulate(x, nw, shift, scale):
    r = lax.rsqrt(jnp.mean(x * x, axis=-1, keepdims=True) + EPS)
    return ((x * r) * nw * (1.0 + scale) + shift).astype(BF16)


def _prenorm(xt, nw, mod, n_lat, name):
    t, d = xt.shape
    nxb = n_lat // ROW_TILE

    def body(x_ref, nw_ref, mod_ref, o_ref):
        ctx = pl.program_id(0) >= nxb
        m = mod_ref[...]
        o_ref[...] = _modulate(x_ref[...], nw_ref[...], jnp.where(ctx, m[3:4], m[0:1]), jnp.where(ctx, m[4:5], m[1:2]))

    row = pl.BlockSpec((ROW_TILE, d), lambda i: (i, 0))
    return pl.pallas_call(body, name=name, grid=(t // ROW_TILE,),
                          in_specs=[row, _full((1, d)), _full((8, d))],
                          out_specs=row, out_shape=jax.ShapeDtypeStruct((t, d), BF16),
                          compiler_params=_params(("parallel",)))(xt, nw, mod)


def _prenorm_first(x, ctx, nw, mod, name, after=()):
    n_lat, d = x.shape
    t = n_lat + ctx.shape[0]
    nxb = n_lat // ROW_TILE

    def body(x_ref, c_ref, nw_ref, mod_ref, *rest):
        o_ref, xt_ref = rest[-2:]
        m = mod_ref[...]
        nw_v = nw_ref[...]

        @pl.when(pl.program_id(0) < nxb)
        def _():
            xv = x_ref[...]
            xt_ref[...] = xv
            o_ref[...] = _modulate(xv, nw_v, m[0:1], m[1:2])

        @pl.when(pl.program_id(0) >= nxb)
        def _():
            xv = c_ref[...]
            xt_ref[...] = xv
            o_ref[...] = _modulate(xv, nw_v, m[3:4], m[4:5])

    row = pl.BlockSpec((ROW_TILE, d), lambda i: (i, 0))
    return pl.pallas_call(
        body, name=name, grid=(t // ROW_TILE,),
        in_specs=[pl.BlockSpec((ROW_TILE, d), lambda i: (jnp.minimum(i, nxb - 1), 0)),
                  pl.BlockSpec((ROW_TILE, d), lambda i: (jnp.maximum(i - nxb, 0), 0)), _full((1, d)), _full((8, d))]
        + [pl.BlockSpec(memory_space=pl.ANY)] * len(after),
        out_specs=[row, row], out_shape=[jax.ShapeDtypeStruct((t, d), BF16), jax.ShapeDtypeStruct((t, d), F32)],
        compiler_params=_params(("parallel",)))(x, ctx, nw, mod, *after)


def _rope_fwd(v, cos, sa, sb):
    return v * cos + pltpu.roll(v, 96, 1) * sa + pltpu.roll(v, 32, 1) * sb


def _rope_bwd(g, cos, sa, sb):
    return g * cos + pltpu.roll(g * sa, 32, 1) + pltpu.roll(g * sb, 96, 1)


N_PLAIN = 5


def _in_proj(hx, wg, cos, sa, sb, s, part, tm, name, after=(), into=None):
    t, d = hx.shape
    n_seg, _, n = wg.shape
    nb = t // tm
    k_scale = HEAD_DIM ** -0.5
    kept = [] if into is None else list(into)

    def body(a_ref, w_ref, cos_ref, sa_ref, sb_ref, *rest):
        u_ref, qkv_ref = rest[-2:]
        g = pl.program_id(1)
        acc = _dot(a_ref[...], w_ref[...], NN)

        @pl.when(g < N_PLAIN)
        def _():
            u_ref[...] = acc

        @pl.when(g == N_PLAIN + 2)
        def _():
            qkv_ref[...] = acc.astype(BF16)

        for which, scale in ((N_PLAIN, 1.0), (N_PLAIN + 1, k_scale)):
            @pl.when(g == which)
            def _(scale=scale):
                co, a, b = cos_ref[...], sa_ref[...], sb_ref[...]
                for h in range(n // HEAD_DIM):
                    sl = slice(h * HEAD_DIM, (h + 1) * HEAD_DIM)
                    qkv_ref[:, sl] = (_rope_fwd(acc[:, sl], co, a, b) * scale).astype(BF16)

    def w_seg(g):
        return jnp.where(g < N_PLAIN - 1, g, jnp.where(g == N_PLAIN - 1, n_seg - 1, g - 1))

    def qkv_at(i, g):
        held = (jnp.where(i == 0, 0, 2), jnp.maximum(i - 1, 0))
        return (jnp.where(g < N_PLAIN, held[0], g - N_PLAIN), jnp.where(g < N_PLAIN, held[1], i), part)

    tab = pl.BlockSpec((tm, HEAD_DIM), lambda i, g: (i, 0))
    hbm = pl.BlockSpec(memory_space=pl.ANY)
    return pl.pallas_call(
        body, name=name, grid=(nb, n_seg),
        in_specs=[pl.BlockSpec((tm, d), lambda i, g: (i, 0)), pl.BlockSpec((None, d, n), lambda i, g: (w_seg(g), 0, 0)),
                  tab, tab, tab] + [hbm] * (len(after) + len(kept)),
        out_specs=[pl.BlockSpec((None, tm, n), lambda i, g: (jnp.minimum(g, N_PLAIN - 1), i, part)),
                   pl.BlockSpec((None, tm, n), qkv_at)],
        out_shape=[jax.ShapeDtypeStruct((N_PLAIN, t, s), F32), jax.ShapeDtypeStruct((3, t, s), BF16)],
        input_output_aliases={5 + len(after) + j: j for j in range(len(kept))},
        compiler_params=_params(("arbitrary", "arbitrary")))(hx, wg, cos, sa, sb, *after, *kept)


def _pair_sweep(xs, ys, tab_f, tab_b, cdf, cdb, n_heads, nx, ncc, reverse, name):
    t, s = xs[0].shape[-2:]
    nc = nx + ncc
    c = CHUNK
    n_pair = nc // 2
    assert nx % 2 == 0 and ncc % 2 == 0

    def f_pair(i):
        step = n_pair - 1 - i if reverse else i
        return jnp.where(step < ncc // 2, nx // 2 + step, step - ncc // 2)

    def b_pair(i):
        return i if reverse else n_pair - 1 - i

    f_subs = (1, 0) if reverse else (0, 1)
    b_subs = (0, 1) if reverse else (1, 0)

    def body(xf_ref, yf_ref, xb_ref, yb_ref, tf, tb, cdf_ref, cdb_ref, sf_out, sb_out, sf, sb):
        @pl.when(pl.program_id(0) == 0)
        def _():
            sf[...] = jnp.zeros_like(sf)
            sb[...] = jnp.zeros_like(sb)

        for step in range(2):
            for x_ref, y_ref, tab, cd, out, st, sub in ((xf_ref, yf_ref, tf, cdf_ref, sf_out, sf, f_subs[step]),
                                                        (xb_ref, yb_ref, tb, cdb_ref, sb_out, sb, b_subs[step])):
                rows = pl.ds(sub * c, c)
                for h in range(n_heads):
                    sl = pl.ds(h * HEAD_DIM, HEAD_DIM)
                    out[sub, h] = st[h].astype(BF16)
                    xd = (x_ref[rows, sl].astype(F32) * tab[h]).astype(BF16)
                    st[h] = cd[h][0:1, :] * st[h] + _dot(xd, y_ref[rows, sl], TN)

    def spec(arr, pair):
        lead = arr[1]
        if lead is None:
            return pl.BlockSpec((2 * c, s), lambda i: (pair(i), 0))
        return pl.BlockSpec((None, 2 * c, s), lambda i: (lead, pair(i), 0))

    st_blk = (2, n_heads, HEAD_DIM, HEAD_DIM)
    return pl.pallas_call(
        body, name=name, grid=(n_pair,),
        in_specs=[spec(xs, f_pair), spec(ys, f_pair), spec(xs, b_pair), spec(ys, b_pair),
                  _full((n_heads, c, HEAD_DIM)), _full((n_heads, c, HEAD_DIM)),
                  _full((n_heads, 8, HEAD_DIM)), _full((n_heads, 8, HEAD_DIM))],
        out_specs=[pl.BlockSpec(st_blk, lambda i: (f_pair(i), 0, 0, 0)), pl.BlockSpec(st_blk, lambda i: (b_pair(i), 0, 0, 0))],
        out_shape=[jax.ShapeDtypeStruct((nc, n_heads, HEAD_DIM, HEAD_DIM), BF16)] * 2,
        scratch_shapes=[pltpu.VMEM((n_heads, HEAD_DIM, HEAD_DIM), F32)] * 2,
        compiler_params=_params(("arbitrary",)),
    )(xs[0], ys[0], xs[0], ys[0], tab_f, tab_b, cdf, cdb)


def _state_sweep(qkv, tabs, n_heads, nx, ncc, name):
    return _pair_sweep((qkv, 1), (qkv, 2), tabs["kf"], tabs["kb"], tabs["cdf"], tabs["cdb"], n_heads, nx, ncc, False, name)


MIX_CHUNKS = 2
MIX_ROWS = MIX_CHUNKS * CHUNK


def _halo_specs(s, n8):
    per = MIX_ROWS // 8

    def prev(g):
        return pl.BlockSpec((None, 8, s), lambda i: (g, jnp.maximum(i * per - 1, 0), 0))

    def nxt(g):
        return pl.BlockSpec((None, 8, s), lambda i: (g, jnp.minimum((i + 1) * per, n8 - 1), 0))

    return prev, nxt


def _shifted(a, before, after, has_prev, has_next):
    rows = a.shape[0]
    rowi = lax.broadcasted_iota(jnp.int32, a.shape, 0)
    am = jnp.where(rowi == 0, jnp.where(has_prev, before, 0.0), pltpu.roll(a, 1, 0))
    ap = jnp.where(rowi == rows - 1, jnp.where(has_next, after, 0.0), pltpu.roll(a, rows - 1, 0))
    return am, ap


def _neighbours(i, nx, nc):
    nxb, ncb = nx // MIX_CHUNKS, nc // MIX_CHUNKS
    return (i != 0) & (i != nxb), (i != nxb - 1) & (i != ncb - 1)


def _mix_fwd(u, qkv, sf, sb, tabs, conv_w, cnw, gnw, n_heads, nx, ncc, name):
    _, t, s = u.shape
    nc = nx + ncc
    c = CHUNK
    assert nx % MIX_CHUNKS == 0 and ncc % MIX_CHUNKS == 0

    def body(h_ref, b_ref, c_ref, z_ref, rz_ref, hp_ref, hn_ref, cp_ref, cn_ref, q_ref, k_ref, v_ref,
             sf_ref, sb_ref, dc_ref, qft, qbt, w_ref, cnw_ref, gnw_ref, y_ref, o_ref):
        i = pl.program_id(0)
        has_prev, has_next = _neighbours(i, nx, nc)
        a = c_ref[...] * h_ref[...]
        am, ap = _shifted(a, cp_ref[7:8] * hp_ref[7:8], cn_ref[0:1] * hn_ref[0:1], has_prev, has_next)
        w = w_ref[...]
        y0 = w[0:1] * am + w[1:2] * a + w[2:3] * ap
        yb = b_ref[...] * y0
        r = lax.rsqrt(jnp.mean(yb * yb, axis=-1, keepdims=True) + EPS)
        y_ref[:, pl.ds(0, s)] = (_silu(z_ref[...]) * ((yb * r) * cnw_ref[...])).astype(BF16)
        for sub in range(MIX_CHUNKS):
            rows = pl.ds(sub * c, c)
            for h in range(n_heads):
                sl = pl.ds(h * HEAD_DIM, HEAD_DIM)
                q, k, v = q_ref[rows, sl], k_ref[rows, sl], v_ref[rows, sl]
                p = (_dot(q, k, NT) * dc_ref[h]).astype(BF16)
                o = _dot(p, v, NN)
                qf = q.astype(F32)
                o += _dot((qf * qft[h]).astype(BF16), sf_ref[sub, h], NN)
                o += _dot((qf * qbt[h]).astype(BF16), sb_ref[sub, h], NN)
                o_ref[rows, sl] = o
                mu = jnp.mean(o, axis=-1, keepdims=True)
                var = jnp.mean(jnp.square(o - mu), axis=-1, keepdims=True)
                on = (o - mu) * lax.rsqrt(var + EPS)
                y_ref[rows, pl.ds(s + h * HEAD_DIM, HEAD_DIM)] = (
                    _silu(rz_ref[rows, sl]) * (on * gnw_ref[:, sl])).astype(BF16)

    def seg(g):
        return pl.BlockSpec((None, MIX_ROWS, s), lambda i: (g, i, 0))

    prev, nxt = _halo_specs(s, t // 8)
    row = pl.BlockSpec((MIX_ROWS, s), lambda i: (i, 0))
    st = pl.BlockSpec((MIX_CHUNKS, n_heads, HEAD_DIM, HEAD_DIM), lambda i: (i, 0, 0, 0))
    return pl.pallas_call(
        body, name=name, grid=(nc // MIX_CHUNKS,),
        in_specs=[seg(0), seg(1), seg(2), seg(3), seg(4), prev(0), nxt(0), prev(2), nxt(2), seg(0), seg(1), seg(2),
                  st, st, _full((n_heads, c, c)), _full((n_heads, c, HEAD_DIM)), _full((n_heads, c, HEAD_DIM)),
                  _full((3, s)), _full((1, s)), _full((1, s))],
        out_specs=[pl.BlockSpec((MIX_ROWS, 2 * s), lambda i: (i, 0)), row],
        out_shape=[jax.ShapeDtypeStruct((t, 2 * s), BF16), jax.ShapeDtypeStruct((t, s), F32)],
        compiler_params=_params(("parallel",)),
    )(u, u, u, u, u, u, u, u, u, qkv, qkv, qkv, sf, sb, tabs["dc"], tabs["qf"], tabs["qb"], conv_w, cnw, gnw)


def _row_gate(mod_ref, row0, rows, n_lat, col):
    rowi = row0 + lax.broadcasted_iota(jnp.int32, (rows, 1), 0)
    return jnp.where(rowi >= n_lat, mod_ref[5:6, col], mod_ref[2:3, col])


def _out_proj(ycat, w_out, xt, mod, n_lat, tm, tn, name):
    t, d = xt.shape

    def body(a_ref, w_ref, x_ref, mod_ref, m_ref, xo_ref):
        m = _dot(a_ref[...], w_ref[...], NN)
        m_ref[...] = m
        gate = _row_gate(mod_ref, pl.program_id(1) * tm, tm, n_lat, slice(None))
        xo_ref[...] = x_ref[...] + gate * m

    blk = pl.BlockSpec((tm, tn), lambda j, i: (i, j))
    return pl.pallas_call(
        body, name=name, grid=(d // tn, t // tm),
        in_specs=[pl.BlockSpec((tm, d), lambda j, i: (i, 0)), pl.BlockSpec((d, tn), lambda j, i: (0, j)), blk,
                  pl.BlockSpec((8, tn), lambda j, i: (0, j))],
        out_specs=[blk, blk], out_shape=[jax.ShapeDtypeStruct((t, d), F32)] * 2,
        compiler_params=_params(("parallel", "parallel")))(ycat, w_out, xt, mod)


def _out_proj_loss(ycat, w_out, xt, mod, tgt, fnw, n_lat, name):
    t, d = xt.shape
    nb = t // ROW_TILE
    nxb = n_lat // ROW_TILE

    def body(a_ref, w_ref, x_ref, mod_ref, t_ref, fw_ref, dx_ref, dm_ref, loss_ref, dw_ref, gacc_ref, xs, ms):
        i = pl.program_id(0)

        @pl.when(i == 0)
        def _():
            xs[...] = jnp.zeros_like(xs)
            ms[...] = jnp.zeros_like(ms)
            loss_ref[...] = jnp.zeros_like(loss_ref)
            dw_ref[...] = jnp.zeros_like(dw_ref)
            gacc_ref[...] = jnp.zeros_like(gacc_ref)

        def step(cur, prev):
            mv = mod_ref[...]
            x_prev, m_prev = xs[prev], ms[prev]
            valid = (i >= 1) & (i - 1 < nxb)
            w = fw_ref[...]
            r = lax.rsqrt(jnp.mean(x_prev * x_prev, axis=-1, keepdims=True) + EPS)
            xn = x_prev * r
            e = xn * w - t_ref[...]
            loss = 0.5 * jnp.sum(jnp.mean(e * e, axis=-1, keepdims=True), axis=0, keepdims=True)
            loss_ref[...] += jnp.where(valid, loss, 0.0)
            dy = e * (1.0 / d)
            dw_ref[0:1, :] += jnp.where(valid, jnp.sum(dy * xn, axis=0, keepdims=True), 0.0)
            dxn = dy * w
            dx = jnp.where(valid, r * (dxn - xn * jnp.mean(dxn * xn, axis=-1, keepdims=True)), 0.0)
            dx_ref[...] = dx
            dm_ref[...] = (dx * mv[2:3]).astype(BF16)
            gacc_ref[2:3, :] += jnp.sum(dx * m_prev, axis=0, keepdims=True)

            m = _dot(a_ref[...], w_ref[...], NN)
            gate = jnp.where(jnp.minimum(i, nb - 1) >= nxb, mv[5:6], mv[2:3])
            xs[cur] = x_ref[...] + gate * m
            ms[cur] = m

        @pl.when(i % 2 == 0)
        def _():
            step(0, 1)

        @pl.when(i % 2 == 1)
        def _():
            step(1, 0)

    cur = pl.BlockSpec((ROW_TILE, d), lambda i: (jnp.minimum(i, nb - 1), 0))
    prev = pl.BlockSpec((ROW_TILE, d), lambda i: (jnp.maximum(i - 1, 0), 0))
    return pl.pallas_call(
        body, name=name, grid=(nb + 1,),
        in_specs=[cur, _full((d, d)), cur, _full((8, d)),
                  pl.BlockSpec((ROW_TILE, d), lambda i: (jnp.clip(i - 1, 0, nxb - 1), 0)), _full((1, d))],
        out_specs=[prev, prev, _full((8, HEAD_DIM)), _full((8, d)), _full((8, d))],
        out_shape=[jax.ShapeDtypeStruct((t, d), F32), jax.ShapeDtypeStruct((t, d), BF16),
                   jax.ShapeDtypeStruct((8, HEAD_DIM), F32), jax.ShapeDtypeStruct((8, d), F32),
                   jax.ShapeDtypeStruct((8, d), F32)],
        scratch_shapes=[pltpu.VMEM((2, ROW_TILE, d), F32), pltpu.VMEM((2, ROW_TILE, d), F32)],
        compiler_params=_params(("arbitrary",)))(ycat, w_out, xt, mod, tgt, fnw)


def _matmul_nt(a, w, tn, name, after=()):
    t, k = a.shape
    n = w.shape[0]
    tm = _mm_rows(t)

    def body(a_ref, w_ref, *rest):
        rest[-1][...] = _dot(a_ref[...], w_ref[...], NT)

    return pl.pallas_call(
        body, name=name, grid=(n // tn, t // tm),
        in_specs=[pl.BlockSpec((tm, k), lambda j, i: (i, 0)), pl.BlockSpec((tn, k), lambda j, i: (j, 0))]
        + [pl.BlockSpec(memory_space=pl.ANY)] * len(after),
        out_specs=pl.BlockSpec((tm, tn), lambda j, i: (i, j)),
        out_shape=jax.ShapeDtypeStruct((t, n), F32),
        compiler_params=_params(("parallel", "parallel")))(a, w, *after)


def _weight_grad(a, b, bm, bt, name):
    t, m = a.shape
    n_g, _, n = b.shape
    nt = t // bt

    def body(a_ref, b_ref, o_ref, acc):
        k = pl.program_id(2)

        @pl.when(k == 0)
        def _():
            acc[...] = jnp.zeros_like(acc)

        acc[...] += _dot(a_ref[...], b_ref[...], TN)

        @pl.when(k == nt - 1)
        def _():
            o_ref[...] = acc[...].astype(o_ref.dtype)

    return pl.pallas_call(
        body, name=name, grid=(n_g, m // bm, nt),
        in_specs=[pl.BlockSpec((bt, bm), lambda g, i, k: (k, i)), pl.BlockSpec((None, bt, n), lambda g, i, k: (g, k, 0))],
        out_specs=pl.BlockSpec((None, bm, n), lambda g, i, k: (g, i, 0)),
        out_shape=jax.ShapeDtypeStruct((n_g, m, n), BF16),
        scratch_shapes=[pltpu.VMEM((bm, n), F32)],
        compiler_params=_params(("parallel", "parallel", "arbitrary")))(a, b)


def _weight_grad_beside_prenorm_bwd(a, b, dhx, xt, dxo, nw, mod, below, n_lat, name):
    t, m = a.shape
    n_g, _, n = b.shape
    d = xt.shape[1]
    bt = _mm_rows(t)
    nt = t // bt
    rows = t // (n_g * nt)
    n_piece = 4 if rows % 32 == 0 and m % 4 == 0 else 1
    rows_p, m_p = rows // n_piece, m // n_piece
    assert rows * n_g * nt == t and rows_p % 8 == 0

    def body(a_ref, b_ref, dh_ref, x_ref, dxo_ref, nw_ref, mod_ref, m_ref, modb_ref,
             o_ref, dx_ref, acc_ref, dm_ref, gacc_ref, acc):
        g, k = pl.program_id(0), pl.program_id(1)
        step = g * nt + k

        @pl.when(step == 0)
        def _():
            acc_ref[...] = jnp.zeros_like(acc_ref)
            gacc_ref[...] = jnp.zeros_like(gacc_ref)

        @pl.when(k == 0)
        def _():
            acc[...] = jnp.zeros_like(acc)

        mv, mb, nw_v = mod_ref[...], modb_ref[...], nw_ref[...]
        for p in range(n_piece):
            rs = pl.ds(p * rows_p, rows_p)
            rowi = step * rows + p * rows_p + lax.broadcasted_iota(jnp.int32, (rows_p, 1), 0)
            ctx = rowi >= n_lat
            w_lat = jnp.where(ctx, 0.0, 1.0)
            w_ctx = 1.0 - w_lat
            scale1 = 1.0 + jnp.where(ctx, mv[4:5], mv[1:2])
            x = x_ref[rs, :]
            r = lax.rsqrt(jnp.mean(x * x, axis=-1, keepdims=True) + EPS)
            xn = x * r
            dh = dh_ref[rs, :]
            dsc = dh * (xn * nw_v)
            acc_ref[0:1, :] += jnp.sum(dh * w_lat, axis=0, keepdims=True)
            acc_ref[1:2, :] += jnp.sum(dsc * w_lat, axis=0, keepdims=True)
            acc_ref[3:4, :] += jnp.sum(dh * w_ctx, axis=0, keepdims=True)
            acc_ref[4:5, :] += jnp.sum(dsc * w_ctx, axis=0, keepdims=True)
            acc_ref[6:7, :] += jnp.sum(dh * scale1 * xn, axis=0, keepdims=True)
            dxn = dh * (nw_v * scale1)
            dx = dxo_ref[rs, :] + r * (dxn - xn * jnp.mean(dxn * xn, axis=-1, keepdims=True))
            dx_ref[rs, :] = dx
            dm_ref[rs, :] = (dx * jnp.where(ctx, mb[5:6], mb[2:3])).astype(BF16)
            dg = dx * m_ref[rs, :]
            gacc_ref[2:3, :] += jnp.sum(dg * w_lat, axis=0, keepdims=True)
            gacc_ref[5:6, :] += jnp.sum(dg * w_ctx, axis=0, keepdims=True)

            ms_ = pl.ds(p * m_p, m_p)
            acc[ms_, :] += _dot(a_ref[:, ms_], b_ref[...], TN)

        @pl.when(k == nt - 1)
        def _():
            o_ref[...] = acc[...].astype(o_ref.dtype)

    side = pl.BlockSpec((rows, d), lambda g, k: (g * nt + k, 0))
    acc8 = _full((8, d))
    return pl.pallas_call(
        body, name=name, grid=(n_g, nt),
        in_specs=[pl.BlockSpec((bt, m), lambda g, k: (k, 0)), pl.BlockSpec((None, bt, n), lambda g, k: (g, k, 0)),
                  side, side, side, _full((1, d)), acc8, side, acc8],
        out_specs=[pl.BlockSpec((None, m, n), lambda g, k: (g, 0, 0)), side, acc8, side, acc8],
        out_shape=[jax.ShapeDtypeStruct((n_g, m, n), BF16), jax.ShapeDtypeStruct((t, d), F32),
                   jax.ShapeDtypeStruct((8, d), F32), jax.ShapeDtypeStruct((t, d), BF16),
                   jax.ShapeDtypeStruct((8, d), F32)],
        scratch_shapes=[pltpu.VMEM((m, n), F32)],
        compiler_params=_params(("arbitrary", "arbitrary")))(a, b, dhx, xt, dxo, nw, mod, *below)


def _mix_bwd_a(dycat, u, o, conv_w, cnw, gnw, n_heads, nx, ncc, name):
    _, t, s = u.shape
    nc = nx + ncc

    def body(dy_ref, h_ref, b_ref, c_ref, z_ref, rz_ref, hp_ref, hn_ref, cp_ref, cn_ref, o_ref, w_ref,
             cnw_ref, gnw_ref, g_ref, dz_ref, db_ref, drz_ref, do_ref, acc_ref):
        i = pl.program_id(0)

        @pl.when(i == 0)
        def _():
            acc_ref[...] = jnp.zeros_like(acc_ref)

        has_prev, has_next = _neighbours(i, nx, nc)
        a = c_ref[...] * h_ref[...]
        am, ap = _shifted(a, cp_ref[7:8] * hp_ref[7:8], cn_ref[0:1] * hn_ref[0:1], has_prev, has_next)
        w = w_ref[...]
        y0 = w[0:1] * am + w[1:2] * a + w[2:3] * ap
        bb = b_ref[...]
        yb = bb * y0
        r = lax.rsqrt(jnp.mean(yb * yb, axis=-1, keepdims=True) + EPS)
        ynn = yb * r
        z = z_ref[...]
        dyc = dy_ref[:, pl.ds(0, s)]
        cw = cnw_ref[...]
        sz, dsz = _silu_and_slope(z)
        dz_ref[...] = (dyc * (ynn * cw) * dsz).astype(BF16)
        dyn = dyc * sz
        acc_ref[0:1, :] += jnp.sum(dyn * ynn, axis=0, keepdims=True)
        dynn = dyn * cw
        dyb = r * (dynn - ynn * jnp.mean(dynn * ynn, axis=-1, keepdims=True))
        db_ref[...] = (dyb * y0).astype(BF16)
        g_ref[...] = dyb * bb
        for h in range(n_heads):
            sl = pl.ds(h * HEAD_DIM, HEAD_DIM)
            ov = o_ref[:, sl]
            mu = jnp.mean(ov, axis=-1, keepdims=True)
            var = jnp.mean(jnp.square(ov - mu), axis=-1, keepdims=True)
            rs = lax.rsqrt(var + EPS)
            on = (ov - mu) * rs
            dyr = dy_ref[:, pl.ds(s + h * HEAD_DIM, HEAD_DIM)]
            rz = rz_ref[:, sl]
            gw = gnw_ref[:, sl]
            srz, dsrz = _silu_and_slope(rz)
            drz_ref[:, sl] = (dyr * (on * gw) * dsrz).astype(BF16)
            dyg = dyr * srz
            acc_ref[1:2, sl] += jnp.sum(dyg * on, axis=0, keepdims=True)
            don = dyg * gw
            do = rs * (don - jnp.mean(don, axis=-1, keepdims=True)
                       - on * jnp.mean(don * on, axis=-1, keepdims=True))
            do_ref[:, sl] = do.astype(BF16)

    def seg(g):
        return pl.BlockSpec((None, MIX_ROWS, s), lambda i: (g, i, 0))

    prev, nxt = _halo_specs(s, t // 8)
    row = pl.BlockSpec((MIX_ROWS, s), lambda i: (i, 0))
    return pl.pallas_call(
        body, name=name, grid=(nc // MIX_CHUNKS,),
        in_specs=[pl.BlockSpec((MIX_ROWS, 2 * s), lambda i: (i, 0)), seg(0), seg(1), seg(2), seg(3), seg(4),
                  prev(0), nxt(0), prev(2), nxt(2), row, _full((3, s)), _full((1, s)), _full((1, s))],
        out_specs=[row, row, row, row, row, _full((8, s))],
        out_shape=[jax.ShapeDtypeStruct((t, s), F32)] + [jax.ShapeDtypeStruct((t, s), BF16)] * 4
        + [jax.ShapeDtypeStruct((8, s), F32)],
        compiler_params=_params(("arbitrary",)),
    )(dycat, u, u, u, u, u, u, u, u, u, o, conv_w, cnw, gnw)


def _grad_state_sweep(qkv, do, tabs, n_heads, nx, ncc, name):
    return _pair_sweep((qkv, 0), (do, None), tabs["qf"], tabs["qb"], tabs["cdf"], tabs["cdb"], n_heads, nx, ncc, True, name)


def _mix_bwd_b(u, g, dz, db, drz, qkv, do, sf, sb, gf, gb, tabs, cos, sa, sb_tab, conv_w,
               n_heads, nx, ncc, name):
    _, t, s = u.shape
    nc = nx + ncc
    c = CHUNK
    k_scale = HEAD_DIM ** -0.5

    def body(h_ref, c_ref, g_ref, gp_ref, gn_ref, dz_ref, db_ref, drz_ref, q_ref, k_ref, v_ref, do_ref,
             sf_ref, sb_ref, gf_ref, gb_ref, dc_t, dlf_t, dlb_t, qft, kft, qbt, kbt, cdf, cdb, lg_ref,
             cos_ref, sa_ref, sb_ref2, w_ref, du_ref, dw_ref, dlg_ref):
        i = pl.program_id(0)

        @pl.when(i == 0)
        def _():
            dw_ref[...] = jnp.zeros_like(dw_ref)
            dlg_ref[...] = jnp.zeros_like(dlg_ref)

        has_prev, has_next = _neighbours(i, nx, nc)
        gv = g_ref[...]
        gm, gp = _shifted(gv, gp_ref[7:8], gn_ref[0:1], has_prev, has_next)
        w = w_ref[...]
        da = w[0:1] * gp + w[1:2] * gv + w[2:3] * gm
        hh, cc = h_ref[...], c_ref[...]
        du_ref[0] = (da * cc).astype(BF16)
        du_ref[2] = (da * hh).astype(BF16)
        a = cc * hh
        dw_ref[0:1, :] += jnp.sum(a * gp, axis=0, keepdims=True)
        dw_ref[1:2, :] += jnp.sum(a * gv, axis=0, keepdims=True)
        dw_ref[2:3, :] += jnp.sum(a * gm, axis=0, keepdims=True)
        du_ref[1] = db_ref[...]
        du_ref[3] = dz_ref[...]
        du_ref[7] = drz_ref[...]

        pos = lax.broadcasted_iota(jnp.int32, (c, HEAD_DIM), 0).astype(F32)
        row8 = lax.broadcasted_iota(jnp.int32, (8, HEAD_DIM), 0)
        lane8 = lax.broadcasted_iota(jnp.int32, (8, HEAD_DIM), 1)
        dlg = jnp.zeros((8, HEAD_DIM), F32)
        for sub, h in [(sub, h) for sub in range(MIX_CHUNKS) for h in range(n_heads)]:
            rows = pl.ds(sub * c, c)
            co, ra, rb = cos_ref[rows, :], sa_ref[rows, :], sb_ref2[rows, :]
            sl = pl.ds(h * HEAD_DIM, HEAD_DIM)
            q, k, v, do = q_ref[rows, sl], k_ref[rows, sl], v_ref[rows, sl], do_ref[rows, sl]
            qf, kf, dof = q.astype(F32), k.astype(F32), do.astype(F32)
            s_f, s_b, g_f, g_b = sf_ref[sub, h], sb_ref[sub, h], gf_ref[sub, h], gb_ref[sub, h]
            p = _dot(q, k, NT)
            pd = _dot(do, v, NT)
            pdd = (pd * dc_t[h]).astype(BF16)
            dq = _dot(pdd, k, NN)
            dk = _dot(pdd, q, TN)
            dv = _dot((p * dc_t[h]).astype(BF16), do, TN)
            dq_f = _dot((dof * qft[h]).astype(BF16), s_f, NT)
            dq_b = _dot((dof * qbt[h]).astype(BF16), s_b, NT)
            dk_f = _dot(v, g_f, NT) * kft[h]
            dk_b = _dot(v, g_b, NT) * kbt[h]
            dv += _dot((kf * kft[h]).astype(BF16), g_f, NN) + _dot((kf * kbt[h]).astype(BF16), g_b, NN)
            ppd = p * pd
            cd_f, cd_b = cdf[h][0:1, :], cdb[h][0:1, :]
            t_f = _sum_all(dlf_t[h] * ppd + (pos + 1.0) * qf * dq_f + (c - 1.0 - pos) * kf * dk_f
                           + float(c) * (cd_f * (g_f.astype(F32) * s_f.astype(F32))))
            t_b = _sum_all(dlb_t[h] * ppd + (c - pos) * qf * dq_b + pos * kf * dk_b
                           + float(c) * (cd_b * (g_b.astype(F32) * s_b.astype(F32))))
            dlg += jnp.where((row8 == 0) & (lane8 == h), t_f, 0.0) + jnp.where((row8 == 1) & (lane8 == h), t_b, 0.0)
            du_ref[4, rows, sl] = _rope_bwd(dq + dq_f + dq_b, co, ra, rb).astype(BF16)
            du_ref[5, rows, sl] = (_rope_bwd(dk + dk_f + dk_b, co, ra, rb) * k_scale).astype(BF16)
            du_ref[6, rows, sl] = dv.astype(BF16)
        dlg_ref[...] += dlg

        @pl.when(i == nc // MIX_CHUNKS - 1)
        def _():
            dlg_ref[...] = dlg_ref[...] * lg_ref[...]

    def seg(gi):
        return pl.BlockSpec((None, MIX_ROWS, s), lambda i: (gi, i, 0))

    per = MIX_ROWS // 8
    n8 = t // 8
    row = pl.BlockSpec((MIX_ROWS, s), lambda i: (i, 0))
    st = pl.BlockSpec((MIX_CHUNKS, n_heads, HEAD_DIM, HEAD_DIM), lambda i: (i, 0, 0, 0))
    tab = pl.BlockSpec((MIX_ROWS, HEAD_DIM), lambda i: (i, 0))
    hc = _full((n_heads, c, HEAD_DIM))
    cc_ = _full((n_heads, c, c))
    h8 = _full((n_heads, 8, HEAD_DIM))
    return pl.pallas_call(
        body, name=name, grid=(nc // MIX_CHUNKS,),
        in_specs=[seg(0), seg(2), row,
                  pl.BlockSpec((8, s), lambda i: (jnp.maximum(i * per - 1, 0), 0)),
                  pl.BlockSpec((8, s), lambda i: (jnp.minimum((i + 1) * per, n8 - 1), 0)),
                  row, row, row, seg(0), seg(1), seg(2), row, st, st, st, st, cc_, cc_, cc_, hc, hc, hc, hc, h8, h8,
                  _full((8, HEAD_DIM)), tab, tab, tab, _full((3, s))],
        out_specs=[pl.BlockSpec((8, MIX_ROWS, s), lambda i: (0, i, 0)), _full((8, s)), _full((8, HEAD_DIM))],
        out_shape=[jax.ShapeDtypeStruct((8, t, s), BF16), jax.ShapeDtypeStruct((8, s), F32),
                   jax.ShapeDtypeStruct((8, HEAD_DIM), F32)],
        compiler_params=_params(("arbitrary",)),
    )(u, u, g, g, g, dz, db, drz, qkv, qkv, qkv, do, sf, sb, gf, gb, tabs["dc"], tabs["dlf"], tabs["dlb"],
      tabs["qf"], tabs["kf"], tabs["qb"], tabs["kb"], tabs["cdf"], tabs["cdb"], tabs["lg"], cos, sa, sb_tab, conv_w)


def _in_proj_bwd(du, wgs, tm, gs, name, after=()):
    n_seg, t, s = du.shape
    d = wgs[0].shape[1]
    n_w = len(wgs)
    widths = [w.shape[2] for w in wgs]
    assert sum(widths) == s

    def body(a_ref, *rest):
        w_refs, o_ref = rest[:n_w], rest[-1]
        g = pl.program_id(1)
        part = None
        for j in range(gs):
            col = 0
            for w_ref, width in zip(w_refs, widths):
                term = _dot(a_ref[j, :, col:col + width], w_ref[j], NT)
                part = term if part is None else part + term
                col += width

        @pl.when(g == 0)
        def _():
            o_ref[...] = part

        @pl.when(g > 0)
        def _():
            o_ref[...] += part

    return pl.pallas_call(
        body, name=name, grid=(t // tm, n_seg // gs),
        in_specs=[pl.BlockSpec((gs, tm, s), lambda i, g: (g, i, 0))]
        + [pl.BlockSpec((gs, d, width), lambda i, g: (g, 0, 0)) for width in widths]
        + [pl.BlockSpec(memory_space=pl.ANY)] * len(after),
        out_specs=pl.BlockSpec((tm, d), lambda i, g: (i, 0)),
        out_shape=jax.ShapeDtypeStruct((t, d), F32),
        compiler_params=_params(("parallel", "arbitrary")))(du, *wgs, *after)


def _prenorm_bwd_first(dhx, xt, dxo, nw, mod, n_lat, name):
    t, d = xt.shape
    nxb = n_lat // ROW_TILE

    def body(dh_ref, x_ref, dxo_ref, nw_ref, mod_ref, dx_ref, acc_ref):
        i = pl.program_id(0)

        @pl.when(i == 0)
        def _():
            acc_ref[...] = jnp.zeros_like(acc_ref)

        ctx = i >= nxb
        m = mod_ref[...]
        scale1 = 1.0 + jnp.where(ctx, m[4:5], m[1:2])
        x = x_ref[...]
        nw_v = nw_ref[...]
        r = lax.rsqrt(jnp.mean(x * x, axis=-1, keepdims=True) + EPS)
        xn = x * r
        dh = dh_ref[...]
        dshift = jnp.sum(dh, axis=0, keepdims=True)
        dscale = jnp.sum(dh * (xn * nw_v), axis=0, keepdims=True)
        acc_ref[6:7, :] += jnp.sum(dh * scale1 * xn, axis=0, keepdims=True)
        dxn = dh * (nw_v * scale1)
        dx = dxo_ref[...] + r * (dxn - xn * jnp.mean(dxn * xn, axis=-1, keepdims=True))

        @pl.when(i < nxb)
        def _():
            acc_ref[0:1, :] += dshift
            acc_ref[1:2, :] += dscale
            dx_ref[...] = dx

        @pl.when(i >= nxb)
        def _():
            acc_ref[3:4, :] += dshift
            acc_ref[4:5, :] += dscale

    row = pl.BlockSpec((ROW_TILE, d), lambda i: (i, 0))
    acc = _full((8, d))
    return pl.pallas_call(body, name=name, grid=(t // ROW_TILE,),
                          in_specs=[row, row, row, _full((1, d)), acc],
                          out_specs=[pl.BlockSpec((ROW_TILE, d), lambda i: (jnp.minimum(i, nxb - 1), 0)), acc],
                          out_shape=[jax.ShapeDtypeStruct((n_lat, d), F32), jax.ShapeDtypeStruct((8, d), F32)],
                          compiler_params=_params(("arbitrary",)))(dhx, xt, dxo, nw, mod)


def _adamw(g, w, m, v):
    m = ADAM_B1 * m + (1.0 - ADAM_B1) * g
    v = ADAM_B2 * v + (1.0 - ADAM_B2) * jnp.square(g)
    m_hat = m / (1.0 - ADAM_B1 ** ADAM_STEP)
    v_hat = v / (1.0 - ADAM_B2 ** ADAM_STEP)
    delta = -ADAM_LR * (m_hat / (jnp.sqrt(v_hat) + ADAM_EPS) + ADAM_WD * w)
    return delta, m, v


def _sum_adamw(parts, w, m, v, name, row0=0, into=None):
    n_p, r, n = parts.shape
    r_all = w.shape[0]
    part_block_bytes = 4 * 1024 * 1024
    br = 8
    for cand in (512, 256, 128, 64, 32, 16):
        if r % cand == 0 and row0 % cand == 0 and n_p * cand * n * parts.dtype.itemsize <= part_block_bytes:
            br = cand
            break
    blk0 = row0 // br

    def body(p_ref, w_ref, m_ref, v_ref, *rest):
        g_out, d_out, m_out, v_out = rest[-4:]
        g = p_ref[0].astype(F32)
        for j in range(1, n_p):
            g = g + p_ref[j].astype(F32)
        g_out[...] = g
        d_out[...], m_out[...], v_out[...] = _adamw(g, w_ref[...], m_ref[...], v_ref[...])

    row = pl.BlockSpec((br, n), lambda i: (i + blk0, 0))
    kept = [] if into is None else list(into)
    return pl.pallas_call(body, name=name, grid=(r // br,),
                          in_specs=[pl.BlockSpec((n_p, br, n), lambda i: (0, i, 0)), row, row, row]
                          + [pl.BlockSpec(memory_space=pl.ANY)] * len(kept),
                          out_specs=[row] * 4, out_shape=[jax.ShapeDtypeStruct((r_all, n), F32)] * 4,
                          input_output_aliases={4 + j: j for j in range(len(kept))},
                          compiler_params=_params(("parallel",)))(parts, w, m, v, *kept)


def _rope_tables(n_lat, n_ctx):
    f = HEAD_DIM // 4
    rows = n_lat // GRID_W
    inv = ROPE_BASE ** (-jnp.arange(f, dtype=F32) / f)
    ang_r = jnp.arange(rows).astype(F32)[:, None] * inv[None, :]
    ang_c = jnp.arange(GRID_W).astype(F32)[:, None] * inv[None, :]

    def by_row(a):
        return jnp.broadcast_to(a[:, None, :], (rows, GRID_W, f)).reshape(n_lat, f)

    def by_col(a):
        return jnp.broadcast_to(a[None, :, :], (rows, GRID_W, f)).reshape(n_lat, f)

    cr, sr, cc, sc = by_row(jnp.cos(ang_r)), by_row(jnp.sin(ang_r)), by_col(jnp.cos(ang_c)), by_col(jnp.sin(ang_c))
    zero = jnp.zeros_like(cr)
    cos = jnp.concatenate([cr, cr, cc, cc], axis=-1)
    sa = jnp.concatenate([-sr, zero, -sc, zero], axis=-1)
    sb = jnp.concatenate([zero, sr, zero, sc], axis=-1)
    pad = jnp.zeros((n_ctx, HEAD_DIM), F32)
    return (jnp.concatenate([cos, pad + 1.0], axis=0), jnp.concatenate([sa, pad], axis=0),
            jnp.concatenate([sb, pad], axis=0))


def _pad_rows(a, rows):
    return jnp.pad(a, [(0, rows - a.shape[0])] + [(0, 0)] * (a.ndim - 1))


def _pad_cols(a, cols):
    return jnp.pad(a, [(0, 0), (0, cols - a.shape[1])])


def kernel(x, c, ctx, c_ctx, norm_w, w_mod, b_mod, w_in, conv_w, conv_norm_w, ret_norm_w, ret_decay_f, ret_decay_b, w_out, final_norm_w, loss_target, m_c_ctx, m_norm_w, m_w_mod, m_b_mod, m_w_in, m_conv_w, m_conv_norm_w, m_ret_norm_w, m_ret_decay_f, m_ret_decay_b, m_w_out, m_final_norm_w, v_c_ctx, v_norm_w, v_w_mod, v_b_mod, v_w_in, v_conv_w, v_conv_norm_w, v_ret_norm_w, v_ret_decay_f, v_ret_decay_b, v_w_out, v_final_norm_w):
    depth = norm_w.shape[0]
    n_lat, d = x.shape[1], x.shape[2]
    n_ctx = ctx.shape[1]
    s = d // 2
    n_heads = ret_decay_f.shape[1]
    nx, ncc = n_lat // CHUNK, n_ctx // CHUNK
    n_mod = w_mod.shape[2]
    n_cw = conv_w.shape[2]
    r_out = w_out.shape[1]
    assert s == n_heads * HEAD_DIM and w_in.shape[2] == s and N_DEV * r_out == d
    assert n_lat % ROW_TILE == 0 and n_ctx % ROW_TILE == 0 and 3 * depth * n_cw <= d and d >= 3 * n_mod // 3
    me = 4 * lax.axis_index("x") + 2 * lax.axis_index("y") + lax.axis_index("c")

    w_in_bf = [w_in[l].astype(BF16) for l in range(depth)]
    w_out_bf = [w_out[l].astype(BF16) for l in range(depth)]

    first = jnp.concatenate([c.reshape(1, d), _pad_cols(conv_w.reshape(1, -1), d), jnp.zeros((6, d), F32)], axis=0)
    (first_g,) = _all_gather([first], "gather_cond", True)
    first_g = first_g.reshape(N_DEV, 8, d)
    c_all = first_g[:, 0, :]
    conv_full = first_g[:, 1, :3 * depth * n_cw].reshape(N_DEV, depth, 3, n_cw)
    conv_full = conv_full.transpose(1, 2, 0, 3).reshape(depth, 3, N_DEV * n_cw)
    c9 = jnp.concatenate([c_all, c_ctx.reshape(1, d), jnp.zeros((7, d), F32)], axis=0)

    b_sh = lax.dynamic_slice(b_mod, (0, me * n_mod), (depth, n_mod))
    mod_sh = jnp.concatenate([_mod_rows(c9, w_mod[l], b_sh[l:l + 1], f"mod_rows_l{l}") for l in range(depth)], axis=0)
    (mod_g,) = _all_gather([mod_sh], "gather_mod", True)
    mod_g = mod_g.reshape(N_DEV, depth, 16, n_mod)
    mods = []
    for l in range(depth):
        mine = lax.dynamic_index_in_dim(mod_g[:, l], me, axis=1, keepdims=False).reshape(3, d)
        cx = mod_g[:, l, 8, :].reshape(3, d)
        mods.append(jnp.concatenate([mine, cx, jnp.zeros((2, d), F32)], axis=0))

    halves = [w_in_bf[0][:, :s // 2], w_in_bf[0][:, s // 2:]]
    near, order = [], [mod_g]
    for j, part in enumerate(halves):
        near.append(_push_start([part], [_landing(part, me)], "near", f"w_in0_start_{j}", after=order))
        order = near[-1][4:]
    w_in_g = [None] * depth
    w_out_g = [None] * depth
    pending, tokens = [], []

    cos, sa, sb_tab = _rope_tables(n_lat, n_ctx)
    t_all = n_lat + n_ctx

    saved = []
    xt = None
    for l in range(depth):
        tiles = _tiles(l, t_all, d)
        names = ["dc", "dlf", "dlb", "qf", "kf", "qb", "kb", "cdf", "cdb", "lg"]
        dec = jnp.stack([ret_decay_f[l], ret_decay_b[l]], axis=0)
        tabs = dict(zip(names, _decay_tables(dec, n_heads, f"decay_tables_l{l}")))
        if l == 0:
            hx, xt = _prenorm_first(x[0], ctx[0], norm_w[0:1], mods[0], "prenorm_l0", after=order)
            gathered, out, after = [], None, hx
            for j in range(2):
                (landed,) = _push_wait(*near[j][:4], "near", after, f"w_in0_wait_{j}")
                relay = _push_start([], [landed], "relay", f"w_in0_relay_start_{j}")
                (landed,) = _push_wait(*relay[:4], "relay", relay[4], f"w_in0_relay_wait_{j}")
                gathered.append(landed)
                tokens = [landed]
                if j == 1:
                    for k in range(depth):
                        srcs = [w_out_bf[k]] + ([w_in_bf[k]] if k > 0 else [])
                        started = _push_start(srcs, [_landing(a, me) for a in srcs], "gather", f"weights_start_l{k}",
                                              after=tokens[-1:])
                        pending.append(started[:4])
                        tokens.append(started[4])
                out = _in_proj(hx, landed, cos, sa, sb_tab, s, j, tiles["in_tm"], f"in_proj_l0_{j}",
                               after=tokens[1:], into=out)
                after = out[0]
            u, qkv = out
            w_in_g[0] = gathered
        else:
            landed = _push_wait(*pending[l], "gather", xt, f"weights_wait_l{l}")
            w_out_g[l], w_in_g[l] = landed[0].reshape(d, d), [landed[1]]
            hx = _prenorm(xt, norm_w[l:l + 1], mods[l], n_lat, f"prenorm_l{l}")
            u, qkv = _in_proj(hx, w_in_g[l][0], cos, sa, sb_tab, s, 0, tiles["in_tm"], f"in_proj_l{l}")
        sf, sb = _state_sweep(qkv, tabs, n_heads, nx, ncc, f"state_sweep_l{l}")
        ycat, o = _mix_fwd(u, qkv, sf, sb, tabs, conv_full[l], conv_norm_w[l:l + 1], ret_norm_w[l:l + 1],
                           n_heads, nx, ncc, f"mix_fwd_l{l}")
        if l == 0:
            (landed,) = _push_wait(*pending[0], "gather", ycat, "weights_wait_l0")
            w_out_g[0] = landed.reshape(d, d)
        m_res = x_new = None
        if l < depth - 1:
            m_res, x_new = _out_proj(ycat, w_out_g[l], xt, mods[l], n_lat, tiles["out_tm"], tiles["out_tn"],
                                     f"out_proj_l{l}")
        else:
            dxt, dm, loss_blk, dfnw, gate_acc = _out_proj_loss(ycat, w_out_g[l], xt, mods[l], loss_target[0],
                                                               final_norm_w.reshape(1, d), n_lat, f"out_proj_loss_l{l}")
        saved.append(dict(tabs=tabs, xt=xt, hx=hx, u=u, qkv=qkv, sf=sf, sb=sb, ycat=ycat, o=o, m=m_res, tiles=tiles))
        xt = x_new

    loss = lax.psum(loss_blk[0, 0], MESH_AXES)

    dmod_x, dmod_c, dnw, dcnw, dgnw, dconv, ddec, dwin, dwout = [], [], [], [], [], [], [], [], []
    started_token = ()
    for l in reversed(range(depth)):
        sv = saved[l]
        tiles = sv["tiles"]
        dycat = _matmul_nt(dm, w_out_g[l], tiles["ob_tn"], f"out_proj_bwd_l{l}", after=started_token)
        dwout.append(_weight_grad(sv["ycat"], dm.reshape(1, *dm.shape), tiles["wo_bm"], _mm_rows(t_all),
                                  f"w_out_grad_l{l}")[0])
        g, dz, db, drz, do, norm_acc = _mix_bwd_a(dycat, sv["u"], sv["o"], conv_full[l], conv_norm_w[l:l + 1],
                                                   ret_norm_w[l:l + 1], n_heads, nx, ncc, f"mix_bwd_a_l{l}")
        gf, gb = _grad_state_sweep(sv["qkv"], do, sv["tabs"], n_heads, nx, ncc, f"grad_state_sweep_l{l}")
        du, conv_acc, dlg = _mix_bwd_b(sv["u"], g, dz, db, drz, sv["qkv"], do, sv["sf"], sv["sb"],
                                       gf, gb, sv["tabs"], cos, sa, sb_tab, conv_full[l], n_heads, nx, ncc,
                                       f"mix_bwd_b_l{l}")
        gate_acc_l = gate_acc
        if l > 0:
            dhx = _in_proj_bwd(du, w_in_g[l], tiles["bwd_tm"], tiles["bwd_gs"], f"in_proj_bwd_l{l}")
            below = (saved[l - 1]["m"], mods[l - 1])
            dwin_l, dxt, pre_acc, dm, gate_acc = _weight_grad_beside_prenorm_bwd(
                sv["hx"], du, dhx, sv["xt"], dxt, norm_w[l:l + 1], mods[l], below, n_lat, f"w_in_grad_l{l}")
        else:
            dwin_l = _weight_grad(sv["hx"], du, tiles["wg_bm"], tiles["wg_bt"], f"w_in_grad_l{l}")
        srcs = [dwin_l, dwout[-1].reshape(N_DEV, r_out, d)]
        lands = [_landing(lax.dynamic_index_in_dim(a, me, axis=0, keepdims=False), me) for a in srcs]
        started = _push_start(srcs, lands, "scatter", f"grads_start_l{l}")
        dwin.append(started[:4])
        started_token = started[4:]
        if l == 0:
            dhx = _in_proj_bwd(du, w_in_g[l], tiles["bwd_tm"], tiles["bwd_gs"], f"in_proj_bwd_l{l}", after=started[4:])
            dxt, pre_acc = _prenorm_bwd_first(dhx, sv["xt"], dxt, norm_w[l:l + 1], mods[l], n_lat, f"prenorm_bwd_l{l}")
        dmod_x.append(jnp.concatenate([pre_acc[0], pre_acc[1], gate_acc_l[2]]))
        dmod_c.append(jnp.concatenate([pre_acc[3], pre_acc[4], gate_acc_l[5]]))
        dnw.append(pre_acc[6])
        dcnw.append(norm_acc[0])
        dgnw.append(norm_acc[1])
        dconv.append(conv_acc[0:3])
        ddec.append(dlg[0:2, :n_heads])
    for lst in (dmod_x, dmod_c, dnw, dcnw, dgnw, dconv, ddec, dwin, dwout):
        lst.reverse()
    grad_x = dxt.reshape(1, n_lat, d)

    rows = []
    for l in range(depth):
        rows += [dmod_x[l], dmod_c[l]]
    (dmod_g,) = _all_gather([_pad_rows(jnp.stack(rows, axis=0), 8)], "gather_dmod", True)
    dmod_g = dmod_g.reshape(N_DEV, 8, 3 * d)
    mine_cols = lax.dynamic_slice(dmod_g, (0, 0, me * n_mod), (N_DEV, 8, n_mod))
    g_wmod, dcc = [], jnp.zeros((d,), F32)
    for l in range(depth):
        gw, dc_part = _mod_grads(mine_cols[:, 2 * l], mine_cols[:, 2 * l + 1], c9, w_mod[l], f"mod_grads_l{l}")
        g_wmod.append(gw)
        dcc = dcc + dc_part[0]

    n_small = 16
    small = jnp.concatenate([
        jnp.stack(dnw, axis=0),
        jnp.concatenate(dcnw).reshape(1, -1),
        jnp.concatenate(dgnw).reshape(1, -1),
        dfnw[0:1],
        dcc.reshape(1, d),
        jnp.stack(dconv, axis=0).reshape(-1, d),
        _pad_cols(jnp.stack(ddec, axis=0).reshape(1, -1), d),
    ], axis=0)
    assert depth * s == d and small.shape[0] <= n_small
    n_rows = small.shape[0]
    (small_g,) = _all_gather([_pad_rows(small, n_small)], "gather_small", True)
    small_g = small_g.reshape(N_DEV, n_small, d)

    def pack_small(nw_, cn_, gn_, fn_, cc_, df_, db_):
        return _pad_rows(jnp.concatenate([
            nw_, cn_.reshape(1, -1), gn_.reshape(1, -1), fn_.reshape(1, d), cc_.reshape(1, d),
            jnp.zeros((n_rows - depth - 5, d), F32),
            _pad_cols(jnp.stack([df_, db_], axis=1).reshape(1, -1), d)], axis=0), n_small)

    w_s = pack_small(norm_w, conv_norm_w, ret_norm_w, final_norm_w, c_ctx, ret_decay_f, ret_decay_b)
    m_s = pack_small(m_norm_w, m_conv_norm_w, m_ret_norm_w, m_final_norm_w, m_c_ctx, m_ret_decay_f, m_ret_decay_b)
    v_s = pack_small(v_norm_w, v_conv_norm_w, v_ret_norm_w, v_final_norm_w, v_c_ctx, v_ret_decay_f, v_ret_decay_b)
    small_out = _sum_adamw(small_g, w_s, m_s, v_s, "adamw_small")

    def unpack_small(a):
        nw_ = a[0:depth]
        cn_ = a[depth].reshape(depth, s)
        gn_ = a[depth + 1].reshape(depth, s)
        fn_ = a[depth + 2]
        cc_ = a[depth + 3]
        dd = a[n_rows - 1, :depth * 2 * n_heads].reshape(depth, 2, n_heads)
        return dict(c_ctx=cc_, norm_w=nw_, conv_norm_w=cn_, ret_norm_w=gn_, ret_decay_f=dd[:, 0], ret_decay_b=dd[:, 1],
                    final_norm_w=fn_)

    res = {}
    for kind, arr in zip(("grad", "delta", "m", "v"), small_out):
        for k_, val in unpack_small(arr).items():
            res[(kind, k_)] = val

    bm_parts = jnp.concatenate([dmod_g[:, 0:2 * depth:2].reshape(N_DEV, depth, 3 * d),
                                dmod_g[:, 1:2 * depth:2].reshape(N_DEV, depth, 3 * d)], axis=0)
    bm_parts = jnp.concatenate([bm_parts, jnp.zeros((2 * N_DEV, 8 - depth, 3 * d), F32)], axis=1)
    pad8 = lambda a: _pad_rows(a, 8)
    bm_out = _sum_adamw(bm_parts, pad8(b_mod), pad8(m_b_mod), pad8(v_b_mod), "adamw_b_mod")
    for kind, arr in zip(("grad", "delta", "m", "v"), bm_out):
        res[(kind, "b_mod")] = arr[:depth]

    conv_rows = small_g[:, depth + 4:depth + 4 + 3 * depth * s // d].reshape(N_DEV, depth * 3, s)
    conv_mine = lax.dynamic_slice(conv_rows, (0, 0, me * n_cw), (N_DEV, depth * 3, n_cw))
    conv_mine = jnp.concatenate([conv_mine, jnp.zeros((N_DEV, 8 - depth * 3, n_cw), F32)], axis=1)
    cw2 = lambda a: _pad_rows(a.reshape(depth * 3, n_cw), 8)
    cw_out = _sum_adamw(conv_mine, cw2(conv_w), cw2(m_conv_w), cw2(v_conv_w), "adamw_conv_w")
    for kind, arr in zip(("grad", "delta", "m", "v"), cw_out):
        res[(kind, "conv_w")] = arr[:depth * 3].reshape(depth, 3, n_cw)

    wm_out = _sum_adamw(jnp.stack(g_wmod, axis=0).reshape(1, depth * d, n_mod), w_mod.reshape(depth * d, n_mod),
                        m_w_mod.reshape(depth * d, n_mod), v_w_mod.reshape(depth * d, n_mod), "adamw_w_mod")
    for kind, arr in zip(("grad", "delta", "m", "v"), wm_out):
        res[(kind, "w_mod")] = arr.reshape(depth, d, n_mod)

    wi_out = wo_out = None
    after = wm_out[0]
    for l in reversed(range(depth)):
        win_parts, wout_parts = _push_wait(*dwin[l], "scatter", after, f"grads_wait_l{l}")
        wi_out = _sum_adamw(win_parts, w_in.reshape(depth * d, s), m_w_in.reshape(depth * d, s),
                            v_w_in.reshape(depth * d, s), f"adamw_w_in_l{l}", row0=l * d, into=wi_out)
        wo_out = _sum_adamw(wout_parts, w_out.reshape(depth * r_out, d), m_w_out.reshape(depth * r_out, d),
                            v_w_out.reshape(depth * r_out, d), f"adamw_w_out_l{l}", row0=l * r_out, into=wo_out)
        after = wo_out[0]
    for kind, arr in zip(("grad", "delta", "m", "v"), wi_out):
        res[(kind, "w_in")] = arr.reshape(depth, d, s)
    for kind, arr in zip(("grad", "delta", "m", "v"), wo_out):
        res[(kind, "w_out")] = arr.reshape(depth, r_out, d)

    order = ["c_ctx", "norm_w", "w_mod", "b_mod", "w_in", "conv_w", "conv_norm_w", "ret_norm_w", "ret_decay_f",
             "ret_decay_b", "w_out", "final_norm_w"]
    outs = [loss, grad_x]
    for kind in ("grad", "delta", "m", "v"):
        outs += [res[(kind, k_)] for k_ in order]
    return tuple(outs)
```

```python
import functools

import jax
import jax.numpy as jnp
from jax import lax
from jax.experimental import pallas as pl
from jax.experimental.pallas import tpu as pltpu

F32 = jnp.float32
BF16 = jnp.bfloat16

EPS = 1e-6
CHUNK = 128
HEAD_DIM = 128
GRID_W = 64
ROPE_BASE = 10000.0
N_DEV = 8
ADAM_LR, ADAM_B1, ADAM_B2, ADAM_EPS, ADAM_WD, ADAM_STEP = 0.001, 0.9, 0.999, 1e-08, 0.01, 10

ROW_TILE = 256
V7X_VMEM_LIMIT = 56 * 1024 * 1024
MESH_AXES = ("x", "y", "c")

NN = ((1,), (0,))
NT = ((1,), (1,))
TN = ((0,), (0,))


def _dot(a, b, dims):
    return lax.dot_general(a, b, (dims, ((), ())), preferred_element_type=F32)


def _params(sem=None):
    if sem is None:
        return pltpu.CompilerParams(vmem_limit_bytes=V7X_VMEM_LIMIT)
    return pltpu.CompilerParams(dimension_semantics=sem, vmem_limit_bytes=V7X_VMEM_LIMIT)


def _silu(z):
    return z * jax.nn.sigmoid(z)


def _dsilu(z):
    s = jax.nn.sigmoid(z)
    return s * (1.0 + z * (1.0 - s))


def _silu_and_slope(z):
    s = jax.nn.sigmoid(z)
    return z * s, s * (1.0 + z * (1.0 - s))


def _sum_all(a):
    return jnp.sum(jnp.sum(a, axis=1, keepdims=True), axis=0, keepdims=True)


def _mm_rows(t):
    return 768 if t % 768 == 0 else ROW_TILE


def _rows_or(t, rows):
    return rows if t % rows == 0 else _mm_rows(t)


def _tiles(layer, t, d):
    return dict(in_tm=_rows_or(t, 1408), bwd_tm=_mm_rows(t), bwd_gs=2, wg_bm=d, wg_bt=_mm_rows(t),
                out_tm=_mm_rows(t), out_tn=min(d, 1024), wo_bm=d, ob_tn=d)


def _full(shape):
    n = len(shape)
    return pl.BlockSpec(shape, lambda *_: (0,) * n)


def _peers(x, y, c):
    return [(x, y, 1 - c), (1 - x, y, c), (x, 1 - y, c), (1 - x, 1 - y, c),
            (1 - x, y, 1 - c), (x, 1 - y, 1 - c), (1 - x, 1 - y, 1 - c)]


def _lin(p):
    return 4 * p[0] + 2 * p[1] + p[2]


def _all_gather(arrays, name, in_vmem):
    n_arr = len(arrays)
    space = pltpu.VMEM if in_vmem else pl.ANY

    def body(*refs):
        ins, outs = refs[:n_arr], refs[n_arr:2 * n_arr]
        send_sems, recv_sems, local_sems = refs[2 * n_arr:]
        x, y, c = lax.axis_index("x"), lax.axis_index("y"), lax.axis_index("c")
        me, sibling = (x, y, c), (x, y, 1 - c)
        chips = [(1 - x, y), (x, 1 - y), (1 - x, 1 - y)]
        every = []
        locals_ = []
        for a in range(n_arr):
            m_per = ins[a].shape[0]
            out_ref = outs[a]

            def rows(p, out_ref=out_ref, m_per=m_per):
                return out_ref.at[pl.ds(_lin(p) * m_per, m_per), :]

            def copy(k, block, to, src=None, a=a, rows=rows):
                return pltpu.make_async_remote_copy(
                    src_ref=rows(block) if src is None else src, dst_ref=rows(block),
                    send_sem=send_sems.at[a, k], recv_sem=recv_sems.at[a, k],
                    device_id=to, device_id_type=pl.DeviceIdType.MESH)

            mine = pltpu.make_async_copy(ins[a], rows(me), local_sems.at[a])
            mine.start()
            locals_.append(mine)
            first = [copy(0, me, sibling, src=ins[a])]
            first += [copy(1 + j, me, (*chip, c), src=ins[a]) for j, chip in enumerate(chips)]
            for cp in first:
                cp.start()
            every.append((copy, first))
        sends = []
        for a in range(n_arr):
            copy, first = every[a]
            passed = [copy(4 + j, (*chip, c), sibling) for j, chip in enumerate(chips)]
            for j, chip in enumerate(chips):
                copy(1 + j, (*chip, c), me).wait_recv()
                passed[j].start()
            sends += first + passed
        for a in range(n_arr):
            copy, _ = every[a]
            copy(0, sibling, me).wait_recv()
            for j, chip in enumerate(chips):
                copy(4 + j, (*chip, 1 - c), me).wait_recv()
        for cp in sends:
            cp.wait_send()
        for mine in locals_:
            mine.wait()

    outs = pl.pallas_call(
        body, name=name,
        out_shape=[jax.ShapeDtypeStruct((N_DEV * a.shape[0], a.shape[1]), a.dtype) for a in arrays],
        in_specs=[pl.BlockSpec(memory_space=space)] * n_arr,
        out_specs=[pl.BlockSpec(memory_space=space)] * n_arr,
        scratch_shapes=[pltpu.SemaphoreType.DMA((n_arr, 7)), pltpu.SemaphoreType.DMA((n_arr, 7)),
                        pltpu.SemaphoreType.DMA((n_arr,))],
        compiler_params=_params(),
    )(*arrays)
    return list(outs)


_HBM = pl.BlockSpec(memory_space=pltpu.HBM)
_SEM = pl.BlockSpec(memory_space=pltpu.SEMAPHORE)
_DATAFLOW = pltpu.SideEffectType.DATAFLOW_SIDE_EFFECTING


PUSH_COPIES = {"scatter": 7, "gather": 7, "near": 4, "relay": 3}


def _push_copies(src_refs, land_refs, send_sems, recv_sems, mode):
    x, y, c = lax.axis_index("x"), lax.axis_index("y"), lax.axis_index("c")
    me, sibling = (x, y, c), (x, y, 1 - c)
    n_k = PUSH_COPIES[mode]
    out, back = [], []
    if mode == "relay":
        for k, chip in enumerate([(1 - x, y), (x, 1 - y), (1 - x, 1 - y)]):
            for a, land in enumerate(land_refs):
                sems = dict(send_sem=send_sems.at[n_k * a + k], recv_sem=recv_sems.at[n_k * a + k],
                            device_id=sibling, device_id_type=pl.DeviceIdType.MESH)
                mine = land.at[_lin((*chip, c))]
                out.append(pltpu.make_async_remote_copy(src_ref=mine, dst_ref=mine, **sems))
                back.append(pltpu.make_async_remote_copy(src_ref=mine, dst_ref=land.at[_lin((*chip, 1 - c))], **sems))
        return out, back
    for k, peer in enumerate(_peers(x, y, c)[:n_k]):
        for a, (src, land) in enumerate(zip(src_refs, land_refs)):
            sems = dict(send_sem=send_sems.at[n_k * a + k], recv_sem=recv_sems.at[n_k * a + k],
                        device_id=peer, device_id_type=pl.DeviceIdType.MESH)
            mine = src.at[_lin(peer)] if mode == "scatter" else src
            out.append(pltpu.make_async_remote_copy(src_ref=mine, dst_ref=land.at[_lin(me)], **sems))
            back.append(pltpu.make_async_remote_copy(src_ref=mine, dst_ref=land.at[_lin(peer)], **sems))
    return out, back


def _push_start(srcs, lands, mode, name, after=()):
    n_src, n = len(srcs), len(lands)
    n_buf = n_src + n
    n_in = n_buf + len(after)
    n_sem = PUSH_COPIES[mode] * n

    def body(*refs):
        send_sems, recv_sems = refs[n_in], refs[n_in + 1]
        out, _ = _push_copies(refs[:n_src], refs[n_src:n_buf], send_sems, recv_sems, mode)
        for cp in out:
            cp.start()
        token = refs[-1]
        token[...] = jnp.zeros_like(token)

    both = list(srcs) + list(lands)
    res = pl.pallas_call(
        body, name=name,
        out_shape=[pltpu.SemaphoreType.DMA((n_sem,)), pltpu.SemaphoreType.DMA((n_sem,))]
        + [pltpu.HBM(a.shape, a.dtype) for a in both] + [jax.ShapeDtypeStruct((8, 128), F32)],
        in_specs=[_HBM] * n_buf + [pl.BlockSpec(memory_space=pl.ANY)] * len(after),
        out_specs=[_SEM, _SEM] + [_HBM] * n_buf + [pl.BlockSpec(memory_space=pltpu.VMEM)],
        input_output_aliases={i: 2 + i for i in range(n_buf)},
        compiler_params=pltpu.CompilerParams(has_side_effects=_DATAFLOW),
    )(*[pltpu.with_memory_space_constraint(a, pltpu.HBM) for a in both], *after)
    return res[0], res[1], list(res[2:2 + n_src]), list(res[2 + n_src:2 + n_buf]), res[-1]


def _push_wait(send_sems, recv_sems, srcs, lands, mode, after, name):
    n_src, n = len(srcs), len(lands)
    n_buf = n_src + n

    def body(*refs):
        out, back = _push_copies(refs[:n_src], refs[n_src:n_buf], refs[n_buf], refs[n_buf + 1], mode)
        for cp in out:
            cp.wait_send()
        for cp in back:
            cp.wait_recv()

    both = list(srcs) + list(lands)
    res = pl.pallas_call(
        body, name=name,
        out_shape=[pltpu.HBM(a.shape, a.dtype) for a in both],
        in_specs=[_HBM] * n_buf + [_SEM, _SEM, pl.BlockSpec(memory_space=pl.ANY)],
        out_specs=[_HBM] * n_buf,
        input_output_aliases={i: i for i in range(n_buf)},
        compiler_params=pltpu.CompilerParams(has_side_effects=_DATAFLOW),
    )(*both, send_sems, recv_sems, after)
    return list(res[n_src:])


def _landing(own, me):
    zone = lax.empty((N_DEV,) + own.shape, own.dtype)
    return lax.dynamic_update_slice(zone, own[None], (me,) + (0,) * own.ndim)


def _mod_rows(c9, w_mod, b_sh, name):
    n = w_mod.shape[1]

    def body(c_ref, w_ref, b_ref, o_ref):
        s9 = _silu(c_ref[...]).astype(BF16)
        o_ref[...] = _dot(s9, w_ref[...].astype(BF16), NN) + b_ref[...]

    return pl.pallas_call(body, name=name, out_shape=jax.ShapeDtypeStruct((16, n), F32),
                          compiler_params=_params())(c9, w_mod, b_sh)


def _mod_grads(dm_rows, dc_rows, c9, w_mod, name):
    d, n = w_mod.shape

    def body(dm_ref, dc_ref, c_ref, w_ref, gw_ref, dc_out):
        dc = dc_ref[...]
        tot = dc[0:1]
        for j in range(1, N_DEV):
            tot = tot + dc[j:j + 1]
        row = lax.broadcasted_iota(jnp.int32, (8, n), 0)
        lower = jnp.where(row == 0, tot, 0.0)
        dmod9 = jnp.concatenate([dm_ref[...], lower], axis=0).astype(BF16)
        c9v = c_ref[...]
        s9 = _silu(c9v).astype(BF16)
        gw_ref[...] = _dot(s9, dmod9, TN)
        ds = _dot(lower.astype(BF16), w_ref[...].astype(BF16), NT)
        dc_out[...] = ds * _dsilu(c9v[8:16])

    return pl.pallas_call(body, name=name,
                          out_shape=[jax.ShapeDtypeStruct((d, n), F32), jax.ShapeDtypeStruct((8, d), F32)],
                          compiler_params=_params())(dm_rows, dc_rows, c9, w_mod)


def _decay_tables(dec, n_heads, name):
    c = CHUNK

    def body(dec_ref, dc_ref, dlf_ref, dlb_ref, qf_ref, kf_ref, qb_ref, kb_ref, cdf_ref, cdb_ref, lg_ref):
        h = pl.program_id(0)
        d = dec_ref[...]
        lane = lax.broadcasted_iota(jnp.int32, d.shape, 1)
        lg = -jnp.exp(jnp.sum(jnp.where(lane == h, d, 0.0), axis=1, keepdims=True))
        lgf, lgb = lg[0:1], lg[1:2]
        i = lax.broadcasted_iota(jnp.int32, (c, c), 0).astype(F32)
        j = lax.broadcasted_iota(jnp.int32, (c, c), 1).astype(F32)
        diff = i - j
        d_f = jnp.where(diff >= 0, jnp.exp(lgf * jnp.maximum(diff, 0.0)), 0.0)
        d_b = jnp.where(diff <= 0, jnp.exp(lgb * jnp.maximum(-diff, 0.0)), 0.0)
        dc_ref[...] = d_f + d_b
        dlf_ref[...] = diff * d_f
        dlb_ref[...] = -diff * d_b
        pos = lax.broadcasted_iota(jnp.int32, (c, HEAD_DIM), 0).astype(F32)
        qf_ref[...] = jnp.exp(lgf * (pos + 1.0))
        kf_ref[...] = jnp.exp(lgf * (c - 1.0 - pos))
        qb_ref[...] = jnp.exp(lgb * (c - pos))
        kb_ref[...] = jnp.exp(lgb * pos)
        ones = jnp.ones((8, HEAD_DIM), F32)
        cdf_ref[...] = jnp.exp(lgf * float(c)) * ones
        cdb_ref[...] = jnp.exp(lgb * float(c)) * ones

        @pl.when(h == 0)
        def _():
            lg_ref[...] = jnp.zeros_like(lg_ref)

        row8 = lax.broadcasted_iota(jnp.int32, (8, HEAD_DIM), 0)
        lane8 = lax.broadcasted_iota(jnp.int32, (8, HEAD_DIM), 1)
        lg_ref[...] += (jnp.where((row8 == 0) & (lane8 == h), lgf, 0.0)
                        + jnp.where((row8 == 1) & (lane8 == h), lgb, 0.0))

    def per_head(*tail):
        return pl.BlockSpec((None,) + tail, lambda h: (h,) + (0,) * len(tail))

    shapes = [(c, c)] * 3 + [(c, HEAD_DIM)] * 4 + [(8, HEAD_DIM)] * 2
    return pl.pallas_call(
        body, name=name, grid=(n_heads,),
        in_specs=[_full(dec.shape)],
        out_specs=[per_head(*s) for s in shapes] + [_full((8, HEAD_DIM))],
        out_shape=[jax.ShapeDtypeStruct((n_heads,) + s, F32) for s in shapes]
        + [jax.ShapeDtypeStruct((8, HEAD_DIM), F32)],
        compiler_params=_params(("arbitrary",)),
    )(dec)


def _modulate(x, nw, shift, scale):
    r = lax.rsqrt(jnp.mean(x * x, axis=-1, keepdims=True) + EPS)
    return ((x * r) * nw * (1.0 + scale) + shift).astype(BF16)


def _prenorm(xt, nw, mod, n_lat, name):
    t, d = xt.shape
    nxb = n_lat // ROW_TILE

    def body(x_ref, nw_ref, mod_ref, o_ref):
        ctx = pl.program_id(0) >= nxb
        m = mod_ref[...]
        o_ref[...] = _modulate(x_ref[...], nw_ref[...], jnp.where(ctx, m[3:4], m[0:1]), jnp.where(ctx, m[4:5], m[1:2]))

    row = pl.BlockSpec((ROW_TILE, d), lambda i: (i, 0))
    return pl.pallas_call(body, name=name, grid=(t // ROW_TILE,),
                          in_specs=[row, _full((1, d)), _full((8, d))],
                          out_specs=row, out_shape=jax.ShapeDtypeStruct((t, d), BF16),
                          compiler_params=_params(("parallel",)))(xt, nw, mod)


def _prenorm_first(x, ctx, nw, mod, name, after=()):
    n_lat, d = x.shape
    t = n_lat + ctx.shape[0]
    nxb = n_lat // ROW_TILE

    def body(x_ref, c_ref, nw_ref, mod_ref, *rest):
        o_ref, xt_ref = rest[-2:]
        m = mod_ref[...]
        nw_v = nw_ref[...]

        @pl.when(pl.program_id(0) < nxb)
        def _():
            xv = x_ref[...]
            xt_ref[...] = xv
            o_ref[...] = _modulate(xv, nw_v, m[0:1], m[1:2])

        @pl.when(pl.program_id(0) >= nxb)
        def _():
            xv = c_ref[...]
            xt_ref[...] = xv
            o_ref[...] = _modulate(xv, nw_v, m[3:4], m[4:5])

    row = pl.BlockSpec((ROW_TILE, d), lambda i: (i, 0))
    return pl.pallas_call(
        body, name=name, grid=(t // ROW_TILE,),
        in_specs=[pl.BlockSpec((ROW_TILE, d), lambda i: (jnp.minimum(i, nxb - 1), 0)),
                  pl.BlockSpec((ROW_TILE, d), lambda i: (jnp.maximum(i - nxb, 0), 0)), _full((1, d)), _full((8, d))]
        + [pl.BlockSpec(memory_space=pl.ANY)] * len(after),
        out_specs=[row, row], out_shape=[jax.ShapeDtypeStruct((t, d), BF16), jax.ShapeDtypeStruct((t, d), F32)],
        compiler_params=_params(("parallel",)))(x, ctx, nw, mod, *after)


def _rope_fwd(v, cos, sa, sb):
    return v * cos + pltpu.roll(v, 96, 1) * sa + pltpu.roll(v, 32, 1) * sb


def _rope_bwd(g, cos, sa, sb):
    return g * cos + pltpu.roll(g * sa, 32, 1) + pltpu.roll(g * sb, 96, 1)


N_PLAIN = 5


def _in_proj(hx, wg, cos, sa, sb, s, part, tm, name, after=(), into=None):
    t, d = hx.shape
    n_seg, _, n = wg.shape
    nb = t // tm
    k_scale = HEAD_DIM ** -0.5
    kept = [] if into is None else list(into)

    def body(a_ref, w_ref, cos_ref, sa_ref, sb_ref, *rest):
        u_ref, qkv_ref = rest[-2:]
        g = pl.program_id(1)
        acc = _dot(a_ref[...], w_ref[...], NN)

        @pl.when(g < N_PLAIN)
        def _():
            u_ref[...] = acc

        @pl.when(g == N_PLAIN + 2)
        def _():
            qkv_ref[...] = acc.astype(BF16)

        for which, scale in ((N_PLAIN, 1.0), (N_PLAIN + 1, k_scale)):
            @pl.when(g == which)
            def _(scale=scale):
                co, a, b = cos_ref[...], sa_ref[...], sb_ref[...]
                for h in range(n // HEAD_DIM):
                    sl = slice(h * HEAD_DIM, (h + 1) * HEAD_DIM)
                    qkv_ref[:, sl] = (_rope_fwd(acc[:, sl], co, a, b) * scale).astype(BF16)

    def w_seg(g):
        return jnp.where(g < N_PLAIN - 1, g, jnp.where(g == N_PLAIN - 1, n_seg - 1, g - 1))

    def qkv_at(i, g):
        held = (jnp.where(i == 0, 0, 2), jnp.maximum(i - 1, 0))
        return (jnp.where(g < N_PLAIN, held[0], g - N_PLAIN), jnp.where(g < N_PLAIN, held[1], i), part)

    tab = pl.BlockSpec((tm, HEAD_DIM), lambda i, g: (i, 0))
    hbm = pl.BlockSpec(memory_space=pl.ANY)
    return pl.pallas_call(
        body, name=name, grid=(nb, n_seg),
        in_specs=[pl.BlockSpec((tm, d), lambda i, g: (i, 0)), pl.BlockSpec((None, d, n), lambda i, g: (w_seg(g), 0, 0)),
                  tab, tab, tab] + [hbm] * (len(after) + len(kept)),
        out_specs=[pl.BlockSpec((None, tm, n), lambda i, g: (jnp.minimum(g, N_PLAIN - 1), i, part)),
                   pl.BlockSpec((None, tm, n), qkv_at)],
        out_shape=[jax.ShapeDtypeStruct((N_PLAIN, t, s), F32), jax.ShapeDtypeStruct((3, t, s), BF16)],
        input_output_aliases={5 + len(after) + j: j for j in range(len(kept))},
        compiler_params=_params(("arbitrary", "arbitrary")))(hx, wg, cos, sa, sb, *after, *kept)


def _pair_sweep(xs, ys, tab_f, tab_b, cdf, cdb, n_heads, nx, ncc, reverse, name):
    t, s = xs[0].shape[-2:]
    nc = nx + ncc
    c = CHUNK
    n_pair = nc // 2
    assert nx % 2 == 0 and ncc % 2 == 0

    def f_pair(i):
        step = n_pair - 1 - i if reverse else i
        return jnp.where(step < ncc // 2, nx // 2 + step, step - ncc // 2)

    def b_pair(i):
        return i if reverse else n_pair - 1 - i

    f_subs = (1, 0) if reverse else (0, 1)
    b_subs = (0, 1) if reverse else (1, 0)

    def body(xf_ref, yf_ref, xb_ref, yb_ref, tf, tb, cdf_ref, cdb_ref, sf_out, sb_out, sf, sb):
        @pl.when(pl.program_id(0) == 0)
        def _():
            sf[...] = jnp.zeros_like(sf)
            sb[...] = jnp.zeros_like(sb)

        for step in range(2):
            for x_ref, y_ref, tab, cd, out, st, sub in ((xf_ref, yf_ref, tf, cdf_ref, sf_out, sf, f_subs[step]),
                                                        (xb_ref, yb_ref, tb, cdb_ref, sb_out, sb, b_subs[step])):
                rows = pl.ds(sub * c, c)
                for h in range(n_heads):
                    sl = pl.ds(h * HEAD_DIM, HEAD_DIM)
                    out[sub, h] = st[h].astype(BF16)
                    xd = (x_ref[rows, sl].astype(F32) * tab[h]).astype(BF16)
                    st[h] = cd[h][0:1, :] * st[h] + _dot(xd, y_ref[rows, sl], TN)

    def spec(arr, pair):
        lead = arr[1]
        if lead is None:
            return pl.BlockSpec((2 * c, s), lambda i: (pair(i), 0))
        return pl.BlockSpec((None, 2 * c, s), lambda i: (lead, pair(i), 0))

    st_blk = (2, n_heads, HEAD_DIM, HEAD_DIM)
    return pl.pallas_call(
        body, name=name, grid=(n_pair,),
        in_specs=[spec(xs, f_pair), spec(ys, f_pair), spec(xs, b_pair), spec(ys, b_pair),
                  _full((n_heads, c, HEAD_DIM)), _full((n_heads, c, HEAD_DIM)),
                  _full((n_heads, 8, HEAD_DIM)), _full((n_heads, 8, HEAD_DIM))],
        out_specs=[pl.BlockSpec(st_blk, lambda i: (f_pair(i), 0, 0, 0)), pl.BlockSpec(st_blk, lambda i: (b_pair(i), 0, 0, 0))],
        out_shape=[jax.ShapeDtypeStruct((nc, n_heads, HEAD_DIM, HEAD_DIM), BF16)] * 2,
        scratch_shapes=[pltpu.VMEM((n_heads, HEAD_DIM, HEAD_DIM), F32)] * 2,
        compiler_params=_params(("arbitrary",)),
    )(xs[0], ys[0], xs[0], ys[0], tab_f, tab_b, cdf, cdb)


def _state_sweep(qkv, tabs, n_heads, nx, ncc, name):
    return _pair_sweep((qkv, 1), (qkv, 2), tabs["kf"], tabs["kb"], tabs["cdf"], tabs["cdb"], n_heads, nx, ncc, False, name)


MIX_CHUNKS = 2
MIX_ROWS = MIX_CHUNKS * CHUNK


def _halo_specs(s, n8):
    per = MIX_ROWS // 8

    def prev(g):
        return pl.BlockSpec((None, 8, s), lambda i: (g, jnp.maximum(i * per - 1, 0), 0))

    def nxt(g):
        return pl.BlockSpec((None, 8, s), lambda i: (g, jnp.minimum((i + 1) * per, n8 - 1), 0))

    return prev, nxt


def _shifted(a, before, after, has_prev, has_next):
    rows = a.shape[0]
    rowi = lax.broadcasted_iota(jnp.int32, a.shape, 0)
    am = jnp.where(rowi == 0, jnp.where(has_prev, before, 0.0), pltpu.roll(a, 1, 0))
    ap = jnp.where(rowi == rows - 1, jnp.where(has_next, after, 0.0), pltpu.roll(a, rows - 1, 0))
    return am, ap


def _neighbours(i, nx, nc):
    nxb, ncb = nx // MIX_CHUNKS, nc // MIX_CHUNKS
    return (i != 0) & (i != nxb), (i != nxb - 1) & (i != ncb - 1)


def _mix_fwd(u, qkv, sf, sb, tabs, conv_w, cnw, gnw, n_heads, nx, ncc, name):
    _, t, s = u.shape
    nc = nx + ncc
    c = CHUNK
    assert nx % MIX_CHUNKS == 0 and ncc % MIX_CHUNKS == 0

    def body(h_ref, b_ref, c_ref, z_ref, rz_ref, hp_ref, hn_ref, cp_ref, cn_ref, q_ref, k_ref, v_ref,
             sf_ref, sb_ref, dc_ref, qft, qbt, w_ref, cnw_ref, gnw_ref, y_ref, o_ref):
        i = pl.program_id(0)
        has_prev, has_next = _neighbours(i, nx, nc)
        a = c_ref[...] * h_ref[...]
        am, ap = _shifted(a, cp_ref[7:8] * hp_ref[7:8], cn_ref[0:1] * hn_ref[0:1], has_prev, has_next)
        w = w_ref[...]
        y0 = w[0:1] * am + w[1:2] * a + w[2:3] * ap
        yb = b_ref[...] * y0
        r = lax.rsqrt(jnp.mean(yb * yb, axis=-1, keepdims=True) + EPS)
        y_ref[:, pl.ds(0, s)] = (_silu(z_ref[...]) * ((yb * r) * cnw_ref[...])).astype(BF16)
        for sub in range(MIX_CHUNKS):
            rows = pl.ds(sub * c, c)
            for h in range(n_heads):
                sl = pl.ds(h * HEAD_DIM, HEAD_DIM)
                q, k, v = q_ref[rows, sl], k_ref[rows, sl], v_ref[rows, sl]
                p = (_dot(q, k, NT) * dc_ref[h]).astype(BF16)
                o = _dot(p, v, NN)
                qf = q.astype(F32)
                o += _dot((qf * qft[h]).astype(BF16), sf_ref[sub, h], NN)
                o += _dot((qf * qbt[h]).astype(BF16), sb_ref[sub, h], NN)
                o_ref[rows, sl] = o
                mu = jnp.mean(o, axis=-1, keepdims=True)
                var = jnp.mean(jnp.square(o - mu), axis=-1, keepdims=True)
                on = (o - mu) * lax.rsqrt(var + EPS)
                y_ref[rows, pl.ds(s + h * HEAD_DIM, HEAD_DIM)] = (
                    _silu(rz_ref[rows, sl]) * (on * gnw_ref[:, sl])).astype(BF16)

    def seg(g):
        return pl.BlockSpec((None, MIX_ROWS, s), lambda i: (g, i, 0))

    prev, nxt = _halo_specs(s, t // 8)
    row = pl.BlockSpec((MIX_ROWS, s), lambda i: (i, 0))
    st = pl.BlockSpec((MIX_CHUNKS, n_heads, HEAD_DIM, HEAD_DIM), lambda i: (i, 0, 0, 0))
    return pl.pallas_call(
        body, name=name, grid=(nc // MIX_CHUNKS,),
        in_specs=[seg(0), seg(1), seg(2), seg(3), seg(4), prev(0), nxt(0), prev(2), nxt(2), seg(0), seg(1), seg(2),
                  st, st, _full((n_heads, c, c)), _full((n_heads, c, HEAD_DIM)), _full((n_heads, c, HEAD_DIM)),
                  _full((3, s)), _full((1, s)), _full((1, s))],
        out_specs=[pl.BlockSpec((MIX_ROWS, 2 * s), lambda i: (i, 0)), row],
        out_shape=[jax.ShapeDtypeStruct((t, 2 * s), BF16), jax.ShapeDtypeStruct((t, s), F32)],
        compiler_params=_params(("parallel",)),
    )(u, u, u, u, u, u, u, u, u, qkv, qkv, qkv, sf, sb, tabs["dc"], tabs["qf"], tabs["qb"], conv_w, cnw, gnw)


def _row_gate(mod_ref, row0, rows, n_lat, col):
    rowi = row0 + lax.broadcasted_iota(jnp.int32, (rows, 1), 0)
    return jnp.where(rowi >= n_lat, mod_ref[5:6, col], mod_ref[2:3, col])


def _out_proj(ycat, w_out, xt, mod, n_lat, tm, tn, name):
    t, d = xt.shape

    def body(a_ref, w_ref, x_ref, mod_ref, m_ref, xo_ref):
        m = _dot(a_ref[...], w_ref[...], NN)
        m_ref[...] = m.astype(BF16)
        gate = _row_gate(mod_ref, pl.program_id(1) * tm, tm, n_lat, slice(None))
        xo_ref[...] = x_ref[...] + gate * m

    blk = pl.BlockSpec((tm, tn), lambda j, i: (i, j))
    return pl.pallas_call(
        body, name=name, grid=(d // tn, t // tm),
        in_specs=[pl.BlockSpec((tm, d), lambda j, i: (i, 0)), pl.BlockSpec((d, tn), lambda j, i: (0, j)), blk,
                  pl.BlockSpec((8, tn), lambda j, i: (0, j))],
        out_specs=[blk, blk], out_shape=[jax.ShapeDtypeStruct((t, d), BF16), jax.ShapeDtypeStruct((t, d), F32)],
        compiler_params=_params(("parallel", "parallel")))(ycat, w_out, xt, mod)


def _out_proj_loss(ycat, w_out, xt, mod, tgt, fnw, n_lat, name):
    t, d = xt.shape
    nb = t // ROW_TILE
    nxb = n_lat // ROW_TILE

    def body(a_ref, w_ref, x_ref, mod_ref, t_ref, fw_ref, dx_ref, dm_ref, loss_ref, dw_ref, gacc_ref, xs, ms):
        i = pl.program_id(0)

        @pl.when(i == 0)
        def _():
            xs[...] = jnp.zeros_like(xs)
            ms[...] = jnp.zeros_like(ms)
            loss_ref[...] = jnp.zeros_like(loss_ref)
            dw_ref[...] = jnp.zeros_like(dw_ref)
            gacc_ref[...] = jnp.zeros_like(gacc_ref)

        def step(cur, prev):
            mv = mod_ref[...]
            x_prev, m_prev = xs[prev], ms[prev]
            valid = (i >= 1) & (i - 1 < nxb)
            w = fw_ref[...]
            r = lax.rsqrt(jnp.mean(x_prev * x_prev, axis=-1, keepdims=True) + EPS)
            xn = x_prev * r
            e = xn * w - t_ref[...]
            loss = 0.5 * jnp.sum(jnp.mean(e * e, axis=-1, keepdims=True), axis=0, keepdims=True)
            loss_ref[...] += jnp.where(valid, loss, 0.0)
            dy = e * (1.0 / d)
            dw_ref[0:1, :] += jnp.where(valid, jnp.sum(dy * xn, axis=0, keepdims=True), 0.0)
            dxn = dy * w
            dx = jnp.where(valid, r * (dxn - xn * jnp.mean(dxn * xn, axis=-1, keepdims=True)), 0.0)
            dx_ref[...] = dx
            dm_ref[...] = (dx * mv[2:3]).astype(BF16)
            gacc_ref[2:3, :] += jnp.sum(dx * m_prev, axis=0, keepdims=True)

            m = _dot(a_ref[...], w_ref[...], NN)
            gate = jnp.where(jnp.minimum(i, nb - 1) >= nxb, mv[5:6], mv[2:3])
            xs[cur] = x_ref[...] + gate * m
            ms[cur] = m

        @pl.when(i % 2 == 0)
        def _():
            step(0, 1)

        @pl.when(i % 2 == 1)
        def _():
            step(1, 0)

    cur = pl.BlockSpec((ROW_TILE, d), lambda i: (jnp.minimum(i, nb - 1), 0))
    prev = pl.BlockSpec((ROW_TILE, d), lambda i: (jnp.maximum(i - 1, 0), 0))
    return pl.pallas_call(
        body, name=name, grid=(nb + 1,),
        in_specs=[cur, _full((d, d)), cur, _full((8, d)),
                  pl.BlockSpec((ROW_TILE, d), lambda i: (jnp.clip(i - 1, 0, nxb - 1), 0)), _full((1, d))],
        out_specs=[prev, prev, _full((8, HEAD_DIM)), _full((8, d)), _full((8, d))],
        out_shape=[jax.ShapeDtypeStruct((t, d), F32), jax.ShapeDtypeStruct((t, d), BF16),
                   jax.ShapeDtypeStruct((8, HEAD_DIM), F32), jax.ShapeDtypeStruct((8, d), F32),
                   jax.ShapeDtypeStruct((8, d), F32)],
        scratch_shapes=[pltpu.VMEM((2, ROW_TILE, d), F32), pltpu.VMEM((2, ROW_TILE, d), F32)],
        compiler_params=_params(("arbitrary",)))(ycat, w_out, xt, mod, tgt, fnw)


def _matmul_nt(a, w, tn, name, after=()):
    t, k = a.shape
    n = w.shape[0]
    tm = _mm_rows(t)

    def body(a_ref, w_ref, *rest):
        rest[-1][...] = _dot(a_ref[...], w_ref[...], NT)

    return pl.pallas_call(
        body, name=name, grid=(n // tn, t // tm),
        in_specs=[pl.BlockSpec((tm, k), lambda j, i: (i, 0)), pl.BlockSpec((tn, k), lambda j, i: (j, 0))]
        + [pl.BlockSpec(memory_space=pl.ANY)] * len(after),
        out_specs=pl.BlockSpec((tm, tn), lambda j, i: (i, j)),
        out_shape=jax.ShapeDtypeStruct((t, n), F32),
        compiler_params=_params(("parallel", "parallel")))(a, w, *after)


def _weight_grad(a, b, bm, bt, name):
    t, m = a.shape
    n_g, _, n = b.shape
    nt = t // bt

    def body(a_ref, b_ref, o_ref, acc):
        k = pl.program_id(2)

        @pl.when(k == 0)
        def _():
            acc[...] = jnp.zeros_like(acc)

        acc[...] += _dot(a_ref[...], b_ref[...], TN)

        @pl.when(k == nt - 1)
        def _():
            o_ref[...] = acc[...].astype(o_ref.dtype)

    return pl.pallas_call(
        body, name=name, grid=(n_g, m // bm, nt),
        in_specs=[pl.BlockSpec((bt, bm), lambda g, i, k: (k, i)), pl.BlockSpec((None, bt, n), lambda g, i, k: (g, k, 0))],
        out_specs=pl.BlockSpec((None, bm, n), lambda g, i, k: (g, i, 0)),
        out_shape=jax.ShapeDtypeStruct((n_g, m, n), BF16),
        scratch_shapes=[pltpu.VMEM((bm, n), F32)],
        compiler_params=_params(("parallel", "parallel", "arbitrary")))(a, b)


def _weight_grad_beside_prenorm_bwd(a, b, dhx, xt, dxo, nw, mod, below, n_lat, name):
    t, m = a.shape
    n_g, _, n = b.shape
    d = xt.shape[1]
    bt = _mm_rows(t)
    nt = t // bt
    rows = t // (n_g * nt)
    n_piece = 2 if rows % 32 == 0 and m % 2 == 0 else 1
    rows_p, m_p = rows // n_piece, m // n_piece
    assert rows * n_g * nt == t and rows_p % 8 == 0

    def body(a_ref, b_ref, dh_ref, x_ref, dxo_ref, nw_ref, mod_ref, m_ref, modb_ref,
             o_ref, dx_ref, acc_ref, dm_ref, gacc_ref, acc):
        g, k = pl.program_id(0), pl.program_id(1)
        step = g * nt + k

        @pl.when(step == 0)
        def _():
            acc_ref[...] = jnp.zeros_like(acc_ref)
            gacc_ref[...] = jnp.zeros_like(gacc_ref)

        @pl.when(k == 0)
        def _():
            acc[...] = jnp.zeros_like(acc)

        mv, mb, nw_v = mod_ref[...], modb_ref[...], nw_ref[...]
        for p in range(n_piece):
            rs = pl.ds(p * rows_p, rows_p)
            rowi = step * rows + p * rows_p + lax.broadcasted_iota(jnp.int32, (rows_p, 1), 0)
            ctx = rowi >= n_lat
            w_lat = jnp.where(ctx, 0.0, 1.0)
            w_ctx = 1.0 - w_lat
            scale1 = 1.0 + jnp.where(ctx, mv[4:5], mv[1:2])
            x = x_ref[rs, :]
            r = lax.rsqrt(jnp.mean(x * x, axis=-1, keepdims=True) + EPS)
            xn = x * r
            dh = dh_ref[rs, :]
            dsc = dh * (xn * nw_v)
            acc_ref[0:1, :] += jnp.sum(dh * w_lat, axis=0, keepdims=True)
            acc_ref[1:2, :] += jnp.sum(dsc * w_lat, axis=0, keepdims=True)
            acc_ref[3:4, :] += jnp.sum(dh * w_ctx, axis=0, keepdims=True)
            acc_ref[4:5, :] += jnp.sum(dsc * w_ctx, axis=0, keepdims=True)
            acc_ref[6:7, :] += jnp.sum(dh * scale1 * xn, axis=0, keepdims=True)
            dxn = dh * (nw_v * scale1)
            dx = dxo_ref[rs, :] + r * (dxn - xn * jnp.mean(dxn * xn, axis=-1, keepdims=True))
            dx_ref[rs, :] = dx
            dm_ref[rs, :] = (dx * jnp.where(ctx, mb[5:6], mb[2:3])).astype(BF16)
            dg = dx * m_ref[rs, :].astype(F32)
            gacc_ref[2:3, :] += jnp.sum(dg * w_lat, axis=0, keepdims=True)
            gacc_ref[5:6, :] += jnp.sum(dg * w_ctx, axis=0, keepdims=True)

            ms_ = pl.ds(p * m_p, m_p)
            acc[ms_, :] += _dot(a_ref[:, ms_], b_ref[...], TN)

        @pl.when(k == nt - 1)
        def _():
            o_ref[...] = acc[...].astype(o_ref.dtype)

    side = pl.BlockSpec((rows, d), lambda g, k: (g * nt + k, 0))
    acc8 = _full((8, d))
    return pl.pallas_call(
        body, name=name, grid=(n_g, nt),
        in_specs=[pl.BlockSpec((bt, m), lambda g, k: (k, 0)), pl.BlockSpec((None, bt, n), lambda g, k: (g, k, 0)),
                  side, side, side, _full((1, d)), acc8, side, acc8],
        out_specs=[pl.BlockSpec((None, m, n), lambda g, k: (g, 0, 0)), side, acc8, side, acc8],
        out_shape=[jax.ShapeDtypeStruct((n_g, m, n), BF16), jax.ShapeDtypeStruct((t, d), F32),
                   jax.ShapeDtypeStruct((8, d), F32), jax.ShapeDtypeStruct((t, d), BF16),
                   jax.ShapeDtypeStruct((8, d), F32)],
        scratch_shapes=[pltpu.VMEM((m, n), F32)],
        compiler_params=_params(("arbitrary", "arbitrary")))(a, b, dhx, xt, dxo, nw, mod, *below)


def _mix_bwd_a(dycat, u, o, conv_w, cnw, gnw, n_heads, nx, ncc, name):
    _, t, s = u.shape
    nc = nx + ncc

    def body(dy_ref, h_ref, b_ref, c_ref, z_ref, rz_ref, hp_ref, hn_ref, cp_ref, cn_ref, o_ref, w_ref,
             cnw_ref, gnw_ref, g_ref, dz_ref, db_ref, drz_ref, do_ref, acc_ref):
        i = pl.program_id(0)

        @pl.when(i == 0)
        def _():
            acc_ref[...] = jnp.zeros_like(acc_ref)

        has_prev, has_next = _neighbours(i, nx, nc)
        a = c_ref[...] * h_ref[...]
        am, ap = _shifted(a, cp_ref[7:8] * hp_ref[7:8], cn_ref[0:1] * hn_ref[0:1], has_prev, has_next)
        w = w_ref[...]
        y0 = w[0:1] * am + w[1:2] * a + w[2:3] * ap
        bb = b_ref[...]
        yb = bb * y0
        r = lax.rsqrt(jnp.mean(yb * yb, axis=-1, keepdims=True) + EPS)
        ynn = yb * r
        z = z_ref[...]
        dyc = dy_ref[:, pl.ds(0, s)]
        cw = cnw_ref[...]
        sz, dsz = _silu_and_slope(z)
        dz_ref[...] = (dyc * (ynn * cw) * dsz).astype(BF16)
        dyn = dyc * sz
        acc_ref[0:1, :] += jnp.sum(dyn * ynn, axis=0, keepdims=True)
        dynn = dyn * cw
        dyb = r * (dynn - ynn * jnp.mean(dynn * ynn, axis=-1, keepdims=True))
        db_ref[...] = (dyb * y0).astype(BF16)
        g_ref[...] = dyb * bb
        for h in range(n_heads):
            sl = pl.ds(h * HEAD_DIM, HEAD_DIM)
            ov = o_ref[:, sl]
            mu = jnp.mean(ov, axis=-1, keepdims=True)
            var = jnp.mean(jnp.square(ov - mu), axis=-1, keepdims=True)
            rs = lax.rsqrt(var + EPS)
            on = (ov - mu) * rs
            dyr = dy_ref[:, pl.ds(s + h * HEAD_DIM, HEAD_DIM)]
            rz = rz_ref[:, sl]
            gw = gnw_ref[:, sl]
            srz, dsrz = _silu_and_slope(rz)
            drz_ref[:, sl] = (dyr * (on * gw) * dsrz).astype(BF16)
            dyg = dyr * srz
            acc_ref[1:2, sl] += jnp.sum(dyg * on, axis=0, keepdims=True)
            don = dyg * gw
            do = rs * (don - jnp.mean(don, axis=-1, keepdims=True)
                       - on * jnp.mean(don * on, axis=-1, keepdims=True))
            do_ref[:, sl] = do.astype(BF16)

    def seg(g):
        return pl.BlockSpec((None, MIX_ROWS, s), lambda i: (g, i, 0))

    prev, nxt = _halo_specs(s, t // 8)
    row = pl.BlockSpec((MIX_ROWS, s), lambda i: (i, 0))
    return pl.pallas_call(
        body, name=name, grid=(nc // MIX_CHUNKS,),
        in_specs=[pl.BlockSpec((MIX_ROWS, 2 * s), lambda i: (i, 0)), seg(0), seg(1), seg(2), seg(3), seg(4),
                  prev(0), nxt(0), prev(2), nxt(2), row, _full((3, s)), _full((1, s)), _full((1, s))],
        out_specs=[row, row, row, row, row, _full((8, s))],
        out_shape=[jax.ShapeDtypeStruct((t, s), F32)] + [jax.ShapeDtypeStruct((t, s), BF16)] * 4
        + [jax.ShapeDtypeStruct((8, s), F32)],
        compiler_params=_params(("arbitrary",)),
    )(dycat, u, u, u, u, u, u, u, u, u, o, conv_w, cnw, gnw)


def _grad_state_sweep(qkv, do, tabs, n_heads, nx, ncc, name):
    return _pair_sweep((qkv, 0), (do, None), tabs["qf"], tabs["qb"], tabs["cdf"], tabs["cdb"], n_heads, nx, ncc, True, name)


def _mix_bwd_b(u, g, dz, db, drz, qkv, do, sf, sb, gf, gb, tabs, cos, sa, sb_tab, conv_w,
               n_heads, nx, ncc, name):
    _, t, s = u.shape
    nc = nx + ncc
    c = CHUNK
    k_scale = HEAD_DIM ** -0.5

    def body(h_ref, c_ref, g_ref, gp_ref, gn_ref, dz_ref, db_ref, drz_ref, q_ref, k_ref, v_ref, do_ref,
             sf_ref, sb_ref, gf_ref, gb_ref, dc_t, dlf_t, dlb_t, qft, kft, qbt, kbt, cdf, cdb, lg_ref,
             cos_ref, sa_ref, sb_ref2, w_ref, du_ref, dw_ref, dlg_ref):
        i = pl.program_id(0)

        @pl.when(i == 0)
        def _():
            dw_ref[...] = jnp.zeros_like(dw_ref)
            dlg_ref[...] = jnp.zeros_like(dlg_ref)

        has_prev, has_next = _neighbours(i, nx, nc)
        gv = g_ref[...]
        gm, gp = _shifted(gv, gp_ref[7:8], gn_ref[0:1], has_prev, has_next)
        w = w_ref[...]
        da = w[0:1] * gp + w[1:2] * gv + w[2:3] * gm
        hh, cc = h_ref[...], c_ref[...]
        du_ref[0] = (da * cc).astype(BF16)
        du_ref[2] = (da * hh).astype(BF16)
        a = cc * hh
        dw_ref[0:1, :] += jnp.sum(a * gp, axis=0, keepdims=True)
        dw_ref[1:2, :] += jnp.sum(a * gv, axis=0, keepdims=True)
        dw_ref[2:3, :] += jnp.sum(a * gm, axis=0, keepdims=True)
        du_ref[1] = db_ref[...]
        du_ref[3] = dz_ref[...]
        du_ref[7] = drz_ref[...]

        pos = lax.broadcasted_iota(jnp.int32, (c, HEAD_DIM), 0).astype(F32)
        row8 = lax.broadcasted_iota(jnp.int32, (8, HEAD_DIM), 0)
        lane8 = lax.broadcasted_iota(jnp.int32, (8, HEAD_DIM), 1)
        dlg = jnp.zeros((8, HEAD_DIM), F32)
        for sub, h in [(sub, h) for sub in range(MIX_CHUNKS) for h in range(n_heads)]:
            rows = pl.ds(sub * c, c)
            co, ra, rb = cos_ref[rows, :], sa_ref[rows, :], sb_ref2[rows, :]
            sl = pl.ds(h * HEAD_DIM, HEAD_DIM)
            q, k, v, do = q_ref[rows, sl], k_ref[rows, sl], v_ref[rows, sl], do_ref[rows, sl]
            qf, kf, dof = q.astype(F32), k.astype(F32), do.astype(F32)
            s_f, s_b, g_f, g_b = sf_ref[sub, h], sb_ref[sub, h], gf_ref[sub, h], gb_ref[sub, h]
            p = _dot(q, k, NT)
            pd = _dot(do, v, NT)
            pdd = (pd * dc_t[h]).astype(BF16)
            dq = _dot(pdd, k, NN)
            dk = _dot(pdd, q, TN)
            dv = _dot((p * dc_t[h]).astype(BF16), do, TN)
            dq_f = _dot((dof * qft[h]).astype(BF16), s_f, NT)
            dq_b = _dot((dof * qbt[h]).astype(BF16), s_b, NT)
            dk_f = _dot(v, g_f, NT) * kft[h]
            dk_b = _dot(v, g_b, NT) * kbt[h]
            dv += _dot((kf * kft[h]).astype(BF16), g_f, NN) + _dot((kf * kbt[h]).astype(BF16), g_b, NN)
            ppd = p * pd
            cd_f, cd_b = cdf[h][0:1, :], cdb[h][0:1, :]
            t_f = _sum_all(dlf_t[h] * ppd + (pos + 1.0) * qf * dq_f + (c - 1.0 - pos) * kf * dk_f
                           + float(c) * (cd_f * (g_f.astype(F32) * s_f.astype(F32))))
            t_b = _sum_all(dlb_t[h] * ppd + (c - pos) * qf * dq_b + pos * kf * dk_b
                           + float(c) * (cd_b * (g_b.astype(F32) * s_b.astype(F32))))
            dlg += jnp.where((row8 == 0) & (lane8 == h), t_f, 0.0) + jnp.where((row8 == 1) & (lane8 == h), t_b, 0.0)
            du_ref[4, rows, sl] = _rope_bwd(dq + dq_f + dq_b, co, ra, rb).astype(BF16)
            du_ref[5, rows, sl] = (_rope_bwd(dk + dk_f + dk_b, co, ra, rb) * k_scale).astype(BF16)
            du_ref[6, rows, sl] = dv.astype(BF16)
        dlg_ref[...] += dlg

        @pl.when(i == nc // MIX_CHUNKS - 1)
        def _():
            dlg_ref[...] = dlg_ref[...] * lg_ref[...]

    def seg(gi):
        return pl.BlockSpec((None, MIX_ROWS, s), lambda i: (gi, i, 0))

    per = MIX_ROWS // 8
    n8 = t // 8
    row = pl.BlockSpec((MIX_ROWS, s), lambda i: (i, 0))
    st = pl.BlockSpec((MIX_CHUNKS, n_heads, HEAD_DIM, HEAD_DIM), lambda i: (i, 0, 0, 0))
    tab = pl.BlockSpec((MIX_ROWS, HEAD_DIM), lambda i: (i, 0))
    hc = _full((n_heads, c, HEAD_DIM))
    cc_ = _full((n_heads, c, c))
    h8 = _full((n_heads, 8, HEAD_DIM))
    return pl.pallas_call(
        body, name=name, grid=(nc // MIX_CHUNKS,),
        in_specs=[seg(0), seg(2), row,
                  pl.BlockSpec((8, s), lambda i: (jnp.maximum(i * per - 1, 0), 0)),
                  pl.BlockSpec((8, s), lambda i: (jnp.minimum((i + 1) * per, n8 - 1), 0)),
                  row, row, row, seg(0), seg(1), seg(2), row, st, st, st, st, cc_, cc_, cc_, hc, hc, hc, hc, h8, h8,
                  _full((8, HEAD_DIM)), tab, tab, tab, _full((3, s))],
        out_specs=[pl.BlockSpec((8, MIX_ROWS, s), lambda i: (0, i, 0)), _full((8, s)), _full((8, HEAD_DIM))],
        out_shape=[jax.ShapeDtypeStruct((8, t, s), BF16), jax.ShapeDtypeStruct((8, s), F32),
                   jax.ShapeDtypeStruct((8, HEAD_DIM), F32)],
        compiler_params=_params(("arbitrary",)),
    )(u, u, g, g, g, dz, db, drz, qkv, qkv, qkv, do, sf, sb, gf, gb, tabs["dc"], tabs["dlf"], tabs["dlb"],
      tabs["qf"], tabs["kf"], tabs["qb"], tabs["kb"], tabs["cdf"], tabs["cdb"], tabs["lg"], cos, sa, sb_tab, conv_w)


def _in_proj_bwd(du, wgs, tm, gs, name, after=()):
    n_seg, t, s = du.shape
    d = wgs[0].shape[1]
    n_w = len(wgs)
    widths = [w.shape[2] for w in wgs]
    assert sum(widths) == s

    def body(a_ref, *rest):
        w_refs, o_ref = rest[:n_w], rest[-1]
        g = pl.program_id(1)
        part = None
        for j in range(gs):
            col = 0
            for w_ref, width in zip(w_refs, widths):
                term = _dot(a_ref[j, :, col:col + width], w_ref[j], NT)
                part = term if part is None else part + term
                col += width

        @pl.when(g == 0)
        def _():
            o_ref[...] = part

        @pl.when(g > 0)
        def _():
            o_ref[...] += part

    return pl.pallas_call(
        body, name=name, grid=(t // tm, n_seg // gs),
        in_specs=[pl.BlockSpec((gs, tm, s), lambda i, g: (g, i, 0))]
        + [pl.BlockSpec((gs, d, width), lambda i, g: (g, 0, 0)) for width in widths]
        + [pl.BlockSpec(memory_space=pl.ANY)] * len(after),
        out_specs=pl.BlockSpec((tm, d), lambda i, g: (i, 0)),
        out_shape=jax.ShapeDtypeStruct((t, d), F32),
        compiler_params=_params(("parallel", "arbitrary")))(du, *wgs, *after)


def _prenorm_bwd_first(dhx, xt, dxo, nw, mod, n_lat, name):
    t, d = xt.shape
    nxb = n_lat // ROW_TILE

    def body(dh_ref, x_ref, dxo_ref, nw_ref, mod_ref, dx_ref, acc_ref):
        i = pl.program_id(0)

        @pl.when(i == 0)
        def _():
            acc_ref[...] = jnp.zeros_like(acc_ref)

        ctx = i >= nxb
        m = mod_ref[...]
        scale1 = 1.0 + jnp.where(ctx, m[4:5], m[1:2])
        x = x_ref[...]
        nw_v = nw_ref[...]
        r = lax.rsqrt(jnp.mean(x * x, axis=-1, keepdims=True) + EPS)
        xn = x * r
        dh = dh_ref[...]
        dshift = jnp.sum(dh, axis=0, keepdims=True)
        dscale = jnp.sum(dh * (xn * nw_v), axis=0, keepdims=True)
        acc_ref[6:7, :] += jnp.sum(dh * scale1 * xn, axis=0, keepdims=True)
        dxn = dh * (nw_v * scale1)
        dx = dxo_ref[...] + r * (dxn - xn * jnp.mean(dxn * xn, axis=-1, keepdims=True))

        @pl.when(i < nxb)
        def _():
            acc_ref[0:1, :] += dshift
            acc_ref[1:2, :] += dscale
            dx_ref[...] = dx

        @pl.when(i >= nxb)
        def _():
            acc_ref[3:4, :] += dshift
            acc_ref[4:5, :] += dscale

    row = pl.BlockSpec((ROW_TILE, d), lambda i: (i, 0))
    acc = _full((8, d))
    return pl.pallas_call(body, name=name, grid=(t // ROW_TILE,),
                          in_specs=[row, row, row, _full((1, d)), acc],
                          out_specs=[pl.BlockSpec((ROW_TILE, d), lambda i: (jnp.minimum(i, nxb - 1), 0)), acc],
                          out_shape=[jax.ShapeDtypeStruct((n_lat, d), F32), jax.ShapeDtypeStruct((8, d), F32)],
                          compiler_params=_params(("arbitrary",)))(dhx, xt, dxo, nw, mod)


def _adamw(g, w, m, v):
    m = ADAM_B1 * m + (1.0 - ADAM_B1) * g
    v = ADAM_B2 * v + (1.0 - ADAM_B2) * jnp.square(g)
    m_hat = m / (1.0 - ADAM_B1 ** ADAM_STEP)
    v_hat = v / (1.0 - ADAM_B2 ** ADAM_STEP)
    delta = -ADAM_LR * (m_hat / (jnp.sqrt(v_hat) + ADAM_EPS) + ADAM_WD * w)
    return delta, m, v


def _sum_adamw(parts, w, m, v, name, row0=0, into=None):
    n_p, r, n = parts.shape
    r_all = w.shape[0]
    part_block_bytes = 4 * 1024 * 1024
    br = 8
    for cand in (512, 256, 128, 64, 32, 16):
        if r % cand == 0 and row0 % cand == 0 and n_p * cand * n * parts.dtype.itemsize <= part_block_bytes:
            br = cand
            break
    blk0 = row0 // br

    def body(p_ref, w_ref, m_ref, v_ref, *rest):
        g_out, d_out, m_out, v_out = rest[-4:]
        g = p_ref[0].astype(F32)
        for j in range(1, n_p):
            g = g + p_ref[j].astype(F32)
        g_out[...] = g
        d_out[...], m_out[...], v_out[...] = _adamw(g, w_ref[...], m_ref[...], v_ref[...])

    row = pl.BlockSpec((br, n), lambda i: (i + blk0, 0))
    kept = [] if into is None else list(into)
    return pl.pallas_call(body, name=name, grid=(r // br,),
                          in_specs=[pl.BlockSpec((n_p, br, n), lambda i: (0, i, 0)), row, row, row]
                          + [pl.BlockSpec(memory_space=pl.ANY)] * len(kept),
                          out_specs=[row] * 4, out_shape=[jax.ShapeDtypeStruct((r_all, n), F32)] * 4,
                          input_output_aliases={4 + j: j for j in range(len(kept))},
                          compiler_params=_params(("parallel",)))(parts, w, m, v, *kept)


def _rope_tables(n_lat, n_ctx):
    f = HEAD_DIM // 4
    rows = n_lat // GRID_W
    inv = ROPE_BASE ** (-jnp.arange(f, dtype=F32) / f)
    ang_r = jnp.arange(rows).astype(F32)[:, None] * inv[None, :]
    ang_c = jnp.arange(GRID_W).astype(F32)[:, None] * inv[None, :]

    def by_row(a):
        return jnp.broadcast_to(a[:, None, :], (rows, GRID_W, f)).reshape(n_lat, f)

    def by_col(a):
        return jnp.broadcast_to(a[None, :, :], (rows, GRID_W, f)).reshape(n_lat, f)

    cr, sr, cc, sc = by_row(jnp.cos(ang_r)), by_row(jnp.sin(ang_r)), by_col(jnp.cos(ang_c)), by_col(jnp.sin(ang_c))
    zero = jnp.zeros_like(cr)
    cos = jnp.concatenate([cr, cr, cc, cc], axis=-1)
    sa = jnp.concatenate([-sr, zero, -sc, zero], axis=-1)
    sb = jnp.concatenate([zero, sr, zero, sc], axis=-1)
    pad = jnp.zeros((n_ctx, HEAD_DIM), F32)
    return (jnp.concatenate([cos, pad + 1.0], axis=0), jnp.concatenate([sa, pad], axis=0),
            jnp.concatenate([sb, pad], axis=0))


def _pad_rows(a, rows):
    return jnp.pad(a, [(0, rows - a.shape[0])] + [(0, 0)] * (a.ndim - 1))


def _pad_cols(a, cols):
    return jnp.pad(a, [(0, 0), (0, cols - a.shape[1])])


def kernel(x, c, ctx, c_ctx, norm_w, w_mod, b_mod, w_in, conv_w, conv_norm_w, ret_norm_w, ret_decay_f, ret_decay_b, w_out, final_norm_w, loss_target, m_c_ctx, m_norm_w, m_w_mod, m_b_mod, m_w_in, m_conv_w, m_conv_norm_w, m_ret_norm_w, m_ret_decay_f, m_ret_decay_b, m_w_out, m_final_norm_w, v_c_ctx, v_norm_w, v_w_mod, v_b_mod, v_w_in, v_conv_w, v_conv_norm_w, v_ret_norm_w, v_ret_decay_f, v_ret_decay_b, v_w_out, v_final_norm_w):
    depth = norm_w.shape[0]
    n_lat, d = x.shape[1], x.shape[2]
    n_ctx = ctx.shape[1]
    s = d // 2
    n_heads = ret_decay_f.shape[1]
    nx, ncc = n_lat // CHUNK, n_ctx // CHUNK
    n_mod = w_mod.shape[2]
    n_cw = conv_w.shape[2]
    r_out = w_out.shape[1]
    assert s == n_heads * HEAD_DIM and w_in.shape[2] == s and N_DEV * r_out == d
    assert n_lat % ROW_TILE == 0 and n_ctx % ROW_TILE == 0 and 3 * depth * n_cw <= d and d >= 3 * n_mod // 3
    me = 4 * lax.axis_index("x") + 2 * lax.axis_index("y") + lax.axis_index("c")

    w_in_bf = [w_in[l].astype(BF16) for l in range(depth)]
    w_out_bf = [w_out[l].astype(BF16) for l in range(depth)]

    first = jnp.concatenate([c.reshape(1, d), _pad_cols(conv_w.reshape(1, -1), d), jnp.zeros((6, d), F32)], axis=0)
    (first_g,) = _all_gather([first], "gather_cond", True)
    first_g = first_g.reshape(N_DEV, 8, d)
    c_all = first_g[:, 0, :]
    conv_full = first_g[:, 1, :3 * depth * n_cw].reshape(N_DEV, depth, 3, n_cw)
    conv_full = conv_full.transpose(1, 2, 0, 3).reshape(depth, 3, N_DEV * n_cw)
    c9 = jnp.concatenate([c_all, c_ctx.reshape(1, d), jnp.zeros((7, d), F32)], axis=0)

    b_sh = lax.dynamic_slice(b_mod, (0, me * n_mod), (depth, n_mod))
    mod_sh = jnp.concatenate([_mod_rows(c9, w_mod[l], b_sh[l:l + 1], f"mod_rows_l{l}") for l in range(depth)], axis=0)
    (mod_g,) = _all_gather([mod_sh], "gather_mod", True)
    mod_g = mod_g.reshape(N_DEV, depth, 16, n_mod)
    mods = []
    for l in range(depth):
        mine = lax.dynamic_index_in_dim(mod_g[:, l], me, axis=1, keepdims=False).reshape(3, d)
        cx = mod_g[:, l, 8, :].reshape(3, d)
        mods.append(jnp.concatenate([mine, cx, jnp.zeros((2, d), F32)], axis=0))

    halves = [w_in_bf[0][:, :s // 2], w_in_bf[0][:, s // 2:]]
    near, order = [], [mod_g]
    for j, part in enumerate(halves):
        near.append(_push_start([part], [_landing(part, me)], "near", f"w_in0_start_{j}", after=order))
        order = near[-1][4:]
    pending = []
    for k in range(depth):
        srcs = [w_out_bf[k]] + ([w_in_bf[k]] if k > 0 else [])
        started = _push_start(srcs, [_landing(a, me) for a in srcs], "gather", f"weights_start_l{k}", after=order)
        pending.append(started[:4])
        order = started[4:]
    w_in_g = [None] * depth
    w_out_g = [None] * depth

    cos, sa, sb_tab = _rope_tables(n_lat, n_ctx)
    t_all = n_lat + n_ctx

    saved = []
    xt = None
    for l in range(depth):
        tiles = _tiles(l, t_all, d)
        names = ["dc", "dlf", "dlb", "qf", "kf", "qb", "kb", "cdf", "cdb", "lg"]
        dec = jnp.stack([ret_decay_f[l], ret_decay_b[l]], axis=0)
        tabs = dict(zip(names, _decay_tables(dec, n_heads, f"decay_tables_l{l}")))
        if l == 0:
            hx, xt = _prenorm_first(x[0], ctx[0], norm_w[0:1], mods[0], "prenorm_l0", after=order)
            gathered, out, after = [], None, hx
            for j in range(2):
                (landed,) = _push_wait(*near[j][:4], "near", after, f"w_in0_wait_{j}")
                relay = _push_start([], [landed], "relay", f"w_in0_relay_start_{j}")
                (landed,) = _push_wait(*relay[:4], "relay", relay[4], f"w_in0_relay_wait_{j}")
                gathered.append(landed)
                out = _in_proj(hx, landed, cos, sa, sb_tab, s, j, tiles["in_tm"], f"in_proj_l0_{j}", into=out)
                after = out[0]
            u, qkv = out
            w_in_g[0] = gathered
        else:
            landed = _push_wait(*pending[l], "gather", xt, f"weights_wait_l{l}")
            w_out_g[l], w_in_g[l] = landed[0].reshape(d, d), [landed[1]]
            hx = _prenorm(xt, norm_w[l:l + 1], mods[l], n_lat, f"prenorm_l{l}")
            u, qkv = _in_proj(hx, w_in_g[l][0], cos, sa, sb_tab, s, 0, tiles["in_tm"], f"in_proj_l{l}")
        sf, sb = _state_sweep(qkv, tabs, n_heads, nx, ncc, f"state_sweep_l{l}")
        ycat, o = _mix_fwd(u, qkv, sf, sb, tabs, conv_full[l], conv_norm_w[l:l + 1], ret_norm_w[l:l + 1],
                           n_heads, nx, ncc, f"mix_fwd_l{l}")
        if l == 0:
            (landed,) = _push_wait(*pending[0], "gather", ycat, "weights_wait_l0")
            w_out_g[0] = landed.reshape(d, d)
        m_res = x_new = None
        if l < depth - 1:
            m_res, x_new = _out_proj(ycat, w_out_g[l], xt, mods[l], n_lat, tiles["out_tm"], tiles["out_tn"],
                                     f"out_proj_l{l}")
        else:
            dxt, dm, loss_blk, dfnw, gate_acc = _out_proj_loss(ycat, w_out_g[l], xt, mods[l], loss_target[0],
                                                               final_norm_w.reshape(1, d), n_lat, f"out_proj_loss_l{l}")
        saved.append(dict(tabs=tabs, xt=xt, hx=hx, u=u, qkv=qkv, sf=sf, sb=sb, ycat=ycat, o=o, m=m_res, tiles=tiles))
        xt = x_new

    loss = lax.psum(loss_blk[0, 0], MESH_AXES)

    dmod_x, dmod_c, dnw, dcnw, dgnw, dconv, ddec, dwin, dwout = [], [], [], [], [], [], [], [], []
    started_token = ()
    for l in reversed(range(depth)):
        sv = saved[l]
        tiles = sv["tiles"]
        dycat = _matmul_nt(dm, w_out_g[l], tiles["ob_tn"], f"out_proj_bwd_l{l}", after=started_token)
        dwout.append(_weight_grad(sv["ycat"], dm.reshape(1, *dm.shape), tiles["wo_bm"], _mm_rows(t_all),
                                  f"w_out_grad_l{l}")[0])
        g, dz, db, drz, do, norm_acc = _mix_bwd_a(dycat, sv["u"], sv["o"], conv_full[l], conv_norm_w[l:l + 1],
                                                   ret_norm_w[l:l + 1], n_heads, nx, ncc, f"mix_bwd_a_l{l}")
        gf, gb = _grad_state_sweep(sv["qkv"], do, sv["tabs"], n_heads, nx, ncc, f"grad_state_sweep_l{l}")
        du, conv_acc, dlg = _mix_bwd_b(sv["u"], g, dz, db, drz, sv["qkv"], do, sv["sf"], sv["sb"],
                                       gf, gb, sv["tabs"], cos, sa, sb_tab, conv_full[l], n_heads, nx, ncc,
                                       f"mix_bwd_b_l{l}")
        gate_acc_l = gate_acc
        if l > 0:
            dhx = _in_proj_bwd(du, w_in_g[l], tiles["bwd_tm"], tiles["bwd_gs"], f"in_proj_bwd_l{l}")
            below = (saved[l - 1]["m"], mods[l - 1])
            dwin_l, dxt, pre_acc, dm, gate_acc = _weight_grad_beside_prenorm_bwd(
                sv["hx"], du, dhx, sv["xt"], dxt, norm_w[l:l + 1], mods[l], below, n_lat, f"w_in_grad_l{l}")
        else:
            dwin_l = _weight_grad(sv["hx"], du, tiles["wg_bm"], tiles["wg_bt"], f"w_in_grad_l{l}")
        srcs = [dwin_l, dwout[-1].reshape(N_DEV, r_out, d)]
        lands = [_landing(lax.dynamic_index_in_dim(a, me, axis=0, keepdims=False), me) for a in srcs]
        started = _push_start(srcs, lands, "scatter", f"grads_start_l{l}")
        dwin.append(started[:4])
        started_token = started[4:]
        if l == 0:
            dhx = _in_proj_bwd(du, w_in_g[l], tiles["bwd_tm"], tiles["bwd_gs"], f"in_proj_bwd_l{l}", after=started[4:])
            dxt, pre_acc = _prenorm_bwd_first(dhx, sv["xt"], dxt, norm_w[l:l + 1], mods[l], n_lat, f"prenorm_bwd_l{l}")
        dmod_x.append(jnp.concatenate([pre_acc[0], pre_acc[1], gate_acc_l[2]]))
        dmod_c.append(jnp.concatenate([pre_acc[3], pre_acc[4], gate_acc_l[5]]))
        dnw.append(pre_acc[6])
        dcnw.append(norm_acc[0])
        dgnw.append(norm_acc[1])
        dconv.append(conv_acc[0:3])
        ddec.append(dlg[0:2, :n_heads])
    for lst in (dmod_x, dmod_c, dnw, dcnw, dgnw, dconv, ddec, dwin, dwout):
        lst.reverse()
    grad_x = dxt.reshape(1, n_lat, d)

    rows = []
    for l in range(depth):
        rows += [dmod_x[l], dmod_c[l]]
    (dmod_g,) = _all_gather([_pad_rows(jnp.stack(rows, axis=0), 8)], "gather_dmod", True)
    dmod_g = dmod_g.reshape(N_DEV, 8, 3 * d)
    mine_cols = lax.dynamic_slice(dmod_g, (0, 0, me * n_mod), (N_DEV, 8, n_mod))
    g_wmod, dcc = [], jnp.zeros((d,), F32)
    for l in range(depth):
        gw, dc_part = _mod_grads(mine_cols[:, 2 * l], mine_cols[:, 2 * l + 1], c9, w_mod[l], f"mod_grads_l{l}")
        g_wmod.append(gw)
        dcc = dcc + dc_part[0]

    n_small = 16
    small = jnp.concatenate([
        jnp.stack(dnw, axis=0),
        jnp.concatenate(dcnw).reshape(1, -1),
        jnp.concatenate(dgnw).reshape(1, -1),
        dfnw[0:1],
        dcc.reshape(1, d),
        jnp.stack(dconv, axis=0).reshape(-1, d),
        _pad_cols(jnp.stack(ddec, axis=0).reshape(1, -1), d),
    ], axis=0)
    assert depth * s == d and small.shape[0] <= n_small
    n_rows = small.shape[0]
    (small_g,) = _all_gather([_pad_rows(small, n_small)], "gather_small", True)
    small_g = small_g.reshape(N_DEV, n_small, d)

    def pack_small(nw_, cn_, gn_, fn_, cc_, df_, db_):
        return _pad_rows(jnp.concatenate([
            nw_, cn_.reshape(1, -1), gn_.reshape(1, -1), fn_.reshape(1, d), cc_.reshape(1, d),
            jnp.zeros((n_rows - depth - 5, d), F32),
            _pad_cols(jnp.stack([df_, db_], axis=1).reshape(1, -1), d)], axis=0), n_small)

    w_s = pack_small(norm_w, conv_norm_w, ret_norm_w, final_norm_w, c_ctx, ret_decay_f, ret_decay_b)
    m_s = pack_small(m_norm_w, m_conv_norm_w, m_ret_norm_w, m_final_norm_w, m_c_ctx, m_ret_decay_f, m_ret_decay_b)
    v_s = pack_small(v_norm_w, v_conv_norm_w, v_ret_norm_w, v_final_norm_w, v_c_ctx, v_ret_decay_f, v_ret_decay_b)
    small_out = _sum_adamw(small_g, w_s, m_s, v_s, "adamw_small")

    def unpack_small(a):
        nw_ = a[0:depth]
        cn_ = a[depth].reshape(depth, s)
        gn_ = a[depth + 1].reshape(depth, s)
        fn_ = a[depth + 2]
        cc_ = a[depth + 3]
        dd = a[n_rows - 1, :depth * 2 * n_heads].reshape(depth, 2, n_heads)
        return dict(c_ctx=cc_, norm_w=nw_, conv_norm_w=cn_, ret_norm_w=gn_, ret_decay_f=dd[:, 0], ret_decay_b=dd[:, 1],
                    final_norm_w=fn_)

    res = {}
    for kind, arr in zip(("grad", "delta", "m", "v"), small_out):
        for k_, val in unpack_small(arr).items():
            res[(kind, k_)] = val

    bm_parts = jnp.concatenate([dmod_g[:, 0:2 * depth:2].reshape(N_DEV, depth, 3 * d),
                                dmod_g[:, 1:2 * depth:2].reshape(N_DEV, depth, 3 * d)], axis=0)
    bm_parts = jnp.concatenate([bm_parts, jnp.zeros((2 * N_DEV, 8 - depth, 3 * d), F32)], axis=1)
    pad8 = lambda a: _pad_rows(a, 8)
    bm_out = _sum_adamw(bm_parts, pad8(b_mod), pad8(m_b_mod), pad8(v_b_mod), "adamw_b_mod")
    for kind, arr in zip(("grad", "delta", "m", "v"), bm_out):
        res[(kind, "b_mod")] = arr[:depth]

    conv_rows = small_g[:, depth + 4:depth + 4 + 3 * depth * s // d].reshape(N_DEV, depth * 3, s)
    conv_mine = lax.dynamic_slice(conv_rows, (0, 0, me * n_cw), (N_DEV, depth * 3, n_cw))
    conv_mine = jnp.concatenate([conv_mine, jnp.zeros((N_DEV, 8 - depth * 3, n_cw), F32)], axis=1)
    cw2 = lambda a: _pad_rows(a.reshape(depth * 3, n_cw), 8)
    cw_out = _sum_adamw(conv_mine, cw2(conv_w), cw2(m_conv_w), cw2(v_conv_w), "adamw_conv_w")
    for kind, arr in zip(("grad", "delta", "m", "v"), cw_out):
        res[(kind, "conv_w")] = arr[:depth * 3].reshape(depth, 3, n_cw)

    wm_out = _sum_adamw(jnp.stack(g_wmod, axis=0).reshape(1, depth * d, n_mod), w_mod.reshape(depth * d, n_mod),
                        m_w_mod.reshape(depth * d, n_mod), v_w_mod.reshape(depth * d, n_mod), "adamw_w_mod")
    for kind, arr in zip(("grad", "delta", "m", "v"), wm_out):
        res[(kind, "w_mod")] = arr.reshape(depth, d, n_mod)

    wi_out = wo_out = None
    after = wm_out[0]
    for l in reversed(range(depth)):
        win_parts, wout_parts = _push_wait(*dwin[l], "scatter", after, f"grads_wait_l{l}")
        wi_out = _sum_adamw(win_parts, w_in.reshape(depth * d, s), m_w_in.reshape(depth * d, s),
                            v_w_in.reshape(depth * d, s), f"adamw_w_in_l{l}", row0=l * d, into=wi_out)
        wo_out = _sum_adamw(wout_parts, w_out.reshape(depth * r_out, d), m_w_out.reshape(depth * r_out, d),
                            v_w_out.reshape(depth * r_out, d), f"adamw_w_out_l{l}", row0=l * r_out, into=wo_out)
        after = wo_out[0]
    for kind, arr in zip(("grad", "delta", "m", "v"), wi_out):
        res[(kind, "w_in")] = arr.reshape(depth, d, s)
    for kind, arr in zip(("grad", "delta", "m", "v"), wo_out):
        res[(kind, "w_out")] = arr.reshape(depth, r_out, d)

    order = ["c_ctx", "norm_w", "w_mod", "b_mod", "w_in", "conv_w", "conv_norm_w", "ret_norm_w", "ret_decay_f",
             "ret_decay_b", "w_out", "final_norm_w"]
    outs = [loss, grad_x]
    for kind in ("grad", "delta", "m", "v"):
        outs += [res[(kind, k_)] for k_ in order]
    return tuple(outs)
```

```python
import functools

import jax
import jax.numpy as jnp
from jax import lax
from jax.experimental import pallas as pl
from jax.experimental.pallas import tpu as pltpu

F32 = jnp.float32
BF16 = jnp.bfloat16

EPS = 1e-6
CHUNK = 128
HEAD_DIM = 128
GRID_W = 64
ROPE_BASE = 10000.0
N_DEV = 8
ADAM_LR, ADAM_B1, ADAM_B2, ADAM_EPS, ADAM_WD, ADAM_STEP = 0.001, 0.9, 0.999, 1e-08, 0.01, 10

ROW_TILE = 256
V7X_VMEM_LIMIT = 56 * 1024 * 1024
MESH_AXES = ("x", "y", "c")

NN = ((1,), (0,))
NT = ((1,), (1,))
TN = ((0,), (0,))


def _dot(a, b, dims):
    return lax.dot_general(a, b, (dims, ((), ())), preferred_element_type=F32)


def _params(sem=None):
    if sem is None:
        return pltpu.CompilerParams(vmem_limit_bytes=V7X_VMEM_LIMIT)
    return pltpu.CompilerParams(dimension_semantics=sem, vmem_limit_bytes=V7X_VMEM_LIMIT)


def _silu(z):
    return z * jax.nn.sigmoid(z)


def _dsilu(z):
    s = jax.nn.sigmoid(z)
    return s * (1.0 + z * (1.0 - s))


def _silu_and_slope(z):
    s = jax.nn.sigmoid(z)
    return z * s, s * (1.0 + z * (1.0 - s))


def _sum_all(a):
    return jnp.sum(jnp.sum(a, axis=1, keepdims=True), axis=0, keepdims=True)


def _mm_rows(t):
    return 768 if t % 768 == 0 else ROW_TILE


def _rows_or(t, rows):
    return rows if t % rows == 0 else _mm_rows(t)


def _tiles(layer, t, d):
    return dict(in_tm=_rows_or(t, 1408), bwd_tm=_mm_rows(t), bwd_gs=2, wg_bm=d, wg_bt=_mm_rows(t),
                out_tm=_mm_rows(t), out_tn=min(d, 1024), wo_bm=d, ob_tn=d)


def _full(shape):
    n = len(shape)
    return pl.BlockSpec(shape, lambda *_: (0,) * n)


def _peers(x, y, c):
    return [(x, y, 1 - c), (1 - x, y, c), (x, 1 - y, c), (1 - x, 1 - y, c),
            (1 - x, y, 1 - c), (x, 1 - y, 1 - c), (1 - x, 1 - y, 1 - c)]


def _lin(p):
    return 4 * p[0] + 2 * p[1] + p[2]


def _all_gather(arrays, name, in_vmem):
    n_arr = len(arrays)
    space = pltpu.VMEM if in_vmem else pl.ANY

    def body(*refs):
        ins, outs = refs[:n_arr], refs[n_arr:2 * n_arr]
        send_sems, recv_sems, local_sems = refs[2 * n_arr:]
        x, y, c = lax.axis_index("x"), lax.axis_index("y"), lax.axis_index("c")
        me, sibling = (x, y, c), (x, y, 1 - c)
        chips = [(1 - x, y), (x, 1 - y), (1 - x, 1 - y)]
        every = []
        locals_ = []
        for a in range(n_arr):
            m_per = ins[a].shape[0]
            out_ref = outs[a]

            def rows(p, out_ref=out_ref, m_per=m_per):
                return out_ref.at[pl.ds(_lin(p) * m_per, m_per), :]

            def copy(k, block, to, src=None, a=a, rows=rows):
                return pltpu.make_async_remote_copy(
                    src_ref=rows(block) if src is None else src, dst_ref=rows(block),
                    send_sem=send_sems.at[a, k], recv_sem=recv_sems.at[a, k],
                    device_id=to, device_id_type=pl.DeviceIdType.MESH)

            mine = pltpu.make_async_copy(ins[a], rows(me), local_sems.at[a])
            mine.start()
            locals_.append(mine)
            first = [copy(0, me, sibling, src=ins[a])]
            first += [copy(1 + j, me, (*chip, c), src=ins[a]) for j, chip in enumerate(chips)]
            for cp in first:
                cp.start()
            every.append((copy, first))
        sends = []
        for a in range(n_arr):
            copy, first = every[a]
            passed = [copy(4 + j, (*chip, c), sibling) for j, chip in enumerate(chips)]
            for j, chip in enumerate(chips):
                copy(1 + j, (*chip, c), me).wait_recv()
                passed[j].start()
            sends += first + passed
        for a in range(n_arr):
            copy, _ = every[a]
            copy(0, sibling, me).wait_recv()
            for j, chip in enumerate(chips):
                copy(4 + j, (*chip, 1 - c), me).wait_recv()
        for cp in sends:
            cp.wait_send()
        for mine in locals_:
            mine.wait()

    outs = pl.pallas_call(
        body, name=name,
        out_shape=[jax.ShapeDtypeStruct((N_DEV * a.shape[0], a.shape[1]), a.dtype) for a in arrays],
        in_specs=[pl.BlockSpec(memory_space=space)] * n_arr,
        out_specs=[pl.BlockSpec(memory_space=space)] * n_arr,
        scratch_shapes=[pltpu.SemaphoreType.DMA((n_arr, 7)), pltpu.SemaphoreType.DMA((n_arr, 7)),
                        pltpu.SemaphoreType.DMA((n_arr,))],
        compiler_params=_params(),
    )(*arrays)
    return list(outs)


_HBM = pl.BlockSpec(memory_space=pltpu.HBM)
_SEM = pl.BlockSpec(memory_space=pltpu.SEMAPHORE)
_DATAFLOW = pltpu.SideEffectType.DATAFLOW_SIDE_EFFECTING


PUSH_COPIES = {"scatter": 7, "gather": 7, "near": 4, "relay": 3}


def _push_copies(src_refs, land_refs, send_sems, recv_sems, mode):
    x, y, c = lax.axis_index("x"), lax.axis_index("y"), lax.axis_index("c")
    me, sibling = (x, y, c), (x, y, 1 - c)
    n_k = PUSH_COPIES[mode]
    out, back = [], []
    if mode == "relay":
        for k, chip in enumerate([(1 - x, y), (x, 1 - y), (1 - x, 1 - y)]):
            for a, land in enumerate(land_refs):
                sems = dict(send_sem=send_sems.at[n_k * a + k], recv_sem=recv_sems.at[n_k * a + k],
                            device_id=sibling, device_id_type=pl.DeviceIdType.MESH)
                mine = land.at[_lin((*chip, c))]
                out.append(pltpu.make_async_remote_copy(src_ref=mine, dst_ref=mine, **sems))
                back.append(pltpu.make_async_remote_copy(src_ref=mine, dst_ref=land.at[_lin((*chip, 1 - c))], **sems))
        return out, back
    for k, peer in enumerate(_peers(x, y, c)[:n_k]):
        for a, (src, land) in enumerate(zip(src_refs, land_refs)):
            sems = dict(send_sem=send_sems.at[n_k * a + k], recv_sem=recv_sems.at[n_k * a + k],
                        device_id=peer, device_id_type=pl.DeviceIdType.MESH)
            mine = src.at[_lin(peer)] if mode == "scatter" else src
            out.append(pltpu.make_async_remote_copy(src_ref=mine, dst_ref=land.at[_lin(me)], **sems))
            back.append(pltpu.make_async_remote_copy(src_ref=mine, dst_ref=land.at[_lin(peer)], **sems))
    return out, back


def _push_start(srcs, lands, mode, name, after=()):
    n_src, n = len(srcs), len(lands)
    n_buf = n_src + n
    n_in = n_buf + len(after)
    n_sem = PUSH_COPIES[mode] * n

    def body(*refs):
        send_sems, recv_sems = refs[n_in], refs[n_in + 1]
        out, _ = _push_copies(refs[:n_src], refs[n_src:n_buf], send_sems, recv_sems, mode)
        for cp in out:
            cp.start()
        token = refs[-1]
        token[...] = jnp.zeros_like(token)

    both = list(srcs) + list(lands)
    res = pl.pallas_call(
        body, name=name,
        out_shape=[pltpu.SemaphoreType.DMA((n_sem,)), pltpu.SemaphoreType.DMA((n_sem,))]
        + [pltpu.HBM(a.shape, a.dtype) for a in both] + [jax.ShapeDtypeStruct((8, 128), F32)],
        in_specs=[_HBM] * n_buf + [pl.BlockSpec(memory_space=pl.ANY)] * len(after),
        out_specs=[_SEM, _SEM] + [_HBM] * n_buf + [pl.BlockSpec(memory_space=pltpu.VMEM)],
        input_output_aliases={i: 2 + i for i in range(n_buf)},
        compiler_params=pltpu.CompilerParams(has_side_effects=_DATAFLOW),
    )(*[pltpu.with_memory_space_constraint(a, pltpu.HBM) for a in both], *after)
    return res[0], res[1], list(res[2:2 + n_src]), list(res[2 + n_src:2 + n_buf]), res[-1]


def _push_wait(send_sems, recv_sems, srcs, lands, mode, after, name):
    n_src, n = len(srcs), len(lands)
    n_buf = n_src + n

    def body(*refs):
        out, back = _push_copies(refs[:n_src], refs[n_src:n_buf], refs[n_buf], refs[n_buf + 1], mode)
        for cp in out:
            cp.wait_send()
        for cp in back:
            cp.wait_recv()

    both = list(srcs) + list(lands)
    res = pl.pallas_call(
        body, name=name,
        out_shape=[pltpu.HBM(a.shape, a.dtype) for a in both],
        in_specs=[_HBM] * n_buf + [_SEM, _SEM, pl.BlockSpec(memory_space=pl.ANY)],
        out_specs=[_HBM] * n_buf,
        input_output_aliases={i: i for i in range(n_buf)},
        compiler_params=pltpu.CompilerParams(has_side_effects=_DATAFLOW),
    )(*both, send_sems, recv_sems, after)
    return list(res[n_src:])


def _landing(own, me):
    zone = lax.empty((N_DEV,) + own.shape, own.dtype)
    return lax.dynamic_update_slice(zone, own[None], (me,) + (0,) * own.ndim)


def _mod_rows(c9, w_mod, b_sh, name):
    n = w_mod.shape[1]

    def body(c_ref, w_ref, b_ref, o_ref):
        s9 = _silu(c_ref[...]).astype(BF16)
        o_ref[...] = _dot(s9, w_ref[...].astype(BF16), NN) + b_ref[...]

    return pl.pallas_call(body, name=name, out_shape=jax.ShapeDtypeStruct((16, n), F32),
                          compiler_params=_params())(c9, w_mod, b_sh)


def _mod_grads(dm_rows, dc_rows, c9, w_mod, name):
    d, n = w_mod.shape

    def body(dm_ref, dc_ref, c_ref, w_ref, gw_ref, dc_out):
        dc = dc_ref[...]
        tot = dc[0:1]
        for j in range(1, N_DEV):
            tot = tot + dc[j:j + 1]
        row = lax.broadcasted_iota(jnp.int32, (8, n), 0)
        lower = jnp.where(row == 0, tot, 0.0)
        dmod9 = jnp.concatenate([dm_ref[...], lower], axis=0).astype(BF16)
        c9v = c_ref[...]
        s9 = _silu(c9v).astype(BF16)
        gw_ref[...] = _dot(s9, dmod9, TN)
        ds = _dot(lower.astype(BF16), w_ref[...].astype(BF16), NT)
        dc_out[...] = ds * _dsilu(c9v[8:16])

    return pl.pallas_call(body, name=name,
                          out_shape=[jax.ShapeDtypeStruct((d, n), F32), jax.ShapeDtypeStruct((8, d), F32)],
                          compiler_params=_params())(dm_rows, dc_rows, c9, w_mod)


def _decay_tables(dec, n_heads, name):
    c = CHUNK

    def body(dec_ref, dc_ref, dlf_ref, dlb_ref, qf_ref, kf_ref, qb_ref, kb_ref, cdf_ref, cdb_ref, lg_ref):
        h = pl.program_id(0)
        d = dec_ref[...]
        lane = lax.broadcasted_iota(jnp.int32, d.shape, 1)
        lg = -jnp.exp(jnp.sum(jnp.where(lane == h, d, 0.0), axis=1, keepdims=True))
        lgf, lgb = lg[0:1], lg[1:2]
        i = lax.broadcasted_iota(jnp.int32, (c, c), 0).astype(F32)
        j = lax.broadcasted_iota(jnp.int32, (c, c), 1).astype(F32)
        diff = i - j
        d_f = jnp.where(diff >= 0, jnp.exp(lgf * jnp.maximum(diff, 0.0)), 0.0)
        d_b = jnp.where(diff <= 0, jnp.exp(lgb * jnp.maximum(-diff, 0.0)), 0.0)
        dc_ref[...] = d_f + d_b
        dlf_ref[...] = diff * d_f
        dlb_ref[...] = -diff * d_b
        pos = lax.broadcasted_iota(jnp.int32, (c, HEAD_DIM), 0).astype(F32)
        qf_ref[...] = jnp.exp(lgf * (pos + 1.0))
        kf_ref[...] = jnp.exp(lgf * (c - 1.0 - pos))
        qb_ref[...] = jnp.exp(lgb * (c - pos))
        kb_ref[...] = jnp.exp(lgb * pos)
        ones = jnp.ones((8, HEAD_DIM), F32)
        cdf_ref[...] = jnp.exp(lgf * float(c)) * ones
        cdb_ref[...] = jnp.exp(lgb * float(c)) * ones

        @pl.when(h == 0)
        def _():
            lg_ref[...] = jnp.zeros_like(lg_ref)

        row8 = lax.broadcasted_iota(jnp.int32, (8, HEAD_DIM), 0)
        lane8 = lax.broadcasted_iota(jnp.int32, (8, HEAD_DIM), 1)
        lg_ref[...] += (jnp.where((row8 == 0) & (lane8 == h), lgf, 0.0)
                        + jnp.where((row8 == 1) & (lane8 == h), lgb, 0.0))

    def per_head(*tail):
        return pl.BlockSpec((None,) + tail, lambda h: (h,) + (0,) * len(tail))

    shapes = [(c, c)] * 3 + [(c, HEAD_DIM)] * 4 + [(8, HEAD_DIM)] * 2
    return pl.pallas_call(
        body, name=name, grid=(n_heads,),
        in_specs=[_full(dec.shape)],
        out_specs=[per_head(*s) for s in shapes] + [_full((8, HEAD_DIM))],
        out_shape=[jax.ShapeDtypeStruct((n_heads,) + s, F32) for s in shapes]
        + [jax.ShapeDtypeStruct((8, HEAD_DIM), F32)],
        compiler_params=_params(("arbitrary",)),
    )(dec)


def _modulate(x, nw, shift, scale):
    r = lax.rsqrt(jnp.mean(x * x, axis=-1, keepdims=True) + EPS)
    return ((x * r) * nw * (1.0 + scale) + shift).astype(BF16)


def _prenorm(xt, nw, mod, n_lat, name):
    t, d = xt.shape
    nxb = n_lat // ROW_TILE

    def body(x_ref, nw_ref, mod_ref, o_ref):
        ctx = pl.program_id(0) >= nxb
        m = mod_ref[...]
        o_ref[...] = _modulate(x_ref[...], nw_ref[...], jnp.where(ctx, m[3:4], m[0:1]), jnp.where(ctx, m[4:5], m[1:2]))

    row = pl.BlockSpec((ROW_TILE, d), lambda i: (i, 0))
    return pl.pallas_call(body, name=name, grid=(t // ROW_TILE,),
                          in_specs=[row, _full((1, d)), _full((8, d))],
                          out_specs=row, out_shape=jax.ShapeDtypeStruct((t, d), BF16),
                          compiler_params=_params(("parallel",)))(xt, nw, mod)


def _prenorm_first(x, ctx, nw, mod, name, after=()):
    n_lat, d = x.shape
    t = n_lat + ctx.shape[0]
    nxb = n_lat // ROW_TILE

    def body(x_ref, c_ref, nw_ref, mod_ref, *rest):
        o_ref, xt_ref = rest[-2:]
        m = mod_ref[...]
        nw_v = nw_ref[...]

        @pl.when(pl.program_id(0) < nxb)
        def _():
            xv = x_ref[...]
            xt_ref[...] = xv
            o_ref[...] = _modulate(xv, nw_v, m[0:1], m[1:2])

        @pl.when(pl.program_id(0) >= nxb)
        def _():
            xv = c_ref[...]
            xt_ref[...] = xv
            o_ref[...] = _modulate(xv, nw_v, m[3:4], m[4:5])

    row = pl.BlockSpec((ROW_TILE, d), lambda i: (i, 0))
    return pl.pallas_call(
        body, name=name, grid=(t // ROW_TILE,),
        in_specs=[pl.BlockSpec((ROW_TILE, d), lambda i: (jnp.minimum(i, nxb - 1), 0)),
                  pl.BlockSpec((ROW_TILE, d), lambda i: (jnp.maximum(i - nxb, 0), 0)), _full((1, d)), _full((8, d))]
        + [pl.BlockSpec(memory_space=pl.ANY)] * len(after),
        out_specs=[row, row], out_shape=[jax.ShapeDtypeStruct((t, d), BF16), jax.ShapeDtypeStruct((t, d), F32)],
        compiler_params=_params(("parallel",)))(x, ctx, nw, mod, *after)


def _rope_fwd(v, cos, sa, sb):
    return v * cos + pltpu.roll(v, 96, 1) * sa + pltpu.roll(v, 32, 1) * sb


def _rope_bwd(g, cos, sa, sb):
    return g * cos + pltpu.roll(g * sa, 32, 1) + pltpu.roll(g * sb, 96, 1)


N_PLAIN = 5
U_DTYPE = BF16


def _in_proj(hx, wg, cos, sa, sb, s, part, tm, name, after=(), into=None):
    t, d = hx.shape
    n_seg, _, n = wg.shape
    nb = t // tm
    k_scale = HEAD_DIM ** -0.5
    kept = [] if into is None else list(into)

    def body(a_ref, w_ref, cos_ref, sa_ref, sb_ref, *rest):
        u_ref, qkv_ref = rest[-2:]
        g = pl.program_id(1)
        acc = _dot(a_ref[...], w_ref[...], NN)

        @pl.when(g < N_PLAIN)
        def _():
            u_ref[...] = acc.astype(U_DTYPE)

        @pl.when(g == N_PLAIN + 2)
        def _():
            qkv_ref[...] = acc.astype(BF16)

        for which, scale in ((N_PLAIN, 1.0), (N_PLAIN + 1, k_scale)):
            @pl.when(g == which)
            def _(scale=scale):
                co, a, b = cos_ref[...], sa_ref[...], sb_ref[...]
                for h in range(n // HEAD_DIM):
                    sl = slice(h * HEAD_DIM, (h + 1) * HEAD_DIM)
                    qkv_ref[:, sl] = (_rope_fwd(acc[:, sl], co, a, b) * scale).astype(BF16)

    def w_seg(g):
        return jnp.where(g < N_PLAIN - 1, g, jnp.where(g == N_PLAIN - 1, n_seg - 1, g - 1))

    def qkv_at(i, g):
        held = (jnp.where(i == 0, 0, 2), jnp.maximum(i - 1, 0))
        return (jnp.where(g < N_PLAIN, held[0], g - N_PLAIN), jnp.where(g < N_PLAIN, held[1], i), part)

    tab = pl.BlockSpec((tm, HEAD_DIM), lambda i, g: (i, 0))
    hbm = pl.BlockSpec(memory_space=pl.ANY)
    return pl.pallas_call(
        body, name=name, grid=(nb, n_seg),
        in_specs=[pl.BlockSpec((tm, d), lambda i, g: (i, 0)), pl.BlockSpec((None, d, n), lambda i, g: (w_seg(g), 0, 0)),
                  tab, tab, tab] + [hbm] * (len(after) + len(kept)),
        out_specs=[pl.BlockSpec((None, tm, n), lambda i, g: (jnp.minimum(g, N_PLAIN - 1), i, part)),
                   pl.BlockSpec((None, tm, n), qkv_at)],
        out_shape=[jax.ShapeDtypeStruct((N_PLAIN, t, s), U_DTYPE), jax.ShapeDtypeStruct((3, t, s), BF16)],
        input_output_aliases={5 + len(after) + j: j for j in range(len(kept))},
        compiler_params=_params(("arbitrary", "arbitrary")))(hx, wg, cos, sa, sb, *after, *kept)


def _pair_sweep(xs, ys, tab_f, tab_b, cdf, cdb, n_heads, nx, ncc, reverse, name):
    t, s = xs[0].shape[-2:]
    nc = nx + ncc
    c = CHUNK
    n_pair = nc // 2
    assert nx % 2 == 0 and ncc % 2 == 0

    def f_pair(i):
        step = n_pair - 1 - i if reverse else i
        return jnp.where(step < ncc // 2, nx // 2 + step, step - ncc // 2)

    def b_pair(i):
        return i if reverse else n_pair - 1 - i

    f_subs = (1, 0) if reverse else (0, 1)
    b_subs = (0, 1) if reverse else (1, 0)

    def body(xf_ref, yf_ref, xb_ref, yb_ref, tf, tb, cdf_ref, cdb_ref, sf_out, sb_out, sf, sb):
        @pl.when(pl.program_id(0) == 0)
        def _():
            sf[...] = jnp.zeros_like(sf)
            sb[...] = jnp.zeros_like(sb)

        for step in range(2):
            for x_ref, y_ref, tab, cd, out, st, sub in ((xf_ref, yf_ref, tf, cdf_ref, sf_out, sf, f_subs[step]),
                                                        (xb_ref, yb_ref, tb, cdb_ref, sb_out, sb, b_subs[step])):
                rows = pl.ds(sub * c, c)
                for h in range(n_heads):
                    sl = pl.ds(h * HEAD_DIM, HEAD_DIM)
                    out[sub, h] = st[h].astype(BF16)
                    xd = (x_ref[rows, sl].astype(F32) * tab[h]).astype(BF16)
                    st[h] = cd[h][0:1, :] * st[h] + _dot(xd, y_ref[rows, sl], TN)

    def spec(arr, pair):
        lead = arr[1]
        if lead is None:
            return pl.BlockSpec((2 * c, s), lambda i: (pair(i), 0))
        return pl.BlockSpec((None, 2 * c, s), lambda i: (lead, pair(i), 0))

    st_blk = (2, n_heads, HEAD_DIM, HEAD_DIM)
    return pl.pallas_call(
        body, name=name, grid=(n_pair,),
        in_specs=[spec(xs, f_pair), spec(ys, f_pair), spec(xs, b_pair), spec(ys, b_pair),
                  _full((n_heads, c, HEAD_DIM)), _full((n_heads, c, HEAD_DIM)),
                  _full((n_heads, 8, HEAD_DIM)), _full((n_heads, 8, HEAD_DIM))],
        out_specs=[pl.BlockSpec(st_blk, lambda i: (f_pair(i), 0, 0, 0)), pl.BlockSpec(st_blk, lambda i: (b_pair(i), 0, 0, 0))],
        out_shape=[jax.ShapeDtypeStruct((nc, n_heads, HEAD_DIM, HEAD_DIM), BF16)] * 2,
        scratch_shapes=[pltpu.VMEM((n_heads, HEAD_DIM, HEAD_DIM), F32)] * 2,
        compiler_params=_params(("arbitrary",)),
    )(xs[0], ys[0], xs[0], ys[0], tab_f, tab_b, cdf, cdb)


def _state_sweep(qkv, tabs, n_heads, nx, ncc, name):
    return _pair_sweep((qkv, 1), (qkv, 2), tabs["kf"], tabs["kb"], tabs["cdf"], tabs["cdb"], n_heads, nx, ncc, False, name)


MIX_CHUNKS = 2
MIX_ROWS = MIX_CHUNKS * CHUNK


HALO = 16


def _halo_specs(s, t):
    per = MIX_ROWS // HALO
    n_halo = t // HALO

    def prev(g):
        return pl.BlockSpec((None, HALO, s), lambda i: (g, jnp.maximum(i * per - 1, 0), 0))

    def nxt(g):
        return pl.BlockSpec((None, HALO, s), lambda i: (g, jnp.minimum((i + 1) * per, n_halo - 1), 0))

    return prev, nxt


def _conv_input(h_ref, c_ref, hp_ref, hn_ref, cp_ref, cn_ref):
    a = c_ref[...].astype(F32) * h_ref[...].astype(F32)
    before = cp_ref[HALO - 1:HALO].astype(F32) * hp_ref[HALO - 1:HALO].astype(F32)
    after = cn_ref[0:1].astype(F32) * hn_ref[0:1].astype(F32)
    return a, before, after


def _shifted(a, before, after, has_prev, has_next):
    rows = a.shape[0]
    rowi = lax.broadcasted_iota(jnp.int32, a.shape, 0)
    am = jnp.where(rowi == 0, jnp.where(has_prev, before, 0.0), pltpu.roll(a, 1, 0))
    ap = jnp.where(rowi == rows - 1, jnp.where(has_next, after, 0.0), pltpu.roll(a, rows - 1, 0))
    return am, ap


def _neighbours(i, nx, nc):
    nxb, ncb = nx // MIX_CHUNKS, nc // MIX_CHUNKS
    return (i != 0) & (i != nxb), (i != nxb - 1) & (i != ncb - 1)


def _mix_fwd(u, qkv, sf, sb, tabs, conv_w, cnw, gnw, n_heads, nx, ncc, name):
    _, t, s = u.shape
    nc = nx + ncc
    c = CHUNK
    assert nx % MIX_CHUNKS == 0 and ncc % MIX_CHUNKS == 0

    def body(h_ref, b_ref, c_ref, z_ref, rz_ref, hp_ref, hn_ref, cp_ref, cn_ref, q_ref, k_ref, v_ref,
             sf_ref, sb_ref, dc_ref, qft, qbt, w_ref, cnw_ref, gnw_ref, y_ref, o_ref):
        i = pl.program_id(0)
        has_prev, has_next = _neighbours(i, nx, nc)
        a, before, after = _conv_input(h_ref, c_ref, hp_ref, hn_ref, cp_ref, cn_ref)
        am, ap = _shifted(a, before, after, has_prev, has_next)
        w = w_ref[...]
        y0 = w[0:1] * am + w[1:2] * a + w[2:3] * ap
        yb = b_ref[...].astype(F32) * y0
        r = lax.rsqrt(jnp.mean(yb * yb, axis=-1, keepdims=True) + EPS)
        y_ref[:, pl.ds(0, s)] = (_silu(z_ref[...].astype(F32)) * ((yb * r) * cnw_ref[...])).astype(BF16)
        for sub in range(MIX_CHUNKS):
            rows = pl.ds(sub * c, c)
            for h in range(n_heads):
                sl = pl.ds(h * HEAD_DIM, HEAD_DIM)
                q, k, v = q_ref[rows, sl], k_ref[rows, sl], v_ref[rows, sl]
                p = (_dot(q, k, NT) * dc_ref[h]).astype(BF16)
                o = _dot(p, v, NN)
                qf = q.astype(F32)
                o += _dot((qf * qft[h]).astype(BF16), sf_ref[sub, h], NN)
                o += _dot((qf * qbt[h]).astype(BF16), sb_ref[sub, h], NN)
                o_ref[rows, sl] = o
                mu = jnp.mean(o, axis=-1, keepdims=True)
                var = jnp.mean(jnp.square(o - mu), axis=-1, keepdims=True)
                on = (o - mu) * lax.rsqrt(var + EPS)
                y_ref[rows, pl.ds(s + h * HEAD_DIM, HEAD_DIM)] = (
                    _silu(rz_ref[rows, sl].astype(F32)) * (on * gnw_ref[:, sl])).astype(BF16)

    def seg(g):
        return pl.BlockSpec((None, MIX_ROWS, s), lambda i: (g, i, 0))

    prev, nxt = _halo_specs(s, t)
    row = pl.BlockSpec((MIX_ROWS, s), lambda i: (i, 0))
    st = pl.BlockSpec((MIX_CHUNKS, n_heads, HEAD_DIM, HEAD_DIM), lambda i: (i, 0, 0, 0))
    return pl.pallas_call(
        body, name=name, grid=(nc // MIX_CHUNKS,),
        in_specs=[seg(0), seg(1), seg(2), seg(3), seg(4), prev(0), nxt(0), prev(2), nxt(2), seg(0), seg(1), seg(2),
                  st, st, _full((n_heads, c, c)), _full((n_heads, c, HEAD_DIM)), _full((n_heads, c, HEAD_DIM)),
                  _full((3, s)), _full((1, s)), _full((1, s))],
        out_specs=[pl.BlockSpec((MIX_ROWS, 2 * s), lambda i: (i, 0)), row],
        out_shape=[jax.ShapeDtypeStruct((t, 2 * s), BF16), jax.ShapeDtypeStruct((t, s), F32)],
        compiler_params=_params(("parallel",)),
    )(u, u, u, u, u, u, u, u, u, qkv, qkv, qkv, sf, sb, tabs["dc"], tabs["qf"], tabs["qb"], conv_w, cnw, gnw)


def _row_gate(mod_ref, row0, rows, n_lat, col):
    rowi = row0 + lax.broadcasted_iota(jnp.int32, (rows, 1), 0)
    return jnp.where(rowi >= n_lat, mod_ref[5:6, col], mod_ref[2:3, col])


def _out_proj(ycat, w_out, xt, mod, n_lat, tm, tn, name):
    t, d = xt.shape

    def body(a_ref, w_ref, x_ref, mod_ref, m_ref, xo_ref):
        m = _dot(a_ref[...], w_ref[...], NN)
        m_ref[...] = m.astype(BF16)
        gate = _row_gate(mod_ref, pl.program_id(1) * tm, tm, n_lat, slice(None))
        xo_ref[...] = x_ref[...] + gate * m

    blk = pl.BlockSpec((tm, tn), lambda j, i: (i, j))
    return pl.pallas_call(
        body, name=name, grid=(d // tn, t // tm),
        in_specs=[pl.BlockSpec((tm, d), lambda j, i: (i, 0)), pl.BlockSpec((d, tn), lambda j, i: (0, j)), blk,
                  pl.BlockSpec((8, tn), lambda j, i: (0, j))],
        out_specs=[blk, blk], out_shape=[jax.ShapeDtypeStruct((t, d), BF16), jax.ShapeDtypeStruct((t, d), F32)],
        compiler_params=_params(("parallel", "parallel")))(ycat, w_out, xt, mod)


def _out_proj_loss(ycat, w_out, xt, mod, tgt, fnw, n_lat, name):
    t, d = xt.shape
    nb = t // ROW_TILE
    nxb = n_lat // ROW_TILE

    def body(a_ref, w_ref, x_ref, mod_ref, t_ref, fw_ref, dx_ref, dm_ref, loss_ref, dw_ref, gacc_ref, xs, ms):
        i = pl.program_id(0)

        @pl.when(i == 0)
        def _():
            xs[...] = jnp.zeros_like(xs)
            ms[...] = jnp.zeros_like(ms)
            loss_ref[...] = jnp.zeros_like(loss_ref)
            dw_ref[...] = jnp.zeros_like(dw_ref)
            gacc_ref[...] = jnp.zeros_like(gacc_ref)

        def step(cur, prev):
            mv = mod_ref[...]
            x_prev, m_prev = xs[prev], ms[prev]
            valid = (i >= 1) & (i - 1 < nxb)
            w = fw_ref[...]
            r = lax.rsqrt(jnp.mean(x_prev * x_prev, axis=-1, keepdims=True) + EPS)
            xn = x_prev * r
            e = xn * w - t_ref[...]
            loss = 0.5 * jnp.sum(jnp.mean(e * e, axis=-1, keepdims=True), axis=0, keepdims=True)
            loss_ref[...] += jnp.where(valid, loss, 0.0)
            dy = e * (1.0 / d)
            dw_ref[0:1, :] += jnp.where(valid, jnp.sum(dy * xn, axis=0, keepdims=True), 0.0)
            dxn = dy * w
            dx = jnp.where(valid, r * (dxn - xn * jnp.mean(dxn * xn, axis=-1, keepdims=True)), 0.0)
            dx_ref[...] = dx
            dm_ref[...] = (dx * mv[2:3]).astype(BF16)
            gacc_ref[2:3, :] += jnp.sum(dx * m_prev, axis=0, keepdims=True)

            m = _dot(a_ref[...], w_ref[...], NN)
            gate = jnp.where(jnp.minimum(i, nb - 1) >= nxb, mv[5:6], mv[2:3])
            xs[cur] = x_ref[...] + gate * m
            ms[cur] = m

        @pl.when(i % 2 == 0)
        def _():
            step(0, 1)

        @pl.when(i % 2 == 1)
        def _():
            step(1, 0)

    cur = pl.BlockSpec((ROW_TILE, d), lambda i: (jnp.minimum(i, nb - 1), 0))
    prev = pl.BlockSpec((ROW_TILE, d), lambda i: (jnp.maximum(i - 1, 0), 0))
    return pl.pallas_call(
        body, name=name, grid=(nb + 1,),
        in_specs=[cur, _full((d, d)), cur, _full((8, d)),
                  pl.BlockSpec((ROW_TILE, d), lambda i: (jnp.clip(i - 1, 0, nxb - 1), 0)), _full((1, d))],
        out_specs=[prev, prev, _full((8, HEAD_DIM)), _full((8, d)), _full((8, d))],
        out_shape=[jax.ShapeDtypeStruct((t, d), F32), jax.ShapeDtypeStruct((t, d), BF16),
                   jax.ShapeDtypeStruct((8, HEAD_DIM), F32), jax.ShapeDtypeStruct((8, d), F32),
                   jax.ShapeDtypeStruct((8, d), F32)],
        scratch_shapes=[pltpu.VMEM((2, ROW_TILE, d), F32), pltpu.VMEM((2, ROW_TILE, d), F32)],
        compiler_params=_params(("arbitrary",)))(ycat, w_out, xt, mod, tgt, fnw)


def _matmul_nt(a, w, tn, name, after=()):
    t, k = a.shape
    n = w.shape[0]
    tm = _mm_rows(t)

    def body(a_ref, w_ref, *rest):
        rest[-1][...] = _dot(a_ref[...], w_ref[...], NT)

    return pl.pallas_call(
        body, name=name, grid=(n // tn, t // tm),
        in_specs=[pl.BlockSpec((tm, k), lambda j, i: (i, 0)), pl.BlockSpec((tn, k), lambda j, i: (j, 0))]
        + [pl.BlockSpec(memory_space=pl.ANY)] * len(after),
        out_specs=pl.BlockSpec((tm, tn), lambda j, i: (i, j)),
        out_shape=jax.ShapeDtypeStruct((t, n), F32),
        compiler_params=_params(("parallel", "parallel")))(a, w, *after)


def _weight_grad(a, b, bm, bt, name):
    t, m = a.shape
    n_g, _, n = b.shape
    nt = t // bt

    def body(a_ref, b_ref, o_ref, acc):
        k = pl.program_id(2)

        @pl.when(k == 0)
        def _():
            acc[...] = jnp.zeros_like(acc)

        acc[...] += _dot(a_ref[...], b_ref[...], TN)

        @pl.when(k == nt - 1)
        def _():
            o_ref[...] = acc[...].astype(o_ref.dtype)

    return pl.pallas_call(
        body, name=name, grid=(n_g, m // bm, nt),
        in_specs=[pl.BlockSpec((bt, bm), lambda g, i, k: (k, i)), pl.BlockSpec((None, bt, n), lambda g, i, k: (g, k, 0))],
        out_specs=pl.BlockSpec((None, bm, n), lambda g, i, k: (g, i, 0)),
        out_shape=jax.ShapeDtypeStruct((n_g, m, n), BF16),
        scratch_shapes=[pltpu.VMEM((bm, n), F32)],
        compiler_params=_params(("parallel", "parallel", "arbitrary")))(a, b)


def _weight_grad_beside_prenorm_bwd(a, b, dhx, xt, dxo, nw, mod, below, n_lat, name):
    t, m = a.shape
    n_g, _, n = b.shape
    d = xt.shape[1]
    bt = _mm_rows(t)
    nt = t // bt
    rows = t // (n_g * nt)
    n_piece = 2 if rows % 32 == 0 and m % 2 == 0 else 1
    rows_p, m_p = rows // n_piece, m // n_piece
    assert rows * n_g * nt == t and rows_p % 8 == 0

    def body(a_ref, b_ref, dh_ref, x_ref, dxo_ref, nw_ref, mod_ref, m_ref, modb_ref,
             o_ref, dx_ref, acc_ref, dm_ref, gacc_ref, acc):
        g, k = pl.program_id(0), pl.program_id(1)
        step = g * nt + k

        @pl.when(step == 0)
        def _():
            acc_ref[...] = jnp.zeros_like(acc_ref)
            gacc_ref[...] = jnp.zeros_like(gacc_ref)

        @pl.when(k == 0)
        def _():
            acc[...] = jnp.zeros_like(acc)

        mv, mb, nw_v = mod_ref[...], modb_ref[...], nw_ref[...]
        for p in range(n_piece):
            rs = pl.ds(p * rows_p, rows_p)
            rowi = step * rows + p * rows_p + lax.broadcasted_iota(jnp.int32, (rows_p, 1), 0)
            ctx = rowi >= n_lat
            w_lat = jnp.where(ctx, 0.0, 1.0)
            w_ctx = 1.0 - w_lat
            scale1 = 1.0 + jnp.where(ctx, mv[4:5], mv[1:2])
            x = x_ref[rs, :]
            r = lax.rsqrt(jnp.mean(x * x, axis=-1, keepdims=True) + EPS)
            xn = x * r
            dh = dh_ref[rs, :]
            dsc = dh * (xn * nw_v)
            acc_ref[0:1, :] += jnp.sum(dh * w_lat, axis=0, keepdims=True)
            acc_ref[1:2, :] += jnp.sum(dsc * w_lat, axis=0, keepdims=True)
            acc_ref[3:4, :] += jnp.sum(dh * w_ctx, axis=0, keepdims=True)
            acc_ref[4:5, :] += jnp.sum(dsc * w_ctx, axis=0, keepdims=True)
            acc_ref[6:7, :] += jnp.sum(dh * scale1 * xn, axis=0, keepdims=True)
            dxn = dh * (nw_v * scale1)
            dx = dxo_ref[rs, :] + r * (dxn - xn * jnp.mean(dxn * xn, axis=-1, keepdims=True))
            dx_ref[rs, :] = dx
            dm_ref[rs, :] = (dx * jnp.where(ctx, mb[5:6], mb[2:3])).astype(BF16)
            dg = dx * m_ref[rs, :].astype(F32)
            gacc_ref[2:3, :] += jnp.sum(dg * w_lat, axis=0, keepdims=True)
            gacc_ref[5:6, :] += jnp.sum(dg * w_ctx, axis=0, keepdims=True)

            ms_ = pl.ds(p * m_p, m_p)
            acc[ms_, :] += _dot(a_ref[:, ms_], b_ref[...], TN)

        @pl.when(k == nt - 1)
        def _():
            o_ref[...] = acc[...].astype(o_ref.dtype)

    side = pl.BlockSpec((rows, d), lambda g, k: (g * nt + k, 0))
    acc8 = _full((8, d))
    return pl.pallas_call(
        body, name=name, grid=(n_g, nt),
        in_specs=[pl.BlockSpec((bt, m), lambda g, k: (k, 0)), pl.BlockSpec((None, bt, n), lambda g, k: (g, k, 0)),
                  side, side, side, _full((1, d)), acc8, side, acc8],
        out_specs=[pl.BlockSpec((None, m, n), lambda g, k: (g, 0, 0)), side, acc8, side, acc8],
        out_shape=[jax.ShapeDtypeStruct((n_g, m, n), BF16), jax.ShapeDtypeStruct((t, d), F32),
                   jax.ShapeDtypeStruct((8, d), F32), jax.ShapeDtypeStruct((t, d), BF16),
                   jax.ShapeDtypeStruct((8, d), F32)],
        scratch_shapes=[pltpu.VMEM((m, n), F32)],
        compiler_params=_params(("arbitrary", "arbitrary")))(a, b, dhx, xt, dxo, nw, mod, *below)


def _mix_bwd_a(dycat, u, o, conv_w, cnw, gnw, n_heads, nx, ncc, name):
    _, t, s = u.shape
    nc = nx + ncc

    def body(dy_ref, h_ref, b_ref, c_ref, z_ref, rz_ref, hp_ref, hn_ref, cp_ref, cn_ref, o_ref, w_ref,
             cnw_ref, gnw_ref, g_ref, dz_ref, db_ref, drz_ref, do_ref, acc_ref):
        i = pl.program_id(0)

        @pl.when(i == 0)
        def _():
            acc_ref[...] = jnp.zeros_like(acc_ref)

        has_prev, has_next = _neighbours(i, nx, nc)
        a, before, after = _conv_input(h_ref, c_ref, hp_ref, hn_ref, cp_ref, cn_ref)
        am, ap = _shifted(a, before, after, has_prev, has_next)
        w = w_ref[...]
        y0 = w[0:1] * am + w[1:2] * a + w[2:3] * ap
        bb = b_ref[...].astype(F32)
        yb = bb * y0
        r = lax.rsqrt(jnp.mean(yb * yb, axis=-1, keepdims=True) + EPS)
        ynn = yb * r
        z = z_ref[...].astype(F32)
        dyc = dy_ref[:, pl.ds(0, s)]
        cw = cnw_ref[...]
        sz, dsz = _silu_and_slope(z)
        dz_ref[...] = (dyc * (ynn * cw) * dsz).astype(BF16)
        dyn = dyc * sz
        acc_ref[0:1, :] += jnp.sum(dyn * ynn, axis=0, keepdims=True)
        dynn = dyn * cw
        dyb = r * (dynn - ynn * jnp.mean(dynn * ynn, axis=-1, keepdims=True))
        db_ref[...] = (dyb * y0).astype(BF16)
        g_ref[...] = dyb * bb
        for h in range(n_heads):
            sl = pl.ds(h * HEAD_DIM, HEAD_DIM)
            ov = o_ref[:, sl]
            mu = jnp.mean(ov, axis=-1, keepdims=True)
            var = jnp.mean(jnp.square(ov - mu), axis=-1, keepdims=True)
            rs = lax.rsqrt(var + EPS)
            on = (ov - mu) * rs
            dyr = dy_ref[:, pl.ds(s + h * HEAD_DIM, HEAD_DIM)]
            rz = rz_ref[:, sl].astype(F32)
            gw = gnw_ref[:, sl]
            srz, dsrz = _silu_and_slope(rz)
            drz_ref[:, sl] = (dyr * (on * gw) * dsrz).astype(BF16)
            dyg = dyr * srz
            acc_ref[1:2, sl] += jnp.sum(dyg * on, axis=0, keepdims=True)
            don = dyg * gw
            do = rs * (don - jnp.mean(don, axis=-1, keepdims=True)
                       - on * jnp.mean(don * on, axis=-1, keepdims=True))
            do_ref[:, sl] = do.astype(BF16)

    def seg(g):
        return pl.BlockSpec((None, MIX_ROWS, s), lambda i: (g, i, 0))

    prev, nxt = _halo_specs(s, t)
    row = pl.BlockSpec((MIX_ROWS, s), lambda i: (i, 0))
    return pl.pallas_call(
        body, name=name, grid=(nc // MIX_CHUNKS,),
        in_specs=[pl.BlockSpec((MIX_ROWS, 2 * s), lambda i: (i, 0)), seg(0), seg(1), seg(2), seg(3), seg(4),
                  prev(0), nxt(0), prev(2), nxt(2), row, _full((3, s)), _full((1, s)), _full((1, s))],
        out_specs=[row, row, row, row, row, _full((8, s))],
        out_shape=[jax.ShapeDtypeStruct((t, s), F32)] + [jax.ShapeDtypeStruct((t, s), BF16)] * 4
        + [jax.ShapeDtypeStruct((8, s), F32)],
        compiler_params=_params(("arbitrary",)),
    )(dycat, u, u, u, u, u, u, u, u, u, o, conv_w, cnw, gnw)


def _grad_state_sweep(qkv, do, tabs, n_heads, nx, ncc, name):
    return _pair_sweep((qkv, 0), (do, None), tabs["qf"], tabs["qb"], tabs["cdf"], tabs["cdb"], n_heads, nx, ncc, True, name)


def _mix_bwd_b(u, g, dz, db, drz, qkv, do, sf, sb, gf, gb, tabs, cos, sa, sb_tab, conv_w,
               n_heads, nx, ncc, name):
    _, t, s = u.shape
    nc = nx + ncc
    c = CHUNK
    k_scale = HEAD_DIM ** -0.5

    def body(h_ref, c_ref, g_ref, gp_ref, gn_ref, dz_ref, db_ref, drz_ref, q_ref, k_ref, v_ref, do_ref,
             sf_ref, sb_ref, gf_ref, gb_ref, dc_t, dlf_t, dlb_t, qft, kft, qbt, kbt, cdf, cdb, lg_ref,
             cos_ref, sa_ref, sb_ref2, w_ref, du_ref, dw_ref, dlg_ref):
        i = pl.program_id(0)

        @pl.when(i == 0)
        def _():
            dw_ref[...] = jnp.zeros_like(dw_ref)
            dlg_ref[...] = jnp.zeros_like(dlg_ref)

        has_prev, has_next = _neighbours(i, nx, nc)
        gv = g_ref[...]
        gm, gp = _shifted(gv, gp_ref[7:8], gn_ref[0:1], has_prev, has_next)
        w = w_ref[...]
        da = w[0:1] * gp + w[1:2] * gv + w[2:3] * gm
        hh, cc = h_ref[...].astype(F32), c_ref[...].astype(F32)
        du_ref[0] = (da * cc).astype(BF16)
        du_ref[2] = (da * hh).astype(BF16)
        a = cc * hh
        dw_ref[0:1, :] += jnp.sum(a * gp, axis=0, keepdims=True)
        dw_ref[1:2, :] += jnp.sum(a * gv, axis=0, keepdims=True)
        dw_ref[2:3, :] += jnp.sum(a * gm, axis=0, keepdims=True)
        du_ref[1] = db_ref[...]
        du_ref[3] = dz_ref[...]
        du_ref[7] = drz_ref[...]

        pos = lax.broadcasted_iota(jnp.int32, (c, HEAD_DIM), 0).astype(F32)
        w_q_f, w_q_b, w_k_f = pos + 1.0, c - pos, c - 1.0 - pos
        row8 = lax.broadcasted_iota(jnp.int32, (8, HEAD_DIM), 0)
        lane8 = lax.broadcasted_iota(jnp.int32, (8, HEAD_DIM), 1)
        dlg = jnp.zeros((8, HEAD_DIM), F32)
        for sub, h in [(sub, h) for sub in range(MIX_CHUNKS) for h in range(n_heads)]:
            rows = pl.ds(sub * c, c)
            co, ra, rb = cos_ref[rows, :], sa_ref[rows, :], sb_ref2[rows, :]
            sl = pl.ds(h * HEAD_DIM, HEAD_DIM)
            q, k, v, do = q_ref[rows, sl], k_ref[rows, sl], v_ref[rows, sl], do_ref[rows, sl]
            qf, kf, dof = q.astype(F32), k.astype(F32), do.astype(F32)
            s_f, s_b, g_f, g_b = sf_ref[sub, h], sb_ref[sub, h], gf_ref[sub, h], gb_ref[sub, h]
            p = _dot(q, k, NT)
            pd = _dot(do, v, NT)
            pdd = (pd * dc_t[h]).astype(BF16)
            dq = _dot(pdd, k, NN)
            dk = _dot(pdd, q, TN)
            dv = _dot((p * dc_t[h]).astype(BF16), do, TN)
            dq_f = _dot((dof * qft[h]).astype(BF16), s_f, NT)
            dq_b = _dot((dof * qbt[h]).astype(BF16), s_b, NT)
            dk_f = _dot(v, g_f, NT) * kft[h]
            dk_b = _dot(v, g_b, NT) * kbt[h]
            dv += _dot((kf * kft[h]).astype(BF16), g_f, NN) + _dot((kf * kbt[h]).astype(BF16), g_b, NN)
            ppd = p * pd
            cd_f, cd_b = cdf[h][0:1, :], cdb[h][0:1, :]
            t_f = _sum_all(dlf_t[h] * ppd + w_q_f * qf * dq_f + w_k_f * kf * dk_f
                           + float(c) * (cd_f * (g_f.astype(F32) * s_f.astype(F32))))
            t_b = _sum_all(dlb_t[h] * ppd + w_q_b * qf * dq_b + pos * kf * dk_b
                           + float(c) * (cd_b * (g_b.astype(F32) * s_b.astype(F32))))
            dlg += jnp.where((row8 == 0) & (lane8 == h), t_f, 0.0) + jnp.where((row8 == 1) & (lane8 == h), t_b, 0.0)
            du_ref[4, rows, sl] = _rope_bwd(dq + dq_f + dq_b, co, ra, rb).astype(BF16)
            du_ref[5, rows, sl] = (_rope_bwd(dk + dk_f + dk_b, co, ra, rb) * k_scale).astype(BF16)
            du_ref[6, rows, sl] = dv.astype(BF16)
        dlg_ref[...] += dlg

        @pl.when(i == nc // MIX_CHUNKS - 1)
        def _():
            dlg_ref[...] = dlg_ref[...] * lg_ref[...]

    def seg(gi):
        return pl.BlockSpec((None, MIX_ROWS, s), lambda i: (gi, i, 0))

    per = MIX_ROWS // 8
    n8 = t // 8
    row = pl.BlockSpec((MIX_ROWS, s), lambda i: (i, 0))
    st = pl.BlockSpec((MIX_CHUNKS, n_heads, HEAD_DIM, HEAD_DIM), lambda i: (i, 0, 0, 0))
    tab = pl.BlockSpec((MIX_ROWS, HEAD_DIM), lambda i: (i, 0))
    hc = _full((n_heads, c, HEAD_DIM))
    cc_ = _full((n_heads, c, c))
    h8 = _full((n_heads, 8, HEAD_DIM))
    return pl.pallas_call(
        body, name=name, grid=(nc // MIX_CHUNKS,),
        in_specs=[seg(0), seg(2), row,
                  pl.BlockSpec((8, s), lambda i: (jnp.maximum(i * per - 1, 0), 0)),
                  pl.BlockSpec((8, s), lambda i: (jnp.minimum((i + 1) * per, n8 - 1), 0)),
                  row, row, row, seg(0), seg(1), seg(2), row, st, st, st, st, cc_, cc_, cc_, hc, hc, hc, hc, h8, h8,
                  _full((8, HEAD_DIM)), tab, tab, tab, _full((3, s))],
        out_specs=[pl.BlockSpec((8, MIX_ROWS, s), lambda i: (0, i, 0)), _full((8, s)), _full((8, HEAD_DIM))],
        out_shape=[jax.ShapeDtypeStruct((8, t, s), BF16), jax.ShapeDtypeStruct((8, s), F32),
                   jax.ShapeDtypeStruct((8, HEAD_DIM), F32)],
        compiler_params=_params(("arbitrary",)),
    )(u, u, g, g, g, dz, db, drz, qkv, qkv, qkv, do, sf, sb, gf, gb, tabs["dc"], tabs["dlf"], tabs["dlb"],
      tabs["qf"], tabs["kf"], tabs["qb"], tabs["kb"], tabs["cdf"], tabs["cdb"], tabs["lg"], cos, sa, sb_tab, conv_w)


def _in_proj_bwd(du, wgs, tm, gs, name, after=()):
    n_seg, t, s = du.shape
    d = wgs[0].shape[1]
    n_w = len(wgs)
    widths = [w.shape[2] for w in wgs]
    assert sum(widths) == s

    def body(a_ref, *rest):
        w_refs, o_ref = rest[:n_w], rest[-1]
        g = pl.program_id(1)
        part = None
        for j in range(gs):
            col = 0
            for w_ref, width in zip(w_refs, widths):
                term = _dot(a_ref[j, :, col:col + width], w_ref[j], NT)
                part = term if part is None else part + term
                col += width

        @pl.when(g == 0)
        def _():
            o_ref[...] = part

        @pl.when(g > 0)
        def _():
            o_ref[...] += part

    return pl.pallas_call(
        body, name=name, grid=(t // tm, n_seg // gs),
        in_specs=[pl.BlockSpec((gs, tm, s), lambda i, g: (g, i, 0))]
        + [pl.BlockSpec((gs, d, width), lambda i, g: (g, 0, 0)) for width in widths]
        + [pl.BlockSpec(memory_space=pl.ANY)] * len(after),
        out_specs=pl.BlockSpec((tm, d), lambda i, g: (i, 0)),
        out_shape=jax.ShapeDtypeStruct((t, d), F32),
        compiler_params=_params(("parallel", "arbitrary")))(du, *wgs, *after)


def _prenorm_bwd_first(dhx, xt, dxo, nw, mod, n_lat, name):
    t, d = xt.shape
    nxb = n_lat // ROW_TILE

    def body(dh_ref, x_ref, dxo_ref, nw_ref, mod_ref, dx_ref, acc_ref):
        i = pl.program_id(0)

        @pl.when(i == 0)
        def _():
            acc_ref[...] = jnp.zeros_like(acc_ref)

        ctx = i >= nxb
        m = mod_ref[...]
        scale1 = 1.0 + jnp.where(ctx, m[4:5], m[1:2])
        x = x_ref[...]
        nw_v = nw_ref[...]
        r = lax.rsqrt(jnp.mean(x * x, axis=-1, keepdims=True) + EPS)
        xn = x * r
        dh = dh_ref[...]
        dshift = jnp.sum(dh, axis=0, keepdims=True)
        dscale = jnp.sum(dh * (xn * nw_v), axis=0, keepdims=True)
        acc_ref[6:7, :] += jnp.sum(dh * scale1 * xn, axis=0, keepdims=True)
        dxn = dh * (nw_v * scale1)
        dx = dxo_ref[...] + r * (dxn - xn * jnp.mean(dxn * xn, axis=-1, keepdims=True))

        @pl.when(i < nxb)
        def _():
            acc_ref[0:1, :] += dshift
            acc_ref[1:2, :] += dscale
            dx_ref[...] = dx

        @pl.when(i >= nxb)
        def _():
            acc_ref[3:4, :] += dshift
            acc_ref[4:5, :] += dscale

    row = pl.BlockSpec((ROW_TILE, d), lambda i: (i, 0))
    acc = _full((8, d))
    return pl.pallas_call(body, name=name, grid=(t // ROW_TILE,),
                          in_specs=[row, row, row, _full((1, d)), acc],
                          out_specs=[pl.BlockSpec((ROW_TILE, d), lambda i: (jnp.minimum(i, nxb - 1), 0)), acc],
                          out_shape=[jax.ShapeDtypeStruct((n_lat, d), F32), jax.ShapeDtypeStruct((8, d), F32)],
                          compiler_params=_params(("arbitrary",)))(dhx, xt, dxo, nw, mod)


def _adamw(g, w, m, v):
    m = ADAM_B1 * m + (1.0 - ADAM_B1) * g
    v = ADAM_B2 * v + (1.0 - ADAM_B2) * jnp.square(g)
    m_hat = m / (1.0 - ADAM_B1 ** ADAM_STEP)
    v_hat = v / (1.0 - ADAM_B2 ** ADAM_STEP)
    delta = -ADAM_LR * (m_hat / (jnp.sqrt(v_hat) + ADAM_EPS) + ADAM_WD * w)
    return delta, m, v


def _sum_adamw(parts, w, m, v, name, row0=0, into=None):
    n_p, r, n = parts.shape
    r_all = w.shape[0]
    part_block_bytes = 4 * 1024 * 1024
    br = 8
    for cand in (512, 256, 128, 64, 32, 16):
        if r % cand == 0 and row0 % cand == 0 and n_p * cand * n * parts.dtype.itemsize <= part_block_bytes:
            br = cand
            break
    blk0 = row0 // br

    def body(p_ref, w_ref, m_ref, v_ref, *rest):
        g_out, d_out, m_out, v_out = rest[-4:]
        g = p_ref[0].astype(F32)
        for j in range(1, n_p):
            g = g + p_ref[j].astype(F32)
        g_out[...] = g
        d_out[...], m_out[...], v_out[...] = _adamw(g, w_ref[...], m_ref[...], v_ref[...])

    row = pl.BlockSpec((br, n), lambda i: (i + blk0, 0))
    kept = [] if into is None else list(into)
    return pl.pallas_call(body, name=name, grid=(r // br,),
                          in_specs=[pl.BlockSpec((n_p, br, n), lambda i: (0, i, 0)), row, row, row]
                          + [pl.BlockSpec(memory_space=pl.ANY)] * len(kept),
                          out_specs=[row] * 4, out_shape=[jax.ShapeDtypeStruct((r_all, n), F32)] * 4,
                          input_output_aliases={4 + j: j for j in range(len(kept))},
                          compiler_params=_params(("parallel",)))(parts, w, m, v, *kept)


def _rope_tables(n_lat, n_ctx):
    f = HEAD_DIM // 4
    rows = n_lat // GRID_W
    inv = ROPE_BASE ** (-jnp.arange(f, dtype=F32) / f)
    ang_r = jnp.arange(rows).astype(F32)[:, None] * inv[None, :]
    ang_c = jnp.arange(GRID_W).astype(F32)[:, None] * inv[None, :]

    def by_row(a):
        return jnp.broadcast_to(a[:, None, :], (rows, GRID_W, f)).reshape(n_lat, f)

    def by_col(a):
        return jnp.broadcast_to(a[None, :, :], (rows, GRID_W, f)).reshape(n_lat, f)

    cr, sr, cc, sc = by_row(jnp.cos(ang_r)), by_row(jnp.sin(ang_r)), by_col(jnp.cos(ang_c)), by_col(jnp.sin(ang_c))
    zero = jnp.zeros_like(cr)
    cos = jnp.concatenate([cr, cr, cc, cc], axis=-1)
    sa = jnp.concatenate([-sr, zero, -sc, zero], axis=-1)
    sb = jnp.concatenate([zero, sr, zero, sc], axis=-1)
    pad = jnp.zeros((n_ctx, HEAD_DIM), F32)
    return (jnp.concatenate([cos, pad + 1.0], axis=0), jnp.concatenate([sa, pad], axis=0),
            jnp.concatenate([sb, pad], axis=0))


def _pad_rows(a, rows):
    return jnp.pad(a, [(0, rows - a.shape[0])] + [(0, 0)] * (a.ndim - 1))


def _pad_cols(a, cols):
    return jnp.pad(a, [(0, 0), (0, cols - a.shape[1])])


def kernel(x, c, ctx, c_ctx, norm_w, w_mod, b_mod, w_in, conv_w, conv_norm_w, ret_norm_w, ret_decay_f, ret_decay_b, w_out, final_norm_w, loss_target, m_c_ctx, m_norm_w, m_w_mod, m_b_mod, m_w_in, m_conv_w, m_conv_norm_w, m_ret_norm_w, m_ret_decay_f, m_ret_decay_b, m_w_out, m_final_norm_w, v_c_ctx, v_norm_w, v_w_mod, v_b_mod, v_w_in, v_conv_w, v_conv_norm_w, v_ret_norm_w, v_ret_decay_f, v_ret_decay_b, v_w_out, v_final_norm_w):
    depth = norm_w.shape[0]
    n_lat, d = x.shape[1], x.shape[2]
    n_ctx = ctx.shape[1]
    s = d // 2
    n_heads = ret_decay_f.shape[1]
    nx, ncc = n_lat // CHUNK, n_ctx // CHUNK
    n_mod = w_mod.shape[2]
    n_cw = conv_w.shape[2]
    r_out = w_out.shape[1]
    assert s == n_heads * HEAD_DIM and w_in.shape[2] == s and N_DEV * r_out == d
    assert n_lat % ROW_TILE == 0 and n_ctx % ROW_TILE == 0 and 3 * depth * n_cw <= d and d >= 3 * n_mod // 3
    me = 4 * lax.axis_index("x") + 2 * lax.axis_index("y") + lax.axis_index("c")

    w_in_bf = [w_in[l].astype(BF16) for l in range(depth)]
    w_out_bf = [w_out[l].astype(BF16) for l in range(depth)]

    first = jnp.concatenate([c.reshape(1, d), _pad_cols(conv_w.reshape(1, -1), d), jnp.zeros((6, d), F32)], axis=0)
    (first_g,) = _all_gather([first], "gather_cond", True)
    first_g = first_g.reshape(N_DEV, 8, d)
    c_all = first_g[:, 0, :]
    conv_full = first_g[:, 1, :3 * depth * n_cw].reshape(N_DEV, depth, 3, n_cw)
    conv_full = conv_full.transpose(1, 2, 0, 3).reshape(depth, 3, N_DEV * n_cw)
    c9 = jnp.concatenate([c_all, c_ctx.reshape(1, d), jnp.zeros((7, d), F32)], axis=0)

    b_sh = lax.dynamic_slice(b_mod, (0, me * n_mod), (depth, n_mod))
    mod_sh = jnp.concatenate([_mod_rows(c9, w_mod[l], b_sh[l:l + 1], f"mod_rows_l{l}") for l in range(depth)], axis=0)
    (mod_g,) = _all_gather([mod_sh], "gather_mod", True)
    mod_g = mod_g.reshape(N_DEV, depth, 16, n_mod)
    mods = []
    for l in range(depth):
        mine = lax.dynamic_index_in_dim(mod_g[:, l], me, axis=1, keepdims=False).reshape(3, d)
        cx = mod_g[:, l, 8, :].reshape(3, d)
        mods.append(jnp.concatenate([mine, cx, jnp.zeros((2, d), F32)], axis=0))

    halves = [w_in_bf[0][:, :s // 2], w_in_bf[0][:, s // 2:]]
    near, order = [], [mod_g]
    for j, part in enumerate(halves):
        near.append(_push_start([part], [_landing(part, me)], "near", f"w_in0_start_{j}", after=order))
        order = near[-1][4:]
    pending = []
    for k in range(depth):
        srcs = [w_out_bf[k]] + ([w_in_bf[k]] if k > 0 else [])
        started = _push_start(srcs, [_landing(a, me) for a in srcs], "gather", f"weights_start_l{k}", after=order)
        pending.append(started[:4])
        order = started[4:]
    w_in_g = [None] * depth
    w_out_g = [None] * depth

    cos, sa, sb_tab = _rope_tables(n_lat, n_ctx)
    t_all = n_lat + n_ctx

    saved = []
    xt = None
    for l in range(depth):
        tiles = _tiles(l, t_all, d)
        names = ["dc", "dlf", "dlb", "qf", "kf", "qb", "kb", "cdf", "cdb", "lg"]
        dec = jnp.stack([ret_decay_f[l], ret_decay_b[l]], axis=0)
        tabs = dict(zip(names, _decay_tables(dec, n_heads, f"decay_tables_l{l}")))
        if l == 0:
            hx, xt = _prenorm_first(x[0], ctx[0], norm_w[0:1], mods[0], "prenorm_l0", after=order)
            gathered, out, after = [], None, hx
            for j in range(2):
                (landed,) = _push_wait(*near[j][:4], "near", after, f"w_in0_wait_{j}")
                relay = _push_start([], [landed], "relay", f"w_in0_relay_start_{j}")
                (landed,) = _push_wait(*relay[:4], "relay", relay[4], f"w_in0_relay_wait_{j}")
                gathered.append(landed)
                out = _in_proj(hx, landed, cos, sa, sb_tab, s, j, tiles["in_tm"], f"in_proj_l0_{j}", into=out)
                after = out[0]
            u, qkv = out
            w_in_g[0] = gathered
        else:
            landed = _push_wait(*pending[l], "gather", xt, f"weights_wait_l{l}")
            w_out_g[l], w_in_g[l] = landed[0].reshape(d, d), [landed[1]]
            hx = _prenorm(xt, norm_w[l:l + 1], mods[l], n_lat, f"prenorm_l{l}")
            u, qkv = _in_proj(hx, w_in_g[l][0], cos, sa, sb_tab, s, 0, tiles["in_tm"], f"in_proj_l{l}")
        sf, sb = _state_sweep(qkv, tabs, n_heads, nx, ncc, f"state_sweep_l{l}")
        ycat, o = _mix_fwd(u, qkv, sf, sb, tabs, conv_full[l], conv_norm_w[l:l + 1], ret_norm_w[l:l + 1],
                           n_heads, nx, ncc, f"mix_fwd_l{l}")
        if l == 0:
            (landed,) = _push_wait(*pending[0], "gather", ycat, "weights_wait_l0")
            w_out_g[0] = landed.reshape(d, d)
        m_res = x_new = None
        if l < depth - 1:
            m_res, x_new = _out_proj(ycat, w_out_g[l], xt, mods[l], n_lat, tiles["out_tm"], tiles["out_tn"],
                                     f"out_proj_l{l}")
        else:
            dxt, dm, loss_blk, dfnw, gate_acc = _out_proj_loss(ycat, w_out_g[l], xt, mods[l], loss_target[0],
                                                               final_norm_w.reshape(1, d), n_lat, f"out_proj_loss_l{l}")
        saved.append(dict(tabs=tabs, xt=xt, hx=hx, u=u, qkv=qkv, sf=sf, sb=sb, ycat=ycat, o=o, m=m_res, tiles=tiles))
        xt = x_new

    loss = lax.psum(loss_blk[0, 0], MESH_AXES)

    dmod_x, dmod_c, dnw, dcnw, dgnw, dconv, ddec, dwin, dwout = [], [], [], [], [], [], [], [], []
    started_token = ()
    for l in reversed(range(depth)):
        sv = saved[l]
        tiles = sv["tiles"]
        dycat = _matmul_nt(dm, w_out_g[l], tiles["ob_tn"], f"out_proj_bwd_l{l}", after=started_token)
        dwout.append(_weight_grad(sv["ycat"], dm.reshape(1, *dm.shape), tiles["wo_bm"], _mm_rows(t_all),
                                  f"w_out_grad_l{l}")[0])
        g, dz, db, drz, do, norm_acc = _mix_bwd_a(dycat, sv["u"], sv["o"], conv_full[l], conv_norm_w[l:l + 1],
                                                   ret_norm_w[l:l + 1], n_heads, nx, ncc, f"mix_bwd_a_l{l}")
        gf, gb = _grad_state_sweep(sv["qkv"], do, sv["tabs"], n_heads, nx, ncc, f"grad_state_sweep_l{l}")
        du, conv_acc, dlg = _mix_bwd_b(sv["u"], g, dz, db, drz, sv["qkv"], do, sv["sf"], sv["sb"],
                                       gf, gb, sv["tabs"], cos, sa, sb_tab, conv_full[l], n_heads, nx, ncc,
                                       f"mix_bwd_b_l{l}")
        gate_acc_l = gate_acc
        if l > 0:
            dhx = _in_proj_bwd(du, w_in_g[l], tiles["bwd_tm"], tiles["bwd_gs"], f"in_proj_bwd_l{l}")
            below = (saved[l - 1]["m"], mods[l - 1])
            dwin_l, dxt, pre_acc, dm, gate_acc = _weight_grad_beside_prenorm_bwd(
                sv["hx"], du, dhx, sv["xt"], dxt, norm_w[l:l + 1], mods[l], below, n_lat, f"w_in_grad_l{l}")
        else:
            dwin_l = _weight_grad(sv["hx"], du, tiles["wg_bm"], tiles["wg_bt"], f"w_in_grad_l{l}")
        srcs = [dwin_l, dwout[-1].reshape(N_DEV, r_out, d)]
        lands = [_landing(lax.dynamic_index_in_dim(a, me, axis=0, keepdims=False), me) for a in srcs]
        started = _push_start(srcs, lands, "scatter", f"grads_start_l{l}")
        dwin.append(started[:4])
        started_token = started[4:]
        if l == 0:
            dhx = _in_proj_bwd(du, w_in_g[l], tiles["bwd_tm"], tiles["bwd_gs"], f"in_proj_bwd_l{l}", after=started[4:])
            dxt, pre_acc = _prenorm_bwd_first(dhx, sv["xt"], dxt, norm_w[l:l + 1], mods[l], n_lat, f"prenorm_bwd_l{l}")
        dmod_x.append(jnp.concatenate([pre_acc[0], pre_acc[1], gate_acc_l[2]]))
        dmod_c.append(jnp.concatenate([pre_acc[3], pre_acc[4], gate_acc_l[5]]))
        dnw.append(pre_acc[6])
        dcnw.append(norm_acc[0])
        dgnw.append(norm_acc[1])
        dconv.append(conv_acc[0:3])
        ddec.append(dlg[0:2, :n_heads])
    for lst in (dmod_x, dmod_c, dnw, dcnw, dgnw, dconv, ddec, dwin, dwout):
        lst.reverse()
    grad_x = dxt.reshape(1, n_lat, d)

    rows = []
    for l in range(depth):
        rows += [dmod_x[l], dmod_c[l]]
    (dmod_g,) = _all_gather([_pad_rows(jnp.stack(rows, axis=0), 8)], "gather_dmod", True)
    dmod_g = dmod_g.reshape(N_DEV, 8, 3 * d)
    mine_cols = lax.dynamic_slice(dmod_g, (0, 0, me * n_mod), (N_DEV, 8, n_mod))
    g_wmod, dcc = [], jnp.zeros((d,), F32)
    for l in range(depth):
        gw, dc_part = _mod_grads(mine_cols[:, 2 * l], mine_cols[:, 2 * l + 1], c9, w_mod[l], f"mod_grads_l{l}")
        g_wmod.append(gw)
        dcc = dcc + dc_part[0]

    n_small = 16
    small = jnp.concatenate([
        jnp.stack(dnw, axis=0),
        jnp.concatenate(dcnw).reshape(1, -1),
        jnp.concatenate(dgnw).reshape(1, -1),
        dfnw[0:1],
        dcc.reshape(1, d),
        jnp.stack(dconv, axis=0).reshape(-1, d),
        _pad_cols(jnp.stack(ddec, axis=0).reshape(1, -1), d),
    ], axis=0)
    assert depth * s == d and small.shape[0] <= n_small
    n_rows = small.shape[0]
    (small_g,) = _all_gather([_pad_rows(small, n_small)], "gather_small", True)
    small_g = small_g.reshape(N_DEV, n_small, d)

    def pack_small(nw_, cn_, gn_, fn_, cc_, df_, db_):
        return _pad_rows(jnp.concatenate([
            nw_, cn_.reshape(1, -1), gn_.reshape(1, -1), fn_.reshape(1, d), cc_.reshape(1, d),
            jnp.zeros((n_rows - depth - 5, d), F32),
            _pad_cols(jnp.stack([df_, db_], axis=1).reshape(1, -1), d)], axis=0), n_small)

    w_s = pack_small(norm_w, conv_norm_w, ret_norm_w, final_norm_w, c_ctx, ret_decay_f, ret_decay_b)
    m_s = pack_small(m_norm_w, m_conv_norm_w, m_ret_norm_w, m_final_norm_w, m_c_ctx, m_ret_decay_f, m_ret_decay_b)
    v_s = pack_small(v_norm_w, v_conv_norm_w, v_ret_norm_w, v_final_norm_w, v_c_ctx, v_ret_decay_f, v_ret_decay_b)
    small_out = _sum_adamw(small_g, w_s, m_s, v_s, "adamw_small")

    def unpack_small(a):
        nw_ = a[0:depth]
        cn_ = a[depth].reshape(depth, s)
        gn_ = a[depth + 1].reshape(depth, s)
        fn_ = a[depth + 2]
        cc_ = a[depth + 3]
        dd = a[n_rows - 1, :depth * 2 * n_heads].reshape(depth, 2, n_heads)
        return dict(c_ctx=cc_, norm_w=nw_, conv_norm_w=cn_, ret_norm_w=gn_, ret_decay_f=dd[:, 0], ret_decay_b=dd[:, 1],
                    final_norm_w=fn_)

    res = {}
    for kind, arr in zip(("grad", "delta", "m", "v"), small_out):
        for k_, val in unpack_small(arr).items():
            res[(kind, k_)] = val

    bm_parts = jnp.concatenate([dmod_g[:, 0:2 * depth:2].reshape(N_DEV, depth, 3 * d),
                                dmod_g[:, 1:2 * depth:2].reshape(N_DEV, depth, 3 * d)], axis=0)
    bm_parts = jnp.concatenate([bm_parts, jnp.zeros((2 * N_DEV, 8 - depth, 3 * d), F32)], axis=1)
    pad8 = lambda a: _pad_rows(a, 8)
    bm_out = _sum_adamw(bm_parts, pad8(b_mod), pad8(m_b_mod), pad8(v_b_mod), "adamw_b_mod")
    for kind, arr in zip(("grad", "delta", "m", "v"), bm_out):
        res[(kind, "b_mod")] = arr[:depth]

    conv_rows = small_g[:, depth + 4:depth + 4 + 3 * depth * s // d].reshape(N_DEV, depth * 3, s)
    conv_mine = lax.dynamic_slice(conv_rows, (0, 0, me * n_cw), (N_DEV, depth * 3, n_cw))
    conv_mine = jnp.concatenate([conv_mine, jnp.zeros((N_DEV, 8 - depth * 3, n_cw), F32)], axis=1)
    cw2 = lambda a: _pad_rows(a.reshape(depth * 3, n_cw), 8)
    cw_out = _sum_adamw(conv_mine, cw2(conv_w), cw2(m_conv_w), cw2(v_conv_w), "adamw_conv_w")
    for kind, arr in zip(("grad", "delta", "m", "v"), cw_out):
        res[(kind, "conv_w")] = arr[:depth * 3].reshape(depth, 3, n_cw)

    wm_out = _sum_adamw(jnp.stack(g_wmod, axis=0).reshape(1, depth * d, n_mod), w_mod.reshape(depth * d, n_mod),
                        m_w_mod.reshape(depth * d, n_mod), v_w_mod.reshape(depth * d, n_mod), "adamw_w_mod")
    for kind, arr in zip(("grad", "delta", "m", "v"), wm_out):
        res[(kind, "w_mod")] = arr.reshape(depth, d, n_mod)

    wi_out = wo_out = None
    after = wm_out[0]
    for l in reversed(range(depth)):
        win_parts, wout_parts = _push_wait(*dwin[l], "scatter", after, f"grads_wait_l{l}")
        wi_out = _sum_adamw(win_parts, w_in.reshape(depth * d, s), m_w_in.reshape(depth * d, s),
                            v_w_in.reshape(depth * d, s), f"adamw_w_in_l{l}", row0=l * d, into=wi_out)
        wo_out = _sum_adamw(wout_parts, w_out.reshape(depth * r_out, d), m_w_out.reshape(depth * r_out, d),
                            v_w_out.reshape(depth * r_out, d), f"adamw_w_out_l{l}", row0=l * r_out, into=wo_out)
        after = wo_out[0]
    for kind, arr in zip(("grad", "delta", "m", "v"), wi_out):
        res[(kind, "w_in")] = arr.reshape(depth, d, s)
    for kind, arr in zip(("grad", "delta", "m", "v"), wo_out):
        res[(kind, "w_out")] = arr.reshape(depth, r_out, d)

    order = ["c_ctx", "norm_w", "w_mod", "b_mod", "w_in", "conv_w", "conv_norm_w", "ret_norm_w", "ret_decay_f",
             "ret_decay_b", "w_out", "final_norm_w"]
    outs = [loss, grad_x]
    for kind in ("grad", "delta", "m", "v"):
        outs += [res[(kind, k_)] for k_ in order]
    return tuple(outs)
```

```python
import functools

import jax
import jax.numpy as jnp
from jax import lax
from jax.experimental import pallas as pl
from jax.experimental.pallas import tpu as pltpu

F32 = jnp.float32
BF16 = jnp.bfloat16

EPS = 1e-6
CHUNK = 128
HEAD_DIM = 128
GRID_W = 64
ROPE_BASE = 10000.0
N_DEV = 8
ADAM_LR, ADAM_B1, ADAM_B2, ADAM_EPS, ADAM_WD, ADAM_STEP = 0.001, 0.9, 0.999, 1e-08, 0.01, 10

ROW_TILE = 256
V7X_VMEM_LIMIT = 56 * 1024 * 1024
MESH_AXES = ("x", "y", "c")

NN = ((1,), (0,))
NT = ((1,), (1,))
TN = ((0,), (0,))


def _dot(a, b, dims):
    return lax.dot_general(a, b, (dims, ((), ())), preferred_element_type=F32)


def _params(sem=None):
    if sem is None:
        return pltpu.CompilerParams(vmem_limit_bytes=V7X_VMEM_LIMIT)
    return pltpu.CompilerParams(dimension_semantics=sem, vmem_limit_bytes=V7X_VMEM_LIMIT)


def _silu(z):
    return z * jax.nn.sigmoid(z)


def _dsilu(z):
    s = jax.nn.sigmoid(z)
    return s * (1.0 + z * (1.0 - s))


def _silu_and_slope(z):
    s = jax.nn.sigmoid(z)
    return z * s, s * (1.0 + z * (1.0 - s))


def _sum_all(a):
    return jnp.sum(jnp.sum(a, axis=1, keepdims=True), axis=0, keepdims=True)


def _mm_rows(t):
    return 768 if t % 768 == 0 else ROW_TILE


def _rows_or(t, rows):
    return rows if t % rows == 0 else _mm_rows(t)


def _tiles(layer, t, d):
    return dict(in_tm=_rows_or(t, 1408), bwd_tm=_mm_rows(t), bwd_gs=2, wg_bm=d, wg_bt=_mm_rows(t), wo_bm=d, ob_tn=d)


def _full(shape):
    n = len(shape)
    return pl.BlockSpec(shape, lambda *_: (0,) * n)


def _peers(x, y, c):
    return [(x, y, 1 - c), (1 - x, y, c), (x, 1 - y, c), (1 - x, 1 - y, c),
            (1 - x, y, 1 - c), (x, 1 - y, 1 - c), (1 - x, 1 - y, 1 - c)]


def _lin(p):
    return 4 * p[0] + 2 * p[1] + p[2]


def _all_gather(arrays, name, in_vmem):
    n_arr = len(arrays)
    space = pltpu.VMEM if in_vmem else pl.ANY

    def body(*refs):
        ins, outs = refs[:n_arr], refs[n_arr:2 * n_arr]
        send_sems, recv_sems, local_sems = refs[2 * n_arr:]
        x, y, c = lax.axis_index("x"), lax.axis_index("y"), lax.axis_index("c")
        me, sibling = (x, y, c), (x, y, 1 - c)
        chips = [(1 - x, y), (x, 1 - y), (1 - x, 1 - y)]
        every = []
        locals_ = []
        for a in range(n_arr):
            m_per = ins[a].shape[0]
            out_ref = outs[a]

            def rows(p, out_ref=out_ref, m_per=m_per):
                return out_ref.at[pl.ds(_lin(p) * m_per, m_per), :]

            def copy(k, block, to, src=None, a=a, rows=rows):
                return pltpu.make_async_remote_copy(
                    src_ref=rows(block) if src is None else src, dst_ref=rows(block),
                    send_sem=send_sems.at[a, k], recv_sem=recv_sems.at[a, k],
                    device_id=to, device_id_type=pl.DeviceIdType.MESH)

            mine = pltpu.make_async_copy(ins[a], rows(me), local_sems.at[a])
            mine.start()
            locals_.append(mine)
            first = [copy(0, me, sibling, src=ins[a])]
            first += [copy(1 + j, me, (*chip, c), src=ins[a]) for j, chip in enumerate(chips)]
            for cp in first:
                cp.start()
            every.append((copy, first))
        sends = []
        for a in range(n_arr):
            copy, first = every[a]
            passed = [copy(4 + j, (*chip, c), sibling) for j, chip in enumerate(chips)]
            for j, chip in enumerate(chips):
                copy(1 + j, (*chip, c), me).wait_recv()
                passed[j].start()
            sends += first + passed
        for a in range(n_arr):
            copy, _ = every[a]
            copy(0, sibling, me).wait_recv()
            for j, chip in enumerate(chips):
                copy(4 + j, (*chip, 1 - c), me).wait_recv()
        for cp in sends:
            cp.wait_send()
        for mine in locals_:
            mine.wait()

    outs = pl.pallas_call(
        body, name=name,
        out_shape=[jax.ShapeDtypeStruct((N_DEV * a.shape[0], a.shape[1]), a.dtype) for a in arrays],
        in_specs=[pl.BlockSpec(memory_space=space)] * n_arr,
        out_specs=[pl.BlockSpec(memory_space=space)] * n_arr,
        scratch_shapes=[pltpu.SemaphoreType.DMA((n_arr, 7)), pltpu.SemaphoreType.DMA((n_arr, 7)),
                        pltpu.SemaphoreType.DMA((n_arr,))],
        compiler_params=_params(),
    )(*arrays)
    return list(outs)


_HBM = pl.BlockSpec(memory_space=pltpu.HBM)
_SEM = pl.BlockSpec(memory_space=pltpu.SEMAPHORE)
_DATAFLOW = pltpu.SideEffectType.DATAFLOW_SIDE_EFFECTING


PUSH_COPIES = {"scatter": 7, "gather": 7, "near": 4, "relay": 3}


def _push_copies(src_refs, land_refs, send_sems, recv_sems, mode):
    x, y, c = lax.axis_index("x"), lax.axis_index("y"), lax.axis_index("c")
    me, sibling = (x, y, c), (x, y, 1 - c)
    n_k = PUSH_COPIES[mode]
    out, back = [], []
    if mode == "relay":
        for k, chip in enumerate([(1 - x, y), (x, 1 - y), (1 - x, 1 - y)]):
            for a, land in enumerate(land_refs):
                sems = dict(send_sem=send_sems.at[n_k * a + k], recv_sem=recv_sems.at[n_k * a + k],
                            device_id=sibling, device_id_type=pl.DeviceIdType.MESH)
                mine = land.at[_lin((*chip, c))]
                out.append(pltpu.make_async_remote_copy(src_ref=mine, dst_ref=mine, **sems))
                back.append(pltpu.make_async_remote_copy(src_ref=mine, dst_ref=land.at[_lin((*chip, 1 - c))], **sems))
        return out, back
    for k, peer in enumerate(_peers(x, y, c)[:n_k]):
        for a, (src, land) in enumerate(zip(src_refs, land_refs)):
            sems = dict(send_sem=send_sems.at[n_k * a + k], recv_sem=recv_sems.at[n_k * a + k],
                        device_id=peer, device_id_type=pl.DeviceIdType.MESH)
            mine = src.at[_lin(peer)] if mode == "scatter" else src
            out.append(pltpu.make_async_remote_copy(src_ref=mine, dst_ref=land.at[_lin(me)], **sems))
            back.append(pltpu.make_async_remote_copy(src_ref=mine, dst_ref=land.at[_lin(peer)], **sems))
    return out, back


def _push_start(srcs, lands, mode, name, after=()):
    n_src, n = len(srcs), len(lands)
    n_buf = n_src + n
    n_in = n_buf + len(after)
    n_sem = PUSH_COPIES[mode] * n

    def body(*refs):
        send_sems, recv_sems = refs[n_in], refs[n_in + 1]
        out, _ = _push_copies(refs[:n_src], refs[n_src:n_buf], send_sems, recv_sems, mode)
        for cp in out:
            cp.start()
        token = refs[-1]
        token[...] = jnp.zeros_like(token)

    both = list(srcs) + list(lands)
    res = pl.pallas_call(
        body, name=name,
        out_shape=[pltpu.SemaphoreType.DMA((n_sem,)), pltpu.SemaphoreType.DMA((n_sem,))]
        + [pltpu.HBM(a.shape, a.dtype) for a in both] + [jax.ShapeDtypeStruct((8, 128), F32)],
        in_specs=[_HBM] * n_buf + [pl.BlockSpec(memory_space=pl.ANY)] * len(after),
        out_specs=[_SEM, _SEM] + [_HBM] * n_buf + [pl.BlockSpec(memory_space=pltpu.VMEM)],
        input_output_aliases={i: 2 + i for i in range(n_buf)},
        compiler_params=pltpu.CompilerParams(has_side_effects=_DATAFLOW),
    )(*[pltpu.with_memory_space_constraint(a, pltpu.HBM) for a in both], *after)
    return res[0], res[1], list(res[2:2 + n_src]), list(res[2 + n_src:2 + n_buf]), res[-1]


def _push_wait(send_sems, recv_sems, srcs, lands, mode, after, name):
    n_src, n = len(srcs), len(lands)
    n_buf = n_src + n

    def body(*refs):
        out, back = _push_copies(refs[:n_src], refs[n_src:n_buf], refs[n_buf], refs[n_buf + 1], mode)
        for cp in out:
            cp.wait_send()
        for cp in back:
            cp.wait_recv()

    both = list(srcs) + list(lands)
    res = pl.pallas_call(
        body, name=name,
        out_shape=[pltpu.HBM(a.shape, a.dtype) for a in both],
        in_specs=[_HBM] * n_buf + [_SEM, _SEM, pl.BlockSpec(memory_space=pl.ANY)],
        out_specs=[_HBM] * n_buf,
        input_output_aliases={i: i for i in range(n_buf)},
        compiler_params=pltpu.CompilerParams(has_side_effects=_DATAFLOW),
    )(*both, send_sems, recv_sems, after)
    return list(res[n_src:])


def _landing(own, me):
    zone = lax.empty((N_DEV,) + own.shape, own.dtype)
    return lax.dynamic_update_slice(zone, own[None], (me,) + (0,) * own.ndim)


def _mod_rows(c9, w_mod, b_sh, name):
    n = w_mod.shape[1]

    def body(c_ref, w_ref, b_ref, o_ref):
        s9 = _silu(c_ref[...]).astype(BF16)
        o_ref[...] = _dot(s9, w_ref[...].astype(BF16), NN) + b_ref[...]

    return pl.pallas_call(body, name=name, out_shape=jax.ShapeDtypeStruct((16, n), F32),
                          compiler_params=_params())(c9, w_mod, b_sh)


def _mod_grads(dm_rows, dc_rows, c9, w_mod, name):
    d, n = w_mod.shape

    def body(dm_ref, dc_ref, c_ref, w_ref, gw_ref, dc_out):
        dc = dc_ref[...]
        tot = dc[0:1]
        for j in range(1, N_DEV):
            tot = tot + dc[j:j + 1]
        row = lax.broadcasted_iota(jnp.int32, (8, n), 0)
        lower = jnp.where(row == 0, tot, 0.0)
        dmod9 = jnp.concatenate([dm_ref[...], lower], axis=0).astype(BF16)
        c9v = c_ref[...]
        s9 = _silu(c9v).astype(BF16)
        gw_ref[...] = _dot(s9, dmod9, TN)
        ds = _dot(lower.astype(BF16), w_ref[...].astype(BF16), NT)
        dc_out[...] = ds * _dsilu(c9v[8:16])

    return pl.pallas_call(body, name=name,
                          out_shape=[jax.ShapeDtypeStruct((d, n), F32), jax.ShapeDtypeStruct((8, d), F32)],
                          compiler_params=_params())(dm_rows, dc_rows, c9, w_mod)


def _decay_tables(dec, n_heads, name):
    c = CHUNK

    def body(dec_ref, dc_ref, dlf_ref, dlb_ref, qf_ref, kf_ref, qb_ref, kb_ref, cdf_ref, cdb_ref, lg_ref):
        h = pl.program_id(0)
        d = dec_ref[...]
        lane = lax.broadcasted_iota(jnp.int32, d.shape, 1)
        lg = -jnp.exp(jnp.sum(jnp.where(lane == h, d, 0.0), axis=1, keepdims=True))
        lgf, lgb = lg[0:1], lg[1:2]
        i = lax.broadcasted_iota(jnp.int32, (c, c), 0).astype(F32)
        j = lax.broadcasted_iota(jnp.int32, (c, c), 1).astype(F32)
        diff = i - j
        d_f = jnp.where(diff >= 0, jnp.exp(lgf * jnp.maximum(diff, 0.0)), 0.0)
        d_b = jnp.where(diff <= 0, jnp.exp(lgb * jnp.maximum(-diff, 0.0)), 0.0)
        dc_ref[...] = d_f + d_b
        dlf_ref[...] = diff * d_f
        dlb_ref[...] = -diff * d_b
        pos = lax.broadcasted_iota(jnp.int32, (c, HEAD_DIM), 0).astype(F32)
        qf_ref[...] = jnp.exp(lgf * (pos + 1.0))
        kf_ref[...] = jnp.exp(lgf * (c - 1.0 - pos))
        qb_ref[...] = jnp.exp(lgb * (c - pos))
        kb_ref[...] = jnp.exp(lgb * pos)
        ones = jnp.ones((8, HEAD_DIM), F32)
        cdf_ref[...] = jnp.exp(lgf * float(c)) * ones
        cdb_ref[...] = jnp.exp(lgb * float(c)) * ones

        @pl.when(h == 0)
        def _():
            lg_ref[...] = jnp.zeros_like(lg_ref)

        row8 = lax.broadcasted_iota(jnp.int32, (8, HEAD_DIM), 0)
        lane8 = lax.broadcasted_iota(jnp.int32, (8, HEAD_DIM), 1)
        lg_ref[...] += (jnp.where((row8 == 0) & (lane8 == h), lgf, 0.0)
                        + jnp.where((row8 == 1) & (lane8 == h), lgb, 0.0))

    def per_head(*tail):
        return pl.BlockSpec((None,) + tail, lambda h: (h,) + (0,) * len(tail))

    shapes = [(c, c)] * 3 + [(c, HEAD_DIM)] * 4 + [(8, HEAD_DIM)] * 2
    return pl.pallas_call(
        body, name=name, grid=(n_heads,),
        in_specs=[_full(dec.shape)],
        out_specs=[per_head(*s) for s in shapes] + [_full((8, HEAD_DIM))],
        out_shape=[jax.ShapeDtypeStruct((n_heads,) + s, F32) for s in shapes]
        + [jax.ShapeDtypeStruct((8, HEAD_DIM), F32)],
        compiler_params=_params(("arbitrary",)),
    )(dec)


def _modulate(x, nw, shift, scale):
    r = lax.rsqrt(jnp.mean(x * x, axis=-1, keepdims=True) + EPS)
    return ((x * r) * nw * (1.0 + scale) + shift).astype(BF16)


def _split_rows(nxb, nb):
    def specs(d, step=lambda i: i):
        lat = pl.BlockSpec((ROW_TILE, d), lambda i: (jnp.minimum(step(i), nxb - 1), 0))
        ctx = pl.BlockSpec((ROW_TILE, d), lambda i: (jnp.clip(step(i) - nxb, 0, nb - nxb - 1), 0))
        return lat, ctx
    return specs


def _prenorm_first(x, ctx, nw, mod, name, after=()):
    n_lat, d = x.shape
    t = n_lat + ctx.shape[0]
    nxb = n_lat // ROW_TILE

    def body(x_ref, c_ref, nw_ref, mod_ref, *rest):
        o_ref = rest[-1]
        m = mod_ref[...]
        nw_v = nw_ref[...]

        @pl.when(pl.program_id(0) < nxb)
        def _():
            o_ref[...] = _modulate(x_ref[...], nw_v, m[0:1], m[1:2])

        @pl.when(pl.program_id(0) >= nxb)
        def _():
            o_ref[...] = _modulate(c_ref[...], nw_v, m[3:4], m[4:5])

    lat, cx = _split_rows(nxb, t // ROW_TILE)(d)
    return pl.pallas_call(
        body, name=name, grid=(t // ROW_TILE,),
        in_specs=[lat, cx, _full((1, d)), _full((8, d))] + [pl.BlockSpec(memory_space=pl.ANY)] * len(after),
        out_specs=pl.BlockSpec((ROW_TILE, d), lambda i: (i, 0)), out_shape=jax.ShapeDtypeStruct((t, d), BF16),
        compiler_params=_params(("parallel",)))(x, ctx, nw, mod, *after)


def _rope_fwd(v, cos, sa, sb):
    return v * cos + pltpu.roll(v, 96, 1) * sa + pltpu.roll(v, 32, 1) * sb


def _rope_bwd(g, cos, sa, sb):
    return g * cos + pltpu.roll(g * sa, 32, 1) + pltpu.roll(g * sb, 96, 1)


N_PLAIN = 5
U_DTYPE = BF16


def _in_proj(hx, wg, cos, sa, sb, s, part, tm, name, after=(), into=None):
    t, d = hx.shape
    n_seg, _, n = wg.shape
    nb = t // tm
    k_scale = HEAD_DIM ** -0.5
    kept = [] if into is None else list(into)

    def body(a_ref, w_ref, cos_ref, sa_ref, sb_ref, *rest):
        u_ref, qkv_ref = rest[-2:]
        g = pl.program_id(1)
        acc = _dot(a_ref[...], w_ref[...], NN)

        @pl.when(g < N_PLAIN)
        def _():
            u_ref[...] = acc.astype(U_DTYPE)

        @pl.when(g == N_PLAIN + 2)
        def _():
            qkv_ref[...] = acc.astype(BF16)

        for which, scale in ((N_PLAIN, 1.0), (N_PLAIN + 1, k_scale)):
            @pl.when(g == which)
            def _(scale=scale):
                co, a, b = cos_ref[...], sa_ref[...], sb_ref[...]
                for h in range(n // HEAD_DIM):
                    sl = slice(h * HEAD_DIM, (h + 1) * HEAD_DIM)
                    qkv_ref[:, sl] = (_rope_fwd(acc[:, sl], co, a, b) * scale).astype(BF16)

    def w_seg(g):
        return jnp.where(g < N_PLAIN - 1, g, jnp.where(g == N_PLAIN - 1, n_seg - 1, g - 1))

    def qkv_at(i, g):
        held = (jnp.where(i == 0, 0, 2), jnp.maximum(i - 1, 0))
        return (jnp.where(g < N_PLAIN, held[0], g - N_PLAIN), jnp.where(g < N_PLAIN, held[1], i), part)

    tab = pl.BlockSpec((tm, HEAD_DIM), lambda i, g: (i, 0))
    hbm = pl.BlockSpec(memory_space=pl.ANY)
    return pl.pallas_call(
        body, name=name, grid=(nb, n_seg),
        in_specs=[pl.BlockSpec((tm, d), lambda i, g: (i, 0)), pl.BlockSpec((None, d, n), lambda i, g: (w_seg(g), 0, 0)),
                  tab, tab, tab] + [hbm] * (len(after) + len(kept)),
        out_specs=[pl.BlockSpec((None, tm, n), lambda i, g: (jnp.minimum(g, N_PLAIN - 1), i, part)),
                   pl.BlockSpec((None, tm, n), qkv_at)],
        out_shape=[jax.ShapeDtypeStruct((N_PLAIN, t, s), U_DTYPE), jax.ShapeDtypeStruct((3, t, s), BF16)],
        input_output_aliases={5 + len(after) + j: j for j in range(len(kept))},
        compiler_params=_params(("arbitrary", "arbitrary")))(hx, wg, cos, sa, sb, *after, *kept)


def _pair_sweep(xs, ys, tab_f, tab_b, cdf, cdb, n_heads, nx, ncc, reverse, name):
    t, s = xs[0].shape[-2:]
    nc = nx + ncc
    c = CHUNK
    n_pair = nc // 2
    assert nx % 2 == 0 and ncc % 2 == 0

    def f_pair(i):
        step = n_pair - 1 - i if reverse else i
        return jnp.where(step < ncc // 2, nx // 2 + step, step - ncc // 2)

    def b_pair(i):
        return i if reverse else n_pair - 1 - i

    f_subs = (1, 0) if reverse else (0, 1)
    b_subs = (0, 1) if reverse else (1, 0)

    def body(xf_ref, yf_ref, xb_ref, yb_ref, tf, tb, cdf_ref, cdb_ref, sf_out, sb_out, sf, sb):
        @pl.when(pl.program_id(0) == 0)
        def _():
            sf[...] = jnp.zeros_like(sf)
            sb[...] = jnp.zeros_like(sb)

        for step in range(2):
            for x_ref, y_ref, tab, cd, out, st, sub in ((xf_ref, yf_ref, tf, cdf_ref, sf_out, sf, f_subs[step]),
                                                        (xb_ref, yb_ref, tb, cdb_ref, sb_out, sb, b_subs[step])):
                rows = pl.ds(sub * c, c)
                for h in range(n_heads):
                    sl = pl.ds(h * HEAD_DIM, HEAD_DIM)
                    out[sub, h] = st[h].astype(BF16)
                    xd = (x_ref[rows, sl].astype(F32) * tab[h]).astype(BF16)
                    st[h] = cd[h][0:1, :] * st[h] + _dot(xd, y_ref[rows, sl], TN)

    def spec(arr, pair):
        lead = arr[1]
        if lead is None:
            return pl.BlockSpec((2 * c, s), lambda i: (pair(i), 0))
        return pl.BlockSpec((None, 2 * c, s), lambda i: (lead, pair(i), 0))

    st_blk = (2, n_heads, HEAD_DIM, HEAD_DIM)
    return pl.pallas_call(
        body, name=name, grid=(n_pair,),
        in_specs=[spec(xs, f_pair), spec(ys, f_pair), spec(xs, b_pair), spec(ys, b_pair),
                  _full((n_heads, c, HEAD_DIM)), _full((n_heads, c, HEAD_DIM)),
                  _full((n_heads, 8, HEAD_DIM)), _full((n_heads, 8, HEAD_DIM))],
        out_specs=[pl.BlockSpec(st_blk, lambda i: (f_pair(i), 0, 0, 0)), pl.BlockSpec(st_blk, lambda i: (b_pair(i), 0, 0, 0))],
        out_shape=[jax.ShapeDtypeStruct((nc, n_heads, HEAD_DIM, HEAD_DIM), BF16)] * 2,
        scratch_shapes=[pltpu.VMEM((n_heads, HEAD_DIM, HEAD_DIM), F32)] * 2,
        compiler_params=_params(("arbitrary",)),
    )(xs[0], ys[0], xs[0], ys[0], tab_f, tab_b, cdf, cdb)


def _state_sweep(qkv, tabs, n_heads, nx, ncc, name):
    return _pair_sweep((qkv, 1), (qkv, 2), tabs["kf"], tabs["kb"], tabs["cdf"], tabs["cdb"], n_heads, nx, ncc, False, name)


MIX_CHUNKS = 2
MIX_ROWS = MIX_CHUNKS * CHUNK


HALO = 16


def _halo_specs(s, t):
    per = MIX_ROWS // HALO
    n_halo = t // HALO

    def prev(g):
        return pl.BlockSpec((None, HALO, s), lambda i: (g, jnp.maximum(i * per - 1, 0), 0))

    def nxt(g):
        return pl.BlockSpec((None, HALO, s), lambda i: (g, jnp.minimum((i + 1) * per, n_halo - 1), 0))

    return prev, nxt


def _conv_input(h_ref, c_ref, hp_ref, hn_ref, cp_ref, cn_ref):
    a = c_ref[...].astype(F32) * h_ref[...].astype(F32)
    before = cp_ref[HALO - 1:HALO].astype(F32) * hp_ref[HALO - 1:HALO].astype(F32)
    after = cn_ref[0:1].astype(F32) * hn_ref[0:1].astype(F32)
    return a, before, after


def _shifted(a, before, after, has_prev, has_next):
    rows = a.shape[0]
    rowi = lax.broadcasted_iota(jnp.int32, a.shape, 0)
    am = jnp.where(rowi == 0, jnp.where(has_prev, before, 0.0), pltpu.roll(a, 1, 0))
    ap = jnp.where(rowi == rows - 1, jnp.where(has_next, after, 0.0), pltpu.roll(a, rows - 1, 0))
    return am, ap


def _neighbours(i, nx, nc):
    nxb, ncb = nx // MIX_CHUNKS, nc // MIX_CHUNKS
    return (i != 0) & (i != nxb), (i != nxb - 1) & (i != ncb - 1)


def _mix_fwd(u, qkv, sf, sb, tabs, conv_w, cnw, gnw, n_heads, nx, ncc, name):
    _, t, s = u.shape
    nc = nx + ncc
    c = CHUNK
    assert nx % MIX_CHUNKS == 0 and ncc % MIX_CHUNKS == 0

    def body(h_ref, b_ref, c_ref, z_ref, rz_ref, hp_ref, hn_ref, cp_ref, cn_ref, q_ref, k_ref, v_ref,
             sf_ref, sb_ref, dc_ref, qft, qbt, w_ref, cnw_ref, gnw_ref, y_ref, o_ref):
        i = pl.program_id(0)
        has_prev, has_next = _neighbours(i, nx, nc)
        a, before, after = _conv_input(h_ref, c_ref, hp_ref, hn_ref, cp_ref, cn_ref)
        am, ap = _shifted(a, before, after, has_prev, has_next)
        w = w_ref[...]
        y0 = w[0:1] * am + w[1:2] * a + w[2:3] * ap
        yb = b_ref[...].astype(F32) * y0
        r = lax.rsqrt(jnp.mean(yb * yb, axis=-1, keepdims=True) + EPS)
        y_ref[:, pl.ds(0, s)] = (_silu(z_ref[...].astype(F32)) * ((yb * r) * cnw_ref[...])).astype(BF16)
        for sub in range(MIX_CHUNKS):
            rows = pl.ds(sub * c, c)
            for h in range(n_heads):
                sl = pl.ds(h * HEAD_DIM, HEAD_DIM)
                q, k, v = q_ref[rows, sl], k_ref[rows, sl], v_ref[rows, sl]
                p = (_dot(q, k, NT) * dc_ref[h]).astype(BF16)
                o = _dot(p, v, NN)
                qf = q.astype(F32)
                o += _dot((qf * qft[h]).astype(BF16), sf_ref[sub, h], NN)
                o += _dot((qf * qbt[h]).astype(BF16), sb_ref[sub, h], NN)
                o_ref[rows, sl] = o
                mu = jnp.mean(o, axis=-1, keepdims=True)
                var = jnp.mean(jnp.square(o - mu), axis=-1, keepdims=True)
                on = (o - mu) * lax.rsqrt(var + EPS)
                y_ref[rows, pl.ds(s + h * HEAD_DIM, HEAD_DIM)] = (
                    _silu(rz_ref[rows, sl].astype(F32)) * (on * gnw_ref[:, sl])).astype(BF16)

    def seg(g):
        return pl.BlockSpec((None, MIX_ROWS, s), lambda i: (g, i, 0))

    prev, nxt = _halo_specs(s, t)
    row = pl.BlockSpec((MIX_ROWS, s), lambda i: (i, 0))
    st = pl.BlockSpec((MIX_CHUNKS, n_heads, HEAD_DIM, HEAD_DIM), lambda i: (i, 0, 0, 0))
    return pl.pallas_call(
        body, name=name, grid=(nc // MIX_CHUNKS,),
        in_specs=[seg(0), seg(1), seg(2), seg(3), seg(4), prev(0), nxt(0), prev(2), nxt(2), seg(0), seg(1), seg(2),
                  st, st, _full((n_heads, c, c)), _full((n_heads, c, HEAD_DIM)), _full((n_heads, c, HEAD_DIM)),
                  _full((3, s)), _full((1, s)), _full((1, s))],
        out_specs=[pl.BlockSpec((MIX_ROWS, 2 * s), lambda i: (i, 0)), row],
        out_shape=[jax.ShapeDtypeStruct((t, 2 * s), BF16), jax.ShapeDtypeStruct((t, s), F32)],
        compiler_params=_params(("parallel",)),
    )(u, u, u, u, u, u, u, u, u, qkv, qkv, qkv, sf, sb, tabs["dc"], tabs["qf"], tabs["qb"], conv_w, cnw, gnw)


def _out_proj_prenorm(ycat, w_out, res, mod, nw_next, mod_next, n_lat, name):
    t, d = ycat.shape
    nb = t // ROW_TILE
    nxb = n_lat // ROW_TILE
    split = len(res) == 2

    def body(a_ref, w_ref, *rest):
        res_refs = rest[:len(res)]
        mod_ref, nw_ref, modn_ref, m_ref, xo_ref, hx_ref, xs = rest[len(res):]
        i = pl.program_id(0)

        @pl.when(i == 0)
        def _():
            xs[...] = jnp.zeros_like(xs)

        cur_ctx = jnp.minimum(i, nb - 1) >= nxb
        prev_ctx = i - 1 >= nxb

        def step(cur, prev):
            mv, mn = mod_ref[...], modn_ref[...]
            shift = jnp.where(prev_ctx, mn[3:4], mn[0:1])
            scale = jnp.where(prev_ctx, mn[4:5], mn[1:2])
            hx_ref[...] = _modulate(xs[prev], nw_ref[...], shift, scale)
            m = _dot(a_ref[...], w_ref[...], NN)
            x_res = jnp.where(cur_ctx, res_refs[1][...], res_refs[0][...]) if split else res_refs[0][...]
            x_new = x_res + jnp.where(cur_ctx, mv[5:6], mv[2:3]) * m
            m_ref[...] = m.astype(BF16)
            xo_ref[...] = x_new
            xs[cur] = x_new

        @pl.when(i % 2 == 0)
        def _():
            step(0, 1)

        @pl.when(i % 2 == 1)
        def _():
            step(1, 0)

    cur = pl.BlockSpec((ROW_TILE, d), lambda i: (jnp.minimum(i, nb - 1), 0))
    prev = pl.BlockSpec((ROW_TILE, d), lambda i: (jnp.maximum(i - 1, 0), 0))
    res_specs = list(_split_rows(nxb, nb)(d, lambda i: jnp.minimum(i, nb - 1))) if split else [cur]
    return pl.pallas_call(
        body, name=name, grid=(nb + 1,),
        in_specs=[cur, _full((d, d))] + res_specs + [_full((8, d)), _full((1, d)), _full((8, d))],
        out_specs=[cur, cur, prev],
        out_shape=[jax.ShapeDtypeStruct((t, d), BF16), jax.ShapeDtypeStruct((t, d), F32),
                   jax.ShapeDtypeStruct((t, d), BF16)],
        scratch_shapes=[pltpu.VMEM((2, ROW_TILE, d), F32)],
        compiler_params=_params(("arbitrary",)))(ycat, w_out, *res, mod, nw_next, mod_next)


def _out_proj_loss(ycat, w_out, xt, mod, tgt, fnw, n_lat, name):
    t, d = xt.shape
    nb = t // ROW_TILE
    nxb = n_lat // ROW_TILE

    def body(a_ref, w_ref, x_ref, mod_ref, t_ref, fw_ref, dx_ref, dm_ref, loss_ref, dw_ref, gacc_ref, xs, ms):
        i = pl.program_id(0)

        @pl.when(i == 0)
        def _():
            xs[...] = jnp.zeros_like(xs)
            ms[...] = jnp.zeros_like(ms)
            loss_ref[...] = jnp.zeros_like(loss_ref)
            dw_ref[...] = jnp.zeros_like(dw_ref)
            gacc_ref[...] = jnp.zeros_like(gacc_ref)

        def step(cur, prev):
            mv = mod_ref[...]
            x_prev, m_prev = xs[prev], ms[prev]
            valid = (i >= 1) & (i - 1 < nxb)
            w = fw_ref[...]
            r = lax.rsqrt(jnp.mean(x_prev * x_prev, axis=-1, keepdims=True) + EPS)
            xn = x_prev * r
            e = xn * w - t_ref[...]
            loss = 0.5 * jnp.sum(jnp.mean(e * e, axis=-1, keepdims=True), axis=0, keepdims=True)
            loss_ref[...] += jnp.where(valid, loss, 0.0)
            dy = e * (1.0 / d)
            dw_ref[0:1, :] += jnp.where(valid, jnp.sum(dy * xn, axis=0, keepdims=True), 0.0)
            dxn = dy * w
            dx = jnp.where(valid, r * (dxn - xn * jnp.mean(dxn * xn, axis=-1, keepdims=True)), 0.0)
            dx_ref[...] = dx
            dm_ref[...] = (dx * mv[2:3]).astype(BF16)
            gacc_ref[2:3, :] += jnp.sum(dx * m_prev, axis=0, keepdims=True)

            m = _dot(a_ref[...], w_ref[...], NN)
            gate = jnp.where(jnp.minimum(i, nb - 1) >= nxb, mv[5:6], mv[2:3])
            xs[cur] = x_ref[...] + gate * m
            ms[cur] = m

        @pl.when(i % 2 == 0)
        def _():
            step(0, 1)

        @pl.when(i % 2 == 1)
        def _():
            step(1, 0)

    cur = pl.BlockSpec((ROW_TILE, d), lambda i: (jnp.minimum(i, nb - 1), 0))
    prev = pl.BlockSpec((ROW_TILE, d), lambda i: (jnp.maximum(i - 1, 0), 0))
    return pl.pallas_call(
        body, name=name, grid=(nb + 1,),
        in_specs=[cur, _full((d, d)), cur, _full((8, d)),
                  pl.BlockSpec((ROW_TILE, d), lambda i: (jnp.clip(i - 1, 0, nxb - 1), 0)), _full((1, d))],
        out_specs=[prev, prev, _full((8, HEAD_DIM)), _full((8, d)), _full((8, d))],
        out_shape=[jax.ShapeDtypeStruct((t, d), F32), jax.ShapeDtypeStruct((t, d), BF16),
                   jax.ShapeDtypeStruct((8, HEAD_DIM), F32), jax.ShapeDtypeStruct((8, d), F32),
                   jax.ShapeDtypeStruct((8, d), F32)],
        scratch_shapes=[pltpu.VMEM((2, ROW_TILE, d), F32), pltpu.VMEM((2, ROW_TILE, d), F32)],
        compiler_params=_params(("arbitrary",)))(ycat, w_out, xt, mod, tgt, fnw)


def _matmul_nt(a, w, tn, name, after=()):
    t, k = a.shape
    n = w.shape[0]
    tm = _mm_rows(t)

    def body(a_ref, w_ref, *rest):
        rest[-1][...] = _dot(a_ref[...], w_ref[...], NT)

    return pl.pallas_call(
        body, name=name, grid=(n // tn, t // tm),
        in_specs=[pl.BlockSpec((tm, k), lambda j, i: (i, 0)), pl.BlockSpec((tn, k), lambda j, i: (j, 0))]
        + [pl.BlockSpec(memory_space=pl.ANY)] * len(after),
        out_specs=pl.BlockSpec((tm, tn), lambda j, i: (i, j)),
        out_shape=jax.ShapeDtypeStruct((t, n), F32),
        compiler_params=_params(("parallel", "parallel")))(a, w, *after)


def _weight_grad(a, b, bm, bt, name):
    t, m = a.shape
    n_g, _, n = b.shape
    nt = t // bt

    def body(a_ref, b_ref, o_ref, acc):
        k = pl.program_id(2)

        @pl.when(k == 0)
        def _():
            acc[...] = jnp.zeros_like(acc)

        acc[...] += _dot(a_ref[...], b_ref[...], TN)

        @pl.when(k == nt - 1)
        def _():
            o_ref[...] = acc[...].astype(o_ref.dtype)

    return pl.pallas_call(
        body, name=name, grid=(n_g, m // bm, nt),
        in_specs=[pl.BlockSpec((bt, bm), lambda g, i, k: (k, i)), pl.BlockSpec((None, bt, n), lambda g, i, k: (g, k, 0))],
        out_specs=pl.BlockSpec((None, bm, n), lambda g, i, k: (g, i, 0)),
        out_shape=jax.ShapeDtypeStruct((n_g, m, n), BF16),
        scratch_shapes=[pltpu.VMEM((bm, n), F32)],
        compiler_params=_params(("parallel", "parallel", "arbitrary")))(a, b)


def _weight_grad_beside_prenorm_bwd(a, b, dhx, xt, dxo, nw, mod, below, n_lat, name):
    t, m = a.shape
    n_g, _, n = b.shape
    d = xt.shape[1]
    bt = _mm_rows(t)
    nt = t // bt
    rows = t // (n_g * nt)
    n_piece = 2 if rows % 32 == 0 and m % 2 == 0 else 1
    rows_p, m_p = rows // n_piece, m // n_piece
    assert rows * n_g * nt == t and rows_p % 8 == 0

    def body(a_ref, b_ref, dh_ref, x_ref, dxo_ref, nw_ref, mod_ref, m_ref, modb_ref,
             o_ref, dx_ref, acc_ref, dm_ref, gacc_ref, acc):
        g, k = pl.program_id(0), pl.program_id(1)
        step = g * nt + k

        @pl.when(step == 0)
        def _():
            acc_ref[...] = jnp.zeros_like(acc_ref)
            gacc_ref[...] = jnp.zeros_like(gacc_ref)

        @pl.when(k == 0)
        def _():
            acc[...] = jnp.zeros_like(acc)

        mv, mb, nw_v = mod_ref[...], modb_ref[...], nw_ref[...]
        for p in range(n_piece):
            rs = pl.ds(p * rows_p, rows_p)
            rowi = step * rows + p * rows_p + lax.broadcasted_iota(jnp.int32, (rows_p, 1), 0)
            ctx = rowi >= n_lat
            w_lat = jnp.where(ctx, 0.0, 1.0)
            w_ctx = 1.0 - w_lat
            scale1 = 1.0 + jnp.where(ctx, mv[4:5], mv[1:2])
            x = x_ref[rs, :]
            r = lax.rsqrt(jnp.mean(x * x, axis=-1, keepdims=True) + EPS)
            xn = x * r
            dh = dh_ref[rs, :]
            dsc = dh * (xn * nw_v)
            acc_ref[0:1, :] += jnp.sum(dh * w_lat, axis=0, keepdims=True)
            acc_ref[1:2, :] += jnp.sum(dsc * w_lat, axis=0, keepdims=True)
            acc_ref[3:4, :] += jnp.sum(dh * w_ctx, axis=0, keepdims=True)
            acc_ref[4:5, :] += jnp.sum(dsc * w_ctx, axis=0, keepdims=True)
            acc_ref[6:7, :] += jnp.sum(dh * scale1 * xn, axis=0, keepdims=True)
            dxn = dh * (nw_v * scale1)
            dx = dxo_ref[rs, :] + r * (dxn - xn * jnp.mean(dxn * xn, axis=-1, keepdims=True))
            dx_ref[rs, :] = dx
            dm_ref[rs, :] = (dx * jnp.where(ctx, mb[5:6], mb[2:3])).astype(BF16)
            dg = dx * m_ref[rs, :].astype(F32)
            gacc_ref[2:3, :] += jnp.sum(dg * w_lat, axis=0, keepdims=True)
            gacc_ref[5:6, :] += jnp.sum(dg * w_ctx, axis=0, keepdims=True)

            ms_ = pl.ds(p * m_p, m_p)
            acc[ms_, :] += _dot(a_ref[:, ms_], b_ref[...], TN)

        @pl.when(k == nt - 1)
        def _():
            o_ref[...] = acc[...].astype(o_ref.dtype)

    side = pl.BlockSpec((rows, d), lambda g, k: (g * nt + k, 0))
    acc8 = _full((8, d))
    return pl.pallas_call(
        body, name=name, grid=(n_g, nt),
        in_specs=[pl.BlockSpec((bt, m), lambda g, k: (k, 0)), pl.BlockSpec((None, bt, n), lambda g, k: (g, k, 0)),
                  side, side, side, _full((1, d)), acc8, side, acc8],
        out_specs=[pl.BlockSpec((None, m, n), lambda g, k: (g, 0, 0)), side, acc8, side, acc8],
        out_shape=[jax.ShapeDtypeStruct((n_g, m, n), BF16), jax.ShapeDtypeStruct((t, d), F32),
                   jax.ShapeDtypeStruct((8, d), F32), jax.ShapeDtypeStruct((t, d), BF16),
                   jax.ShapeDtypeStruct((8, d), F32)],
        scratch_shapes=[pltpu.VMEM((m, n), F32)],
        compiler_params=_params(("arbitrary", "arbitrary")))(a, b, dhx, xt, dxo, nw, mod, *below)


def _mix_bwd_a(dycat, u, o, conv_w, cnw, gnw, n_heads, nx, ncc, name):
    _, t, s = u.shape
    nc = nx + ncc

    def body(dy_ref, h_ref, b_ref, c_ref, z_ref, rz_ref, hp_ref, hn_ref, cp_ref, cn_ref, o_ref, w_ref,
             cnw_ref, gnw_ref, g_ref, dz_ref, db_ref, drz_ref, do_ref, acc_ref):
        i = pl.program_id(0)

        @pl.when(i == 0)
        def _():
            acc_ref[...] = jnp.zeros_like(acc_ref)

        has_prev, has_next = _neighbours(i, nx, nc)
        a, before, after = _conv_input(h_ref, c_ref, hp_ref, hn_ref, cp_ref, cn_ref)
        am, ap = _shifted(a, before, after, has_prev, has_next)
        w = w_ref[...]
        y0 = w[0:1] * am + w[1:2] * a + w[2:3] * ap
        bb = b_ref[...].astype(F32)
        yb = bb * y0
        r = lax.rsqrt(jnp.mean(yb * yb, axis=-1, keepdims=True) + EPS)
        ynn = yb * r
        z = z_ref[...].astype(F32)
        dyc = dy_ref[:, pl.ds(0, s)]
        cw = cnw_ref[...]
        sz, dsz = _silu_and_slope(z)
        dz_ref[...] = (dyc * (ynn * cw) * dsz).astype(BF16)
        dyn = dyc * sz
        acc_ref[0:1, :] += jnp.sum(dyn * ynn, axis=0, keepdims=True)
        dynn = dyn * cw
        dyb = r * (dynn - ynn * jnp.mean(dynn * ynn, axis=-1, keepdims=True))
        db_ref[...] = (dyb * y0).astype(BF16)
        g_ref[...] = dyb * bb
        for h in range(n_heads):
            sl = pl.ds(h * HEAD_DIM, HEAD_DIM)
            ov = o_ref[:, sl]
            mu = jnp.mean(ov, axis=-1, keepdims=True)
            var = jnp.mean(jnp.square(ov - mu), axis=-1, keepdims=True)
            rs = lax.rsqrt(var + EPS)
            on = (ov - mu) * rs
            dyr = dy_ref[:, pl.ds(s + h * HEAD_DIM, HEAD_DIM)]
            rz = rz_ref[:, sl].astype(F32)
            gw = gnw_ref[:, sl]
            srz, dsrz = _silu_and_slope(rz)
            drz_ref[:, sl] = (dyr * (on * gw) * dsrz).astype(BF16)
            dyg = dyr * srz
            acc_ref[1:2, sl] += jnp.sum(dyg * on, axis=0, keepdims=True)
            don = dyg * gw
            do = rs * (don - jnp.mean(don, axis=-1, keepdims=True)
                       - on * jnp.mean(don * on, axis=-1, keepdims=True))
            do_ref[:, sl] = do.astype(BF16)

    def seg(g):
        return pl.BlockSpec((None, MIX_ROWS, s), lambda i: (g, i, 0))

    prev, nxt = _halo_specs(s, t)
    row = pl.BlockSpec((MIX_ROWS, s), lambda i: (i, 0))
    return pl.pallas_call(
        body, name=name, grid=(nc // MIX_CHUNKS,),
        in_specs=[pl.BlockSpec((MIX_ROWS, 2 * s), lambda i: (i, 0)), seg(0), seg(1), seg(2), seg(3), seg(4),
                  prev(0), nxt(0), prev(2), nxt(2), row, _full((3, s)), _full((1, s)), _full((1, s))],
        out_specs=[row, row, row, row, row, _full((8, s))],
        out_shape=[jax.ShapeDtypeStruct((t, s), F32)] + [jax.ShapeDtypeStruct((t, s), BF16)] * 4
        + [jax.ShapeDtypeStruct((8, s), F32)],
        compiler_params=_params(("arbitrary",)),
    )(dycat, u, u, u, u, u, u, u, u, u, o, conv_w, cnw, gnw)


def _grad_state_sweep(qkv, do, tabs, n_heads, nx, ncc, name):
    return _pair_sweep((qkv, 0), (do, None), tabs["qf"], tabs["qb"], tabs["cdf"], tabs["cdb"], n_heads, nx, ncc, True, name)


def _mix_bwd_b(u, g, dz, db, drz, qkv, do, sf, sb, gf, gb, tabs, cos, sa, sb_tab, conv_w,
               n_heads, nx, ncc, name):
    _, t, s = u.shape
    nc = nx + ncc
    c = CHUNK
    k_scale = HEAD_DIM ** -0.5

    def body(h_ref, c_ref, g_ref, gp_ref, gn_ref, dz_ref, db_ref, drz_ref, q_ref, k_ref, v_ref, do_ref,
             sf_ref, sb_ref, gf_ref, gb_ref, dc_t, dlf_t, dlb_t, qft, kft, qbt, kbt, cdf, cdb, lg_ref,
             cos_ref, sa_ref, sb_ref2, w_ref, du_ref, dw_ref, dlg_ref):
        i = pl.program_id(0)

        @pl.when(i == 0)
        def _():
            dw_ref[...] = jnp.zeros_like(dw_ref)
            dlg_ref[...] = jnp.zeros_like(dlg_ref)

        has_prev, has_next = _neighbours(i, nx, nc)
        gv = g_ref[...]
        gm, gp = _shifted(gv, gp_ref[7:8], gn_ref[0:1], has_prev, has_next)
        w = w_ref[...]
        da = w[0:1] * gp + w[1:2] * gv + w[2:3] * gm
        hh, cc = h_ref[...].astype(F32), c_ref[...].astype(F32)
        du_ref[0] = (da * cc).astype(BF16)
        du_ref[2] = (da * hh).astype(BF16)
        a = cc * hh
        dw_ref[0:1, :] += jnp.sum(a * gp, axis=0, keepdims=True)
        dw_ref[1:2, :] += jnp.sum(a * gv, axis=0, keepdims=True)
        dw_ref[2:3, :] += jnp.sum(a * gm, axis=0, keepdims=True)
        du_ref[1] = db_ref[...]
        du_ref[3] = dz_ref[...]
        du_ref[7] = drz_ref[...]

        pos = lax.broadcasted_iota(jnp.int32, (c, HEAD_DIM), 0).astype(F32)
        w_q_f, w_q_b, w_k_f = pos + 1.0, c - pos, c - 1.0 - pos
        row8 = lax.broadcasted_iota(jnp.int32, (8, HEAD_DIM), 0)
        lane8 = lax.broadcasted_iota(jnp.int32, (8, HEAD_DIM), 1)
        dlg = jnp.zeros((8, HEAD_DIM), F32)
        for sub, h in [(sub, h) for sub in range(MIX_CHUNKS) for h in range(n_heads)]:
            rows = pl.ds(sub * c, c)
            co, ra, rb = cos_ref[rows, :], sa_ref[rows, :], sb_ref2[rows, :]
            sl = pl.ds(h * HEAD_DIM, HEAD_DIM)
            q, k, v, do = q_ref[rows, sl], k_ref[rows, sl], v_ref[rows, sl], do_ref[rows, sl]
            qf, kf, dof = q.astype(F32), k.astype(F32), do.astype(F32)
            s_f, s_b, g_f, g_b = sf_ref[sub, h], sb_ref[sub, h], gf_ref[sub, h], gb_ref[sub, h]
            p = _dot(q, k, NT)
            pd = _dot(do, v, NT)
            pdd = (pd * dc_t[h]).astype(BF16)
            dq = _dot(pdd, k, NN)
            dk = _dot(pdd, q, TN)
            dv = _dot((p * dc_t[h]).astype(BF16), do, TN)
            dq_f = _dot((dof * qft[h]).astype(BF16), s_f, NT)
            dq_b = _dot((dof * qbt[h]).astype(BF16), s_b, NT)
            dk_f = _dot(v, g_f, NT) * kft[h]
            dk_b = _dot(v, g_b, NT) * kbt[h]
            dv += _dot((kf * kft[h]).astype(BF16), g_f, NN) + _dot((kf * kbt[h]).astype(BF16), g_b, NN)
            ppd = p * pd
            cd_f, cd_b = cdf[h][0:1, :], cdb[h][0:1, :]
            t_f = _sum_all(dlf_t[h] * ppd + w_q_f * qf * dq_f + w_k_f * kf * dk_f
                           + float(c) * (cd_f * (g_f.astype(F32) * s_f.astype(F32))))
            t_b = _sum_all(dlb_t[h] * ppd + w_q_b * qf * dq_b + pos * kf * dk_b
                           + float(c) * (cd_b * (g_b.astype(F32) * s_b.astype(F32))))
            dlg += jnp.where((row8 == 0) & (lane8 == h), t_f, 0.0) + jnp.where((row8 == 1) & (lane8 == h), t_b, 0.0)
            du_ref[4, rows, sl] = _rope_bwd(dq + dq_f + dq_b, co, ra, rb).astype(BF16)
            du_ref[5, rows, sl] = (_rope_bwd(dk + dk_f + dk_b, co, ra, rb) * k_scale).astype(BF16)
            du_ref[6, rows, sl] = dv.astype(BF16)
        dlg_ref[...] += dlg

        @pl.when(i == nc // MIX_CHUNKS - 1)
        def _():
            dlg_ref[...] = dlg_ref[...] * lg_ref[...]

    def seg(gi):
        return pl.BlockSpec((None, MIX_ROWS, s), lambda i: (gi, i, 0))

    per = MIX_ROWS // 8
    n8 = t // 8
    row = pl.BlockSpec((MIX_ROWS, s), lambda i: (i, 0))
    st = pl.BlockSpec((MIX_CHUNKS, n_heads, HEAD_DIM, HEAD_DIM), lambda i: (i, 0, 0, 0))
    tab = pl.BlockSpec((MIX_ROWS, HEAD_DIM), lambda i: (i, 0))
    hc = _full((n_heads, c, HEAD_DIM))
    cc_ = _full((n_heads, c, c))
    h8 = _full((n_heads, 8, HEAD_DIM))
    return pl.pallas_call(
        body, name=name, grid=(nc // MIX_CHUNKS,),
        in_specs=[seg(0), seg(2), row,
                  pl.BlockSpec((8, s), lambda i: (jnp.maximum(i * per - 1, 0), 0)),
                  pl.BlockSpec((8, s), lambda i: (jnp.minimum((i + 1) * per, n8 - 1), 0)),
                  row, row, row, seg(0), seg(1), seg(2), row, st, st, st, st, cc_, cc_, cc_, hc, hc, hc, hc, h8, h8,
                  _full((8, HEAD_DIM)), tab, tab, tab, _full((3, s))],
        out_specs=[pl.BlockSpec((8, MIX_ROWS, s), lambda i: (0, i, 0)), _full((8, s)), _full((8, HEAD_DIM))],
        out_shape=[jax.ShapeDtypeStruct((8, t, s), BF16), jax.ShapeDtypeStruct((8, s), F32),
                   jax.ShapeDtypeStruct((8, HEAD_DIM), F32)],
        compiler_params=_params(("arbitrary",)),
    )(u, u, g, g, g, dz, db, drz, qkv, qkv, qkv, do, sf, sb, gf, gb, tabs["dc"], tabs["dlf"], tabs["dlb"],
      tabs["qf"], tabs["kf"], tabs["qb"], tabs["kb"], tabs["cdf"], tabs["cdb"], tabs["lg"], cos, sa, sb_tab, conv_w)


def _in_proj_bwd(du, wgs, tm, gs, name, after=()):
    n_seg, t, s = du.shape
    d = wgs[0].shape[1]
    n_w = len(wgs)
    widths = [w.shape[2] for w in wgs]
    assert sum(widths) == s

    def body(a_ref, *rest):
        w_refs, o_ref = rest[:n_w], rest[-1]
        g = pl.program_id(1)
        part = None
        for j in range(gs):
            col = 0
            for w_ref, width in zip(w_refs, widths):
                term = _dot(a_ref[j, :, col:col + width], w_ref[j], NT)
                part = term if part is None else part + term
                col += width

        @pl.when(g == 0)
        def _():
            o_ref[...] = part

        @pl.when(g > 0)
        def _():
            o_ref[...] += part

    return pl.pallas_call(
        body, name=name, grid=(t // tm, n_seg // gs),
        in_specs=[pl.BlockSpec((gs, tm, s), lambda i, g: (g, i, 0))]
        + [pl.BlockSpec((gs, d, width), lambda i, g: (g, 0, 0)) for width in widths]
        + [pl.BlockSpec(memory_space=pl.ANY)] * len(after),
        out_specs=pl.BlockSpec((tm, d), lambda i, g: (i, 0)),
        out_shape=jax.ShapeDtypeStruct((t, d), F32),
        compiler_params=_params(("parallel", "arbitrary")))(du, *wgs, *after)


def _prenorm_bwd_first(dhx, x, ctx, dxo, nw, mod, name):
    n_lat, d = x.shape
    t = n_lat + ctx.shape[0]
    nxb = n_lat // ROW_TILE

    def body(dh_ref, x_ref, c_ref, dxo_ref, nw_ref, mod_ref, dx_ref, acc_ref):
        i = pl.program_id(0)

        @pl.when(i == 0)
        def _():
            acc_ref[...] = jnp.zeros_like(acc_ref)

        ctx = i >= nxb
        m = mod_ref[...]
        scale1 = 1.0 + jnp.where(ctx, m[4:5], m[1:2])
        x = jnp.where(ctx, c_ref[...], x_ref[...])
        nw_v = nw_ref[...]
        r = lax.rsqrt(jnp.mean(x * x, axis=-1, keepdims=True) + EPS)
        xn = x * r
        dh = dh_ref[...]
        dshift = jnp.sum(dh, axis=0, keepdims=True)
        dscale = jnp.sum(dh * (xn * nw_v), axis=0, keepdims=True)
        acc_ref[6:7, :] += jnp.sum(dh * scale1 * xn, axis=0, keepdims=True)
        dxn = dh * (nw_v * scale1)
        dx = dxo_ref[...] + r * (dxn - xn * jnp.mean(dxn * xn, axis=-1, keepdims=True))

        @pl.when(i < nxb)
        def _():
            acc_ref[0:1, :] += dshift
            acc_ref[1:2, :] += dscale
            dx_ref[...] = dx

        @pl.when(i >= nxb)
        def _():
            acc_ref[3:4, :] += dshift
            acc_ref[4:5, :] += dscale

    row = pl.BlockSpec((ROW_TILE, d), lambda i: (i, 0))
    lat, cx = _split_rows(nxb, t // ROW_TILE)(d)
    acc = _full((8, d))
    return pl.pallas_call(body, name=name, grid=(t // ROW_TILE,),
                          in_specs=[row, lat, cx, row, _full((1, d)), acc],
                          out_specs=[lat, acc],
                          out_shape=[jax.ShapeDtypeStruct((n_lat, d), F32), jax.ShapeDtypeStruct((8, d), F32)],
                          compiler_params=_params(("arbitrary",)))(dhx, x, ctx, dxo, nw, mod)


def _adamw(g, w, m, v):
    m = ADAM_B1 * m + (1.0 - ADAM_B1) * g
    v = ADAM_B2 * v + (1.0 - ADAM_B2) * jnp.square(g)
    m_hat = m / (1.0 - ADAM_B1 ** ADAM_STEP)
    v_hat = v / (1.0 - ADAM_B2 ** ADAM_STEP)
    delta = -ADAM_LR * (m_hat / (jnp.sqrt(v_hat) + ADAM_EPS) + ADAM_WD * w)
    return delta, m, v


def _sum_adamw(parts, w, m, v, name, row0=0, into=None):
    n_p, r, n = parts.shape
    r_all = w.shape[0]
    part_block_bytes = 4 * 1024 * 1024
    br = 8
    for cand in (512, 256, 128, 64, 32, 16):
        if r % cand == 0 and row0 % cand == 0 and n_p * cand * n * parts.dtype.itemsize <= part_block_bytes:
            br = cand
            break
    blk0 = row0 // br

    def body(p_ref, w_ref, m_ref, v_ref, *rest):
        g_out, d_out, m_out, v_out = rest[-4:]
        g = p_ref[0].astype(F32)
        for j in range(1, n_p):
            g = g + p_ref[j].astype(F32)
        g_out[...] = g
        d_out[...], m_out[...], v_out[...] = _adamw(g, w_ref[...], m_ref[...], v_ref[...])

    row = pl.BlockSpec((br, n), lambda i: (i + blk0, 0))
    kept = [] if into is None else list(into)
    return pl.pallas_call(body, name=name, grid=(r // br,),
                          in_specs=[pl.BlockSpec((n_p, br, n), lambda i: (0, i, 0)), row, row, row]
                          + [pl.BlockSpec(memory_space=pl.ANY)] * len(kept),
                          out_specs=[row] * 4, out_shape=[jax.ShapeDtypeStruct((r_all, n), F32)] * 4,
                          input_output_aliases={4 + j: j for j in range(len(kept))},
                          compiler_params=_params(("parallel",)))(parts, w, m, v, *kept)


def _rope_tables(n_lat, n_ctx):
    f = HEAD_DIM // 4
    rows = n_lat // GRID_W
    inv = ROPE_BASE ** (-jnp.arange(f, dtype=F32) / f)
    ang_r = jnp.arange(rows).astype(F32)[:, None] * inv[None, :]
    ang_c = jnp.arange(GRID_W).astype(F32)[:, None] * inv[None, :]

    def by_row(a):
        return jnp.broadcast_to(a[:, None, :], (rows, GRID_W, f)).reshape(n_lat, f)

    def by_col(a):
        return jnp.broadcast_to(a[None, :, :], (rows, GRID_W, f)).reshape(n_lat, f)

    cr, sr, cc, sc = by_row(jnp.cos(ang_r)), by_row(jnp.sin(ang_r)), by_col(jnp.cos(ang_c)), by_col(jnp.sin(ang_c))
    zero = jnp.zeros_like(cr)
    cos = jnp.concatenate([cr, cr, cc, cc], axis=-1)
    sa = jnp.concatenate([-sr, zero, -sc, zero], axis=-1)
    sb = jnp.concatenate([zero, sr, zero, sc], axis=-1)
    pad = jnp.zeros((n_ctx, HEAD_DIM), F32)
    return (jnp.concatenate([cos, pad + 1.0], axis=0), jnp.concatenate([sa, pad], axis=0),
            jnp.concatenate([sb, pad], axis=0))


def _pad_rows(a, rows):
    return jnp.pad(a, [(0, rows - a.shape[0])] + [(0, 0)] * (a.ndim - 1))


def _pad_cols(a, cols):
    return jnp.pad(a, [(0, 0), (0, cols - a.shape[1])])


def kernel(x, c, ctx, c_ctx, norm_w, w_mod, b_mod, w_in, conv_w, conv_norm_w, ret_norm_w, ret_decay_f, ret_decay_b, w_out, final_norm_w, loss_target, m_c_ctx, m_norm_w, m_w_mod, m_b_mod, m_w_in, m_conv_w, m_conv_norm_w, m_ret_norm_w, m_ret_decay_f, m_ret_decay_b, m_w_out, m_final_norm_w, v_c_ctx, v_norm_w, v_w_mod, v_b_mod, v_w_in, v_conv_w, v_conv_norm_w, v_ret_norm_w, v_ret_decay_f, v_ret_decay_b, v_w_out, v_final_norm_w):
    depth = norm_w.shape[0]
    n_lat, d = x.shape[1], x.shape[2]
    n_ctx = ctx.shape[1]
    s = d // 2
    n_heads = ret_decay_f.shape[1]
    nx, ncc = n_lat // CHUNK, n_ctx // CHUNK
    n_mod = w_mod.shape[2]
    n_cw = conv_w.shape[2]
    r_out = w_out.shape[1]
    assert s == n_heads * HEAD_DIM and w_in.shape[2] == s and N_DEV * r_out == d
    assert n_lat % ROW_TILE == 0 and n_ctx % ROW_TILE == 0 and 3 * depth * n_cw <= d and d >= 3 * n_mod // 3
    me = 4 * lax.axis_index("x") + 2 * lax.axis_index("y") + lax.axis_index("c")

    w_in_bf = [w_in[l].astype(BF16) for l in range(depth)]
    w_out_bf = [w_out[l].astype(BF16) for l in range(depth)]

    first = jnp.concatenate([c.reshape(1, d), _pad_cols(conv_w.reshape(1, -1), d), jnp.zeros((6, d), F32)], axis=0)
    (first_g,) = _all_gather([first], "gather_cond", True)
    first_g = first_g.reshape(N_DEV, 8, d)
    c_all = first_g[:, 0, :]
    conv_full = first_g[:, 1, :3 * depth * n_cw].reshape(N_DEV, depth, 3, n_cw)
    conv_full = conv_full.transpose(1, 2, 0, 3).reshape(depth, 3, N_DEV * n_cw)
    c9 = jnp.concatenate([c_all, c_ctx.reshape(1, d), jnp.zeros((7, d), F32)], axis=0)

    b_sh = lax.dynamic_slice(b_mod, (0, me * n_mod), (depth, n_mod))
    mod_sh = jnp.concatenate([_mod_rows(c9, w_mod[l], b_sh[l:l + 1], f"mod_rows_l{l}") for l in range(depth)], axis=0)
    (mod_g,) = _all_gather([mod_sh], "gather_mod", True)
    mod_g = mod_g.reshape(N_DEV, depth, 16, n_mod)
    mods = []
    for l in range(depth):
        mine = lax.dynamic_index_in_dim(mod_g[:, l], me, axis=1, keepdims=False).reshape(3, d)
        cx = mod_g[:, l, 8, :].reshape(3, d)
        mods.append(jnp.concatenate([mine, cx, jnp.zeros((2, d), F32)], axis=0))

    halves = [w_in_bf[0][:, :s // 2], w_in_bf[0][:, s // 2:]]
    near, order = [], [mod_g]
    for j, part in enumerate(halves):
        near.append(_push_start([part], [_landing(part, me)], "near", f"w_in0_start_{j}", after=order))
        order = near[-1][4:]
    pending = []
    for k in range(depth):
        srcs = [w_out_bf[k]] + ([w_in_bf[k]] if k > 0 else [])
        started = _push_start(srcs, [_landing(a, me) for a in srcs], "gather", f"weights_start_l{k}", after=order)
        pending.append(started[:4])
        order = started[4:]
    w_in_g = [None] * depth
    w_out_g = [None] * depth

    cos, sa, sb_tab = _rope_tables(n_lat, n_ctx)
    t_all = n_lat + n_ctx

    saved = []
    xt = hx_next = None
    for l in range(depth):
        tiles = _tiles(l, t_all, d)
        names = ["dc", "dlf", "dlb", "qf", "kf", "qb", "kb", "cdf", "cdb", "lg"]
        dec = jnp.stack([ret_decay_f[l], ret_decay_b[l]], axis=0)
        tabs = dict(zip(names, _decay_tables(dec, n_heads, f"decay_tables_l{l}")))
        if l == 0:
            hx = _prenorm_first(x[0], ctx[0], norm_w[0:1], mods[0], "prenorm_l0", after=order)
            gathered, out, after = [], None, hx
            for j in range(2):
                (landed,) = _push_wait(*near[j][:4], "near", after, f"w_in0_wait_{j}")
                relay = _push_start([], [landed], "relay", f"w_in0_relay_start_{j}")
                (landed,) = _push_wait(*relay[:4], "relay", relay[4], f"w_in0_relay_wait_{j}")
                gathered.append(landed)
                out = _in_proj(hx, landed, cos, sa, sb_tab, s, j, tiles["in_tm"], f"in_proj_l0_{j}", into=out)
                after = out[0]
            u, qkv = out
            w_in_g[0] = gathered
        else:
            landed = _push_wait(*pending[l], "gather", xt, f"weights_wait_l{l}")
            w_out_g[l], w_in_g[l] = landed[0].reshape(d, d), [landed[1]]
            hx = hx_next
            u, qkv = _in_proj(hx, w_in_g[l][0], cos, sa, sb_tab, s, 0, tiles["in_tm"], f"in_proj_l{l}")
        sf, sb = _state_sweep(qkv, tabs, n_heads, nx, ncc, f"state_sweep_l{l}")
        ycat, o = _mix_fwd(u, qkv, sf, sb, tabs, conv_full[l], conv_norm_w[l:l + 1], ret_norm_w[l:l + 1],
                           n_heads, nx, ncc, f"mix_fwd_l{l}")
        if l == 0:
            (landed,) = _push_wait(*pending[0], "gather", ycat, "weights_wait_l0")
            w_out_g[0] = landed.reshape(d, d)
        m_res = x_new = None
        if l < depth - 1:
            res = (x[0], ctx[0]) if l == 0 else (xt,)
            m_res, x_new, hx_next = _out_proj_prenorm(ycat, w_out_g[l], res, mods[l], norm_w[l + 1:l + 2], mods[l + 1],
                                                      n_lat, f"out_proj_l{l}")
        else:
            dxt, dm, loss_blk, dfnw, gate_acc = _out_proj_loss(ycat, w_out_g[l], xt, mods[l], loss_target[0],
                                                               final_norm_w.reshape(1, d), n_lat, f"out_proj_loss_l{l}")
        saved.append(dict(tabs=tabs, xt=xt, hx=hx, u=u, qkv=qkv, sf=sf, sb=sb, ycat=ycat, o=o, m=m_res, tiles=tiles))
        xt = x_new

    loss = lax.psum(loss_blk[0, 0], MESH_AXES)

    dmod_x, dmod_c, dnw, dcnw, dgnw, dconv, ddec, dwin, dwout = [], [], [], [], [], [], [], [], []
    started_token = ()
    for l in reversed(range(depth)):
        sv = saved[l]
        tiles = sv["tiles"]
        dycat = _matmul_nt(dm, w_out_g[l], tiles["ob_tn"], f"out_proj_bwd_l{l}", after=started_token)
        dwout.append(_weight_grad(sv["ycat"], dm.reshape(1, *dm.shape), tiles["wo_bm"], _mm_rows(t_all),
                                  f"w_out_grad_l{l}")[0])
        g, dz, db, drz, do, norm_acc = _mix_bwd_a(dycat, sv["u"], sv["o"], conv_full[l], conv_norm_w[l:l + 1],
                                                   ret_norm_w[l:l + 1], n_heads, nx, ncc, f"mix_bwd_a_l{l}")
        gf, gb = _grad_state_sweep(sv["qkv"], do, sv["tabs"], n_heads, nx, ncc, f"grad_state_sweep_l{l}")
        du, conv_acc, dlg = _mix_bwd_b(sv["u"], g, dz, db, drz, sv["qkv"], do, sv["sf"], sv["sb"],
                                       gf, gb, sv["tabs"], cos, sa, sb_tab, conv_full[l], n_heads, nx, ncc,
                                       f"mix_bwd_b_l{l}")
        gate_acc_l = gate_acc
        if l > 0:
            dhx = _in_proj_bwd(du, w_in_g[l], tiles["bwd_tm"], tiles["bwd_gs"], f"in_proj_bwd_l{l}")
            below = (saved[l - 1]["m"], mods[l - 1])
            dwin_l, dxt, pre_acc, dm, gate_acc = _weight_grad_beside_prenorm_bwd(
                sv["hx"], du, dhx, sv["xt"], dxt, norm_w[l:l + 1], mods[l], below, n_lat, f"w_in_grad_l{l}")
        else:
            dwin_l = _weight_grad(sv["hx"], du, tiles["wg_bm"], tiles["wg_bt"], f"w_in_grad_l{l}")
        srcs = [dwin_l, dwout[-1].reshape(N_DEV, r_out, d)]
        lands = [_landing(lax.dynamic_index_in_dim(a, me, axis=0, keepdims=False), me) for a in srcs]
        started = _push_start(srcs, lands, "scatter", f"grads_start_l{l}")
        dwin.append(started[:4])
        started_token = started[4:]
        if l == 0:
            dhx = _in_proj_bwd(du, w_in_g[l], tiles["bwd_tm"], tiles["bwd_gs"], f"in_proj_bwd_l{l}", after=started[4:])
            dxt, pre_acc = _prenorm_bwd_first(dhx, x[0], ctx[0], dxt, norm_w[l:l + 1], mods[l], f"prenorm_bwd_l{l}")
        dmod_x.append(jnp.concatenate([pre_acc[0], pre_acc[1], gate_acc_l[2]]))
        dmod_c.append(jnp.concatenate([pre_acc[3], pre_acc[4], gate_acc_l[5]]))
        dnw.append(pre_acc[6])
        dcnw.append(norm_acc[0])
        dgnw.append(norm_acc[1])
        dconv.append(conv_acc[0:3])
        ddec.append(dlg[0:2, :n_heads])
    for lst in (dmod_x, dmod_c, dnw, dcnw, dgnw, dconv, ddec, dwin, dwout):
        lst.reverse()
    grad_x = dxt.reshape(1, n_lat, d)

    rows = []
    for l in range(depth):
        rows += [dmod_x[l], dmod_c[l]]
    (dmod_g,) = _all_gather([_pad_rows(jnp.stack(rows, axis=0), 8)], "gather_dmod", True)
    dmod_g = dmod_g.reshape(N_DEV, 8, 3 * d)
    mine_cols = lax.dynamic_slice(dmod_g, (0, 0, me * n_mod), (N_DEV, 8, n_mod))
    g_wmod, dcc = [], jnp.zeros((d,), F32)
    for l in range(depth):
        gw, dc_part = _mod_grads(mine_cols[:, 2 * l], mine_cols[:, 2 * l + 1], c9, w_mod[l], f"mod_grads_l{l}")
        g_wmod.append(gw)
        dcc = dcc + dc_part[0]

    n_small = 16
    small = jnp.concatenate([
        jnp.stack(dnw, axis=0),
        jnp.concatenate(dcnw).reshape(1, -1),
        jnp.concatenate(dgnw).reshape(1, -1),
        dfnw[0:1],
        dcc.reshape(1, d),
        jnp.stack(dconv, axis=0).reshape(-1, d),
        _pad_cols(jnp.stack(ddec, axis=0).reshape(1, -1), d),
    ], axis=0)
    assert depth * s == d and small.shape[0] <= n_small
    n_rows = small.shape[0]
    (small_g,) = _all_gather([_pad_rows(small, n_small)], "gather_small", True)
    small_g = small_g.reshape(N_DEV, n_small, d)

    def pack_small(nw_, cn_, gn_, fn_, cc_, df_, db_):
        return _pad_rows(jnp.concatenate([
            nw_, cn_.reshape(1, -1), gn_.reshape(1, -1), fn_.reshape(1, d), cc_.reshape(1, d),
            jnp.zeros((n_rows - depth - 5, d), F32),
            _pad_cols(jnp.stack([df_, db_], axis=1).reshape(1, -1), d)], axis=0), n_small)

    w_s = pack_small(norm_w, conv_norm_w, ret_norm_w, final_norm_w, c_ctx, ret_decay_f, ret_decay_b)
    m_s = pack_small(m_norm_w, m_conv_norm_w, m_ret_norm_w, m_final_norm_w, m_c_ctx, m_ret_decay_f, m_ret_decay_b)
    v_s = pack_small(v_norm_w, v_conv_norm_w, v_ret_norm_w, v_final_norm_w, v_c_ctx, v_ret_decay_f, v_ret_decay_b)
    small_out = _sum_adamw(small_g, w_s, m_s, v_s, "adamw_small")

    def unpack_small(a):
        nw_ = a[0:depth]
        cn_ = a[depth].reshape(depth, s)
        gn_ = a[depth + 1].reshape(depth, s)
        fn_ = a[depth + 2]
        cc_ = a[depth + 3]
        dd = a[n_rows - 1, :depth * 2 * n_heads].reshape(depth, 2, n_heads)
        return dict(c_ctx=cc_, norm_w=nw_, conv_norm_w=cn_, ret_norm_w=gn_, ret_decay_f=dd[:, 0], ret_decay_b=dd[:, 1],
                    final_norm_w=fn_)

    res = {}
    for kind, arr in zip(("grad", "delta", "m", "v"), small_out):
        for k_, val in unpack_small(arr).items():
            res[(kind, k_)] = val

    bm_parts = jnp.concatenate([dmod_g[:, 0:2 * depth:2].reshape(N_DEV, depth, 3 * d),
                                dmod_g[:, 1:2 * depth:2].reshape(N_DEV, depth, 3 * d)], axis=0)
    bm_parts = jnp.concatenate([bm_parts, jnp.zeros((2 * N_DEV, 8 - depth, 3 * d), F32)], axis=1)
    pad8 = lambda a: _pad_rows(a, 8)
    bm_out = _sum_adamw(bm_parts, pad8(b_mod), pad8(m_b_mod), pad8(v_b_mod), "adamw_b_mod")
    for kind, arr in zip(("grad", "delta", "m", "v"), bm_out):
        res[(kind, "b_mod")] = arr[:depth]

    conv_rows = small_g[:, depth + 4:depth + 4 + 3 * depth * s // d].reshape(N_DEV, depth * 3, s)
    conv_mine = lax.dynamic_slice(conv_rows, (0, 0, me * n_cw), (N_DEV, depth * 3, n_cw))
    conv_mine = jnp.concatenate([conv_mine, jnp.zeros((N_DEV, 8 - depth * 3, n_cw), F32)], axis=1)
    cw2 = lambda a: _pad_rows(a.reshape(depth * 3, n_cw), 8)
    cw_out = _sum_adamw(conv_mine, cw2(conv_w), cw2(m_conv_w), cw2(v_conv_w), "adamw_conv_w")
    for kind, arr in zip(("grad", "delta", "m", "v"), cw_out):
        res[(kind, "conv_w")] = arr[:depth * 3].reshape(depth, 3, n_cw)

    wm_out = _sum_adamw(jnp.stack(g_wmod, axis=0).reshape(1, depth * d, n_mod), w_mod.reshape(depth * d, n_mod),
                        m_w_mod.reshape(depth * d, n_mod), v_w_mod.reshape(depth * d, n_mod), "adamw_w_mod")
    for kind, arr in zip(("grad", "delta", "m", "v"), wm_out):
        res[(kind, "w_mod")] = arr.reshape(depth, d, n_mod)

    wi_out = wo_out = None
    after = wm_out[0]
    for l in reversed(range(depth)):
        win_parts, wout_parts = _push_wait(*dwin[l], "scatter", after, f"grads_wait_l{l}")
        wi_out = _sum_adamw(win_parts, w_in.reshape(depth * d, s), m_w_in.reshape(depth * d, s),
                            v_w_in.reshape(depth * d, s), f"adamw_w_in_l{l}", row0=l * d, into=wi_out)
        wo_out = _sum_adamw(wout_parts, w_out.reshape(depth * r_out, d), m_w_out.reshape(depth * r_out, d),
                            v_w_out.reshape(depth * r_out, d), f"adamw_w_out_l{l}", row0=l * r_out, into=wo_out)
        after = wo_out[0]
    for kind, arr in zip(("grad", "delta", "m", "v"), wi_out):
        res[(kind, "w_in")] = arr.reshape(depth, d, s)
    for kind, arr in zip(("grad", "delta", "m", "v"), wo_out):
        res[(kind, "w_out")] = arr.reshape(depth, r_out, d)

    order = ["c_ctx", "norm_w", "w_mod", "b_mod", "w_in", "conv_w", "conv_norm_w", "ret_norm_w", "ret_decay_f",
             "ret_decay_b", "w_out", "final_norm_w"]
    outs = [loss, grad_x]
    for kind in ("grad", "delta", "m", "v"):
        outs += [res[(kind, k_)] for k_ in order]
    return tuple(outs)
```

```python
import jax
import jax.numpy as jnp
from jax import lax
from jax.experimental import pallas as pl
from jax.experimental.pallas import tpu as pltpu

F32 = jnp.float32
BF16 = jnp.bfloat16

EPS = 1e-6
CHUNK = 128
HEAD_DIM = 128
GRID_W = 64
ROPE_BASE = 10000.0
N_DEV = 8
ADAM_LR, ADAM_B1, ADAM_B2, ADAM_EPS, ADAM_WD, ADAM_STEP = 0.001, 0.9, 0.999, 1e-08, 0.01, 10

ROW_TILE = 256
V7X_VMEM_LIMIT = 56 * 1024 * 1024
MESH_AXES = ("x", "y", "c")

NN = ((1,), (0,))
NT = ((1,), (1,))
TN = ((0,), (0,))


def _dot(a, b, dims):
    return lax.dot_general(a, b, (dims, ((), ())), preferred_element_type=F32)


def _params(sem=None):
    if sem is None:
        return pltpu.CompilerParams(vmem_limit_bytes=V7X_VMEM_LIMIT)
    return pltpu.CompilerParams(dimension_semantics=sem, vmem_limit_bytes=V7X_VMEM_LIMIT)


def _silu(z):
    return z * jax.nn.sigmoid(z)


def _dsilu(z):
    s = jax.nn.sigmoid(z)
    return s * (1.0 + z * (1.0 - s))


def _silu_and_slope(z):
    s = jax.nn.sigmoid(z)
    return z * s, s * (1.0 + z * (1.0 - s))


def _sum_all(a):
    return jnp.sum(jnp.sum(a, axis=1, keepdims=True), axis=0, keepdims=True)


def _mm_rows(t):
    return 768 if t % 768 == 0 else ROW_TILE


def _rows_or(t, rows):
    return rows if t % rows == 0 else _mm_rows(t)


def _tiles(layer, t, d):
    return dict(in_tm=_rows_or(t, 1408), in_tm_half=_rows_or(t, 2112), bwd_tm=_mm_rows(t), bwd_gs=2, wg_bm=d,
                wg_bt=_mm_rows(t), wo_bm=d, ob_tn=d)


def _full(shape):
    n = len(shape)
    return pl.BlockSpec(shape, lambda *_: (0,) * n)


def _peers(x, y, c):
    return [(x, y, 1 - c), (1 - x, y, c), (x, 1 - y, c), (1 - x, 1 - y, c),
            (1 - x, y, 1 - c), (x, 1 - y, 1 - c), (1 - x, 1 - y, 1 - c)]


def _lin(p):
    return 4 * p[0] + 2 * p[1] + p[2]


def _all_gather(arrays, name):
    n_arr = len(arrays)
    space = pltpu.VMEM

    def body(*refs):
        ins, outs = refs[:n_arr], refs[n_arr:2 * n_arr]
        send_sems, recv_sems, local_sems = refs[2 * n_arr:]
        x, y, c = lax.axis_index("x"), lax.axis_index("y"), lax.axis_index("c")
        me, sibling = (x, y, c), (x, y, 1 - c)
        chips = [(1 - x, y), (x, 1 - y), (1 - x, 1 - y)]
        every = []
        locals_ = []
        for a in range(n_arr):
            m_per = ins[a].shape[0]
            out_ref = outs[a]

            def rows(p, out_ref=out_ref, m_per=m_per):
                return out_ref.at[pl.ds(_lin(p) * m_per, m_per), :]

            def copy(k, block, to, src=None, a=a, rows=rows):
                return pltpu.make_async_remote_copy(
                    src_ref=rows(block) if src is None else src, dst_ref=rows(block),
                    send_sem=send_sems.at[a, k], recv_sem=recv_sems.at[a, k],
                    device_id=to, device_id_type=pl.DeviceIdType.MESH)

            mine = pltpu.make_async_copy(ins[a], rows(me), local_sems.at[a])
            mine.start()
            locals_.append(mine)
            first = [copy(0, me, sibling, src=ins[a])]
            first += [copy(1 + j, me, (*chip, c), src=ins[a]) for j, chip in enumerate(chips)]
            for cp in first:
                cp.start()
            every.append((copy, first))
        sends = []
        for a in range(n_arr):
            copy, first = every[a]
            passed = [copy(4 + j, (*chip, c), sibling) for j, chip in enumerate(chips)]
            for j, chip in enumerate(chips):
                copy(1 + j, (*chip, c), me).wait_recv()
                passed[j].start()
            sends += first + passed
        for a in range(n_arr):
            copy, _ = every[a]
            copy(0, sibling, me).wait_recv()
            for j, chip in enumerate(chips):
                copy(4 + j, (*chip, 1 - c), me).wait_recv()
        for cp in sends:
            cp.wait_send()
        for mine in locals_:
            mine.wait()

    outs = pl.pallas_call(
        body, name=name,
        out_shape=[jax.ShapeDtypeStruct((N_DEV * a.shape[0], a.shape[1]), a.dtype) for a in arrays],
        in_specs=[pl.BlockSpec(memory_space=space)] * n_arr,
        out_specs=[pl.BlockSpec(memory_space=space)] * n_arr,
        scratch_shapes=[pltpu.SemaphoreType.DMA((n_arr, 7)), pltpu.SemaphoreType.DMA((n_arr, 7)),
                        pltpu.SemaphoreType.DMA((n_arr,))],
        compiler_params=_params(),
    )(*arrays)
    return list(outs)


_HBM = pl.BlockSpec(memory_space=pltpu.HBM)
_SEM = pl.BlockSpec(memory_space=pltpu.SEMAPHORE)
_DATAFLOW = pltpu.SideEffectType.DATAFLOW_SIDE_EFFECTING


PUSH_COPIES = {"scatter": 7, "gather": 7, "near": 4, "relay": 3}


def _push_copies(src_refs, land_refs, send_sems, recv_sems, mode):
    x, y, c = lax.axis_index("x"), lax.axis_index("y"), lax.axis_index("c")
    me, sibling = (x, y, c), (x, y, 1 - c)
    n_k = PUSH_COPIES[mode]
    out, back = [], []
    if mode == "relay":
        for k, chip in enumerate([(1 - x, y), (x, 1 - y), (1 - x, 1 - y)]):
            for a, land in enumerate(land_refs):
                sems = dict(send_sem=send_sems.at[n_k * a + k], recv_sem=recv_sems.at[n_k * a + k],
                            device_id=sibling, device_id_type=pl.DeviceIdType.MESH)
                mine = land.at[_lin((*chip, c))]
                out.append(pltpu.make_async_remote_copy(src_ref=mine, dst_ref=mine, **sems))
                back.append(pltpu.make_async_remote_copy(src_ref=mine, dst_ref=land.at[_lin((*chip, 1 - c))], **sems))
        return out, back
    for k, peer in enumerate(_peers(x, y, c)[:n_k]):
        for a, (src, land) in enumerate(zip(src_refs, land_refs)):
            sems = dict(send_sem=send_sems.at[n_k * a + k], recv_sem=recv_sems.at[n_k * a + k],
                        device_id=peer, device_id_type=pl.DeviceIdType.MESH)
            mine = src.at[_lin(peer)] if mode == "scatter" else src
            out.append(pltpu.make_async_remote_copy(src_ref=mine, dst_ref=land.at[_lin(me)], **sems))
            back.append(pltpu.make_async_remote_copy(src_ref=mine, dst_ref=land.at[_lin(peer)], **sems))
    return out, back


def _push_start(srcs, lands, mode, name, after=()):
    n_src, n = len(srcs), len(lands)
    n_buf = n_src + n
    n_in = n_buf + len(after)
    n_sem = PUSH_COPIES[mode] * n

    def body(*refs):
        send_sems, recv_sems = refs[n_in], refs[n_in + 1]
        out, _ = _push_copies(refs[:n_src], refs[n_src:n_buf], send_sems, recv_sems, mode)
        for cp in out:
            cp.start()
        token = refs[-1]
        token[...] = jnp.zeros_like(token)

    both = list(srcs) + list(lands)
    res = pl.pallas_call(
        body, name=name,
        out_shape=[pltpu.SemaphoreType.DMA((n_sem,)), pltpu.SemaphoreType.DMA((n_sem,))]
        + [pltpu.HBM(a.shape, a.dtype) for a in both] + [jax.ShapeDtypeStruct((8, 128), F32)],
        in_specs=[_HBM] * n_buf + [pl.BlockSpec(memory_space=pl.ANY)] * len(after),
        out_specs=[_SEM, _SEM] + [_HBM] * n_buf + [pl.BlockSpec(memory_space=pltpu.VMEM)],
        input_output_aliases={i: 2 + i for i in range(n_buf)},
        compiler_params=pltpu.CompilerParams(has_side_effects=_DATAFLOW),
    )(*[pltpu.with_memory_space_constraint(a, pltpu.HBM) for a in both], *after)
    return res[0], res[1], list(res[2:2 + n_src]), list(res[2 + n_src:2 + n_buf]), res[-1]


def _push_wait(send_sems, recv_sems, srcs, lands, mode, after, name):
    n_src, n = len(srcs), len(lands)
    n_buf = n_src + n

    def body(*refs):
        out, back = _push_copies(refs[:n_src], refs[n_src:n_buf], refs[n_buf], refs[n_buf + 1], mode)
        for cp in out:
            cp.wait_send()
        for cp in back:
            cp.wait_recv()

    both = list(srcs) + list(lands)
    res = pl.pallas_call(
        body, name=name,
        out_shape=[pltpu.HBM(a.shape, a.dtype) for a in both],
        in_specs=[_HBM] * n_buf + [_SEM, _SEM, pl.BlockSpec(memory_space=pl.ANY)],
        out_specs=[_HBM] * n_buf,
        input_output_aliases={i: i for i in range(n_buf)},
        compiler_params=pltpu.CompilerParams(has_side_effects=_DATAFLOW),
    )(*both, send_sems, recv_sems, after)
    return list(res[n_src:])


def _landing(own, me):
    zone = lax.empty((N_DEV,) + own.shape, own.dtype)
    return lax.dynamic_update_slice(zone, own[None], (me,) + (0,) * own.ndim)


def _mod_rows(c9, w_mod, b_sh, name):
    n = w_mod.shape[1]

    def body(c_ref, w_ref, b_ref, o_ref):
        s9 = _silu(c_ref[...]).astype(BF16)
        o_ref[...] = _dot(s9, w_ref[...].astype(BF16), NN) + b_ref[...]

    return pl.pallas_call(body, name=name, out_shape=jax.ShapeDtypeStruct((16, n), F32),
                          compiler_params=_params())(c9, w_mod, b_sh)


def _mod_grads(dm_rows, dc_rows, c9, w_mod, name):
    d, n = w_mod.shape

    def body(dm_ref, dc_ref, c_ref, w_ref, gw_ref, dc_out):
        dc = dc_ref[...]
        tot = dc[0:1]
        for j in range(1, N_DEV):
            tot = tot + dc[j:j + 1]
        row = lax.broadcasted_iota(jnp.int32, (8, n), 0)
        lower = jnp.where(row == 0, tot, 0.0)
        dmod9 = jnp.concatenate([dm_ref[...], lower], axis=0).astype(BF16)
        c9v = c_ref[...]
        s9 = _silu(c9v).astype(BF16)
        gw_ref[...] = _dot(s9, dmod9, TN)
        ds = _dot(lower.astype(BF16), w_ref[...].astype(BF16), NT)
        dc_out[...] = ds * _dsilu(c9v[8:16])

    return pl.pallas_call(body, name=name,
                          out_shape=[jax.ShapeDtypeStruct((d, n), F32), jax.ShapeDtypeStruct((8, d), F32)],
                          compiler_params=_params())(dm_rows, dc_rows, c9, w_mod)


def _decay_tables(dec, n_heads, name):
    c = CHUNK

    def body(dec_ref, dc_ref, dlf_ref, dlb_ref, qf_ref, kf_ref, qb_ref, kb_ref, cdf_ref, cdb_ref, lg_ref):
        h = pl.program_id(0)
        d = dec_ref[...]
        lane = lax.broadcasted_iota(jnp.int32, d.shape, 1)
        lg = -jnp.exp(jnp.sum(jnp.where(lane == h, d, 0.0), axis=1, keepdims=True))
        lgf, lgb = lg[0:1], lg[1:2]
        i = lax.broadcasted_iota(jnp.int32, (c, c), 0).astype(F32)
        j = lax.broadcasted_iota(jnp.int32, (c, c), 1).astype(F32)
        diff = i - j
        d_f = jnp.where(diff >= 0, jnp.exp(lgf * jnp.maximum(diff, 0.0)), 0.0)
        d_b = jnp.where(diff <= 0, jnp.exp(lgb * jnp.maximum(-diff, 0.0)), 0.0)
        dc_ref[...] = d_f + d_b
        dlf_ref[...] = diff * d_f
        dlb_ref[...] = -diff * d_b
        pos = lax.broadcasted_iota(jnp.int32, (c, HEAD_DIM), 0).astype(F32)
        qf_ref[...] = jnp.exp(lgf * (pos + 1.0))
        kf_ref[...] = jnp.exp(lgf * (c - 1.0 - pos))
        qb_ref[...] = jnp.exp(lgb * (c - pos))
        kb_ref[...] = jnp.exp(lgb * pos)
        ones = jnp.ones((8, HEAD_DIM), F32)
        cdf_ref[...] = jnp.exp(lgf * float(c)) * ones
        cdb_ref[...] = jnp.exp(lgb * float(c)) * ones

        @pl.when(h == 0)
        def _():
            lg_ref[...] = jnp.zeros_like(lg_ref)

        row8 = lax.broadcasted_iota(jnp.int32, (8, HEAD_DIM), 0)
        lane8 = lax.broadcasted_iota(jnp.int32, (8, HEAD_DIM), 1)
        lg_ref[...] += (jnp.where((row8 == 0) & (lane8 == h), lgf, 0.0)
                        + jnp.where((row8 == 1) & (lane8 == h), lgb, 0.0))

    def per_head(*tail):
        return pl.BlockSpec((None,) + tail, lambda h: (h,) + (0,) * len(tail))

    shapes = [(c, c)] * 3 + [(c, HEAD_DIM)] * 4 + [(8, HEAD_DIM)] * 2
    return pl.pallas_call(
        body, name=name, grid=(n_heads,),
        in_specs=[_full(dec.shape)],
        out_specs=[per_head(*s) for s in shapes] + [_full((8, HEAD_DIM))],
        out_shape=[jax.ShapeDtypeStruct((n_heads,) + s, F32) for s in shapes]
        + [jax.ShapeDtypeStruct((8, HEAD_DIM), F32)],
        compiler_params=_params(("arbitrary",)),
    )(dec)


def _modulate(x, nw, shift, scale):
    r = lax.rsqrt(jnp.mean(x * x, axis=-1, keepdims=True) + EPS)
    return ((x * r) * nw * (1.0 + scale) + shift).astype(BF16)


def _split_rows(nxb, nb):
    def specs(d, step=lambda i: i):
        lat = pl.BlockSpec((ROW_TILE, d), lambda i: (jnp.minimum(step(i), nxb - 1), 0))
        ctx = pl.BlockSpec((ROW_TILE, d), lambda i: (jnp.clip(step(i) - nxb, 0, nb - nxb - 1), 0))
        return lat, ctx
    return specs


def _prenorm_first(x, ctx, nw, mod, name, after=()):
    n_lat, d = x.shape
    t = n_lat + ctx.shape[0]
    nxb = n_lat // ROW_TILE

    def body(x_ref, c_ref, nw_ref, mod_ref, *rest):
        o_ref = rest[-1]
        m = mod_ref[...]
        nw_v = nw_ref[...]

        @pl.when(pl.program_id(0) < nxb)
        def _():
            o_ref[...] = _modulate(x_ref[...], nw_v, m[0:1], m[1:2])

        @pl.when(pl.program_id(0) >= nxb)
        def _():
            o_ref[...] = _modulate(c_ref[...], nw_v, m[3:4], m[4:5])

    lat, cx = _split_rows(nxb, t // ROW_TILE)(d)
    return pl.pallas_call(
        body, name=name, grid=(t // ROW_TILE,),
        in_specs=[lat, cx, _full((1, d)), _full((8, d))] + [pl.BlockSpec(memory_space=pl.ANY)] * len(after),
        out_specs=pl.BlockSpec((ROW_TILE, d), lambda i: (i, 0)), out_shape=jax.ShapeDtypeStruct((t, d), BF16),
        compiler_params=_params(("parallel",)))(x, ctx, nw, mod, *after)


def _rope_fwd(v, cos, sa, sb):
    return v * cos + pltpu.roll(v, 96, 1) * sa + pltpu.roll(v, 32, 1) * sb


def _rope_bwd(g, cos, sa, sb):
    return g * cos + pltpu.roll(g * sa, 32, 1) + pltpu.roll(g * sb, 96, 1)


N_PLAIN = 5
U_DTYPE = BF16


def _in_proj(hx, wg, cos, sa, sb, s, part, tm, name, after=(), into=None):
    t, d = hx.shape
    n_seg, _, n = wg.shape
    nb = t // tm
    k_scale = HEAD_DIM ** -0.5
    kept = [] if into is None else list(into)

    def body(a_ref, w_ref, cos_ref, sa_ref, sb_ref, *rest):
        u_ref, qkv_ref = rest[-2:]
        g = pl.program_id(1)
        acc = _dot(a_ref[...], w_ref[...], NN)

        @pl.when(g < N_PLAIN)
        def _():
            u_ref[...] = acc.astype(U_DTYPE)

        @pl.when(g == N_PLAIN + 2)
        def _():
            qkv_ref[...] = acc.astype(BF16)

        for which, scale in ((N_PLAIN, 1.0), (N_PLAIN + 1, k_scale)):
            @pl.when(g == which)
            def _(scale=scale):
                co, a, b = cos_ref[...], sa_ref[...], sb_ref[...]
                for h in range(n // HEAD_DIM):
                    sl = slice(h * HEAD_DIM, (h + 1) * HEAD_DIM)
                    qkv_ref[:, sl] = (_rope_fwd(acc[:, sl], co, a, b) * scale).astype(BF16)

    def w_seg(g):
        return jnp.where(g < N_PLAIN - 1, g, jnp.where(g == N_PLAIN - 1, n_seg - 1, g - 1))

    def qkv_at(i, g):
        held = (jnp.where(i == 0, 0, 2), jnp.maximum(i - 1, 0))
        return (jnp.where(g < N_PLAIN, held[0], g - N_PLAIN), jnp.where(g < N_PLAIN, held[1], i), part)

    tab = pl.BlockSpec((tm, HEAD_DIM), lambda i, g: (i, 0))
    hbm = pl.BlockSpec(memory_space=pl.ANY)
    return pl.pallas_call(
        body, name=name, grid=(nb, n_seg),
        in_specs=[pl.BlockSpec((tm, d), lambda i, g: (i, 0)), pl.BlockSpec((None, d, n), lambda i, g: (w_seg(g), 0, 0)),
                  tab, tab, tab] + [hbm] * (len(after) + len(kept)),
        out_specs=[pl.BlockSpec((None, tm, n), lambda i, g: (jnp.minimum(g, N_PLAIN - 1), i, part)),
                   pl.BlockSpec((None, tm, n), qkv_at)],
        out_shape=[jax.ShapeDtypeStruct((N_PLAIN, t, s), U_DTYPE), jax.ShapeDtypeStruct((3, t, s), BF16)],
        input_output_aliases={5 + len(after) + j: j for j in range(len(kept))},
        compiler_params=_params(("arbitrary", "arbitrary")))(hx, wg, cos, sa, sb, *after, *kept)


def _pair_sweep(xs, ys, tab_f, tab_b, cdf, cdb, n_heads, nx, ncc, reverse, name):
    t, s = xs[0].shape[-2:]
    nc = nx + ncc
    c = CHUNK
    n_pair = nc // 2
    assert nx % 2 == 0 and ncc % 2 == 0

    def f_pair(i):
        step = n_pair - 1 - i if reverse else i
        return jnp.where(step < ncc // 2, nx // 2 + step, step - ncc // 2)

    def b_pair(i):
        return i if reverse else n_pair - 1 - i

    f_subs = (1, 0) if reverse else (0, 1)
    b_subs = (0, 1) if reverse else (1, 0)

    def body(xf_ref, yf_ref, xb_ref, yb_ref, tf, tb, cdf_ref, cdb_ref, sf_out, sb_out, sf, sb):
        @pl.when(pl.program_id(0) == 0)
        def _():
            sf[...] = jnp.zeros_like(sf)
            sb[...] = jnp.zeros_like(sb)

        for step in range(2):
            for x_ref, y_ref, tab, cd, out, st, sub in ((xf_ref, yf_ref, tf, cdf_ref, sf_out, sf, f_subs[step]),
                                                        (xb_ref, yb_ref, tb, cdb_ref, sb_out, sb, b_subs[step])):
                rows = pl.ds(sub * c, c)
                for h in range(n_heads):
                    sl = pl.ds(h * HEAD_DIM, HEAD_DIM)
                    out[sub, h] = st[h].astype(BF16)
                    xd = (x_ref[rows, sl].astype(F32) * tab[h]).astype(BF16)
                    st[h] = cd[h][0:1, :] * st[h] + _dot(xd, y_ref[rows, sl], TN)

    def spec(arr, pair):
        lead = arr[1]
        if lead is None:
            return pl.BlockSpec((2 * c, s), lambda i: (pair(i), 0))
        return pl.BlockSpec((None, 2 * c, s), lambda i: (lead, pair(i), 0))

    st_blk = (2, n_heads, HEAD_DIM, HEAD_DIM)
    return pl.pallas_call(
        body, name=name, grid=(n_pair,),
        in_specs=[spec(xs, f_pair), spec(ys, f_pair), spec(xs, b_pair), spec(ys, b_pair),
                  _full((n_heads, c, HEAD_DIM)), _full((n_heads, c, HEAD_DIM)),
                  _full((n_heads, 8, HEAD_DIM)), _full((n_heads, 8, HEAD_DIM))],
        out_specs=[pl.BlockSpec(st_blk, lambda i: (f_pair(i), 0, 0, 0)), pl.BlockSpec(st_blk, lambda i: (b_pair(i), 0, 0, 0))],
        out_shape=[jax.ShapeDtypeStruct((nc, n_heads, HEAD_DIM, HEAD_DIM), BF16)] * 2,
        scratch_shapes=[pltpu.VMEM((n_heads, HEAD_DIM, HEAD_DIM), F32)] * 2,
        compiler_params=_params(("arbitrary",)),
    )(xs[0], ys[0], xs[0], ys[0], tab_f, tab_b, cdf, cdb)


def _state_sweep(qkv, tabs, n_heads, nx, ncc, name):
    return _pair_sweep((qkv, 1), (qkv, 2), tabs["kf"], tabs["kb"], tabs["cdf"], tabs["cdb"], n_heads, nx, ncc, False, name)


MIX_CHUNKS = 2
MIX_ROWS = MIX_CHUNKS * CHUNK


HALO = 16


def _halo_specs(s, t):
    per = MIX_ROWS // HALO
    n_halo = t // HALO

    def prev(g):
        return pl.BlockSpec((None, HALO, s), lambda i: (g, jnp.maximum(i * per - 1, 0), 0))

    def nxt(g):
        return pl.BlockSpec((None, HALO, s), lambda i: (g, jnp.minimum((i + 1) * per, n_halo - 1), 0))

    return prev, nxt


def _conv_input(h_ref, c_ref, hp_ref, hn_ref, cp_ref, cn_ref):
    a = c_ref[...].astype(F32) * h_ref[...].astype(F32)
    before = cp_ref[HALO - 1:HALO].astype(F32) * hp_ref[HALO - 1:HALO].astype(F32)
    after = cn_ref[0:1].astype(F32) * hn_ref[0:1].astype(F32)
    return a, before, after


def _shifted(a, before, after, has_prev, has_next):
    rows = a.shape[0]
    rowi = lax.broadcasted_iota(jnp.int32, a.shape, 0)
    am = jnp.where(rowi == 0, jnp.where(has_prev, before, 0.0), pltpu.roll(a, 1, 0))
    ap = jnp.where(rowi == rows - 1, jnp.where(has_next, after, 0.0), pltpu.roll(a, rows - 1, 0))
    return am, ap


def _neighbours(i, nx, nc):
    nxb, ncb = nx // MIX_CHUNKS, nc // MIX_CHUNKS
    return (i != 0) & (i != nxb), (i != nxb - 1) & (i != ncb - 1)


def _mix_fwd(u, qkv, sf, sb, tabs, conv_w, cnw, gnw, n_heads, nx, ncc, name):
    _, t, s = u.shape
    nc = nx + ncc
    c = CHUNK
    assert nx % MIX_CHUNKS == 0 and ncc % MIX_CHUNKS == 0

    def body(h_ref, b_ref, c_ref, z_ref, rz_ref, hp_ref, hn_ref, cp_ref, cn_ref, q_ref, k_ref, v_ref,
             sf_ref, sb_ref, dc_ref, qft, qbt, w_ref, cnw_ref, gnw_ref, y_ref, o_ref):
        i = pl.program_id(0)
        has_prev, has_next = _neighbours(i, nx, nc)
        a, before, after = _conv_input(h_ref, c_ref, hp_ref, hn_ref, cp_ref, cn_ref)
        am, ap = _shifted(a, before, after, has_prev, has_next)
        w = w_ref[...]
        y0 = w[0:1] * am + w[1:2] * a + w[2:3] * ap
        yb = b_ref[...].astype(F32) * y0
        r = lax.rsqrt(jnp.mean(yb * yb, axis=-1, keepdims=True) + EPS)
        y_ref[:, pl.ds(0, s)] = (_silu(z_ref[...].astype(F32)) * ((yb * r) * cnw_ref[...])).astype(BF16)
        for sub in range(MIX_CHUNKS):
            rows = pl.ds(sub * c, c)
            for h in range(n_heads):
                sl = pl.ds(h * HEAD_DIM, HEAD_DIM)
                q, k, v = q_ref[rows, sl], k_ref[rows, sl], v_ref[rows, sl]
                p = (_dot(q, k, NT) * dc_ref[h]).astype(BF16)
                o = _dot(p, v, NN)
                qf = q.astype(F32)
                o += _dot((qf * qft[h]).astype(BF16), sf_ref[sub, h], NN)
                o += _dot((qf * qbt[h]).astype(BF16), sb_ref[sub, h], NN)
                o_ref[rows, sl] = o
                mu = jnp.mean(o, axis=-1, keepdims=True)
                var = jnp.mean(jnp.square(o - mu), axis=-1, keepdims=True)
                on = (o - mu) * lax.rsqrt(var + EPS)
                y_ref[rows, pl.ds(s + h * HEAD_DIM, HEAD_DIM)] = (
                    _silu(rz_ref[rows, sl].astype(F32)) * (on * gnw_ref[:, sl])).astype(BF16)

    def seg(g):
        return pl.BlockSpec((None, MIX_ROWS, s), lambda i: (g, i, 0))

    prev, nxt = _halo_specs(s, t)
    row = pl.BlockSpec((MIX_ROWS, s), lambda i: (i, 0))
    st = pl.BlockSpec((MIX_CHUNKS, n_heads, HEAD_DIM, HEAD_DIM), lambda i: (i, 0, 0, 0))
    return pl.pallas_call(
        body, name=name, grid=(nc // MIX_CHUNKS,),
        in_specs=[seg(0), seg(1), seg(2), seg(3), seg(4), prev(0), nxt(0), prev(2), nxt(2), seg(0), seg(1), seg(2),
                  st, st, _full((n_heads, c, c)), _full((n_heads, c, HEAD_DIM)), _full((n_heads, c, HEAD_DIM)),
                  _full((3, s)), _full((1, s)), _full((1, s))],
        out_specs=[pl.BlockSpec((MIX_ROWS, 2 * s), lambda i: (i, 0)), row],
        out_shape=[jax.ShapeDtypeStruct((t, 2 * s), BF16), jax.ShapeDtypeStruct((t, s), F32)],
        compiler_params=_params(("parallel",)),
    )(u, u, u, u, u, u, u, u, u, qkv, qkv, qkv, sf, sb, tabs["dc"], tabs["qf"], tabs["qb"], conv_w, cnw, gnw)


def _out_proj_prenorm(ycat, w_out, res, mod, nw_next, mod_next, n_lat, name):
    t, d = ycat.shape
    nb = t // ROW_TILE
    nxb = n_lat // ROW_TILE
    split = len(res) == 2

    def body(a_ref, w_ref, *rest):
        res_refs = rest[:len(res)]
        mod_ref, nw_ref, modn_ref, m_ref, xo_ref, hx_ref, xs = rest[len(res):]
        i = pl.program_id(0)

        @pl.when(i == 0)
        def _():
            xs[...] = jnp.zeros_like(xs)

        cur_ctx = jnp.minimum(i, nb - 1) >= nxb
        prev_ctx = i - 1 >= nxb

        def step(cur, prev):
            mv, mn = mod_ref[...], modn_ref[...]
            shift = jnp.where(prev_ctx, mn[3:4], mn[0:1])
            scale = jnp.where(prev_ctx, mn[4:5], mn[1:2])
            hx_ref[...] = _modulate(xs[prev], nw_ref[...], shift, scale)
            m = _dot(a_ref[...], w_ref[...], NN)
            x_res = jnp.where(cur_ctx, res_refs[1][...], res_refs[0][...]) if split else res_refs[0][...]
            x_new = x_res + jnp.where(cur_ctx, mv[5:6], mv[2:3]) * m
            m_ref[...] = m.astype(BF16)
            xo_ref[...] = x_new
            xs[cur] = x_new

        @pl.when(i % 2 == 0)
        def _():
            step(0, 1)

        @pl.when(i % 2 == 1)
        def _():
            step(1, 0)

    cur = pl.BlockSpec((ROW_TILE, d), lambda i: (jnp.minimum(i, nb - 1), 0))
    prev = pl.BlockSpec((ROW_TILE, d), lambda i: (jnp.maximum(i - 1, 0), 0))
    res_specs = list(_split_rows(nxb, nb)(d, lambda i: jnp.minimum(i, nb - 1))) if split else [cur]
    return pl.pallas_call(
        body, name=name, grid=(nb + 1,),
        in_specs=[cur, _full((d, d))] + res_specs + [_full((8, d)), _full((1, d)), _full((8, d))],
        out_specs=[cur, cur, prev],
        out_shape=[jax.ShapeDtypeStruct((t, d), BF16), jax.ShapeDtypeStruct((t, d), F32),
                   jax.ShapeDtypeStruct((t, d), BF16)],
        scratch_shapes=[pltpu.VMEM((2, ROW_TILE, d), F32)],
        compiler_params=_params(("arbitrary",)))(ycat, w_out, *res, mod, nw_next, mod_next)


def _out_proj_loss(ycat, w_out, xt, mod, tgt, fnw, n_lat, name):
    t, d = xt.shape
    nb = t // ROW_TILE
    nxb = n_lat // ROW_TILE

    def body(a_ref, w_ref, x_ref, mod_ref, t_ref, fw_ref, dx_ref, dm_ref, loss_ref, dw_ref, gacc_ref, xs, ms):
        i = pl.program_id(0)

        @pl.when(i == 0)
        def _():
            xs[...] = jnp.zeros_like(xs)
            ms[...] = jnp.zeros_like(ms)
            loss_ref[...] = jnp.zeros_like(loss_ref)
            dw_ref[...] = jnp.zeros_like(dw_ref)
            gacc_ref[...] = jnp.zeros_like(gacc_ref)

        def step(cur, prev):
            mv = mod_ref[...]
            x_prev, m_prev = xs[prev], ms[prev]
            valid = (i >= 1) & (i - 1 < nxb)
            w = fw_ref[...]
            r = lax.rsqrt(jnp.mean(x_prev * x_prev, axis=-1, keepdims=True) + EPS)
            xn = x_prev * r
            e = xn * w - t_ref[...]
            loss = 0.5 * jnp.sum(jnp.mean(e * e, axis=-1, keepdims=True), axis=0, keepdims=True)
            loss_ref[...] += jnp.where(valid, loss, 0.0)
            dy = e * (1.0 / d)
            dw_ref[0:1, :] += jnp.where(valid, jnp.sum(dy * xn, axis=0, keepdims=True), 0.0)
            dxn = dy * w
            dx = jnp.where(valid, r * (dxn - xn * jnp.mean(dxn * xn, axis=-1, keepdims=True)), 0.0)
            dx_ref[...] = dx
            dm_ref[...] = (dx * mv[2:3]).astype(BF16)
            gacc_ref[2:3, :] += jnp.sum(dx * m_prev, axis=0, keepdims=True)

            m = _dot(a_ref[...], w_ref[...], NN)
            gate = jnp.where(jnp.minimum(i, nb - 1) >= nxb, mv[5:6], mv[2:3])
            xs[cur] = x_ref[...] + gate * m
            ms[cur] = m

        @pl.when(i % 2 == 0)
        def _():
            step(0, 1)

        @pl.when(i % 2 == 1)
        def _():
            step(1, 0)

    cur = pl.BlockSpec((ROW_TILE, d), lambda i: (jnp.minimum(i, nb - 1), 0))
    prev = pl.BlockSpec((ROW_TILE, d), lambda i: (jnp.maximum(i - 1, 0), 0))
    return pl.pallas_call(
        body, name=name, grid=(nb + 1,),
        in_specs=[cur, _full((d, d)), cur, _full((8, d)),
                  pl.BlockSpec((ROW_TILE, d), lambda i: (jnp.clip(i - 1, 0, nxb - 1), 0)), _full((1, d))],
        out_specs=[prev, prev, _full((8, HEAD_DIM)), _full((8, d)), _full((8, d))],
        out_shape=[jax.ShapeDtypeStruct((t, d), F32), jax.ShapeDtypeStruct((t, d), BF16),
                   jax.ShapeDtypeStruct((8, HEAD_DIM), F32), jax.ShapeDtypeStruct((8, d), F32),
                   jax.ShapeDtypeStruct((8, d), F32)],
        scratch_shapes=[pltpu.VMEM((2, ROW_TILE, d), F32), pltpu.VMEM((2, ROW_TILE, d), F32)],
        compiler_params=_params(("arbitrary",)))(ycat, w_out, xt, mod, tgt, fnw)


def _matmul_nt(a, w, tn, name, after=()):
    t, k = a.shape
    n = w.shape[0]
    tm = _mm_rows(t)

    def body(a_ref, w_ref, *rest):
        rest[-1][...] = _dot(a_ref[...], w_ref[...], NT)

    return pl.pallas_call(
        body, name=name, grid=(n // tn, t // tm),
        in_specs=[pl.BlockSpec((tm, k), lambda j, i: (i, 0)), pl.BlockSpec((tn, k), lambda j, i: (j, 0))]
        + [pl.BlockSpec(memory_space=pl.ANY)] * len(after),
        out_specs=pl.BlockSpec((tm, tn), lambda j, i: (i, j)),
        out_shape=jax.ShapeDtypeStruct((t, n), F32),
        compiler_params=_params(("parallel", "parallel")))(a, w, *after)


def _weight_grad(a, b, bm, bt, name):
    t, m = a.shape
    n_g, _, n = b.shape
    nt = t // bt

    def body(a_ref, b_ref, o_ref, acc):
        k = pl.program_id(2)

        @pl.when(k == 0)
        def _():
            acc[...] = jnp.zeros_like(acc)

        acc[...] += _dot(a_ref[...], b_ref[...], TN)

        @pl.when(k == nt - 1)
        def _():
            o_ref[...] = acc[...].astype(o_ref.dtype)

    return pl.pallas_call(
        body, name=name, grid=(n_g, m // bm, nt),
        in_specs=[pl.BlockSpec((bt, bm), lambda g, i, k: (k, i)), pl.BlockSpec((None, bt, n), lambda g, i, k: (g, k, 0))],
        out_specs=pl.BlockSpec((None, bm, n), lambda g, i, k: (g, i, 0)),
        out_shape=jax.ShapeDtypeStruct((n_g, m, n), BF16),
        scratch_shapes=[pltpu.VMEM((bm, n), F32)],
        compiler_params=_params(("parallel", "parallel", "arbitrary")))(a, b)


def _weight_grad_beside_prenorm_bwd(a, b, dhx, xt, dxo, nw, mod, below, n_lat, name):
    t, m = a.shape
    n_g, _, n = b.shape
    d = xt.shape[1]
    bt = _mm_rows(t)
    nt = t // bt
    rows = t // (n_g * nt)
    n_piece = 2 if rows % 32 == 0 and m % 2 == 0 else 1
    rows_p, m_p = rows // n_piece, m // n_piece
    assert rows * n_g * nt == t and rows_p % 8 == 0

    def body(a_ref, b_ref, dh_ref, x_ref, dxo_ref, nw_ref, mod_ref, m_ref, modb_ref,
             o_ref, dx_ref, acc_ref, dm_ref, gacc_ref, acc):
        g, k = pl.program_id(0), pl.program_id(1)
        step = g * nt + k

        @pl.when(step == 0)
        def _():
            acc_ref[...] = jnp.zeros_like(acc_ref)
            gacc_ref[...] = jnp.zeros_like(gacc_ref)

        @pl.when(k == 0)
        def _():
            acc[...] = jnp.zeros_like(acc)

        mv, mb, nw_v = mod_ref[...], modb_ref[...], nw_ref[...]
        for p in range(n_piece):
            rs = pl.ds(p * rows_p, rows_p)
            rowi = step * rows + p * rows_p + lax.broadcasted_iota(jnp.int32, (rows_p, 1), 0)
            ctx = rowi >= n_lat
            w_lat = jnp.where(ctx, 0.0, 1.0)
            w_ctx = 1.0 - w_lat
            scale1 = 1.0 + jnp.where(ctx, mv[4:5], mv[1:2])
            x = x_ref[rs, :]
            r = lax.rsqrt(jnp.mean(x * x, axis=-1, keepdims=True) + EPS)
            xn = x * r
            dh = dh_ref[rs, :]
            dsc = dh * (xn * nw_v)
            acc_ref[0:1, :] += jnp.sum(dh * w_lat, axis=0, keepdims=True)
            acc_ref[1:2, :] += jnp.sum(dsc * w_lat, axis=0, keepdims=True)
            acc_ref[3:4, :] += jnp.sum(dh * w_ctx, axis=0, keepdims=True)
            acc_ref[4:5, :] += jnp.sum(dsc * w_ctx, axis=0, keepdims=True)
            acc_ref[6:7, :] += jnp.sum(dh * scale1 * xn, axis=0, keepdims=True)
            dxn = dh * (nw_v * scale1)
            dx = dxo_ref[rs, :] + r * (dxn - xn * jnp.mean(dxn * xn, axis=-1, keepdims=True))
            dx_ref[rs, :] = dx
            dm_ref[rs, :] = (dx * jnp.where(ctx, mb[5:6], mb[2:3])).astype(BF16)
            dg = dx * m_ref[rs, :].astype(F32)
            gacc_ref[2:3, :] += jnp.sum(dg * w_lat, axis=0, keepdims=True)
            gacc_ref[5:6, :] += jnp.sum(dg * w_ctx, axis=0, keepdims=True)

            ms_ = pl.ds(p * m_p, m_p)
            acc[ms_, :] += _dot(a_ref[:, ms_], b_ref[...], TN)

        @pl.when(k == nt - 1)
        def _():
            o_ref[...] = acc[...].astype(o_ref.dtype)

    side = pl.BlockSpec((rows, d), lambda g, k: (g * nt + k, 0))
    acc8 = _full((8, d))
    return pl.pallas_call(
        body, name=name, grid=(n_g, nt),
        in_specs=[pl.BlockSpec((bt, m), lambda g, k: (k, 0)), pl.BlockSpec((None, bt, n), lambda g, k: (g, k, 0)),
                  side, side, side, _full((1, d)), acc8, side, acc8],
        out_specs=[pl.BlockSpec((None, m, n), lambda g, k: (g, 0, 0)), side, acc8, side, acc8],
        out_shape=[jax.ShapeDtypeStruct((n_g, m, n), BF16), jax.ShapeDtypeStruct((t, d), F32),
                   jax.ShapeDtypeStruct((8, d), F32), jax.ShapeDtypeStruct((t, d), BF16),
                   jax.ShapeDtypeStruct((8, d), F32)],
        scratch_shapes=[pltpu.VMEM((m, n), F32)],
        compiler_params=_params(("arbitrary", "arbitrary")))(a, b, dhx, xt, dxo, nw, mod, *below)


def _mix_bwd_a(dycat, u, o, conv_w, cnw, gnw, n_heads, nx, ncc, name):
    _, t, s = u.shape
    nc = nx + ncc

    def body(dy_ref, h_ref, b_ref, c_ref, z_ref, rz_ref, hp_ref, hn_ref, cp_ref, cn_ref, o_ref, w_ref,
             cnw_ref, gnw_ref, g_ref, dz_ref, db_ref, drz_ref, do_ref, acc_ref):
        i = pl.program_id(0)

        @pl.when(i == 0)
        def _():
            acc_ref[...] = jnp.zeros_like(acc_ref)

        has_prev, has_next = _neighbours(i, nx, nc)
        a, before, after = _conv_input(h_ref, c_ref, hp_ref, hn_ref, cp_ref, cn_ref)
        am, ap = _shifted(a, before, after, has_prev, has_next)
        w = w_ref[...]
        y0 = w[0:1] * am + w[1:2] * a + w[2:3] * ap
        bb = b_ref[...].astype(F32)
        yb = bb * y0
        r = lax.rsqrt(jnp.mean(yb * yb, axis=-1, keepdims=True) + EPS)
        ynn = yb * r
        z = z_ref[...].astype(F32)
        dyc = dy_ref[:, pl.ds(0, s)]
        cw = cnw_ref[...]
        sz, dsz = _silu_and_slope(z)
        dz_ref[...] = (dyc * (ynn * cw) * dsz).astype(BF16)
        dyn = dyc * sz
        acc_ref[0:1, :] += jnp.sum(dyn * ynn, axis=0, keepdims=True)
        dynn = dyn * cw
        dyb = r * (dynn - ynn * jnp.mean(dynn * ynn, axis=-1, keepdims=True))
        db_ref[...] = (dyb * y0).astype(BF16)
        g_ref[...] = dyb * bb
        for h in range(n_heads):
            sl = pl.ds(h * HEAD_DIM, HEAD_DIM)
            ov = o_ref[:, sl]
            mu = jnp.mean(ov, axis=-1, keepdims=True)
            var = jnp.mean(jnp.square(ov - mu), axis=-1, keepdims=True)
            rs = lax.rsqrt(var + EPS)
            on = (ov - mu) * rs
            dyr = dy_ref[:, pl.ds(s + h * HEAD_DIM, HEAD_DIM)]
            rz = rz_ref[:, sl].astype(F32)
            gw = gnw_ref[:, sl]
            srz, dsrz = _silu_and_slope(rz)
            drz_ref[:, sl] = (dyr * (on * gw) * dsrz).astype(BF16)
            dyg = dyr * srz
            acc_ref[1:2, sl] += jnp.sum(dyg * on, axis=0, keepdims=True)
            don = dyg * gw
            do = rs * (don - jnp.mean(don, axis=-1, keepdims=True)
                       - on * jnp.mean(don * on, axis=-1, keepdims=True))
            do_ref[:, sl] = do.astype(BF16)

    def seg(g):
        return pl.BlockSpec((None, MIX_ROWS, s), lambda i: (g, i, 0))

    prev, nxt = _halo_specs(s, t)
    row = pl.BlockSpec((MIX_ROWS, s), lambda i: (i, 0))
    return pl.pallas_call(
        body, name=name, grid=(nc // MIX_CHUNKS,),
        in_specs=[pl.BlockSpec((MIX_ROWS, 2 * s), lambda i: (i, 0)), seg(0), seg(1), seg(2), seg(3), seg(4),
                  prev(0), nxt(0), prev(2), nxt(2), row, _full((3, s)), _full((1, s)), _full((1, s))],
        out_specs=[row, row, row, row, row, _full((8, s))],
        out_shape=[jax.ShapeDtypeStruct((t, s), F32)] + [jax.ShapeDtypeStruct((t, s), BF16)] * 4
        + [jax.ShapeDtypeStruct((8, s), F32)],
        compiler_params=_params(("arbitrary",)),
    )(dycat, u, u, u, u, u, u, u, u, u, o, conv_w, cnw, gnw)


def _grad_state_sweep(qkv, do, tabs, n_heads, nx, ncc, name):
    return _pair_sweep((qkv, 0), (do, None), tabs["qf"], tabs["qb"], tabs["cdf"], tabs["cdb"], n_heads, nx, ncc, True, name)


def _mix_bwd_b(u, g, dz, db, drz, qkv, do, sf, sb, gf, gb, tabs, cos, sa, sb_tab, conv_w,
               n_heads, nx, ncc, name):
    _, t, s = u.shape
    nc = nx + ncc
    c = CHUNK
    k_scale = HEAD_DIM ** -0.5

    def body(h_ref, c_ref, g_ref, gp_ref, gn_ref, dz_ref, db_ref, drz_ref, q_ref, k_ref, v_ref, do_ref,
             sf_ref, sb_ref, gf_ref, gb_ref, dc_t, dlf_t, dlb_t, qft, kft, qbt, kbt, cdf, cdb, lg_ref,
             cos_ref, sa_ref, sb_ref2, w_ref, du_ref, dw_ref, dlg_ref):
        i = pl.program_id(0)

        @pl.when(i == 0)
        def _():
            dw_ref[...] = jnp.zeros_like(dw_ref)
            dlg_ref[...] = jnp.zeros_like(dlg_ref)

        has_prev, has_next = _neighbours(i, nx, nc)
        gv = g_ref[...]
        gm, gp = _shifted(gv, gp_ref[7:8], gn_ref[0:1], has_prev, has_next)
        w = w_ref[...]
        da = w[0:1] * gp + w[1:2] * gv + w[2:3] * gm
        hh, cc = h_ref[...].astype(F32), c_ref[...].astype(F32)
        du_ref[0] = (da * cc).astype(BF16)
        du_ref[2] = (da * hh).astype(BF16)
        a = cc * hh
        dw_ref[0:1, :] += jnp.sum(a * gp, axis=0, keepdims=True)
        dw_ref[1:2, :] += jnp.sum(a * gv, axis=0, keepdims=True)
        dw_ref[2:3, :] += jnp.sum(a * gm, axis=0, keepdims=True)
        du_ref[1] = db_ref[...]
        du_ref[3] = dz_ref[...]
        du_ref[7] = drz_ref[...]

        pos = lax.broadcasted_iota(jnp.int32, (c, HEAD_DIM), 0).astype(F32)
        w_q_f, w_q_b, w_k_f = pos + 1.0, c - pos, c - 1.0 - pos
        row8 = lax.broadcasted_iota(jnp.int32, (8, HEAD_DIM), 0)
        lane8 = lax.broadcasted_iota(jnp.int32, (8, HEAD_DIM), 1)
        dlg = jnp.zeros((8, HEAD_DIM), F32)
        for sub, h in [(sub, h) for sub in range(MIX_CHUNKS) for h in range(n_heads)]:
            rows = pl.ds(sub * c, c)
            co, ra, rb = cos_ref[rows, :], sa_ref[rows, :], sb_ref2[rows, :]
            sl = pl.ds(h * HEAD_DIM, HEAD_DIM)
            q, k, v, do = q_ref[rows, sl], k_ref[rows, sl], v_ref[rows, sl], do_ref[rows, sl]
            qf, kf, dof = q.astype(F32), k.astype(F32), do.astype(F32)
            s_f, s_b, g_f, g_b = sf_ref[sub, h], sb_ref[sub, h], gf_ref[sub, h], gb_ref[sub, h]
            p = _dot(q, k, NT)
            pd = _dot(do, v, NT)
            pdd = (pd * dc_t[h]).astype(BF16)
            dq = _dot(pdd, k, NN)
            dk = _dot(pdd, q, TN)
            dv = _dot((p * dc_t[h]).astype(BF16), do, TN)
            dq_f = _dot((dof * qft[h]).astype(BF16), s_f, NT)
            dq_b = _dot((dof * qbt[h]).astype(BF16), s_b, NT)
            dk_f = _dot(v, g_f, NT) * kft[h]
            dk_b = _dot(v, g_b, NT) * kbt[h]
            dv += _dot((kf * kft[h]).astype(BF16), g_f, NN) + _dot((kf * kbt[h]).astype(BF16), g_b, NN)
            ppd = p * pd
            cd_f, cd_b = cdf[h][0:1, :], cdb[h][0:1, :]
            t_f = _sum_all(dlf_t[h] * ppd + w_q_f * qf * dq_f + w_k_f * kf * dk_f
                           + float(c) * (cd_f * (g_f.astype(F32) * s_f.astype(F32))))
            t_b = _sum_all(dlb_t[h] * ppd + w_q_b * qf * dq_b + pos * kf * dk_b
                           + float(c) * (cd_b * (g_b.astype(F32) * s_b.astype(F32))))
            dlg += jnp.where((row8 == 0) & (lane8 == h), t_f, 0.0) + jnp.where((row8 == 1) & (lane8 == h), t_b, 0.0)
            du_ref[4, rows, sl] = _rope_bwd(dq + dq_f + dq_b, co, ra, rb).astype(BF16)
            du_ref[5, rows, sl] = (_rope_bwd(dk + dk_f + dk_b, co, ra, rb) * k_scale).astype(BF16)
            du_ref[6, rows, sl] = dv.astype(BF16)
        dlg_ref[...] += dlg

        @pl.when(i == nc // MIX_CHUNKS - 1)
        def _():
            dlg_ref[...] = dlg_ref[...] * lg_ref[...]

    def seg(gi):
        return pl.BlockSpec((None, MIX_ROWS, s), lambda i: (gi, i, 0))

    per = MIX_ROWS // 8
    n8 = t // 8
    row = pl.BlockSpec((MIX_ROWS, s), lambda i: (i, 0))
    st = pl.BlockSpec((MIX_CHUNKS, n_heads, HEAD_DIM, HEAD_DIM), lambda i: (i, 0, 0, 0))
    tab = pl.BlockSpec((MIX_ROWS, HEAD_DIM), lambda i: (i, 0))
    hc = _full((n_heads, c, HEAD_DIM))
    cc_ = _full((n_heads, c, c))
    h8 = _full((n_heads, 8, HEAD_DIM))
    return pl.pallas_call(
        body, name=name, grid=(nc // MIX_CHUNKS,),
        in_specs=[seg(0), seg(2), row,
                  pl.BlockSpec((8, s), lambda i: (jnp.maximum(i * per - 1, 0), 0)),
                  pl.BlockSpec((8, s), lambda i: (jnp.minimum((i + 1) * per, n8 - 1), 0)),
                  row, row, row, seg(0), seg(1), seg(2), row, st, st, st, st, cc_, cc_, cc_, hc, hc, hc, hc, h8, h8,
                  _full((8, HEAD_DIM)), tab, tab, tab, _full((3, s))],
        out_specs=[pl.BlockSpec((8, MIX_ROWS, s), lambda i: (0, i, 0)), _full((8, s)), _full((8, HEAD_DIM))],
        out_shape=[jax.ShapeDtypeStruct((8, t, s), BF16), jax.ShapeDtypeStruct((8, s), F32),
                   jax.ShapeDtypeStruct((8, HEAD_DIM), F32)],
        compiler_params=_params(("arbitrary",)),
    )(u, u, g, g, g, dz, db, drz, qkv, qkv, qkv, do, sf, sb, gf, gb, tabs["dc"], tabs["dlf"], tabs["dlb"],
      tabs["qf"], tabs["kf"], tabs["qb"], tabs["kb"], tabs["cdf"], tabs["cdb"], tabs["lg"], cos, sa, sb_tab, conv_w)


def _in_proj_bwd(du, wgs, tm, gs, name, after=()):
    n_seg, t, s = du.shape
    d = wgs[0].shape[1]
    n_w = len(wgs)
    widths = [w.shape[2] for w in wgs]
    assert sum(widths) == s

    def body(a_ref, *rest):
        w_refs, o_ref = rest[:n_w], rest[-1]
        g = pl.program_id(1)
        part = None
        for j in range(gs):
            col = 0
            for w_ref, width in zip(w_refs, widths):
                term = _dot(a_ref[j, :, col:col + width], w_ref[j], NT)
                part = term if part is None else part + term
                col += width

        @pl.when(g == 0)
        def _():
            o_ref[...] = part

        @pl.when(g > 0)
        def _():
            o_ref[...] += part

    return pl.pallas_call(
        body, name=name, grid=(t // tm, n_seg // gs),
        in_specs=[pl.BlockSpec((gs, tm, s), lambda i, g: (g, i, 0))]
        + [pl.BlockSpec((gs, d, width), lambda i, g: (g, 0, 0)) for width in widths]
        + [pl.BlockSpec(memory_space=pl.ANY)] * len(after),
        out_specs=pl.BlockSpec((tm, d), lambda i, g: (i, 0)),
        out_shape=jax.ShapeDtypeStruct((t, d), F32),
        compiler_params=_params(("parallel", "arbitrary")))(du, *wgs, *after)


def _prenorm_bwd_first(dhx, x, ctx, dxo, nw, mod, name):
    n_lat, d = x.shape
    t = n_lat + ctx.shape[0]
    nxb = n_lat // ROW_TILE

    def body(dh_ref, x_ref, c_ref, dxo_ref, nw_ref, mod_ref, dx_ref, acc_ref):
        i = pl.program_id(0)

        @pl.when(i == 0)
        def _():
            acc_ref[...] = jnp.zeros_like(acc_ref)

        ctx = i >= nxb
        m = mod_ref[...]
        scale1 = 1.0 + jnp.where(ctx, m[4:5], m[1:2])
        x = jnp.where(ctx, c_ref[...], x_ref[...])
        nw_v = nw_ref[...]
        r = lax.rsqrt(jnp.mean(x * x, axis=-1, keepdims=True) + EPS)
        xn = x * r
        dh = dh_ref[...]
        dshift = jnp.sum(dh, axis=0, keepdims=True)
        dscale = jnp.sum(dh * (xn * nw_v), axis=0, keepdims=True)
        acc_ref[6:7, :] += jnp.sum(dh * scale1 * xn, axis=0, keepdims=True)
        dxn = dh * (nw_v * scale1)
        dx = dxo_ref[...] + r * (dxn - xn * jnp.mean(dxn * xn, axis=-1, keepdims=True))

        @pl.when(i < nxb)
        def _():
            acc_ref[0:1, :] += dshift
            acc_ref[1:2, :] += dscale
            dx_ref[...] = dx

        @pl.when(i >= nxb)
        def _():
            acc_ref[3:4, :] += dshift
            acc_ref[4:5, :] += dscale

    row = pl.BlockSpec((ROW_TILE, d), lambda i: (i, 0))
    lat, cx = _split_rows(nxb, t // ROW_TILE)(d)
    acc = _full((8, d))
    return pl.pallas_call(body, name=name, grid=(t // ROW_TILE,),
                          in_specs=[row, lat, cx, row, _full((1, d)), acc],
                          out_specs=[lat, acc],
                          out_shape=[jax.ShapeDtypeStruct((n_lat, d), F32), jax.ShapeDtypeStruct((8, d), F32)],
                          compiler_params=_params(("arbitrary",)))(dhx, x, ctx, dxo, nw, mod)


def _adamw(g, w, m, v):
    m = ADAM_B1 * m + (1.0 - ADAM_B1) * g
    v = ADAM_B2 * v + (1.0 - ADAM_B2) * jnp.square(g)
    m_hat = m / (1.0 - ADAM_B1 ** ADAM_STEP)
    v_hat = v / (1.0 - ADAM_B2 ** ADAM_STEP)
    delta = -ADAM_LR * (m_hat / (jnp.sqrt(v_hat) + ADAM_EPS) + ADAM_WD * w)
    return delta, m, v


def _sum_adamw(parts, w, m, v, name, row0=0, into=None):
    n_p, r, n = parts.shape
    r_all = w.shape[0]
    part_block_bytes = 4 * 1024 * 1024
    br = 8
    for cand in (512, 256, 128, 64, 32, 16):
        if r % cand == 0 and row0 % cand == 0 and n_p * cand * n * parts.dtype.itemsize <= part_block_bytes:
            br = cand
            break
    blk0 = row0 // br

    def body(p_ref, w_ref, m_ref, v_ref, *rest):
        g_out, d_out, m_out, v_out = rest[-4:]
        g = p_ref[0].astype(F32)
        for j in range(1, n_p):
            g = g + p_ref[j].astype(F32)
        g_out[...] = g
        d_out[...], m_out[...], v_out[...] = _adamw(g, w_ref[...], m_ref[...], v_ref[...])

    row = pl.BlockSpec((br, n), lambda i: (i + blk0, 0))
    kept = [] if into is None else list(into)
    return pl.pallas_call(body, name=name, grid=(r // br,),
                          in_specs=[pl.BlockSpec((n_p, br, n), lambda i: (0, i, 0)), row, row, row]
                          + [pl.BlockSpec(memory_space=pl.ANY)] * len(kept),
                          out_specs=[row] * 4, out_shape=[jax.ShapeDtypeStruct((r_all, n), F32)] * 4,
                          input_output_aliases={4 + j: j for j in range(len(kept))},
                          compiler_params=_params(("parallel",)))(parts, w, m, v, *kept)


def _rope_tables(n_lat, n_ctx):
    f = HEAD_DIM // 4
    rows = n_lat // GRID_W
    inv = ROPE_BASE ** (-jnp.arange(f, dtype=F32) / f)
    ang_r = jnp.arange(rows).astype(F32)[:, None] * inv[None, :]
    ang_c = jnp.arange(GRID_W).astype(F32)[:, None] * inv[None, :]

    cr, sr, cc, sc = jnp.cos(ang_r), jnp.sin(ang_r), jnp.cos(ang_c), jnp.sin(ang_c)
    zr, zc = jnp.zeros_like(cr), jnp.zeros_like(cc)

    def table(by_row, by_col):
        both = by_row[:, None, :] + by_col[None, :, :]
        return both.reshape(n_lat, HEAD_DIM)

    cos = table(jnp.concatenate([cr, cr, zr, zr], axis=-1), jnp.concatenate([zc, zc, cc, cc], axis=-1))
    sa = table(jnp.concatenate([-sr, zr, zr, zr], axis=-1), jnp.concatenate([zc, zc, -sc, zc], axis=-1))
    sb = table(jnp.concatenate([zr, sr, zr, zr], axis=-1), jnp.concatenate([zc, zc, zc, sc], axis=-1))
    pad = jnp.zeros((n_ctx, HEAD_DIM), F32)
    return (jnp.concatenate([cos, pad + 1.0], axis=0), jnp.concatenate([sa, pad], axis=0),
            jnp.concatenate([sb, pad], axis=0))


def _pad_rows(a, rows):
    return jnp.pad(a, [(0, rows - a.shape[0])] + [(0, 0)] * (a.ndim - 1))


def _pad_cols(a, cols):
    return jnp.pad(a, [(0, 0), (0, cols - a.shape[1])])


def kernel(x, c, ctx, c_ctx, norm_w, w_mod, b_mod, w_in, conv_w, conv_norm_w, ret_norm_w, ret_decay_f, ret_decay_b, w_out, final_norm_w, loss_target, m_c_ctx, m_norm_w, m_w_mod, m_b_mod, m_w_in, m_conv_w, m_conv_norm_w, m_ret_norm_w, m_ret_decay_f, m_ret_decay_b, m_w_out, m_final_norm_w, v_c_ctx, v_norm_w, v_w_mod, v_b_mod, v_w_in, v_conv_w, v_conv_norm_w, v_ret_norm_w, v_ret_decay_f, v_ret_decay_b, v_w_out, v_final_norm_w):
    depth = norm_w.shape[0]
    n_lat, d = x.shape[1], x.shape[2]
    n_ctx = ctx.shape[1]
    s = d // 2
    n_heads = ret_decay_f.shape[1]
    nx, ncc = n_lat // CHUNK, n_ctx // CHUNK
    n_mod = w_mod.shape[2]
    n_cw = conv_w.shape[2]
    r_out = w_out.shape[1]
    assert s == n_heads * HEAD_DIM and w_in.shape[2] == s and N_DEV * r_out == d
    assert n_lat % ROW_TILE == 0 and n_ctx % ROW_TILE == 0 and 3 * depth * n_cw <= d and d >= 3 * n_mod // 3
    me = 4 * lax.axis_index("x") + 2 * lax.axis_index("y") + lax.axis_index("c")

    w_in_bf = [w_in[l].astype(BF16) for l in range(depth)]
    w_out_bf = [w_out[l].astype(BF16) for l in range(depth)]

    first = jnp.concatenate([c.reshape(1, d), _pad_cols(conv_w.reshape(1, -1), d), jnp.zeros((6, d), F32)], axis=0)
    (first_g,) = _all_gather([first], "gather_cond")
    first_g = first_g.reshape(N_DEV, 8, d)
    c_all = first_g[:, 0, :]
    conv_full = first_g[:, 1, :3 * depth * n_cw].reshape(N_DEV, depth, 3, n_cw)
    conv_full = conv_full.transpose(1, 2, 0, 3).reshape(depth, 3, N_DEV * n_cw)
    c9 = jnp.concatenate([c_all, c_ctx.reshape(1, d), jnp.zeros((7, d), F32)], axis=0)

    b_sh = lax.dynamic_slice(b_mod, (0, me * n_mod), (depth, n_mod))
    mod_sh = jnp.concatenate([_mod_rows(c9, w_mod[l], b_sh[l:l + 1], f"mod_rows_l{l}") for l in range(depth)], axis=0)
    (mod_g,) = _all_gather([mod_sh], "gather_mod")
    mod_g = mod_g.reshape(N_DEV, depth, 16, n_mod)
    mods = []
    for l in range(depth):
        mine = lax.dynamic_index_in_dim(mod_g[:, l], me, axis=1, keepdims=False).reshape(3, d)
        cx = mod_g[:, l, 8, :].reshape(3, d)
        mods.append(jnp.concatenate([mine, cx, jnp.zeros((2, d), F32)], axis=0))

    halves = [w_in_bf[0][:, :s // 2], w_in_bf[0][:, s // 2:]]
    near, order = [], [mod_g]
    for j, part in enumerate(halves):
        near.append(_push_start([part], [_landing(part, me)], "near", f"w_in0_start_{j}", after=order))
        order = near[-1][4:]
    pending = []
    for k in range(depth):
        srcs = [w_out_bf[k]] + ([w_in_bf[k]] if k > 0 else [])
        started = _push_start(srcs, [_landing(a, me) for a in srcs], "gather", f"weights_start_l{k}", after=order)
        pending.append(started[:4])
        order = started[4:]
    w_in_g = [None] * depth
    w_out_g = [None] * depth

    cos, sa, sb_tab = _rope_tables(n_lat, n_ctx)
    t_all = n_lat + n_ctx

    saved = []
    xt = hx_next = None
    for l in range(depth):
        tiles = _tiles(l, t_all, d)
        names = ["dc", "dlf", "dlb", "qf", "kf", "qb", "kb", "cdf", "cdb", "lg"]
        dec = jnp.stack([ret_decay_f[l], ret_decay_b[l]], axis=0)
        tabs = dict(zip(names, _decay_tables(dec, n_heads, f"decay_tables_l{l}")))
        if l == 0:
            hx = _prenorm_first(x[0], ctx[0], norm_w[0:1], mods[0], "prenorm_l0", after=order)
            gathered, out, after = [], None, hx
            for j in range(2):
                (landed,) = _push_wait(*near[j][:4], "near", after, f"w_in0_wait_{j}")
                relay = _push_start([], [landed], "relay", f"w_in0_relay_start_{j}")
                (landed,) = _push_wait(*relay[:4], "relay", relay[4], f"w_in0_relay_wait_{j}")
                gathered.append(landed)
                out = _in_proj(hx, landed, cos, sa, sb_tab, s, j, tiles["in_tm_half"], f"in_proj_l0_{j}", into=out)
                after = out[0]
            u, qkv = out
            w_in_g[0] = gathered
        else:
            landed = _push_wait(*pending[l], "gather", xt, f"weights_wait_l{l}")
            w_out_g[l], w_in_g[l] = landed[0].reshape(d, d), [landed[1]]
            hx = hx_next
            u, qkv = _in_proj(hx, w_in_g[l][0], cos, sa, sb_tab, s, 0, tiles["in_tm"], f"in_proj_l{l}")
        sf, sb = _state_sweep(qkv, tabs, n_heads, nx, ncc, f"state_sweep_l{l}")
        ycat, o = _mix_fwd(u, qkv, sf, sb, tabs, conv_full[l], conv_norm_w[l:l + 1], ret_norm_w[l:l + 1],
                           n_heads, nx, ncc, f"mix_fwd_l{l}")
        if l == 0:
            (landed,) = _push_wait(*pending[0], "gather", ycat, "weights_wait_l0")
            w_out_g[0] = landed.reshape(d, d)
        m_res = x_new = None
        if l < depth - 1:
            res = (x[0], ctx[0]) if l == 0 else (xt,)
            m_res, x_new, hx_next = _out_proj_prenorm(ycat, w_out_g[l], res, mods[l], norm_w[l + 1:l + 2], mods[l + 1],
                                                      n_lat, f"out_proj_l{l}")
        else:
            dxt, dm, loss_blk, dfnw, gate_acc = _out_proj_loss(ycat, w_out_g[l], xt, mods[l], loss_target[0],
                                                               final_norm_w.reshape(1, d), n_lat, f"out_proj_loss_l{l}")
        saved.append(dict(tabs=tabs, xt=xt, hx=hx, u=u, qkv=qkv, sf=sf, sb=sb, ycat=ycat, o=o, m=m_res, tiles=tiles))
        xt = x_new

    loss = lax.psum(loss_blk[0, 0], MESH_AXES)

    dmod_x, dmod_c, dnw, dcnw, dgnw, dconv, ddec, dwin, dwout = [], [], [], [], [], [], [], [], []
    started_token = ()
    for l in reversed(range(depth)):
        sv = saved[l]
        tiles = sv["tiles"]
        dycat = _matmul_nt(dm, w_out_g[l], tiles["ob_tn"], f"out_proj_bwd_l{l}", after=started_token)
        dwout.append(_weight_grad(sv["ycat"], dm.reshape(1, *dm.shape), tiles["wo_bm"], _mm_rows(t_all),
                                  f"w_out_grad_l{l}")[0])
        g, dz, db, drz, do, norm_acc = _mix_bwd_a(dycat, sv["u"], sv["o"], conv_full[l], conv_norm_w[l:l + 1],
                                                   ret_norm_w[l:l + 1], n_heads, nx, ncc, f"mix_bwd_a_l{l}")
        gf, gb = _grad_state_sweep(sv["qkv"], do, sv["tabs"], n_heads, nx, ncc, f"grad_state_sweep_l{l}")
        du, conv_acc, dlg = _mix_bwd_b(sv["u"], g, dz, db, drz, sv["qkv"], do, sv["sf"], sv["sb"],
                                       gf, gb, sv["tabs"], cos, sa, sb_tab, conv_full[l], n_heads, nx, ncc,
                                       f"mix_bwd_b_l{l}")
        gate_acc_l = gate_acc
        if l > 0:
            dhx = _in_proj_bwd(du, w_in_g[l], tiles["bwd_tm"], tiles["bwd_gs"], f"in_proj_bwd_l{l}")
            below = (saved[l - 1]["m"], mods[l - 1])
            dwin_l, dxt, pre_acc, dm, gate_acc = _weight_grad_beside_prenorm_bwd(
                sv["hx"], du, dhx, sv["xt"], dxt, norm_w[l:l + 1], mods[l], below, n_lat, f"w_in_grad_l{l}")
        else:
            dwin_l = _weight_grad(sv["hx"], du, tiles["wg_bm"], tiles["wg_bt"], f"w_in_grad_l{l}")
        srcs = [dwin_l, dwout[-1].reshape(N_DEV, r_out, d)]
        lands = [_landing(lax.dynamic_index_in_dim(a, me, axis=0, keepdims=False), me) for a in srcs]
        started = _push_start(srcs, lands, "scatter", f"grads_start_l{l}")
        dwin.append(started[:4])
        started_token = started[4:]
        if l == 0:
            dhx = _in_proj_bwd(du, w_in_g[l], tiles["bwd_tm"], tiles["bwd_gs"], f"in_proj_bwd_l{l}", after=started[4:])
            dxt, pre_acc = _prenorm_bwd_first(dhx, x[0], ctx[0], dxt, norm_w[l:l + 1], mods[l], f"prenorm_bwd_l{l}")
        dmod_x.append(jnp.concatenate([pre_acc[0], pre_acc[1], gate_acc_l[2]]))
        dmod_c.append(jnp.concatenate([pre_acc[3], pre_acc[4], gate_acc_l[5]]))
        dnw.append(pre_acc[6])
        dcnw.append(norm_acc[0])
        dgnw.append(norm_acc[1])
        dconv.append(conv_acc[0:3])
        ddec.append(dlg[0:2, :n_heads])
    for lst in (dmod_x, dmod_c, dnw, dcnw, dgnw, dconv, ddec, dwin, dwout):
        lst.reverse()
    grad_x = dxt.reshape(1, n_lat, d)

    rows = []
    for l in range(depth):
        rows += [dmod_x[l], dmod_c[l]]
    (dmod_g,) = _all_gather([_pad_rows(jnp.stack(rows, axis=0), 8)], "gather_dmod")
    dmod_g = dmod_g.reshape(N_DEV, 8, 3 * d)
    mine_cols = lax.dynamic_slice(dmod_g, (0, 0, me * n_mod), (N_DEV, 8, n_mod))
    g_wmod, dcc = [], jnp.zeros((d,), F32)
    for l in range(depth):
        gw, dc_part = _mod_grads(mine_cols[:, 2 * l], mine_cols[:, 2 * l + 1], c9, w_mod[l], f"mod_grads_l{l}")
        g_wmod.append(gw)
        dcc = dcc + dc_part[0]

    n_small = 16
    small = jnp.concatenate([
        jnp.stack(dnw, axis=0),
        jnp.concatenate(dcnw).reshape(1, -1),
        jnp.concatenate(dgnw).reshape(1, -1),
        dfnw[0:1],
        dcc.reshape(1, d),
        jnp.stack(dconv, axis=0).reshape(-1, d),
        _pad_cols(jnp.stack(ddec, axis=0).reshape(1, -1), d),
    ], axis=0)
    assert depth * s == d and small.shape[0] <= n_small
    n_rows = small.shape[0]
    (small_g,) = _all_gather([_pad_rows(small, n_small)], "gather_small")
    small_g = small_g.reshape(N_DEV, n_small, d)

    def pack_small(nw_, cn_, gn_, fn_, cc_, df_, db_):
        return _pad_rows(jnp.concatenate([
            nw_, cn_.reshape(1, -1), gn_.reshape(1, -1), fn_.reshape(1, d), cc_.reshape(1, d),
            jnp.zeros((n_rows - depth - 5, d), F32),
            _pad_cols(jnp.stack([df_, db_], axis=1).reshape(1, -1), d)], axis=0), n_small)

    w_s = pack_small(norm_w, conv_norm_w, ret_norm_w, final_norm_w, c_ctx, ret_decay_f, ret_decay_b)
    m_s = pack_small(m_norm_w, m_conv_norm_w, m_ret_norm_w, m_final_norm_w, m_c_ctx, m_ret_decay_f, m_ret_decay_b)
    v_s = pack_small(v_norm_w, v_conv_norm_w, v_ret_norm_w, v_final_norm_w, v_c_ctx, v_ret_decay_f, v_ret_decay_b)
    small_out = _sum_adamw(small_g, w_s, m_s, v_s, "adamw_small")

    def unpack_small(a):
        nw_ = a[0:depth]
        cn_ = a[depth].reshape(depth, s)
        gn_ = a[depth + 1].reshape(depth, s)
        fn_ = a[depth + 2]
        cc_ = a[depth + 3]
        dd = a[n_rows - 1, :depth * 2 * n_heads].reshape(depth, 2, n_heads)
        return dict(c_ctx=cc_, norm_w=nw_, conv_norm_w=cn_, ret_norm_w=gn_, ret_decay_f=dd[:, 0], ret_decay_b=dd[:, 1],
                    final_norm_w=fn_)

    res = {}
    for kind, arr in zip(("grad", "delta", "m", "v"), small_out):
        for k_, val in unpack_small(arr).items():
            res[(kind, k_)] = val

    bm_parts = jnp.concatenate([dmod_g[:, 0:2 * depth:2].reshape(N_DEV, depth, 3 * d),
                                dmod_g[:, 1:2 * depth:2].reshape(N_DEV, depth, 3 * d)], axis=0)
    bm_parts = jnp.concatenate([bm_parts, jnp.zeros((2 * N_DEV, 8 - depth, 3 * d), F32)], axis=1)
    pad8 = lambda a: _pad_rows(a, 8)
    bm_out = _sum_adamw(bm_parts, pad8(b_mod), pad8(m_b_mod), pad8(v_b_mod), "adamw_b_mod")
    for kind, arr in zip(("grad", "delta", "m", "v"), bm_out):
        res[(kind, "b_mod")] = arr[:depth]

    conv_rows = small_g[:, depth + 4:depth + 4 + 3 * depth * s // d].reshape(N_DEV, depth * 3, s)
    conv_mine = lax.dynamic_slice(conv_rows, (0, 0, me * n_cw), (N_DEV, depth * 3, n_cw))
    conv_mine = jnp.concatenate([conv_mine, jnp.zeros((N_DEV, 8 - depth * 3, n_cw), F32)], axis=1)
    cw2 = lambda a: _pad_rows(a.reshape(depth * 3, n_cw), 8)
    cw_out = _sum_adamw(conv_mine, cw2(conv_w), cw2(m_conv_w), cw2(v_conv_w), "adamw_conv_w")
    for kind, arr in zip(("grad", "delta", "m", "v"), cw_out):
        res[(kind, "conv_w")] = arr[:depth * 3].reshape(depth, 3, n_cw)

    wm_out = _sum_adamw(jnp.stack(g_wmod, axis=0).reshape(1, depth * d, n_mod), w_mod.reshape(depth * d, n_mod),
                        m_w_mod.reshape(depth * d, n_mod), v_w_mod.reshape(depth * d, n_mod), "adamw_w_mod")
    for kind, arr in zip(("grad", "delta", "m", "v"), wm_out):
        res[(kind, "w_mod")] = arr.reshape(depth, d, n_mod)

    wi_out = wo_out = None
    after = wm_out[0]
    for l in reversed(range(depth)):
        win_parts, wout_parts = _push_wait(*dwin[l], "scatter", after, f"grads_wait_l{l}")
        wi_out = _sum_adamw(win_parts, w_in.reshape(depth * d, s), m_w_in.reshape(depth * d, s),
                            v_w_in.reshape(depth * d, s), f"adamw_w_in_l{l}", row0=l * d, into=wi_out)
        wo_out = _sum_adamw(wout_parts, w_out.reshape(depth * r_out, d), m_w_out.reshape(depth * r_out, d),
                            v_w_out.reshape(depth * r_out, d), f"adamw_w_out_l{l}", row0=l * r_out, into=wo_out)
        after = wo_out[0]
    for kind, arr in zip(("grad", "delta", "m", "v"), wi_out):
        res[(kind, "w_in")] = arr.reshape(depth, d, s)
    for kind, arr in zip(("grad", "delta", "m", "v"), wo_out):
        res[(kind, "w_out")] = arr.reshape(depth, r_out, d)

    order = ["c_ctx", "norm_w", "w_mod", "b_mod", "w_in", "conv_w", "conv_norm_w", "ret_norm_w", "ret_decay_f",
             "ret_decay_b", "w_out", "final_norm_w"]
    outs = [loss, grad_x]
    for kind in ("grad", "delta", "m", "v"):
        outs += [res[(kind, k_)] for k_ in order]
    return tuple(outs)
```

```python
import jax
import jax.numpy as jnp
from jax import lax
from jax.experimental import pallas as pl
from jax.experimental.pallas import tpu as pltpu

F32 = jnp.float32
BF16 = jnp.bfloat16

EPS = 1e-6
CHUNK = 128
HEAD_DIM = 128
GRID_W = 64
ROPE_BASE = 10000.0
N_DEV = 8
ADAM_LR, ADAM_B1, ADAM_B2, ADAM_EPS, ADAM_WD, ADAM_STEP = 0.001, 0.9, 0.999, 1e-08, 0.01, 10

ROW_TILE = 256
V7X_VMEM_LIMIT = 56 * 1024 * 1024
MESH_AXES = ("x", "y", "c")

NN = ((1,), (0,))
NT = ((1,), (1,))
TN = ((0,), (0,))


def _dot(a, b, dims):
    return lax.dot_general(a, b, (dims, ((), ())), preferred_element_type=F32)


def _params(sem=None):
    if sem is None:
        return pltpu.CompilerParams(vmem_limit_bytes=V7X_VMEM_LIMIT)
    return pltpu.CompilerParams(dimension_semantics=sem, vmem_limit_bytes=V7X_VMEM_LIMIT)


def _silu(z):
    return z * jax.nn.sigmoid(z)


def _dsilu(z):
    s = jax.nn.sigmoid(z)
    return s * (1.0 + z * (1.0 - s))


def _silu_and_slope(z):
    s = jax.nn.sigmoid(z)
    return z * s, s * (1.0 + z * (1.0 - s))


def _sum_all(a):
    return jnp.sum(jnp.sum(a, axis=1, keepdims=True), axis=0, keepdims=True)


def _mm_rows(t):
    return 768 if t % 768 == 0 else ROW_TILE


def _rows_or(t, rows):
    return rows if t % rows == 0 else _mm_rows(t)


def _tiles(layer, t, d):
    return dict(in_tm=_rows_or(t, 1408), in_tm_half=_rows_or(t, 2112), bwd_gs=2, wg_bm=d, wg_bt=_mm_rows(t), wo_bm=d,
                ob_tn=d, bwd_tm=_rows_or(t, 1056) if layer > 0 else _mm_rows(t))


def _full(shape):
    n = len(shape)
    return pl.BlockSpec(shape, lambda *_: (0,) * n)


def _peers(x, y, c):
    return [(x, y, 1 - c), (1 - x, y, c), (x, 1 - y, c), (1 - x, 1 - y, c),
            (1 - x, y, 1 - c), (x, 1 - y, 1 - c), (1 - x, 1 - y, 1 - c)]


def _lin(p):
    return 4 * p[0] + 2 * p[1] + p[2]


def _all_gather(arrays, name):
    n_arr = len(arrays)
    space = pltpu.VMEM

    def body(*refs):
        ins, outs = refs[:n_arr], refs[n_arr:2 * n_arr]
        send_sems, recv_sems, local_sems = refs[2 * n_arr:]
        x, y, c = lax.axis_index("x"), lax.axis_index("y"), lax.axis_index("c")
        me, sibling = (x, y, c), (x, y, 1 - c)
        chips = [(1 - x, y), (x, 1 - y), (1 - x, 1 - y)]
        every = []
        locals_ = []
        for a in range(n_arr):
            m_per = ins[a].shape[0]
            out_ref = outs[a]

            def rows(p, out_ref=out_ref, m_per=m_per):
                return out_ref.at[pl.ds(_lin(p) * m_per, m_per), :]

            def copy(k, block, to, src=None, a=a, rows=rows):
                return pltpu.make_async_remote_copy(
                    src_ref=rows(block) if src is None else src, dst_ref=rows(block),
                    send_sem=send_sems.at[a, k], recv_sem=recv_sems.at[a, k],
                    device_id=to, device_id_type=pl.DeviceIdType.MESH)

            mine = pltpu.make_async_copy(ins[a], rows(me), local_sems.at[a])
            mine.start()
            locals_.append(mine)
            first = [copy(0, me, sibling, src=ins[a])]
            first += [copy(1 + j, me, (*chip, c), src=ins[a]) for j, chip in enumerate(chips)]
            for cp in first:
                cp.start()
            every.append((copy, first))
        sends = []
        for a in range(n_arr):
            copy, first = every[a]
            passed = [copy(4 + j, (*chip, c), sibling) for j, chip in enumerate(chips)]
            for j, chip in enumerate(chips):
                copy(1 + j, (*chip, c), me).wait_recv()
                passed[j].start()
            sends += first + passed
        for a in range(n_arr):
            copy, _ = every[a]
            copy(0, sibling, me).wait_recv()
            for j, chip in enumerate(chips):
                copy(4 + j, (*chip, 1 - c), me).wait_recv()
        for cp in sends:
            cp.wait_send()
        for mine in locals_:
            mine.wait()

    outs = pl.pallas_call(
        body, name=name,
        out_shape=[jax.ShapeDtypeStruct((N_DEV * a.shape[0], a.shape[1]), a.dtype) for a in arrays],
        in_specs=[pl.BlockSpec(memory_space=space)] * n_arr,
        out_specs=[pl.BlockSpec(memory_space=space)] * n_arr,
        scratch_shapes=[pltpu.SemaphoreType.DMA((n_arr, 7)), pltpu.SemaphoreType.DMA((n_arr, 7)),
                        pltpu.SemaphoreType.DMA((n_arr,))],
        compiler_params=_params(),
    )(*arrays)
    return list(outs)


_HBM = pl.BlockSpec(memory_space=pltpu.HBM)
_SEM = pl.BlockSpec(memory_space=pltpu.SEMAPHORE)
_DATAFLOW = pltpu.SideEffectType.DATAFLOW_SIDE_EFFECTING


PUSH_COPIES = {"scatter": 7, "gather": 7, "near": 4, "relay": 3}


def _push_copies(src_refs, land_refs, send_sems, recv_sems, mode):
    x, y, c = lax.axis_index("x"), lax.axis_index("y"), lax.axis_index("c")
    me, sibling = (x, y, c), (x, y, 1 - c)
    n_k = PUSH_COPIES[mode]
    out, back = [], []
    if mode == "relay":
        for k, chip in enumerate([(1 - x, y), (x, 1 - y), (1 - x, 1 - y)]):
            for a, land in enumerate(land_refs):
                sems = dict(send_sem=send_sems.at[n_k * a + k], recv_sem=recv_sems.at[n_k * a + k],
                            device_id=sibling, device_id_type=pl.DeviceIdType.MESH)
                mine = land.at[_lin((*chip, c))]
                out.append(pltpu.make_async_remote_copy(src_ref=mine, dst_ref=mine, **sems))
                back.append(pltpu.make_async_remote_copy(src_ref=mine, dst_ref=land.at[_lin((*chip, 1 - c))], **sems))
        return out, back
    for k, peer in enumerate(_peers(x, y, c)[:n_k]):
        for a, (src, land) in enumerate(zip(src_refs, land_refs)):
            sems = dict(send_sem=send_sems.at[n_k * a + k], recv_sem=recv_sems.at[n_k * a + k],
                        device_id=peer, device_id_type=pl.DeviceIdType.MESH)
            mine = src.at[_lin(peer)] if mode == "scatter" else src
            out.append(pltpu.make_async_remote_copy(src_ref=mine, dst_ref=land.at[_lin(me)], **sems))
            back.append(pltpu.make_async_remote_copy(src_ref=mine, dst_ref=land.at[_lin(peer)], **sems))
    return out, back


def _push_start(srcs, lands, mode, name, after=()):
    n_src, n = len(srcs), len(lands)
    n_buf = n_src + n
    n_in = n_buf + len(after)
    n_sem = PUSH_COPIES[mode] * n

    def body(*refs):
        send_sems, recv_sems = refs[n_in], refs[n_in + 1]
        out, _ = _push_copies(refs[:n_src], refs[n_src:n_buf], send_sems, recv_sems, mode)
        for cp in out:
            cp.start()
        token = refs[-1]
        token[...] = jnp.zeros_like(token)

    both = list(srcs) + list(lands)
    res = pl.pallas_call(
        body, name=name,
        out_shape=[pltpu.SemaphoreType.DMA((n_sem,)), pltpu.SemaphoreType.DMA((n_sem,))]
        + [pltpu.HBM(a.shape, a.dtype) for a in both] + [jax.ShapeDtypeStruct((8, 128), F32)],
        in_specs=[_HBM] * n_buf + [pl.BlockSpec(memory_space=pl.ANY)] * len(after),
        out_specs=[_SEM, _SEM] + [_HBM] * n_buf + [pl.BlockSpec(memory_space=pltpu.VMEM)],
        input_output_aliases={i: 2 + i for i in range(n_buf)},
        compiler_params=pltpu.CompilerParams(has_side_effects=_DATAFLOW),
    )(*[pltpu.with_memory_space_constraint(a, pltpu.HBM) for a in both], *after)
    return res[0], res[1], list(res[2:2 + n_src]), list(res[2 + n_src:2 + n_buf]), res[-1]


def _push_wait(send_sems, recv_sems, srcs, lands, mode, after, name):
    n_src, n = len(srcs), len(lands)
    n_buf = n_src + n

    def body(*refs):
        out, back = _push_copies(refs[:n_src], refs[n_src:n_buf], refs[n_buf], refs[n_buf + 1], mode)
        for cp in out:
            cp.wait_send()
        for cp in back:
            cp.wait_recv()

    both = list(srcs) + list(lands)
    res = pl.pallas_call(
        body, name=name,
        out_shape=[pltpu.HBM(a.shape, a.dtype) for a in both],
        in_specs=[_HBM] * n_buf + [_SEM, _SEM, pl.BlockSpec(memory_space=pl.ANY)],
        out_specs=[_HBM] * n_buf,
        input_output_aliases={i: i for i in range(n_buf)},
        compiler_params=pltpu.CompilerParams(has_side_effects=_DATAFLOW),
    )(*both, send_sems, recv_sems, after)
    return list(res[n_src:])


def _landing(own, me):
    zone = lax.empty((N_DEV,) + own.shape, own.dtype)
    return lax.dynamic_update_slice(zone, own[None], (me,) + (0,) * own.ndim)


def _mod_rows(c9, w_mod, b_sh, name):
    n = w_mod.shape[1]

    def body(c_ref, w_ref, b_ref, o_ref):
        s9 = _silu(c_ref[...]).astype(BF16)
        o_ref[...] = _dot(s9, w_ref[...].astype(BF16), NN) + b_ref[...]

    return pl.pallas_call(body, name=name, out_shape=jax.ShapeDtypeStruct((16, n), F32),
                          compiler_params=_params())(c9, w_mod, b_sh)


def _mod_grads(dm_rows, dc_rows, c9, w_mod, name):
    d, n = w_mod.shape

    def body(dm_ref, dc_ref, c_ref, w_ref, gw_ref, dc_out):
        dc = dc_ref[...]
        tot = dc[0:1]
        for j in range(1, N_DEV):
            tot = tot + dc[j:j + 1]
        row = lax.broadcasted_iota(jnp.int32, (8, n), 0)
        lower = jnp.where(row == 0, tot, 0.0)
        dmod9 = jnp.concatenate([dm_ref[...], lower], axis=0).astype(BF16)
        c9v = c_ref[...]
        s9 = _silu(c9v).astype(BF16)
        gw_ref[...] = _dot(s9, dmod9, TN)
        ds = _dot(lower.astype(BF16), w_ref[...].astype(BF16), NT)
        dc_out[...] = ds * _dsilu(c9v[8:16])

    return pl.pallas_call(body, name=name,
                          out_shape=[jax.ShapeDtypeStruct((d, n), F32), jax.ShapeDtypeStruct((8, d), F32)],
                          compiler_params=_params())(dm_rows, dc_rows, c9, w_mod)


def _decay_tables(dec, n_heads, name):
    c = CHUNK

    def body(dec_ref, dc_ref, dlf_ref, dlb_ref, qf_ref, kf_ref, qb_ref, kb_ref, cdf_ref, cdb_ref, lg_ref):
        h = pl.program_id(0)
        d = dec_ref[...]
        lane = lax.broadcasted_iota(jnp.int32, d.shape, 1)
        lg = -jnp.exp(jnp.sum(jnp.where(lane == h, d, 0.0), axis=1, keepdims=True))
        lgf, lgb = lg[0:1], lg[1:2]
        i = lax.broadcasted_iota(jnp.int32, (c, c), 0).astype(F32)
        j = lax.broadcasted_iota(jnp.int32, (c, c), 1).astype(F32)
        diff = i - j
        d_f = jnp.where(diff >= 0, jnp.exp(lgf * jnp.maximum(diff, 0.0)), 0.0)
        d_b = jnp.where(diff <= 0, jnp.exp(lgb * jnp.maximum(-diff, 0.0)), 0.0)
        dc_ref[...] = d_f + d_b
        dlf_ref[...] = diff * d_f
        dlb_ref[...] = -diff * d_b
        pos = lax.broadcasted_iota(jnp.int32, (c, HEAD_DIM), 0).astype(F32)
        qf_ref[...] = jnp.exp(lgf * (pos + 1.0))
        kf_ref[...] = jnp.exp(lgf * (c - 1.0 - pos))
        qb_ref[...] = jnp.exp(lgb * (c - pos))
        kb_ref[...] = jnp.exp(lgb * pos)
        ones = jnp.ones((8, HEAD_DIM), F32)
        cdf_ref[...] = jnp.exp(lgf * float(c)) * ones
        cdb_ref[...] = jnp.exp(lgb * float(c)) * ones

        @pl.when(h == 0)
        def _():
            lg_ref[...] = jnp.zeros_like(lg_ref)

        row8 = lax.broadcasted_iota(jnp.int32, (8, HEAD_DIM), 0)
        lane8 = lax.broadcasted_iota(jnp.int32, (8, HEAD_DIM), 1)
        lg_ref[...] += (jnp.where((row8 == 0) & (lane8 == h), lgf, 0.0)
                        + jnp.where((row8 == 1) & (lane8 == h), lgb, 0.0))

    def per_head(*tail):
        return pl.BlockSpec((None,) + tail, lambda h: (h,) + (0,) * len(tail))

    shapes = [(c, c)] * 3 + [(c, HEAD_DIM)] * 4 + [(8, HEAD_DIM)] * 2
    return pl.pallas_call(
        body, name=name, grid=(n_heads,),
        in_specs=[_full(dec.shape)],
        out_specs=[per_head(*s) for s in shapes] + [_full((8, HEAD_DIM))],
        out_shape=[jax.ShapeDtypeStruct((n_heads,) + s, F32) for s in shapes]
        + [jax.ShapeDtypeStruct((8, HEAD_DIM), F32)],
        compiler_params=_params(("arbitrary",)),
    )(dec)


def _modulate(x, nw, shift, scale):
    r = lax.rsqrt(jnp.mean(x * x, axis=-1, keepdims=True) + EPS)
    return ((x * r) * nw * (1.0 + scale) + shift).astype(BF16)


def _split_rows(nxb, nb):
    def specs(d, step=lambda i: i):
        lat = pl.BlockSpec((ROW_TILE, d), lambda i: (jnp.minimum(step(i), nxb - 1), 0))
        ctx = pl.BlockSpec((ROW_TILE, d), lambda i: (jnp.clip(step(i) - nxb, 0, nb - nxb - 1), 0))
        return lat, ctx
    return specs


def _prenorm_first(x, ctx, nw, mod, name, after=()):
    n_lat, d = x.shape
    t = n_lat + ctx.shape[0]
    nxb = n_lat // ROW_TILE

    def body(x_ref, c_ref, nw_ref, mod_ref, *rest):
        o_ref = rest[-1]
        m = mod_ref[...]
        nw_v = nw_ref[...]

        @pl.when(pl.program_id(0) < nxb)
        def _():
            o_ref[...] = _modulate(x_ref[...], nw_v, m[0:1], m[1:2])

        @pl.when(pl.program_id(0) >= nxb)
        def _():
            o_ref[...] = _modulate(c_ref[...], nw_v, m[3:4], m[4:5])

    lat, cx = _split_rows(nxb, t // ROW_TILE)(d)
    return pl.pallas_call(
        body, name=name, grid=(t // ROW_TILE,),
        in_specs=[lat, cx, _full((1, d)), _full((8, d))] + [pl.BlockSpec(memory_space=pl.ANY)] * len(after),
        out_specs=pl.BlockSpec((ROW_TILE, d), lambda i: (i, 0)), out_shape=jax.ShapeDtypeStruct((t, d), BF16),
        compiler_params=_params(("parallel",)))(x, ctx, nw, mod, *after)


def _rope_fwd(v, cos, sa, sb):
    return v * cos + pltpu.roll(v, 96, 1) * sa + pltpu.roll(v, 32, 1) * sb


def _rope_bwd(g, cos, sa, sb):
    return g * cos + pltpu.roll(g * sa, 32, 1) + pltpu.roll(g * sb, 96, 1)


N_PLAIN = 5
U_DTYPE = BF16


def _in_proj(hx, wg, cos, sa, sb, s, part, tm, name, after=(), into=None):
    t, d = hx.shape
    n_seg, _, n = wg.shape
    nb = t // tm
    k_scale = HEAD_DIM ** -0.5
    kept = [] if into is None else list(into)

    def body(a_ref, w_ref, cos_ref, sa_ref, sb_ref, *rest):
        u_ref, qkv_ref = rest[-2:]
        g = pl.program_id(1)
        acc = _dot(a_ref[...], w_ref[...], NN)

        @pl.when(g < N_PLAIN)
        def _():
            u_ref[...] = acc.astype(U_DTYPE)

        @pl.when(g == N_PLAIN + 2)
        def _():
            qkv_ref[...] = acc.astype(BF16)

        for which, scale in ((N_PLAIN, 1.0), (N_PLAIN + 1, k_scale)):
            @pl.when(g == which)
            def _(scale=scale):
                co, a, b = cos_ref[...], sa_ref[...], sb_ref[...]
                for h in range(n // HEAD_DIM):
                    sl = slice(h * HEAD_DIM, (h + 1) * HEAD_DIM)
                    qkv_ref[:, sl] = (_rope_fwd(acc[:, sl], co, a, b) * scale).astype(BF16)

    def w_seg(g):
        return jnp.where(g < N_PLAIN - 1, g, jnp.where(g == N_PLAIN - 1, n_seg - 1, g - 1))

    def qkv_at(i, g):
        held = (jnp.where(i == 0, 0, 2), jnp.maximum(i - 1, 0))
        return (jnp.where(g < N_PLAIN, held[0], g - N_PLAIN), jnp.where(g < N_PLAIN, held[1], i), part)

    tab = pl.BlockSpec((tm, HEAD_DIM), lambda i, g: (i, 0))
    hbm = pl.BlockSpec(memory_space=pl.ANY)
    return pl.pallas_call(
        body, name=name, grid=(nb, n_seg),
        in_specs=[pl.BlockSpec((tm, d), lambda i, g: (i, 0)), pl.BlockSpec((None, d, n), lambda i, g: (w_seg(g), 0, 0)),
                  tab, tab, tab] + [hbm] * (len(after) + len(kept)),
        out_specs=[pl.BlockSpec((None, tm, n), lambda i, g: (jnp.minimum(g, N_PLAIN - 1), i, part)),
                   pl.BlockSpec((None, tm, n), qkv_at)],
        out_shape=[jax.ShapeDtypeStruct((N_PLAIN, t, s), U_DTYPE), jax.ShapeDtypeStruct((3, t, s), BF16)],
        input_output_aliases={5 + len(after) + j: j for j in range(len(kept))},
        compiler_params=_params(("arbitrary", "arbitrary")))(hx, wg, cos, sa, sb, *after, *kept)


def _pair_sweep(xs, ys, tab_f, tab_b, cdf, cdb, n_heads, nx, ncc, reverse, name):
    t, s = xs[0].shape[-2:]
    nc = nx + ncc
    c = CHUNK
    n_pair = nc // 2
    assert nx % 2 == 0 and ncc % 2 == 0

    def f_pair(i):
        step = n_pair - 1 - i if reverse else i
        return jnp.where(step < ncc // 2, nx // 2 + step, step - ncc // 2)

    def b_pair(i):
        return i if reverse else n_pair - 1 - i

    f_subs = (1, 0) if reverse else (0, 1)
    b_subs = (0, 1) if reverse else (1, 0)

    def body(xf_ref, yf_ref, xb_ref, yb_ref, tf, tb, cdf_ref, cdb_ref, sf_out, sb_out, sf, sb):
        @pl.when(pl.program_id(0) == 0)
        def _():
            sf[...] = jnp.zeros_like(sf)
            sb[...] = jnp.zeros_like(sb)

        for step in range(2):
            for x_ref, y_ref, tab, cd, out, st, sub in ((xf_ref, yf_ref, tf, cdf_ref, sf_out, sf, f_subs[step]),
                                                        (xb_ref, yb_ref, tb, cdb_ref, sb_out, sb, b_subs[step])):
                rows = pl.ds(sub * c, c)
                for h in range(n_heads):
                    sl = pl.ds(h * HEAD_DIM, HEAD_DIM)
                    out[sub, h] = st[h].astype(BF16)
                    xd = (x_ref[rows, sl].astype(F32) * tab[h]).astype(BF16)
                    st[h] = cd[h][0:1, :] * st[h] + _dot(xd, y_ref[rows, sl], TN)

    def spec(arr, pair):
        lead = arr[1]
        if lead is None:
            return pl.BlockSpec((2 * c, s), lambda i: (pair(i), 0))
        return pl.BlockSpec((None, 2 * c, s), lambda i: (lead, pair(i), 0))

    st_blk = (2, n_heads, HEAD_DIM, HEAD_DIM)
    return pl.pallas_call(
        body, name=name, grid=(n_pair,),
        in_specs=[spec(xs, f_pair), spec(ys, f_pair), spec(xs, b_pair), spec(ys, b_pair),
                  _full((n_heads, c, HEAD_DIM)), _full((n_heads, c, HEAD_DIM)),
                  _full((n_heads, 8, HEAD_DIM)), _full((n_heads, 8, HEAD_DIM))],
        out_specs=[pl.BlockSpec(st_blk, lambda i: (f_pair(i), 0, 0, 0)), pl.BlockSpec(st_blk, lambda i: (b_pair(i), 0, 0, 0))],
        out_shape=[jax.ShapeDtypeStruct((nc, n_heads, HEAD_DIM, HEAD_DIM), BF16)] * 2,
        scratch_shapes=[pltpu.VMEM((n_heads, HEAD_DIM, HEAD_DIM), F32)] * 2,
        compiler_params=_params(("arbitrary",)),
    )(xs[0], ys[0], xs[0], ys[0], tab_f, tab_b, cdf, cdb)


def _state_sweep(qkv, tabs, n_heads, nx, ncc, name):
    return _pair_sweep((qkv, 1), (qkv, 2), tabs["kf"], tabs["kb"], tabs["cdf"], tabs["cdb"], n_heads, nx, ncc, False, name)


MIX_CHUNKS = 2
MIX_ROWS = MIX_CHUNKS * CHUNK


HALO = 16


def _halo_specs(s, t):
    per = MIX_ROWS // HALO
    n_halo = t // HALO

    def prev(g):
        return pl.BlockSpec((None, HALO, s), lambda i: (g, jnp.maximum(i * per - 1, 0), 0))

    def nxt(g):
        return pl.BlockSpec((None, HALO, s), lambda i: (g, jnp.minimum((i + 1) * per, n_halo - 1), 0))

    return prev, nxt


def _conv_input(h_ref, c_ref, hp_ref, hn_ref, cp_ref, cn_ref):
    a = c_ref[...].astype(F32) * h_ref[...].astype(F32)
    before = cp_ref[HALO - 1:HALO].astype(F32) * hp_ref[HALO - 1:HALO].astype(F32)
    after = cn_ref[0:1].astype(F32) * hn_ref[0:1].astype(F32)
    return a, before, after


def _shifted(a, before, after, has_prev, has_next):
    rows = a.shape[0]
    rowi = lax.broadcasted_iota(jnp.int32, a.shape, 0)
    am = jnp.where(rowi == 0, jnp.where(has_prev, before, 0.0), pltpu.roll(a, 1, 0))
    ap = jnp.where(rowi == rows - 1, jnp.where(has_next, after, 0.0), pltpu.roll(a, rows - 1, 0))
    return am, ap


def _neighbours(i, nx, nc):
    nxb, ncb = nx // MIX_CHUNKS, nc // MIX_CHUNKS
    return (i != 0) & (i != nxb), (i != nxb - 1) & (i != ncb - 1)


def _mix_fwd(u, qkv, sf, sb, tabs, conv_w, cnw, gnw, n_heads, nx, ncc, name):
    _, t, s = u.shape
    nc = nx + ncc
    c = CHUNK
    assert nx % MIX_CHUNKS == 0 and ncc % MIX_CHUNKS == 0

    def body(h_ref, b_ref, c_ref, z_ref, rz_ref, hp_ref, hn_ref, cp_ref, cn_ref, q_ref, k_ref, v_ref,
             sf_ref, sb_ref, dc_ref, qft, qbt, w_ref, cnw_ref, gnw_ref, y_ref, o_ref):
        i = pl.program_id(0)
        has_prev, has_next = _neighbours(i, nx, nc)
        a, before, after = _conv_input(h_ref, c_ref, hp_ref, hn_ref, cp_ref, cn_ref)
        am, ap = _shifted(a, before, after, has_prev, has_next)
        w = w_ref[...]
        y0 = w[0:1] * am + w[1:2] * a + w[2:3] * ap
        yb = b_ref[...].astype(F32) * y0
        r = lax.rsqrt(jnp.mean(yb * yb, axis=-1, keepdims=True) + EPS)
        y_ref[:, pl.ds(0, s)] = (_silu(z_ref[...].astype(F32)) * ((yb * r) * cnw_ref[...])).astype(BF16)
        for sub in range(MIX_CHUNKS):
            rows = pl.ds(sub * c, c)
            for h in range(n_heads):
                sl = pl.ds(h * HEAD_DIM, HEAD_DIM)
                q, k, v = q_ref[rows, sl], k_ref[rows, sl], v_ref[rows, sl]
                p = (_dot(q, k, NT) * dc_ref[h]).astype(BF16)
                o = _dot(p, v, NN)
                qf = q.astype(F32)
                o += _dot((qf * qft[h]).astype(BF16), sf_ref[sub, h], NN)
                o += _dot((qf * qbt[h]).astype(BF16), sb_ref[sub, h], NN)
                o_ref[rows, sl] = o
                mu = jnp.mean(o, axis=-1, keepdims=True)
                var = jnp.mean(jnp.square(o - mu), axis=-1, keepdims=True)
                on = (o - mu) * lax.rsqrt(var + EPS)
                y_ref[rows, pl.ds(s + h * HEAD_DIM, HEAD_DIM)] = (
                    _silu(rz_ref[rows, sl].astype(F32)) * (on * gnw_ref[:, sl])).astype(BF16)

    def seg(g):
        return pl.BlockSpec((None, MIX_ROWS, s), lambda i: (g, i, 0))

    prev, nxt = _halo_specs(s, t)
    row = pl.BlockSpec((MIX_ROWS, s), lambda i: (i, 0))
    st = pl.BlockSpec((MIX_CHUNKS, n_heads, HEAD_DIM, HEAD_DIM), lambda i: (i, 0, 0, 0))
    return pl.pallas_call(
        body, name=name, grid=(nc // MIX_CHUNKS,),
        in_specs=[seg(0), seg(1), seg(2), seg(3), seg(4), prev(0), nxt(0), prev(2), nxt(2), seg(0), seg(1), seg(2),
                  st, st, _full((n_heads, c, c)), _full((n_heads, c, HEAD_DIM)), _full((n_heads, c, HEAD_DIM)),
                  _full((3, s)), _full((1, s)), _full((1, s))],
        out_specs=[pl.BlockSpec((MIX_ROWS, 2 * s), lambda i: (i, 0)), row],
        out_shape=[jax.ShapeDtypeStruct((t, 2 * s), BF16), jax.ShapeDtypeStruct((t, s), F32)],
        compiler_params=_params(("parallel",)),
    )(u, u, u, u, u, u, u, u, u, qkv, qkv, qkv, sf, sb, tabs["dc"], tabs["qf"], tabs["qb"], conv_w, cnw, gnw)


def _out_proj_prenorm(ycat, w_out, res, mod, nw_next, mod_next, n_lat, name):
    t, d = ycat.shape
    nb = t // ROW_TILE
    nxb = n_lat // ROW_TILE
    split = len(res) == 2

    def body(a_ref, w_ref, *rest):
        res_refs = rest[:len(res)]
        mod_ref, nw_ref, modn_ref, m_ref, xo_ref, hx_ref, xs = rest[len(res):]
        i = pl.program_id(0)

        @pl.when(i == 0)
        def _():
            xs[...] = jnp.zeros_like(xs)

        cur_ctx = jnp.minimum(i, nb - 1) >= nxb
        prev_ctx = i - 1 >= nxb

        def step(cur, prev):
            mv, mn = mod_ref[...], modn_ref[...]
            shift = jnp.where(prev_ctx, mn[3:4], mn[0:1])
            scale = jnp.where(prev_ctx, mn[4:5], mn[1:2])
            hx_ref[...] = _modulate(xs[prev], nw_ref[...], shift, scale)
            m = _dot(a_ref[...], w_ref[...], NN)
            x_res = jnp.where(cur_ctx, res_refs[1][...], res_refs[0][...]) if split else res_refs[0][...]
            x_new = x_res + jnp.where(cur_ctx, mv[5:6], mv[2:3]) * m
            m_ref[...] = m.astype(BF16)
            xo_ref[...] = x_new
            xs[cur] = x_new

        @pl.when(i % 2 == 0)
        def _():
            step(0, 1)

        @pl.when(i % 2 == 1)
        def _():
            step(1, 0)

    cur = pl.BlockSpec((ROW_TILE, d), lambda i: (jnp.minimum(i, nb - 1), 0))
    prev = pl.BlockSpec((ROW_TILE, d), lambda i: (jnp.maximum(i - 1, 0), 0))
    res_specs = list(_split_rows(nxb, nb)(d, lambda i: jnp.minimum(i, nb - 1))) if split else [cur]
    return pl.pallas_call(
        body, name=name, grid=(nb + 1,),
        in_specs=[cur, _full((d, d))] + res_specs + [_full((8, d)), _full((1, d)), _full((8, d))],
        out_specs=[cur, cur, prev],
        out_shape=[jax.ShapeDtypeStruct((t, d), BF16), jax.ShapeDtypeStruct((t, d), F32),
                   jax.ShapeDtypeStruct((t, d), BF16)],
        scratch_shapes=[pltpu.VMEM((2, ROW_TILE, d), F32)],
        compiler_params=_params(("arbitrary",)))(ycat, w_out, *res, mod, nw_next, mod_next)


def _out_proj_loss(ycat, w_out, xt, mod, tgt, fnw, n_lat, name):
    t, d = xt.shape
    nb = t // ROW_TILE
    nxb = n_lat // ROW_TILE

    def body(a_ref, w_ref, x_ref, mod_ref, t_ref, fw_ref, dx_ref, dm_ref, loss_ref, dw_ref, gacc_ref, xs, ms):
        i = pl.program_id(0)

        @pl.when(i == 0)
        def _():
            xs[...] = jnp.zeros_like(xs)
            ms[...] = jnp.zeros_like(ms)
            loss_ref[...] = jnp.zeros_like(loss_ref)
            dw_ref[...] = jnp.zeros_like(dw_ref)
            gacc_ref[...] = jnp.zeros_like(gacc_ref)

        def step(cur, prev):
            mv = mod_ref[...]
            x_prev, m_prev = xs[prev], ms[prev]
            valid = (i >= 1) & (i - 1 < nxb)
            w = fw_ref[...]
            r = lax.rsqrt(jnp.mean(x_prev * x_prev, axis=-1, keepdims=True) + EPS)
            xn = x_prev * r
            e = xn * w - t_ref[...]
            loss = 0.5 * jnp.sum(jnp.mean(e * e, axis=-1, keepdims=True), axis=0, keepdims=True)
            loss_ref[...] += jnp.where(valid, loss, 0.0)
            dy = e * (1.0 / d)
            dw_ref[0:1, :] += jnp.where(valid, jnp.sum(dy * xn, axis=0, keepdims=True), 0.0)
            dxn = dy * w
            dx = jnp.where(valid, r * (dxn - xn * jnp.mean(dxn * xn, axis=-1, keepdims=True)), 0.0)
            dx_ref[...] = dx
            dm_ref[...] = (dx * mv[2:3]).astype(BF16)
            gacc_ref[2:3, :] += jnp.sum(dx * m_prev, axis=0, keepdims=True)

            m = _dot(a_ref[...], w_ref[...], NN)
            gate = jnp.where(jnp.minimum(i, nb - 1) >= nxb, mv[5:6], mv[2:3])
            xs[cur] = x_ref[...] + gate * m
            ms[cur] = m

        @pl.when(i % 2 == 0)
        def _():
            step(0, 1)

        @pl.when(i % 2 == 1)
        def _():
            step(1, 0)

    cur = pl.BlockSpec((ROW_TILE, d), lambda i: (jnp.minimum(i, nb - 1), 0))
    prev = pl.BlockSpec((ROW_TILE, d), lambda i: (jnp.maximum(i - 1, 0), 0))
    return pl.pallas_call(
        body, name=name, grid=(nb + 1,),
        in_specs=[cur, _full((d, d)), cur, _full((8, d)),
                  pl.BlockSpec((ROW_TILE, d), lambda i: (jnp.clip(i - 1, 0, nxb - 1), 0)), _full((1, d))],
        out_specs=[prev, prev, _full((8, HEAD_DIM)), _full((8, d)), _full((8, d))],
        out_shape=[jax.ShapeDtypeStruct((t, d), F32), jax.ShapeDtypeStruct((t, d), BF16),
                   jax.ShapeDtypeStruct((8, HEAD_DIM), F32), jax.ShapeDtypeStruct((8, d), F32),
                   jax.ShapeDtypeStruct((8, d), F32)],
        scratch_shapes=[pltpu.VMEM((2, ROW_TILE, d), F32), pltpu.VMEM((2, ROW_TILE, d), F32)],
        compiler_params=_params(("arbitrary",)))(ycat, w_out, xt, mod, tgt, fnw)


def _matmul_nt(a, w, tn, name, after=()):
    t, k = a.shape
    n = w.shape[0]
    tm = _mm_rows(t)

    def body(a_ref, w_ref, *rest):
        rest[-1][...] = _dot(a_ref[...], w_ref[...], NT)

    return pl.pallas_call(
        body, name=name, grid=(n // tn, t // tm),
        in_specs=[pl.BlockSpec((tm, k), lambda j, i: (i, 0)), pl.BlockSpec((tn, k), lambda j, i: (j, 0))]
        + [pl.BlockSpec(memory_space=pl.ANY)] * len(after),
        out_specs=pl.BlockSpec((tm, tn), lambda j, i: (i, j)),
        out_shape=jax.ShapeDtypeStruct((t, n), F32),
        compiler_params=_params(("parallel", "parallel")))(a, w, *after)


def _weight_grad(a, b, bm, bt, name):
    t, m = a.shape
    n_g, _, n = b.shape
    nt = t // bt

    def body(a_ref, b_ref, o_ref, acc):
        k = pl.program_id(2)

        @pl.when(k == 0)
        def _():
            acc[...] = jnp.zeros_like(acc)

        acc[...] += _dot(a_ref[...], b_ref[...], TN)

        @pl.when(k == nt - 1)
        def _():
            o_ref[...] = acc[...].astype(o_ref.dtype)

    return pl.pallas_call(
        body, name=name, grid=(n_g, m // bm, nt),
        in_specs=[pl.BlockSpec((bt, bm), lambda g, i, k: (k, i)), pl.BlockSpec((None, bt, n), lambda g, i, k: (g, k, 0))],
        out_specs=pl.BlockSpec((None, bm, n), lambda g, i, k: (g, i, 0)),
        out_shape=jax.ShapeDtypeStruct((n_g, m, n), BF16),
        scratch_shapes=[pltpu.VMEM((bm, n), F32)],
        compiler_params=_params(("parallel", "parallel", "arbitrary")))(a, b)


def _weight_grad_beside_prenorm_bwd(a, b, dhx, xt, dxo, nw, mod, below, n_lat, name):
    t, m = a.shape
    n_g, _, n = b.shape
    d = xt.shape[1]
    bt = _mm_rows(t)
    nt = t // bt
    rows = t // (n_g * nt)
    n_piece = 2 if rows % 32 == 0 and m % 2 == 0 else 1
    rows_p, m_p = rows // n_piece, m // n_piece
    assert rows * n_g * nt == t and rows_p % 8 == 0

    def body(a_ref, b_ref, dh_ref, x_ref, dxo_ref, nw_ref, mod_ref, m_ref, modb_ref,
             o_ref, dx_ref, acc_ref, dm_ref, gacc_ref, acc):
        g, k = pl.program_id(0), pl.program_id(1)
        step = g * nt + k

        @pl.when(step == 0)
        def _():
            acc_ref[...] = jnp.zeros_like(acc_ref)
            gacc_ref[...] = jnp.zeros_like(gacc_ref)

        @pl.when(k == 0)
        def _():
            acc[...] = jnp.zeros_like(acc)

        mv, mb, nw_v = mod_ref[...], modb_ref[...], nw_ref[...]
        for p in range(n_piece):
            rs = pl.ds(p * rows_p, rows_p)
            rowi = step * rows + p * rows_p + lax.broadcasted_iota(jnp.int32, (rows_p, 1), 0)
            ctx = rowi >= n_lat
            w_lat = jnp.where(ctx, 0.0, 1.0)
            w_ctx = 1.0 - w_lat
            scale1 = 1.0 + jnp.where(ctx, mv[4:5], mv[1:2])
            x = x_ref[rs, :]
            r = lax.rsqrt(jnp.mean(x * x, axis=-1, keepdims=True) + EPS)
            xn = x * r
            dh = dh_ref[rs, :]
            dsc = dh * (xn * nw_v)
            acc_ref[0:1, :] += jnp.sum(dh * w_lat, axis=0, keepdims=True)
            acc_ref[1:2, :] += jnp.sum(dsc * w_lat, axis=0, keepdims=True)
            acc_ref[3:4, :] += jnp.sum(dh * w_ctx, axis=0, keepdims=True)
            acc_ref[4:5, :] += jnp.sum(dsc * w_ctx, axis=0, keepdims=True)
            acc_ref[6:7, :] += jnp.sum(dh * scale1 * xn, axis=0, keepdims=True)
            dxn = dh * (nw_v * scale1)
            dx = dxo_ref[rs, :] + r * (dxn - xn * jnp.mean(dxn * xn, axis=-1, keepdims=True))
            dx_ref[rs, :] = dx
            dm_ref[rs, :] = (dx * jnp.where(ctx, mb[5:6], mb[2:3])).astype(BF16)
            dg = dx * m_ref[rs, :].astype(F32)
            gacc_ref[2:3, :] += jnp.sum(dg * w_lat, axis=0, keepdims=True)
            gacc_ref[5:6, :] += jnp.sum(dg * w_ctx, axis=0, keepdims=True)

            ms_ = pl.ds(p * m_p, m_p)
            acc[ms_, :] += _dot(a_ref[:, ms_], b_ref[...], TN)

        @pl.when(k == nt - 1)
        def _():
            o_ref[...] = acc[...].astype(o_ref.dtype)

    side = pl.BlockSpec((rows, d), lambda g, k: (g * nt + k, 0))
    acc8 = _full((8, d))
    return pl.pallas_call(
        body, name=name, grid=(n_g, nt),
        in_specs=[pl.BlockSpec((bt, m), lambda g, k: (k, 0)), pl.BlockSpec((None, bt, n), lambda g, k: (g, k, 0)),
                  side, side, side, _full((1, d)), acc8, side, acc8],
        out_specs=[pl.BlockSpec((None, m, n), lambda g, k: (g, 0, 0)), side, acc8, side, acc8],
        out_shape=[jax.ShapeDtypeStruct((n_g, m, n), BF16), jax.ShapeDtypeStruct((t, d), F32),
                   jax.ShapeDtypeStruct((8, d), F32), jax.ShapeDtypeStruct((t, d), BF16),
                   jax.ShapeDtypeStruct((8, d), F32)],
        scratch_shapes=[pltpu.VMEM((m, n), F32)],
        compiler_params=_params(("arbitrary", "arbitrary")))(a, b, dhx, xt, dxo, nw, mod, *below)


def _mix_bwd_a(dycat, u, o, conv_w, cnw, gnw, n_heads, nx, ncc, name):
    _, t, s = u.shape
    nc = nx + ncc

    def body(dy_ref, h_ref, b_ref, c_ref, z_ref, rz_ref, hp_ref, hn_ref, cp_ref, cn_ref, o_ref, w_ref,
             cnw_ref, gnw_ref, g_ref, dz_ref, db_ref, drz_ref, do_ref, acc_ref):
        i = pl.program_id(0)

        @pl.when(i == 0)
        def _():
            acc_ref[...] = jnp.zeros_like(acc_ref)

        has_prev, has_next = _neighbours(i, nx, nc)
        a, before, after = _conv_input(h_ref, c_ref, hp_ref, hn_ref, cp_ref, cn_ref)
        am, ap = _shifted(a, before, after, has_prev, has_next)
        w = w_ref[...]
        y0 = w[0:1] * am + w[1:2] * a + w[2:3] * ap
        bb = b_ref[...].astype(F32)
        yb = bb * y0
        r = lax.rsqrt(jnp.mean(yb * yb, axis=-1, keepdims=True) + EPS)
        ynn = yb * r
        z = z_ref[...].astype(F32)
        dyc = dy_ref[:, pl.ds(0, s)]
        cw = cnw_ref[...]
        sz, dsz = _silu_and_slope(z)
        dz_ref[...] = (dyc * (ynn * cw) * dsz).astype(BF16)
        dyn = dyc * sz
        acc_ref[0:1, :] += jnp.sum(dyn * ynn, axis=0, keepdims=True)
        dynn = dyn * cw
        dyb = r * (dynn - ynn * jnp.mean(dynn * ynn, axis=-1, keepdims=True))
        db_ref[...] = (dyb * y0).astype(BF16)
        g_ref[...] = dyb * bb
        for h in range(n_heads):
            sl = pl.ds(h * HEAD_DIM, HEAD_DIM)
            ov = o_ref[:, sl]
            mu = jnp.mean(ov, axis=-1, keepdims=True)
            var = jnp.mean(jnp.square(ov - mu), axis=-1, keepdims=True)
            rs = lax.rsqrt(var + EPS)
            on = (ov - mu) * rs
            dyr = dy_ref[:, pl.ds(s + h * HEAD_DIM, HEAD_DIM)]
            rz = rz_ref[:, sl].astype(F32)
            gw = gnw_ref[:, sl]
            srz, dsrz = _silu_and_slope(rz)
            drz_ref[:, sl] = (dyr * (on * gw) * dsrz).astype(BF16)
            dyg = dyr * srz
            acc_ref[1:2, sl] += jnp.sum(dyg * on, axis=0, keepdims=True)
            don = dyg * gw
            do = rs * (don - jnp.mean(don, axis=-1, keepdims=True)
                       - on * jnp.mean(don * on, axis=-1, keepdims=True))
            do_ref[:, sl] = do.astype(BF16)

    def seg(g):
        return pl.BlockSpec((None, MIX_ROWS, s), lambda i: (g, i, 0))

    prev, nxt = _halo_specs(s, t)
    row = pl.BlockSpec((MIX_ROWS, s), lambda i: (i, 0))
    return pl.pallas_call(
        body, name=name, grid=(nc // MIX_CHUNKS,),
        in_specs=[pl.BlockSpec((MIX_ROWS, 2 * s), lambda i: (i, 0)), seg(0), seg(1), seg(2), seg(3), seg(4),
                  prev(0), nxt(0), prev(2), nxt(2), row, _full((3, s)), _full((1, s)), _full((1, s))],
        out_specs=[row, row, row, row, row, _full((8, s))],
        out_shape=[jax.ShapeDtypeStruct((t, s), F32)] + [jax.ShapeDtypeStruct((t, s), BF16)] * 4
        + [jax.ShapeDtypeStruct((8, s), F32)],
        compiler_params=_params(("arbitrary",)),
    )(dycat, u, u, u, u, u, u, u, u, u, o, conv_w, cnw, gnw)


def _grad_state_sweep(qkv, do, tabs, n_heads, nx, ncc, name):
    return _pair_sweep((qkv, 0), (do, None), tabs["qf"], tabs["qb"], tabs["cdf"], tabs["cdb"], n_heads, nx, ncc, True, name)


def _mix_bwd_b(u, g, dz, db, drz, qkv, do, sf, sb, gf, gb, tabs, cos, sa, sb_tab, conv_w,
               n_heads, nx, ncc, name):
    _, t, s = u.shape
    nc = nx + ncc
    c = CHUNK
    k_scale = HEAD_DIM ** -0.5

    def body(h_ref, c_ref, g_ref, gp_ref, gn_ref, dz_ref, db_ref, drz_ref, q_ref, k_ref, v_ref, do_ref,
             sf_ref, sb_ref, gf_ref, gb_ref, dc_t, dlf_t, dlb_t, qft, kft, qbt, kbt, cdf, cdb, lg_ref,
             cos_ref, sa_ref, sb_ref2, w_ref, du_ref, dw_ref, dlg_ref):
        i = pl.program_id(0)

        @pl.when(i == 0)
        def _():
            dw_ref[...] = jnp.zeros_like(dw_ref)
            dlg_ref[...] = jnp.zeros_like(dlg_ref)

        has_prev, has_next = _neighbours(i, nx, nc)
        gv = g_ref[...]
        gm, gp = _shifted(gv, gp_ref[7:8], gn_ref[0:1], has_prev, has_next)
        w = w_ref[...]
        da = w[0:1] * gp + w[1:2] * gv + w[2:3] * gm
        hh, cc = h_ref[...].astype(F32), c_ref[...].astype(F32)
        du_ref[0] = (da * cc).astype(BF16)
        du_ref[2] = (da * hh).astype(BF16)
        a = cc * hh
        dw_ref[0:1, :] += jnp.sum(a * gp, axis=0, keepdims=True)
        dw_ref[1:2, :] += jnp.sum(a * gv, axis=0, keepdims=True)
        dw_ref[2:3, :] += jnp.sum(a * gm, axis=0, keepdims=True)
        du_ref[1] = db_ref[...]
        du_ref[3] = dz_ref[...]
        du_ref[7] = drz_ref[...]

        pos = lax.broadcasted_iota(jnp.int32, (c, HEAD_DIM), 0).astype(F32)
        w_q_f, w_q_b, w_k_f = pos + 1.0, c - pos, c - 1.0 - pos
        row8 = lax.broadcasted_iota(jnp.int32, (8, HEAD_DIM), 0)
        lane8 = lax.broadcasted_iota(jnp.int32, (8, HEAD_DIM), 1)
        dlg = jnp.zeros((8, HEAD_DIM), F32)
        for sub, h in [(sub, h) for sub in range(MIX_CHUNKS) for h in range(n_heads)]:
            rows = pl.ds(sub * c, c)
            co, ra, rb = cos_ref[rows, :], sa_ref[rows, :], sb_ref2[rows, :]
            sl = pl.ds(h * HEAD_DIM, HEAD_DIM)
            q, k, v, do = q_ref[rows, sl], k_ref[rows, sl], v_ref[rows, sl], do_ref[rows, sl]
            qf, kf, dof = q.astype(F32), k.astype(F32), do.astype(F32)
            s_f, s_b, g_f, g_b = sf_ref[sub, h], sb_ref[sub, h], gf_ref[sub, h], gb_ref[sub, h]
            p = _dot(q, k, NT)
            pd = _dot(do, v, NT)
            pdd = (pd * dc_t[h]).astype(BF16)
            dq = _dot(pdd, k, NN)
            dk = _dot(pdd, q, TN)
            dv = _dot((p * dc_t[h]).astype(BF16), do, TN)
            dq_f = _dot((dof * qft[h]).astype(BF16), s_f, NT)
            dq_b = _dot((dof * qbt[h]).astype(BF16), s_b, NT)
            dk_f = _dot(v, g_f, NT) * kft[h]
            dk_b = _dot(v, g_b, NT) * kbt[h]
            dv += _dot((kf * kft[h]).astype(BF16), g_f, NN) + _dot((kf * kbt[h]).astype(BF16), g_b, NN)
            ppd = p * pd
            cd_f, cd_b = cdf[h][0:1, :], cdb[h][0:1, :]
            t_f = _sum_all(dlf_t[h] * ppd + w_q_f * qf * dq_f + w_k_f * kf * dk_f
                           + float(c) * (cd_f * (g_f.astype(F32) * s_f.astype(F32))))
            t_b = _sum_all(dlb_t[h] * ppd + w_q_b * qf * dq_b + pos * kf * dk_b
                           + float(c) * (cd_b * (g_b.astype(F32) * s_b.astype(F32))))
            dlg += jnp.where((row8 == 0) & (lane8 == h), t_f, 0.0) + jnp.where((row8 == 1) & (lane8 == h), t_b, 0.0)
            du_ref[4, rows, sl] = _rope_bwd(dq + dq_f + dq_b, co, ra, rb).astype(BF16)
            du_ref[5, rows, sl] = (_rope_bwd(dk + dk_f + dk_b, co, ra, rb) * k_scale).astype(BF16)
            du_ref[6, rows, sl] = dv.astype(BF16)
        dlg_ref[...] += dlg

        @pl.when(i == nc // MIX_CHUNKS - 1)
        def _():
            dlg_ref[...] = dlg_ref[...] * lg_ref[...]

    def seg(gi):
        return pl.BlockSpec((None, MIX_ROWS, s), lambda i: (gi, i, 0))

    per = MIX_ROWS // 8
    n8 = t // 8
    row = pl.BlockSpec((MIX_ROWS, s), lambda i: (i, 0))
    st = pl.BlockSpec((MIX_CHUNKS, n_heads, HEAD_DIM, HEAD_DIM), lambda i: (i, 0, 0, 0))
    tab = pl.BlockSpec((MIX_ROWS, HEAD_DIM), lambda i: (i, 0))
    hc = _full((n_heads, c, HEAD_DIM))
    cc_ = _full((n_heads, c, c))
    h8 = _full((n_heads, 8, HEAD_DIM))
    return pl.pallas_call(
        body, name=name, grid=(nc // MIX_CHUNKS,),
        in_specs=[seg(0), seg(2), row,
                  pl.BlockSpec((8, s), lambda i: (jnp.maximum(i * per - 1, 0), 0)),
                  pl.BlockSpec((8, s), lambda i: (jnp.minimum((i + 1) * per, n8 - 1), 0)),
                  row, row, row, seg(0), seg(1), seg(2), row, st, st, st, st, cc_, cc_, cc_, hc, hc, hc, hc, h8, h8,
                  _full((8, HEAD_DIM)), tab, tab, tab, _full((3, s))],
        out_specs=[pl.BlockSpec((8, MIX_ROWS, s), lambda i: (0, i, 0)), _full((8, s)), _full((8, HEAD_DIM))],
        out_shape=[jax.ShapeDtypeStruct((8, t, s), BF16), jax.ShapeDtypeStruct((8, s), F32),
                   jax.ShapeDtypeStruct((8, HEAD_DIM), F32)],
        compiler_params=_params(("arbitrary",)),
    )(u, u, g, g, g, dz, db, drz, qkv, qkv, qkv, do, sf, sb, gf, gb, tabs["dc"], tabs["dlf"], tabs["dlb"],
      tabs["qf"], tabs["kf"], tabs["qb"], tabs["kb"], tabs["cdf"], tabs["cdb"], tabs["lg"], cos, sa, sb_tab, conv_w)


def _in_proj_bwd(du, wgs, tm, gs, name, after=()):
    n_seg, t, s = du.shape
    d = wgs[0].shape[1]
    n_w = len(wgs)
    widths = [w.shape[2] for w in wgs]
    assert sum(widths) == s

    def body(a_ref, *rest):
        w_refs, o_ref = rest[:n_w], rest[-1]
        g = pl.program_id(1)
        part = None
        for j in range(gs):
            col = 0
            for w_ref, width in zip(w_refs, widths):
                term = _dot(a_ref[j, :, col:col + width], w_ref[j], NT)
                part = term if part is None else part + term
                col += width

        @pl.when(g == 0)
        def _():
            o_ref[...] = part

        @pl.when(g > 0)
        def _():
            o_ref[...] += part

    return pl.pallas_call(
        body, name=name, grid=(t // tm, n_seg // gs),
        in_specs=[pl.BlockSpec((gs, tm, s), lambda i, g: (g, i, 0))]
        + [pl.BlockSpec((gs, d, width), lambda i, g: (g, 0, 0)) for width in widths]
        + [pl.BlockSpec(memory_space=pl.ANY)] * len(after),
        out_specs=pl.BlockSpec((tm, d), lambda i, g: (i, 0)),
        out_shape=jax.ShapeDtypeStruct((t, d), F32),
        compiler_params=_params(("parallel", "arbitrary")))(du, *wgs, *after)


def _prenorm_bwd_first(dhx, x, ctx, dxo, nw, mod, name):
    n_lat, d = x.shape
    t = n_lat + ctx.shape[0]
    nxb = n_lat // ROW_TILE

    def body(dh_ref, x_ref, c_ref, dxo_ref, nw_ref, mod_ref, dx_ref, acc_ref):
        i = pl.program_id(0)

        @pl.when(i == 0)
        def _():
            acc_ref[...] = jnp.zeros_like(acc_ref)

        ctx = i >= nxb
        m = mod_ref[...]
        scale1 = 1.0 + jnp.where(ctx, m[4:5], m[1:2])
        x = jnp.where(ctx, c_ref[...], x_ref[...])
        nw_v = nw_ref[...]
        r = lax.rsqrt(jnp.mean(x * x, axis=-1, keepdims=True) + EPS)
        xn = x * r
        dh = dh_ref[...]
        dshift = jnp.sum(dh, axis=0, keepdims=True)
        dscale = jnp.sum(dh * (xn * nw_v), axis=0, keepdims=True)
        acc_ref[6:7, :] += jnp.sum(dh * scale1 * xn, axis=0, keepdims=True)
        dxn = dh * (nw_v * scale1)
        dx = dxo_ref[...] + r * (dxn - xn * jnp.mean(dxn * xn, axis=-1, keepdims=True))

        @pl.when(i < nxb)
        def _():
            acc_ref[0:1, :] += dshift
            acc_ref[1:2, :] += dscale
            dx_ref[...] = dx

        @pl.when(i >= nxb)
        def _():
            acc_ref[3:4, :] += dshift
            acc_ref[4:5, :] += dscale

    row = pl.BlockSpec((ROW_TILE, d), lambda i: (i, 0))
    lat, cx = _split_rows(nxb, t // ROW_TILE)(d)
    acc = _full((8, d))
    return pl.pallas_call(body, name=name, grid=(t // ROW_TILE,),
                          in_specs=[row, lat, cx, row, _full((1, d)), acc],
                          out_specs=[lat, acc],
                          out_shape=[jax.ShapeDtypeStruct((n_lat, d), F32), jax.ShapeDtypeStruct((8, d), F32)],
                          compiler_params=_params(("arbitrary",)))(dhx, x, ctx, dxo, nw, mod)


def _adamw(g, w, m, v):
    m = ADAM_B1 * m + (1.0 - ADAM_B1) * g
    v = ADAM_B2 * v + (1.0 - ADAM_B2) * jnp.square(g)
    m_hat = m / (1.0 - ADAM_B1 ** ADAM_STEP)
    v_hat = v / (1.0 - ADAM_B2 ** ADAM_STEP)
    delta = -ADAM_LR * (m_hat / (jnp.sqrt(v_hat) + ADAM_EPS) + ADAM_WD * w)
    return delta, m, v


def _sum_adamw(parts, w, m, v, name, row0=0, into=None):
    n_p, r, n = parts.shape
    r_all = w.shape[0]
    part_block_bytes = 4 * 1024 * 1024
    br = 8
    for cand in (512, 256, 128, 64, 32, 16):
        if r % cand == 0 and row0 % cand == 0 and n_p * cand * n * parts.dtype.itemsize <= part_block_bytes:
            br = cand
            break
    blk0 = row0 // br

    def body(p_ref, w_ref, m_ref, v_ref, *rest):
        g_out, d_out, m_out, v_out = rest[-4:]
        g = p_ref[0].astype(F32)
        for j in range(1, n_p):
            g = g + p_ref[j].astype(F32)
        g_out[...] = g
        d_out[...], m_out[...], v_out[...] = _adamw(g, w_ref[...], m_ref[...], v_ref[...])

    row = pl.BlockSpec((br, n), lambda i: (i + blk0, 0))
    kept = [] if into is None else list(into)
    return pl.pallas_call(body, name=name, grid=(r // br,),
                          in_specs=[pl.BlockSpec((n_p, br, n), lambda i: (0, i, 0)), row, row, row]
                          + [pl.BlockSpec(memory_space=pl.ANY)] * len(kept),
                          out_specs=[row] * 4, out_shape=[jax.ShapeDtypeStruct((r_all, n), F32)] * 4,
                          input_output_aliases={4 + j: j for j in range(len(kept))},
                          compiler_params=_params(("parallel",)))(parts, w, m, v, *kept)


def _rope_tables(n_lat, n_ctx):
    f = HEAD_DIM // 4
    rows = n_lat // GRID_W
    inv = ROPE_BASE ** (-jnp.arange(f, dtype=F32) / f)
    ang_r = jnp.arange(rows).astype(F32)[:, None] * inv[None, :]
    ang_c = jnp.arange(GRID_W).astype(F32)[:, None] * inv[None, :]

    cr, sr, cc, sc = jnp.cos(ang_r), jnp.sin(ang_r), jnp.cos(ang_c), jnp.sin(ang_c)
    zr, zc = jnp.zeros_like(cr), jnp.zeros_like(cc)

    def table(by_row, by_col):
        both = by_row[:, None, :] + by_col[None, :, :]
        return both.reshape(n_lat, HEAD_DIM)

    cos = table(jnp.concatenate([cr, cr, zr, zr], axis=-1), jnp.concatenate([zc, zc, cc, cc], axis=-1))
    sa = table(jnp.concatenate([-sr, zr, zr, zr], axis=-1), jnp.concatenate([zc, zc, -sc, zc], axis=-1))
    sb = table(jnp.concatenate([zr, sr, zr, zr], axis=-1), jnp.concatenate([zc, zc, zc, sc], axis=-1))
    pad = jnp.zeros((n_ctx, HEAD_DIM), F32)
    return (jnp.concatenate([cos, pad + 1.0], axis=0), jnp.concatenate([sa, pad], axis=0),
            jnp.concatenate([sb, pad], axis=0))


def _pad_rows(a, rows):
    return jnp.pad(a, [(0, rows - a.shape[0])] + [(0, 0)] * (a.ndim - 1))


def _pad_cols(a, cols):
    return jnp.pad(a, [(0, 0), (0, cols - a.shape[1])])


def kernel(x, c, ctx, c_ctx, norm_w, w_mod, b_mod, w_in, conv_w, conv_norm_w, ret_norm_w, ret_decay_f, ret_decay_b, w_out, final_norm_w, loss_target, m_c_ctx, m_norm_w, m_w_mod, m_b_mod, m_w_in, m_conv_w, m_conv_norm_w, m_ret_norm_w, m_ret_decay_f, m_ret_decay_b, m_w_out, m_final_norm_w, v_c_ctx, v_norm_w, v_w_mod, v_b_mod, v_w_in, v_conv_w, v_conv_norm_w, v_ret_norm_w, v_ret_decay_f, v_ret_decay_b, v_w_out, v_final_norm_w):
    depth = norm_w.shape[0]
    n_lat, d = x.shape[1], x.shape[2]
    n_ctx = ctx.shape[1]
    s = d // 2
    n_heads = ret_decay_f.shape[1]
    nx, ncc = n_lat // CHUNK, n_ctx // CHUNK
    n_mod = w_mod.shape[2]
    n_cw = conv_w.shape[2]
    r_out = w_out.shape[1]
    assert s == n_heads * HEAD_DIM and w_in.shape[2] == s and N_DEV * r_out == d
    assert n_lat % ROW_TILE == 0 and n_ctx % ROW_TILE == 0 and 3 * depth * n_cw <= d and N_DEV * n_mod == 3 * d
    me = 4 * lax.axis_index("x") + 2 * lax.axis_index("y") + lax.axis_index("c")

    w_in_bf = [w_in[l].astype(BF16) for l in range(depth)]
    w_out_bf = [w_out[l].astype(BF16) for l in range(depth)]

    first = jnp.concatenate([c.reshape(1, d), _pad_cols(conv_w.reshape(1, -1), d), jnp.zeros((6, d), F32)], axis=0)
    (first_g,) = _all_gather([first], "gather_cond")
    first_g = first_g.reshape(N_DEV, 8, d)
    c_all = first_g[:, 0, :]
    conv_full = first_g[:, 1, :3 * depth * n_cw].reshape(N_DEV, depth, 3, n_cw)
    conv_full = conv_full.transpose(1, 2, 0, 3).reshape(depth, 3, N_DEV * n_cw)
    c9 = jnp.concatenate([c_all, c_ctx.reshape(1, d), jnp.zeros((7, d), F32)], axis=0)

    b_sh = lax.dynamic_slice(b_mod, (0, me * n_mod), (depth, n_mod))
    mod_sh = jnp.concatenate([_mod_rows(c9, w_mod[l], b_sh[l:l + 1], f"mod_rows_l{l}") for l in range(depth)], axis=0)
    (mod_g,) = _all_gather([mod_sh], "gather_mod")
    mod_g = mod_g.reshape(N_DEV, depth, 16, n_mod)
    mods = []
    for l in range(depth):
        mine = lax.dynamic_index_in_dim(mod_g[:, l], me, axis=1, keepdims=False).reshape(3, d)
        cx = mod_g[:, l, 8, :].reshape(3, d)
        mods.append(jnp.concatenate([mine, cx, jnp.zeros((2, d), F32)], axis=0))

    halves = [w_in_bf[0][:, :s // 2], w_in_bf[0][:, s // 2:]]
    near, order = [], [mod_g]
    for j, part in enumerate(halves):
        near.append(_push_start([part], [_landing(part, me)], "near", f"w_in0_start_{j}", after=order))
        order = near[-1][4:]
    pending = []
    for k in range(depth):
        srcs = [w_out_bf[k]] + ([w_in_bf[k]] if k > 0 else [])
        started = _push_start(srcs, [_landing(a, me) for a in srcs], "gather", f"weights_start_l{k}", after=order)
        pending.append(started[:4])
        order = started[4:]
    w_in_g = [None] * depth
    w_out_g = [None] * depth

    cos, sa, sb_tab = _rope_tables(n_lat, n_ctx)
    t_all = n_lat + n_ctx

    saved = []
    xt = hx_next = None
    for l in range(depth):
        tiles = _tiles(l, t_all, d)
        names = ["dc", "dlf", "dlb", "qf", "kf", "qb", "kb", "cdf", "cdb", "lg"]
        dec = jnp.stack([ret_decay_f[l], ret_decay_b[l]], axis=0)
        tabs = dict(zip(names, _decay_tables(dec, n_heads, f"decay_tables_l{l}")))
        if l == 0:
            hx = _prenorm_first(x[0], ctx[0], norm_w[0:1], mods[0], "prenorm_l0", after=order)
            gathered, out, after = [], None, hx
            for j in range(2):
                (landed,) = _push_wait(*near[j][:4], "near", after, f"w_in0_wait_{j}")
                relay = _push_start([], [landed], "relay", f"w_in0_relay_start_{j}")
                (landed,) = _push_wait(*relay[:4], "relay", relay[4], f"w_in0_relay_wait_{j}")
                gathered.append(landed)
                out = _in_proj(hx, landed, cos, sa, sb_tab, s, j, tiles["in_tm_half"], f"in_proj_l0_{j}", into=out)
                after = out[0]
            u, qkv = out
            w_in_g[0] = gathered
        else:
            landed = _push_wait(*pending[l], "gather", xt, f"weights_wait_l{l}")
            w_out_g[l], w_in_g[l] = landed[0].reshape(d, d), [landed[1]]
            hx = hx_next
            u, qkv = _in_proj(hx, w_in_g[l][0], cos, sa, sb_tab, s, 0, tiles["in_tm"], f"in_proj_l{l}")
        sf, sb = _state_sweep(qkv, tabs, n_heads, nx, ncc, f"state_sweep_l{l}")
        ycat, o = _mix_fwd(u, qkv, sf, sb, tabs, conv_full[l], conv_norm_w[l:l + 1], ret_norm_w[l:l + 1],
                           n_heads, nx, ncc, f"mix_fwd_l{l}")
        if l == 0:
            (landed,) = _push_wait(*pending[0], "gather", ycat, "weights_wait_l0")
            w_out_g[0] = landed.reshape(d, d)
        m_res = x_new = None
        if l < depth - 1:
            res = (x[0], ctx[0]) if l == 0 else (xt,)
            m_res, x_new, hx_next = _out_proj_prenorm(ycat, w_out_g[l], res, mods[l], norm_w[l + 1:l + 2], mods[l + 1],
                                                      n_lat, f"out_proj_l{l}")
        else:
            dxt, dm, loss_blk, dfnw, gate_acc = _out_proj_loss(ycat, w_out_g[l], xt, mods[l], loss_target[0],
                                                               final_norm_w.reshape(1, d), n_lat, f"out_proj_loss_l{l}")
        saved.append(dict(tabs=tabs, xt=xt, hx=hx, u=u, qkv=qkv, sf=sf, sb=sb, ycat=ycat, o=o, m=m_res, tiles=tiles))
        xt = x_new

    loss = lax.psum(loss_blk[0, 0], MESH_AXES)

    dmod_x, dmod_c, dnw, dcnw, dgnw, dconv, ddec, dwin, dwout = [], [], [], [], [], [], [], [], []
    started_token = ()
    for l in reversed(range(depth)):
        sv = saved[l]
        tiles = sv["tiles"]
        dycat = _matmul_nt(dm, w_out_g[l], tiles["ob_tn"], f"out_proj_bwd_l{l}", after=started_token)
        dwout.append(_weight_grad(sv["ycat"], dm.reshape(1, *dm.shape), tiles["wo_bm"], _mm_rows(t_all),
                                  f"w_out_grad_l{l}")[0])
        g, dz, db, drz, do, norm_acc = _mix_bwd_a(dycat, sv["u"], sv["o"], conv_full[l], conv_norm_w[l:l + 1],
                                                   ret_norm_w[l:l + 1], n_heads, nx, ncc, f"mix_bwd_a_l{l}")
        gf, gb = _grad_state_sweep(sv["qkv"], do, sv["tabs"], n_heads, nx, ncc, f"grad_state_sweep_l{l}")
        du, conv_acc, dlg = _mix_bwd_b(sv["u"], g, dz, db, drz, sv["qkv"], do, sv["sf"], sv["sb"],
                                       gf, gb, sv["tabs"], cos, sa, sb_tab, conv_full[l], n_heads, nx, ncc,
                                       f"mix_bwd_b_l{l}")
        gate_acc_l = gate_acc
        if l > 0:
            dhx = _in_proj_bwd(du, w_in_g[l], tiles["bwd_tm"], tiles["bwd_gs"], f"in_proj_bwd_l{l}")
            below = (saved[l - 1]["m"], mods[l - 1])
            dwin_l, dxt, pre_acc, dm, gate_acc = _weight_grad_beside_prenorm_bwd(
                sv["hx"], du, dhx, sv["xt"], dxt, norm_w[l:l + 1], mods[l], below, n_lat, f"w_in_grad_l{l}")
        else:
            dwin_l = _weight_grad(sv["hx"], du, tiles["wg_bm"], tiles["wg_bt"], f"w_in_grad_l{l}")
        srcs = [dwin_l, dwout[-1].reshape(N_DEV, r_out, d)]
        lands = [_landing(lax.dynamic_index_in_dim(a, me, axis=0, keepdims=False), me) for a in srcs]
        started = _push_start(srcs, lands, "scatter", f"grads_start_l{l}")
        dwin.append(started[:4])
        started_token = started[4:]
        if l == 0:
            dhx = _in_proj_bwd(du, w_in_g[l], tiles["bwd_tm"], tiles["bwd_gs"], f"in_proj_bwd_l{l}", after=started[4:])
            dxt, pre_acc = _prenorm_bwd_first(dhx, x[0], ctx[0], dxt, norm_w[l:l + 1], mods[l], f"prenorm_bwd_l{l}")
        dmod_x.append(jnp.concatenate([pre_acc[0], pre_acc[1], gate_acc_l[2]]))
        dmod_c.append(jnp.concatenate([pre_acc[3], pre_acc[4], gate_acc_l[5]]))
        dnw.append(pre_acc[6])
        dcnw.append(norm_acc[0])
        dgnw.append(norm_acc[1])
        dconv.append(conv_acc[0:3])
        ddec.append(dlg[0:2, :n_heads])
    for lst in (dmod_x, dmod_c, dnw, dcnw, dgnw, dconv, ddec, dwin, dwout):
        lst.reverse()
    grad_x = dxt.reshape(1, n_lat, d)

    rows = []
    for l in range(depth):
        rows += [dmod_x[l], dmod_c[l]]
    (dmod_g,) = _all_gather([_pad_rows(jnp.stack(rows, axis=0), 8)], "gather_dmod")
    dmod_g = dmod_g.reshape(N_DEV, 8, 3 * d)
    mine_cols = lax.dynamic_slice(dmod_g, (0, 0, me * n_mod), (N_DEV, 8, n_mod))
    g_wmod, dcc = [], jnp.zeros((d,), F32)
    for l in range(depth):
        gw, dc_part = _mod_grads(mine_cols[:, 2 * l], mine_cols[:, 2 * l + 1], c9, w_mod[l], f"mod_grads_l{l}")
        g_wmod.append(gw)
        dcc = dcc + dc_part[0]

    n_small = 16
    small = jnp.concatenate([
        jnp.stack(dnw, axis=0),
        jnp.concatenate(dcnw).reshape(1, -1),
        jnp.concatenate(dgnw).reshape(1, -1),
        dfnw[0:1],
        dcc.reshape(1, d),
        jnp.stack(dconv, axis=0).reshape(-1, d),
        _pad_cols(jnp.stack(ddec, axis=0).reshape(1, -1), d),
    ], axis=0)
    assert depth * s == d and small.shape[0] <= n_small
    n_rows = small.shape[0]
    (small_g,) = _all_gather([_pad_rows(small, n_small)], "gather_small")
    small_g = small_g.reshape(N_DEV, n_small, d)

    def pack_small(nw_, cn_, gn_, fn_, cc_, df_, db_):
        return _pad_rows(jnp.concatenate([
            nw_, cn_.reshape(1, -1), gn_.reshape(1, -1), fn_.reshape(1, d), cc_.reshape(1, d),
            jnp.zeros((n_rows - depth - 5, d), F32),
            _pad_cols(jnp.stack([df_, db_], axis=1).reshape(1, -1), d)], axis=0), n_small)

    w_s = pack_small(norm_w, conv_norm_w, ret_norm_w, final_norm_w, c_ctx, ret_decay_f, ret_decay_b)
    m_s = pack_small(m_norm_w, m_conv_norm_w, m_ret_norm_w, m_final_norm_w, m_c_ctx, m_ret_decay_f, m_ret_decay_b)
    v_s = pack_small(v_norm_w, v_conv_norm_w, v_ret_norm_w, v_final_norm_w, v_c_ctx, v_ret_decay_f, v_ret_decay_b)
    small_out = _sum_adamw(small_g, w_s, m_s, v_s, "adamw_small")

    def unpack_small(a):
        nw_ = a[0:depth]
        cn_ = a[depth].reshape(depth, s)
        gn_ = a[depth + 1].reshape(depth, s)
        fn_ = a[depth + 2]
        cc_ = a[depth + 3]
        dd = a[n_rows - 1, :depth * 2 * n_heads].reshape(depth, 2, n_heads)
        return dict(c_ctx=cc_, norm_w=nw_, conv_norm_w=cn_, ret_norm_w=gn_, ret_decay_f=dd[:, 0], ret_decay_b=dd[:, 1],
                    final_norm_w=fn_)

    res = {}
    for kind, arr in zip(("grad", "delta", "m", "v"), small_out):
        for k_, val in unpack_small(arr).items():
            res[(kind, k_)] = val

    bm_parts = jnp.concatenate([dmod_g[:, 0:2 * depth:2].reshape(N_DEV, depth, 3 * d),
                                dmod_g[:, 1:2 * depth:2].reshape(N_DEV, depth, 3 * d)], axis=0)
    bm_parts = jnp.concatenate([bm_parts, jnp.zeros((2 * N_DEV, 8 - depth, 3 * d), F32)], axis=1)
    pad8 = lambda a: _pad_rows(a, 8)
    bm_out = _sum_adamw(bm_parts, pad8(b_mod), pad8(m_b_mod), pad8(v_b_mod), "adamw_b_mod")
    for kind, arr in zip(("grad", "delta", "m", "v"), bm_out):
        res[(kind, "b_mod")] = arr[:depth]

    conv_rows = small_g[:, depth + 4:depth + 4 + 3 * depth * s // d].reshape(N_DEV, depth * 3, s)
    conv_mine = lax.dynamic_slice(conv_rows, (0, 0, me * n_cw), (N_DEV, depth * 3, n_cw))
    conv_mine = jnp.concatenate([conv_mine, jnp.zeros((N_DEV, 8 - depth * 3, n_cw), F32)], axis=1)
    cw2 = lambda a: _pad_rows(a.reshape(depth * 3, n_cw), 8)
    cw_out = _sum_adamw(conv_mine, cw2(conv_w), cw2(m_conv_w), cw2(v_conv_w), "adamw_conv_w")
    for kind, arr in zip(("grad", "delta", "m", "v"), cw_out):
        res[(kind, "conv_w")] = arr[:depth * 3].reshape(depth, 3, n_cw)

    wm_out = _sum_adamw(jnp.stack(g_wmod, axis=0).reshape(1, depth * d, n_mod), w_mod.reshape(depth * d, n_mod),
                        m_w_mod.reshape(depth * d, n_mod), v_w_mod.reshape(depth * d, n_mod), "adamw_w_mod")
    for kind, arr in zip(("grad", "delta", "m", "v"), wm_out):
        res[(kind, "w_mod")] = arr.reshape(depth, d, n_mod)

    wi_out = wo_out = None
    after = wm_out[0]
    for l in reversed(range(depth)):
        win_parts, wout_parts = _push_wait(*dwin[l], "scatter", after, f"grads_wait_l{l}")
        wi_out = _sum_adamw(win_parts, w_in.reshape(depth * d, s), m_w_in.reshape(depth * d, s),
                            v_w_in.reshape(depth * d, s), f"adamw_w_in_l{l}", row0=l * d, into=wi_out)
        wo_out = _sum_adamw(wout_parts, w_out.reshape(depth * r_out, d), m_w_out.reshape(depth * r_out, d),
                            v_w_out.reshape(depth * r_out, d), f"adamw_w_out_l{l}", row0=l * r_out, into=wo_out)
        after = wo_out[0]
    for kind, arr in zip(("grad", "delta", "m", "v"), wi_out):
        res[(kind, "w_in")] = arr.reshape(depth, d, s)
    for kind, arr in zip(("grad", "delta", "m", "v"), wo_out):
        res[(kind, "w_out")] = arr.reshape(depth, r_out, d)

    order = ["c_ctx", "norm_w", "w_mod", "b_mod", "w_in", "conv_w", "conv_norm_w", "ret_norm_w", "ret_decay_f",
             "ret_decay_b", "w_out", "final_norm_w"]
    outs = [loss, grad_x]
    for kind in ("grad", "delta", "m", "v"):
        outs += [res[(kind, k_)] for k_ in order]
    return tuple(outs)
```

```python
import jax
import jax.numpy as jnp
from jax import lax
from jax.experimental import pallas as pl
from jax.experimental.pallas import tpu as pltpu

F32 = jnp.float32
BF16 = jnp.bfloat16

EPS = 1e-6
CHUNK = 128
HEAD_DIM = 128
GRID_W = 64
ROPE_BASE = 10000.0
N_DEV = 8
ADAM_LR, ADAM_B1, ADAM_B2, ADAM_EPS, ADAM_WD, ADAM_STEP = 0.001, 0.9, 0.999, 1e-08, 0.01, 10

ROW_TILE = 256
V7X_VMEM_LIMIT = 56 * 1024 * 1024
MESH_AXES = ("x", "y", "c")

NN = ((1,), (0,))
NT = ((1,), (1,))
TN = ((0,), (0,))


def _dot(a, b, dims):
    return lax.dot_general(a, b, (dims, ((), ())), preferred_element_type=F32)


def _params(sem=None):
    if sem is None:
        return pltpu.CompilerParams(vmem_limit_bytes=V7X_VMEM_LIMIT)
    return pltpu.CompilerParams(dimension_semantics=sem, vmem_limit_bytes=V7X_VMEM_LIMIT)


def _silu(z):
    return z * jax.nn.sigmoid(z)


def _dsilu(z):
    s = jax.nn.sigmoid(z)
    return s * (1.0 + z * (1.0 - s))


def _silu_and_slope(z):
    s = jax.nn.sigmoid(z)
    return z * s, s * (1.0 + z * (1.0 - s))


def _sum_all(a):
    return jnp.sum(jnp.sum(a, axis=1, keepdims=True), axis=0, keepdims=True)


def _mm_rows(t):
    return 768 if t % 768 == 0 else ROW_TILE


def _rows_or(t, rows):
    return rows if t % rows == 0 else _mm_rows(t)


def _tiles(layer, t, d):
    return dict(in_tm=_rows_or(t, 1408), in_tm_half=_rows_or(t, 2112), bwd_gs=2, wg_bm=d, wg_bt=_rows_or(t, 1056),
                wo_bm=d, wo_bt=_rows_or(t, 1056) if layer > 0 else _mm_rows(t), ob_tn=d, bwd_tm=_rows_or(t, 1056))


def _full(shape):
    n = len(shape)
    return pl.BlockSpec(shape, lambda *_: (0,) * n)


def _peers(x, y, c):
    return [(x, y, 1 - c), (1 - x, y, c), (x, 1 - y, c), (1 - x, 1 - y, c),
            (1 - x, y, 1 - c), (x, 1 - y, 1 - c), (1 - x, 1 - y, 1 - c)]


def _lin(p):
    return 4 * p[0] + 2 * p[1] + p[2]


def _all_gather(arrays, name):
    n_arr = len(arrays)
    space = pltpu.VMEM

    def body(*refs):
        ins, outs = refs[:n_arr], refs[n_arr:2 * n_arr]
        send_sems, recv_sems, local_sems = refs[2 * n_arr:]
        x, y, c = lax.axis_index("x"), lax.axis_index("y"), lax.axis_index("c")
        me, sibling = (x, y, c), (x, y, 1 - c)
        chips = [(1 - x, y), (x, 1 - y), (1 - x, 1 - y)]
        every = []
        locals_ = []
        for a in range(n_arr):
            m_per = ins[a].shape[0]
            out_ref = outs[a]

            def rows(p, out_ref=out_ref, m_per=m_per):
                return out_ref.at[pl.ds(_lin(p) * m_per, m_per), :]

            def copy(k, block, to, src=None, a=a, rows=rows):
                return pltpu.make_async_remote_copy(
                    src_ref=rows(block) if src is None else src, dst_ref=rows(block),
                    send_sem=send_sems.at[a, k], recv_sem=recv_sems.at[a, k],
                    device_id=to, device_id_type=pl.DeviceIdType.MESH)

            mine = pltpu.make_async_copy(ins[a], rows(me), local_sems.at[a])
            mine.start()
            locals_.append(mine)
            first = [copy(0, me, sibling, src=ins[a])]
            first += [copy(1 + j, me, (*chip, c), src=ins[a]) for j, chip in enumerate(chips)]
            for cp in first:
                cp.start()
            every.append((copy, first))
        sends = []
        for a in range(n_arr):
            copy, first = every[a]
            passed = [copy(4 + j, (*chip, c), sibling) for j, chip in enumerate(chips)]
            for j, chip in enumerate(chips):
                copy(1 + j, (*chip, c), me).wait_recv()
                passed[j].start()
            sends += first + passed
        for a in range(n_arr):
            copy, _ = every[a]
            copy(0, sibling, me).wait_recv()
            for j, chip in enumerate(chips):
                copy(4 + j, (*chip, 1 - c), me).wait_recv()
        for cp in sends:
            cp.wait_send()
        for mine in locals_:
            mine.wait()

    outs = pl.pallas_call(
        body, name=name,
        out_shape=[jax.ShapeDtypeStruct((N_DEV * a.shape[0], a.shape[1]), a.dtype) for a in arrays],
        in_specs=[pl.BlockSpec(memory_space=space)] * n_arr,
        out_specs=[pl.BlockSpec(memory_space=space)] * n_arr,
        scratch_shapes=[pltpu.SemaphoreType.DMA((n_arr, 7)), pltpu.SemaphoreType.DMA((n_arr, 7)),
                        pltpu.SemaphoreType.DMA((n_arr,))],
        compiler_params=_params(),
    )(*arrays)
    return list(outs)


_HBM = pl.BlockSpec(memory_space=pltpu.HBM)
_SEM = pl.BlockSpec(memory_space=pltpu.SEMAPHORE)
_DATAFLOW = pltpu.SideEffectType.DATAFLOW_SIDE_EFFECTING


PUSH_COPIES = {"scatter": 7, "gather": 7, "near": 4, "relay": 3}


def _push_copies(src_refs, land_refs, send_sems, recv_sems, mode):
    x, y, c = lax.axis_index("x"), lax.axis_index("y"), lax.axis_index("c")
    me, sibling = (x, y, c), (x, y, 1 - c)
    n_k = PUSH_COPIES[mode]
    out, back = [], []
    if mode == "relay":
        for k, chip in enumerate([(1 - x, y), (x, 1 - y), (1 - x, 1 - y)]):
            for a, land in enumerate(land_refs):
                sems = dict(send_sem=send_sems.at[n_k * a + k], recv_sem=recv_sems.at[n_k * a + k],
                            device_id=sibling, device_id_type=pl.DeviceIdType.MESH)
                mine = land.at[_lin((*chip, c))]
                out.append(pltpu.make_async_remote_copy(src_ref=mine, dst_ref=mine, **sems))
                back.append(pltpu.make_async_remote_copy(src_ref=mine, dst_ref=land.at[_lin((*chip, 1 - c))], **sems))
        return out, back
    for k, peer in enumerate(_peers(x, y, c)[:n_k]):
        for a, (src, land) in enumerate(zip(src_refs, land_refs)):
            sems = dict(send_sem=send_sems.at[n_k * a + k], recv_sem=recv_sems.at[n_k * a + k],
                        device_id=peer, device_id_type=pl.DeviceIdType.MESH)
            mine = src.at[_lin(peer)] if mode == "scatter" else src
            out.append(pltpu.make_async_remote_copy(src_ref=mine, dst_ref=land.at[_lin(me)], **sems))
            back.append(pltpu.make_async_remote_copy(src_ref=mine, dst_ref=land.at[_lin(peer)], **sems))
    return out, back


def _push_start(srcs, lands, mode, name, after=()):
    n_src, n = len(srcs), len(lands)
    n_buf = n_src + n
    n_in = n_buf + len(after)
    n_sem = PUSH_COPIES[mode] * n

    def body(*refs):
        send_sems, recv_sems = refs[n_in], refs[n_in + 1]
        out, _ = _push_copies(refs[:n_src], refs[n_src:n_buf], send_sems, recv_sems, mode)
        for cp in out:
            cp.start()
        token = refs[-1]
        token[...] = jnp.zeros_like(token)

    both = list(srcs) + list(lands)
    res = pl.pallas_call(
        body, name=name,
        out_shape=[pltpu.SemaphoreType.DMA((n_sem,)), pltpu.SemaphoreType.DMA((n_sem,))]
        + [pltpu.HBM(a.shape, a.dtype) for a in both] + [jax.ShapeDtypeStruct((8, 128), F32)],
        in_specs=[_HBM] * n_buf + [pl.BlockSpec(memory_space=pl.ANY)] * len(after),
        out_specs=[_SEM, _SEM] + [_HBM] * n_buf + [pl.BlockSpec(memory_space=pltpu.VMEM)],
        input_output_aliases={i: 2 + i for i in range(n_buf)},
        compiler_params=pltpu.CompilerParams(has_side_effects=_DATAFLOW),
    )(*[pltpu.with_memory_space_constraint(a, pltpu.HBM) for a in both], *after)
    return res[0], res[1], list(res[2:2 + n_src]), list(res[2 + n_src:2 + n_buf]), res[-1]


def _push_wait(send_sems, recv_sems, srcs, lands, mode, after, name):
    n_src, n = len(srcs), len(lands)
    n_buf = n_src + n

    def body(*refs):
        out, back = _push_copies(refs[:n_src], refs[n_src:n_buf], refs[n_buf], refs[n_buf + 1], mode)
        for cp in out:
            cp.wait_send()
        for cp in back:
            cp.wait_recv()

    both = list(srcs) + list(lands)
    res = pl.pallas_call(
        body, name=name,
        out_shape=[pltpu.HBM(a.shape, a.dtype) for a in both],
        in_specs=[_HBM] * n_buf + [_SEM, _SEM, pl.BlockSpec(memory_space=pl.ANY)],
        out_specs=[_HBM] * n_buf,
        input_output_aliases={i: i for i in range(n_buf)},
        compiler_params=pltpu.CompilerParams(has_side_effects=_DATAFLOW),
    )(*both, send_sems, recv_sems, after)
    return list(res[n_src:])


def _landing(own, me):
    zone = lax.empty((N_DEV,) + own.shape, own.dtype)
    return lax.dynamic_update_slice(zone, own[None], (me,) + (0,) * own.ndim)


def _mod_rows(c9, w_mod, b_sh, name):
    n = w_mod.shape[1]

    def body(c_ref, w_ref, b_ref, o_ref):
        s9 = _silu(c_ref[...]).astype(BF16)
        o_ref[...] = _dot(s9, w_ref[...].astype(BF16), NN) + b_ref[...]

    return pl.pallas_call(body, name=name, out_shape=jax.ShapeDtypeStruct((16, n), F32),
                          compiler_params=_params())(c9, w_mod, b_sh)


def _mod_grads(dm_rows, dc_rows, c9, w_mod, name):
    d, n = w_mod.shape

    def body(dm_ref, dc_ref, c_ref, w_ref, gw_ref, dc_out):
        dc = dc_ref[...]
        tot = dc[0:1]
        for j in range(1, N_DEV):
            tot = tot + dc[j:j + 1]
        row = lax.broadcasted_iota(jnp.int32, (8, n), 0)
        lower = jnp.where(row == 0, tot, 0.0)
        dmod9 = jnp.concatenate([dm_ref[...], lower], axis=0).astype(BF16)
        c9v = c_ref[...]
        s9 = _silu(c9v).astype(BF16)
        gw_ref[...] = _dot(s9, dmod9, TN)
        ds = _dot(lower.astype(BF16), w_ref[...].astype(BF16), NT)
        dc_out[...] = ds * _dsilu(c9v[8:16])

    return pl.pallas_call(body, name=name,
                          out_shape=[jax.ShapeDtypeStruct((d, n), F32), jax.ShapeDtypeStruct((8, d), F32)],
                          compiler_params=_params())(dm_rows, dc_rows, c9, w_mod)


def _decay_tables(dec, n_heads, name):
    c = CHUNK

    def body(dec_ref, dc_ref, dlf_ref, dlb_ref, qf_ref, kf_ref, qb_ref, kb_ref, cdf_ref, cdb_ref, lg_ref):
        h = pl.program_id(0)
        d = dec_ref[...]
        lane = lax.broadcasted_iota(jnp.int32, d.shape, 1)
        lg = -jnp.exp(jnp.sum(jnp.where(lane == h, d, 0.0), axis=1, keepdims=True))
        lgf, lgb = lg[0:1], lg[1:2]
        i = lax.broadcasted_iota(jnp.int32, (c, c), 0).astype(F32)
        j = lax.broadcasted_iota(jnp.int32, (c, c), 1).astype(F32)
        diff = i - j
        d_f = jnp.where(diff >= 0, jnp.exp(lgf * jnp.maximum(diff, 0.0)), 0.0)
        d_b = jnp.where(diff <= 0, jnp.exp(lgb * jnp.maximum(-diff, 0.0)), 0.0)
        dc_ref[...] = d_f + d_b
        dlf_ref[...] = diff * d_f
        dlb_ref[...] = -diff * d_b
        pos = lax.broadcasted_iota(jnp.int32, (c, HEAD_DIM), 0).astype(F32)
        qf_ref[...] = jnp.exp(lgf * (pos + 1.0))
        kf_ref[...] = jnp.exp(lgf * (c - 1.0 - pos))
        qb_ref[...] = jnp.exp(lgb * (c - pos))
        kb_ref[...] = jnp.exp(lgb * pos)
        ones = jnp.ones((8, HEAD_DIM), F32)
        cdf_ref[...] = jnp.exp(lgf * float(c)) * ones
        cdb_ref[...] = jnp.exp(lgb * float(c)) * ones

        @pl.when(h == 0)
        def _():
            lg_ref[...] = jnp.zeros_like(lg_ref)

        row8 = lax.broadcasted_iota(jnp.int32, (8, HEAD_DIM), 0)
        lane8 = lax.broadcasted_iota(jnp.int32, (8, HEAD_DIM), 1)
        lg_ref[...] += (jnp.where((row8 == 0) & (lane8 == h), lgf, 0.0)
                        + jnp.where((row8 == 1) & (lane8 == h), lgb, 0.0))

    def per_head(*tail):
        return pl.BlockSpec((None,) + tail, lambda h: (h,) + (0,) * len(tail))

    shapes = [(c, c)] * 3 + [(c, HEAD_DIM)] * 4 + [(8, HEAD_DIM)] * 2
    return pl.pallas_call(
        body, name=name, grid=(n_heads,),
        in_specs=[_full(dec.shape)],
        out_specs=[per_head(*s) for s in shapes] + [_full((8, HEAD_DIM))],
        out_shape=[jax.ShapeDtypeStruct((n_heads,) + s, F32) for s in shapes]
        + [jax.ShapeDtypeStruct((8, HEAD_DIM), F32)],
        compiler_params=_params(("arbitrary",)),
    )(dec)


def _modulate(x, nw, shift, scale):
    r = lax.rsqrt(jnp.mean(x * x, axis=-1, keepdims=True) + EPS)
    return ((x * r) * nw * (1.0 + scale) + shift).astype(BF16)


def _split_rows(nxb, nb):
    def specs(d, step=lambda i: i):
        lat = pl.BlockSpec((ROW_TILE, d), lambda i: (jnp.minimum(step(i), nxb - 1), 0))
        ctx = pl.BlockSpec((ROW_TILE, d), lambda i: (jnp.clip(step(i) - nxb, 0, nb - nxb - 1), 0))
        return lat, ctx
    return specs


def _prenorm_first(x, ctx, nw, mod, name, after=()):
    n_lat, d = x.shape
    t = n_lat + ctx.shape[0]
    nxb = n_lat // ROW_TILE

    def body(x_ref, c_ref, nw_ref, mod_ref, *rest):
        o_ref = rest[-1]
        m = mod_ref[...]
        nw_v = nw_ref[...]

        @pl.when(pl.program_id(0) < nxb)
        def _():
            o_ref[...] = _modulate(x_ref[...], nw_v, m[0:1], m[1:2])

        @pl.when(pl.program_id(0) >= nxb)
        def _():
            o_ref[...] = _modulate(c_ref[...], nw_v, m[3:4], m[4:5])

    lat, cx = _split_rows(nxb, t // ROW_TILE)(d)
    return pl.pallas_call(
        body, name=name, grid=(t // ROW_TILE,),
        in_specs=[lat, cx, _full((1, d)), _full((8, d))] + [pl.BlockSpec(memory_space=pl.ANY)] * len(after),
        out_specs=pl.BlockSpec((ROW_TILE, d), lambda i: (i, 0)), out_shape=jax.ShapeDtypeStruct((t, d), BF16),
        compiler_params=_params(("parallel",)))(x, ctx, nw, mod, *after)


def _rope_fwd(v, cos, sa, sb):
    return v * cos + pltpu.roll(v, 96, 1) * sa + pltpu.roll(v, 32, 1) * sb


def _rope_bwd(g, cos, sa, sb):
    return g * cos + pltpu.roll(g * sa, 32, 1) + pltpu.roll(g * sb, 96, 1)


N_PLAIN = 5
U_DTYPE = BF16


def _in_proj(hx, wg, cos, sa, sb, s, part, tm, name, after=(), into=None):
    t, d = hx.shape
    n_seg, _, n = wg.shape
    nb = t // tm
    k_scale = HEAD_DIM ** -0.5
    kept = [] if into is None else list(into)

    def body(a_ref, w_ref, cos_ref, sa_ref, sb_ref, *rest):
        u_ref, qkv_ref = rest[-2:]
        g = pl.program_id(1)
        acc = _dot(a_ref[...], w_ref[...], NN)

        @pl.when(g < N_PLAIN)
        def _():
            u_ref[...] = acc.astype(U_DTYPE)

        @pl.when(g == N_PLAIN + 2)
        def _():
            qkv_ref[...] = acc.astype(BF16)

        for which, scale in ((N_PLAIN, 1.0), (N_PLAIN + 1, k_scale)):
            @pl.when(g == which)
            def _(scale=scale):
                co, a, b = cos_ref[...], sa_ref[...], sb_ref[...]
                for h in range(n // HEAD_DIM):
                    sl = slice(h * HEAD_DIM, (h + 1) * HEAD_DIM)
                    qkv_ref[:, sl] = (_rope_fwd(acc[:, sl], co, a, b) * scale).astype(BF16)

    def w_seg(g):
        return jnp.where(g < N_PLAIN - 1, g, jnp.where(g == N_PLAIN - 1, n_seg - 1, g - 1))

    def qkv_at(i, g):
        held = (jnp.where(i == 0, 0, 2), jnp.maximum(i - 1, 0))
        return (jnp.where(g < N_PLAIN, held[0], g - N_PLAIN), jnp.where(g < N_PLAIN, held[1], i), part)

    tab = pl.BlockSpec((tm, HEAD_DIM), lambda i, g: (i, 0))
    hbm = pl.BlockSpec(memory_space=pl.ANY)
    return pl.pallas_call(
        body, name=name, grid=(nb, n_seg),
        in_specs=[pl.BlockSpec((tm, d), lambda i, g: (i, 0)), pl.BlockSpec((None, d, n), lambda i, g: (w_seg(g), 0, 0)),
                  tab, tab, tab] + [hbm] * (len(after) + len(kept)),
        out_specs=[pl.BlockSpec((None, tm, n), lambda i, g: (jnp.minimum(g, N_PLAIN - 1), i, part)),
                   pl.BlockSpec((None, tm, n), qkv_at)],
        out_shape=[jax.ShapeDtypeStruct((N_PLAIN, t, s), U_DTYPE), jax.ShapeDtypeStruct((3, t, s), BF16)],
        input_output_aliases={5 + len(after) + j: j for j in range(len(kept))},
        compiler_params=_params(("arbitrary", "arbitrary")))(hx, wg, cos, sa, sb, *after, *kept)


def _pair_sweep(xs, ys, tab_f, tab_b, cdf, cdb, n_heads, nx, ncc, reverse, name):
    t, s = xs[0].shape[-2:]
    nc = nx + ncc
    c = CHUNK
    n_pair = nc // 2
    assert nx % 2 == 0 and ncc % 2 == 0

    def f_pair(i):
        step = n_pair - 1 - i if reverse else i
        return jnp.where(step < ncc // 2, nx // 2 + step, step - ncc // 2)

    def b_pair(i):
        return i if reverse else n_pair - 1 - i

    f_subs = (1, 0) if reverse else (0, 1)
    b_subs = (0, 1) if reverse else (1, 0)

    def body(xf_ref, yf_ref, xb_ref, yb_ref, tf, tb, cdf_ref, cdb_ref, sf_out, sb_out, sf, sb):
        @pl.when(pl.program_id(0) == 0)
        def _():
            sf[...] = jnp.zeros_like(sf)
            sb[...] = jnp.zeros_like(sb)

        for step in range(2):
            for x_ref, y_ref, tab, cd, out, st, sub in ((xf_ref, yf_ref, tf, cdf_ref, sf_out, sf, f_subs[step]),
                                                        (xb_ref, yb_ref, tb, cdb_ref, sb_out, sb, b_subs[step])):
                rows = pl.ds(sub * c, c)
                for h in range(n_heads):
                    sl = pl.ds(h * HEAD_DIM, HEAD_DIM)
                    out[sub, h] = st[h].astype(BF16)
                    xd = (x_ref[rows, sl].astype(F32) * tab[h]).astype(BF16)
                    st[h] = cd[h][0:1, :] * st[h] + _dot(xd, y_ref[rows, sl], TN)

    def spec(arr, pair):
        lead = arr[1]
        if lead is None:
            return pl.BlockSpec((2 * c, s), lambda i: (pair(i), 0))
        return pl.BlockSpec((None, 2 * c, s), lambda i: (lead, pair(i), 0))

    st_blk = (2, n_heads, HEAD_DIM, HEAD_DIM)
    return pl.pallas_call(
        body, name=name, grid=(n_pair,),
        in_specs=[spec(xs, f_pair), spec(ys, f_pair), spec(xs, b_pair), spec(ys, b_pair),
                  _full((n_heads, c, HEAD_DIM)), _full((n_heads, c, HEAD_DIM)),
                  _full((n_heads, 8, HEAD_DIM)), _full((n_heads, 8, HEAD_DIM))],
        out_specs=[pl.BlockSpec(st_blk, lambda i: (f_pair(i), 0, 0, 0)), pl.BlockSpec(st_blk, lambda i: (b_pair(i), 0, 0, 0))],
        out_shape=[jax.ShapeDtypeStruct((nc, n_heads, HEAD_DIM, HEAD_DIM), BF16)] * 2,
        scratch_shapes=[pltpu.VMEM((n_heads, HEAD_DIM, HEAD_DIM), F32)] * 2,
        compiler_params=_params(("arbitrary",)),
    )(xs[0], ys[0], xs[0], ys[0], tab_f, tab_b, cdf, cdb)


def _state_sweep(qkv, tabs, n_heads, nx, ncc, name):
    return _pair_sweep((qkv, 1), (qkv, 2), tabs["kf"], tabs["kb"], tabs["cdf"], tabs["cdb"], n_heads, nx, ncc, False, name)


MIX_CHUNKS = 2
MIX_ROWS = MIX_CHUNKS * CHUNK


HALO = 16


def _halo_specs(s, t):
    per = MIX_ROWS // HALO
    n_halo = t // HALO

    def prev(g):
        return pl.BlockSpec((None, HALO, s), lambda i: (g, jnp.maximum(i * per - 1, 0), 0))

    def nxt(g):
        return pl.BlockSpec((None, HALO, s), lambda i: (g, jnp.minimum((i + 1) * per, n_halo - 1), 0))

    return prev, nxt


def _conv_input(h_ref, c_ref, hp_ref, hn_ref, cp_ref, cn_ref):
    a = c_ref[...].astype(F32) * h_ref[...].astype(F32)
    before = cp_ref[HALO - 1:HALO].astype(F32) * hp_ref[HALO - 1:HALO].astype(F32)
    after = cn_ref[0:1].astype(F32) * hn_ref[0:1].astype(F32)
    return a, before, after


def _shifted(a, before, after, has_prev, has_next):
    rows = a.shape[0]
    rowi = lax.broadcasted_iota(jnp.int32, a.shape, 0)
    am = jnp.where(rowi == 0, jnp.where(has_prev, before, 0.0), pltpu.roll(a, 1, 0))
    ap = jnp.where(rowi == rows - 1, jnp.where(has_next, after, 0.0), pltpu.roll(a, rows - 1, 0))
    return am, ap


def _neighbours(i, nx, nc):
    nxb, ncb = nx // MIX_CHUNKS, nc // MIX_CHUNKS
    return (i != 0) & (i != nxb), (i != nxb - 1) & (i != ncb - 1)


def _mix_fwd(u, qkv, sf, sb, tabs, conv_w, cnw, gnw, n_heads, nx, ncc, name):
    _, t, s = u.shape
    nc = nx + ncc
    c = CHUNK
    assert nx % MIX_CHUNKS == 0 and ncc % MIX_CHUNKS == 0

    def body(h_ref, b_ref, c_ref, z_ref, rz_ref, hp_ref, hn_ref, cp_ref, cn_ref, q_ref, k_ref, v_ref,
             sf_ref, sb_ref, dc_ref, qft, qbt, w_ref, cnw_ref, gnw_ref, y_ref, o_ref):
        i = pl.program_id(0)
        has_prev, has_next = _neighbours(i, nx, nc)
        a, before, after = _conv_input(h_ref, c_ref, hp_ref, hn_ref, cp_ref, cn_ref)
        am, ap = _shifted(a, before, after, has_prev, has_next)
        w = w_ref[...]
        y0 = w[0:1] * am + w[1:2] * a + w[2:3] * ap
        yb = b_ref[...].astype(F32) * y0
        r = lax.rsqrt(jnp.mean(yb * yb, axis=-1, keepdims=True) + EPS)
        y_ref[:, pl.ds(0, s)] = (_silu(z_ref[...].astype(F32)) * ((yb * r) * cnw_ref[...])).astype(BF16)
        for sub in range(MIX_CHUNKS):
            rows = pl.ds(sub * c, c)
            for h in range(n_heads):
                sl = pl.ds(h * HEAD_DIM, HEAD_DIM)
                q, k, v = q_ref[rows, sl], k_ref[rows, sl], v_ref[rows, sl]
                p = (_dot(q, k, NT) * dc_ref[h]).astype(BF16)
                o = _dot(p, v, NN)
                qf = q.astype(F32)
                o += _dot((qf * qft[h]).astype(BF16), sf_ref[sub, h], NN)
                o += _dot((qf * qbt[h]).astype(BF16), sb_ref[sub, h], NN)
                o_ref[rows, sl] = o
                mu = jnp.mean(o, axis=-1, keepdims=True)
                var = jnp.mean(jnp.square(o - mu), axis=-1, keepdims=True)
                on = (o - mu) * lax.rsqrt(var + EPS)
                y_ref[rows, pl.ds(s + h * HEAD_DIM, HEAD_DIM)] = (
                    _silu(rz_ref[rows, sl].astype(F32)) * (on * gnw_ref[:, sl])).astype(BF16)

    def seg(g):
        return pl.BlockSpec((None, MIX_ROWS, s), lambda i: (g, i, 0))

    prev, nxt = _halo_specs(s, t)
    row = pl.BlockSpec((MIX_ROWS, s), lambda i: (i, 0))
    st = pl.BlockSpec((MIX_CHUNKS, n_heads, HEAD_DIM, HEAD_DIM), lambda i: (i, 0, 0, 0))
    return pl.pallas_call(
        body, name=name, grid=(nc // MIX_CHUNKS,),
        in_specs=[seg(0), seg(1), seg(2), seg(3), seg(4), prev(0), nxt(0), prev(2), nxt(2), seg(0), seg(1), seg(2),
                  st, st, _full((n_heads, c, c)), _full((n_heads, c, HEAD_DIM)), _full((n_heads, c, HEAD_DIM)),
                  _full((3, s)), _full((1, s)), _full((1, s))],
        out_specs=[pl.BlockSpec((MIX_ROWS, 2 * s), lambda i: (i, 0)), row],
        out_shape=[jax.ShapeDtypeStruct((t, 2 * s), BF16), jax.ShapeDtypeStruct((t, s), F32)],
        compiler_params=_params(("parallel",)),
    )(u, u, u, u, u, u, u, u, u, qkv, qkv, qkv, sf, sb, tabs["dc"], tabs["qf"], tabs["qb"], conv_w, cnw, gnw)


def _out_proj_prenorm(ycat, w_out, res, mod, nw_next, mod_next, n_lat, name):
    t, d = ycat.shape
    nb = t // ROW_TILE
    nxb = n_lat // ROW_TILE
    split = len(res) == 2

    def body(a_ref, w_ref, *rest):
        res_refs = rest[:len(res)]
        mod_ref, nw_ref, modn_ref, m_ref, xo_ref, hx_ref, xs = rest[len(res):]
        i = pl.program_id(0)

        @pl.when(i == 0)
        def _():
            xs[...] = jnp.zeros_like(xs)

        cur_ctx = jnp.minimum(i, nb - 1) >= nxb
        prev_ctx = i - 1 >= nxb

        def step(cur, prev):
            mv, mn = mod_ref[...], modn_ref[...]
            shift = jnp.where(prev_ctx, mn[3:4], mn[0:1])
            scale = jnp.where(prev_ctx, mn[4:5], mn[1:2])
            hx_ref[...] = _modulate(xs[prev], nw_ref[...], shift, scale)
            m = _dot(a_ref[...], w_ref[...], NN)
            x_res = jnp.where(cur_ctx, res_refs[1][...], res_refs[0][...]) if split else res_refs[0][...]
            x_new = x_res + jnp.where(cur_ctx, mv[5:6], mv[2:3]) * m
            m_ref[...] = m.astype(BF16)
            xo_ref[...] = x_new
            xs[cur] = x_new

        @pl.when(i % 2 == 0)
        def _():
            step(0, 1)

        @pl.when(i % 2 == 1)
        def _():
            step(1, 0)

    cur = pl.BlockSpec((ROW_TILE, d), lambda i: (jnp.minimum(i, nb - 1), 0))
    prev = pl.BlockSpec((ROW_TILE, d), lambda i: (jnp.maximum(i - 1, 0), 0))
    res_specs = list(_split_rows(nxb, nb)(d, lambda i: jnp.minimum(i, nb - 1))) if split else [cur]
    return pl.pallas_call(
        body, name=name, grid=(nb + 1,),
        in_specs=[cur, _full((d, d))] + res_specs + [_full((8, d)), _full((1, d)), _full((8, d))],
        out_specs=[cur, cur, prev],
        out_shape=[jax.ShapeDtypeStruct((t, d), BF16), jax.ShapeDtypeStruct((t, d), F32),
                   jax.ShapeDtypeStruct((t, d), BF16)],
        scratch_shapes=[pltpu.VMEM((2, ROW_TILE, d), F32)],
        compiler_params=_params(("arbitrary",)))(ycat, w_out, *res, mod, nw_next, mod_next)


def _out_proj_loss(ycat, w_out, xt, mod, tgt, fnw, n_lat, name):
    t, d = xt.shape
    nb = t // ROW_TILE
    nxb = n_lat // ROW_TILE

    def body(a_ref, w_ref, x_ref, mod_ref, t_ref, fw_ref, dx_ref, dm_ref, loss_ref, dw_ref, gacc_ref, xs, ms):
        i = pl.program_id(0)

        @pl.when(i == 0)
        def _():
            xs[...] = jnp.zeros_like(xs)
            ms[...] = jnp.zeros_like(ms)
            loss_ref[...] = jnp.zeros_like(loss_ref)
            dw_ref[...] = jnp.zeros_like(dw_ref)
            gacc_ref[...] = jnp.zeros_like(gacc_ref)

        def step(cur, prev):
            mv = mod_ref[...]
            x_prev, m_prev = xs[prev], ms[prev]
            valid = (i >= 1) & (i - 1 < nxb)
            w = fw_ref[...]
            r = lax.rsqrt(jnp.mean(x_prev * x_prev, axis=-1, keepdims=True) + EPS)
            xn = x_prev * r
            e = xn * w - t_ref[...]
            loss = 0.5 * jnp.sum(jnp.mean(e * e, axis=-1, keepdims=True), axis=0, keepdims=True)
            loss_ref[...] += jnp.where(valid, loss, 0.0)
            dy = e * (1.0 / d)
            dw_ref[0:1, :] += jnp.where(valid, jnp.sum(dy * xn, axis=0, keepdims=True), 0.0)
            dxn = dy * w
            dx = jnp.where(valid, r * (dxn - xn * jnp.mean(dxn * xn, axis=-1, keepdims=True)), 0.0)
            dx_ref[...] = dx
            dm_ref[...] = (dx * mv[2:3]).astype(BF16)
            gacc_ref[2:3, :] += jnp.sum(dx * m_prev, axis=0, keepdims=True)

            m = _dot(a_ref[...], w_ref[...], NN)
            gate = jnp.where(jnp.minimum(i, nb - 1) >= nxb, mv[5:6], mv[2:3])
            xs[cur] = x_ref[...] + gate * m
            ms[cur] = m

        @pl.when(i % 2 == 0)
        def _():
            step(0, 1)

        @pl.when(i % 2 == 1)
        def _():
            step(1, 0)

    cur = pl.BlockSpec((ROW_TILE, d), lambda i: (jnp.minimum(i, nb - 1), 0))
    prev = pl.BlockSpec((ROW_TILE, d), lambda i: (jnp.maximum(i - 1, 0), 0))
    return pl.pallas_call(
        body, name=name, grid=(nb + 1,),
        in_specs=[cur, _full((d, d)), cur, _full((8, d)),
                  pl.BlockSpec((ROW_TILE, d), lambda i: (jnp.clip(i - 1, 0, nxb - 1), 0)), _full((1, d))],
        out_specs=[prev, prev, _full((8, HEAD_DIM)), _full((8, d)), _full((8, d))],
        out_shape=[jax.ShapeDtypeStruct((t, d), F32), jax.ShapeDtypeStruct((t, d), BF16),
                   jax.ShapeDtypeStruct((8, HEAD_DIM), F32), jax.ShapeDtypeStruct((8, d), F32),
                   jax.ShapeDtypeStruct((8, d), F32)],
        scratch_shapes=[pltpu.VMEM((2, ROW_TILE, d), F32), pltpu.VMEM((2, ROW_TILE, d), F32)],
        compiler_params=_params(("arbitrary",)))(ycat, w_out, xt, mod, tgt, fnw)


def _matmul_nt(a, w, tn, name, after=()):
    t, k = a.shape
    n = w.shape[0]
    tm = _mm_rows(t)

    def body(a_ref, w_ref, *rest):
        rest[-1][...] = _dot(a_ref[...], w_ref[...], NT)

    return pl.pallas_call(
        body, name=name, grid=(n // tn, t // tm),
        in_specs=[pl.BlockSpec((tm, k), lambda j, i: (i, 0)), pl.BlockSpec((tn, k), lambda j, i: (j, 0))]
        + [pl.BlockSpec(memory_space=pl.ANY)] * len(after),
        out_specs=pl.BlockSpec((tm, tn), lambda j, i: (i, j)),
        out_shape=jax.ShapeDtypeStruct((t, n), F32),
        compiler_params=_params(("parallel", "parallel")))(a, w, *after)


def _weight_grad(a, b, bm, bt, name):
    t, m = a.shape
    n_g, _, n = b.shape
    nt = t // bt

    def body(a_ref, b_ref, o_ref, acc):
        k = pl.program_id(2)

        @pl.when(k == 0)
        def _():
            acc[...] = jnp.zeros_like(acc)

        acc[...] += _dot(a_ref[...], b_ref[...], TN)

        @pl.when(k == nt - 1)
        def _():
            o_ref[...] = acc[...].astype(o_ref.dtype)

    return pl.pallas_call(
        body, name=name, grid=(n_g, m // bm, nt),
        in_specs=[pl.BlockSpec((bt, bm), lambda g, i, k: (k, i)), pl.BlockSpec((None, bt, n), lambda g, i, k: (g, k, 0))],
        out_specs=pl.BlockSpec((None, bm, n), lambda g, i, k: (g, i, 0)),
        out_shape=jax.ShapeDtypeStruct((n_g, m, n), BF16),
        scratch_shapes=[pltpu.VMEM((bm, n), F32)],
        compiler_params=_params(("parallel", "parallel", "arbitrary")))(a, b)


def _weight_grad_beside_prenorm_bwd(a, b, dhx, xt, dxo, nw, mod, below, n_lat, name):
    t, m = a.shape
    n_g, _, n = b.shape
    d = xt.shape[1]
    bt = _mm_rows(t)
    nt = t // bt
    rows = t // (n_g * nt)
    n_piece = 2 if rows % 32 == 0 and m % 2 == 0 else 1
    rows_p, m_p = rows // n_piece, m // n_piece
    assert rows * n_g * nt == t and rows_p % 8 == 0

    def body(a_ref, b_ref, dh_ref, x_ref, dxo_ref, nw_ref, mod_ref, m_ref, modb_ref,
             o_ref, dx_ref, acc_ref, dm_ref, gacc_ref, acc):
        g, k = pl.program_id(0), pl.program_id(1)
        step = g * nt + k

        @pl.when(step == 0)
        def _():
            acc_ref[...] = jnp.zeros_like(acc_ref)
            gacc_ref[...] = jnp.zeros_like(gacc_ref)

        @pl.when(k == 0)
        def _():
            acc[...] = jnp.zeros_like(acc)

        mv, mb, nw_v = mod_ref[...], modb_ref[...], nw_ref[...]
        for p in range(n_piece):
            rs = pl.ds(p * rows_p, rows_p)
            rowi = step * rows + p * rows_p + lax.broadcasted_iota(jnp.int32, (rows_p, 1), 0)
            ctx = rowi >= n_lat
            w_lat = jnp.where(ctx, 0.0, 1.0)
            w_ctx = 1.0 - w_lat
            scale1 = 1.0 + jnp.where(ctx, mv[4:5], mv[1:2])
            x = x_ref[rs, :]
            r = lax.rsqrt(jnp.mean(x * x, axis=-1, keepdims=True) + EPS)
            xn = x * r
            dh = dh_ref[rs, :]
            dsc = dh * (xn * nw_v)
            acc_ref[0:1, :] += jnp.sum(dh * w_lat, axis=0, keepdims=True)
            acc_ref[1:2, :] += jnp.sum(dsc * w_lat, axis=0, keepdims=True)
            acc_ref[3:4, :] += jnp.sum(dh * w_ctx, axis=0, keepdims=True)
            acc_ref[4:5, :] += jnp.sum(dsc * w_ctx, axis=0, keepdims=True)
            acc_ref[6:7, :] += jnp.sum(dh * scale1 * xn, axis=0, keepdims=True)
            dxn = dh * (nw_v * scale1)
            dx = dxo_ref[rs, :] + r * (dxn - xn * jnp.mean(dxn * xn, axis=-1, keepdims=True))
            dx_ref[rs, :] = dx
            dm_ref[rs, :] = (dx * jnp.where(ctx, mb[5:6], mb[2:3])).astype(BF16)
            dg = dx * m_ref[rs, :].astype(F32)
            gacc_ref[2:3, :] += jnp.sum(dg * w_lat, axis=0, keepdims=True)
            gacc_ref[5:6, :] += jnp.sum(dg * w_ctx, axis=0, keepdims=True)

            ms_ = pl.ds(p * m_p, m_p)
            acc[ms_, :] += _dot(a_ref[:, ms_], b_ref[...], TN)

        @pl.when(k == nt - 1)
        def _():
            o_ref[...] = acc[...].astype(o_ref.dtype)

    side = pl.BlockSpec((rows, d), lambda g, k: (g * nt + k, 0))
    acc8 = _full((8, d))
    return pl.pallas_call(
        body, name=name, grid=(n_g, nt),
        in_specs=[pl.BlockSpec((bt, m), lambda g, k: (k, 0)), pl.BlockSpec((None, bt, n), lambda g, k: (g, k, 0)),
                  side, side, side, _full((1, d)), acc8, side, acc8],
        out_specs=[pl.BlockSpec((None, m, n), lambda g, k: (g, 0, 0)), side, acc8, side, acc8],
        out_shape=[jax.ShapeDtypeStruct((n_g, m, n), BF16), jax.ShapeDtypeStruct((t, d), F32),
                   jax.ShapeDtypeStruct((8, d), F32), jax.ShapeDtypeStruct((t, d), BF16),
                   jax.ShapeDtypeStruct((8, d), F32)],
        scratch_shapes=[pltpu.VMEM((m, n), F32)],
        compiler_params=_params(("arbitrary", "arbitrary")))(a, b, dhx, xt, dxo, nw, mod, *below)


def _mix_bwd_a(dycat, u, o, conv_w, cnw, gnw, n_heads, nx, ncc, name):
    _, t, s = u.shape
    nc = nx + ncc

    def body(dy_ref, h_ref, b_ref, c_ref, z_ref, rz_ref, hp_ref, hn_ref, cp_ref, cn_ref, o_ref, w_ref,
             cnw_ref, gnw_ref, g_ref, dz_ref, db_ref, drz_ref, do_ref, acc_ref):
        i = pl.program_id(0)

        @pl.when(i == 0)
        def _():
            acc_ref[...] = jnp.zeros_like(acc_ref)

        has_prev, has_next = _neighbours(i, nx, nc)
        a, before, after = _conv_input(h_ref, c_ref, hp_ref, hn_ref, cp_ref, cn_ref)
        am, ap = _shifted(a, before, after, has_prev, has_next)
        w = w_ref[...]
        y0 = w[0:1] * am + w[1:2] * a + w[2:3] * ap
        bb = b_ref[...].astype(F32)
        yb = bb * y0
        r = lax.rsqrt(jnp.mean(yb * yb, axis=-1, keepdims=True) + EPS)
        ynn = yb * r
        z = z_ref[...].astype(F32)
        dyc = dy_ref[:, pl.ds(0, s)]
        cw = cnw_ref[...]
        sz, dsz = _silu_and_slope(z)
        dz_ref[...] = (dyc * (ynn * cw) * dsz).astype(BF16)
        dyn = dyc * sz
        acc_ref[0:1, :] += jnp.sum(dyn * ynn, axis=0, keepdims=True)
        dynn = dyn * cw
        dyb = r * (dynn - ynn * jnp.mean(dynn * ynn, axis=-1, keepdims=True))
        db_ref[...] = (dyb * y0).astype(BF16)
        g_ref[...] = dyb * bb
        for h in range(n_heads):
            sl = pl.ds(h * HEAD_DIM, HEAD_DIM)
            ov = o_ref[:, sl]
            mu = jnp.mean(ov, axis=-1, keepdims=True)
            var = jnp.mean(jnp.square(ov - mu), axis=-1, keepdims=True)
            rs = lax.rsqrt(var + EPS)
            on = (ov - mu) * rs
            dyr = dy_ref[:, pl.ds(s + h * HEAD_DIM, HEAD_DIM)]
            rz = rz_ref[:, sl].astype(F32)
            gw = gnw_ref[:, sl]
            srz, dsrz = _silu_and_slope(rz)
            drz_ref[:, sl] = (dyr * (on * gw) * dsrz).astype(BF16)
            dyg = dyr * srz
            acc_ref[1:2, sl] += jnp.sum(dyg * on, axis=0, keepdims=True)
            don = dyg * gw
            do = rs * (don - jnp.mean(don, axis=-1, keepdims=True)
                       - on * jnp.mean(don * on, axis=-1, keepdims=True))
            do_ref[:, sl] = do.astype(BF16)

    def seg(g):
        return pl.BlockSpec((None, MIX_ROWS, s), lambda i: (g, i, 0))

    prev, nxt = _halo_specs(s, t)
    row = pl.BlockSpec((MIX_ROWS, s), lambda i: (i, 0))
    return pl.pallas_call(
        body, name=name, grid=(nc // MIX_CHUNKS,),
        in_specs=[pl.BlockSpec((MIX_ROWS, 2 * s), lambda i: (i, 0)), seg(0), seg(1), seg(2), seg(3), seg(4),
                  prev(0), nxt(0), prev(2), nxt(2), row, _full((3, s)), _full((1, s)), _full((1, s))],
        out_specs=[row, row, row, row, row, _full((8, s))],
        out_shape=[jax.ShapeDtypeStruct((t, s), F32)] + [jax.ShapeDtypeStruct((t, s), BF16)] * 4
        + [jax.ShapeDtypeStruct((8, s), F32)],
        compiler_params=_params(("arbitrary",)),
    )(dycat, u, u, u, u, u, u, u, u, u, o, conv_w, cnw, gnw)


def _grad_state_sweep(qkv, do, tabs, n_heads, nx, ncc, name):
    return _pair_sweep((qkv, 0), (do, None), tabs["qf"], tabs["qb"], tabs["cdf"], tabs["cdb"], n_heads, nx, ncc, True, name)


def _mix_bwd_b(u, g, dz, db, drz, qkv, do, sf, sb, gf, gb, tabs, cos, sa, sb_tab, conv_w,
               n_heads, nx, ncc, name):
    _, t, s = u.shape
    nc = nx + ncc
    c = CHUNK
    k_scale = HEAD_DIM ** -0.5

    def body(h_ref, c_ref, g_ref, gp_ref, gn_ref, dz_ref, db_ref, drz_ref, q_ref, k_ref, v_ref, do_ref,
             sf_ref, sb_ref, gf_ref, gb_ref, dc_t, dlf_t, dlb_t, qft, kft, qbt, kbt, cdf, cdb, lg_ref,
             cos_ref, sa_ref, sb_ref2, w_ref, du_ref, dw_ref, dlg_ref):
        i = pl.program_id(0)

        @pl.when(i == 0)
        def _():
            dw_ref[...] = jnp.zeros_like(dw_ref)
            dlg_ref[...] = jnp.zeros_like(dlg_ref)

        has_prev, has_next = _neighbours(i, nx, nc)
        gv = g_ref[...]
        gm, gp = _shifted(gv, gp_ref[7:8], gn_ref[0:1], has_prev, has_next)
        w = w_ref[...]
        da = w[0:1] * gp + w[1:2] * gv + w[2:3] * gm
        hh, cc = h_ref[...].astype(F32), c_ref[...].astype(F32)
        du_ref[0] = (da * cc).astype(BF16)
        du_ref[2] = (da * hh).astype(BF16)
        a = cc * hh
        dw_ref[0:1, :] += jnp.sum(a * gp, axis=0, keepdims=True)
        dw_ref[1:2, :] += jnp.sum(a * gv, axis=0, keepdims=True)
        dw_ref[2:3, :] += jnp.sum(a * gm, axis=0, keepdims=True)
        du_ref[1] = db_ref[...]
        du_ref[3] = dz_ref[...]
        du_ref[7] = drz_ref[...]

        pos = lax.broadcasted_iota(jnp.int32, (c, HEAD_DIM), 0).astype(F32)
        w_q_f, w_q_b, w_k_f = pos + 1.0, c - pos, c - 1.0 - pos
        row8 = lax.broadcasted_iota(jnp.int32, (8, HEAD_DIM), 0)
        lane8 = lax.broadcasted_iota(jnp.int32, (8, HEAD_DIM), 1)
        dlg = jnp.zeros((8, HEAD_DIM), F32)
        for sub, h in [(sub, h) for sub in range(MIX_CHUNKS) for h in range(n_heads)]:
            rows = pl.ds(sub * c, c)
            co, ra, rb = cos_ref[rows, :], sa_ref[rows, :], sb_ref2[rows, :]
            sl = pl.ds(h * HEAD_DIM, HEAD_DIM)
            q, k, v, do = q_ref[rows, sl], k_ref[rows, sl], v_ref[rows, sl], do_ref[rows, sl]
            qf, kf, dof = q.astype(F32), k.astype(F32), do.astype(F32)
            s_f, s_b, g_f, g_b = sf_ref[sub, h], sb_ref[sub, h], gf_ref[sub, h], gb_ref[sub, h]
            p = _dot(q, k, NT)
            pd = _dot(do, v, NT)
            pdd = (pd * dc_t[h]).astype(BF16)
            dq = _dot(pdd, k, NN)
            dk = _dot(pdd, q, TN)
            dv = _dot((p * dc_t[h]).astype(BF16), do, TN)
            dq_f = _dot((dof * qft[h]).astype(BF16), s_f, NT)
            dq_b = _dot((dof * qbt[h]).astype(BF16), s_b, NT)
            dk_f = _dot(v, g_f, NT) * kft[h]
            dk_b = _dot(v, g_b, NT) * kbt[h]
            dv += _dot((kf * kft[h]).astype(BF16), g_f, NN) + _dot((kf * kbt[h]).astype(BF16), g_b, NN)
            ppd = p * pd
            cd_f, cd_b = cdf[h][0:1, :], cdb[h][0:1, :]
            t_f = _sum_all(dlf_t[h] * ppd + w_q_f * qf * dq_f + w_k_f * kf * dk_f
                           + float(c) * (cd_f * (g_f.astype(F32) * s_f.astype(F32))))
            t_b = _sum_all(dlb_t[h] * ppd + w_q_b * qf * dq_b + pos * kf * dk_b
                           + float(c) * (cd_b * (g_b.astype(F32) * s_b.astype(F32))))
            dlg += jnp.where((row8 == 0) & (lane8 == h), t_f, 0.0) + jnp.where((row8 == 1) & (lane8 == h), t_b, 0.0)
            du_ref[4, rows, sl] = _rope_bwd(dq + dq_f + dq_b, co, ra, rb).astype(BF16)
            du_ref[5, rows, sl] = (_rope_bwd(dk + dk_f + dk_b, co, ra, rb) * k_scale).astype(BF16)
            du_ref[6, rows, sl] = dv.astype(BF16)
        dlg_ref[...] += dlg

        @pl.when(i == nc // MIX_CHUNKS - 1)
        def _():
            dlg_ref[...] = dlg_ref[...] * lg_ref[...]

    def seg(gi):
        return pl.BlockSpec((None, MIX_ROWS, s), lambda i: (gi, i, 0))

    per = MIX_ROWS // 8
    n8 = t // 8
    row = pl.BlockSpec((MIX_ROWS, s), lambda i: (i, 0))
    st = pl.BlockSpec((MIX_CHUNKS, n_heads, HEAD_DIM, HEAD_DIM), lambda i: (i, 0, 0, 0))
    tab = pl.BlockSpec((MIX_ROWS, HEAD_DIM), lambda i: (i, 0))
    hc = _full((n_heads, c, HEAD_DIM))
    cc_ = _full((n_heads, c, c))
    h8 = _full((n_heads, 8, HEAD_DIM))
    return pl.pallas_call(
        body, name=name, grid=(nc // MIX_CHUNKS,),
        in_specs=[seg(0), seg(2), row,
                  pl.BlockSpec((8, s), lambda i: (jnp.maximum(i * per - 1, 0), 0)),
                  pl.BlockSpec((8, s), lambda i: (jnp.minimum((i + 1) * per, n8 - 1), 0)),
                  row, row, row, seg(0), seg(1), seg(2), row, st, st, st, st, cc_, cc_, cc_, hc, hc, hc, hc, h8, h8,
                  _full((8, HEAD_DIM)), tab, tab, tab, _full((3, s))],
        out_specs=[pl.BlockSpec((8, MIX_ROWS, s), lambda i: (0, i, 0)), _full((8, s)), _full((8, HEAD_DIM))],
        out_shape=[jax.ShapeDtypeStruct((8, t, s), BF16), jax.ShapeDtypeStruct((8, s), F32),
                   jax.ShapeDtypeStruct((8, HEAD_DIM), F32)],
        compiler_params=_params(("arbitrary",)),
    )(u, u, g, g, g, dz, db, drz, qkv, qkv, qkv, do, sf, sb, gf, gb, tabs["dc"], tabs["dlf"], tabs["dlb"],
      tabs["qf"], tabs["kf"], tabs["qb"], tabs["kb"], tabs["cdf"], tabs["cdb"], tabs["lg"], cos, sa, sb_tab, conv_w)


def _in_proj_bwd(du, wgs, tm, gs, name, after=()):
    n_seg, t, s = du.shape
    d = wgs[0].shape[1]
    n_w = len(wgs)
    widths = [w.shape[2] for w in wgs]
    assert sum(widths) == s

    def body(a_ref, *rest):
        w_refs, o_ref = rest[:n_w], rest[-1]
        g = pl.program_id(1)
        part = None
        for j in range(gs):
            col = 0
            for w_ref, width in zip(w_refs, widths):
                term = _dot(a_ref[j, :, col:col + width], w_ref[j], NT)
                part = term if part is None else part + term
                col += width

        @pl.when(g == 0)
        def _():
            o_ref[...] = part

        @pl.when(g > 0)
        def _():
            o_ref[...] += part

    return pl.pallas_call(
        body, name=name, grid=(t // tm, n_seg // gs),
        in_specs=[pl.BlockSpec((gs, tm, s), lambda i, g: (g, i, 0))]
        + [pl.BlockSpec((gs, d, width), lambda i, g: (g, 0, 0)) for width in widths]
        + [pl.BlockSpec(memory_space=pl.ANY)] * len(after),
        out_specs=pl.BlockSpec((tm, d), lambda i, g: (i, 0)),
        out_shape=jax.ShapeDtypeStruct((t, d), F32),
        compiler_params=_params(("parallel", "arbitrary")))(du, *wgs, *after)


def _prenorm_bwd_first(dhx, x, ctx, dxo, nw, mod, name):
    n_lat, d = x.shape
    t = n_lat + ctx.shape[0]
    nxb = n_lat // ROW_TILE

    def body(dh_ref, x_ref, c_ref, dxo_ref, nw_ref, mod_ref, dx_ref, acc_ref):
        i = pl.program_id(0)

        @pl.when(i == 0)
        def _():
            acc_ref[...] = jnp.zeros_like(acc_ref)

        ctx = i >= nxb
        m = mod_ref[...]
        scale1 = 1.0 + jnp.where(ctx, m[4:5], m[1:2])
        x = jnp.where(ctx, c_ref[...], x_ref[...])
        nw_v = nw_ref[...]
        r = lax.rsqrt(jnp.mean(x * x, axis=-1, keepdims=True) + EPS)
        xn = x * r
        dh = dh_ref[...]
        dshift = jnp.sum(dh, axis=0, keepdims=True)
        dscale = jnp.sum(dh * (xn * nw_v), axis=0, keepdims=True)
        acc_ref[6:7, :] += jnp.sum(dh * scale1 * xn, axis=0, keepdims=True)
        dxn = dh * (nw_v * scale1)
        dx = dxo_ref[...] + r * (dxn - xn * jnp.mean(dxn * xn, axis=-1, keepdims=True))

        @pl.when(i < nxb)
        def _():
            acc_ref[0:1, :] += dshift
            acc_ref[1:2, :] += dscale
            dx_ref[...] = dx

        @pl.when(i >= nxb)
        def _():
            acc_ref[3:4, :] += dshift
            acc_ref[4:5, :] += dscale

    row = pl.BlockSpec((ROW_TILE, d), lambda i: (i, 0))
    lat, cx = _split_rows(nxb, t // ROW_TILE)(d)
    acc = _full((8, d))
    return pl.pallas_call(body, name=name, grid=(t // ROW_TILE,),
                          in_specs=[row, lat, cx, row, _full((1, d)), acc],
                          out_specs=[lat, acc],
                          out_shape=[jax.ShapeDtypeStruct((n_lat, d), F32), jax.ShapeDtypeStruct((8, d), F32)],
                          compiler_params=_params(("arbitrary",)))(dhx, x, ctx, dxo, nw, mod)


def _adamw(g, w, m, v):
    m = ADAM_B1 * m + (1.0 - ADAM_B1) * g
    v = ADAM_B2 * v + (1.0 - ADAM_B2) * jnp.square(g)
    m_hat = m / (1.0 - ADAM_B1 ** ADAM_STEP)
    v_hat = v / (1.0 - ADAM_B2 ** ADAM_STEP)
    delta = -ADAM_LR * (m_hat / (jnp.sqrt(v_hat) + ADAM_EPS) + ADAM_WD * w)
    return delta, m, v


def _sum_adamw(parts, w, m, v, name, row0=0, into=None):
    n_p, r, n = parts.shape
    r_all = w.shape[0]
    part_block_bytes = 4 * 1024 * 1024
    br = 8
    for cand in (512, 256, 128, 64, 32, 16):
        if r % cand == 0 and row0 % cand == 0 and n_p * cand * n * parts.dtype.itemsize <= part_block_bytes:
            br = cand
            break
    blk0 = row0 // br

    def body(p_ref, w_ref, m_ref, v_ref, *rest):
        g_out, d_out, m_out, v_out = rest[-4:]
        g = p_ref[0].astype(F32)
        for j in range(1, n_p):
            g = g + p_ref[j].astype(F32)
        g_out[...] = g
        d_out[...], m_out[...], v_out[...] = _adamw(g, w_ref[...], m_ref[...], v_ref[...])

    row = pl.BlockSpec((br, n), lambda i: (i + blk0, 0))
    kept = [] if into is None else list(into)
    return pl.pallas_call(body, name=name, grid=(r // br,),
                          in_specs=[pl.BlockSpec((n_p, br, n), lambda i: (0, i, 0)), row, row, row]
                          + [pl.BlockSpec(memory_space=pl.ANY)] * len(kept),
                          out_specs=[row] * 4, out_shape=[jax.ShapeDtypeStruct((r_all, n), F32)] * 4,
                          input_output_aliases={4 + j: j for j in range(len(kept))},
                          compiler_params=_params(("parallel",)))(parts, w, m, v, *kept)


def _rope_tables(n_lat, n_ctx):
    f = HEAD_DIM // 4
    rows = n_lat // GRID_W
    inv = ROPE_BASE ** (-jnp.arange(f, dtype=F32) / f)
    ang_r = jnp.arange(rows).astype(F32)[:, None] * inv[None, :]
    ang_c = jnp.arange(GRID_W).astype(F32)[:, None] * inv[None, :]

    cr, sr, cc, sc = jnp.cos(ang_r), jnp.sin(ang_r), jnp.cos(ang_c), jnp.sin(ang_c)
    zr, zc = jnp.zeros_like(cr), jnp.zeros_like(cc)

    def table(by_row, by_col):
        both = by_row[:, None, :] + by_col[None, :, :]
        return both.reshape(n_lat, HEAD_DIM)

    cos = table(jnp.concatenate([cr, cr, zr, zr], axis=-1), jnp.concatenate([zc, zc, cc, cc], axis=-1))
    sa = table(jnp.concatenate([-sr, zr, zr, zr], axis=-1), jnp.concatenate([zc, zc, -sc, zc], axis=-1))
    sb = table(jnp.concatenate([zr, sr, zr, zr], axis=-1), jnp.concatenate([zc, zc, zc, sc], axis=-1))
    pad = jnp.zeros((n_ctx, HEAD_DIM), F32)
    return (jnp.concatenate([cos, pad + 1.0], axis=0), jnp.concatenate([sa, pad], axis=0),
            jnp.concatenate([sb, pad], axis=0))


def _pad_rows(a, rows):
    return jnp.pad(a, [(0, rows - a.shape[0])] + [(0, 0)] * (a.ndim - 1))


def _pad_cols(a, cols):
    return jnp.pad(a, [(0, 0), (0, cols - a.shape[1])])


def kernel(x, c, ctx, c_ctx, norm_w, w_mod, b_mod, w_in, conv_w, conv_norm_w, ret_norm_w, ret_decay_f, ret_decay_b, w_out, final_norm_w, loss_target, m_c_ctx, m_norm_w, m_w_mod, m_b_mod, m_w_in, m_conv_w, m_conv_norm_w, m_ret_norm_w, m_ret_decay_f, m_ret_decay_b, m_w_out, m_final_norm_w, v_c_ctx, v_norm_w, v_w_mod, v_b_mod, v_w_in, v_conv_w, v_conv_norm_w, v_ret_norm_w, v_ret_decay_f, v_ret_decay_b, v_w_out, v_final_norm_w):
    depth = norm_w.shape[0]
    n_lat, d = x.shape[1], x.shape[2]
    n_ctx = ctx.shape[1]
    s = d // 2
    n_heads = ret_decay_f.shape[1]
    nx, ncc = n_lat // CHUNK, n_ctx // CHUNK
    n_mod = w_mod.shape[2]
    n_cw = conv_w.shape[2]
    r_out = w_out.shape[1]
    assert s == n_heads * HEAD_DIM and w_in.shape[2] == s and N_DEV * r_out == d
    assert n_lat % ROW_TILE == 0 and n_ctx % ROW_TILE == 0 and 3 * depth * n_cw <= d and N_DEV * n_mod == 3 * d
    me = 4 * lax.axis_index("x") + 2 * lax.axis_index("y") + lax.axis_index("c")

    w_in_bf = [w_in[l].astype(BF16) for l in range(depth)]
    w_out_bf = [w_out[l].astype(BF16) for l in range(depth)]

    first = jnp.concatenate([c.reshape(1, d), _pad_cols(conv_w.reshape(1, -1), d), jnp.zeros((6, d), F32)], axis=0)
    (first_g,) = _all_gather([first], "gather_cond")
    first_g = first_g.reshape(N_DEV, 8, d)
    c_all = first_g[:, 0, :]
    conv_full = first_g[:, 1, :3 * depth * n_cw].reshape(N_DEV, depth, 3, n_cw)
    conv_full = conv_full.transpose(1, 2, 0, 3).reshape(depth, 3, N_DEV * n_cw)
    c9 = jnp.concatenate([c_all, c_ctx.reshape(1, d), jnp.zeros((7, d), F32)], axis=0)

    b_sh = lax.dynamic_slice(b_mod, (0, me * n_mod), (depth, n_mod))
    mod_sh = jnp.concatenate([_mod_rows(c9, w_mod[l], b_sh[l:l + 1], f"mod_rows_l{l}") for l in range(depth)], axis=0)
    (mod_g,) = _all_gather([mod_sh], "gather_mod")
    mod_g = mod_g.reshape(N_DEV, depth, 16, n_mod)
    mods = []
    for l in range(depth):
        mine = lax.dynamic_index_in_dim(mod_g[:, l], me, axis=1, keepdims=False).reshape(3, d)
        cx = mod_g[:, l, 8, :].reshape(3, d)
        mods.append(jnp.concatenate([mine, cx, jnp.zeros((2, d), F32)], axis=0))

    halves = [w_in_bf[0][:, :s // 2], w_in_bf[0][:, s // 2:]]
    near, order = [], [mod_g]
    for j, part in enumerate(halves):
        near.append(_push_start([part], [_landing(part, me)], "near", f"w_in0_start_{j}", after=order))
        order = near[-1][4:]
    pending = []
    for k in range(depth):
        srcs = [w_out_bf[k]] + ([w_in_bf[k]] if k > 0 else [])
        started = _push_start(srcs, [_landing(a, me) for a in srcs], "gather", f"weights_start_l{k}", after=order)
        pending.append(started[:4])
        order = started[4:]
    w_in_g = [None] * depth
    w_out_g = [None] * depth

    cos, sa, sb_tab = _rope_tables(n_lat, n_ctx)
    t_all = n_lat + n_ctx

    saved = []
    xt = hx_next = None
    for l in range(depth):
        tiles = _tiles(l, t_all, d)
        names = ["dc", "dlf", "dlb", "qf", "kf", "qb", "kb", "cdf", "cdb", "lg"]
        dec = jnp.stack([ret_decay_f[l], ret_decay_b[l]], axis=0)
        tabs = dict(zip(names, _decay_tables(dec, n_heads, f"decay_tables_l{l}")))
        if l == 0:
            hx = _prenorm_first(x[0], ctx[0], norm_w[0:1], mods[0], "prenorm_l0", after=order)
            gathered, out, after = [], None, hx
            for j in range(2):
                (landed,) = _push_wait(*near[j][:4], "near", after, f"w_in0_wait_{j}")
                relay = _push_start([], [landed], "relay", f"w_in0_relay_start_{j}")
                (landed,) = _push_wait(*relay[:4], "relay", relay[4], f"w_in0_relay_wait_{j}")
                gathered.append(landed)
                out = _in_proj(hx, landed, cos, sa, sb_tab, s, j, tiles["in_tm_half"], f"in_proj_l0_{j}", into=out)
                after = out[0]
            u, qkv = out
            w_in_g[0] = gathered
        else:
            landed = _push_wait(*pending[l], "gather", xt, f"weights_wait_l{l}")
            w_out_g[l], w_in_g[l] = landed[0].reshape(d, d), [landed[1]]
            hx = hx_next
            u, qkv = _in_proj(hx, w_in_g[l][0], cos, sa, sb_tab, s, 0, tiles["in_tm"], f"in_proj_l{l}")
        sf, sb = _state_sweep(qkv, tabs, n_heads, nx, ncc, f"state_sweep_l{l}")
        ycat, o = _mix_fwd(u, qkv, sf, sb, tabs, conv_full[l], conv_norm_w[l:l + 1], ret_norm_w[l:l + 1],
                           n_heads, nx, ncc, f"mix_fwd_l{l}")
        if l == 0:
            (landed,) = _push_wait(*pending[0], "gather", ycat, "weights_wait_l0")
            w_out_g[0] = landed.reshape(d, d)
        m_res = x_new = None
        if l < depth - 1:
            res = (x[0], ctx[0]) if l == 0 else (xt,)
            m_res, x_new, hx_next = _out_proj_prenorm(ycat, w_out_g[l], res, mods[l], norm_w[l + 1:l + 2], mods[l + 1],
                                                      n_lat, f"out_proj_l{l}")
        else:
            dxt, dm, loss_blk, dfnw, gate_acc = _out_proj_loss(ycat, w_out_g[l], xt, mods[l], loss_target[0],
                                                               final_norm_w.reshape(1, d), n_lat, f"out_proj_loss_l{l}")
        saved.append(dict(tabs=tabs, xt=xt, hx=hx, u=u, qkv=qkv, sf=sf, sb=sb, ycat=ycat, o=o, m=m_res, tiles=tiles))
        xt = x_new

    loss = lax.psum(loss_blk[0, 0], MESH_AXES)

    dmod_x, dmod_c, dnw, dcnw, dgnw, dconv, ddec, dwin, dwout = [], [], [], [], [], [], [], [], []
    started_token = ()
    for l in reversed(range(depth)):
        sv = saved[l]
        tiles = sv["tiles"]
        dycat = _matmul_nt(dm, w_out_g[l], tiles["ob_tn"], f"out_proj_bwd_l{l}", after=started_token)
        dwout.append(_weight_grad(sv["ycat"], dm.reshape(1, *dm.shape), tiles["wo_bm"], tiles["wo_bt"],
                                  f"w_out_grad_l{l}")[0])
        g, dz, db, drz, do, norm_acc = _mix_bwd_a(dycat, sv["u"], sv["o"], conv_full[l], conv_norm_w[l:l + 1],
                                                   ret_norm_w[l:l + 1], n_heads, nx, ncc, f"mix_bwd_a_l{l}")
        gf, gb = _grad_state_sweep(sv["qkv"], do, sv["tabs"], n_heads, nx, ncc, f"grad_state_sweep_l{l}")
        du, conv_acc, dlg = _mix_bwd_b(sv["u"], g, dz, db, drz, sv["qkv"], do, sv["sf"], sv["sb"],
                                       gf, gb, sv["tabs"], cos, sa, sb_tab, conv_full[l], n_heads, nx, ncc,
                                       f"mix_bwd_b_l{l}")
        gate_acc_l = gate_acc
        if l > 0:
            dhx = _in_proj_bwd(du, w_in_g[l], tiles["bwd_tm"], tiles["bwd_gs"], f"in_proj_bwd_l{l}")
            below = (saved[l - 1]["m"], mods[l - 1])
            dwin_l, dxt, pre_acc, dm, gate_acc = _weight_grad_beside_prenorm_bwd(
                sv["hx"], du, dhx, sv["xt"], dxt, norm_w[l:l + 1], mods[l], below, n_lat, f"w_in_grad_l{l}")
        else:
            dwin_l = _weight_grad(sv["hx"], du, tiles["wg_bm"], tiles["wg_bt"], f"w_in_grad_l{l}")
        srcs = [dwin_l, dwout[-1].reshape(N_DEV, r_out, d)]
        lands = [_landing(lax.dynamic_index_in_dim(a, me, axis=0, keepdims=False), me) for a in srcs]
        started = _push_start(srcs, lands, "scatter", f"grads_start_l{l}")
        dwin.append(started[:4])
        started_token = started[4:]
        if l == 0:
            dhx = _in_proj_bwd(du, w_in_g[l], tiles["bwd_tm"], tiles["bwd_gs"], f"in_proj_bwd_l{l}", after=started[4:])
            dxt, pre_acc = _prenorm_bwd_first(dhx, x[0], ctx[0], dxt, norm_w[l:l + 1], mods[l], f"prenorm_bwd_l{l}")
        dmod_x.append(jnp.concatenate([pre_acc[0], pre_acc[1], gate_acc_l[2]]))
        dmod_c.append(jnp.concatenate([pre_acc[3], pre_acc[4], gate_acc_l[5]]))
        dnw.append(pre_acc[6])
        dcnw.append(norm_acc[0])
        dgnw.append(norm_acc[1])
        dconv.append(conv_acc[0:3])
        ddec.append(dlg[0:2, :n_heads])
    for lst in (dmod_x, dmod_c, dnw, dcnw, dgnw, dconv, ddec, dwin, dwout):
        lst.reverse()
    grad_x = dxt.reshape(1, n_lat, d)

    rows = []
    for l in range(depth):
        rows += [dmod_x[l], dmod_c[l]]
    (dmod_g,) = _all_gather([_pad_rows(jnp.stack(rows, axis=0), 8)], "gather_dmod")
    dmod_g = dmod_g.reshape(N_DEV, 8, 3 * d)
    mine_cols = lax.dynamic_slice(dmod_g, (0, 0, me * n_mod), (N_DEV, 8, n_mod))
    g_wmod, dcc = [], jnp.zeros((d,), F32)
    for l in range(depth):
        gw, dc_part = _mod_grads(mine_cols[:, 2 * l], mine_cols[:, 2 * l + 1], c9, w_mod[l], f"mod_grads_l{l}")
        g_wmod.append(gw)
        dcc = dcc + dc_part[0]

    n_small = 16
    small = jnp.concatenate([
        jnp.stack(dnw, axis=0),
        jnp.concatenate(dcnw).reshape(1, -1),
        jnp.concatenate(dgnw).reshape(1, -1),
        dfnw[0:1],
        dcc.reshape(1, d),
        jnp.stack(dconv, axis=0).reshape(-1, d),
        _pad_cols(jnp.stack(ddec, axis=0).reshape(1, -1), d),
    ], axis=0)
    assert depth * s == d and small.shape[0] <= n_small
    n_rows = small.shape[0]
    (small_g,) = _all_gather([_pad_rows(small, n_small)], "gather_small")
    small_g = small_g.reshape(N_DEV, n_small, d)

    def pack_small(nw_, cn_, gn_, fn_, cc_, df_, db_):
        return _pad_rows(jnp.concatenate([
            nw_, cn_.reshape(1, -1), gn_.reshape(1, -1), fn_.reshape(1, d), cc_.reshape(1, d),
            jnp.zeros((n_rows - depth - 5, d), F32),
            _pad_cols(jnp.stack([df_, db_], axis=1).reshape(1, -1), d)], axis=0), n_small)

    w_s = pack_small(norm_w, conv_norm_w, ret_norm_w, final_norm_w, c_ctx, ret_decay_f, ret_decay_b)
    m_s = pack_small(m_norm_w, m_conv_norm_w, m_ret_norm_w, m_final_norm_w, m_c_ctx, m_ret_decay_f, m_ret_decay_b)
    v_s = pack_small(v_norm_w, v_conv_norm_w, v_ret_norm_w, v_final_norm_w, v_c_ctx, v_ret_decay_f, v_ret_decay_b)
    small_out = _sum_adamw(small_g, w_s, m_s, v_s, "adamw_small")

    def unpack_small(a):
        nw_ = a[0:depth]
        cn_ = a[depth].reshape(depth, s)
        gn_ = a[depth + 1].reshape(depth, s)
        fn_ = a[depth + 2]
        cc_ = a[depth + 3]
        dd = a[n_rows - 1, :depth * 2 * n_heads].reshape(depth, 2, n_heads)
        return dict(c_ctx=cc_, norm_w=nw_, conv_norm_w=cn_, ret_norm_w=gn_, ret_decay_f=dd[:, 0], ret_decay_b=dd[:, 1],
                    final_norm_w=fn_)

    res = {}
    for kind, arr in zip(("grad", "delta", "m", "v"), small_out):
        for k_, val in unpack_small(arr).items():
            res[(kind, k_)] = val

    bm_parts = jnp.concatenate([dmod_g[:, 0:2 * depth:2].reshape(N_DEV, depth, 3 * d),
                                dmod_g[:, 1:2 * depth:2].reshape(N_DEV, depth, 3 * d)], axis=0)
    bm_parts = jnp.concatenate([bm_parts, jnp.zeros((2 * N_DEV, 8 - depth, 3 * d), F32)], axis=1)
    pad8 = lambda a: _pad_rows(a, 8)
    bm_out = _sum_adamw(bm_parts, pad8(b_mod), pad8(m_b_mod), pad8(v_b_mod), "adamw_b_mod")
    for kind, arr in zip(("grad", "delta", "m", "v"), bm_out):
        res[(kind, "b_mod")] = arr[:depth]

    conv_rows = small_g[:, depth + 4:depth + 4 + 3 * depth * s // d].reshape(N_DEV, depth * 3, s)
    conv_mine = lax.dynamic_slice(conv_rows, (0, 0, me * n_cw), (N_DEV, depth * 3, n_cw))
    conv_mine = jnp.concatenate([conv_mine, jnp.zeros((N_DEV, 8 - depth * 3, n_cw), F32)], axis=1)
    cw2 = lambda a: _pad_rows(a.reshape(depth * 3, n_cw), 8)
    cw_out = _sum_adamw(conv_mine, cw2(conv_w), cw2(m_conv_w), cw2(v_conv_w), "adamw_conv_w")
    for kind, arr in zip(("grad", "delta", "m", "v"), cw_out):
        res[(kind, "conv_w")] = arr[:depth * 3].reshape(depth, 3, n_cw)

    wm_out = _sum_adamw(jnp.stack(g_wmod, axis=0).reshape(1, depth * d, n_mod), w_mod.reshape(depth * d, n_mod),
                        m_w_mod.reshape(depth * d, n_mod), v_w_mod.reshape(depth * d, n_mod), "adamw_w_mod")
    for kind, arr in zip(("grad", "delta", "m", "v"), wm_out):
        res[(kind, "w_mod")] = arr.reshape(depth, d, n_mod)

    wi_out = wo_out = None
    after = wm_out[0]
    for l in reversed(range(depth)):
        win_parts, wout_parts = _push_wait(*dwin[l], "scatter", after, f"grads_wait_l{l}")
        wi_out = _sum_adamw(win_parts, w_in.reshape(depth * d, s), m_w_in.reshape(depth * d, s),
                            v_w_in.reshape(depth * d, s), f"adamw_w_in_l{l}", row0=l * d, into=wi_out)
        wo_out = _sum_adamw(wout_parts, w_out.reshape(depth * r_out, d), m_w_out.reshape(depth * r_out, d),
                            v_w_out.reshape(depth * r_out, d), f"adamw_w_out_l{l}", row0=l * r_out, into=wo_out)
        after = wo_out[0]
    for kind, arr in zip(("grad", "delta", "m", "v"), wi_out):
        res[(kind, "w_in")] = arr.reshape(depth, d, s)
    for kind, arr in zip(("grad", "delta", "m", "v"), wo_out):
        res[(kind, "w_out")] = arr.reshape(depth, r_out, d)

    order = ["c_ctx", "norm_w", "w_mod", "b_mod", "w_in", "conv_w", "conv_norm_w", "ret_norm_w", "ret_decay_f",
             "ret_decay_b", "w_out", "final_norm_w"]
    outs = [loss, grad_x]
    for kind in ("grad", "delta", "m", "v"):
        outs += [res[(kind, k_)] for k_ in order]
    return tuple(outs)
```

```python
import jax
import jax.numpy as jnp
from jax import lax
from jax.experimental import pallas as pl
from jax.experimental.pallas import tpu as pltpu

F32 = jnp.float32
BF16 = jnp.bfloat16

EPS = 1e-6
CHUNK = 128
HEAD_DIM = 128
GRID_W = 64
ROPE_BASE = 10000.0
N_DEV = 8
ADAM_LR, ADAM_B1, ADAM_B2, ADAM_EPS, ADAM_WD, ADAM_STEP = 0.001, 0.9, 0.999, 1e-08, 0.01, 10

ROW_TILE = 256
V7X_VMEM_LIMIT = 56 * 1024 * 1024

NN = ((1,), (0,))
NT = ((1,), (1,))
TN = ((0,), (0,))


def _dot(a, b, dims):
    return lax.dot_general(a, b, (dims, ((), ())), preferred_element_type=F32)


def _params(sem=None):
    if sem is None:
        return pltpu.CompilerParams(vmem_limit_bytes=V7X_VMEM_LIMIT)
    return pltpu.CompilerParams(dimension_semantics=sem, vmem_limit_bytes=V7X_VMEM_LIMIT)


def _silu(z):
    return z * jax.nn.sigmoid(z)


def _dsilu(z):
    s = jax.nn.sigmoid(z)
    return s * (1.0 + z * (1.0 - s))


def _silu_and_slope(z):
    s = jax.nn.sigmoid(z)
    return z * s, s * (1.0 + z * (1.0 - s))


def _sum_all(a):
    return jnp.sum(jnp.sum(a, axis=1, keepdims=True), axis=0, keepdims=True)


def _mm_rows(t):
    return 768 if t % 768 == 0 else ROW_TILE


def _rows_or(t, rows):
    return rows if t % rows == 0 else _mm_rows(t)


def _tiles(layer, t, d):
    return dict(in_tm=_rows_or(t, 1408), in_tm_half=_rows_or(t, 2112), bwd_gs=2, wg_bm=d, wg_bt=_mm_rows(t),
                wo_bm=d, wo_bt=_mm_rows(t), ob_tn=d, bwd_tm=_rows_or(t, 1056))


def _full(shape):
    n = len(shape)
    return pl.BlockSpec(shape, lambda *_: (0,) * n)


def _peers(x, y, c):
    return [(x, y, 1 - c), (1 - x, y, c), (x, 1 - y, c), (1 - x, 1 - y, c),
            (1 - x, y, 1 - c), (x, 1 - y, 1 - c), (1 - x, 1 - y, 1 - c)]


def _lin(p):
    return 4 * p[0] + 2 * p[1] + p[2]


def _all_gather(arrays, name):
    n_arr = len(arrays)
    space = pltpu.VMEM

    def body(*refs):
        ins, outs = refs[:n_arr], refs[n_arr:2 * n_arr]
        send_sems, recv_sems, local_sems = refs[2 * n_arr:]
        x, y, c = lax.axis_index("x"), lax.axis_index("y"), lax.axis_index("c")
        me, sibling = (x, y, c), (x, y, 1 - c)
        chips = [(1 - x, y), (x, 1 - y), (1 - x, 1 - y)]
        every = []
        locals_ = []
        for a in range(n_arr):
            m_per = ins[a].shape[0]
            out_ref = outs[a]

            def rows(p, out_ref=out_ref, m_per=m_per):
                return out_ref.at[pl.ds(_lin(p) * m_per, m_per), :]

            def copy(k, block, to, src=None, a=a, rows=rows):
                return pltpu.make_async_remote_copy(
                    src_ref=rows(block) if src is None else src, dst_ref=rows(block),
                    send_sem=send_sems.at[a, k], recv_sem=recv_sems.at[a, k],
                    device_id=to, device_id_type=pl.DeviceIdType.MESH)

            mine = pltpu.make_async_copy(ins[a], rows(me), local_sems.at[a])
            mine.start()
            locals_.append(mine)
            first = [copy(0, me, sibling, src=ins[a])]
            first += [copy(1 + j, me, (*chip, c), src=ins[a]) for j, chip in enumerate(chips)]
            for cp in first:
                cp.start()
            every.append((copy, first))
        sends = []
        for a in range(n_arr):
            copy, first = every[a]
            passed = [copy(4 + j, (*chip, c), sibling) for j, chip in enumerate(chips)]
            for j, chip in enumerate(chips):
                copy(1 + j, (*chip, c), me).wait_recv()
                passed[j].start()
            sends += first + passed
        for a in range(n_arr):
            copy, _ = every[a]
            copy(0, sibling, me).wait_recv()
            for j, chip in enumerate(chips):
                copy(4 + j, (*chip, 1 - c), me).wait_recv()
        for cp in sends:
            cp.wait_send()
        for mine in locals_:
            mine.wait()

    outs = pl.pallas_call(
        body, name=name,
        out_shape=[jax.ShapeDtypeStruct((N_DEV * a.shape[0], a.shape[1]), a.dtype) for a in arrays],
        in_specs=[pl.BlockSpec(memory_space=space)] * n_arr,
        out_specs=[pl.BlockSpec(memory_space=space)] * n_arr,
        scratch_shapes=[pltpu.SemaphoreType.DMA((n_arr, 7)), pltpu.SemaphoreType.DMA((n_arr, 7)),
                        pltpu.SemaphoreType.DMA((n_arr,))],
        compiler_params=_params(),
    )(*arrays)
    return list(outs)


_HBM = pl.BlockSpec(memory_space=pltpu.HBM)
_SEM = pl.BlockSpec(memory_space=pltpu.SEMAPHORE)
_DATAFLOW = pltpu.SideEffectType.DATAFLOW_SIDE_EFFECTING


PUSH_COPIES = {"scatter": 7, "gather": 7, "near": 4, "relay": 3}


def _push_copies(src_refs, land_refs, send_sems, recv_sems, mode):
    x, y, c = lax.axis_index("x"), lax.axis_index("y"), lax.axis_index("c")
    me, sibling = (x, y, c), (x, y, 1 - c)
    n_k = PUSH_COPIES[mode]
    out, back = [], []
    if mode == "relay":
        for k, chip in enumerate([(1 - x, y), (x, 1 - y), (1 - x, 1 - y)]):
            for a, land in enumerate(land_refs):
                sems = dict(send_sem=send_sems.at[n_k * a + k], recv_sem=recv_sems.at[n_k * a + k],
                            device_id=sibling, device_id_type=pl.DeviceIdType.MESH)
                mine = land.at[_lin((*chip, c))]
                out.append(pltpu.make_async_remote_copy(src_ref=mine, dst_ref=mine, **sems))
                back.append(pltpu.make_async_remote_copy(src_ref=mine, dst_ref=land.at[_lin((*chip, 1 - c))], **sems))
        return out, back
    for k, peer in enumerate(_peers(x, y, c)[:n_k]):
        for a, (src, land) in enumerate(zip(src_refs, land_refs)):
            sems = dict(send_sem=send_sems.at[n_k * a + k], recv_sem=recv_sems.at[n_k * a + k],
                        device_id=peer, device_id_type=pl.DeviceIdType.MESH)
            mine = src.at[_lin(peer)] if mode == "scatter" else src
            out.append(pltpu.make_async_remote_copy(src_ref=mine, dst_ref=land.at[_lin(me)], **sems))
            back.append(pltpu.make_async_remote_copy(src_ref=mine, dst_ref=land.at[_lin(peer)], **sems))
    return out, back


def _push_start(srcs, lands, mode, name, after=()):
    n_src, n = len(srcs), len(lands)
    n_buf = n_src + n
    n_in = n_buf + len(after)
    n_sem = PUSH_COPIES[mode] * n

    def body(*refs):
        send_sems, recv_sems = refs[n_in], refs[n_in + 1]
        out, _ = _push_copies(refs[:n_src], refs[n_src:n_buf], send_sems, recv_sems, mode)
        for cp in out:
            cp.start()
        token = refs[-1]
        token[...] = jnp.zeros_like(token)

    both = list(srcs) + list(lands)
    res = pl.pallas_call(
        body, name=name,
        out_shape=[pltpu.SemaphoreType.DMA((n_sem,)), pltpu.SemaphoreType.DMA((n_sem,))]
        + [pltpu.HBM(a.shape, a.dtype) for a in both] + [jax.ShapeDtypeStruct((8, 128), F32)],
        in_specs=[_HBM] * n_buf + [pl.BlockSpec(memory_space=pl.ANY)] * len(after),
        out_specs=[_SEM, _SEM] + [_HBM] * n_buf + [pl.BlockSpec(memory_space=pltpu.VMEM)],
        input_output_aliases={i: 2 + i for i in range(n_buf)},
        compiler_params=pltpu.CompilerParams(has_side_effects=_DATAFLOW),
    )(*[pltpu.with_memory_space_constraint(a, pltpu.HBM) for a in both], *after)
    return res[0], res[1], list(res[2:2 + n_src]), list(res[2 + n_src:2 + n_buf]), res[-1]


def _push_wait(send_sems, recv_sems, srcs, lands, mode, after, name):
    n_src, n = len(srcs), len(lands)
    n_buf = n_src + n

    def body(*refs):
        out, back = _push_copies(refs[:n_src], refs[n_src:n_buf], refs[n_buf], refs[n_buf + 1], mode)
        for cp in out:
            cp.wait_send()
        for cp in back:
            cp.wait_recv()

    both = list(srcs) + list(lands)
    res = pl.pallas_call(
        body, name=name,
        out_shape=[pltpu.HBM(a.shape, a.dtype) for a in both],
        in_specs=[_HBM] * n_buf + [_SEM, _SEM, pl.BlockSpec(memory_space=pl.ANY)],
        out_specs=[_HBM] * n_buf,
        input_output_aliases={i: i for i in range(n_buf)},
        compiler_params=pltpu.CompilerParams(has_side_effects=_DATAFLOW),
    )(*both, send_sems, recv_sems, after)
    return list(res[n_src:])


def _landing(own, me):
    zone = lax.empty((N_DEV,) + own.shape, own.dtype)
    return lax.dynamic_update_slice(zone, own[None], (me,) + (0,) * own.ndim)


def _mod_rows(c9, w_mod, b_sh, name):
    n = w_mod.shape[1]

    def body(c_ref, w_ref, b_ref, o_ref):
        s9 = _silu(c_ref[...]).astype(BF16)
        o_ref[...] = _dot(s9, w_ref[...].astype(BF16), NN) + b_ref[...]

    return pl.pallas_call(body, name=name, out_shape=jax.ShapeDtypeStruct((16, n), F32),
                          compiler_params=_params())(c9, w_mod, b_sh)


def _mod_grads(dm_rows, dc_rows, c9, w_mod, name):
    d, n = w_mod.shape

    def body(dm_ref, dc_ref, c_ref, w_ref, gw_ref, dc_out):
        dc = dc_ref[...]
        tot = dc[0:1]
        for j in range(1, N_DEV):
            tot = tot + dc[j:j + 1]
        row = lax.broadcasted_iota(jnp.int32, (8, n), 0)
        lower = jnp.where(row == 0, tot, 0.0)
        dmod9 = jnp.concatenate([dm_ref[...], lower], axis=0).astype(BF16)
        c9v = c_ref[...]
        s9 = _silu(c9v).astype(BF16)
        gw_ref[...] = _dot(s9, dmod9, TN)
        ds = _dot(lower.astype(BF16), w_ref[...].astype(BF16), NT)
        dc_out[...] = ds * _dsilu(c9v[8:16])

    return pl.pallas_call(body, name=name,
                          out_shape=[jax.ShapeDtypeStruct((d, n), F32), jax.ShapeDtypeStruct((8, d), F32)],
                          compiler_params=_params())(dm_rows, dc_rows, c9, w_mod)


def _decay_tables(dec, n_heads, name):
    c = CHUNK

    def body(dec_ref, dc_ref, dlf_ref, dlb_ref, qf_ref, kf_ref, qb_ref, kb_ref, cdf_ref, cdb_ref, lg_ref):
        h = pl.program_id(0)
        d = dec_ref[...]
        lane = lax.broadcasted_iota(jnp.int32, d.shape, 1)
        lg = -jnp.exp(jnp.sum(jnp.where(lane == h, d, 0.0), axis=1, keepdims=True))
        lgf, lgb = lg[0:1], lg[1:2]
        i = lax.broadcasted_iota(jnp.int32, (c, c), 0).astype(F32)
        j = lax.broadcasted_iota(jnp.int32, (c, c), 1).astype(F32)
        diff = i - j
        d_f = jnp.where(diff >= 0, jnp.exp(lgf * jnp.maximum(diff, 0.0)), 0.0)
        d_b = jnp.where(diff <= 0, jnp.exp(lgb * jnp.maximum(-diff, 0.0)), 0.0)
        dc_ref[...] = d_f + d_b
        dlf_ref[...] = diff * d_f
        dlb_ref[...] = -diff * d_b
        pos = lax.broadcasted_iota(jnp.int32, (c, HEAD_DIM), 0).astype(F32)
        qf_ref[...] = jnp.exp(lgf * (pos + 1.0))
        kf_ref[...] = jnp.exp(lgf * (c - 1.0 - pos))
        qb_ref[...] = jnp.exp(lgb * (c - pos))
        kb_ref[...] = jnp.exp(lgb * pos)
        ones = jnp.ones((8, HEAD_DIM), F32)
        cdf_ref[...] = jnp.exp(lgf * float(c)) * ones
        cdb_ref[...] = jnp.exp(lgb * float(c)) * ones

        @pl.when(h == 0)
        def _():
            lg_ref[...] = jnp.zeros_like(lg_ref)

        row8 = lax.broadcasted_iota(jnp.int32, (8, HEAD_DIM), 0)
        lane8 = lax.broadcasted_iota(jnp.int32, (8, HEAD_DIM), 1)
        lg_ref[...] += (jnp.where((row8 == 0) & (lane8 == h), lgf, 0.0)
                        + jnp.where((row8 == 1) & (lane8 == h), lgb, 0.0))

    def per_head(*tail):
        return pl.BlockSpec((None,) + tail, lambda h: (h,) + (0,) * len(tail))

    shapes = [(c, c)] * 3 + [(c, HEAD_DIM)] * 4 + [(8, HEAD_DIM)] * 2
    return pl.pallas_call(
        body, name=name, grid=(n_heads,),
        in_specs=[_full(dec.shape)],
        out_specs=[per_head(*s) for s in shapes] + [_full((8, HEAD_DIM))],
        out_shape=[jax.ShapeDtypeStruct((n_heads,) + s, F32) for s in shapes]
        + [jax.ShapeDtypeStruct((8, HEAD_DIM), F32)],
        compiler_params=_params(("arbitrary",)),
    )(dec)


def _modulate(x, nw, shift, scale):
    r = lax.rsqrt(jnp.mean(x * x, axis=-1, keepdims=True) + EPS)
    return ((x * r) * nw * (1.0 + scale) + shift).astype(BF16)


def _split_rows(nxb, nb):
    def specs(d, step=lambda i: i):
        lat = pl.BlockSpec((ROW_TILE, d), lambda i: (jnp.minimum(step(i), nxb - 1), 0))
        ctx = pl.BlockSpec((ROW_TILE, d), lambda i: (jnp.clip(step(i) - nxb, 0, nb - nxb - 1), 0))
        return lat, ctx
    return specs


def _prenorm_first(x, ctx, nw, mod, name, after=()):
    n_lat, d = x.shape
    t = n_lat + ctx.shape[0]
    nxb = n_lat // ROW_TILE

    def body(x_ref, c_ref, nw_ref, mod_ref, *rest):
        o_ref = rest[-1]
        m = mod_ref[...]
        nw_v = nw_ref[...]

        @pl.when(pl.program_id(0) < nxb)
        def _():
            o_ref[...] = _modulate(x_ref[...], nw_v, m[0:1], m[1:2])

        @pl.when(pl.program_id(0) >= nxb)
        def _():
            o_ref[...] = _modulate(c_ref[...], nw_v, m[3:4], m[4:5])

    lat, cx = _split_rows(nxb, t // ROW_TILE)(d)
    return pl.pallas_call(
        body, name=name, grid=(t // ROW_TILE,),
        in_specs=[lat, cx, _full((1, d)), _full((8, d))] + [pl.BlockSpec(memory_space=pl.ANY)] * len(after),
        out_specs=pl.BlockSpec((ROW_TILE, d), lambda i: (i, 0)), out_shape=jax.ShapeDtypeStruct((t, d), BF16),
        compiler_params=_params(("parallel",)))(x, ctx, nw, mod, *after)


def _rope_fwd(v, cos, sa, sb):
    return v * cos + pltpu.roll(v, 96, 1) * sa + pltpu.roll(v, 32, 1) * sb


def _rope_bwd(g, cos, sa, sb):
    return g * cos + pltpu.roll(g * sa, 32, 1) + pltpu.roll(g * sb, 96, 1)


N_PLAIN = 5
U_DTYPE = BF16


def _in_proj(hx, wg, cos, sa, sb, s, part, tm, name, after=(), into=None):
    t, d = hx.shape
    n_seg, _, n = wg.shape
    nb = t // tm
    k_scale = HEAD_DIM ** -0.5
    kept = [] if into is None else list(into)

    def body(a_ref, w_ref, cos_ref, sa_ref, sb_ref, *rest):
        u_ref, qkv_ref = rest[-2:]
        g = pl.program_id(1)
        acc = _dot(a_ref[...], w_ref[...], NN)

        @pl.when(g < N_PLAIN)
        def _():
            u_ref[...] = acc.astype(U_DTYPE)

        @pl.when(g == N_PLAIN + 2)
        def _():
            qkv_ref[...] = acc.astype(BF16)

        for which, scale in ((N_PLAIN, 1.0), (N_PLAIN + 1, k_scale)):
            @pl.when(g == which)
            def _(scale=scale):
                co, a, b = cos_ref[...], sa_ref[...], sb_ref[...]
                for h in range(n // HEAD_DIM):
                    sl = slice(h * HEAD_DIM, (h + 1) * HEAD_DIM)
                    qkv_ref[:, sl] = (_rope_fwd(acc[:, sl], co, a, b) * scale).astype(BF16)

    def w_seg(g):
        return jnp.where(g < N_PLAIN - 1, g, jnp.where(g == N_PLAIN - 1, n_seg - 1, g - 1))

    def qkv_at(i, g):
        held = (jnp.where(i == 0, 0, 2), jnp.maximum(i - 1, 0))
        return (jnp.where(g < N_PLAIN, held[0], g - N_PLAIN), jnp.where(g < N_PLAIN, held[1], i), part)

    tab = pl.BlockSpec((tm, HEAD_DIM), lambda i, g: (i, 0))
    hbm = pl.BlockSpec(memory_space=pl.ANY)
    return pl.pallas_call(
        body, name=name, grid=(nb, n_seg),
        in_specs=[pl.BlockSpec((tm, d), lambda i, g: (i, 0)), pl.BlockSpec((None, d, n), lambda i, g: (w_seg(g), 0, 0)),
                  tab, tab, tab] + [hbm] * (len(after) + len(kept)),
        out_specs=[pl.BlockSpec((None, tm, n), lambda i, g: (jnp.minimum(g, N_PLAIN - 1), i, part)),
                   pl.BlockSpec((None, tm, n), qkv_at)],
        out_shape=[jax.ShapeDtypeStruct((N_PLAIN, t, s), U_DTYPE), jax.ShapeDtypeStruct((3, t, s), BF16)],
        input_output_aliases={5 + len(after) + j: j for j in range(len(kept))},
        compiler_params=_params(("arbitrary", "arbitrary")))(hx, wg, cos, sa, sb, *after, *kept)


def _pair_sweep(xs, ys, tab_f, tab_b, cdf, cdb, n_heads, nx, ncc, reverse, name):
    t, s = xs[0].shape[-2:]
    nc = nx + ncc
    c = CHUNK
    n_pair = nc // 2
    assert nx % 2 == 0 and ncc % 2 == 0

    def f_pair(i):
        step = n_pair - 1 - i if reverse else i
        return jnp.where(step < ncc // 2, nx // 2 + step, step - ncc // 2)

    def b_pair(i):
        return i if reverse else n_pair - 1 - i

    f_subs = (1, 0) if reverse else (0, 1)
    b_subs = (0, 1) if reverse else (1, 0)

    def body(xf_ref, yf_ref, xb_ref, yb_ref, tf, tb, cdf_ref, cdb_ref, sf_out, sb_out, sf, sb):
        @pl.when(pl.program_id(0) == 0)
        def _():
            sf[...] = jnp.zeros_like(sf)
            sb[...] = jnp.zeros_like(sb)

        for step in range(2):
            for x_ref, y_ref, tab, cd, out, st, sub in ((xf_ref, yf_ref, tf, cdf_ref, sf_out, sf, f_subs[step]),
                                                        (xb_ref, yb_ref, tb, cdb_ref, sb_out, sb, b_subs[step])):
                rows = pl.ds(sub * c, c)
                for h in range(n_heads):
                    sl = pl.ds(h * HEAD_DIM, HEAD_DIM)
                    out[sub, h] = st[h].astype(BF16)
                    xd = (x_ref[rows, sl].astype(F32) * tab[h]).astype(BF16)
                    st[h] = cd[h][0:1, :] * st[h] + _dot(xd, y_ref[rows, sl], TN)

    def spec(arr, pair):
        lead = arr[1]
        if lead is None:
            return pl.BlockSpec((2 * c, s), lambda i: (pair(i), 0))
        return pl.BlockSpec((None, 2 * c, s), lambda i: (lead, pair(i), 0))

    st_blk = (2, n_heads, HEAD_DIM, HEAD_DIM)
    return pl.pallas_call(
        body, name=name, grid=(n_pair,),
        in_specs=[spec(xs, f_pair), spec(ys, f_pair), spec(xs, b_pair), spec(ys, b_pair),
                  _full((n_heads, c, HEAD_DIM)), _full((n_heads, c, HEAD_DIM)),
                  _full((n_heads, 8, HEAD_DIM)), _full((n_heads, 8, HEAD_DIM))],
        out_specs=[pl.BlockSpec(st_blk, lambda i: (f_pair(i), 0, 0, 0)), pl.BlockSpec(st_blk, lambda i: (b_pair(i), 0, 0, 0))],
        out_shape=[jax.ShapeDtypeStruct((nc, n_heads, HEAD_DIM, HEAD_DIM), BF16)] * 2,
        scratch_shapes=[pltpu.VMEM((n_heads, HEAD_DIM, HEAD_DIM), F32)] * 2,
        compiler_params=_params(("arbitrary",)),
    )(xs[0], ys[0], xs[0], ys[0], tab_f, tab_b, cdf, cdb)


def _state_sweep(qkv, tabs, n_heads, nx, ncc, name):
    return _pair_sweep((qkv, 1), (qkv, 2), tabs["kf"], tabs["kb"], tabs["cdf"], tabs["cdb"], n_heads, nx, ncc, False, name)


MIX_CHUNKS = 2
MIX_ROWS = MIX_CHUNKS * CHUNK


HALO = 16


def _halo_specs(s, t):
    per = MIX_ROWS // HALO
    n_halo = t // HALO

    def prev(g):
        return pl.BlockSpec((None, HALO, s), lambda i: (g, jnp.maximum(i * per - 1, 0), 0))

    def nxt(g):
        return pl.BlockSpec((None, HALO, s), lambda i: (g, jnp.minimum((i + 1) * per, n_halo - 1), 0))

    return prev, nxt


def _conv_input(h_ref, c_ref, hp_ref, hn_ref, cp_ref, cn_ref):
    a = c_ref[...].astype(F32) * h_ref[...].astype(F32)
    before = cp_ref[HALO - 1:HALO].astype(F32) * hp_ref[HALO - 1:HALO].astype(F32)
    after = cn_ref[0:1].astype(F32) * hn_ref[0:1].astype(F32)
    return a, before, after


def _shifted(a, before, after, has_prev, has_next):
    rows = a.shape[0]
    rowi = lax.broadcasted_iota(jnp.int32, a.shape, 0)
    am = jnp.where(rowi == 0, jnp.where(has_prev, before, 0.0), pltpu.roll(a, 1, 0))
    ap = jnp.where(rowi == rows - 1, jnp.where(has_next, after, 0.0), pltpu.roll(a, rows - 1, 0))
    return am, ap


def _neighbours(i, nx, nc):
    nxb, ncb = nx // MIX_CHUNKS, nc // MIX_CHUNKS
    return (i != 0) & (i != nxb), (i != nxb - 1) & (i != ncb - 1)


def _mix_fwd(u, qkv, sf, sb, tabs, conv_w, cnw, gnw, n_heads, nx, ncc, name):
    _, t, s = u.shape
    nc = nx + ncc
    c = CHUNK
    assert nx % MIX_CHUNKS == 0 and ncc % MIX_CHUNKS == 0

    def body(h_ref, b_ref, c_ref, z_ref, rz_ref, hp_ref, hn_ref, cp_ref, cn_ref, q_ref, k_ref, v_ref,
             sf_ref, sb_ref, dc_ref, qft, qbt, w_ref, cnw_ref, gnw_ref, y_ref, o_ref):
        i = pl.program_id(0)
        has_prev, has_next = _neighbours(i, nx, nc)
        a, before, after = _conv_input(h_ref, c_ref, hp_ref, hn_ref, cp_ref, cn_ref)
        am, ap = _shifted(a, before, after, has_prev, has_next)
        w = w_ref[...]
        y0 = w[0:1] * am + w[1:2] * a + w[2:3] * ap
        yb = b_ref[...].astype(F32) * y0
        r = lax.rsqrt(jnp.mean(yb * yb, axis=-1, keepdims=True) + EPS)
        y_ref[:, pl.ds(0, s)] = (_silu(z_ref[...].astype(F32)) * ((yb * r) * cnw_ref[...])).astype(BF16)
        for sub in range(MIX_CHUNKS):
            rows = pl.ds(sub * c, c)
            for h in range(n_heads):
                sl = pl.ds(h * HEAD_DIM, HEAD_DIM)
                q, k, v = q_ref[rows, sl], k_ref[rows, sl], v_ref[rows, sl]
                p = (_dot(q, k, NT) * dc_ref[h]).astype(BF16)
                o = _dot(p, v, NN)
                qf = q.astype(F32)
                o += _dot((qf * qft[h]).astype(BF16), sf_ref[sub, h], NN)
                o += _dot((qf * qbt[h]).astype(BF16), sb_ref[sub, h], NN)
                o_ref[rows, sl] = o
                mu = jnp.mean(o, axis=-1, keepdims=True)
                var = jnp.mean(jnp.square(o - mu), axis=-1, keepdims=True)
                on = (o - mu) * lax.rsqrt(var + EPS)
                y_ref[rows, pl.ds(s + h * HEAD_DIM, HEAD_DIM)] = (
                    _silu(rz_ref[rows, sl].astype(F32)) * (on * gnw_ref[:, sl])).astype(BF16)

    def seg(g):
        return pl.BlockSpec((None, MIX_ROWS, s), lambda i: (g, i, 0))

    prev, nxt = _halo_specs(s, t)
    row = pl.BlockSpec((MIX_ROWS, s), lambda i: (i, 0))
    st = pl.BlockSpec((MIX_CHUNKS, n_heads, HEAD_DIM, HEAD_DIM), lambda i: (i, 0, 0, 0))
    return pl.pallas_call(
        body, name=name, grid=(nc // MIX_CHUNKS,),
        in_specs=[seg(0), seg(1), seg(2), seg(3), seg(4), prev(0), nxt(0), prev(2), nxt(2), seg(0), seg(1), seg(2),
                  st, st, _full((n_heads, c, c)), _full((n_heads, c, HEAD_DIM)), _full((n_heads, c, HEAD_DIM)),
                  _full((3, s)), _full((1, s)), _full((1, s))],
        out_specs=[pl.BlockSpec((MIX_ROWS, 2 * s), lambda i: (i, 0)), row],
        out_shape=[jax.ShapeDtypeStruct((t, 2 * s), BF16), jax.ShapeDtypeStruct((t, s), F32)],
        compiler_params=_params(("parallel",)),
    )(u, u, u, u, u, u, u, u, u, qkv, qkv, qkv, sf, sb, tabs["dc"], tabs["qf"], tabs["qb"], conv_w, cnw, gnw)


def _out_proj_prenorm(ycat, w_out, res, mod, nw_next, mod_next, n_lat, name):
    t, d = ycat.shape
    nb = t // ROW_TILE
    nxb = n_lat // ROW_TILE
    split = len(res) == 2

    def body(a_ref, w_ref, *rest):
        res_refs = rest[:len(res)]
        mod_ref, nw_ref, modn_ref, m_ref, xo_ref, hx_ref, xs = rest[len(res):]
        i = pl.program_id(0)

        @pl.when(i == 0)
        def _():
            xs[...] = jnp.zeros_like(xs)

        cur_ctx = jnp.minimum(i, nb - 1) >= nxb
        prev_ctx = i - 1 >= nxb

        def step(cur, prev):
            mv, mn = mod_ref[...], modn_ref[...]
            shift = jnp.where(prev_ctx, mn[3:4], mn[0:1])
            scale = jnp.where(prev_ctx, mn[4:5], mn[1:2])
            hx_ref[...] = _modulate(xs[prev], nw_ref[...], shift, scale)
            m = _dot(a_ref[...], w_ref[...], NN)
            x_res = jnp.where(cur_ctx, res_refs[1][...], res_refs[0][...]) if split else res_refs[0][...]
            x_new = x_res + jnp.where(cur_ctx, mv[5:6], mv[2:3]) * m
            m_ref[...] = m.astype(BF16)
            xo_ref[...] = x_new
            xs[cur] = x_new

        @pl.when(i % 2 == 0)
        def _():
            step(0, 1)

        @pl.when(i % 2 == 1)
        def _():
            step(1, 0)

    cur = pl.BlockSpec((ROW_TILE, d), lambda i: (jnp.minimum(i, nb - 1), 0))
    prev = pl.BlockSpec((ROW_TILE, d), lambda i: (jnp.maximum(i - 1, 0), 0))
    res_specs = list(_split_rows(nxb, nb)(d, lambda i: jnp.minimum(i, nb - 1))) if split else [cur]
    return pl.pallas_call(
        body, name=name, grid=(nb + 1,),
        in_specs=[cur, _full((d, d))] + res_specs + [_full((8, d)), _full((1, d)), _full((8, d))],
        out_specs=[cur, cur, prev],
        out_shape=[jax.ShapeDtypeStruct((t, d), BF16), jax.ShapeDtypeStruct((t, d), F32),
                   jax.ShapeDtypeStruct((t, d), BF16)],
        scratch_shapes=[pltpu.VMEM((2, ROW_TILE, d), F32)],
        compiler_params=_params(("arbitrary",)))(ycat, w_out, *res, mod, nw_next, mod_next)


def _out_proj_loss(ycat, w_out, xt, mod, tgt, fnw, n_lat, name):
    t, d = xt.shape
    nb = t // ROW_TILE
    nxb = n_lat // ROW_TILE

    def body(a_ref, w_ref, x_ref, mod_ref, t_ref, fw_ref, dx_ref, dm_ref, loss_ref, dw_ref, gacc_ref, xs, ms):
        i = pl.program_id(0)

        @pl.when(i == 0)
        def _():
            xs[...] = jnp.zeros_like(xs)
            ms[...] = jnp.zeros_like(ms)
            loss_ref[...] = jnp.zeros_like(loss_ref)
            dw_ref[...] = jnp.zeros_like(dw_ref)
            gacc_ref[...] = jnp.zeros_like(gacc_ref)

        def step(cur, prev):
            mv = mod_ref[...]
            x_prev, m_prev = xs[prev], ms[prev]
            valid = (i >= 1) & (i - 1 < nxb)
            w = fw_ref[...]
            r = lax.rsqrt(jnp.mean(x_prev * x_prev, axis=-1, keepdims=True) + EPS)
            xn = x_prev * r
            e = xn * w - t_ref[...]
            loss = 0.5 * jnp.sum(jnp.mean(e * e, axis=-1, keepdims=True), axis=0, keepdims=True)
            loss_ref[...] += jnp.where(valid, loss, 0.0)
            dy = e * (1.0 / d)
            dw_ref[0:1, :] += jnp.where(valid, jnp.sum(dy * xn, axis=0, keepdims=True), 0.0)
            dxn = dy * w
            dx = jnp.where(valid, r * (dxn - xn * jnp.mean(dxn * xn, axis=-1, keepdims=True)), 0.0)
            dx_ref[...] = dx
            dm_ref[...] = (dx * mv[2:3]).astype(BF16)
            gacc_ref[2:3, :] += jnp.sum(dx * m_prev, axis=0, keepdims=True)

            m = _dot(a_ref[...], w_ref[...], NN)
            gate = jnp.where(jnp.minimum(i, nb - 1) >= nxb, mv[5:6], mv[2:3])
            xs[cur] = x_ref[...] + gate * m
            ms[cur] = m

        @pl.when(i % 2 == 0)
        def _():
            step(0, 1)

        @pl.when(i % 2 == 1)
        def _():
            step(1, 0)

    cur = pl.BlockSpec((ROW_TILE, d), lambda i: (jnp.minimum(i, nb - 1), 0))
    prev = pl.BlockSpec((ROW_TILE, d), lambda i: (jnp.maximum(i - 1, 0), 0))
    return pl.pallas_call(
        body, name=name, grid=(nb + 1,),
        in_specs=[cur, _full((d, d)), cur, _full((8, d)),
                  pl.BlockSpec((ROW_TILE, d), lambda i: (jnp.clip(i - 1, 0, nxb - 1), 0)), _full((1, d))],
        out_specs=[prev, prev, _full((8, HEAD_DIM)), _full((8, d)), _full((8, d))],
        out_shape=[jax.ShapeDtypeStruct((t, d), F32), jax.ShapeDtypeStruct((t, d), BF16),
                   jax.ShapeDtypeStruct((8, HEAD_DIM), F32), jax.ShapeDtypeStruct((8, d), F32),
                   jax.ShapeDtypeStruct((8, d), F32)],
        scratch_shapes=[pltpu.VMEM((2, ROW_TILE, d), F32), pltpu.VMEM((2, ROW_TILE, d), F32)],
        compiler_params=_params(("arbitrary",)))(ycat, w_out, xt, mod, tgt, fnw)


def _matmul_nt(a, w, tn, name, after=()):
    t, k = a.shape
    n = w.shape[0]
    tm = _mm_rows(t)

    def body(a_ref, w_ref, *rest):
        rest[-1][...] = _dot(a_ref[...], w_ref[...], NT)

    return pl.pallas_call(
        body, name=name, grid=(n // tn, t // tm),
        in_specs=[pl.BlockSpec((tm, k), lambda j, i: (i, 0)), pl.BlockSpec((tn, k), lambda j, i: (j, 0))]
        + [pl.BlockSpec(memory_space=pl.ANY)] * len(after),
        out_specs=pl.BlockSpec((tm, tn), lambda j, i: (i, j)),
        out_shape=jax.ShapeDtypeStruct((t, n), F32),
        compiler_params=_params(("parallel", "parallel")))(a, w, *after)


def _weight_grad(a, b, bm, bt, name):
    t, m = a.shape
    n_g, _, n = b.shape
    nt = t // bt

    def body(a_ref, b_ref, o_ref, acc):
        k = pl.program_id(2)

        @pl.when(k == 0)
        def _():
            acc[...] = jnp.zeros_like(acc)

        acc[...] += _dot(a_ref[...], b_ref[...], TN)

        @pl.when(k == nt - 1)
        def _():
            o_ref[...] = acc[...].astype(o_ref.dtype)

    return pl.pallas_call(
        body, name=name, grid=(n_g, m // bm, nt),
        in_specs=[pl.BlockSpec((bt, bm), lambda g, i, k: (k, i)), pl.BlockSpec((None, bt, n), lambda g, i, k: (g, k, 0))],
        out_specs=pl.BlockSpec((None, bm, n), lambda g, i, k: (g, i, 0)),
        out_shape=jax.ShapeDtypeStruct((n_g, m, n), BF16),
        scratch_shapes=[pltpu.VMEM((bm, n), F32)],
        compiler_params=_params(("parallel", "parallel", "arbitrary")))(a, b)


def _weight_grad_beside_prenorm_bwd(a, b, dhx, xt, dxo, nw, mod, below, n_lat, name):
    t, m = a.shape
    n_g, _, n = b.shape
    d = xt.shape[1]
    bt = _mm_rows(t)
    nt = t // bt
    rows = t // (n_g * nt)
    n_piece = 2 if rows % 32 == 0 and m % 2 == 0 else 1
    rows_p, m_p = rows // n_piece, m // n_piece
    assert rows * n_g * nt == t and rows_p % 8 == 0

    def body(a_ref, b_ref, dh_ref, x_ref, dxo_ref, nw_ref, mod_ref, m_ref, modb_ref,
             o_ref, dx_ref, acc_ref, dm_ref, gacc_ref, acc):
        g, k = pl.program_id(0), pl.program_id(1)
        step = g * nt + k

        @pl.when(step == 0)
        def _():
            acc_ref[...] = jnp.zeros_like(acc_ref)
            gacc_ref[...] = jnp.zeros_like(gacc_ref)

        @pl.when(k == 0)
        def _():
            acc[...] = jnp.zeros_like(acc)

        mv, mb, nw_v = mod_ref[...], modb_ref[...], nw_ref[...]
        for p in range(n_piece):
            rs = pl.ds(p * rows_p, rows_p)
            rowi = step * rows + p * rows_p + lax.broadcasted_iota(jnp.int32, (rows_p, 1), 0)
            ctx = rowi >= n_lat
            w_lat = jnp.where(ctx, 0.0, 1.0)
            w_ctx = 1.0 - w_lat
            scale1 = 1.0 + jnp.where(ctx, mv[4:5], mv[1:2])
            x = x_ref[rs, :]
            r = lax.rsqrt(jnp.mean(x * x, axis=-1, keepdims=True) + EPS)
            xn = x * r
            dh = dh_ref[rs, :]
            dsc = dh * (xn * nw_v)
            acc_ref[0:1, :] += jnp.sum(dh * w_lat, axis=0, keepdims=True)
            acc_ref[1:2, :] += jnp.sum(dsc * w_lat, axis=0, keepdims=True)
            acc_ref[3:4, :] += jnp.sum(dh * w_ctx, axis=0, keepdims=True)
            acc_ref[4:5, :] += jnp.sum(dsc * w_ctx, axis=0, keepdims=True)
            acc_ref[6:7, :] += jnp.sum(dh * scale1 * xn, axis=0, keepdims=True)
            dxn = dh * (nw_v * scale1)
            dx = dxo_ref[rs, :] + r * (dxn - xn * jnp.mean(dxn * xn, axis=-1, keepdims=True))
            dx_ref[rs, :] = dx
            dm_ref[rs, :] = (dx * jnp.where(ctx, mb[5:6], mb[2:3])).astype(BF16)
            dg = dx * m_ref[rs, :].astype(F32)
            gacc_ref[2:3, :] += jnp.sum(dg * w_lat, axis=0, keepdims=True)
            gacc_ref[5:6, :] += jnp.sum(dg * w_ctx, axis=0, keepdims=True)

            ms_ = pl.ds(p * m_p, m_p)
            acc[ms_, :] += _dot(a_ref[:, ms_], b_ref[...], TN)

        @pl.when(k == nt - 1)
        def _():
            o_ref[...] = acc[...].astype(o_ref.dtype)

    side = pl.BlockSpec((rows, d), lambda g, k: (g * nt + k, 0))
    acc8 = _full((8, d))
    return pl.pallas_call(
        body, name=name, grid=(n_g, nt),
        in_specs=[pl.BlockSpec((bt, m), lambda g, k: (k, 0)), pl.BlockSpec((None, bt, n), lambda g, k: (g, k, 0)),
                  side, side, side, _full((1, d)), acc8, side, acc8],
        out_specs=[pl.BlockSpec((None, m, n), lambda g, k: (g, 0, 0)), side, acc8, side, acc8],
        out_shape=[jax.ShapeDtypeStruct((n_g, m, n), BF16), jax.ShapeDtypeStruct((t, d), F32),
                   jax.ShapeDtypeStruct((8, d), F32), jax.ShapeDtypeStruct((t, d), BF16),
                   jax.ShapeDtypeStruct((8, d), F32)],
        scratch_shapes=[pltpu.VMEM((m, n), F32)],
        compiler_params=_params(("arbitrary", "arbitrary")))(a, b, dhx, xt, dxo, nw, mod, *below)


def _mix_bwd_a(dycat, u, o, conv_w, cnw, gnw, n_heads, nx, ncc, name):
    _, t, s = u.shape
    nc = nx + ncc

    def body(dy_ref, h_ref, b_ref, c_ref, z_ref, rz_ref, hp_ref, hn_ref, cp_ref, cn_ref, o_ref, w_ref,
             cnw_ref, gnw_ref, g_ref, dz_ref, db_ref, drz_ref, do_ref, acc_ref):
        i = pl.program_id(0)

        @pl.when(i == 0)
        def _():
            acc_ref[...] = jnp.zeros_like(acc_ref)

        has_prev, has_next = _neighbours(i, nx, nc)
        a, before, after = _conv_input(h_ref, c_ref, hp_ref, hn_ref, cp_ref, cn_ref)
        am, ap = _shifted(a, before, after, has_prev, has_next)
        w = w_ref[...]
        y0 = w[0:1] * am + w[1:2] * a + w[2:3] * ap
        bb = b_ref[...].astype(F32)
        yb = bb * y0
        r = lax.rsqrt(jnp.mean(yb * yb, axis=-1, keepdims=True) + EPS)
        ynn = yb * r
        z = z_ref[...].astype(F32)
        dyc = dy_ref[:, pl.ds(0, s)]
        cw = cnw_ref[...]
        sz, dsz = _silu_and_slope(z)
        dz_ref[...] = (dyc * (ynn * cw) * dsz).astype(BF16)
        dyn = dyc * sz
        acc_ref[0:1, :] += jnp.sum(dyn * ynn, axis=0, keepdims=True)
        dynn = dyn * cw
        dyb = r * (dynn - ynn * jnp.mean(dynn * ynn, axis=-1, keepdims=True))
        db_ref[...] = (dyb * y0).astype(BF16)
        g_ref[...] = dyb * bb
        for h in range(n_heads):
            sl = pl.ds(h * HEAD_DIM, HEAD_DIM)
            ov = o_ref[:, sl]
            mu = jnp.mean(ov, axis=-1, keepdims=True)
            var = jnp.mean(jnp.square(ov - mu), axis=-1, keepdims=True)
            rs = lax.rsqrt(var + EPS)
            on = (ov - mu) * rs
            dyr = dy_ref[:, pl.ds(s + h * HEAD_DIM, HEAD_DIM)]
            rz = rz_ref[:, sl].astype(F32)
            gw = gnw_ref[:, sl]
            srz, dsrz = _silu_and_slope(rz)
            drz_ref[:, sl] = (dyr * (on * gw) * dsrz).astype(BF16)
            dyg = dyr * srz
            acc_ref[1:2, sl] += jnp.sum(dyg * on, axis=0, keepdims=True)
            don = dyg * gw
            do = rs * (don - jnp.mean(don, axis=-1, keepdims=True)
                       - on * jnp.mean(don * on, axis=-1, keepdims=True))
            do_ref[:, sl] = do.astype(BF16)

    def seg(g):
        return pl.BlockSpec((None, MIX_ROWS, s), lambda i: (g, i, 0))

    prev, nxt = _halo_specs(s, t)
    row = pl.BlockSpec((MIX_ROWS, s), lambda i: (i, 0))
    return pl.pallas_call(
        body, name=name, grid=(nc // MIX_CHUNKS,),
        in_specs=[pl.BlockSpec((MIX_ROWS, 2 * s), lambda i: (i, 0)), seg(0), seg(1), seg(2), seg(3), seg(4),
                  prev(0), nxt(0), prev(2), nxt(2), row, _full((3, s)), _full((1, s)), _full((1, s))],
        out_specs=[row, row, row, row, row, _full((8, s))],
        out_shape=[jax.ShapeDtypeStruct((t, s), F32)] + [jax.ShapeDtypeStruct((t, s), BF16)] * 4
        + [jax.ShapeDtypeStruct((8, s), F32)],
        compiler_params=_params(("arbitrary",)),
    )(dycat, u, u, u, u, u, u, u, u, u, o, conv_w, cnw, gnw)


def _grad_state_sweep(qkv, do, tabs, n_heads, nx, ncc, name):
    return _pair_sweep((qkv, 0), (do, None), tabs["qf"], tabs["qb"], tabs["cdf"], tabs["cdb"], n_heads, nx, ncc, True, name)


def _mix_bwd_b(u, g, dz, db, drz, qkv, do, sf, sb, gf, gb, tabs, cos, sa, sb_tab, conv_w,
               n_heads, nx, ncc, name):
    _, t, s = u.shape
    nc = nx + ncc
    c = CHUNK
    k_scale = HEAD_DIM ** -0.5

    def body(h_ref, c_ref, g_ref, gp_ref, gn_ref, dz_ref, db_ref, drz_ref, q_ref, k_ref, v_ref, do_ref,
             sf_ref, sb_ref, gf_ref, gb_ref, dc_t, dlf_t, dlb_t, qft, kft, qbt, kbt, cdf, cdb, lg_ref,
             cos_ref, sa_ref, sb_ref2, w_ref, du_ref, dw_ref, dlg_ref):
        i = pl.program_id(0)

        @pl.when(i == 0)
        def _():
            dw_ref[...] = jnp.zeros_like(dw_ref)
            dlg_ref[...] = jnp.zeros_like(dlg_ref)

        has_prev, has_next = _neighbours(i, nx, nc)
        gv = g_ref[...]
        gm, gp = _shifted(gv, gp_ref[7:8], gn_ref[0:1], has_prev, has_next)
        w = w_ref[...]
        da = w[0:1] * gp + w[1:2] * gv + w[2:3] * gm
        hh, cc = h_ref[...].astype(F32), c_ref[...].astype(F32)
        du_ref[0] = (da * cc).astype(BF16)
        du_ref[2] = (da * hh).astype(BF16)
        a = cc * hh
        dw_ref[0:1, :] += jnp.sum(a * gp, axis=0, keepdims=True)
        dw_ref[1:2, :] += jnp.sum(a * gv, axis=0, keepdims=True)
        dw_ref[2:3, :] += jnp.sum(a * gm, axis=0, keepdims=True)
        du_ref[1] = db_ref[...]
        du_ref[3] = dz_ref[...]
        du_ref[7] = drz_ref[...]

        pos = lax.broadcasted_iota(jnp.int32, (c, HEAD_DIM), 0).astype(F32)
        w_q_f, w_q_b, w_k_f = pos + 1.0, c - pos, c - 1.0 - pos
        row8 = lax.broadcasted_iota(jnp.int32, (8, HEAD_DIM), 0)
        lane8 = lax.broadcasted_iota(jnp.int32, (8, HEAD_DIM), 1)
        dlg = jnp.zeros((8, HEAD_DIM), F32)
        for sub, h in [(sub, h) for sub in range(MIX_CHUNKS) for h in range(n_heads)]:
            rows = pl.ds(sub * c, c)
            co, ra, rb = cos_ref[rows, :], sa_ref[rows, :], sb_ref2[rows, :]
            sl = pl.ds(h * HEAD_DIM, HEAD_DIM)
            q, k, v, do = q_ref[rows, sl], k_ref[rows, sl], v_ref[rows, sl], do_ref[rows, sl]
            qf, kf, dof = q.astype(F32), k.astype(F32), do.astype(F32)
            s_f, s_b, g_f, g_b = sf_ref[sub, h], sb_ref[sub, h], gf_ref[sub, h], gb_ref[sub, h]
            p = _dot(q, k, NT)
            pd = _dot(do, v, NT)
            pdd = (pd * dc_t[h]).astype(BF16)
            dq = _dot(pdd, k, NN)
            dk = _dot(pdd, q, TN)
            dv = _dot((p * dc_t[h]).astype(BF16), do, TN)
            dq_f = _dot((dof * qft[h]).astype(BF16), s_f, NT)
            dq_b = _dot((dof * qbt[h]).astype(BF16), s_b, NT)
            dk_f = _dot(v, g_f, NT) * kft[h]
            dk_b = _dot(v, g_b, NT) * kbt[h]
            dv += _dot((kf * kft[h]).astype(BF16), g_f, NN) + _dot((kf * kbt[h]).astype(BF16), g_b, NN)
            ppd = p * pd
            cd_f, cd_b = cdf[h][0:1, :], cdb[h][0:1, :]
            t_f = _sum_all(dlf_t[h] * ppd + w_q_f * qf * dq_f + w_k_f * kf * dk_f
                           + float(c) * (cd_f * (g_f.astype(F32) * s_f.astype(F32))))
            t_b = _sum_all(dlb_t[h] * ppd + w_q_b * qf * dq_b + pos * kf * dk_b
                           + float(c) * (cd_b * (g_b.astype(F32) * s_b.astype(F32))))
            dlg += jnp.where((row8 == 0) & (lane8 == h), t_f, 0.0) + jnp.where((row8 == 1) & (lane8 == h), t_b, 0.0)
            du_ref[4, rows, sl] = _rope_bwd(dq + dq_f + dq_b, co, ra, rb).astype(BF16)
            du_ref[5, rows, sl] = (_rope_bwd(dk + dk_f + dk_b, co, ra, rb) * k_scale).astype(BF16)
            du_ref[6, rows, sl] = dv.astype(BF16)
        dlg_ref[...] += dlg

        @pl.when(i == nc // MIX_CHUNKS - 1)
        def _():
            dlg_ref[...] = dlg_ref[...] * lg_ref[...]

    def seg(gi):
        return pl.BlockSpec((None, MIX_ROWS, s), lambda i: (gi, i, 0))

    per = MIX_ROWS // 8
    n8 = t // 8
    row = pl.BlockSpec((MIX_ROWS, s), lambda i: (i, 0))
    st = pl.BlockSpec((MIX_CHUNKS, n_heads, HEAD_DIM, HEAD_DIM), lambda i: (i, 0, 0, 0))
    tab = pl.BlockSpec((MIX_ROWS, HEAD_DIM), lambda i: (i, 0))
    hc = _full((n_heads, c, HEAD_DIM))
    cc_ = _full((n_heads, c, c))
    h8 = _full((n_heads, 8, HEAD_DIM))
    return pl.pallas_call(
        body, name=name, grid=(nc // MIX_CHUNKS,),
        in_specs=[seg(0), seg(2), row,
                  pl.BlockSpec((8, s), lambda i: (jnp.maximum(i * per - 1, 0), 0)),
                  pl.BlockSpec((8, s), lambda i: (jnp.minimum((i + 1) * per, n8 - 1), 0)),
                  row, row, row, seg(0), seg(1), seg(2), row, st, st, st, st, cc_, cc_, cc_, hc, hc, hc, hc, h8, h8,
                  _full((8, HEAD_DIM)), tab, tab, tab, _full((3, s))],
        out_specs=[pl.BlockSpec((8, MIX_ROWS, s), lambda i: (0, i, 0)), _full((8, s)), _full((8, HEAD_DIM))],
        out_shape=[jax.ShapeDtypeStruct((8, t, s), BF16), jax.ShapeDtypeStruct((8, s), F32),
                   jax.ShapeDtypeStruct((8, HEAD_DIM), F32)],
        compiler_params=_params(("arbitrary",)),
    )(u, u, g, g, g, dz, db, drz, qkv, qkv, qkv, do, sf, sb, gf, gb, tabs["dc"], tabs["dlf"], tabs["dlb"],
      tabs["qf"], tabs["kf"], tabs["qb"], tabs["kb"], tabs["cdf"], tabs["cdb"], tabs["lg"], cos, sa, sb_tab, conv_w)


def _in_proj_bwd(du, wgs, tm, gs, name, after=()):
    n_seg, t, s = du.shape
    d = wgs[0].shape[1]
    n_w = len(wgs)
    widths = [w.shape[2] for w in wgs]
    assert sum(widths) == s

    def body(a_ref, *rest):
        w_refs, o_ref = rest[:n_w], rest[-1]
        g = pl.program_id(1)
        part = None
        for j in range(gs):
            col = 0
            for w_ref, width in zip(w_refs, widths):
                term = _dot(a_ref[j, :, col:col + width], w_ref[j], NT)
                part = term if part is None else part + term
                col += width

        @pl.when(g == 0)
        def _():
            o_ref[...] = part

        @pl.when(g > 0)
        def _():
            o_ref[...] += part

    return pl.pallas_call(
        body, name=name, grid=(t // tm, n_seg // gs),
        in_specs=[pl.BlockSpec((gs, tm, s), lambda i, g: (g, i, 0))]
        + [pl.BlockSpec((gs, d, width), lambda i, g: (g, 0, 0)) for width in widths]
        + [pl.BlockSpec(memory_space=pl.ANY)] * len(after),
        out_specs=pl.BlockSpec((tm, d), lambda i, g: (i, 0)),
        out_shape=jax.ShapeDtypeStruct((t, d), F32),
        compiler_params=_params(("parallel", "arbitrary")))(du, *wgs, *after)


def _prenorm_bwd_first(dhx, x, ctx, dxo, nw, mod, name):
    n_lat, d = x.shape
    t = n_lat + ctx.shape[0]
    nxb = n_lat // ROW_TILE

    def body(dh_ref, x_ref, c_ref, dxo_ref, nw_ref, mod_ref, dx_ref, acc_ref):
        i = pl.program_id(0)

        @pl.when(i == 0)
        def _():
            acc_ref[...] = jnp.zeros_like(acc_ref)

        ctx = i >= nxb
        m = mod_ref[...]
        scale1 = 1.0 + jnp.where(ctx, m[4:5], m[1:2])
        x = jnp.where(ctx, c_ref[...], x_ref[...])
        nw_v = nw_ref[...]
        r = lax.rsqrt(jnp.mean(x * x, axis=-1, keepdims=True) + EPS)
        xn = x * r
        dh = dh_ref[...]
        dshift = jnp.sum(dh, axis=0, keepdims=True)
        dscale = jnp.sum(dh * (xn * nw_v), axis=0, keepdims=True)
        acc_ref[6:7, :] += jnp.sum(dh * scale1 * xn, axis=0, keepdims=True)
        dxn = dh * (nw_v * scale1)
        dx = dxo_ref[...] + r * (dxn - xn * jnp.mean(dxn * xn, axis=-1, keepdims=True))

        @pl.when(i < nxb)
        def _():
            acc_ref[0:1, :] += dshift
            acc_ref[1:2, :] += dscale
            dx_ref[...] = dx

        @pl.when(i >= nxb)
        def _():
            acc_ref[3:4, :] += dshift
            acc_ref[4:5, :] += dscale

    row = pl.BlockSpec((ROW_TILE, d), lambda i: (i, 0))
    lat, cx = _split_rows(nxb, t // ROW_TILE)(d)
    acc = _full((8, d))
    return pl.pallas_call(body, name=name, grid=(t // ROW_TILE,),
                          in_specs=[row, lat, cx, row, _full((1, d)), acc],
                          out_specs=[lat, acc],
                          out_shape=[jax.ShapeDtypeStruct((n_lat, d), F32), jax.ShapeDtypeStruct((8, d), F32)],
                          compiler_params=_params(("arbitrary",)))(dhx, x, ctx, dxo, nw, mod)


def _adamw(g, w, m, v):
    m = ADAM_B1 * m + (1.0 - ADAM_B1) * g
    v = ADAM_B2 * v + (1.0 - ADAM_B2) * jnp.square(g)
    m_hat = m / (1.0 - ADAM_B1 ** ADAM_STEP)
    v_hat = v / (1.0 - ADAM_B2 ** ADAM_STEP)
    delta = -ADAM_LR * (m_hat / (jnp.sqrt(v_hat) + ADAM_EPS) + ADAM_WD * w)
    return delta, m, v


def _sum_adamw(parts, w, m, v, name, row0=0, into=None):
    n_p, r, n = parts.shape
    r_all = w.shape[0]
    part_block_bytes = 4 * 1024 * 1024
    br = 8
    for cand in (512, 256, 128, 64, 32, 16):
        if r % cand == 0 and row0 % cand == 0 and n_p * cand * n * parts.dtype.itemsize <= part_block_bytes:
            br = cand
            break
    blk0 = row0 // br

    def body(p_ref, w_ref, m_ref, v_ref, *rest):
        g_out, d_out, m_out, v_out = rest[-4:]
        g = p_ref[0].astype(F32)
        for j in range(1, n_p):
            g = g + p_ref[j].astype(F32)
        g_out[...] = g
        d_out[...], m_out[...], v_out[...] = _adamw(g, w_ref[...], m_ref[...], v_ref[...])

    row = pl.BlockSpec((br, n), lambda i: (i + blk0, 0))
    kept = [] if into is None else list(into)
    return pl.pallas_call(body, name=name, grid=(r // br,),
                          in_specs=[pl.BlockSpec((n_p, br, n), lambda i: (0, i, 0)), row, row, row]
                          + [pl.BlockSpec(memory_space=pl.ANY)] * len(kept),
                          out_specs=[row] * 4, out_shape=[jax.ShapeDtypeStruct((r_all, n), F32)] * 4,
                          input_output_aliases={4 + j: j for j in range(len(kept))},
                          compiler_params=_params(("parallel",)))(parts, w, m, v, *kept)


def _rope_tables(n_lat, n_ctx):
    f = HEAD_DIM // 4
    rows = n_lat // GRID_W
    inv = ROPE_BASE ** (-jnp.arange(f, dtype=F32) / f)
    ang_r = jnp.arange(rows).astype(F32)[:, None] * inv[None, :]
    ang_c = jnp.arange(GRID_W).astype(F32)[:, None] * inv[None, :]

    cr, sr, cc, sc = jnp.cos(ang_r), jnp.sin(ang_r), jnp.cos(ang_c), jnp.sin(ang_c)
    zr, zc = jnp.zeros_like(cr), jnp.zeros_like(cc)

    def table(by_row, by_col):
        both = by_row[:, None, :] + by_col[None, :, :]
        return both.reshape(n_lat, HEAD_DIM)

    cos = table(jnp.concatenate([cr, cr, zr, zr], axis=-1), jnp.concatenate([zc, zc, cc, cc], axis=-1))
    sa = table(jnp.concatenate([-sr, zr, zr, zr], axis=-1), jnp.concatenate([zc, zc, -sc, zc], axis=-1))
    sb = table(jnp.concatenate([zr, sr, zr, zr], axis=-1), jnp.concatenate([zc, zc, zc, sc], axis=-1))
    pad = jnp.zeros((n_ctx, HEAD_DIM), F32)
    return (jnp.concatenate([cos, pad + 1.0], axis=0), jnp.concatenate([sa, pad], axis=0),
            jnp.concatenate([sb, pad], axis=0))


def _pad_rows(a, rows):
    return jnp.pad(a, [(0, rows - a.shape[0])] + [(0, 0)] * (a.ndim - 1))


def _pad_cols(a, cols):
    return jnp.pad(a, [(0, 0), (0, cols - a.shape[1])])


def kernel(x, c, ctx, c_ctx, norm_w, w_mod, b_mod, w_in, conv_w, conv_norm_w, ret_norm_w, ret_decay_f, ret_decay_b, w_out, final_norm_w, loss_target, m_c_ctx, m_norm_w, m_w_mod, m_b_mod, m_w_in, m_conv_w, m_conv_norm_w, m_ret_norm_w, m_ret_decay_f, m_ret_decay_b, m_w_out, m_final_norm_w, v_c_ctx, v_norm_w, v_w_mod, v_b_mod, v_w_in, v_conv_w, v_conv_norm_w, v_ret_norm_w, v_ret_decay_f, v_ret_decay_b, v_w_out, v_final_norm_w):
    depth = norm_w.shape[0]
    n_lat, d = x.shape[1], x.shape[2]
    n_ctx = ctx.shape[1]
    s = d // 2
    n_heads = ret_decay_f.shape[1]
    nx, ncc = n_lat // CHUNK, n_ctx // CHUNK
    n_mod = w_mod.shape[2]
    n_cw = conv_w.shape[2]
    r_out = w_out.shape[1]
    assert s == n_heads * HEAD_DIM and w_in.shape[2] == s and N_DEV * r_out == d
    assert n_lat % ROW_TILE == 0 and n_ctx % ROW_TILE == 0 and 3 * depth * n_cw <= d and N_DEV * n_mod == 3 * d
    me = 4 * lax.axis_index("x") + 2 * lax.axis_index("y") + lax.axis_index("c")

    w_in_bf = [w_in[l].astype(BF16) for l in range(depth)]
    w_out_bf = [w_out[l].astype(BF16) for l in range(depth)]

    first = jnp.concatenate([c.reshape(1, d), _pad_cols(conv_w.reshape(1, -1), d), jnp.zeros((6, d), F32)], axis=0)
    (first_g,) = _all_gather([first], "gather_cond")
    first_g = first_g.reshape(N_DEV, 8, d)
    c_all = first_g[:, 0, :]
    conv_full = first_g[:, 1, :3 * depth * n_cw].reshape(N_DEV, depth, 3, n_cw)
    conv_full = conv_full.transpose(1, 2, 0, 3).reshape(depth, 3, N_DEV * n_cw)
    c9 = jnp.concatenate([c_all, c_ctx.reshape(1, d), jnp.zeros((7, d), F32)], axis=0)

    b_sh = lax.dynamic_slice(b_mod, (0, me * n_mod), (depth, n_mod))
    mod_sh = jnp.concatenate([_mod_rows(c9, w_mod[l], b_sh[l:l + 1], f"mod_rows_l{l}") for l in range(depth)], axis=0)
    (mod_g,) = _all_gather([mod_sh], "gather_mod")
    mod_g = mod_g.reshape(N_DEV, depth, 16, n_mod)
    mods = []
    for l in range(depth):
        mine = lax.dynamic_index_in_dim(mod_g[:, l], me, axis=1, keepdims=False).reshape(3, d)
        cx = mod_g[:, l, 8, :].reshape(3, d)
        mods.append(jnp.concatenate([mine, cx, jnp.zeros((2, d), F32)], axis=0))

    halves = [w_in_bf[0][:, :s // 2], w_in_bf[0][:, s // 2:]]
    near, order = [], [mod_g]
    for j, part in enumerate(halves):
        near.append(_push_start([part], [_landing(part, me)], "near", f"w_in0_start_{j}", after=order))
        order = near[-1][4:]
    pending = []
    for k in range(depth):
        srcs = [w_out_bf[k]] + ([w_in_bf[k]] if k > 0 else [])
        started = _push_start(srcs, [_landing(a, me) for a in srcs], "gather", f"weights_start_l{k}", after=order)
        pending.append(started[:4])
        order = started[4:]
    w_in_g = [None] * depth
    w_out_g = [None] * depth

    cos, sa, sb_tab = _rope_tables(n_lat, n_ctx)
    t_all = n_lat + n_ctx

    saved = []
    xt = hx_next = None
    for l in range(depth):
        tiles = _tiles(l, t_all, d)
        names = ["dc", "dlf", "dlb", "qf", "kf", "qb", "kb", "cdf", "cdb", "lg"]
        dec = jnp.stack([ret_decay_f[l], ret_decay_b[l]], axis=0)
        tabs = dict(zip(names, _decay_tables(dec, n_heads, f"decay_tables_l{l}")))
        if l == 0:
            hx = _prenorm_first(x[0], ctx[0], norm_w[0:1], mods[0], "prenorm_l0", after=order)
            gathered, out, after = [], None, hx
            for j in range(2):
                (landed,) = _push_wait(*near[j][:4], "near", after, f"w_in0_wait_{j}")
                relay = _push_start([], [landed], "relay", f"w_in0_relay_start_{j}")
                (landed,) = _push_wait(*relay[:4], "relay", relay[4], f"w_in0_relay_wait_{j}")
                gathered.append(landed)
                out = _in_proj(hx, landed, cos, sa, sb_tab, s, j, tiles["in_tm_half"], f"in_proj_l0_{j}", into=out)
                after = out[0]
            u, qkv = out
            w_in_g[0] = gathered
        else:
            landed = _push_wait(*pending[l], "gather", xt, f"weights_wait_l{l}")
            w_out_g[l], w_in_g[l] = landed[0].reshape(d, d), [landed[1]]
            hx = hx_next
            u, qkv = _in_proj(hx, w_in_g[l][0], cos, sa, sb_tab, s, 0, tiles["in_tm"], f"in_proj_l{l}")
        sf, sb = _state_sweep(qkv, tabs, n_heads, nx, ncc, f"state_sweep_l{l}")
        ycat, o = _mix_fwd(u, qkv, sf, sb, tabs, conv_full[l], conv_norm_w[l:l + 1], ret_norm_w[l:l + 1],
                           n_heads, nx, ncc, f"mix_fwd_l{l}")
        if l == 0:
            (landed,) = _push_wait(*pending[0], "gather", ycat, "weights_wait_l0")
            w_out_g[0] = landed.reshape(d, d)
        m_res = x_new = None
        if l < depth - 1:
            res = (x[0], ctx[0]) if l == 0 else (xt,)
            m_res, x_new, hx_next = _out_proj_prenorm(ycat, w_out_g[l], res, mods[l], norm_w[l + 1:l + 2], mods[l + 1],
                                                      n_lat, f"out_proj_l{l}")
        else:
            dxt, dm, loss_blk, dfnw, gate_acc = _out_proj_loss(ycat, w_out_g[l], xt, mods[l], loss_target[0],
                                                               final_norm_w.reshape(1, d), n_lat, f"out_proj_loss_l{l}")
        saved.append(dict(tabs=tabs, xt=xt, hx=hx, u=u, qkv=qkv, sf=sf, sb=sb, ycat=ycat, o=o, m=m_res, tiles=tiles))
        xt = x_new

    dmod_x, dmod_c, dnw, dcnw, dgnw, dconv, ddec, dwin, dwout = [], [], [], [], [], [], [], [], []
    started_token = ()
    for l in reversed(range(depth)):
        sv = saved[l]
        tiles = sv["tiles"]
        dycat = _matmul_nt(dm, w_out_g[l], tiles["ob_tn"], f"out_proj_bwd_l{l}", after=started_token)
        dwout.append(_weight_grad(sv["ycat"], dm.reshape(1, *dm.shape), tiles["wo_bm"], tiles["wo_bt"],
                                  f"w_out_grad_l{l}")[0])
        g, dz, db, drz, do, norm_acc = _mix_bwd_a(dycat, sv["u"], sv["o"], conv_full[l], conv_norm_w[l:l + 1],
                                                   ret_norm_w[l:l + 1], n_heads, nx, ncc, f"mix_bwd_a_l{l}")
        gf, gb = _grad_state_sweep(sv["qkv"], do, sv["tabs"], n_heads, nx, ncc, f"grad_state_sweep_l{l}")
        du, conv_acc, dlg = _mix_bwd_b(sv["u"], g, dz, db, drz, sv["qkv"], do, sv["sf"], sv["sb"],
                                       gf, gb, sv["tabs"], cos, sa, sb_tab, conv_full[l], n_heads, nx, ncc,
                                       f"mix_bwd_b_l{l}")
        gate_acc_l = gate_acc
        if l > 0:
            dhx = _in_proj_bwd(du, w_in_g[l], tiles["bwd_tm"], tiles["bwd_gs"], f"in_proj_bwd_l{l}")
            below = (saved[l - 1]["m"], mods[l - 1])
            dwin_l, dxt, pre_acc, dm, gate_acc = _weight_grad_beside_prenorm_bwd(
                sv["hx"], du, dhx, sv["xt"], dxt, norm_w[l:l + 1], mods[l], below, n_lat, f"w_in_grad_l{l}")
        else:
            dwin_l = _weight_grad(sv["hx"], du, tiles["wg_bm"], tiles["wg_bt"], f"w_in_grad_l{l}")
        srcs = [dwin_l, dwout[-1].reshape(N_DEV, r_out, d)]
        lands = [_landing(lax.dynamic_index_in_dim(a, me, axis=0, keepdims=False), me) for a in srcs]
        started = _push_start(srcs, lands, "scatter", f"grads_start_l{l}")
        dwin.append(started[:4])
        started_token = started[4:]
        if l == 0:
            dhx = _in_proj_bwd(du, w_in_g[l], tiles["bwd_tm"], tiles["bwd_gs"], f"in_proj_bwd_l{l}", after=started[4:])
            dxt, pre_acc = _prenorm_bwd_first(dhx, x[0], ctx[0], dxt, norm_w[l:l + 1], mods[l], f"prenorm_bwd_l{l}")
        dmod_x.append(jnp.concatenate([pre_acc[0], pre_acc[1], gate_acc_l[2]]))
        dmod_c.append(jnp.concatenate([pre_acc[3], pre_acc[4], gate_acc_l[5]]))
        dnw.append(pre_acc[6])
        dcnw.append(norm_acc[0])
        dgnw.append(norm_acc[1])
        dconv.append(conv_acc[0:3])
        ddec.append(dlg[0:2, :n_heads])
    for lst in (dmod_x, dmod_c, dnw, dcnw, dgnw, dconv, ddec, dwin, dwout):
        lst.reverse()
    grad_x = dxt.reshape(1, n_lat, d)

    rows = []
    for l in range(depth):
        rows += [dmod_x[l], dmod_c[l]]
    (dmod_g,) = _all_gather([_pad_rows(jnp.stack(rows, axis=0), 8)], "gather_dmod")
    dmod_g = dmod_g.reshape(N_DEV, 8, 3 * d)
    mine_cols = lax.dynamic_slice(dmod_g, (0, 0, me * n_mod), (N_DEV, 8, n_mod))
    g_wmod, dcc = [], jnp.zeros((d,), F32)
    for l in range(depth):
        gw, dc_part = _mod_grads(mine_cols[:, 2 * l], mine_cols[:, 2 * l + 1], c9, w_mod[l], f"mod_grads_l{l}")
        g_wmod.append(gw)
        dcc = dcc + dc_part[0]

    n_small = 16
    small = jnp.concatenate([
        jnp.stack(dnw, axis=0),
        jnp.concatenate(dcnw).reshape(1, -1),
        jnp.concatenate(dgnw).reshape(1, -1),
        dfnw[0:1],
        dcc.reshape(1, d),
        jnp.stack(dconv, axis=0).reshape(-1, d),
        _pad_cols(jnp.stack(ddec, axis=0).reshape(1, -1), d),
    ], axis=0)
    assert depth * s == d and small.shape[0] < n_small
    n_rows = small.shape[0]
    small = jnp.concatenate([small, _pad_cols(loss_blk[0:1], d)], axis=0)
    (small_g,) = _all_gather([_pad_rows(small, n_small)], "gather_small")
    small_g = small_g.reshape(N_DEV, n_small, d)

    def pack_small(nw_, cn_, gn_, fn_, cc_, df_, db_):
        return _pad_rows(jnp.concatenate([
            nw_, cn_.reshape(1, -1), gn_.reshape(1, -1), fn_.reshape(1, d), cc_.reshape(1, d),
            jnp.zeros((n_rows - depth - 5, d), F32),
            _pad_cols(jnp.stack([df_, db_], axis=1).reshape(1, -1), d)], axis=0), n_small)

    w_s = pack_small(norm_w, conv_norm_w, ret_norm_w, final_norm_w, c_ctx, ret_decay_f, ret_decay_b)
    m_s = pack_small(m_norm_w, m_conv_norm_w, m_ret_norm_w, m_final_norm_w, m_c_ctx, m_ret_decay_f, m_ret_decay_b)
    v_s = pack_small(v_norm_w, v_conv_norm_w, v_ret_norm_w, v_final_norm_w, v_c_ctx, v_ret_decay_f, v_ret_decay_b)
    small_out = _sum_adamw(small_g, w_s, m_s, v_s, "adamw_small")
    loss = small_out[0][n_rows, 0]

    def unpack_small(a):
        nw_ = a[0:depth]
        cn_ = a[depth].reshape(depth, s)
        gn_ = a[depth + 1].reshape(depth, s)
        fn_ = a[depth + 2]
        cc_ = a[depth + 3]
        dd = a[n_rows - 1, :depth * 2 * n_heads].reshape(depth, 2, n_heads)
        return dict(c_ctx=cc_, norm_w=nw_, conv_norm_w=cn_, ret_norm_w=gn_, ret_decay_f=dd[:, 0], ret_decay_b=dd[:, 1],
                    final_norm_w=fn_)

    res = {}
    for kind, arr in zip(("grad", "delta", "m", "v"), small_out):
        for k_, val in unpack_small(arr).items():
            res[(kind, k_)] = val

    bm_parts = jnp.concatenate([dmod_g[:, 0:2 * depth:2].reshape(N_DEV, depth, 3 * d),
                                dmod_g[:, 1:2 * depth:2].reshape(N_DEV, depth, 3 * d)], axis=0)
    bm_parts = jnp.concatenate([bm_parts, jnp.zeros((2 * N_DEV, 8 - depth, 3 * d), F32)], axis=1)
    pad8 = lambda a: _pad_rows(a, 8)
    bm_out = _sum_adamw(bm_parts, pad8(b_mod), pad8(m_b_mod), pad8(v_b_mod), "adamw_b_mod")
    for kind, arr in zip(("grad", "delta", "m", "v"), bm_out):
        res[(kind, "b_mod")] = arr[:depth]

    conv_rows = small_g[:, depth + 4:depth + 4 + 3 * depth * s // d].reshape(N_DEV, depth * 3, s)
    conv_mine = lax.dynamic_slice(conv_rows, (0, 0, me * n_cw), (N_DEV, depth * 3, n_cw))
    conv_mine = jnp.concatenate([conv_mine, jnp.zeros((N_DEV, 8 - depth * 3, n_cw), F32)], axis=1)
    cw2 = lambda a: _pad_rows(a.reshape(depth * 3, n_cw), 8)
    cw_out = _sum_adamw(conv_mine, cw2(conv_w), cw2(m_conv_w), cw2(v_conv_w), "adamw_conv_w")
    for kind, arr in zip(("grad", "delta", "m", "v"), cw_out):
        res[(kind, "conv_w")] = arr[:depth * 3].reshape(depth, 3, n_cw)

    wm_out = _sum_adamw(jnp.stack(g_wmod, axis=0).reshape(1, depth * d, n_mod), w_mod.reshape(depth * d, n_mod),
                        m_w_mod.reshape(depth * d, n_mod), v_w_mod.reshape(depth * d, n_mod), "adamw_w_mod")
    for kind, arr in zip(("grad", "delta", "m", "v"), wm_out):
        res[(kind, "w_mod")] = arr.reshape(depth, d, n_mod)

    wi_out = wo_out = None
    after = wm_out[0]
    for l in reversed(range(depth)):
        win_parts, wout_parts = _push_wait(*dwin[l], "scatter", after, f"grads_wait_l{l}")
        wi_out = _sum_adamw(win_parts, w_in.reshape(depth * d, s), m_w_in.reshape(depth * d, s),
                            v_w_in.reshape(depth * d, s), f"adamw_w_in_l{l}", row0=l * d, into=wi_out)
        wo_out = _sum_adamw(wout_parts, w_out.reshape(depth * r_out, d), m_w_out.reshape(depth * r_out, d),
                            v_w_out.reshape(depth * r_out, d), f"adamw_w_out_l{l}", row0=l * r_out, into=wo_out)
        after = wo_out[0]
    for kind, arr in zip(("grad", "delta", "m", "v"), wi_out):
        res[(kind, "w_in")] = arr.reshape(depth, d, s)
    for kind, arr in zip(("grad", "delta", "m", "v"), wo_out):
        res[(kind, "w_out")] = arr.reshape(depth, r_out, d)

    order = ["c_ctx", "norm_w", "w_mod", "b_mod", "w_in", "conv_w", "conv_norm_w", "ret_norm_w", "ret_decay_f",
             "ret_decay_b", "w_out", "final_norm_w"]
    outs = [loss, grad_x]
    for kind in ("grad", "delta", "m", "v"):
        outs += [res[(kind, k_)] for k_ in order]
    return tuple(outs)
```

```python
import jax
import jax.numpy as jnp
from jax import lax
from jax.experimental import pallas as pl
from jax.experimental.pallas import tpu as pltpu

F32 = jnp.float32
BF16 = jnp.bfloat16

EPS = 1e-6
CHUNK = 128
HEAD_DIM = 128
GRID_W = 64
ROPE_BASE = 10000.0
N_DEV = 8
ADAM_LR, ADAM_B1, ADAM_B2, ADAM_EPS, ADAM_WD, ADAM_STEP = 0.001, 0.9, 0.999, 1e-08, 0.01, 10

ROW_TILE = 256
V7X_VMEM_LIMIT = 56 * 1024 * 1024

NN = ((1,), (0,))
NT = ((1,), (1,))
TN = ((0,), (0,))


def _dot(a, b, dims):
    return lax.dot_general(a, b, (dims, ((), ())), preferred_element_type=F32)


def _params(sem=None):
    if sem is None:
        return pltpu.CompilerParams(vmem_limit_bytes=V7X_VMEM_LIMIT)
    return pltpu.CompilerParams(dimension_semantics=sem, vmem_limit_bytes=V7X_VMEM_LIMIT)


def _silu(z):
    return z * jax.nn.sigmoid(z)


def _dsilu(z):
    s = jax.nn.sigmoid(z)
    return s * (1.0 + z * (1.0 - s))


def _silu_and_slope(z):
    s = jax.nn.sigmoid(z)
    return z * s, s * (1.0 + z * (1.0 - s))


def _sum_all(a):
    return jnp.sum(jnp.sum(a, axis=1, keepdims=True), axis=0, keepdims=True)


def _mm_rows(t):
    return 768 if t % 768 == 0 else ROW_TILE


def _rows_or(t, rows):
    return rows if t % rows == 0 else _mm_rows(t)


def _tiles(layer, t, d):
    return dict(in_tm=_rows_or(t, 1408), in_tm_half=_rows_or(t, 2112), bwd_gs=2, wg_bm=d, wg_bt=_mm_rows(t),
                wo_bm=d, wo_bt=_mm_rows(t), ob_tn=d, bwd_tm=_rows_or(t, 1056))


def _full(shape):
    n = len(shape)
    return pl.BlockSpec(shape, lambda *_: (0,) * n)


def _peers(x, y, c):
    return [(x, y, 1 - c), (1 - x, y, c), (x, 1 - y, c), (1 - x, 1 - y, c),
            (1 - x, y, 1 - c), (x, 1 - y, 1 - c), (1 - x, 1 - y, 1 - c)]


def _lin(p):
    return 4 * p[0] + 2 * p[1] + p[2]


def _all_gather(arrays, name):
    n_arr = len(arrays)
    space = pltpu.VMEM

    def body(*refs):
        ins, outs = refs[:n_arr], refs[n_arr:2 * n_arr]
        send_sems, recv_sems, local_sems = refs[2 * n_arr:]
        x, y, c = lax.axis_index("x"), lax.axis_index("y"), lax.axis_index("c")
        me, sibling = (x, y, c), (x, y, 1 - c)
        chips = [(1 - x, y), (x, 1 - y), (1 - x, 1 - y)]
        every = []
        locals_ = []
        for a in range(n_arr):
            m_per = ins[a].shape[0]
            out_ref = outs[a]

            def rows(p, out_ref=out_ref, m_per=m_per):
                return out_ref.at[pl.ds(_lin(p) * m_per, m_per), :]

            def copy(k, block, to, src=None, a=a, rows=rows):
                return pltpu.make_async_remote_copy(
                    src_ref=rows(block) if src is None else src, dst_ref=rows(block),
                    send_sem=send_sems.at[a, k], recv_sem=recv_sems.at[a, k],
                    device_id=to, device_id_type=pl.DeviceIdType.MESH)

            mine = pltpu.make_async_copy(ins[a], rows(me), local_sems.at[a])
            mine.start()
            locals_.append(mine)
            first = [copy(0, me, sibling, src=ins[a])]
            first += [copy(1 + j, me, (*chip, c), src=ins[a]) for j, chip in enumerate(chips)]
            for cp in first:
                cp.start()
            every.append((copy, first))
        sends = []
        for a in range(n_arr):
            copy, first = every[a]
            passed = [copy(4 + j, (*chip, c), sibling) for j, chip in enumerate(chips)]
            for j, chip in enumerate(chips):
                copy(1 + j, (*chip, c), me).wait_recv()
                passed[j].start()
            sends += first + passed
        for a in range(n_arr):
            copy, _ = every[a]
            copy(0, sibling, me).wait_recv()
            for j, chip in enumerate(chips):
                copy(4 + j, (*chip, 1 - c), me).wait_recv()
        for cp in sends:
            cp.wait_send()
        for mine in locals_:
            mine.wait()

    outs = pl.pallas_call(
        body, name=name,
        out_shape=[jax.ShapeDtypeStruct((N_DEV * a.shape[0], a.shape[1]), a.dtype) for a in arrays],
        in_specs=[pl.BlockSpec(memory_space=space)] * n_arr,
        out_specs=[pl.BlockSpec(memory_space=space)] * n_arr,
        scratch_shapes=[pltpu.SemaphoreType.DMA((n_arr, 7)), pltpu.SemaphoreType.DMA((n_arr, 7)),
                        pltpu.SemaphoreType.DMA((n_arr,))],
        compiler_params=_params(),
    )(*arrays)
    return list(outs)


_HBM = pl.BlockSpec(memory_space=pltpu.HBM)
_SEM = pl.BlockSpec(memory_space=pltpu.SEMAPHORE)
_DATAFLOW = pltpu.SideEffectType.DATAFLOW_SIDE_EFFECTING


PUSH_COPIES = {"scatter": 7, "gather": 7, "near": 4, "relay": 3}
HALF_PUSH = ("near", "gather")


def _push_copies(src_refs, land_refs, send_sems, recv_sems, mode):
    x, y, c = lax.axis_index("x"), lax.axis_index("y"), lax.axis_index("c")
    me, sibling = (x, y, c), (x, y, 1 - c)
    n_k = PUSH_COPIES[mode]
    out, back = [], []
    if mode == "relay":
        for k, chip in enumerate([(1 - x, y), (x, 1 - y), (1 - x, 1 - y)]):
            for a, land in enumerate(land_refs):
                sems = dict(send_sem=send_sems.at[n_k * a + k], recv_sem=recv_sems.at[n_k * a + k],
                            device_id=sibling, device_id_type=pl.DeviceIdType.MESH)
                mine = land.at[_lin((*chip, c))]
                out.append(pltpu.make_async_remote_copy(src_ref=mine, dst_ref=mine, **sems))
                back.append(pltpu.make_async_remote_copy(src_ref=mine, dst_ref=land.at[_lin((*chip, 1 - c))], **sems))
        return out, back
    for k, peer in enumerate(_peers(x, y, c)[:n_k]):
        for a, (src, land) in enumerate(zip(src_refs, land_refs)):
            sems = dict(send_sem=send_sems.at[n_k * a + k], recv_sem=recv_sems.at[n_k * a + k],
                        device_id=peer, device_id_type=pl.DeviceIdType.MESH)
            mine = src.at[_lin(peer)] if mode == "scatter" else src
            out.append(pltpu.make_async_remote_copy(src_ref=mine, dst_ref=land.at[_lin(me)], **sems))
            back.append(pltpu.make_async_remote_copy(src_ref=mine, dst_ref=land.at[_lin(peer)], **sems))
    return out, back


def _push_start(srcs, lands, mode, name, after=()):
    n_src, n = len(srcs), len(lands)
    n_buf = n_src + n
    n_in = n_buf + len(after)
    n_sem = PUSH_COPIES[mode] * n

    def body(*refs):
        send_sems, recv_sems = refs[n_in], refs[n_in + 1]
        out, _ = _push_copies(refs[:n_src], refs[n_src:n_buf], send_sems, recv_sems, mode)
        for cp in out:
            cp.start()
        token = refs[-1]
        token[...] = jnp.zeros_like(token)

    both = list(srcs) + list(lands)
    res = pl.pallas_call(
        body, name=name,
        out_shape=[pltpu.SemaphoreType.DMA((n_sem,)), pltpu.SemaphoreType.DMA((n_sem,))]
        + [pltpu.HBM(a.shape, a.dtype) for a in both] + [jax.ShapeDtypeStruct((8, 128), F32)],
        in_specs=[_HBM] * n_buf + [pl.BlockSpec(memory_space=pl.ANY)] * len(after),
        out_specs=[_SEM, _SEM] + [_HBM] * n_buf + [pl.BlockSpec(memory_space=pltpu.VMEM)],
        input_output_aliases={i: 2 + i for i in range(n_buf)},
        compiler_params=pltpu.CompilerParams(has_side_effects=_DATAFLOW),
    )(*[pltpu.with_memory_space_constraint(a, pltpu.HBM) for a in both], *after)
    return res[0], res[1], list(res[2:2 + n_src]), list(res[2 + n_src:2 + n_buf]), res[-1]


def _push_wait(send_sems, recv_sems, srcs, lands, mode, after, name):
    n_src, n = len(srcs), len(lands)
    n_buf = n_src + n

    def body(*refs):
        out, back = _push_copies(refs[:n_src], refs[n_src:n_buf], refs[n_buf], refs[n_buf + 1], mode)
        for cp in out:
            cp.wait_send()
        for cp in back:
            cp.wait_recv()

    both = list(srcs) + list(lands)
    res = pl.pallas_call(
        body, name=name,
        out_shape=[pltpu.HBM(a.shape, a.dtype) for a in both],
        in_specs=[_HBM] * n_buf + [_SEM, _SEM, pl.BlockSpec(memory_space=pl.ANY)],
        out_specs=[_HBM] * n_buf,
        input_output_aliases={i: i for i in range(n_buf)},
        compiler_params=pltpu.CompilerParams(has_side_effects=_DATAFLOW),
    )(*both, send_sems, recv_sems, after)
    return list(res[n_src:])


def _landing(own, me):
    zone = lax.empty((N_DEV,) + own.shape, own.dtype)
    return lax.dynamic_update_slice(zone, own[None], (me,) + (0,) * own.ndim)


def _mod_rows(c9, w_mod, b_sh, name):
    n = w_mod.shape[1]

    def body(c_ref, w_ref, b_ref, o_ref):
        s9 = _silu(c_ref[...]).astype(BF16)
        o_ref[...] = _dot(s9, w_ref[...].astype(BF16), NN) + b_ref[...]

    return pl.pallas_call(body, name=name, out_shape=jax.ShapeDtypeStruct((16, n), F32),
                          compiler_params=_params())(c9, w_mod, b_sh)


def _mod_grads(dm_rows, dc_rows, c9, w_mod, name):
    d, n = w_mod.shape

    def body(dm_ref, dc_ref, c_ref, w_ref, gw_ref, dc_out):
        dc = dc_ref[...]
        tot = dc[0:1]
        for j in range(1, N_DEV):
            tot = tot + dc[j:j + 1]
        row = lax.broadcasted_iota(jnp.int32, (8, n), 0)
        lower = jnp.where(row == 0, tot, 0.0)
        dmod9 = jnp.concatenate([dm_ref[...], lower], axis=0).astype(BF16)
        c9v = c_ref[...]
        s9 = _silu(c9v).astype(BF16)
        gw_ref[...] = _dot(s9, dmod9, TN)
        ds = _dot(lower.astype(BF16), w_ref[...].astype(BF16), NT)
        dc_out[...] = ds * _dsilu(c9v[8:16])

    return pl.pallas_call(body, name=name,
                          out_shape=[jax.ShapeDtypeStruct((d, n), F32), jax.ShapeDtypeStruct((8, d), F32)],
                          compiler_params=_params())(dm_rows, dc_rows, c9, w_mod)


def _decay_tables(dec, n_heads, name):
    c = CHUNK

    def body(dec_ref, dc_ref, dlf_ref, dlb_ref, qf_ref, kf_ref, qb_ref, kb_ref, cdf_ref, cdb_ref, lg_ref):
        h = pl.program_id(0)
        d = dec_ref[...]
        lane = lax.broadcasted_iota(jnp.int32, d.shape, 1)
        lg = -jnp.exp(jnp.sum(jnp.where(lane == h, d, 0.0), axis=1, keepdims=True))
        lgf, lgb = lg[0:1], lg[1:2]
        i = lax.broadcasted_iota(jnp.int32, (c, c), 0).astype(F32)
        j = lax.broadcasted_iota(jnp.int32, (c, c), 1).astype(F32)
        diff = i - j
        d_f = jnp.where(diff >= 0, jnp.exp(lgf * jnp.maximum(diff, 0.0)), 0.0)
        d_b = jnp.where(diff <= 0, jnp.exp(lgb * jnp.maximum(-diff, 0.0)), 0.0)
        dc_ref[...] = d_f + d_b
        dlf_ref[...] = diff * d_f
        dlb_ref[...] = -diff * d_b
        pos = lax.broadcasted_iota(jnp.int32, (c, HEAD_DIM), 0).astype(F32)
        qf_ref[...] = jnp.exp(lgf * (pos + 1.0))
        kf_ref[...] = jnp.exp(lgf * (c - 1.0 - pos))
        qb_ref[...] = jnp.exp(lgb * (c - pos))
        kb_ref[...] = jnp.exp(lgb * pos)
        ones = jnp.ones((8, HEAD_DIM), F32)
        cdf_ref[...] = jnp.exp(lgf * float(c)) * ones
        cdb_ref[...] = jnp.exp(lgb * float(c)) * ones

        @pl.when(h == 0)
        def _():
            lg_ref[...] = jnp.zeros_like(lg_ref)

        row8 = lax.broadcasted_iota(jnp.int32, (8, HEAD_DIM), 0)
        lane8 = lax.broadcasted_iota(jnp.int32, (8, HEAD_DIM), 1)
        lg_ref[...] += (jnp.where((row8 == 0) & (lane8 == h), lgf, 0.0)
                        + jnp.where((row8 == 1) & (lane8 == h), lgb, 0.0))

    def per_head(*tail):
        return pl.BlockSpec((None,) + tail, lambda h: (h,) + (0,) * len(tail))

    shapes = [(c, c)] * 3 + [(c, HEAD_DIM)] * 4 + [(8, HEAD_DIM)] * 2
    return pl.pallas_call(
        body, name=name, grid=(n_heads,),
        in_specs=[_full(dec.shape)],
        out_specs=[per_head(*s) for s in shapes] + [_full((8, HEAD_DIM))],
        out_shape=[jax.ShapeDtypeStruct((n_heads,) + s, F32) for s in shapes]
        + [jax.ShapeDtypeStruct((8, HEAD_DIM), F32)],
        compiler_params=_params(("arbitrary",)),
    )(dec)


def _modulate(x, nw, shift, scale):
    r = lax.rsqrt(jnp.mean(x * x, axis=-1, keepdims=True) + EPS)
    return ((x * r) * nw * (1.0 + scale) + shift).astype(BF16)


def _split_rows(nxb, nb):
    def specs(d, step=lambda i: i):
        lat = pl.BlockSpec((ROW_TILE, d), lambda i: (jnp.minimum(step(i), nxb - 1), 0))
        ctx = pl.BlockSpec((ROW_TILE, d), lambda i: (jnp.clip(step(i) - nxb, 0, nb - nxb - 1), 0))
        return lat, ctx
    return specs


def _prenorm_first(x, ctx, nw, mod, name, after=()):
    n_lat, d = x.shape
    t = n_lat + ctx.shape[0]
    nxb = n_lat // ROW_TILE

    def body(x_ref, c_ref, nw_ref, mod_ref, *rest):
        o_ref = rest[-1]
        m = mod_ref[...]
        nw_v = nw_ref[...]

        @pl.when(pl.program_id(0) < nxb)
        def _():
            o_ref[...] = _modulate(x_ref[...], nw_v, m[0:1], m[1:2])

        @pl.when(pl.program_id(0) >= nxb)
        def _():
            o_ref[...] = _modulate(c_ref[...], nw_v, m[3:4], m[4:5])

    lat, cx = _split_rows(nxb, t // ROW_TILE)(d)
    return pl.pallas_call(
        body, name=name, grid=(t // ROW_TILE,),
        in_specs=[lat, cx, _full((1, d)), _full((8, d))] + [pl.BlockSpec(memory_space=pl.ANY)] * len(after),
        out_specs=pl.BlockSpec((ROW_TILE, d), lambda i: (i, 0)), out_shape=jax.ShapeDtypeStruct((t, d), BF16),
        compiler_params=_params(("parallel",)))(x, ctx, nw, mod, *after)


def _rope_fwd(v, cos, sa, sb):
    return v * cos + pltpu.roll(v, 96, 1) * sa + pltpu.roll(v, 32, 1) * sb


def _rope_bwd(g, cos, sa, sb):
    return g * cos + pltpu.roll(g * sa, 32, 1) + pltpu.roll(g * sb, 96, 1)


N_PLAIN = 5
U_DTYPE = BF16


def _in_proj(hx, wg, cos, sa, sb, s, part, tm, name, after=(), into=None):
    t, d = hx.shape
    n_seg, _, n = wg.shape
    nb = t // tm
    k_scale = HEAD_DIM ** -0.5
    kept = [] if into is None else list(into)

    def body(a_ref, w_ref, cos_ref, sa_ref, sb_ref, *rest):
        u_ref, qkv_ref = rest[-2:]
        g = pl.program_id(1)
        acc = _dot(a_ref[...], w_ref[...], NN)

        @pl.when(g < N_PLAIN)
        def _():
            u_ref[...] = acc.astype(U_DTYPE)

        @pl.when(g == N_PLAIN + 2)
        def _():
            qkv_ref[...] = acc.astype(BF16)

        for which, scale in ((N_PLAIN, 1.0), (N_PLAIN + 1, k_scale)):
            @pl.when(g == which)
            def _(scale=scale):
                co, a, b = cos_ref[...], sa_ref[...], sb_ref[...]
                for h in range(n // HEAD_DIM):
                    sl = slice(h * HEAD_DIM, (h + 1) * HEAD_DIM)
                    qkv_ref[:, sl] = (_rope_fwd(acc[:, sl], co, a, b) * scale).astype(BF16)

    def w_seg(g):
        return jnp.where(g < N_PLAIN - 1, g, jnp.where(g == N_PLAIN - 1, n_seg - 1, g - 1))

    def qkv_at(i, g):
        held = (jnp.where(i == 0, 0, 2), jnp.maximum(i - 1, 0))
        return (jnp.where(g < N_PLAIN, held[0], g - N_PLAIN), jnp.where(g < N_PLAIN, held[1], i), part)

    tab = pl.BlockSpec((tm, HEAD_DIM), lambda i, g: (i, 0))
    hbm = pl.BlockSpec(memory_space=pl.ANY)
    return pl.pallas_call(
        body, name=name, grid=(nb, n_seg),
        in_specs=[pl.BlockSpec((tm, d), lambda i, g: (i, 0)), pl.BlockSpec((None, d, n), lambda i, g: (w_seg(g), 0, 0)),
                  tab, tab, tab] + [hbm] * (len(after) + len(kept)),
        out_specs=[pl.BlockSpec((None, tm, n), lambda i, g: (jnp.minimum(g, N_PLAIN - 1), i, part)),
                   pl.BlockSpec((None, tm, n), qkv_at)],
        out_shape=[jax.ShapeDtypeStruct((N_PLAIN, t, s), U_DTYPE), jax.ShapeDtypeStruct((3, t, s), BF16)],
        input_output_aliases={5 + len(after) + j: j for j in range(len(kept))},
        compiler_params=_params(("arbitrary", "arbitrary")))(hx, wg, cos, sa, sb, *after, *kept)


def _pair_sweep(xs, ys, tab_f, tab_b, cdf, cdb, n_heads, nx, ncc, reverse, name):
    t, s = xs[0].shape[-2:]
    nc = nx + ncc
    c = CHUNK
    n_pair = nc // 2
    assert nx % 2 == 0 and ncc % 2 == 0

    def f_pair(i):
        step = n_pair - 1 - i if reverse else i
        return jnp.where(step < ncc // 2, nx // 2 + step, step - ncc // 2)

    def b_pair(i):
        return i if reverse else n_pair - 1 - i

    f_subs = (1, 0) if reverse else (0, 1)
    b_subs = (0, 1) if reverse else (1, 0)

    def body(xf_ref, yf_ref, xb_ref, yb_ref, tf, tb, cdf_ref, cdb_ref, sf_out, sb_out, sf, sb):
        @pl.when(pl.program_id(0) == 0)
        def _():
            sf[...] = jnp.zeros_like(sf)
            sb[...] = jnp.zeros_like(sb)

        for step in range(2):
            for x_ref, y_ref, tab, cd, out, st, sub in ((xf_ref, yf_ref, tf, cdf_ref, sf_out, sf, f_subs[step]),
                                                        (xb_ref, yb_ref, tb, cdb_ref, sb_out, sb, b_subs[step])):
                rows = pl.ds(sub * c, c)
                for h in range(n_heads):
                    sl = pl.ds(h * HEAD_DIM, HEAD_DIM)
                    out[sub, h] = st[h].astype(BF16)
                    xd = (x_ref[rows, sl].astype(F32) * tab[h]).astype(BF16)
                    st[h] = cd[h][0:1, :] * st[h] + _dot(xd, y_ref[rows, sl], TN)

    def spec(arr, pair):
        lead = arr[1]
        if lead is None:
            return pl.BlockSpec((2 * c, s), lambda i: (pair(i), 0))
        return pl.BlockSpec((None, 2 * c, s), lambda i: (lead, pair(i), 0))

    st_blk = (2, n_heads, HEAD_DIM, HEAD_DIM)
    return pl.pallas_call(
        body, name=name, grid=(n_pair,),
        in_specs=[spec(xs, f_pair), spec(ys, f_pair), spec(xs, b_pair), spec(ys, b_pair),
                  _full((n_heads, c, HEAD_DIM)), _full((n_heads, c, HEAD_DIM)),
                  _full((n_heads, 8, HEAD_DIM)), _full((n_heads, 8, HEAD_DIM))],
        out_specs=[pl.BlockSpec(st_blk, lambda i: (f_pair(i), 0, 0, 0)), pl.BlockSpec(st_blk, lambda i: (b_pair(i), 0, 0, 0))],
        out_shape=[jax.ShapeDtypeStruct((nc, n_heads, HEAD_DIM, HEAD_DIM), BF16)] * 2,
        scratch_shapes=[pltpu.VMEM((n_heads, HEAD_DIM, HEAD_DIM), F32)] * 2,
        compiler_params=_params(("arbitrary",)),
    )(xs[0], ys[0], xs[0], ys[0], tab_f, tab_b, cdf, cdb)


def _state_sweep(qkv, tabs, n_heads, nx, ncc, name):
    return _pair_sweep((qkv, 1), (qkv, 2), tabs["kf"], tabs["kb"], tabs["cdf"], tabs["cdb"], n_heads, nx, ncc, False, name)


MIX_CHUNKS = 2
MIX_ROWS = MIX_CHUNKS * CHUNK


HALO = 16


def _halo_specs(s, t):
    per = MIX_ROWS // HALO
    n_halo = t // HALO

    def prev(g):
        return pl.BlockSpec((None, HALO, s), lambda i: (g, jnp.maximum(i * per - 1, 0), 0))

    def nxt(g):
        return pl.BlockSpec((None, HALO, s), lambda i: (g, jnp.minimum((i + 1) * per, n_halo - 1), 0))

    return prev, nxt


def _conv_input(h_ref, c_ref, hp_ref, hn_ref, cp_ref, cn_ref):
    a = c_ref[...].astype(F32) * h_ref[...].astype(F32)
    before = cp_ref[HALO - 1:HALO].astype(F32) * hp_ref[HALO - 1:HALO].astype(F32)
    after = cn_ref[0:1].astype(F32) * hn_ref[0:1].astype(F32)
    return a, before, after


def _shifted(a, before, after, has_prev, has_next):
    rows = a.shape[0]
    rowi = lax.broadcasted_iota(jnp.int32, a.shape, 0)
    am = jnp.where(rowi == 0, jnp.where(has_prev, before, 0.0), pltpu.roll(a, 1, 0))
    ap = jnp.where(rowi == rows - 1, jnp.where(has_next, after, 0.0), pltpu.roll(a, rows - 1, 0))
    return am, ap


def _neighbours(i, nx, nc):
    nxb, ncb = nx // MIX_CHUNKS, nc // MIX_CHUNKS
    return (i != 0) & (i != nxb), (i != nxb - 1) & (i != ncb - 1)


def _mix_fwd(u, qkv, sf, sb, tabs, conv_w, cnw, gnw, n_heads, nx, ncc, name):
    _, t, s = u.shape
    nc = nx + ncc
    c = CHUNK
    assert nx % MIX_CHUNKS == 0 and ncc % MIX_CHUNKS == 0

    def body(h_ref, b_ref, c_ref, z_ref, rz_ref, hp_ref, hn_ref, cp_ref, cn_ref, q_ref, k_ref, v_ref,
             sf_ref, sb_ref, dc_ref, qft, qbt, w_ref, cnw_ref, gnw_ref, y_ref, o_ref):
        i = pl.program_id(0)
        has_prev, has_next = _neighbours(i, nx, nc)
        a, before, after = _conv_input(h_ref, c_ref, hp_ref, hn_ref, cp_ref, cn_ref)
        am, ap = _shifted(a, before, after, has_prev, has_next)
        w = w_ref[...]
        y0 = w[0:1] * am + w[1:2] * a + w[2:3] * ap
        yb = b_ref[...].astype(F32) * y0
        r = lax.rsqrt(jnp.mean(yb * yb, axis=-1, keepdims=True) + EPS)
        y_ref[:, pl.ds(0, s)] = (_silu(z_ref[...].astype(F32)) * ((yb * r) * cnw_ref[...])).astype(BF16)
        for sub in range(MIX_CHUNKS):
            rows = pl.ds(sub * c, c)
            for h in range(n_heads):
                sl = pl.ds(h * HEAD_DIM, HEAD_DIM)
                q, k, v = q_ref[rows, sl], k_ref[rows, sl], v_ref[rows, sl]
                p = (_dot(q, k, NT) * dc_ref[h]).astype(BF16)
                o = _dot(p, v, NN)
                qf = q.astype(F32)
                o += _dot((qf * qft[h]).astype(BF16), sf_ref[sub, h], NN)
                o += _dot((qf * qbt[h]).astype(BF16), sb_ref[sub, h], NN)
                o_ref[rows, sl] = o
                mu = jnp.mean(o, axis=-1, keepdims=True)
                var = jnp.mean(jnp.square(o - mu), axis=-1, keepdims=True)
                on = (o - mu) * lax.rsqrt(var + EPS)
                y_ref[rows, pl.ds(s + h * HEAD_DIM, HEAD_DIM)] = (
                    _silu(rz_ref[rows, sl].astype(F32)) * (on * gnw_ref[:, sl])).astype(BF16)

    def seg(g):
        return pl.BlockSpec((None, MIX_ROWS, s), lambda i: (g, i, 0))

    prev, nxt = _halo_specs(s, t)
    row = pl.BlockSpec((MIX_ROWS, s), lambda i: (i, 0))
    st = pl.BlockSpec((MIX_CHUNKS, n_heads, HEAD_DIM, HEAD_DIM), lambda i: (i, 0, 0, 0))
    return pl.pallas_call(
        body, name=name, grid=(nc // MIX_CHUNKS,),
        in_specs=[seg(0), seg(1), seg(2), seg(3), seg(4), prev(0), nxt(0), prev(2), nxt(2), seg(0), seg(1), seg(2),
                  st, st, _full((n_heads, c, c)), _full((n_heads, c, HEAD_DIM)), _full((n_heads, c, HEAD_DIM)),
                  _full((3, s)), _full((1, s)), _full((1, s))],
        out_specs=[pl.BlockSpec((MIX_ROWS, 2 * s), lambda i: (i, 0)), row],
        out_shape=[jax.ShapeDtypeStruct((t, 2 * s), BF16), jax.ShapeDtypeStruct((t, s), F32)],
        compiler_params=_params(("parallel",)),
    )(u, u, u, u, u, u, u, u, u, qkv, qkv, qkv, sf, sb, tabs["dc"], tabs["qf"], tabs["qb"], conv_w, cnw, gnw)


def _out_proj_prenorm(ycat, w_out, res, mod, nw_next, mod_next, n_lat, name):
    t, d = ycat.shape
    nb = t // ROW_TILE
    nxb = n_lat // ROW_TILE
    split = len(res) == 2

    def body(a_ref, w_ref, *rest):
        res_refs = rest[:len(res)]
        mod_ref, nw_ref, modn_ref, m_ref, xo_ref, hx_ref, xs = rest[len(res):]
        i = pl.program_id(0)

        @pl.when(i == 0)
        def _():
            xs[...] = jnp.zeros_like(xs)

        cur_ctx = jnp.minimum(i, nb - 1) >= nxb
        prev_ctx = i - 1 >= nxb

        def step(cur, prev):
            mv, mn = mod_ref[...], modn_ref[...]
            shift = jnp.where(prev_ctx, mn[3:4], mn[0:1])
            scale = jnp.where(prev_ctx, mn[4:5], mn[1:2])
            hx_ref[...] = _modulate(xs[prev], nw_ref[...], shift, scale)
            m = _dot(a_ref[...], w_ref[...], NN)
            x_res = jnp.where(cur_ctx, res_refs[1][...], res_refs[0][...]) if split else res_refs[0][...]
            x_new = x_res + jnp.where(cur_ctx, mv[5:6], mv[2:3]) * m
            m_ref[...] = m.astype(BF16)
            xo_ref[...] = x_new
            xs[cur] = x_new

        @pl.when(i % 2 == 0)
        def _():
            step(0, 1)

        @pl.when(i % 2 == 1)
        def _():
            step(1, 0)

    cur = pl.BlockSpec((ROW_TILE, d), lambda i: (jnp.minimum(i, nb - 1), 0))
    prev = pl.BlockSpec((ROW_TILE, d), lambda i: (jnp.maximum(i - 1, 0), 0))
    res_specs = list(_split_rows(nxb, nb)(d, lambda i: jnp.minimum(i, nb - 1))) if split else [cur]
    return pl.pallas_call(
        body, name=name, grid=(nb + 1,),
        in_specs=[cur, _full((d, d))] + res_specs + [_full((8, d)), _full((1, d)), _full((8, d))],
        out_specs=[cur, cur, prev],
        out_shape=[jax.ShapeDtypeStruct((t, d), BF16), jax.ShapeDtypeStruct((t, d), F32),
                   jax.ShapeDtypeStruct((t, d), BF16)],
        scratch_shapes=[pltpu.VMEM((2, ROW_TILE, d), F32)],
        compiler_params=_params(("arbitrary",)))(ycat, w_out, *res, mod, nw_next, mod_next)


def _out_proj_loss(ycat, w_out, xt, mod, tgt, fnw, n_lat, name):
    t, d = xt.shape
    nb = t // ROW_TILE
    nxb = n_lat // ROW_TILE

    def body(a_ref, w_ref, x_ref, mod_ref, t_ref, fw_ref, dx_ref, dm_ref, loss_ref, dw_ref, gacc_ref, xs, ms):
        i = pl.program_id(0)

        @pl.when(i == 0)
        def _():
            xs[...] = jnp.zeros_like(xs)
            ms[...] = jnp.zeros_like(ms)
            loss_ref[...] = jnp.zeros_like(loss_ref)
            dw_ref[...] = jnp.zeros_like(dw_ref)
            gacc_ref[...] = jnp.zeros_like(gacc_ref)

        def step(cur, prev):
            mv = mod_ref[...]
            x_prev, m_prev = xs[prev], ms[prev]
            valid = (i >= 1) & (i - 1 < nxb)
            w = fw_ref[...]
            r = lax.rsqrt(jnp.mean(x_prev * x_prev, axis=-1, keepdims=True) + EPS)
            xn = x_prev * r
            e = xn * w - t_ref[...]
            loss = 0.5 * jnp.sum(jnp.mean(e * e, axis=-1, keepdims=True), axis=0, keepdims=True)
            loss_ref[...] += jnp.where(valid, loss, 0.0)
            dy = e * (1.0 / d)
            dw_ref[0:1, :] += jnp.where(valid, jnp.sum(dy * xn, axis=0, keepdims=True), 0.0)
            dxn = dy * w
            dx = jnp.where(valid, r * (dxn - xn * jnp.mean(dxn * xn, axis=-1, keepdims=True)), 0.0)
            dx_ref[...] = dx
            dm_ref[...] = (dx * mv[2:3]).astype(BF16)
            gacc_ref[2:3, :] += jnp.sum(dx * m_prev, axis=0, keepdims=True)

            m = _dot(a_ref[...], w_ref[...], NN)
            gate = jnp.where(jnp.minimum(i, nb - 1) >= nxb, mv[5:6], mv[2:3])
            xs[cur] = x_ref[...] + gate * m
            ms[cur] = m

        @pl.when(i % 2 == 0)
        def _():
            step(0, 1)

        @pl.when(i % 2 == 1)
        def _():
            step(1, 0)

    cur = pl.BlockSpec((ROW_TILE, d), lambda i: (jnp.minimum(i, nb - 1), 0))
    prev = pl.BlockSpec((ROW_TILE, d), lambda i: (jnp.maximum(i - 1, 0), 0))
    return pl.pallas_call(
        body, name=name, grid=(nb + 1,),
        in_specs=[cur, _full((d, d)), cur, _full((8, d)),
                  pl.BlockSpec((ROW_TILE, d), lambda i: (jnp.clip(i - 1, 0, nxb - 1), 0)), _full((1, d))],
        out_specs=[prev, prev, _full((8, HEAD_DIM)), _full((8, d)), _full((8, d))],
        out_shape=[jax.ShapeDtypeStruct((t, d), F32), jax.ShapeDtypeStruct((t, d), BF16),
                   jax.ShapeDtypeStruct((8, HEAD_DIM), F32), jax.ShapeDtypeStruct((8, d), F32),
                   jax.ShapeDtypeStruct((8, d), F32)],
        scratch_shapes=[pltpu.VMEM((2, ROW_TILE, d), F32), pltpu.VMEM((2, ROW_TILE, d), F32)],
        compiler_params=_params(("arbitrary",)))(ycat, w_out, xt, mod, tgt, fnw)


def _matmul_nt(a, w, tn, name, after=()):
    t, k = a.shape
    n = w.shape[0]
    tm = _mm_rows(t)

    def body(a_ref, w_ref, *rest):
        rest[-1][...] = _dot(a_ref[...], w_ref[...], NT)

    return pl.pallas_call(
        body, name=name, grid=(n // tn, t // tm),
        in_specs=[pl.BlockSpec((tm, k), lambda j, i: (i, 0)), pl.BlockSpec((tn, k), lambda j, i: (j, 0))]
        + [pl.BlockSpec(memory_space=pl.ANY)] * len(after),
        out_specs=pl.BlockSpec((tm, tn), lambda j, i: (i, j)),
        out_shape=jax.ShapeDtypeStruct((t, n), F32),
        compiler_params=_params(("parallel", "parallel")))(a, w, *after)


def _weight_grad(a, b, bm, bt, name):
    t, m = a.shape
    n_g, _, n = b.shape
    nt = t // bt

    def body(a_ref, b_ref, o_ref, acc):
        k = pl.program_id(2)

        @pl.when(k == 0)
        def _():
            acc[...] = jnp.zeros_like(acc)

        acc[...] += _dot(a_ref[...], b_ref[...], TN)

        @pl.when(k == nt - 1)
        def _():
            o_ref[...] = acc[...].astype(o_ref.dtype)

    return pl.pallas_call(
        body, name=name, grid=(n_g, m // bm, nt),
        in_specs=[pl.BlockSpec((bt, bm), lambda g, i, k: (k, i)), pl.BlockSpec((None, bt, n), lambda g, i, k: (g, k, 0))],
        out_specs=pl.BlockSpec((None, bm, n), lambda g, i, k: (g, i, 0)),
        out_shape=jax.ShapeDtypeStruct((n_g, m, n), BF16),
        scratch_shapes=[pltpu.VMEM((bm, n), F32)],
        compiler_params=_params(("parallel", "parallel", "arbitrary")))(a, b)


def _weight_grad_beside_prenorm_bwd(a, b, dhx, xt, dxo, nw, mod, below, n_lat, name):
    t, m = a.shape
    n_g, _, n = b.shape
    d = xt.shape[1]
    bt = _mm_rows(t)
    nt = t // bt
    rows = t // (n_g * nt)
    n_piece = 2 if rows % 32 == 0 and m % 2 == 0 else 1
    rows_p, m_p = rows // n_piece, m // n_piece
    assert rows * n_g * nt == t and rows_p % 8 == 0

    def body(a_ref, b_ref, dh_ref, x_ref, dxo_ref, nw_ref, mod_ref, m_ref, modb_ref,
             o_ref, dx_ref, acc_ref, dm_ref, gacc_ref, acc):
        g, k = pl.program_id(0), pl.program_id(1)
        step = g * nt + k

        @pl.when(step == 0)
        def _():
            acc_ref[...] = jnp.zeros_like(acc_ref)
            gacc_ref[...] = jnp.zeros_like(gacc_ref)

        @pl.when(k == 0)
        def _():
            acc[...] = jnp.zeros_like(acc)

        mv, mb, nw_v = mod_ref[...], modb_ref[...], nw_ref[...]
        for p in range(n_piece):
            rs = pl.ds(p * rows_p, rows_p)
            rowi = step * rows + p * rows_p + lax.broadcasted_iota(jnp.int32, (rows_p, 1), 0)
            ctx = rowi >= n_lat
            w_lat = jnp.where(ctx, 0.0, 1.0)
            w_ctx = 1.0 - w_lat
            scale1 = 1.0 + jnp.where(ctx, mv[4:5], mv[1:2])
            x = x_ref[rs, :]
            r = lax.rsqrt(jnp.mean(x * x, axis=-1, keepdims=True) + EPS)
            xn = x * r
            dh = dh_ref[rs, :]
            dsc = dh * (xn * nw_v)
            acc_ref[0:1, :] += jnp.sum(dh * w_lat, axis=0, keepdims=True)
            acc_ref[1:2, :] += jnp.sum(dsc * w_lat, axis=0, keepdims=True)
            acc_ref[3:4, :] += jnp.sum(dh * w_ctx, axis=0, keepdims=True)
            acc_ref[4:5, :] += jnp.sum(dsc * w_ctx, axis=0, keepdims=True)
            acc_ref[6:7, :] += jnp.sum(dh * scale1 * xn, axis=0, keepdims=True)
            dxn = dh * (nw_v * scale1)
            dx = dxo_ref[rs, :] + r * (dxn - xn * jnp.mean(dxn * xn, axis=-1, keepdims=True))
            dx_ref[rs, :] = dx
            dm_ref[rs, :] = (dx * jnp.where(ctx, mb[5:6], mb[2:3])).astype(BF16)
            dg = dx * m_ref[rs, :].astype(F32)
            gacc_ref[2:3, :] += jnp.sum(dg * w_lat, axis=0, keepdims=True)
            gacc_ref[5:6, :] += jnp.sum(dg * w_ctx, axis=0, keepdims=True)

            ms_ = pl.ds(p * m_p, m_p)
            acc[ms_, :] += _dot(a_ref[:, ms_], b_ref[...], TN)

        @pl.when(k == nt - 1)
        def _():
            o_ref[...] = acc[...].astype(o_ref.dtype)

    side = pl.BlockSpec((rows, d), lambda g, k: (g * nt + k, 0))
    acc8 = _full((8, d))
    return pl.pallas_call(
        body, name=name, grid=(n_g, nt),
        in_specs=[pl.BlockSpec((bt, m), lambda g, k: (k, 0)), pl.BlockSpec((None, bt, n), lambda g, k: (g, k, 0)),
                  side, side, side, _full((1, d)), acc8, side, acc8],
        out_specs=[pl.BlockSpec((None, m, n), lambda g, k: (g, 0, 0)), side, acc8, side, acc8],
        out_shape=[jax.ShapeDtypeStruct((n_g, m, n), BF16), jax.ShapeDtypeStruct((t, d), F32),
                   jax.ShapeDtypeStruct((8, d), F32), jax.ShapeDtypeStruct((t, d), BF16),
                   jax.ShapeDtypeStruct((8, d), F32)],
        scratch_shapes=[pltpu.VMEM((m, n), F32)],
        compiler_params=_params(("arbitrary", "arbitrary")))(a, b, dhx, xt, dxo, nw, mod, *below)


def _mix_bwd_a(dycat, u, o, conv_w, cnw, gnw, n_heads, nx, ncc, name):
    _, t, s = u.shape
    nc = nx + ncc

    def body(dy_ref, h_ref, b_ref, c_ref, z_ref, rz_ref, hp_ref, hn_ref, cp_ref, cn_ref, o_ref, w_ref,
             cnw_ref, gnw_ref, g_ref, dz_ref, db_ref, drz_ref, do_ref, acc_ref):
        i = pl.program_id(0)

        @pl.when(i == 0)
        def _():
            acc_ref[...] = jnp.zeros_like(acc_ref)

        has_prev, has_next = _neighbours(i, nx, nc)
        a, before, after = _conv_input(h_ref, c_ref, hp_ref, hn_ref, cp_ref, cn_ref)
        am, ap = _shifted(a, before, after, has_prev, has_next)
        w = w_ref[...]
        y0 = w[0:1] * am + w[1:2] * a + w[2:3] * ap
        bb = b_ref[...].astype(F32)
        yb = bb * y0
        r = lax.rsqrt(jnp.mean(yb * yb, axis=-1, keepdims=True) + EPS)
        ynn = yb * r
        z = z_ref[...].astype(F32)
        dyc = dy_ref[:, pl.ds(0, s)]
        cw = cnw_ref[...]
        sz, dsz = _silu_and_slope(z)
        dz_ref[...] = (dyc * (ynn * cw) * dsz).astype(BF16)
        dyn = dyc * sz
        acc_ref[0:1, :] += jnp.sum(dyn * ynn, axis=0, keepdims=True)
        dynn = dyn * cw
        dyb = r * (dynn - ynn * jnp.mean(dynn * ynn, axis=-1, keepdims=True))
        db_ref[...] = (dyb * y0).astype(BF16)
        g_ref[...] = dyb * bb
        for h in range(n_heads):
            sl = pl.ds(h * HEAD_DIM, HEAD_DIM)
            ov = o_ref[:, sl]
            mu = jnp.mean(ov, axis=-1, keepdims=True)
            var = jnp.mean(jnp.square(ov - mu), axis=-1, keepdims=True)
            rs = lax.rsqrt(var + EPS)
            on = (ov - mu) * rs
            dyr = dy_ref[:, pl.ds(s + h * HEAD_DIM, HEAD_DIM)]
            rz = rz_ref[:, sl].astype(F32)
            gw = gnw_ref[:, sl]
            srz, dsrz = _silu_and_slope(rz)
            drz_ref[:, sl] = (dyr * (on * gw) * dsrz).astype(BF16)
            dyg = dyr * srz
            acc_ref[1:2, sl] += jnp.sum(dyg * on, axis=0, keepdims=True)
            don = dyg * gw
            do = rs * (don - jnp.mean(don, axis=-1, keepdims=True)
                       - on * jnp.mean(don * on, axis=-1, keepdims=True))
            do_ref[:, sl] = do.astype(BF16)

    def seg(g):
        return pl.BlockSpec((None, MIX_ROWS, s), lambda i: (g, i, 0))

    prev, nxt = _halo_specs(s, t)
    row = pl.BlockSpec((MIX_ROWS, s), lambda i: (i, 0))
    return pl.pallas_call(
        body, name=name, grid=(nc // MIX_CHUNKS,),
        in_specs=[pl.BlockSpec((MIX_ROWS, 2 * s), lambda i: (i, 0)), seg(0), seg(1), seg(2), seg(3), seg(4),
                  prev(0), nxt(0), prev(2), nxt(2), row, _full((3, s)), _full((1, s)), _full((1, s))],
        out_specs=[row, row, row, row, row, _full((8, s))],
        out_shape=[jax.ShapeDtypeStruct((t, s), F32)] + [jax.ShapeDtypeStruct((t, s), BF16)] * 4
        + [jax.ShapeDtypeStruct((8, s), F32)],
        compiler_params=_params(("arbitrary",)),
    )(dycat, u, u, u, u, u, u, u, u, u, o, conv_w, cnw, gnw)


def _grad_state_sweep(qkv, do, tabs, n_heads, nx, ncc, name):
    return _pair_sweep((qkv, 0), (do, None), tabs["qf"], tabs["qb"], tabs["cdf"], tabs["cdb"], n_heads, nx, ncc, True, name)


def _mix_bwd_b(u, g, dz, db, drz, qkv, do, sf, sb, gf, gb, tabs, cos, sa, sb_tab, conv_w,
               n_heads, nx, ncc, name):
    _, t, s = u.shape
    nc = nx + ncc
    c = CHUNK
    k_scale = HEAD_DIM ** -0.5

    def body(h_ref, c_ref, g_ref, gp_ref, gn_ref, dz_ref, db_ref, drz_ref, q_ref, k_ref, v_ref, do_ref,
             sf_ref, sb_ref, gf_ref, gb_ref, dc_t, dlf_t, dlb_t, qft, kft, qbt, kbt, cdf, cdb, lg_ref,
             cos_ref, sa_ref, sb_ref2, w_ref, du_ref, dw_ref, dlg_ref):
        i = pl.program_id(0)

        @pl.when(i == 0)
        def _():
            dw_ref[...] = jnp.zeros_like(dw_ref)
            dlg_ref[...] = jnp.zeros_like(dlg_ref)

        has_prev, has_next = _neighbours(i, nx, nc)
        gv = g_ref[...]
        gm, gp = _shifted(gv, gp_ref[7:8], gn_ref[0:1], has_prev, has_next)
        w = w_ref[...]
        da = w[0:1] * gp + w[1:2] * gv + w[2:3] * gm
        hh, cc = h_ref[...].astype(F32), c_ref[...].astype(F32)
        du_ref[0] = (da * cc).astype(BF16)
        du_ref[2] = (da * hh).astype(BF16)
        a = cc * hh
        dw_ref[0:1, :] += jnp.sum(a * gp, axis=0, keepdims=True)
        dw_ref[1:2, :] += jnp.sum(a * gv, axis=0, keepdims=True)
        dw_ref[2:3, :] += jnp.sum(a * gm, axis=0, keepdims=True)
        du_ref[1] = db_ref[...]
        du_ref[3] = dz_ref[...]
        du_ref[7] = drz_ref[...]

        pos = lax.broadcasted_iota(jnp.int32, (c, HEAD_DIM), 0).astype(F32)
        w_q_f, w_q_b, w_k_f = pos + 1.0, c - pos, c - 1.0 - pos
        row8 = lax.broadcasted_iota(jnp.int32, (8, HEAD_DIM), 0)
        lane8 = lax.broadcasted_iota(jnp.int32, (8, HEAD_DIM), 1)
        dlg = jnp.zeros((8, HEAD_DIM), F32)
        for sub, h in [(sub, h) for sub in range(MIX_CHUNKS) for h in range(n_heads)]:
            rows = pl.ds(sub * c, c)
            co, ra, rb = cos_ref[rows, :], sa_ref[rows, :], sb_ref2[rows, :]
            sl = pl.ds(h * HEAD_DIM, HEAD_DIM)
            q, k, v, do = q_ref[rows, sl], k_ref[rows, sl], v_ref[rows, sl], do_ref[rows, sl]
            qf, kf, dof = q.astype(F32), k.astype(F32), do.astype(F32)
            s_f, s_b, g_f, g_b = sf_ref[sub, h], sb_ref[sub, h], gf_ref[sub, h], gb_ref[sub, h]
            p = _dot(q, k, NT)
            pd = _dot(do, v, NT)
            pdd = (pd * dc_t[h]).astype(BF16)
            dq = _dot(pdd, k, NN)
            dk = _dot(pdd, q, TN)
            dv = _dot((p * dc_t[h]).astype(BF16), do, TN)
            dq_f = _dot((dof * qft[h]).astype(BF16), s_f, NT)
            dq_b = _dot((dof * qbt[h]).astype(BF16), s_b, NT)
            dk_f = _dot(v, g_f, NT) * kft[h]
            dk_b = _dot(v, g_b, NT) * kbt[h]
            dv += _dot((kf * kft[h]).astype(BF16), g_f, NN) + _dot((kf * kbt[h]).astype(BF16), g_b, NN)
            ppd = p * pd
            cd_f, cd_b = cdf[h][0:1, :], cdb[h][0:1, :]
            t_f = _sum_all(dlf_t[h] * ppd + w_q_f * qf * dq_f + w_k_f * kf * dk_f
                           + float(c) * (cd_f * (g_f.astype(F32) * s_f.astype(F32))))
            t_b = _sum_all(dlb_t[h] * ppd + w_q_b * qf * dq_b + pos * kf * dk_b
                           + float(c) * (cd_b * (g_b.astype(F32) * s_b.astype(F32))))
            dlg += jnp.where((row8 == 0) & (lane8 == h), t_f, 0.0) + jnp.where((row8 == 1) & (lane8 == h), t_b, 0.0)
            du_ref[4, rows, sl] = _rope_bwd(dq + dq_f + dq_b, co, ra, rb).astype(BF16)
            du_ref[5, rows, sl] = (_rope_bwd(dk + dk_f + dk_b, co, ra, rb) * k_scale).astype(BF16)
            du_ref[6, rows, sl] = dv.astype(BF16)
        dlg_ref[...] += dlg

        @pl.when(i == nc // MIX_CHUNKS - 1)
        def _():
            dlg_ref[...] = dlg_ref[...] * lg_ref[...]

    def seg(gi):
        return pl.BlockSpec((None, MIX_ROWS, s), lambda i: (gi, i, 0))

    per = MIX_ROWS // 8
    n8 = t // 8
    row = pl.BlockSpec((MIX_ROWS, s), lambda i: (i, 0))
    st = pl.BlockSpec((MIX_CHUNKS, n_heads, HEAD_DIM, HEAD_DIM), lambda i: (i, 0, 0, 0))
    tab = pl.BlockSpec((MIX_ROWS, HEAD_DIM), lambda i: (i, 0))
    hc = _full((n_heads, c, HEAD_DIM))
    cc_ = _full((n_heads, c, c))
    h8 = _full((n_heads, 8, HEAD_DIM))
    return pl.pallas_call(
        body, name=name, grid=(nc // MIX_CHUNKS,),
        in_specs=[seg(0), seg(2), row,
                  pl.BlockSpec((8, s), lambda i: (jnp.maximum(i * per - 1, 0), 0)),
                  pl.BlockSpec((8, s), lambda i: (jnp.minimum((i + 1) * per, n8 - 1), 0)),
                  row, row, row, seg(0), seg(1), seg(2), row, st, st, st, st, cc_, cc_, cc_, hc, hc, hc, hc, h8, h8,
                  _full((8, HEAD_DIM)), tab, tab, tab, _full((3, s))],
        out_specs=[pl.BlockSpec((8, MIX_ROWS, s), lambda i: (0, i, 0)), _full((8, s)), _full((8, HEAD_DIM))],
        out_shape=[jax.ShapeDtypeStruct((8, t, s), BF16), jax.ShapeDtypeStruct((8, s), F32),
                   jax.ShapeDtypeStruct((8, HEAD_DIM), F32)],
        compiler_params=_params(("arbitrary",)),
    )(u, u, g, g, g, dz, db, drz, qkv, qkv, qkv, do, sf, sb, gf, gb, tabs["dc"], tabs["dlf"], tabs["dlb"],
      tabs["qf"], tabs["kf"], tabs["qb"], tabs["kb"], tabs["cdf"], tabs["cdb"], tabs["lg"], cos, sa, sb_tab, conv_w)


def _in_proj_bwd(du, wgs, tm, gs, name, after=()):
    n_seg, t, s = du.shape
    d = wgs[0].shape[1]
    n_w = len(wgs)
    widths = [w.shape[2] for w in wgs]
    assert sum(widths) == s

    def body(a_ref, *rest):
        w_refs, o_ref = rest[:n_w], rest[-1]
        g = pl.program_id(1)
        part = None
        for j in range(gs):
            col = 0
            for w_ref, width in zip(w_refs, widths):
                term = _dot(a_ref[j, :, col:col + width], w_ref[j], NT)
                part = term if part is None else part + term
                col += width

        @pl.when(g == 0)
        def _():
            o_ref[...] = part

        @pl.when(g > 0)
        def _():
            o_ref[...] += part

    return pl.pallas_call(
        body, name=name, grid=(t // tm, n_seg // gs),
        in_specs=[pl.BlockSpec((gs, tm, s), lambda i, g: (g, i, 0))]
        + [pl.BlockSpec((gs, d, width), lambda i, g: (g, 0, 0)) for width in widths]
        + [pl.BlockSpec(memory_space=pl.ANY)] * len(after),
        out_specs=pl.BlockSpec((tm, d), lambda i, g: (i, 0)),
        out_shape=jax.ShapeDtypeStruct((t, d), F32),
        compiler_params=_params(("parallel", "arbitrary")))(du, *wgs, *after)


def _prenorm_bwd_first(dhx, x, ctx, dxo, nw, mod, name):
    n_lat, d = x.shape
    t = n_lat + ctx.shape[0]
    nxb = n_lat // ROW_TILE

    def body(dh_ref, x_ref, c_ref, dxo_ref, nw_ref, mod_ref, dx_ref, acc_ref):
        i = pl.program_id(0)

        @pl.when(i == 0)
        def _():
            acc_ref[...] = jnp.zeros_like(acc_ref)

        ctx = i >= nxb
        m = mod_ref[...]
        scale1 = 1.0 + jnp.where(ctx, m[4:5], m[1:2])
        x = jnp.where(ctx, c_ref[...], x_ref[...])
        nw_v = nw_ref[...]
        r = lax.rsqrt(jnp.mean(x * x, axis=-1, keepdims=True) + EPS)
        xn = x * r
        dh = dh_ref[...]
        dshift = jnp.sum(dh, axis=0, keepdims=True)
        dscale = jnp.sum(dh * (xn * nw_v), axis=0, keepdims=True)
        acc_ref[6:7, :] += jnp.sum(dh * scale1 * xn, axis=0, keepdims=True)
        dxn = dh * (nw_v * scale1)
        dx = dxo_ref[...] + r * (dxn - xn * jnp.mean(dxn * xn, axis=-1, keepdims=True))

        @pl.when(i < nxb)
        def _():
            acc_ref[0:1, :] += dshift
            acc_ref[1:2, :] += dscale
            dx_ref[...] = dx

        @pl.when(i >= nxb)
        def _():
            acc_ref[3:4, :] += dshift
            acc_ref[4:5, :] += dscale

    row = pl.BlockSpec((ROW_TILE, d), lambda i: (i, 0))
    lat, cx = _split_rows(nxb, t // ROW_TILE)(d)
    acc = _full((8, d))
    return pl.pallas_call(body, name=name, grid=(t // ROW_TILE,),
                          in_specs=[row, lat, cx, row, _full((1, d)), acc],
                          out_specs=[lat, acc],
                          out_shape=[jax.ShapeDtypeStruct((n_lat, d), F32), jax.ShapeDtypeStruct((8, d), F32)],
                          compiler_params=_params(("arbitrary",)))(dhx, x, ctx, dxo, nw, mod)


def _adamw(g, w, m, v):
    m = ADAM_B1 * m + (1.0 - ADAM_B1) * g
    v = ADAM_B2 * v + (1.0 - ADAM_B2) * jnp.square(g)
    m_hat = m / (1.0 - ADAM_B1 ** ADAM_STEP)
    v_hat = v / (1.0 - ADAM_B2 ** ADAM_STEP)
    delta = -ADAM_LR * (m_hat / (jnp.sqrt(v_hat) + ADAM_EPS) + ADAM_WD * w)
    return delta, m, v


def _sum_adamw(parts, w, m, v, name, row0=0, into=None):
    n_p, r, n = parts.shape
    r_all = w.shape[0]
    part_block_bytes = 4 * 1024 * 1024
    br = 8
    for cand in (512, 256, 128, 64, 32, 16):
        if r % cand == 0 and row0 % cand == 0 and n_p * cand * n * parts.dtype.itemsize <= part_block_bytes:
            br = cand
            break
    blk0 = row0 // br

    def body(p_ref, w_ref, m_ref, v_ref, *rest):
        g_out, d_out, m_out, v_out = rest[-4:]
        g = p_ref[0].astype(F32)
        for j in range(1, n_p):
            g = g + p_ref[j].astype(F32)
        g_out[...] = g
        d_out[...], m_out[...], v_out[...] = _adamw(g, w_ref[...], m_ref[...], v_ref[...])

    row = pl.BlockSpec((br, n), lambda i: (i + blk0, 0))
    kept = [] if into is None else list(into)
    return pl.pallas_call(body, name=name, grid=(r // br,),
                          in_specs=[pl.BlockSpec((n_p, br, n), lambda i: (0, i, 0)), row, row, row]
                          + [pl.BlockSpec(memory_space=pl.ANY)] * len(kept),
                          out_specs=[row] * 4, out_shape=[jax.ShapeDtypeStruct((r_all, n), F32)] * 4,
                          input_output_aliases={4 + j: j for j in range(len(kept))},
                          compiler_params=_params(("parallel",)))(parts, w, m, v, *kept)


def _rope_tables(n_lat, n_ctx):
    f = HEAD_DIM // 4
    rows = n_lat // GRID_W
    inv = ROPE_BASE ** (-jnp.arange(f, dtype=F32) / f)
    ang_r = jnp.arange(rows).astype(F32)[:, None] * inv[None, :]
    ang_c = jnp.arange(GRID_W).astype(F32)[:, None] * inv[None, :]

    cr, sr, cc, sc = jnp.cos(ang_r), jnp.sin(ang_r), jnp.cos(ang_c), jnp.sin(ang_c)
    zr, zc = jnp.zeros_like(cr), jnp.zeros_like(cc)

    def table(by_row, by_col):
        both = by_row[:, None, :] + by_col[None, :, :]
        return both.reshape(n_lat, HEAD_DIM)

    cos = table(jnp.concatenate([cr, cr, zr, zr], axis=-1), jnp.concatenate([zc, zc, cc, cc], axis=-1))
    sa = table(jnp.concatenate([-sr, zr, zr, zr], axis=-1), jnp.concatenate([zc, zc, -sc, zc], axis=-1))
    sb = table(jnp.concatenate([zr, sr, zr, zr], axis=-1), jnp.concatenate([zc, zc, zc, sc], axis=-1))
    pad = jnp.zeros((n_ctx, HEAD_DIM), F32)
    return (jnp.concatenate([cos, pad + 1.0], axis=0), jnp.concatenate([sa, pad], axis=0),
            jnp.concatenate([sb, pad], axis=0))


def _pad_rows(a, rows):
    return jnp.pad(a, [(0, rows - a.shape[0])] + [(0, 0)] * (a.ndim - 1))


def _pad_cols(a, cols):
    return jnp.pad(a, [(0, 0), (0, cols - a.shape[1])])


def kernel(x, c, ctx, c_ctx, norm_w, w_mod, b_mod, w_in, conv_w, conv_norm_w, ret_norm_w, ret_decay_f, ret_decay_b, w_out, final_norm_w, loss_target, m_c_ctx, m_norm_w, m_w_mod, m_b_mod, m_w_in, m_conv_w, m_conv_norm_w, m_ret_norm_w, m_ret_decay_f, m_ret_decay_b, m_w_out, m_final_norm_w, v_c_ctx, v_norm_w, v_w_mod, v_b_mod, v_w_in, v_conv_w, v_conv_norm_w, v_ret_norm_w, v_ret_decay_f, v_ret_decay_b, v_w_out, v_final_norm_w):
    depth = norm_w.shape[0]
    n_lat, d = x.shape[1], x.shape[2]
    n_ctx = ctx.shape[1]
    s = d // 2
    n_heads = ret_decay_f.shape[1]
    nx, ncc = n_lat // CHUNK, n_ctx // CHUNK
    n_mod = w_mod.shape[2]
    n_cw = conv_w.shape[2]
    r_out = w_out.shape[1]
    assert s == n_heads * HEAD_DIM and w_in.shape[2] == s and N_DEV * r_out == d
    assert n_lat % ROW_TILE == 0 and n_ctx % ROW_TILE == 0 and 3 * depth * n_cw <= d and N_DEV * n_mod == 3 * d
    me = 4 * lax.axis_index("x") + 2 * lax.axis_index("y") + lax.axis_index("c")

    w_in_bf = [w_in[l].astype(BF16) for l in range(depth)]
    w_out_bf = [w_out[l].astype(BF16) for l in range(depth)]

    first = jnp.concatenate([c.reshape(1, d), _pad_cols(conv_w.reshape(1, -1), d), jnp.zeros((6, d), F32)], axis=0)
    (first_g,) = _all_gather([first], "gather_cond")
    first_g = first_g.reshape(N_DEV, 8, d)
    c_all = first_g[:, 0, :]
    conv_full = first_g[:, 1, :3 * depth * n_cw].reshape(N_DEV, depth, 3, n_cw)
    conv_full = conv_full.transpose(1, 2, 0, 3).reshape(depth, 3, N_DEV * n_cw)
    c9 = jnp.concatenate([c_all, c_ctx.reshape(1, d), jnp.zeros((7, d), F32)], axis=0)

    b_sh = lax.dynamic_slice(b_mod, (0, me * n_mod), (depth, n_mod))
    mod_sh = jnp.concatenate([_mod_rows(c9, w_mod[l], b_sh[l:l + 1], f"mod_rows_l{l}") for l in range(depth)], axis=0)
    (mod_g,) = _all_gather([mod_sh], "gather_mod")
    mod_g = mod_g.reshape(N_DEV, depth, 16, n_mod)
    mods = []
    for l in range(depth):
        mine = lax.dynamic_index_in_dim(mod_g[:, l], me, axis=1, keepdims=False).reshape(3, d)
        cx = mod_g[:, l, 8, :].reshape(3, d)
        mods.append(jnp.concatenate([mine, cx, jnp.zeros((2, d), F32)], axis=0))

    halves = [w_in_bf[0][:, :s // 2], w_in_bf[0][:, s // 2:]]
    near, order = [], [mod_g]
    for j, part in enumerate(halves):
        near.append(_push_start([part], [_landing(part, me)], HALF_PUSH[j], f"w_in0_start_{j}", after=order))
        order = near[-1][4:]
    pending = []
    for k in range(depth):
        srcs = [w_out_bf[k]] + ([w_in_bf[k]] if k > 0 else [])
        started = _push_start(srcs, [_landing(a, me) for a in srcs], "gather", f"weights_start_l{k}", after=order)
        pending.append(started[:4])
        order = started[4:]
    w_in_g = [None] * depth
    w_out_g = [None] * depth

    cos, sa, sb_tab = _rope_tables(n_lat, n_ctx)
    t_all = n_lat + n_ctx

    saved = []
    xt = hx_next = None
    for l in range(depth):
        tiles = _tiles(l, t_all, d)
        names = ["dc", "dlf", "dlb", "qf", "kf", "qb", "kb", "cdf", "cdb", "lg"]
        dec = jnp.stack([ret_decay_f[l], ret_decay_b[l]], axis=0)
        tabs = dict(zip(names, _decay_tables(dec, n_heads, f"decay_tables_l{l}")))
        if l == 0:
            hx = _prenorm_first(x[0], ctx[0], norm_w[0:1], mods[0], "prenorm_l0", after=order)
            gathered, out, after = [], None, hx
            for j in range(2):
                (landed,) = _push_wait(*near[j][:4], HALF_PUSH[j], after, f"w_in0_wait_{j}")
                if HALF_PUSH[j] == "near":
                    relay = _push_start([], [landed], "relay", f"w_in0_relay_start_{j}")
                    (landed,) = _push_wait(*relay[:4], "relay", relay[4], f"w_in0_relay_wait_{j}")
                gathered.append(landed)
                out = _in_proj(hx, landed, cos, sa, sb_tab, s, j, tiles["in_tm_half"], f"in_proj_l0_{j}", into=out)
                after = out[0]
            u, qkv = out
            w_in_g[0] = gathered
        else:
            landed = _push_wait(*pending[l], "gather", xt, f"weights_wait_l{l}")
            w_out_g[l], w_in_g[l] = landed[0].reshape(d, d), [landed[1]]
            hx = hx_next
            u, qkv = _in_proj(hx, w_in_g[l][0], cos, sa, sb_tab, s, 0, tiles["in_tm"], f"in_proj_l{l}")
        sf, sb = _state_sweep(qkv, tabs, n_heads, nx, ncc, f"state_sweep_l{l}")
        ycat, o = _mix_fwd(u, qkv, sf, sb, tabs, conv_full[l], conv_norm_w[l:l + 1], ret_norm_w[l:l + 1],
                           n_heads, nx, ncc, f"mix_fwd_l{l}")
        if l == 0:
            (landed,) = _push_wait(*pending[0], "gather", ycat, "weights_wait_l0")
            w_out_g[0] = landed.reshape(d, d)
        m_res = x_new = None
        if l < depth - 1:
            res = (x[0], ctx[0]) if l == 0 else (xt,)
            m_res, x_new, hx_next = _out_proj_prenorm(ycat, w_out_g[l], res, mods[l], norm_w[l + 1:l + 2], mods[l + 1],
                                                      n_lat, f"out_proj_l{l}")
        else:
            dxt, dm, loss_blk, dfnw, gate_acc = _out_proj_loss(ycat, w_out_g[l], xt, mods[l], loss_target[0],
                                                               final_norm_w.reshape(1, d), n_lat, f"out_proj_loss_l{l}")
        saved.append(dict(tabs=tabs, xt=xt, hx=hx, u=u, qkv=qkv, sf=sf, sb=sb, ycat=ycat, o=o, m=m_res, tiles=tiles))
        xt = x_new

    dmod_x, dmod_c, dnw, dcnw, dgnw, dconv, ddec, dwin, dwout = [], [], [], [], [], [], [], [], []
    started_token = ()
    for l in reversed(range(depth)):
        sv = saved[l]
        tiles = sv["tiles"]
        dycat = _matmul_nt(dm, w_out_g[l], tiles["ob_tn"], f"out_proj_bwd_l{l}", after=started_token)
        dwout.append(_weight_grad(sv["ycat"], dm.reshape(1, *dm.shape), tiles["wo_bm"], tiles["wo_bt"],
                                  f"w_out_grad_l{l}")[0])
        g, dz, db, drz, do, norm_acc = _mix_bwd_a(dycat, sv["u"], sv["o"], conv_full[l], conv_norm_w[l:l + 1],
                                                   ret_norm_w[l:l + 1], n_heads, nx, ncc, f"mix_bwd_a_l{l}")
        gf, gb = _grad_state_sweep(sv["qkv"], do, sv["tabs"], n_heads, nx, ncc, f"grad_state_sweep_l{l}")
        du, conv_acc, dlg = _mix_bwd_b(sv["u"], g, dz, db, drz, sv["qkv"], do, sv["sf"], sv["sb"],
                                       gf, gb, sv["tabs"], cos, sa, sb_tab, conv_full[l], n_heads, nx, ncc,
                                       f"mix_bwd_b_l{l}")
        gate_acc_l = gate_acc
        if l > 0:
            dhx = _in_proj_bwd(du, w_in_g[l], tiles["bwd_tm"], tiles["bwd_gs"], f"in_proj_bwd_l{l}")
            below = (saved[l - 1]["m"], mods[l - 1])
            dwin_l, dxt, pre_acc, dm, gate_acc = _weight_grad_beside_prenorm_bwd(
                sv["hx"], du, dhx, sv["xt"], dxt, norm_w[l:l + 1], mods[l], below, n_lat, f"w_in_grad_l{l}")
        else:
            dwin_l = _weight_grad(sv["hx"], du, tiles["wg_bm"], tiles["wg_bt"], f"w_in_grad_l{l}")
        srcs = [dwin_l, dwout[-1].reshape(N_DEV, r_out, d)]
        lands = [_landing(lax.dynamic_index_in_dim(a, me, axis=0, keepdims=False), me) for a in srcs]
        started = _push_start(srcs, lands, "scatter", f"grads_start_l{l}")
        dwin.append(started[:4])
        started_token = started[4:]
        if l == 0:
            dhx = _in_proj_bwd(du, w_in_g[l], tiles["bwd_tm"], tiles["bwd_gs"], f"in_proj_bwd_l{l}", after=started[4:])
            dxt, pre_acc = _prenorm_bwd_first(dhx, x[0], ctx[0], dxt, norm_w[l:l + 1], mods[l], f"prenorm_bwd_l{l}")
        dmod_x.append(jnp.concatenate([pre_acc[0], pre_acc[1], gate_acc_l[2]]))
        dmod_c.append(jnp.concatenate([pre_acc[3], pre_acc[4], gate_acc_l[5]]))
        dnw.append(pre_acc[6])
        dcnw.append(norm_acc[0])
        dgnw.append(norm_acc[1])
        dconv.append(conv_acc[0:3])
        ddec.append(dlg[0:2, :n_heads])
    for lst in (dmod_x, dmod_c, dnw, dcnw, dgnw, dconv, ddec, dwin, dwout):
        lst.reverse()
    grad_x = dxt.reshape(1, n_lat, d)

    rows = []
    for l in range(depth):
        rows += [dmod_x[l], dmod_c[l]]
    (dmod_g,) = _all_gather([_pad_rows(jnp.stack(rows, axis=0), 8)], "gather_dmod")
    dmod_g = dmod_g.reshape(N_DEV, 8, 3 * d)
    mine_cols = lax.dynamic_slice(dmod_g, (0, 0, me * n_mod), (N_DEV, 8, n_mod))
    g_wmod, dcc = [], jnp.zeros((d,), F32)
    for l in range(depth):
        gw, dc_part = _mod_grads(mine_cols[:, 2 * l], mine_cols[:, 2 * l + 1], c9, w_mod[l], f"mod_grads_l{l}")
        g_wmod.append(gw)
        dcc = dcc + dc_part[0]

    n_small = 16
    small = jnp.concatenate([
        jnp.stack(dnw, axis=0),
        jnp.concatenate(dcnw).reshape(1, -1),
        jnp.concatenate(dgnw).reshape(1, -1),
        dfnw[0:1],
        dcc.reshape(1, d),
        jnp.stack(dconv, axis=0).reshape(-1, d),
        _pad_cols(jnp.stack(ddec, axis=0).reshape(1, -1), d),
    ], axis=0)
    assert depth * s == d and small.shape[0] < n_small
    n_rows = small.shape[0]
    small = jnp.concatenate([small, _pad_cols(loss_blk[0:1], d)], axis=0)
    (small_g,) = _all_gather([_pad_rows(small, n_small)], "gather_small")
    small_g = small_g.reshape(N_DEV, n_small, d)

    def pack_small(nw_, cn_, gn_, fn_, cc_, df_, db_):
        return _pad_rows(jnp.concatenate([
            nw_, cn_.reshape(1, -1), gn_.reshape(1, -1), fn_.reshape(1, d), cc_.reshape(1, d),
            jnp.zeros((n_rows - depth - 5, d), F32),
            _pad_cols(jnp.stack([df_, db_], axis=1).reshape(1, -1), d)], axis=0), n_small)

    w_s = pack_small(norm_w, conv_norm_w, ret_norm_w, final_norm_w, c_ctx, ret_decay_f, ret_decay_b)
    m_s = pack_small(m_norm_w, m_conv_norm_w, m_ret_norm_w, m_final_norm_w, m_c_ctx, m_ret_decay_f, m_ret_decay_b)
    v_s = pack_small(v_norm_w, v_conv_norm_w, v_ret_norm_w, v_final_norm_w, v_c_ctx, v_ret_decay_f, v_ret_decay_b)
    small_out = _sum_adamw(small_g, w_s, m_s, v_s, "adamw_small")
    loss = small_out[0][n_rows, 0]

    def unpack_small(a):
        nw_ = a[0:depth]
        cn_ = a[depth].reshape(depth, s)
        gn_ = a[depth + 1].reshape(depth, s)
        fn_ = a[depth + 2]
        cc_ = a[depth + 3]
        dd = a[n_rows - 1, :depth * 2 * n_heads].reshape(depth, 2, n_heads)
        return dict(c_ctx=cc_, norm_w=nw_, conv_norm_w=cn_, ret_norm_w=gn_, ret_decay_f=dd[:, 0], ret_decay_b=dd[:, 1],
                    final_norm_w=fn_)

    res = {}
    for kind, arr in zip(("grad", "delta", "m", "v"), small_out):
        for k_, val in unpack_small(arr).items():
            res[(kind, k_)] = val

    bm_parts = jnp.concatenate([dmod_g[:, 0:2 * depth:2].reshape(N_DEV, depth, 3 * d),
                                dmod_g[:, 1:2 * depth:2].reshape(N_DEV, depth, 3 * d)], axis=0)
    bm_parts = jnp.concatenate([bm_parts, jnp.zeros((2 * N_DEV, 8 - depth, 3 * d), F32)], axis=1)
    pad8 = lambda a: _pad_rows(a, 8)
    bm_out = _sum_adamw(bm_parts, pad8(b_mod), pad8(m_b_mod), pad8(v_b_mod), "adamw_b_mod")
    for kind, arr in zip(("grad", "delta", "m", "v"), bm_out):
        res[(kind, "b_mod")] = arr[:depth]

    conv_rows = small_g[:, depth + 4:depth + 4 + 3 * depth * s // d].reshape(N_DEV, depth * 3, s)
    conv_mine = lax.dynamic_slice(conv_rows, (0, 0, me * n_cw), (N_DEV, depth * 3, n_cw))
    conv_mine = jnp.concatenate([conv_mine, jnp.zeros((N_DEV, 8 - depth * 3, n_cw), F32)], axis=1)
    cw2 = lambda a: _pad_rows(a.reshape(depth * 3, n_cw), 8)
    cw_out = _sum_adamw(conv_mine, cw2(conv_w), cw2(m_conv_w), cw2(v_conv_w), "adamw_conv_w")
    for kind, arr in zip(("grad", "delta", "m", "v"), cw_out):
        res[(kind, "conv_w")] = arr[:depth * 3].reshape(depth, 3, n_cw)

    wm_out = _sum_adamw(jnp.stack(g_wmod, axis=0).reshape(1, depth * d, n_mod), w_mod.reshape(depth * d, n_mod),
                        m_w_mod.reshape(depth * d, n_mod), v_w_mod.reshape(depth * d, n_mod), "adamw_w_mod")
    for kind, arr in zip(("grad", "delta", "m", "v"), wm_out):
        res[(kind, "w_mod")] = arr.reshape(depth, d, n_mod)

    wi_out = wo_out = None
    after = wm_out[0]
    for l in reversed(range(depth)):
        win_parts, wout_parts = _push_wait(*dwin[l], "scatter", after, f"grads_wait_l{l}")
        wi_out = _sum_adamw(win_parts, w_in.reshape(depth * d, s), m_w_in.reshape(depth * d, s),
                            v_w_in.reshape(depth * d, s), f"adamw_w_in_l{l}", row0=l * d, into=wi_out)
        wo_out = _sum_adamw(wout_parts, w_out.reshape(depth * r_out, d), m_w_out.reshape(depth * r_out, d),
                            v_w_out.reshape(depth * r_out, d), f"adamw_w_out_l{l}", row0=l * r_out, into=wo_out)
        after = wo_out[0]
    for kind, arr in zip(("grad", "delta", "m", "v"), wi_out):
        res[(kind, "w_in")] = arr.reshape(depth, d, s)
    for kind, arr in zip(("grad", "delta", "m", "v"), wo_out):
        res[(kind, "w_out")] = arr.reshape(depth, r_out, d)

    order = ["c_ctx", "norm_w", "w_mod", "b_mod", "w_in", "conv_w", "conv_norm_w", "ret_norm_w", "ret_decay_f",
             "ret_decay_b", "w_out", "final_norm_w"]
    outs = [loss, grad_x]
    for kind in ("grad", "delta", "m", "v"):
        outs += [res[(kind, k_)] for k_ in order]
    return tuple(outs)
```

```python
import jax
import jax.numpy as jnp
from jax import lax
from jax.experimental import pallas as pl
from jax.experimental.pallas import tpu as pltpu

F32 = jnp.float32
BF16 = jnp.bfloat16

EPS = 1e-6
CHUNK = 128
HEAD_DIM = 128
GRID_W = 64
ROPE_BASE = 10000.0
N_DEV = 8
ADAM_LR, ADAM_B1, ADAM_B2, ADAM_EPS, ADAM_WD, ADAM_STEP = 0.001, 0.9, 0.999, 1e-08, 0.01, 10

ROW_TILE = 256
V7X_VMEM_LIMIT = 56 * 1024 * 1024

NN = ((1,), (0,))
NT = ((1,), (1,))
TN = ((0,), (0,))


def _dot(a, b, dims):
    return lax.dot_general(a, b, (dims, ((), ())), preferred_element_type=F32)


def _params(sem=None):
    if sem is None:
        return pltpu.CompilerParams(vmem_limit_bytes=V7X_VMEM_LIMIT)
    return pltpu.CompilerParams(dimension_semantics=sem, vmem_limit_bytes=V7X_VMEM_LIMIT)


def _silu(z):
    return z * jax.nn.sigmoid(z)


def _dsilu(z):
    s = jax.nn.sigmoid(z)
    return s * (1.0 + z * (1.0 - s))


def _silu_and_slope(z):
    s = jax.nn.sigmoid(z)
    return z * s, s * (1.0 + z * (1.0 - s))


def _sum_all(a):
    return jnp.sum(jnp.sum(a, axis=1, keepdims=True), axis=0, keepdims=True)


def _mm_rows(t):
    return 768 if t % 768 == 0 else ROW_TILE


def _rows_or(t, rows):
    return rows if t % rows == 0 else _mm_rows(t)


def _tiles(layer, t, d):
    return dict(in_tm=_rows_or(t, 1408), in_tm_half=_rows_or(t, 2112), bwd_gs=2, wg_bm=d, wg_bt=_mm_rows(t),
                wo_bm=d, wo_bt=_mm_rows(t), ob_tn=d, bwd_tm=_rows_or(t, 1056))


def _full(shape):
    n = len(shape)
    return pl.BlockSpec(shape, lambda *_: (0,) * n)


def _peers(x, y, c):
    return [(x, y, 1 - c), (1 - x, y, c), (x, 1 - y, c), (1 - x, 1 - y, c),
            (1 - x, y, 1 - c), (x, 1 - y, 1 - c), (1 - x, 1 - y, 1 - c)]


def _lin(p):
    return 4 * p[0] + 2 * p[1] + p[2]


def _all_gather(arrays, name):
    n_arr = len(arrays)
    space = pltpu.VMEM

    def body(*refs):
        ins, outs = refs[:n_arr], refs[n_arr:2 * n_arr]
        send_sems, recv_sems, local_sems = refs[2 * n_arr:]
        x, y, c = lax.axis_index("x"), lax.axis_index("y"), lax.axis_index("c")
        me, sibling = (x, y, c), (x, y, 1 - c)
        chips = [(1 - x, y), (x, 1 - y), (1 - x, 1 - y)]
        every = []
        locals_ = []
        for a in range(n_arr):
            m_per = ins[a].shape[0]
            out_ref = outs[a]

            def rows(p, out_ref=out_ref, m_per=m_per):
                return out_ref.at[pl.ds(_lin(p) * m_per, m_per), :]

            def copy(k, block, to, src=None, a=a, rows=rows):
                return pltpu.make_async_remote_copy(
                    src_ref=rows(block) if src is None else src, dst_ref=rows(block),
                    send_sem=send_sems.at[a, k], recv_sem=recv_sems.at[a, k],
                    device_id=to, device_id_type=pl.DeviceIdType.MESH)

            mine = pltpu.make_async_copy(ins[a], rows(me), local_sems.at[a])
            mine.start()
            locals_.append(mine)
            first = [copy(0, me, sibling, src=ins[a])]
            first += [copy(1 + j, me, (*chip, c), src=ins[a]) for j, chip in enumerate(chips)]
            for cp in first:
                cp.start()
            every.append((copy, first))
        sends = []
        for a in range(n_arr):
            copy, first = every[a]
            passed = [copy(4 + j, (*chip, c), sibling) for j, chip in enumerate(chips)]
            for j, chip in enumerate(chips):
                copy(1 + j, (*chip, c), me).wait_recv()
                passed[j].start()
            sends += first + passed
        for a in range(n_arr):
            copy, _ = every[a]
            copy(0, sibling, me).wait_recv()
            for j, chip in enumerate(chips):
                copy(4 + j, (*chip, 1 - c), me).wait_recv()
        for cp in sends:
            cp.wait_send()
        for mine in locals_:
            mine.wait()

    outs = pl.pallas_call(
        body, name=name,
        out_shape=[jax.ShapeDtypeStruct((N_DEV * a.shape[0], a.shape[1]), a.dtype) for a in arrays],
        in_specs=[pl.BlockSpec(memory_space=space)] * n_arr,
        out_specs=[pl.BlockSpec(memory_space=space)] * n_arr,
        scratch_shapes=[pltpu.SemaphoreType.DMA((n_arr, 7)), pltpu.SemaphoreType.DMA((n_arr, 7)),
                        pltpu.SemaphoreType.DMA((n_arr,))],
        compiler_params=_params(),
    )(*arrays)
    return list(outs)


_HBM = pl.BlockSpec(memory_space=pltpu.HBM)
_SEM = pl.BlockSpec(memory_space=pltpu.SEMAPHORE)
_DATAFLOW = pltpu.SideEffectType.DATAFLOW_SIDE_EFFECTING


PUSH_COPIES = {"scatter": 7, "gather": 7, "near": 4, "relay": 3}


def _push_copies(src_refs, land_refs, send_sems, recv_sems, mode):
    x, y, c = lax.axis_index("x"), lax.axis_index("y"), lax.axis_index("c")
    me, sibling = (x, y, c), (x, y, 1 - c)
    n_k = PUSH_COPIES[mode]
    out, back = [], []
    if mode == "relay":
        for k, chip in enumerate([(1 - x, y), (x, 1 - y), (1 - x, 1 - y)]):
            for a, land in enumerate(land_refs):
                sems = dict(send_sem=send_sems.at[n_k * a + k], recv_sem=recv_sems.at[n_k * a + k],
                            device_id=sibling, device_id_type=pl.DeviceIdType.MESH)
                mine = land.at[_lin((*chip, c))]
                out.append(pltpu.make_async_remote_copy(src_ref=mine, dst_ref=mine, **sems))
                back.append(pltpu.make_async_remote_copy(src_ref=mine, dst_ref=land.at[_lin((*chip, 1 - c))], **sems))
        return out, back
    for k, peer in enumerate(_peers(x, y, c)[:n_k]):
        for a, (src, land) in enumerate(zip(src_refs, land_refs)):
            sems = dict(send_sem=send_sems.at[n_k * a + k], recv_sem=recv_sems.at[n_k * a + k],
                        device_id=peer, device_id_type=pl.DeviceIdType.MESH)
            mine = src.at[_lin(peer)] if mode == "scatter" else src
            out.append(pltpu.make_async_remote_copy(src_ref=mine, dst_ref=land.at[_lin(me)], **sems))
            back.append(pltpu.make_async_remote_copy(src_ref=mine, dst_ref=land.at[_lin(peer)], **sems))
    return out, back


def _push_start(srcs, lands, mode, name, after=()):
    n_src, n = len(srcs), len(lands)
    n_buf = n_src + n
    n_in = n_buf + len(after)
    n_sem = PUSH_COPIES[mode] * n

    def body(*refs):
        send_sems, recv_sems = refs[n_in], refs[n_in + 1]
        out, _ = _push_copies(refs[:n_src], refs[n_src:n_buf], send_sems, recv_sems, mode)
        for cp in out:
            cp.start()
        token = refs[-1]
        token[...] = jnp.zeros_like(token)

    both = list(srcs) + list(lands)
    res = pl.pallas_call(
        body, name=name,
        out_shape=[pltpu.SemaphoreType.DMA((n_sem,)), pltpu.SemaphoreType.DMA((n_sem,))]
        + [pltpu.HBM(a.shape, a.dtype) for a in both] + [jax.ShapeDtypeStruct((8, 128), F32)],
        in_specs=[_HBM] * n_buf + [pl.BlockSpec(memory_space=pl.ANY)] * len(after),
        out_specs=[_SEM, _SEM] + [_HBM] * n_buf + [pl.BlockSpec(memory_space=pltpu.VMEM)],
        input_output_aliases={i: 2 + i for i in range(n_buf)},
        compiler_params=pltpu.CompilerParams(has_side_effects=_DATAFLOW),
    )(*[pltpu.with_memory_space_constraint(a, pltpu.HBM) for a in both], *after)
    return res[0], res[1], list(res[2:2 + n_src]), list(res[2 + n_src:2 + n_buf]), res[-1]


def _push_wait(send_sems, recv_sems, srcs, lands, mode, after, name):
    n_src, n = len(srcs), len(lands)
    n_buf = n_src + n

    def body(*refs):
        out, back = _push_copies(refs[:n_src], refs[n_src:n_buf], refs[n_buf], refs[n_buf + 1], mode)
        for cp in out:
            cp.wait_send()
        for cp in back:
            cp.wait_recv()

    both = list(srcs) + list(lands)
    res = pl.pallas_call(
        body, name=name,
        out_shape=[pltpu.HBM(a.shape, a.dtype) for a in both],
        in_specs=[_HBM] * n_buf + [_SEM, _SEM, pl.BlockSpec(memory_space=pl.ANY)],
        out_specs=[_HBM] * n_buf,
        input_output_aliases={i: i for i in range(n_buf)},
        compiler_params=pltpu.CompilerParams(has_side_effects=_DATAFLOW),
    )(*both, send_sems, recv_sems, after)
    return list(res[n_src:])


def _landing(own, me):
    zone = lax.empty((N_DEV,) + own.shape, own.dtype)
    return lax.dynamic_update_slice(zone, own[None], (me,) + (0,) * own.ndim)


def _mod_rows(c9, w_mod, b_sh, name):
    n = w_mod.shape[1]

    def body(c_ref, w_ref, b_ref, o_ref):
        s9 = _silu(c_ref[...]).astype(BF16)
        o_ref[...] = _dot(s9, w_ref[...].astype(BF16), NN) + b_ref[...]

    return pl.pallas_call(body, name=name, out_shape=jax.ShapeDtypeStruct((16, n), F32),
                          compiler_params=_params())(c9, w_mod, b_sh)


def _mod_grads(dm_rows, dc_rows, c9, w_mod, name):
    d, n = w_mod.shape

    def body(dm_ref, dc_ref, c_ref, w_ref, gw_ref, dc_out):
        dc = dc_ref[...]
        tot = dc[0:1]
        for j in range(1, N_DEV):
            tot = tot + dc[j:j + 1]
        row = lax.broadcasted_iota(jnp.int32, (8, n), 0)
        lower = jnp.where(row == 0, tot, 0.0)
        dmod9 = jnp.concatenate([dm_ref[...], lower], axis=0).astype(BF16)
        c9v = c_ref[...]
        s9 = _silu(c9v).astype(BF16)
        gw_ref[...] = _dot(s9, dmod9, TN)
        ds = _dot(lower.astype(BF16), w_ref[...].astype(BF16), NT)
        dc_out[...] = ds * _dsilu(c9v[8:16])

    return pl.pallas_call(body, name=name,
                          out_shape=[jax.ShapeDtypeStruct((d, n), F32), jax.ShapeDtypeStruct((8, d), F32)],
                          compiler_params=_params())(dm_rows, dc_rows, c9, w_mod)


def _decay_tables(dec, n_heads, name):
    c = CHUNK

    def body(dec_ref, dc_ref, dlf_ref, dlb_ref, qf_ref, kf_ref, qb_ref, kb_ref, cdf_ref, cdb_ref, lg_ref):
        h = pl.program_id(0)
        d = dec_ref[...]
        lane = lax.broadcasted_iota(jnp.int32, d.shape, 1)
        lg = -jnp.exp(jnp.sum(jnp.where(lane == h, d, 0.0), axis=1, keepdims=True))
        lgf, lgb = lg[0:1], lg[1:2]
        i = lax.broadcasted_iota(jnp.int32, (c, c), 0).astype(F32)
        j = lax.broadcasted_iota(jnp.int32, (c, c), 1).astype(F32)
        diff = i - j
        d_f = jnp.where(diff >= 0, jnp.exp(lgf * jnp.maximum(diff, 0.0)), 0.0)
        d_b = jnp.where(diff <= 0, jnp.exp(lgb * jnp.maximum(-diff, 0.0)), 0.0)
        dc_ref[...] = d_f + d_b
        dlf_ref[...] = diff * d_f
        dlb_ref[...] = -diff * d_b
        pos = lax.broadcasted_iota(jnp.int32, (c, HEAD_DIM), 0).astype(F32)
        qf_ref[...] = jnp.exp(lgf * (pos + 1.0))
        kf_ref[...] = jnp.exp(lgf * (c - 1.0 - pos))
        qb_ref[...] = jnp.exp(lgb * (c - pos))
        kb_ref[...] = jnp.exp(lgb * pos)
        ones = jnp.ones((8, HEAD_DIM), F32)
        cdf_ref[...] = jnp.exp(lgf * float(c)) * ones
        cdb_ref[...] = jnp.exp(lgb * float(c)) * ones

        @pl.when(h == 0)
        def _():
            lg_ref[...] = jnp.zeros_like(lg_ref)

        row8 = lax.broadcasted_iota(jnp.int32, (8, HEAD_DIM), 0)
        lane8 = lax.broadcasted_iota(jnp.int32, (8, HEAD_DIM), 1)
        lg_ref[...] += (jnp.where((row8 == 0) & (lane8 == h), lgf, 0.0)
                        + jnp.where((row8 == 1) & (lane8 == h), lgb, 0.0))

    def per_head(*tail):
        return pl.BlockSpec((None,) + tail, lambda h: (h,) + (0,) * len(tail))

    shapes = [(c, c)] * 3 + [(c, HEAD_DIM)] * 4 + [(8, HEAD_DIM)] * 2
    return pl.pallas_call(
        body, name=name, grid=(n_heads,),
        in_specs=[_full(dec.shape)],
        out_specs=[per_head(*s) for s in shapes] + [_full((8, HEAD_DIM))],
        out_shape=[jax.ShapeDtypeStruct((n_heads,) + s, F32) for s in shapes]
        + [jax.ShapeDtypeStruct((8, HEAD_DIM), F32)],
        compiler_params=_params(("arbitrary",)),
    )(dec)


def _modulate(x, nw, shift, scale):
    r = lax.rsqrt(jnp.mean(x * x, axis=-1, keepdims=True) + EPS)
    return ((x * r) * nw * (1.0 + scale) + shift).astype(BF16)


def _split_rows(nxb, nb):
    def specs(d, step=lambda i: i):
        lat = pl.BlockSpec((ROW_TILE, d), lambda i: (jnp.minimum(step(i), nxb - 1), 0))
        ctx = pl.BlockSpec((ROW_TILE, d), lambda i: (jnp.clip(step(i) - nxb, 0, nb - nxb - 1), 0))
        return lat, ctx
    return specs


def _prenorm_first(x, ctx, nw, mod, name, after=()):
    n_lat, d = x.shape
    t = n_lat + ctx.shape[0]
    nxb = n_lat // ROW_TILE

    def body(x_ref, c_ref, nw_ref, mod_ref, *rest):
        o_ref = rest[-1]
        m = mod_ref[...]
        nw_v = nw_ref[...]

        @pl.when(pl.program_id(0) < nxb)
        def _():
            o_ref[...] = _modulate(x_ref[...], nw_v, m[0:1], m[1:2])

        @pl.when(pl.program_id(0) >= nxb)
        def _():
            o_ref[...] = _modulate(c_ref[...], nw_v, m[3:4], m[4:5])

    lat, cx = _split_rows(nxb, t // ROW_TILE)(d)
    return pl.pallas_call(
        body, name=name, grid=(t // ROW_TILE,),
        in_specs=[lat, cx, _full((1, d)), _full((8, d))] + [pl.BlockSpec(memory_space=pl.ANY)] * len(after),
        out_specs=pl.BlockSpec((ROW_TILE, d), lambda i: (i, 0)), out_shape=jax.ShapeDtypeStruct((t, d), BF16),
        compiler_params=_params(("parallel",)))(x, ctx, nw, mod, *after)


def _rope_fwd(v, cos, sa, sb):
    return v * cos + pltpu.roll(v, 96, 1) * sa + pltpu.roll(v, 32, 1) * sb


def _rope_bwd(g, cos, sa, sb):
    return g * cos + pltpu.roll(g * sa, 32, 1) + pltpu.roll(g * sb, 96, 1)


N_PLAIN = 5
U_DTYPE = BF16


def _in_proj(hx, wg, cos, sa, sb, s, part, tm, name, after=(), into=None):
    t, d = hx.shape
    n_seg, _, n = wg.shape
    nb = t // tm
    k_scale = HEAD_DIM ** -0.5
    kept = [] if into is None else list(into)

    def body(a_ref, w_ref, cos_ref, sa_ref, sb_ref, *rest):
        u_ref, qkv_ref = rest[-2:]
        g = pl.program_id(1)
        acc = _dot(a_ref[...], w_ref[...], NN)

        @pl.when(g < N_PLAIN)
        def _():
            u_ref[...] = acc.astype(U_DTYPE)

        @pl.when(g == N_PLAIN + 2)
        def _():
            qkv_ref[...] = acc.astype(BF16)

        for which, scale in ((N_PLAIN, 1.0), (N_PLAIN + 1, k_scale)):
            @pl.when(g == which)
            def _(scale=scale):
                co, a, b = cos_ref[...], sa_ref[...], sb_ref[...]
                for h in range(n // HEAD_DIM):
                    sl = slice(h * HEAD_DIM, (h + 1) * HEAD_DIM)
                    qkv_ref[:, sl] = (_rope_fwd(acc[:, sl], co, a, b) * scale).astype(BF16)

    def w_seg(g):
        return jnp.where(g < N_PLAIN - 1, g, jnp.where(g == N_PLAIN - 1, n_seg - 1, g - 1))

    def qkv_at(i, g):
        held = (jnp.where(i == 0, 0, 2), jnp.maximum(i - 1, 0))
        return (jnp.where(g < N_PLAIN, held[0], g - N_PLAIN), jnp.where(g < N_PLAIN, held[1], i), part)

    tab = pl.BlockSpec((tm, HEAD_DIM), lambda i, g: (i, 0))
    hbm = pl.BlockSpec(memory_space=pl.ANY)
    return pl.pallas_call(
        body, name=name, grid=(nb, n_seg),
        in_specs=[pl.BlockSpec((tm, d), lambda i, g: (i, 0)), pl.BlockSpec((None, d, n), lambda i, g: (w_seg(g), 0, 0)),
                  tab, tab, tab] + [hbm] * (len(after) + len(kept)),
        out_specs=[pl.BlockSpec((None, tm, n), lambda i, g: (jnp.minimum(g, N_PLAIN - 1), i, part)),
                   pl.BlockSpec((None, tm, n), qkv_at)],
        out_shape=[jax.ShapeDtypeStruct((N_PLAIN, t, s), U_DTYPE), jax.ShapeDtypeStruct((3, t, s), BF16)],
        input_output_aliases={5 + len(after) + j: j for j in range(len(kept))},
        compiler_params=_params(("arbitrary", "arbitrary")))(hx, wg, cos, sa, sb, *after, *kept)


def _pair_sweep(xs, ys, tab_f, tab_b, cdf, cdb, n_heads, nx, ncc, reverse, name):
    t, s = xs[0].shape[-2:]
    nc = nx + ncc
    c = CHUNK
    n_pair = nc // 2
    assert nx % 2 == 0 and ncc % 2 == 0

    def f_pair(i):
        step = n_pair - 1 - i if reverse else i
        return jnp.where(step < ncc // 2, nx // 2 + step, step - ncc // 2)

    def b_pair(i):
        return i if reverse else n_pair - 1 - i

    f_subs = (1, 0) if reverse else (0, 1)
    b_subs = (0, 1) if reverse else (1, 0)

    n_slot = 3

    def body(x_any, y_any, tf, tb, cdf_ref, cdb_ref, sf_out, sb_out, sf, sb, xf_buf, yf_buf, xb_buf, yb_buf, sems):
        i = pl.program_id(0)

        def fetch(step):
            slot = step % n_slot
            copies = []
            for k, (arr, ref, pair, buf) in enumerate(((xs, x_any, f_pair, xf_buf), (ys, y_any, f_pair, yf_buf),
                                                       (xs, x_any, b_pair, xb_buf), (ys, y_any, b_pair, yb_buf))):
                rows = pl.ds(pl.multiple_of(pair(step) * 2 * c, 2 * c), 2 * c)
                src = ref.at[rows, :] if arr[1] is None else ref.at[arr[1], rows, :]
                copies.append(pltpu.make_async_copy(src, buf.at[slot], sems.at[k, slot]))
            return copies

        @pl.when(i == 0)
        def _():
            sf[...] = jnp.zeros_like(sf)
            sb[...] = jnp.zeros_like(sb)
            for cp in fetch(0) + fetch(1):
                cp.start()

        @pl.when(i + 2 < n_pair)
        def _():
            for cp in fetch(i + 2):
                cp.start()

        for cp in fetch(i):
            cp.wait()
        slot = i % n_slot
        xf_ref, yf_ref, xb_ref, yb_ref = xf_buf.at[slot], yf_buf.at[slot], xb_buf.at[slot], yb_buf.at[slot]

        for step in range(2):
            for x_ref, y_ref, tab, cd, out, st, sub in ((xf_ref, yf_ref, tf, cdf_ref, sf_out, sf, f_subs[step]),
                                                        (xb_ref, yb_ref, tb, cdb_ref, sb_out, sb, b_subs[step])):
                rows = pl.ds(sub * c, c)
                for h in range(n_heads):
                    sl = pl.ds(h * HEAD_DIM, HEAD_DIM)
                    out[sub, h] = st[h].astype(BF16)
                    xd = (x_ref[rows, sl].astype(F32) * tab[h]).astype(BF16)
                    st[h] = cd[h][0:1, :] * st[h] + _dot(xd, y_ref[rows, sl], TN)

    assert n_pair >= 2
    st_blk = (2, n_heads, HEAD_DIM, HEAD_DIM)
    ring = pltpu.VMEM((n_slot, 2 * c, s), BF16)
    return pl.pallas_call(
        body, name=name, grid=(n_pair,),
        in_specs=[pl.BlockSpec(memory_space=pl.ANY), pl.BlockSpec(memory_space=pl.ANY),
                  _full((n_heads, c, HEAD_DIM)), _full((n_heads, c, HEAD_DIM)),
                  _full((n_heads, 8, HEAD_DIM)), _full((n_heads, 8, HEAD_DIM))],
        out_specs=[pl.BlockSpec(st_blk, lambda i: (f_pair(i), 0, 0, 0)), pl.BlockSpec(st_blk, lambda i: (b_pair(i), 0, 0, 0))],
        out_shape=[jax.ShapeDtypeStruct((nc, n_heads, HEAD_DIM, HEAD_DIM), BF16)] * 2,
        scratch_shapes=[pltpu.VMEM((n_heads, HEAD_DIM, HEAD_DIM), F32)] * 2 + [ring] * 4
        + [pltpu.SemaphoreType.DMA((4, n_slot))],
        compiler_params=_params(("arbitrary",)),
    )(xs[0], ys[0], tab_f, tab_b, cdf, cdb)


def _state_sweep(qkv, tabs, n_heads, nx, ncc, name):
    return _pair_sweep((qkv, 1), (qkv, 2), tabs["kf"], tabs["kb"], tabs["cdf"], tabs["cdb"], n_heads, nx, ncc, False, name)


MIX_CHUNKS = 2
MIX_ROWS = MIX_CHUNKS * CHUNK


HALO = 16


def _halo_specs(s, t):
    per = MIX_ROWS // HALO
    n_halo = t // HALO

    def prev(g):
        return pl.BlockSpec((None, HALO, s), lambda i: (g, jnp.maximum(i * per - 1, 0), 0))

    def nxt(g):
        return pl.BlockSpec((None, HALO, s), lambda i: (g, jnp.minimum((i + 1) * per, n_halo - 1), 0))

    return prev, nxt


def _conv_input(h_ref, c_ref, hp_ref, hn_ref, cp_ref, cn_ref):
    a = c_ref[...].astype(F32) * h_ref[...].astype(F32)
    before = cp_ref[HALO - 1:HALO].astype(F32) * hp_ref[HALO - 1:HALO].astype(F32)
    after = cn_ref[0:1].astype(F32) * hn_ref[0:1].astype(F32)
    return a, before, after


def _shifted(a, before, after, has_prev, has_next):
    rows = a.shape[0]
    rowi = lax.broadcasted_iota(jnp.int32, a.shape, 0)
    am = jnp.where(rowi == 0, jnp.where(has_prev, before, 0.0), pltpu.roll(a, 1, 0))
    ap = jnp.where(rowi == rows - 1, jnp.where(has_next, after, 0.0), pltpu.roll(a, rows - 1, 0))
    return am, ap


def _neighbours(i, nx, nc):
    nxb, ncb = nx // MIX_CHUNKS, nc // MIX_CHUNKS
    return (i != 0) & (i != nxb), (i != nxb - 1) & (i != ncb - 1)


def _mix_fwd(u, qkv, sf, sb, tabs, conv_w, cnw, gnw, n_heads, nx, ncc, name):
    _, t, s = u.shape
    nc = nx + ncc
    c = CHUNK
    assert nx % MIX_CHUNKS == 0 and ncc % MIX_CHUNKS == 0

    def body(h_ref, b_ref, c_ref, z_ref, rz_ref, hp_ref, hn_ref, cp_ref, cn_ref, q_ref, k_ref, v_ref,
             sf_ref, sb_ref, dc_ref, qft, qbt, w_ref, cnw_ref, gnw_ref, y_ref, o_ref):
        i = pl.program_id(0)
        has_prev, has_next = _neighbours(i, nx, nc)
        a, before, after = _conv_input(h_ref, c_ref, hp_ref, hn_ref, cp_ref, cn_ref)
        am, ap = _shifted(a, before, after, has_prev, has_next)
        w = w_ref[...]
        y0 = w[0:1] * am + w[1:2] * a + w[2:3] * ap
        yb = b_ref[...].astype(F32) * y0
        r = lax.rsqrt(jnp.mean(yb * yb, axis=-1, keepdims=True) + EPS)
        y_ref[:, pl.ds(0, s)] = (_silu(z_ref[...].astype(F32)) * ((yb * r) * cnw_ref[...])).astype(BF16)
        for sub in range(MIX_CHUNKS):
            rows = pl.ds(sub * c, c)
            for h in range(n_heads):
                sl = pl.ds(h * HEAD_DIM, HEAD_DIM)
                q, k, v = q_ref[rows, sl], k_ref[rows, sl], v_ref[rows, sl]
                p = (_dot(q, k, NT) * dc_ref[h]).astype(BF16)
                o = _dot(p, v, NN)
                qf = q.astype(F32)
                o += _dot((qf * qft[h]).astype(BF16), sf_ref[sub, h], NN)
                o += _dot((qf * qbt[h]).astype(BF16), sb_ref[sub, h], NN)
                o_ref[rows, sl] = o
                mu = jnp.mean(o, axis=-1, keepdims=True)
                var = jnp.mean(jnp.square(o - mu), axis=-1, keepdims=True)
                on = (o - mu) * lax.rsqrt(var + EPS)
                y_ref[rows, pl.ds(s + h * HEAD_DIM, HEAD_DIM)] = (
                    _silu(rz_ref[rows, sl].astype(F32)) * (on * gnw_ref[:, sl])).astype(BF16)

    def seg(g):
        return pl.BlockSpec((None, MIX_ROWS, s), lambda i: (g, i, 0))

    prev, nxt = _halo_specs(s, t)
    row = pl.BlockSpec((MIX_ROWS, s), lambda i: (i, 0))
    st = pl.BlockSpec((MIX_CHUNKS, n_heads, HEAD_DIM, HEAD_DIM), lambda i: (i, 0, 0, 0))
    return pl.pallas_call(
        body, name=name, grid=(nc // MIX_CHUNKS,),
        in_specs=[seg(0), seg(1), seg(2), seg(3), seg(4), prev(0), nxt(0), prev(2), nxt(2), seg(0), seg(1), seg(2),
                  st, st, _full((n_heads, c, c)), _full((n_heads, c, HEAD_DIM)), _full((n_heads, c, HEAD_DIM)),
                  _full((3, s)), _full((1, s)), _full((1, s))],
        out_specs=[pl.BlockSpec((MIX_ROWS, 2 * s), lambda i: (i, 0)), row],
        out_shape=[jax.ShapeDtypeStruct((t, 2 * s), BF16), jax.ShapeDtypeStruct((t, s), F32)],
        compiler_params=_params(("parallel",)),
    )(u, u, u, u, u, u, u, u, u, qkv, qkv, qkv, sf, sb, tabs["dc"], tabs["qf"], tabs["qb"], conv_w, cnw, gnw)


def _out_proj_prenorm(ycat, w_out, res, mod, nw_next, mod_next, n_lat, name):
    t, d = ycat.shape
    nb = t // ROW_TILE
    nxb = n_lat // ROW_TILE
    split = len(res) == 2

    def body(a_ref, w_ref, *rest):
        res_refs = rest[:len(res)]
        mod_ref, nw_ref, modn_ref, m_ref, xo_ref, hx_ref, xs = rest[len(res):]
        i = pl.program_id(0)

        @pl.when(i == 0)
        def _():
            xs[...] = jnp.zeros_like(xs)

        cur_ctx = jnp.minimum(i, nb - 1) >= nxb
        prev_ctx = i - 1 >= nxb

        def step(cur, prev):
            mv, mn = mod_ref[...], modn_ref[...]
            shift = jnp.where(prev_ctx, mn[3:4], mn[0:1])
            scale = jnp.where(prev_ctx, mn[4:5], mn[1:2])
            hx_ref[...] = _modulate(xs[prev], nw_ref[...], shift, scale)
            m = _dot(a_ref[...], w_ref[...], NN)
            x_res = jnp.where(cur_ctx, res_refs[1][...], res_refs[0][...]) if split else res_refs[0][...]
            x_new = x_res + jnp.where(cur_ctx, mv[5:6], mv[2:3]) * m
            m_ref[...] = m.astype(BF16)
            xo_ref[...] = x_new
            xs[cur] = x_new

        @pl.when(i % 2 == 0)
        def _():
            step(0, 1)

        @pl.when(i % 2 == 1)
        def _():
            step(1, 0)

    cur = pl.BlockSpec((ROW_TILE, d), lambda i: (jnp.minimum(i, nb - 1), 0))
    prev = pl.BlockSpec((ROW_TILE, d), lambda i: (jnp.maximum(i - 1, 0), 0))
    res_specs = list(_split_rows(nxb, nb)(d, lambda i: jnp.minimum(i, nb - 1))) if split else [cur]
    return pl.pallas_call(
        body, name=name, grid=(nb + 1,),
        in_specs=[cur, _full((d, d))] + res_specs + [_full((8, d)), _full((1, d)), _full((8, d))],
        out_specs=[cur, cur, prev],
        out_shape=[jax.ShapeDtypeStruct((t, d), BF16), jax.ShapeDtypeStruct((t, d), F32),
                   jax.ShapeDtypeStruct((t, d), BF16)],
        scratch_shapes=[pltpu.VMEM((2, ROW_TILE, d), F32)],
        compiler_params=_params(("arbitrary",)))(ycat, w_out, *res, mod, nw_next, mod_next)


def _out_proj_loss(ycat, w_out, xt, mod, tgt, fnw, n_lat, name):
    t, d = xt.shape
    nb = t // ROW_TILE
    nxb = n_lat // ROW_TILE

    def body(a_ref, w_ref, x_ref, mod_ref, t_ref, fw_ref, dx_ref, dm_ref, loss_ref, dw_ref, gacc_ref, xs, ms):
        i = pl.program_id(0)

        @pl.when(i == 0)
        def _():
            xs[...] = jnp.zeros_like(xs)
            ms[...] = jnp.zeros_like(ms)
            loss_ref[...] = jnp.zeros_like(loss_ref)
            dw_ref[...] = jnp.zeros_like(dw_ref)
            gacc_ref[...] = jnp.zeros_like(gacc_ref)

        def step(cur, prev):
            mv = mod_ref[...]
            x_prev, m_prev = xs[prev], ms[prev]
            valid = (i >= 1) & (i - 1 < nxb)
            w = fw_ref[...]
            r = lax.rsqrt(jnp.mean(x_prev * x_prev, axis=-1, keepdims=True) + EPS)
            xn = x_prev * r
            e = xn * w - t_ref[...]
            loss = 0.5 * jnp.sum(jnp.mean(e * e, axis=-1, keepdims=True), axis=0, keepdims=True)
            loss_ref[...] += jnp.where(valid, loss, 0.0)
            dy = e * (1.0 / d)
            dw_ref[0:1, :] += jnp.where(valid, jnp.sum(dy * xn, axis=0, keepdims=True), 0.0)
            dxn = dy * w
            dx = jnp.where(valid, r * (dxn - xn * jnp.mean(dxn * xn, axis=-1, keepdims=True)), 0.0)
            dx_ref[...] = dx
            dm_ref[...] = (dx * mv[2:3]).astype(BF16)
            gacc_ref[2:3, :] += jnp.sum(dx * m_prev, axis=0, keepdims=True)

            m = _dot(a_ref[...], w_ref[...], NN)
            gate = jnp.where(jnp.minimum(i, nb - 1) >= nxb, mv[5:6], mv[2:3])
            xs[cur] = x_ref[...] + gate * m
            ms[cur] = m

        @pl.when(i % 2 == 0)
        def _():
            step(0, 1)

        @pl.when(i % 2 == 1)
        def _():
            step(1, 0)

    cur = pl.BlockSpec((ROW_TILE, d), lambda i: (jnp.minimum(i, nb - 1), 0))
    prev = pl.BlockSpec((ROW_TILE, d), lambda i: (jnp.maximum(i - 1, 0), 0))
    return pl.pallas_call(
        body, name=name, grid=(nb + 1,),
        in_specs=[cur, _full((d, d)), cur, _full((8, d)),
                  pl.BlockSpec((ROW_TILE, d), lambda i: (jnp.clip(i - 1, 0, nxb - 1), 0)), _full((1, d))],
        out_specs=[prev, prev, _full((8, HEAD_DIM)), _full((8, d)), _full((8, d))],
        out_shape=[jax.ShapeDtypeStruct((t, d), F32), jax.ShapeDtypeStruct((t, d), BF16),
                   jax.ShapeDtypeStruct((8, HEAD_DIM), F32), jax.ShapeDtypeStruct((8, d), F32),
                   jax.ShapeDtypeStruct((8, d), F32)],
        scratch_shapes=[pltpu.VMEM((2, ROW_TILE, d), F32), pltpu.VMEM((2, ROW_TILE, d), F32)],
        compiler_params=_params(("arbitrary",)))(ycat, w_out, xt, mod, tgt, fnw)


def _matmul_nt(a, w, tn, name, after=()):
    t, k = a.shape
    n = w.shape[0]
    tm = _mm_rows(t)

    def body(a_ref, w_ref, *rest):
        rest[-1][...] = _dot(a_ref[...], w_ref[...], NT)

    return pl.pallas_call(
        body, name=name, grid=(n // tn, t // tm),
        in_specs=[pl.BlockSpec((tm, k), lambda j, i: (i, 0)), pl.BlockSpec((tn, k), lambda j, i: (j, 0))]
        + [pl.BlockSpec(memory_space=pl.ANY)] * len(after),
        out_specs=pl.BlockSpec((tm, tn), lambda j, i: (i, j)),
        out_shape=jax.ShapeDtypeStruct((t, n), F32),
        compiler_params=_params(("parallel", "parallel")))(a, w, *after)


def _weight_grad(a, b, bm, bt, name):
    t, m = a.shape
    n_g, _, n = b.shape
    nt = t // bt

    def body(a_ref, b_ref, o_ref, acc):
        k = pl.program_id(2)

        @pl.when(k == 0)
        def _():
            acc[...] = jnp.zeros_like(acc)

        acc[...] += _dot(a_ref[...], b_ref[...], TN)

        @pl.when(k == nt - 1)
        def _():
            o_ref[...] = acc[...].astype(o_ref.dtype)

    return pl.pallas_call(
        body, name=name, grid=(n_g, m // bm, nt),
        in_specs=[pl.BlockSpec((bt, bm), lambda g, i, k: (k, i)), pl.BlockSpec((None, bt, n), lambda g, i, k: (g, k, 0))],
        out_specs=pl.BlockSpec((None, bm, n), lambda g, i, k: (g, i, 0)),
        out_shape=jax.ShapeDtypeStruct((n_g, m, n), BF16),
        scratch_shapes=[pltpu.VMEM((bm, n), F32)],
        compiler_params=_params(("parallel", "parallel", "arbitrary")))(a, b)


def _weight_grad_beside_prenorm_bwd(a, b, dhx, xt, dxo, nw, mod, below, n_lat, name):
    t, m = a.shape
    n_g, _, n = b.shape
    d = xt.shape[1]
    bt = _mm_rows(t)
    nt = t // bt
    rows = t // (n_g * nt)
    n_piece = 2 if rows % 32 == 0 and m % 2 == 0 else 1
    rows_p, m_p = rows // n_piece, m // n_piece
    assert rows * n_g * nt == t and rows_p % 8 == 0

    def body(a_ref, b_ref, dh_ref, x_ref, dxo_ref, nw_ref, mod_ref, m_ref, modb_ref,
             o_ref, dx_ref, acc_ref, dm_ref, gacc_ref, acc):
        g, k = pl.program_id(0), pl.program_id(1)
        step = g * nt + k

        @pl.when(step == 0)
        def _():
            acc_ref[...] = jnp.zeros_like(acc_ref)
            gacc_ref[...] = jnp.zeros_like(gacc_ref)

        @pl.when(k == 0)
        def _():
            acc[...] = jnp.zeros_like(acc)

        mv, mb, nw_v = mod_ref[...], modb_ref[...], nw_ref[...]
        for p in range(n_piece):
            rs = pl.ds(p * rows_p, rows_p)
            rowi = step * rows + p * rows_p + lax.broadcasted_iota(jnp.int32, (rows_p, 1), 0)
            ctx = rowi >= n_lat
            w_lat = jnp.where(ctx, 0.0, 1.0)
            w_ctx = 1.0 - w_lat
            scale1 = 1.0 + jnp.where(ctx, mv[4:5], mv[1:2])
            x = x_ref[rs, :]
            r = lax.rsqrt(jnp.mean(x * x, axis=-1, keepdims=True) + EPS)
            xn = x * r
            dh = dh_ref[rs, :]
            dsc = dh * (xn * nw_v)
            acc_ref[0:1, :] += jnp.sum(dh * w_lat, axis=0, keepdims=True)
            acc_ref[1:2, :] += jnp.sum(dsc * w_lat, axis=0, keepdims=True)
            acc_ref[3:4, :] += jnp.sum(dh * w_ctx, axis=0, keepdims=True)
            acc_ref[4:5, :] += jnp.sum(dsc * w_ctx, axis=0, keepdims=True)
            acc_ref[6:7, :] += jnp.sum(dh * scale1 * xn, axis=0, keepdims=True)
            dxn = dh * (nw_v * scale1)
            dx = dxo_ref[rs, :] + r * (dxn - xn * jnp.mean(dxn * xn, axis=-1, keepdims=True))
            dx_ref[rs, :] = dx
            dm_ref[rs, :] = (dx * jnp.where(ctx, mb[5:6], mb[2:3])).astype(BF16)
            dg = dx * m_ref[rs, :].astype(F32)
            gacc_ref[2:3, :] += jnp.sum(dg * w_lat, axis=0, keepdims=True)
            gacc_ref[5:6, :] += jnp.sum(dg * w_ctx, axis=0, keepdims=True)

            ms_ = pl.ds(p * m_p, m_p)
            acc[ms_, :] += _dot(a_ref[:, ms_], b_ref[...], TN)

        @pl.when(k == nt - 1)
        def _():
            o_ref[...] = acc[...].astype(o_ref.dtype)

    side = pl.BlockSpec((rows, d), lambda g, k: (g * nt + k, 0))
    acc8 = _full((8, d))
    return pl.pallas_call(
        body, name=name, grid=(n_g, nt),
        in_specs=[pl.BlockSpec((bt, m), lambda g, k: (k, 0)), pl.BlockSpec((None, bt, n), lambda g, k: (g, k, 0)),
                  side, side, side, _full((1, d)), acc8, side, acc8],
        out_specs=[pl.BlockSpec((None, m, n), lambda g, k: (g, 0, 0)), side, acc8, side, acc8],
        out_shape=[jax.ShapeDtypeStruct((n_g, m, n), BF16), jax.ShapeDtypeStruct((t, d), F32),
                   jax.ShapeDtypeStruct((8, d), F32), jax.ShapeDtypeStruct((t, d), BF16),
                   jax.ShapeDtypeStruct((8, d), F32)],
        scratch_shapes=[pltpu.VMEM((m, n), F32)],
        compiler_params=_params(("arbitrary", "arbitrary")))(a, b, dhx, xt, dxo, nw, mod, *below)


def _mix_bwd_a(dycat, u, o, conv_w, cnw, gnw, n_heads, nx, ncc, name):
    _, t, s = u.shape
    nc = nx + ncc

    def body(dy_ref, h_ref, b_ref, c_ref, z_ref, rz_ref, hp_ref, hn_ref, cp_ref, cn_ref, o_ref, w_ref,
             cnw_ref, gnw_ref, g_ref, dz_ref, db_ref, drz_ref, do_ref, acc_ref):
        i = pl.program_id(0)

        @pl.when(i == 0)
        def _():
            acc_ref[...] = jnp.zeros_like(acc_ref)

        has_prev, has_next = _neighbours(i, nx, nc)
        a, before, after = _conv_input(h_ref, c_ref, hp_ref, hn_ref, cp_ref, cn_ref)
        am, ap = _shifted(a, before, after, has_prev, has_next)
        w = w_ref[...]
        y0 = w[0:1] * am + w[1:2] * a + w[2:3] * ap
        bb = b_ref[...].astype(F32)
        yb = bb * y0
        r = lax.rsqrt(jnp.mean(yb * yb, axis=-1, keepdims=True) + EPS)
        ynn = yb * r
        z = z_ref[...].astype(F32)
        dyc = dy_ref[:, pl.ds(0, s)]
        cw = cnw_ref[...]
        sz, dsz = _silu_and_slope(z)
        dz_ref[...] = (dyc * (ynn * cw) * dsz).astype(BF16)
        dyn = dyc * sz
        acc_ref[0:1, :] += jnp.sum(dyn * ynn, axis=0, keepdims=True)
        dynn = dyn * cw
        dyb = r * (dynn - ynn * jnp.mean(dynn * ynn, axis=-1, keepdims=True))
        db_ref[...] = (dyb * y0).astype(BF16)
        g_ref[...] = dyb * bb
        for h in range(n_heads):
            sl = pl.ds(h * HEAD_DIM, HEAD_DIM)
            ov = o_ref[:, sl]
            mu = jnp.mean(ov, axis=-1, keepdims=True)
            var = jnp.mean(jnp.square(ov - mu), axis=-1, keepdims=True)
            rs = lax.rsqrt(var + EPS)
            on = (ov - mu) * rs
            dyr = dy_ref[:, pl.ds(s + h * HEAD_DIM, HEAD_DIM)]
            rz = rz_ref[:, sl].astype(F32)
            gw = gnw_ref[:, sl]
            srz, dsrz = _silu_and_slope(rz)
            drz_ref[:, sl] = (dyr * (on * gw) * dsrz).astype(BF16)
            dyg = dyr * srz
            acc_ref[1:2, sl] += jnp.sum(dyg * on, axis=0, keepdims=True)
            don = dyg * gw
            do = rs * (don - jnp.mean(don, axis=-1, keepdims=True)
                       - on * jnp.mean(don * on, axis=-1, keepdims=True))
            do_ref[:, sl] = do.astype(BF16)

    def seg(g):
        return pl.BlockSpec((None, MIX_ROWS, s), lambda i: (g, i, 0))

    prev, nxt = _halo_specs(s, t)
    row = pl.BlockSpec((MIX_ROWS, s), lambda i: (i, 0))
    return pl.pallas_call(
        body, name=name, grid=(nc // MIX_CHUNKS,),
        in_specs=[pl.BlockSpec((MIX_ROWS, 2 * s), lambda i: (i, 0)), seg(0), seg(1), seg(2), seg(3), seg(4),
                  prev(0), nxt(0), prev(2), nxt(2), row, _full((3, s)), _full((1, s)), _full((1, s))],
        out_specs=[row, row, row, row, row, _full((8, s))],
        out_shape=[jax.ShapeDtypeStruct((t, s), F32)] + [jax.ShapeDtypeStruct((t, s), BF16)] * 4
        + [jax.ShapeDtypeStruct((8, s), F32)],
        compiler_params=_params(("arbitrary",)),
    )(dycat, u, u, u, u, u, u, u, u, u, o, conv_w, cnw, gnw)


def _grad_state_sweep(qkv, do, tabs, n_heads, nx, ncc, name):
    return _pair_sweep((qkv, 0), (do, None), tabs["qf"], tabs["qb"], tabs["cdf"], tabs["cdb"], n_heads, nx, ncc, True, name)


def _mix_bwd_b(u, g, dz, db, drz, qkv, do, sf, sb, gf, gb, tabs, cos, sa, sb_tab, conv_w,
               n_heads, nx, ncc, name):
    _, t, s = u.shape
    nc = nx + ncc
    c = CHUNK
    k_scale = HEAD_DIM ** -0.5

    def body(h_ref, c_ref, g_ref, gp_ref, gn_ref, dz_ref, db_ref, drz_ref, q_ref, k_ref, v_ref, do_ref,
             sf_ref, sb_ref, gf_ref, gb_ref, dc_t, dlf_t, dlb_t, qft, kft, qbt, kbt, cdf, cdb, lg_ref,
             cos_ref, sa_ref, sb_ref2, w_ref, du_ref, dw_ref, dlg_ref):
        i = pl.program_id(0)

        @pl.when(i == 0)
        def _():
            dw_ref[...] = jnp.zeros_like(dw_ref)
            dlg_ref[...] = jnp.zeros_like(dlg_ref)

        has_prev, has_next = _neighbours(i, nx, nc)
        gv = g_ref[...]
        gm, gp = _shifted(gv, gp_ref[7:8], gn_ref[0:1], has_prev, has_next)
        w = w_ref[...]
        da = w[0:1] * gp + w[1:2] * gv + w[2:3] * gm
        hh, cc = h_ref[...].astype(F32), c_ref[...].astype(F32)
        du_ref[0] = (da * cc).astype(BF16)
        du_ref[2] = (da * hh).astype(BF16)
        a = cc * hh
        dw_ref[0:1, :] += jnp.sum(a * gp, axis=0, keepdims=True)
        dw_ref[1:2, :] += jnp.sum(a * gv, axis=0, keepdims=True)
        dw_ref[2:3, :] += jnp.sum(a * gm, axis=0, keepdims=True)
        du_ref[1] = db_ref[...]
        du_ref[3] = dz_ref[...]
        du_ref[7] = drz_ref[...]

        pos = lax.broadcasted_iota(jnp.int32, (c, HEAD_DIM), 0).astype(F32)
        w_q_f, w_q_b, w_k_f = pos + 1.0, c - pos, c - 1.0 - pos
        row8 = lax.broadcasted_iota(jnp.int32, (8, HEAD_DIM), 0)
        lane8 = lax.broadcasted_iota(jnp.int32, (8, HEAD_DIM), 1)
        dlg = jnp.zeros((8, HEAD_DIM), F32)
        for sub, h in [(sub, h) for sub in range(MIX_CHUNKS) for h in range(n_heads)]:
            rows = pl.ds(sub * c, c)
            co, ra, rb = cos_ref[rows, :], sa_ref[rows, :], sb_ref2[rows, :]
            sl = pl.ds(h * HEAD_DIM, HEAD_DIM)
            q, k, v, do = q_ref[rows, sl], k_ref[rows, sl], v_ref[rows, sl], do_ref[rows, sl]
            qf, kf, dof = q.astype(F32), k.astype(F32), do.astype(F32)
            s_f, s_b, g_f, g_b = sf_ref[sub, h], sb_ref[sub, h], gf_ref[sub, h], gb_ref[sub, h]
            p = _dot(q, k, NT)
            pd = _dot(do, v, NT)
            pdd = (pd * dc_t[h]).astype(BF16)
            dq = _dot(pdd, k, NN)
            dk = _dot(pdd, q, TN)
            dv = _dot((p * dc_t[h]).astype(BF16), do, TN)
            dq_f = _dot((dof * qft[h]).astype(BF16), s_f, NT)
            dq_b = _dot((dof * qbt[h]).astype(BF16), s_b, NT)
            dk_f = _dot(v, g_f, NT) * kft[h]
            dk_b = _dot(v, g_b, NT) * kbt[h]
            dv += _dot((kf * kft[h]).astype(BF16), g_f, NN) + _dot((kf * kbt[h]).astype(BF16), g_b, NN)
            ppd = p * pd
            cd_f, cd_b = cdf[h][0:1, :], cdb[h][0:1, :]
            t_f = _sum_all(dlf_t[h] * ppd + w_q_f * qf * dq_f + w_k_f * kf * dk_f
                           + float(c) * (cd_f * (g_f.astype(F32) * s_f.astype(F32))))
            t_b = _sum_all(dlb_t[h] * ppd + w_q_b * qf * dq_b + pos * kf * dk_b
                           + float(c) * (cd_b * (g_b.astype(F32) * s_b.astype(F32))))
            dlg += jnp.where((row8 == 0) & (lane8 == h), t_f, 0.0) + jnp.where((row8 == 1) & (lane8 == h), t_b, 0.0)
            du_ref[4, rows, sl] = _rope_bwd(dq + dq_f + dq_b, co, ra, rb).astype(BF16)
            du_ref[5, rows, sl] = (_rope_bwd(dk + dk_f + dk_b, co, ra, rb) * k_scale).astype(BF16)
            du_ref[6, rows, sl] = dv.astype(BF16)
        dlg_ref[...] += dlg

        @pl.when(i == nc // MIX_CHUNKS - 1)
        def _():
            dlg_ref[...] = dlg_ref[...] * lg_ref[...]

    def seg(gi):
        return pl.BlockSpec((None, MIX_ROWS, s), lambda i: (gi, i, 0))

    per = MIX_ROWS // 8
    n8 = t // 8
    row = pl.BlockSpec((MIX_ROWS, s), lambda i: (i, 0))
    st = pl.BlockSpec((MIX_CHUNKS, n_heads, HEAD_DIM, HEAD_DIM), lambda i: (i, 0, 0, 0))
    tab = pl.BlockSpec((MIX_ROWS, HEAD_DIM), lambda i: (i, 0))
    hc = _full((n_heads, c, HEAD_DIM))
    cc_ = _full((n_heads, c, c))
    h8 = _full((n_heads, 8, HEAD_DIM))
    return pl.pallas_call(
        body, name=name, grid=(nc // MIX_CHUNKS,),
        in_specs=[seg(0), seg(2), row,
                  pl.BlockSpec((8, s), lambda i: (jnp.maximum(i * per - 1, 0), 0)),
                  pl.BlockSpec((8, s), lambda i: (jnp.minimum((i + 1) * per, n8 - 1), 0)),
                  row, row, row, seg(0), seg(1), seg(2), row, st, st, st, st, cc_, cc_, cc_, hc, hc, hc, hc, h8, h8,
                  _full((8, HEAD_DIM)), tab, tab, tab, _full((3, s))],
        out_specs=[pl.BlockSpec((8, MIX_ROWS, s), lambda i: (0, i, 0)), _full((8, s)), _full((8, HEAD_DIM))],
        out_shape=[jax.ShapeDtypeStruct((8, t, s), BF16), jax.ShapeDtypeStruct((8, s), F32),
                   jax.ShapeDtypeStruct((8, HEAD_DIM), F32)],
        compiler_params=_params(("arbitrary",)),
    )(u, u, g, g, g, dz, db, drz, qkv, qkv, qkv, do, sf, sb, gf, gb, tabs["dc"], tabs["dlf"], tabs["dlb"],
      tabs["qf"], tabs["kf"], tabs["qb"], tabs["kb"], tabs["cdf"], tabs["cdb"], tabs["lg"], cos, sa, sb_tab, conv_w)


def _in_proj_bwd(du, wgs, tm, gs, name, after=()):
    n_seg, t, s = du.shape
    d = wgs[0].shape[1]
    n_w = len(wgs)
    widths = [w.shape[2] for w in wgs]
    assert sum(widths) == s

    def body(a_ref, *rest):
        w_refs, o_ref = rest[:n_w], rest[-1]
        g = pl.program_id(1)
        part = None
        for j in range(gs):
            col = 0
            for w_ref, width in zip(w_refs, widths):
                term = _dot(a_ref[j, :, col:col + width], w_ref[j], NT)
                part = term if part is None else part + term
                col += width

        @pl.when(g == 0)
        def _():
            o_ref[...] = part

        @pl.when(g > 0)
        def _():
            o_ref[...] += part

    return pl.pallas_call(
        body, name=name, grid=(t // tm, n_seg // gs),
        in_specs=[pl.BlockSpec((gs, tm, s), lambda i, g: (g, i, 0))]
        + [pl.BlockSpec((gs, d, width), lambda i, g: (g, 0, 0)) for width in widths]
        + [pl.BlockSpec(memory_space=pl.ANY)] * len(after),
        out_specs=pl.BlockSpec((tm, d), lambda i, g: (i, 0)),
        out_shape=jax.ShapeDtypeStruct((t, d), F32),
        compiler_params=_params(("parallel", "arbitrary")))(du, *wgs, *after)


def _prenorm_bwd_first(dhx, x, ctx, dxo, nw, mod, name):
    n_lat, d = x.shape
    t = n_lat + ctx.shape[0]
    nxb = n_lat // ROW_TILE

    def body(dh_ref, x_ref, c_ref, dxo_ref, nw_ref, mod_ref, dx_ref, acc_ref):
        i = pl.program_id(0)

        @pl.when(i == 0)
        def _():
            acc_ref[...] = jnp.zeros_like(acc_ref)

        ctx = i >= nxb
        m = mod_ref[...]
        scale1 = 1.0 + jnp.where(ctx, m[4:5], m[1:2])
        x = jnp.where(ctx, c_ref[...], x_ref[...])
        nw_v = nw_ref[...]
        r = lax.rsqrt(jnp.mean(x * x, axis=-1, keepdims=True) + EPS)
        xn = x * r
        dh = dh_ref[...]
        dshift = jnp.sum(dh, axis=0, keepdims=True)
        dscale = jnp.sum(dh * (xn * nw_v), axis=0, keepdims=True)
        acc_ref[6:7, :] += jnp.sum(dh * scale1 * xn, axis=0, keepdims=True)
        dxn = dh * (nw_v * scale1)
        dx = dxo_ref[...] + r * (dxn - xn * jnp.mean(dxn * xn, axis=-1, keepdims=True))

        @pl.when(i < nxb)
        def _():
            acc_ref[0:1, :] += dshift
            acc_ref[1:2, :] += dscale
            dx_ref[...] = dx

        @pl.when(i >= nxb)
        def _():
            acc_ref[3:4, :] += dshift
            acc_ref[4:5, :] += dscale

    row = pl.BlockSpec((ROW_TILE, d), lambda i: (i, 0))
    lat, cx = _split_rows(nxb, t // ROW_TILE)(d)
    acc = _full((8, d))
    return pl.pallas_call(body, name=name, grid=(t // ROW_TILE,),
                          in_specs=[row, lat, cx, row, _full((1, d)), acc],
                          out_specs=[lat, acc],
                          out_shape=[jax.ShapeDtypeStruct((n_lat, d), F32), jax.ShapeDtypeStruct((8, d), F32)],
                          compiler_params=_params(("arbitrary",)))(dhx, x, ctx, dxo, nw, mod)


def _adamw(g, w, m, v):
    m = ADAM_B1 * m + (1.0 - ADAM_B1) * g
    v = ADAM_B2 * v + (1.0 - ADAM_B2) * jnp.square(g)
    m_hat = m / (1.0 - ADAM_B1 ** ADAM_STEP)
    v_hat = v / (1.0 - ADAM_B2 ** ADAM_STEP)
    delta = -ADAM_LR * (m_hat / (jnp.sqrt(v_hat) + ADAM_EPS) + ADAM_WD * w)
    return delta, m, v


def _sum_adamw(parts, w, m, v, name, row0=0, into=None):
    n_p, r, n = parts.shape
    r_all = w.shape[0]
    part_block_bytes = 4 * 1024 * 1024
    br = 8
    for cand in (512, 256, 128, 64, 32, 16):
        if r % cand == 0 and row0 % cand == 0 and n_p * cand * n * parts.dtype.itemsize <= part_block_bytes:
            br = cand
            break
    blk0 = row0 // br

    def body(p_ref, w_ref, m_ref, v_ref, *rest):
        g_out, d_out, m_out, v_out = rest[-4:]
        g = p_ref[0].astype(F32)
        for j in range(1, n_p):
            g = g + p_ref[j].astype(F32)
        g_out[...] = g
        d_out[...], m_out[...], v_out[...] = _adamw(g, w_ref[...], m_ref[...], v_ref[...])

    row = pl.BlockSpec((br, n), lambda i: (i + blk0, 0))
    kept = [] if into is None else list(into)
    return pl.pallas_call(body, name=name, grid=(r // br,),
                          in_specs=[pl.BlockSpec((n_p, br, n), lambda i: (0, i, 0)), row, row, row]
                          + [pl.BlockSpec(memory_space=pl.ANY)] * len(kept),
                          out_specs=[row] * 4, out_shape=[jax.ShapeDtypeStruct((r_all, n), F32)] * 4,
                          input_output_aliases={4 + j: j for j in range(len(kept))},
                          compiler_params=_params(("parallel",)))(parts, w, m, v, *kept)


def _rope_tables(n_lat, n_ctx):
    f = HEAD_DIM // 4
    rows = n_lat // GRID_W
    inv = ROPE_BASE ** (-jnp.arange(f, dtype=F32) / f)
    ang_r = jnp.arange(rows).astype(F32)[:, None] * inv[None, :]
    ang_c = jnp.arange(GRID_W).astype(F32)[:, None] * inv[None, :]

    cr, sr, cc, sc = jnp.cos(ang_r), jnp.sin(ang_r), jnp.cos(ang_c), jnp.sin(ang_c)
    zr, zc = jnp.zeros_like(cr), jnp.zeros_like(cc)

    def table(by_row, by_col):
        both = by_row[:, None, :] + by_col[None, :, :]
        return both.reshape(n_lat, HEAD_DIM)

    cos = table(jnp.concatenate([cr, cr, zr, zr], axis=-1), jnp.concatenate([zc, zc, cc, cc], axis=-1))
    sa = table(jnp.concatenate([-sr, zr, zr, zr], axis=-1), jnp.concatenate([zc, zc, -sc, zc], axis=-1))
    sb = table(jnp.concatenate([zr, sr, zr, zr], axis=-1), jnp.concatenate([zc, zc, zc, sc], axis=-1))
    pad = jnp.zeros((n_ctx, HEAD_DIM), F32)
    return (jnp.concatenate([cos, pad + 1.0], axis=0), jnp.concatenate([sa, pad], axis=0),
            jnp.concatenate([sb, pad], axis=0))


def _pad_rows(a, rows):
    return jnp.pad(a, [(0, rows - a.shape[0])] + [(0, 0)] * (a.ndim - 1))


def _pad_cols(a, cols):
    return jnp.pad(a, [(0, 0), (0, cols - a.shape[1])])


def kernel(x, c, ctx, c_ctx, norm_w, w_mod, b_mod, w_in, conv_w, conv_norm_w, ret_norm_w, ret_decay_f, ret_decay_b, w_out, final_norm_w, loss_target, m_c_ctx, m_norm_w, m_w_mod, m_b_mod, m_w_in, m_conv_w, m_conv_norm_w, m_ret_norm_w, m_ret_decay_f, m_ret_decay_b, m_w_out, m_final_norm_w, v_c_ctx, v_norm_w, v_w_mod, v_b_mod, v_w_in, v_conv_w, v_conv_norm_w, v_ret_norm_w, v_ret_decay_f, v_ret_decay_b, v_w_out, v_final_norm_w):
    depth = norm_w.shape[0]
    n_lat, d = x.shape[1], x.shape[2]
    n_ctx = ctx.shape[1]
    s = d // 2
    n_heads = ret_decay_f.shape[1]
    nx, ncc = n_lat // CHUNK, n_ctx // CHUNK
    n_mod = w_mod.shape[2]
    n_cw = conv_w.shape[2]
    r_out = w_out.shape[1]
    assert s == n_heads * HEAD_DIM and w_in.shape[2] == s and N_DEV * r_out == d
    assert n_lat % ROW_TILE == 0 and n_ctx % ROW_TILE == 0 and 3 * depth * n_cw <= d and N_DEV * n_mod == 3 * d
    me = 4 * lax.axis_index("x") + 2 * lax.axis_index("y") + lax.axis_index("c")

    w_in_bf = [w_in[l].astype(BF16) for l in range(depth)]
    w_out_bf = [w_out[l].astype(BF16) for l in range(depth)]

    first = jnp.concatenate([c.reshape(1, d), _pad_cols(conv_w.reshape(1, -1), d), jnp.zeros((6, d), F32)], axis=0)
    (first_g,) = _all_gather([first], "gather_cond")
    first_g = first_g.reshape(N_DEV, 8, d)
    c_all = first_g[:, 0, :]
    conv_full = first_g[:, 1, :3 * depth * n_cw].reshape(N_DEV, depth, 3, n_cw)
    conv_full = conv_full.transpose(1, 2, 0, 3).reshape(depth, 3, N_DEV * n_cw)
    c9 = jnp.concatenate([c_all, c_ctx.reshape(1, d), jnp.zeros((7, d), F32)], axis=0)

    b_sh = lax.dynamic_slice(b_mod, (0, me * n_mod), (depth, n_mod))
    mod_sh = jnp.concatenate([_mod_rows(c9, w_mod[l], b_sh[l:l + 1], f"mod_rows_l{l}") for l in range(depth)], axis=0)
    (mod_g,) = _all_gather([mod_sh], "gather_mod")
    mod_g = mod_g.reshape(N_DEV, depth, 16, n_mod)
    mods = []
    for l in range(depth):
        mine = lax.dynamic_index_in_dim(mod_g[:, l], me, axis=1, keepdims=False).reshape(3, d)
        cx = mod_g[:, l, 8, :].reshape(3, d)
        mods.append(jnp.concatenate([mine, cx, jnp.zeros((2, d), F32)], axis=0))

    halves = [w_in_bf[0][:, :s // 2], w_in_bf[0][:, s // 2:]]
    near, order = [], [mod_g]
    for j, part in enumerate(halves):
        near.append(_push_start([part], [_landing(part, me)], "near", f"w_in0_start_{j}", after=order))
        order = near[-1][4:]
    pending = []
    for k in range(depth):
        srcs = [w_out_bf[k]] + ([w_in_bf[k]] if k > 0 else [])
        started = _push_start(srcs, [_landing(a, me) for a in srcs], "gather", f"weights_start_l{k}", after=order)
        pending.append(started[:4])
        order = started[4:]
    w_in_g = [None] * depth
    w_out_g = [None] * depth

    cos, sa, sb_tab = _rope_tables(n_lat, n_ctx)
    t_all = n_lat + n_ctx

    saved = []
    xt = hx_next = None
    for l in range(depth):
        tiles = _tiles(l, t_all, d)
        names = ["dc", "dlf", "dlb", "qf", "kf", "qb", "kb", "cdf", "cdb", "lg"]
        dec = jnp.stack([ret_decay_f[l], ret_decay_b[l]], axis=0)
        tabs = dict(zip(names, _decay_tables(dec, n_heads, f"decay_tables_l{l}")))
        if l == 0:
            hx = _prenorm_first(x[0], ctx[0], norm_w[0:1], mods[0], "prenorm_l0", after=order)
            gathered, out, after = [], None, hx
            for j in range(2):
                (landed,) = _push_wait(*near[j][:4], "near", after, f"w_in0_wait_{j}")
                relay = _push_start([], [landed], "relay", f"w_in0_relay_start_{j}")
                (landed,) = _push_wait(*relay[:4], "relay", relay[4], f"w_in0_relay_wait_{j}")
                gathered.append(landed)
                out = _in_proj(hx, landed, cos, sa, sb_tab, s, j, tiles["in_tm_half"], f"in_proj_l0_{j}", into=out)
                after = out[0]
            u, qkv = out
            w_in_g[0] = gathered
        else:
            landed = _push_wait(*pending[l], "gather", xt, f"weights_wait_l{l}")
            w_out_g[l], w_in_g[l] = landed[0].reshape(d, d), [landed[1]]
            hx = hx_next
            u, qkv = _in_proj(hx, w_in_g[l][0], cos, sa, sb_tab, s, 0, tiles["in_tm"], f"in_proj_l{l}")
        sf, sb = _state_sweep(qkv, tabs, n_heads, nx, ncc, f"state_sweep_l{l}")
        ycat, o = _mix_fwd(u, qkv, sf, sb, tabs, conv_full[l], conv_norm_w[l:l + 1], ret_norm_w[l:l + 1],
                           n_heads, nx, ncc, f"mix_fwd_l{l}")
        if l == 0:
            (landed,) = _push_wait(*pending[0], "gather", ycat, "weights_wait_l0")
            w_out_g[0] = landed.reshape(d, d)
        m_res = x_new = None
        if l < depth - 1:
            res = (x[0], ctx[0]) if l == 0 else (xt,)
            m_res, x_new, hx_next = _out_proj_prenorm(ycat, w_out_g[l], res, mods[l], norm_w[l + 1:l + 2], mods[l + 1],
                                                      n_lat, f"out_proj_l{l}")
        else:
            dxt, dm, loss_blk, dfnw, gate_acc = _out_proj_loss(ycat, w_out_g[l], xt, mods[l], loss_target[0],
                                                               final_norm_w.reshape(1, d), n_lat, f"out_proj_loss_l{l}")
        saved.append(dict(tabs=tabs, xt=xt, hx=hx, u=u, qkv=qkv, sf=sf, sb=sb, ycat=ycat, o=o, m=m_res, tiles=tiles))
        xt = x_new

    dmod_x, dmod_c, dnw, dcnw, dgnw, dconv, ddec, dwin, dwout = [], [], [], [], [], [], [], [], []
    started_token = ()
    for l in reversed(range(depth)):
        sv = saved[l]
        tiles = sv["tiles"]
        dycat = _matmul_nt(dm, w_out_g[l], tiles["ob_tn"], f"out_proj_bwd_l{l}", after=started_token)
        dwout.append(_weight_grad(sv["ycat"], dm.reshape(1, *dm.shape), tiles["wo_bm"], tiles["wo_bt"],
                                  f"w_out_grad_l{l}")[0])
        g, dz, db, drz, do, norm_acc = _mix_bwd_a(dycat, sv["u"], sv["o"], conv_full[l], conv_norm_w[l:l + 1],
                                                   ret_norm_w[l:l + 1], n_heads, nx, ncc, f"mix_bwd_a_l{l}")
        gf, gb = _grad_state_sweep(sv["qkv"], do, sv["tabs"], n_heads, nx, ncc, f"grad_state_sweep_l{l}")
        du, conv_acc, dlg = _mix_bwd_b(sv["u"], g, dz, db, drz, sv["qkv"], do, sv["sf"], sv["sb"],
                                       gf, gb, sv["tabs"], cos, sa, sb_tab, conv_full[l], n_heads, nx, ncc,
                                       f"mix_bwd_b_l{l}")
        gate_acc_l = gate_acc
        if l > 0:
            dhx = _in_proj_bwd(du, w_in_g[l], tiles["bwd_tm"], tiles["bwd_gs"], f"in_proj_bwd_l{l}")
            below = (saved[l - 1]["m"], mods[l - 1])
            dwin_l, dxt, pre_acc, dm, gate_acc = _weight_grad_beside_prenorm_bwd(
                sv["hx"], du, dhx, sv["xt"], dxt, norm_w[l:l + 1], mods[l], below, n_lat, f"w_in_grad_l{l}")
        else:
            dwin_l = _weight_grad(sv["hx"], du, tiles["wg_bm"], tiles["wg_bt"], f"w_in_grad_l{l}")
        srcs = [dwin_l, dwout[-1].reshape(N_DEV, r_out, d)]
        lands = [_landing(lax.dynamic_index_in_dim(a, me, axis=0, keepdims=False), me) for a in srcs]
        started = _push_start(srcs, lands, "scatter", f"grads_start_l{l}")
        dwin.append(started[:4])
        started_token = started[4:]
        if l == 0:
            dhx = _in_proj_bwd(du, w_in_g[l], tiles["bwd_tm"], tiles["bwd_gs"], f"in_proj_bwd_l{l}", after=started[4:])
            dxt, pre_acc = _prenorm_bwd_first(dhx, x[0], ctx[0], dxt, norm_w[l:l + 1], mods[l], f"prenorm_bwd_l{l}")
        dmod_x.append(jnp.concatenate([pre_acc[0], pre_acc[1], gate_acc_l[2]]))
        dmod_c.append(jnp.concatenate([pre_acc[3], pre_acc[4], gate_acc_l[5]]))
        dnw.append(pre_acc[6])
        dcnw.append(norm_acc[0])
        dgnw.append(norm_acc[1])
        dconv.append(conv_acc[0:3])
        ddec.append(dlg[0:2, :n_heads])
    for lst in (dmod_x, dmod_c, dnw, dcnw, dgnw, dconv, ddec, dwin, dwout):
        lst.reverse()
    grad_x = dxt.reshape(1, n_lat, d)

    rows = []
    for l in range(depth):
        rows += [dmod_x[l], dmod_c[l]]
    (dmod_g,) = _all_gather([_pad_rows(jnp.stack(rows, axis=0), 8)], "gather_dmod")
    dmod_g = dmod_g.reshape(N_DEV, 8, 3 * d)
    mine_cols = lax.dynamic_slice(dmod_g, (0, 0, me * n_mod), (N_DEV, 8, n_mod))
    g_wmod, dcc = [], jnp.zeros((d,), F32)
    for l in range(depth):
        gw, dc_part = _mod_grads(mine_cols[:, 2 * l], mine_cols[:, 2 * l + 1], c9, w_mod[l], f"mod_grads_l{l}")
        g_wmod.append(gw)
        dcc = dcc + dc_part[0]

    n_small = 16
    small = jnp.concatenate([
        jnp.stack(dnw, axis=0),
        jnp.concatenate(dcnw).reshape(1, -1),
        jnp.concatenate(dgnw).reshape(1, -1),
        dfnw[0:1],
        dcc.reshape(1, d),
        jnp.stack(dconv, axis=0).reshape(-1, d),
        _pad_cols(jnp.stack(ddec, axis=0).reshape(1, -1), d),
    ], axis=0)
    assert depth * s == d and small.shape[0] < n_small
    n_rows = small.shape[0]
    small = jnp.concatenate([small, _pad_cols(loss_blk[0:1], d)], axis=0)
    (small_g,) = _all_gather([_pad_rows(small, n_small)], "gather_small")
    small_g = small_g.reshape(N_DEV, n_small, d)

    def pack_small(nw_, cn_, gn_, fn_, cc_, df_, db_):
        return _pad_rows(jnp.concatenate([
            nw_, cn_.reshape(1, -1), gn_.reshape(1, -1), fn_.reshape(1, d), cc_.reshape(1, d),
            jnp.zeros((n_rows - depth - 5, d), F32),
            _pad_cols(jnp.stack([df_, db_], axis=1).reshape(1, -1), d)], axis=0), n_small)

    w_s = pack_small(norm_w, conv_norm_w, ret_norm_w, final_norm_w, c_ctx, ret_decay_f, ret_decay_b)
    m_s = pack_small(m_norm_w, m_conv_norm_w, m_ret_norm_w, m_final_norm_w, m_c_ctx, m_ret_decay_f, m_ret_decay_b)
    v_s = pack_small(v_norm_w, v_conv_norm_w, v_ret_norm_w, v_final_norm_w, v_c_ctx, v_ret_decay_f, v_ret_decay_b)
    small_out = _sum_adamw(small_g, w_s, m_s, v_s, "adamw_small")
    loss = small_out[0][n_rows, 0]

    def unpack_small(a):
        nw_ = a[0:depth]
        cn_ = a[depth].reshape(depth, s)
        gn_ = a[depth + 1].reshape(depth, s)
        fn_ = a[depth + 2]
        cc_ = a[depth + 3]
        dd = a[n_rows - 1, :depth * 2 * n_heads].reshape(depth, 2, n_heads)
        return dict(c_ctx=cc_, norm_w=nw_, conv_norm_w=cn_, ret_norm_w=gn_, ret_decay_f=dd[:, 0], ret_decay_b=dd[:, 1],
                    final_norm_w=fn_)

    res = {}
    for kind, arr in zip(("grad", "delta", "m", "v"), small_out):
        for k_, val in unpack_small(arr).items():
            res[(kind, k_)] = val

    bm_parts = jnp.concatenate([dmod_g[:, 0:2 * depth:2].reshape(N_DEV, depth, 3 * d),
                                dmod_g[:, 1:2 * depth:2].reshape(N_DEV, depth, 3 * d)], axis=0)
    bm_parts = jnp.concatenate([bm_parts, jnp.zeros((2 * N_DEV, 8 - depth, 3 * d), F32)], axis=1)
    pad8 = lambda a: _pad_rows(a, 8)
    bm_out = _sum_adamw(bm_parts, pad8(b_mod), pad8(m_b_mod), pad8(v_b_mod), "adamw_b_mod")
    for kind, arr in zip(("grad", "delta", "m", "v"), bm_out):
        res[(kind, "b_mod")] = arr[:depth]

    conv_rows = small_g[:, depth + 4:depth + 4 + 3 * depth * s // d].reshape(N_DEV, depth * 3, s)
    conv_mine = lax.dynamic_slice(conv_rows, (0, 0, me * n_cw), (N_DEV, depth * 3, n_cw))
    conv_mine = jnp.concatenate([conv_mine, jnp.zeros((N_DEV, 8 - depth * 3, n_cw), F32)], axis=1)
    cw2 = lambda a: _pad_rows(a.reshape(depth * 3, n_cw), 8)
    cw_out = _sum_adamw(conv_mine, cw2(conv_w), cw2(m_conv_w), cw2(v_conv_w), "adamw_conv_w")
    for kind, arr in zip(("grad", "delta", "m", "v"), cw_out):
        res[(kind, "conv_w")] = arr[:depth * 3].reshape(depth, 3, n_cw)

    wm_out = _sum_adamw(jnp.stack(g_wmod, axis=0).reshape(1, depth * d, n_mod), w_mod.reshape(depth * d, n_mod),
                        m_w_mod.reshape(depth * d, n_mod), v_w_mod.reshape(depth * d, n_mod), "adamw_w_mod")
    for kind, arr in zip(("grad", "delta", "m", "v"), wm_out):
        res[(kind, "w_mod")] = arr.reshape(depth, d, n_mod)

    wi_out = wo_out = None
    after = wm_out[0]
    for l in reversed(range(depth)):
        win_parts, wout_parts = _push_wait(*dwin[l], "scatter", after, f"grads_wait_l{l}")
        wi_out = _sum_adamw(win_parts, w_in.reshape(depth * d, s), m_w_in.reshape(depth * d, s),
                            v_w_in.reshape(depth * d, s), f"adamw_w_in_l{l}", row0=l * d, into=wi_out)
        wo_out = _sum_adamw(wout_parts, w_out.reshape(depth * r_out, d), m_w_out.reshape(depth * r_out, d),
                            v_w_out.reshape(depth * r_out, d), f"adamw_w_out_l{l}", row0=l * r_out, into=wo_out)
        after = wo_out[0]
    for kind, arr in zip(("grad", "delta", "m", "v"), wi_out):
        res[(kind, "w_in")] = arr.reshape(depth, d, s)
    for kind, arr in zip(("grad", "delta", "m", "v"), wo_out):
        res[(kind, "w_out")] = arr.reshape(depth, r_out, d)

    order = ["c_ctx", "norm_w", "w_mod", "b_mod", "w_in", "conv_w", "conv_norm_w", "ret_norm_w", "ret_decay_f",
             "ret_decay_b", "w_out", "final_norm_w"]
    outs = [loss, grad_x]
    for kind in ("grad", "delta", "m", "v"):
        outs += [res[(kind, k_)] for k_ in order]
    return tuple(outs)
```

```python
import jax
import jax.numpy as jnp
from jax import lax
from jax.experimental import pallas as pl
from jax.experimental.pallas import tpu as pltpu

F32 = jnp.float32
BF16 = jnp.bfloat16

EPS = 1e-6
CHUNK = 128
HEAD_DIM = 128
GRID_W = 64
ROPE_BASE = 10000.0
N_DEV = 8
ADAM_LR, ADAM_B1, ADAM_B2, ADAM_EPS, ADAM_WD, ADAM_STEP = 0.001, 0.9, 0.999, 1e-08, 0.01, 10

ROW_TILE = 256
V7X_VMEM_LIMIT = 56 * 1024 * 1024

NN = ((1,), (0,))
NT = ((1,), (1,))
TN = ((0,), (0,))


def _dot(a, b, dims):
    return lax.dot_general(a, b, (dims, ((), ())), preferred_element_type=F32)


def _params(sem=None):
    if sem is None:
        return pltpu.CompilerParams(vmem_limit_bytes=V7X_VMEM_LIMIT)
    return pltpu.CompilerParams(dimension_semantics=sem, vmem_limit_bytes=V7X_VMEM_LIMIT)


def _silu(z):
    return z * jax.nn.sigmoid(z)


def _dsilu(z):
    s = jax.nn.sigmoid(z)
    return s * (1.0 + z * (1.0 - s))


def _silu_and_slope(z):
    s = jax.nn.sigmoid(z)
    return z * s, s * (1.0 + z * (1.0 - s))


def _sum_all(a):
    return jnp.sum(jnp.sum(a, axis=1, keepdims=True), axis=0, keepdims=True)


def _mm_rows(t):
    return 768 if t % 768 == 0 else ROW_TILE


def _rows_or(t, rows):
    return rows if t % rows == 0 else _mm_rows(t)


def _tiles(layer, t, d):
    return dict(in_tm=_rows_or(t, 1408), in_tm_half=_rows_or(t, 2112), bwd_gs=2, wg_bm=d, wg_bt=_mm_rows(t),
                wo_bm=d, wo_bt=_mm_rows(t), ob_tn=d, bwd_tm=_rows_or(t, 1056))


def _full(shape):
    n = len(shape)
    return pl.BlockSpec(shape, lambda *_: (0,) * n)


def _peers(x, y, c):
    return [(x, y, 1 - c), (1 - x, y, c), (x, 1 - y, c), (1 - x, 1 - y, c),
            (1 - x, y, 1 - c), (x, 1 - y, 1 - c), (1 - x, 1 - y, 1 - c)]


def _lin(p):
    return 4 * p[0] + 2 * p[1] + p[2]


def _all_gather(arrays, name):
    n_arr = len(arrays)
    space = pltpu.VMEM

    def body(*refs):
        ins, outs = refs[:n_arr], refs[n_arr:2 * n_arr]
        send_sems, recv_sems, local_sems = refs[2 * n_arr:]
        x, y, c = lax.axis_index("x"), lax.axis_index("y"), lax.axis_index("c")
        me, sibling = (x, y, c), (x, y, 1 - c)
        chips = [(1 - x, y), (x, 1 - y), (1 - x, 1 - y)]
        every = []
        locals_ = []
        for a in range(n_arr):
            m_per = ins[a].shape[0]
            out_ref = outs[a]

            def rows(p, out_ref=out_ref, m_per=m_per):
                return out_ref.at[pl.ds(_lin(p) * m_per, m_per), :]

            def copy(k, block, to, src=None, a=a, rows=rows):
                return pltpu.make_async_remote_copy(
                    src_ref=rows(block) if src is None else src, dst_ref=rows(block),
                    send_sem=send_sems.at[a, k], recv_sem=recv_sems.at[a, k],
                    device_id=to, device_id_type=pl.DeviceIdType.MESH)

            mine = pltpu.make_async_copy(ins[a], rows(me), local_sems.at[a])
            mine.start()
            locals_.append(mine)
            first = [copy(0, me, sibling, src=ins[a])]
            first += [copy(1 + j, me, (*chip, c), src=ins[a]) for j, chip in enumerate(chips)]
            for cp in first:
                cp.start()
            every.append((copy, first))
        sends = []
        for a in range(n_arr):
            copy, first = every[a]
            passed = [copy(4 + j, (*chip, c), sibling) for j, chip in enumerate(chips)]
            for j, chip in enumerate(chips):
                copy(1 + j, (*chip, c), me).wait_recv()
                passed[j].start()
            sends += first + passed
        for a in range(n_arr):
            copy, _ = every[a]
            copy(0, sibling, me).wait_recv()
            for j, chip in enumerate(chips):
                copy(4 + j, (*chip, 1 - c), me).wait_recv()
        for cp in sends:
            cp.wait_send()
        for mine in locals_:
            mine.wait()

    outs = pl.pallas_call(
        body, name=name,
        out_shape=[jax.ShapeDtypeStruct((N_DEV * a.shape[0], a.shape[1]), a.dtype) for a in arrays],
        in_specs=[pl.BlockSpec(memory_space=space)] * n_arr,
        out_specs=[pl.BlockSpec(memory_space=space)] * n_arr,
        scratch_shapes=[pltpu.SemaphoreType.DMA((n_arr, 7)), pltpu.SemaphoreType.DMA((n_arr, 7)),
                        pltpu.SemaphoreType.DMA((n_arr,))],
        compiler_params=_params(),
    )(*arrays)
    return list(outs)


_HBM = pl.BlockSpec(memory_space=pltpu.HBM)
_SEM = pl.BlockSpec(memory_space=pltpu.SEMAPHORE)
_DATAFLOW = pltpu.SideEffectType.DATAFLOW_SIDE_EFFECTING


PUSH_COPIES = {"scatter": 7, "gather": 7, "near": 4, "relay": 3}


def _push_copies(src_refs, land_refs, send_sems, recv_sems, mode):
    x, y, c = lax.axis_index("x"), lax.axis_index("y"), lax.axis_index("c")
    me, sibling = (x, y, c), (x, y, 1 - c)
    n_k = PUSH_COPIES[mode]
    out, back = [], []
    if mode == "relay":
        for k, chip in enumerate([(1 - x, y), (x, 1 - y), (1 - x, 1 - y)]):
            for a, land in enumerate(land_refs):
                sems = dict(send_sem=send_sems.at[n_k * a + k], recv_sem=recv_sems.at[n_k * a + k],
                            device_id=sibling, device_id_type=pl.DeviceIdType.MESH)
                mine = land.at[_lin((*chip, c))]
                out.append(pltpu.make_async_remote_copy(src_ref=mine, dst_ref=mine, **sems))
                back.append(pltpu.make_async_remote_copy(src_ref=mine, dst_ref=land.at[_lin((*chip, 1 - c))], **sems))
        return out, back
    for k, peer in enumerate(_peers(x, y, c)[:n_k]):
        for a, (src, land) in enumerate(zip(src_refs, land_refs)):
            sems = dict(send_sem=send_sems.at[n_k * a + k], recv_sem=recv_sems.at[n_k * a + k],
                        device_id=peer, device_id_type=pl.DeviceIdType.MESH)
            mine = src.at[_lin(peer)] if mode == "scatter" else src
            out.append(pltpu.make_async_remote_copy(src_ref=mine, dst_ref=land.at[_lin(me)], **sems))
            back.append(pltpu.make_async_remote_copy(src_ref=mine, dst_ref=land.at[_lin(peer)], **sems))
    return out, back


def _push_start(srcs, lands, mode, name, after=()):
    n_src, n = len(srcs), len(lands)
    n_buf = n_src + n
    n_in = n_buf + len(after)
    n_sem = PUSH_COPIES[mode] * n

    def body(*refs):
        send_sems, recv_sems = refs[n_in], refs[n_in + 1]
        out, _ = _push_copies(refs[:n_src], refs[n_src:n_buf], send_sems, recv_sems, mode)
        for cp in out:
            cp.start()
        token = refs[-1]
        token[...] = jnp.zeros_like(token)

    both = list(srcs) + list(lands)
    res = pl.pallas_call(
        body, name=name,
        out_shape=[pltpu.SemaphoreType.DMA((n_sem,)), pltpu.SemaphoreType.DMA((n_sem,))]
        + [pltpu.HBM(a.shape, a.dtype) for a in both] + [jax.ShapeDtypeStruct((8, 128), F32)],
        in_specs=[_HBM] * n_buf + [pl.BlockSpec(memory_space=pl.ANY)] * len(after),
        out_specs=[_SEM, _SEM] + [_HBM] * n_buf + [pl.BlockSpec(memory_space=pltpu.VMEM)],
        input_output_aliases={i: 2 + i for i in range(n_buf)},
        compiler_params=pltpu.CompilerParams(has_side_effects=_DATAFLOW),
    )(*[pltpu.with_memory_space_constraint(a, pltpu.HBM) for a in both], *after)
    return res[0], res[1], list(res[2:2 + n_src]), list(res[2 + n_src:2 + n_buf]), res[-1]


def _push_wait(send_sems, recv_sems, srcs, lands, mode, after, name):
    n_src, n = len(srcs), len(lands)
    n_buf = n_src + n

    def body(*refs):
        out, back = _push_copies(refs[:n_src], refs[n_src:n_buf], refs[n_buf], refs[n_buf + 1], mode)
        for cp in out:
            cp.wait_send()
        for cp in back:
            cp.wait_recv()

    both = list(srcs) + list(lands)
    res = pl.pallas_call(
        body, name=name,
        out_shape=[pltpu.HBM(a.shape, a.dtype) for a in both],
        in_specs=[_HBM] * n_buf + [_SEM, _SEM, pl.BlockSpec(memory_space=pl.ANY)],
        out_specs=[_HBM] * n_buf,
        input_output_aliases={i: i for i in range(n_buf)},
        compiler_params=pltpu.CompilerParams(has_side_effects=_DATAFLOW),
    )(*both, send_sems, recv_sems, after)
    return list(res[n_src:])


def _landing(own, me):
    zone = lax.empty((N_DEV,) + own.shape, own.dtype)
    return lax.dynamic_update_slice(zone, own[None], (me,) + (0,) * own.ndim)


def _mod_rows(c9, w_mod, b_sh, name):
    n = w_mod.shape[1]

    def body(c_ref, w_ref, b_ref, o_ref):
        s9 = _silu(c_ref[...]).astype(BF16)
        o_ref[...] = _dot(s9, w_ref[...].astype(BF16), NN) + b_ref[...]

    return pl.pallas_call(body, name=name, out_shape=jax.ShapeDtypeStruct((16, n), F32),
                          compiler_params=_params())(c9, w_mod, b_sh)


def _mod_grads(dm_rows, dc_rows, c9, w_mod, name):
    d, n = w_mod.shape

    def body(dm_ref, dc_ref, c_ref, w_ref, gw_ref, dc_out):
        dc = dc_ref[...]
        tot = dc[0:1]
        for j in range(1, N_DEV):
            tot = tot + dc[j:j + 1]
        row = lax.broadcasted_iota(jnp.int32, (8, n), 0)
        lower = jnp.where(row == 0, tot, 0.0)
        dmod9 = jnp.concatenate([dm_ref[...], lower], axis=0).astype(BF16)
        c9v = c_ref[...]
        s9 = _silu(c9v).astype(BF16)
        gw_ref[...] = _dot(s9, dmod9, TN)
        ds = _dot(lower.astype(BF16), w_ref[...].astype(BF16), NT)
        dc_out[...] = ds * _dsilu(c9v[8:16])

    return pl.pallas_call(body, name=name,
                          out_shape=[jax.ShapeDtypeStruct((d, n), F32), jax.ShapeDtypeStruct((8, d), F32)],
                          compiler_params=_params())(dm_rows, dc_rows, c9, w_mod)


def _decay_tables(dec, n_heads, name):
    c = CHUNK

    def body(dec_ref, dc_ref, dlf_ref, dlb_ref, qf_ref, kf_ref, qb_ref, kb_ref, cdf_ref, cdb_ref, lg_ref):
        h = pl.program_id(0)
        d = dec_ref[...]
        lane = lax.broadcasted_iota(jnp.int32, d.shape, 1)
        lg = -jnp.exp(jnp.sum(jnp.where(lane == h, d, 0.0), axis=1, keepdims=True))
        lgf, lgb = lg[0:1], lg[1:2]
        i = lax.broadcasted_iota(jnp.int32, (c, c), 0).astype(F32)
        j = lax.broadcasted_iota(jnp.int32, (c, c), 1).astype(F32)
        diff = i - j
        d_f = jnp.where(diff >= 0, jnp.exp(lgf * jnp.maximum(diff, 0.0)), 0.0)
        d_b = jnp.where(diff <= 0, jnp.exp(lgb * jnp.maximum(-diff, 0.0)), 0.0)
        dc_ref[...] = d_f + d_b
        dlf_ref[...] = diff * d_f
        dlb_ref[...] = -diff * d_b
        pos = lax.broadcasted_iota(jnp.int32, (c, HEAD_DIM), 0).astype(F32)
        qf_ref[...] = jnp.exp(lgf * (pos + 1.0))
        kf_ref[...] = jnp.exp(lgf * (c - 1.0 - pos))
        qb_ref[...] = jnp.exp(lgb * (c - pos))
        kb_ref[...] = jnp.exp(lgb * pos)
        ones = jnp.ones((8, HEAD_DIM), F32)
        cdf_ref[...] = jnp.exp(lgf * float(c)) * ones
        cdb_ref[...] = jnp.exp(lgb * float(c)) * ones

        @pl.when(h == 0)
        def _():
            lg_ref[...] = jnp.zeros_like(lg_ref)

        row8 = lax.broadcasted_iota(jnp.int32, (8, HEAD_DIM), 0)
        lane8 = lax.broadcasted_iota(jnp.int32, (8, HEAD_DIM), 1)
        lg_ref[...] += (jnp.where((row8 == 0) & (lane8 == h), lgf, 0.0)
                        + jnp.where((row8 == 1) & (lane8 == h), lgb, 0.0))

    def per_head(*tail):
        return pl.BlockSpec((None,) + tail, lambda h: (h,) + (0,) * len(tail))

    shapes = [(c, c)] * 3 + [(c, HEAD_DIM)] * 4 + [(8, HEAD_DIM)] * 2
    return pl.pallas_call(
        body, name=name, grid=(n_heads,),
        in_specs=[_full(dec.shape)],
        out_specs=[per_head(*s) for s in shapes] + [_full((8, HEAD_DIM))],
        out_shape=[jax.ShapeDtypeStruct((n_heads,) + s, F32) for s in shapes]
        + [jax.ShapeDtypeStruct((8, HEAD_DIM), F32)],
        compiler_params=_params(("arbitrary",)),
    )(dec)


def _modulate(x, nw, shift, scale):
    r = lax.rsqrt(jnp.mean(x * x, axis=-1, keepdims=True) + EPS)
    return ((x * r) * nw * (1.0 + scale) + shift).astype(BF16)


def _split_rows(nxb, nb):
    def specs(d, step=lambda i: i):
        lat = pl.BlockSpec((ROW_TILE, d), lambda i: (jnp.minimum(step(i), nxb - 1), 0))
        ctx = pl.BlockSpec((ROW_TILE, d), lambda i: (jnp.clip(step(i) - nxb, 0, nb - nxb - 1), 0))
        return lat, ctx
    return specs


def _prenorm_first(x, ctx, nw, mod, name, after=()):
    n_lat, d = x.shape
    t = n_lat + ctx.shape[0]
    nxb = n_lat // ROW_TILE

    def body(x_ref, c_ref, nw_ref, mod_ref, *rest):
        o_ref = rest[-1]
        m = mod_ref[...]
        nw_v = nw_ref[...]

        @pl.when(pl.program_id(0) < nxb)
        def _():
            o_ref[...] = _modulate(x_ref[...], nw_v, m[0:1], m[1:2])

        @pl.when(pl.program_id(0) >= nxb)
        def _():
            o_ref[...] = _modulate(c_ref[...], nw_v, m[3:4], m[4:5])

    lat, cx = _split_rows(nxb, t // ROW_TILE)(d)
    return pl.pallas_call(
        body, name=name, grid=(t // ROW_TILE,),
        in_specs=[lat, cx, _full((1, d)), _full((8, d))] + [pl.BlockSpec(memory_space=pl.ANY)] * len(after),
        out_specs=pl.BlockSpec((ROW_TILE, d), lambda i: (i, 0)), out_shape=jax.ShapeDtypeStruct((t, d), BF16),
        compiler_params=_params(("parallel",)))(x, ctx, nw, mod, *after)


def _rope_fwd(v, cos, sa, sb):
    return v * cos + pltpu.roll(v, 96, 1) * sa + pltpu.roll(v, 32, 1) * sb


def _rope_bwd(g, cos, sa, sb):
    return g * cos + pltpu.roll(g * sa, 32, 1) + pltpu.roll(g * sb, 96, 1)


N_PLAIN = 5
U_DTYPE = BF16


def _in_proj(hx, wg, cos, sa, sb, s, part, tm, name, after=(), into=None):
    t, d = hx.shape
    n_seg, _, n = wg.shape
    nb = t // tm
    k_scale = HEAD_DIM ** -0.5
    kept = [] if into is None else list(into)

    def body(a_ref, w_ref, cos_ref, sa_ref, sb_ref, *rest):
        u_ref, qkv_ref = rest[-2:]
        g = pl.program_id(1)
        acc = _dot(a_ref[...], w_ref[...], NN)

        @pl.when(g < N_PLAIN)
        def _():
            u_ref[...] = acc.astype(U_DTYPE)

        @pl.when(g == N_PLAIN + 2)
        def _():
            qkv_ref[...] = acc.astype(BF16)

        for which, scale in ((N_PLAIN, 1.0), (N_PLAIN + 1, k_scale)):
            @pl.when(g == which)
            def _(scale=scale):
                co, a, b = cos_ref[...], sa_ref[...], sb_ref[...]
                for h in range(n // HEAD_DIM):
                    sl = slice(h * HEAD_DIM, (h + 1) * HEAD_DIM)
                    qkv_ref[:, sl] = (_rope_fwd(acc[:, sl], co, a, b) * scale).astype(BF16)

    def w_seg(g):
        return jnp.where(g < N_PLAIN - 1, g, jnp.where(g == N_PLAIN - 1, n_seg - 1, g - 1))

    def qkv_at(i, g):
        held = (jnp.where(i == 0, 0, 2), jnp.maximum(i - 1, 0))
        return (jnp.where(g < N_PLAIN, held[0], g - N_PLAIN), jnp.where(g < N_PLAIN, held[1], i), part)

    tab = pl.BlockSpec((tm, HEAD_DIM), lambda i, g: (i, 0))
    hbm = pl.BlockSpec(memory_space=pl.ANY)
    return pl.pallas_call(
        body, name=name, grid=(nb, n_seg),
        in_specs=[pl.BlockSpec((tm, d), lambda i, g: (i, 0)), pl.BlockSpec((None, d, n), lambda i, g: (w_seg(g), 0, 0)),
                  tab, tab, tab] + [hbm] * (len(after) + len(kept)),
        out_specs=[pl.BlockSpec((None, tm, n), lambda i, g: (jnp.minimum(g, N_PLAIN - 1), i, part)),
                   pl.BlockSpec((None, tm, n), qkv_at)],
        out_shape=[jax.ShapeDtypeStruct((N_PLAIN, t, s), U_DTYPE), jax.ShapeDtypeStruct((3, t, s), BF16)],
        input_output_aliases={5 + len(after) + j: j for j in range(len(kept))},
        compiler_params=_params(("arbitrary", "arbitrary")))(hx, wg, cos, sa, sb, *after, *kept)


def _pair_sweep(xs, ys, tab_f, tab_b, cdf, cdb, n_heads, nx, ncc, reverse, name):
    t, s = xs[0].shape[-2:]
    nc = nx + ncc
    c = CHUNK
    n_pair = nc // 2
    assert nx % 2 == 0 and ncc % 2 == 0

    def f_pair(i):
        step = n_pair - 1 - i if reverse else i
        return jnp.where(step < ncc // 2, nx // 2 + step, step - ncc // 2)

    def b_pair(i):
        return i if reverse else n_pair - 1 - i

    f_subs = (1, 0) if reverse else (0, 1)
    b_subs = (0, 1) if reverse else (1, 0)

    n_slot = 3

    def body(x_any, y_any, tf, tb, cdf_ref, cdb_ref, sf_out, sb_out, sf, sb, xf_buf, yf_buf, xb_buf, yb_buf, sems):
        i = pl.program_id(0)

        def fetch(step):
            slot = step % n_slot
            copies = []
            for k, (arr, ref, pair, buf) in enumerate(((xs, x_any, f_pair, xf_buf), (ys, y_any, f_pair, yf_buf),
                                                       (xs, x_any, b_pair, xb_buf), (ys, y_any, b_pair, yb_buf))):
                rows = pl.ds(pl.multiple_of(pair(step) * 2 * c, 2 * c), 2 * c)
                src = ref.at[rows, :] if arr[1] is None else ref.at[arr[1], rows, :]
                copies.append(pltpu.make_async_copy(src, buf.at[slot], sems.at[k, slot]))
            return copies

        @pl.when(i == 0)
        def _():
            sf[...] = jnp.zeros_like(sf)
            sb[...] = jnp.zeros_like(sb)
            for cp in fetch(0) + fetch(1):
                cp.start()

        @pl.when(i + 2 < n_pair)
        def _():
            for cp in fetch(i + 2):
                cp.start()

        for cp in fetch(i):
            cp.wait()
        slot = i % n_slot
        xf_ref, yf_ref, xb_ref, yb_ref = xf_buf.at[slot], yf_buf.at[slot], xb_buf.at[slot], yb_buf.at[slot]

        for step in range(2):
            for x_ref, y_ref, tab, cd, out, st, sub in ((xf_ref, yf_ref, tf, cdf_ref, sf_out, sf, f_subs[step]),
                                                        (xb_ref, yb_ref, tb, cdb_ref, sb_out, sb, b_subs[step])):
                rows = pl.ds(sub * c, c)
                for h in range(n_heads):
                    sl = pl.ds(h * HEAD_DIM, HEAD_DIM)
                    out[sub, h] = st[h].astype(BF16)
                    xd = (x_ref[rows, sl].astype(F32) * tab[h]).astype(BF16)
                    st[h] = cd[h][0:1, :] * st[h] + _dot(xd, y_ref[rows, sl], TN)

    assert n_pair >= 2
    st_blk = (2, n_heads, HEAD_DIM, HEAD_DIM)
    ring = pltpu.VMEM((n_slot, 2 * c, s), BF16)
    return pl.pallas_call(
        body, name=name, grid=(n_pair,),
        in_specs=[pl.BlockSpec(memory_space=pl.ANY), pl.BlockSpec(memory_space=pl.ANY),
                  _full((n_heads, c, HEAD_DIM)), _full((n_heads, c, HEAD_DIM)),
                  _full((n_heads, 8, HEAD_DIM)), _full((n_heads, 8, HEAD_DIM))],
        out_specs=[pl.BlockSpec(st_blk, lambda i: (f_pair(i), 0, 0, 0)), pl.BlockSpec(st_blk, lambda i: (b_pair(i), 0, 0, 0))],
        out_shape=[jax.ShapeDtypeStruct((nc, n_heads, HEAD_DIM, HEAD_DIM), BF16)] * 2,
        scratch_shapes=[pltpu.VMEM((n_heads, HEAD_DIM, HEAD_DIM), F32)] * 2 + [ring] * 4
        + [pltpu.SemaphoreType.DMA((4, n_slot))],
        compiler_params=_params(("arbitrary",)),
    )(xs[0], ys[0], tab_f, tab_b, cdf, cdb)


def _state_sweep(qkv, tabs, n_heads, nx, ncc, name):
    return _pair_sweep((qkv, 1), (qkv, 2), tabs["kf"], tabs["kb"], tabs["cdf"], tabs["cdb"], n_heads, nx, ncc, False, name)


MIX_CHUNKS = 2
MIX_ROWS = MIX_CHUNKS * CHUNK


HALO = 16


def _halo_specs(s, t):
    per = MIX_ROWS // HALO
    n_halo = t // HALO

    def prev(g):
        return pl.BlockSpec((None, HALO, s), lambda i: (g, jnp.maximum(i * per - 1, 0), 0))

    def nxt(g):
        return pl.BlockSpec((None, HALO, s), lambda i: (g, jnp.minimum((i + 1) * per, n_halo - 1), 0))

    return prev, nxt


def _conv_input(h_ref, c_ref, hp_ref, hn_ref, cp_ref, cn_ref):
    a = c_ref[...].astype(F32) * h_ref[...].astype(F32)
    before = cp_ref[HALO - 1:HALO].astype(F32) * hp_ref[HALO - 1:HALO].astype(F32)
    after = cn_ref[0:1].astype(F32) * hn_ref[0:1].astype(F32)
    return a, before, after


def _shifted(a, before, after, has_prev, has_next):
    rows = a.shape[0]
    rowi = lax.broadcasted_iota(jnp.int32, a.shape, 0)
    am = jnp.where(rowi == 0, jnp.where(has_prev, before, 0.0), pltpu.roll(a, 1, 0))
    ap = jnp.where(rowi == rows - 1, jnp.where(has_next, after, 0.0), pltpu.roll(a, rows - 1, 0))
    return am, ap


def _neighbours(i, nx, nc):
    nxb, ncb = nx // MIX_CHUNKS, nc // MIX_CHUNKS
    return (i != 0) & (i != nxb), (i != nxb - 1) & (i != ncb - 1)


def _mix_fwd(u, qkv, sf, sb, tabs, conv_w, cnw, gnw, n_heads, nx, ncc, name):
    _, t, s = u.shape
    nc = nx + ncc
    c = CHUNK
    assert nx % MIX_CHUNKS == 0 and ncc % MIX_CHUNKS == 0

    def body(h_ref, b_ref, c_ref, z_ref, rz_ref, hp_ref, hn_ref, cp_ref, cn_ref, q_ref, k_ref, v_ref,
             sf_ref, sb_ref, dc_ref, qft, qbt, w_ref, cnw_ref, gnw_ref, y_ref, o_ref):
        i = pl.program_id(0)
        has_prev, has_next = _neighbours(i, nx, nc)
        a, before, after = _conv_input(h_ref, c_ref, hp_ref, hn_ref, cp_ref, cn_ref)
        am, ap = _shifted(a, before, after, has_prev, has_next)
        w = w_ref[...]
        y0 = w[0:1] * am + w[1:2] * a + w[2:3] * ap
        yb = b_ref[...].astype(F32) * y0
        r = lax.rsqrt(jnp.mean(yb * yb, axis=-1, keepdims=True) + EPS)
        y_ref[:, pl.ds(0, s)] = (_silu(z_ref[...].astype(F32)) * ((yb * r) * cnw_ref[...])).astype(BF16)
        for sub in range(MIX_CHUNKS):
            rows = pl.ds(sub * c, c)
            for h in range(n_heads):
                sl = pl.ds(h * HEAD_DIM, HEAD_DIM)
                q, k, v = q_ref[rows, sl], k_ref[rows, sl], v_ref[rows, sl]
                p = (_dot(q, k, NT) * dc_ref[h]).astype(BF16)
                o = _dot(p, v, NN)
                qf = q.astype(F32)
                o += _dot((qf * qft[h]).astype(BF16), sf_ref[sub, h], NN)
                o += _dot((qf * qbt[h]).astype(BF16), sb_ref[sub, h], NN)
                o_ref[rows, sl] = o
                mu = jnp.mean(o, axis=-1, keepdims=True)
                var = jnp.mean(jnp.square(o - mu), axis=-1, keepdims=True)
                on = (o - mu) * lax.rsqrt(var + EPS)
                y_ref[rows, pl.ds(s + h * HEAD_DIM, HEAD_DIM)] = (
                    _silu(rz_ref[rows, sl].astype(F32)) * (on * gnw_ref[:, sl])).astype(BF16)

    def seg(g):
        return pl.BlockSpec((None, MIX_ROWS, s), lambda i: (g, i, 0))

    prev, nxt = _halo_specs(s, t)
    row = pl.BlockSpec((MIX_ROWS, s), lambda i: (i, 0))
    st = pl.BlockSpec((MIX_CHUNKS, n_heads, HEAD_DIM, HEAD_DIM), lambda i: (i, 0, 0, 0))
    return pl.pallas_call(
        body, name=name, grid=(nc // MIX_CHUNKS,),
        in_specs=[seg(0), seg(1), seg(2), seg(3), seg(4), prev(0), nxt(0), prev(2), nxt(2), seg(0), seg(1), seg(2),
                  st, st, _full((n_heads, c, c)), _full((n_heads, c, HEAD_DIM)), _full((n_heads, c, HEAD_DIM)),
                  _full((3, s)), _full((1, s)), _full((1, s))],
        out_specs=[pl.BlockSpec((MIX_ROWS, 2 * s), lambda i: (i, 0)), row],
        out_shape=[jax.ShapeDtypeStruct((t, 2 * s), BF16), jax.ShapeDtypeStruct((t, s), F32)],
        compiler_params=_params(("parallel",)),
    )(u, u, u, u, u, u, u, u, u, qkv, qkv, qkv, sf, sb, tabs["dc"], tabs["qf"], tabs["qb"], conv_w, cnw, gnw)


def _out_proj_prenorm(ycat, w_out, res, mod, nw_next, mod_next, n_lat, name):
    t, d = ycat.shape
    nb = t // ROW_TILE
    nxb = n_lat // ROW_TILE
    split = len(res) == 2

    def body(a_ref, w_ref, *rest):
        res_refs = rest[:len(res)]
        mod_ref, nw_ref, modn_ref, m_ref, xo_ref, hx_ref, xs = rest[len(res):]
        i = pl.program_id(0)

        @pl.when(i == 0)
        def _():
            xs[...] = jnp.zeros_like(xs)

        cur_ctx = jnp.minimum(i, nb - 1) >= nxb
        prev_ctx = i - 1 >= nxb

        def step(cur, prev):
            mv, mn = mod_ref[...], modn_ref[...]
            shift = jnp.where(prev_ctx, mn[3:4], mn[0:1])
            scale = jnp.where(prev_ctx, mn[4:5], mn[1:2])
            hx_ref[...] = _modulate(xs[prev], nw_ref[...], shift, scale)
            m = _dot(a_ref[...], w_ref[...], NN)
            x_res = jnp.where(cur_ctx, res_refs[1][...], res_refs[0][...]) if split else res_refs[0][...]
            x_new = x_res + jnp.where(cur_ctx, mv[5:6], mv[2:3]) * m
            m_ref[...] = m.astype(BF16)
            xo_ref[...] = x_new
            xs[cur] = x_new

        @pl.when(i % 2 == 0)
        def _():
            step(0, 1)

        @pl.when(i % 2 == 1)
        def _():
            step(1, 0)

    cur = pl.BlockSpec((ROW_TILE, d), lambda i: (jnp.minimum(i, nb - 1), 0))
    prev = pl.BlockSpec((ROW_TILE, d), lambda i: (jnp.maximum(i - 1, 0), 0))
    res_specs = list(_split_rows(nxb, nb)(d, lambda i: jnp.minimum(i, nb - 1))) if split else [cur]
    return pl.pallas_call(
        body, name=name, grid=(nb + 1,),
        in_specs=[cur, _full((d, d))] + res_specs + [_full((8, d)), _full((1, d)), _full((8, d))],
        out_specs=[cur, cur, prev],
        out_shape=[jax.ShapeDtypeStruct((t, d), BF16), jax.ShapeDtypeStruct((t, d), F32),
                   jax.ShapeDtypeStruct((t, d), BF16)],
        scratch_shapes=[pltpu.VMEM((2, ROW_TILE, d), F32)],
        compiler_params=_params(("arbitrary",)))(ycat, w_out, *res, mod, nw_next, mod_next)


def _out_proj_loss(ycat, w_out, xt, mod, tgt, fnw, n_lat, name):
    t, d = xt.shape
    nb = t // ROW_TILE
    nxb = n_lat // ROW_TILE

    def body(a_ref, w_ref, x_ref, mod_ref, t_ref, fw_ref, dx_ref, dm_ref, loss_ref, dw_ref, gacc_ref, xs, ms):
        i = pl.program_id(0)

        @pl.when(i == 0)
        def _():
            xs[...] = jnp.zeros_like(xs)
            ms[...] = jnp.zeros_like(ms)
            loss_ref[...] = jnp.zeros_like(loss_ref)
            dw_ref[...] = jnp.zeros_like(dw_ref)
            gacc_ref[...] = jnp.zeros_like(gacc_ref)

        def step(cur, prev):
            mv = mod_ref[...]
            x_prev, m_prev = xs[prev], ms[prev]
            valid = (i >= 1) & (i - 1 < nxb)
            w = fw_ref[...]
            r = lax.rsqrt(jnp.mean(x_prev * x_prev, axis=-1, keepdims=True) + EPS)
            xn = x_prev * r
            e = xn * w - t_ref[...]
            loss = 0.5 * jnp.sum(jnp.mean(e * e, axis=-1, keepdims=True), axis=0, keepdims=True)
            loss_ref[...] += jnp.where(valid, loss, 0.0)
            dy = e * (1.0 / d)
            dw_ref[0:1, :] += jnp.where(valid, jnp.sum(dy * xn, axis=0, keepdims=True), 0.0)
            dxn = dy * w
            dx = jnp.where(valid, r * (dxn - xn * jnp.mean(dxn * xn, axis=-1, keepdims=True)), 0.0)
            dx_ref[...] = dx
            dm_ref[...] = (dx * mv[2:3]).astype(BF16)
            gacc_ref[2:3, :] += jnp.sum(dx * m_prev, axis=0, keepdims=True)

            m = _dot(a_ref[...], w_ref[...], NN)
            gate = jnp.where(jnp.minimum(i, nb - 1) >= nxb, mv[5:6], mv[2:3])
            xs[cur] = x_ref[...] + gate * m
            ms[cur] = m

        @pl.when(i % 2 == 0)
        def _():
            step(0, 1)

        @pl.when(i % 2 == 1)
        def _():
            step(1, 0)

    cur = pl.BlockSpec((ROW_TILE, d), lambda i: (jnp.minimum(i, nb - 1), 0))
    prev = pl.BlockSpec((ROW_TILE, d), lambda i: (jnp.maximum(i - 1, 0), 0))
    return pl.pallas_call(
        body, name=name, grid=(nb + 1,),
        in_specs=[cur, _full((d, d)), cur, _full((8, d)),
                  pl.BlockSpec((ROW_TILE, d), lambda i: (jnp.clip(i - 1, 0, nxb - 1), 0)), _full((1, d))],
        out_specs=[prev, prev, _full((8, HEAD_DIM)), _full((8, d)), _full((8, d))],
        out_shape=[jax.ShapeDtypeStruct((t, d), F32), jax.ShapeDtypeStruct((t, d), BF16),
                   jax.ShapeDtypeStruct((8, HEAD_DIM), F32), jax.ShapeDtypeStruct((8, d), F32),
                   jax.ShapeDtypeStruct((8, d), F32)],
        scratch_shapes=[pltpu.VMEM((2, ROW_TILE, d), F32), pltpu.VMEM((2, ROW_TILE, d), F32)],
        compiler_params=_params(("arbitrary",)))(ycat, w_out, xt, mod, tgt, fnw)


def _matmul_nt(a, w, tn, name, after=()):
    t, k = a.shape
    n = w.shape[0]
    tm = _mm_rows(t)

    def body(a_ref, w_ref, *rest):
        rest[-1][...] = _dot(a_ref[...], w_ref[...], NT)

    return pl.pallas_call(
        body, name=name, grid=(n // tn, t // tm),
        in_specs=[pl.BlockSpec((tm, k), lambda j, i: (i, 0)), pl.BlockSpec((tn, k), lambda j, i: (j, 0))]
        + [pl.BlockSpec(memory_space=pl.ANY)] * len(after),
        out_specs=pl.BlockSpec((tm, tn), lambda j, i: (i, j)),
        out_shape=jax.ShapeDtypeStruct((t, n), F32),
        compiler_params=_params(("parallel", "parallel")))(a, w, *after)


def _weight_grad(a, b, bm, bt, name):
    t, m = a.shape
    n_g, _, n = b.shape
    nt = t // bt

    def body(a_ref, b_ref, o_ref, acc):
        k = pl.program_id(2)

        @pl.when(k == 0)
        def _():
            acc[...] = jnp.zeros_like(acc)

        acc[...] += _dot(a_ref[...], b_ref[...], TN)

        @pl.when(k == nt - 1)
        def _():
            o_ref[...] = acc[...].astype(o_ref.dtype)

    return pl.pallas_call(
        body, name=name, grid=(n_g, m // bm, nt),
        in_specs=[pl.BlockSpec((bt, bm), lambda g, i, k: (k, i)), pl.BlockSpec((None, bt, n), lambda g, i, k: (g, k, 0))],
        out_specs=pl.BlockSpec((None, bm, n), lambda g, i, k: (g, i, 0)),
        out_shape=jax.ShapeDtypeStruct((n_g, m, n), BF16),
        scratch_shapes=[pltpu.VMEM((bm, n), F32)],
        compiler_params=_params(("parallel", "parallel", "arbitrary")))(a, b)


def _weight_grad_beside_prenorm_bwd(a, b, dhx, xt, dxo, nw, mod, below, n_lat, name):
    t, m = a.shape
    n_g, _, n = b.shape
    d = xt.shape[1]
    bt = _mm_rows(t)
    nt = t // bt
    rows = t // (n_g * nt)
    n_piece = 2 if rows % 32 == 0 and m % 2 == 0 else 1
    rows_p, m_p = rows // n_piece, m // n_piece
    assert rows * n_g * nt == t and rows_p % 8 == 0

    def body(a_ref, b_ref, dh_ref, x_ref, dxo_ref, nw_ref, mod_ref, m_ref, modb_ref,
             o_ref, dx_ref, acc_ref, dm_ref, gacc_ref, acc):
        g, k = pl.program_id(0), pl.program_id(1)
        step = g * nt + k

        @pl.when(step == 0)
        def _():
            acc_ref[...] = jnp.zeros_like(acc_ref)
            gacc_ref[...] = jnp.zeros_like(gacc_ref)

        @pl.when(k == 0)
        def _():
            acc[...] = jnp.zeros_like(acc)

        mv, mb, nw_v = mod_ref[...], modb_ref[...], nw_ref[...]
        for p in range(n_piece):
            rs = pl.ds(p * rows_p, rows_p)
            rowi = step * rows + p * rows_p + lax.broadcasted_iota(jnp.int32, (rows_p, 1), 0)
            ctx = rowi >= n_lat
            w_lat = jnp.where(ctx, 0.0, 1.0)
            w_ctx = 1.0 - w_lat
            scale1 = 1.0 + jnp.where(ctx, mv[4:5], mv[1:2])
            x = x_ref[rs, :]
            r = lax.rsqrt(jnp.mean(x * x, axis=-1, keepdims=True) + EPS)
            xn = x * r
            dh = dh_ref[rs, :]
            dsc = dh * (xn * nw_v)
            acc_ref[0:1, :] += jnp.sum(dh * w_lat, axis=0, keepdims=True)
            acc_ref[1:2, :] += jnp.sum(dsc * w_lat, axis=0, keepdims=True)
            acc_ref[3:4, :] += jnp.sum(dh * w_ctx, axis=0, keepdims=True)
            acc_ref[4:5, :] += jnp.sum(dsc * w_ctx, axis=0, keepdims=True)
            acc_ref[6:7, :] += jnp.sum(dh * scale1 * xn, axis=0, keepdims=True)
            dxn = dh * (nw_v * scale1)
            dx = dxo_ref[rs, :] + r * (dxn - xn * jnp.mean(dxn * xn, axis=-1, keepdims=True))
            dx_ref[rs, :] = dx
            dm_ref[rs, :] = (dx * jnp.where(ctx, mb[5:6], mb[2:3])).astype(BF16)
            dg = dx * m_ref[rs, :].astype(F32)
            gacc_ref[2:3, :] += jnp.sum(dg * w_lat, axis=0, keepdims=True)
            gacc_ref[5:6, :] += jnp.sum(dg * w_ctx, axis=0, keepdims=True)

            ms_ = pl.ds(p * m_p, m_p)
            acc[ms_, :] += _dot(a_ref[:, ms_], b_ref[...], TN)

        @pl.when(k == nt - 1)
        def _():
            o_ref[...] = acc[...].astype(o_ref.dtype)

    side = pl.BlockSpec((rows, d), lambda g, k: (g * nt + k, 0))
    acc8 = _full((8, d))
    return pl.pallas_call(
        body, name=name, grid=(n_g, nt),
        in_specs=[pl.BlockSpec((bt, m), lambda g, k: (k, 0)), pl.BlockSpec((None, bt, n), lambda g, k: (g, k, 0)),
                  side, side, side, _full((1, d)), acc8, side, acc8],
        out_specs=[pl.BlockSpec((None, m, n), lambda g, k: (g, 0, 0)), side, acc8, side, acc8],
        out_shape=[jax.ShapeDtypeStruct((n_g, m, n), BF16), jax.ShapeDtypeStruct((t, d), F32),
                   jax.ShapeDtypeStruct((8, d), F32), jax.ShapeDtypeStruct((t, d), BF16),
                   jax.ShapeDtypeStruct((8, d), F32)],
        scratch_shapes=[pltpu.VMEM((m, n), F32)],
        compiler_params=_params(("arbitrary", "arbitrary")))(a, b, dhx, xt, dxo, nw, mod, *below)


def _mix_bwd_a(dycat, u, o, conv_w, cnw, gnw, n_heads, nx, ncc, name):
    _, t, s = u.shape
    nc = nx + ncc

    def body(dy_ref, h_ref, b_ref, c_ref, z_ref, rz_ref, hp_ref, hn_ref, cp_ref, cn_ref, o_ref, w_ref,
             cnw_ref, gnw_ref, g_ref, dz_ref, db_ref, drz_ref, do_ref, acc_ref):
        i = pl.program_id(0)

        @pl.when(i == 0)
        def _():
            acc_ref[...] = jnp.zeros_like(acc_ref)

        has_prev, has_next = _neighbours(i, nx, nc)
        a, before, after = _conv_input(h_ref, c_ref, hp_ref, hn_ref, cp_ref, cn_ref)
        am, ap = _shifted(a, before, after, has_prev, has_next)
        w = w_ref[...]
        y0 = w[0:1] * am + w[1:2] * a + w[2:3] * ap
        bb = b_ref[...].astype(F32)
        yb = bb * y0
        r = lax.rsqrt(jnp.mean(yb * yb, axis=-1, keepdims=True) + EPS)
        ynn = yb * r
        z = z_ref[...].astype(F32)
        dyc = dy_ref[:, pl.ds(0, s)]
        cw = cnw_ref[...]
        sz, dsz = _silu_and_slope(z)
        dz_ref[...] = (dyc * (ynn * cw) * dsz).astype(BF16)
        dyn = dyc * sz
        acc_ref[0:1, :] += jnp.sum(dyn * ynn, axis=0, keepdims=True)
        dynn = dyn * cw
        dyb = r * (dynn - ynn * jnp.mean(dynn * ynn, axis=-1, keepdims=True))
        db_ref[...] = (dyb * y0).astype(BF16)
        g_ref[...] = dyb * bb
        for h in range(n_heads):
            sl = pl.ds(h * HEAD_DIM, HEAD_DIM)
            ov = o_ref[:, sl]
            mu = jnp.mean(ov, axis=-1, keepdims=True)
            var = jnp.mean(jnp.square(ov - mu), axis=-1, keepdims=True)
            rs = lax.rsqrt(var + EPS)
            on = (ov - mu) * rs
            dyr = dy_ref[:, pl.ds(s + h * HEAD_DIM, HEAD_DIM)]
            rz = rz_ref[:, sl].astype(F32)
            gw = gnw_ref[:, sl]
            srz, dsrz = _silu_and_slope(rz)
            drz_ref[:, sl] = (dyr * (on * gw) * dsrz).astype(BF16)
            dyg = dyr * srz
            acc_ref[1:2, sl] += jnp.sum(dyg * on, axis=0, keepdims=True)
            don = dyg * gw
            do = rs * (don - jnp.mean(don, axis=-1, keepdims=True)
                       - on * jnp.mean(don * on, axis=-1, keepdims=True))
            do_ref[:, sl] = do.astype(BF16)

    def seg(g):
        return pl.BlockSpec((None, MIX_ROWS, s), lambda i: (g, i, 0))

    prev, nxt = _halo_specs(s, t)
    row = pl.BlockSpec((MIX_ROWS, s), lambda i: (i, 0))
    return pl.pallas_call(
        body, name=name, grid=(nc // MIX_CHUNKS,),
        in_specs=[pl.BlockSpec((MIX_ROWS, 2 * s), lambda i: (i, 0)), seg(0), seg(1), seg(2), seg(3), seg(4),
                  prev(0), nxt(0), prev(2), nxt(2), row, _full((3, s)), _full((1, s)), _full((1, s))],
        out_specs=[row, row, row, row, row, _full((8, s))],
        out_shape=[jax.ShapeDtypeStruct((t, s), F32)] + [jax.ShapeDtypeStruct((t, s), BF16)] * 4
        + [jax.ShapeDtypeStruct((8, s), F32)],
        compiler_params=_params(("arbitrary",)),
    )(dycat, u, u, u, u, u, u, u, u, u, o, conv_w, cnw, gnw)


def _grad_state_sweep(qkv, do, tabs, n_heads, nx, ncc, name):
    return _pair_sweep((qkv, 0), (do, None), tabs["qf"], tabs["qb"], tabs["cdf"], tabs["cdb"], n_heads, nx, ncc, True, name)


def _mix_bwd_b(u, g, dz, db, drz, qkv, do, sf, sb, gf, gb, tabs, cos, sa, sb_tab, conv_w,
               n_heads, nx, ncc, name):
    _, t, s = u.shape
    nc = nx + ncc
    c = CHUNK
    k_scale = HEAD_DIM ** -0.5

    def body(h_ref, c_ref, g_ref, gp_ref, gn_ref, dz_ref, db_ref, drz_ref, q_ref, k_ref, v_ref, do_ref,
             sf_ref, sb_ref, gf_ref, gb_ref, dc_t, dlf_t, dlb_t, qft, kft, qbt, kbt, cdf, cdb, lg_ref,
             cos_ref, sa_ref, sb_ref2, w_ref, du_ref, dw_ref, dlg_ref):
        i = pl.program_id(0)

        @pl.when(i == 0)
        def _():
            dw_ref[...] = jnp.zeros_like(dw_ref)
            dlg_ref[...] = jnp.zeros_like(dlg_ref)

        has_prev, has_next = _neighbours(i, nx, nc)
        gv = g_ref[...]
        gm, gp = _shifted(gv, gp_ref[7:8], gn_ref[0:1], has_prev, has_next)
        w = w_ref[...]
        da = w[0:1] * gp + w[1:2] * gv + w[2:3] * gm
        hh, cc = h_ref[...].astype(F32), c_ref[...].astype(F32)
        du_ref[0] = (da * cc).astype(BF16)
        du_ref[2] = (da * hh).astype(BF16)
        a = cc * hh
        dw_ref[0:1, :] += jnp.sum(a * gp, axis=0, keepdims=True)
        dw_ref[1:2, :] += jnp.sum(a * gv, axis=0, keepdims=True)
        dw_ref[2:3, :] += jnp.sum(a * gm, axis=0, keepdims=True)
        du_ref[1] = db_ref[...]
        du_ref[3] = dz_ref[...]
        du_ref[7] = drz_ref[...]

        pos = lax.broadcasted_iota(jnp.int32, (c, HEAD_DIM), 0).astype(F32)
        w_q_f, w_q_b, w_k_f = pos + 1.0, c - pos, c - 1.0 - pos
        row8 = lax.broadcasted_iota(jnp.int32, (8, HEAD_DIM), 0)
        lane8 = lax.broadcasted_iota(jnp.int32, (8, HEAD_DIM), 1)
        dlg = jnp.zeros((8, HEAD_DIM), F32)
        for sub, h in [(sub, h) for sub in range(MIX_CHUNKS) for h in range(n_heads)]:
            rows = pl.ds(sub * c, c)
            co, ra, rb = cos_ref[rows, :], sa_ref[rows, :], sb_ref2[rows, :]
            sl = pl.ds(h * HEAD_DIM, HEAD_DIM)
            q, k, v, do = q_ref[rows, sl], k_ref[rows, sl], v_ref[rows, sl], do_ref[rows, sl]
            qf, kf, dof = q.astype(F32), k.astype(F32), do.astype(F32)
            s_f, s_b, g_f, g_b = sf_ref[sub, h], sb_ref[sub, h], gf_ref[sub, h], gb_ref[sub, h]
            p = _dot(q, k, NT)
            pd = _dot(do, v, NT)
            pdd = (pd * dc_t[h]).astype(BF16)
            dq = _dot(pdd, k, NN)
            dk = _dot(pdd, q, TN)
            dv = _dot((p * dc_t[h]).astype(BF16), do, TN)
            dq_f = _dot((dof * qft[h]).astype(BF16), s_f, NT)
            dq_b = _dot((dof * qbt[h]).astype(BF16), s_b, NT)
            dk_f = _dot(v, g_f, NT) * kft[h]
            dk_b = _dot(v, g_b, NT) * kbt[h]
            dv += _dot((kf * kft[h]).astype(BF16), g_f, NN) + _dot((kf * kbt[h]).astype(BF16), g_b, NN)
            ppd = p * pd
            cd_f, cd_b = cdf[h][0:1, :], cdb[h][0:1, :]
            t_f = _sum_all(dlf_t[h] * ppd + w_q_f * qf * dq_f + w_k_f * kf * dk_f
                           + float(c) * (cd_f * (g_f.astype(F32) * s_f.astype(F32))))
            t_b = _sum_all(dlb_t[h] * ppd + w_q_b * qf * dq_b + pos * kf * dk_b
                           + float(c) * (cd_b * (g_b.astype(F32) * s_b.astype(F32))))
            dlg += jnp.where((row8 == 0) & (lane8 == h), t_f, 0.0) + jnp.where((row8 == 1) & (lane8 == h), t_b, 0.0)
            du_ref[4, rows, sl] = _rope_bwd(dq + dq_f + dq_b, co, ra, rb).astype(BF16)
            du_ref[5, rows, sl] = (_rope_bwd(dk + dk_f + dk_b, co, ra, rb) * k_scale).astype(BF16)
            du_ref[6, rows, sl] = dv.astype(BF16)
        dlg_ref[...] += dlg

        @pl.when(i == nc // MIX_CHUNKS - 1)
        def _():
            dlg_ref[...] = dlg_ref[...] * lg_ref[...]

    def seg(gi):
        return pl.BlockSpec((None, MIX_ROWS, s), lambda i: (gi, i, 0))

    per = MIX_ROWS // 8
    n8 = t // 8
    row = pl.BlockSpec((MIX_ROWS, s), lambda i: (i, 0))
    st = pl.BlockSpec((MIX_CHUNKS, n_heads, HEAD_DIM, HEAD_DIM), lambda i: (i, 0, 0, 0))
    tab = pl.BlockSpec((MIX_ROWS, HEAD_DIM), lambda i: (i, 0))
    hc = _full((n_heads, c, HEAD_DIM))
    cc_ = _full((n_heads, c, c))
    h8 = _full((n_heads, 8, HEAD_DIM))
    return pl.pallas_call(
        body, name=name, grid=(nc // MIX_CHUNKS,),
        in_specs=[seg(0), seg(2), row,
                  pl.BlockSpec((8, s), lambda i: (jnp.maximum(i * per - 1, 0), 0)),
                  pl.BlockSpec((8, s), lambda i: (jnp.minimum((i + 1) * per, n8 - 1), 0)),
                  row, row, row, seg(0), seg(1), seg(2), row, st, st, st, st, cc_, cc_, cc_, hc, hc, hc, hc, h8, h8,
                  _full((8, HEAD_DIM)), tab, tab, tab, _full((3, s))],
        out_specs=[pl.BlockSpec((8, MIX_ROWS, s), lambda i: (0, i, 0)), _full((8, s)), _full((8, HEAD_DIM))],
        out_shape=[jax.ShapeDtypeStruct((8, t, s), BF16), jax.ShapeDtypeStruct((8, s), F32),
                   jax.ShapeDtypeStruct((8, HEAD_DIM), F32)],
        compiler_params=_params(("arbitrary",)),
    )(u, u, g, g, g, dz, db, drz, qkv, qkv, qkv, do, sf, sb, gf, gb, tabs["dc"], tabs["dlf"], tabs["dlb"],
      tabs["qf"], tabs["kf"], tabs["qb"], tabs["kb"], tabs["cdf"], tabs["cdb"], tabs["lg"], cos, sa, sb_tab, conv_w)


def _in_proj_bwd(du, wgs, tm, gs, name, after=()):
    n_seg, t, s = du.shape
    d = wgs[0].shape[1]
    n_w = len(wgs)
    widths = [w.shape[2] for w in wgs]
    assert sum(widths) == s

    def body(a_ref, *rest):
        w_refs, o_ref = rest[:n_w], rest[-1]
        g = pl.program_id(1)
        part = None
        for j in range(gs):
            col = 0
            for w_ref, width in zip(w_refs, widths):
                term = _dot(a_ref[j, :, col:col + width], w_ref[j], NT)
                part = term if part is None else part + term
                col += width

        @pl.when(g == 0)
        def _():
            o_ref[...] = part

        @pl.when(g > 0)
        def _():
            o_ref[...] += part

    return pl.pallas_call(
        body, name=name, grid=(t // tm, n_seg // gs),
        in_specs=[pl.BlockSpec((gs, tm, s), lambda i, g: (g, i, 0))]
        + [pl.BlockSpec((gs, d, width), lambda i, g: (g, 0, 0)) for width in widths]
        + [pl.BlockSpec(memory_space=pl.ANY)] * len(after),
        out_specs=pl.BlockSpec((tm, d), lambda i, g: (i, 0)),
        out_shape=jax.ShapeDtypeStruct((t, d), F32),
        compiler_params=_params(("parallel", "arbitrary")))(du, *wgs, *after)


def _prenorm_bwd_first(dhx, x, ctx, dxo, nw, mod, name):
    n_lat, d = x.shape
    t = n_lat + ctx.shape[0]
    nxb = n_lat // ROW_TILE
    nb = t // ROW_TILE
    n_slot = 3
    assert nxb >= 2

    def body(dh_any, x_any, c_any, dxo_any, nw_ref, mod_ref, dx_ref, acc_ref, dh_buf, x_buf, dxo_buf, sems):
        i = pl.program_id(0)

        def fetch(step, from_ctx):
            slot = step % n_slot
            rows = pl.ds(pl.multiple_of(step * ROW_TILE, ROW_TILE), ROW_TILE)
            own = pl.ds(pl.multiple_of((step - nxb) * ROW_TILE, ROW_TILE), ROW_TILE) if from_ctx else rows
            return [pltpu.make_async_copy(dh_any.at[rows, :], dh_buf.at[slot], sems.at[0, slot]),
                    pltpu.make_async_copy((c_any if from_ctx else x_any).at[own, :], x_buf.at[slot], sems.at[1, slot]),
                    pltpu.make_async_copy(dxo_any.at[rows, :], dxo_buf.at[slot], sems.at[2, slot])]

        @pl.when(i == 0)
        def _():
            acc_ref[...] = jnp.zeros_like(acc_ref)
            for cp in fetch(0, False) + fetch(1, False):
                cp.start()

        @pl.when(i + 2 < nxb)
        def _():
            for cp in fetch(i + 2, False):
                cp.start()

        @pl.when((i + 2 >= nxb) & (i + 2 < nb))
        def _():
            for cp in fetch(i + 2, True):
                cp.start()

        slot = i % n_slot
        head = pl.ds(0, ROW_TILE)
        for k, (src, buf) in enumerate(((dh_any, dh_buf), (x_any, x_buf), (dxo_any, dxo_buf))):
            pltpu.make_async_copy(src.at[head, :], buf.at[slot], sems.at[k, slot]).wait()

        ctx = i >= nxb
        m = mod_ref[...]
        scale1 = 1.0 + jnp.where(ctx, m[4:5], m[1:2])
        x = x_buf[slot]
        nw_v = nw_ref[...]
        r = lax.rsqrt(jnp.mean(x * x, axis=-1, keepdims=True) + EPS)
        xn = x * r
        dh = dh_buf[slot]
        dshift = jnp.sum(dh, axis=0, keepdims=True)
        dscale = jnp.sum(dh * (xn * nw_v), axis=0, keepdims=True)
        acc_ref[6:7, :] += jnp.sum(dh * scale1 * xn, axis=0, keepdims=True)
        dxn = dh * (nw_v * scale1)
        dx = dxo_buf[slot] + r * (dxn - xn * jnp.mean(dxn * xn, axis=-1, keepdims=True))

        @pl.when(i < nxb)
        def _():
            acc_ref[0:1, :] += dshift
            acc_ref[1:2, :] += dscale
            dx_ref[...] = dx

        @pl.when(i >= nxb)
        def _():
            acc_ref[3:4, :] += dshift
            acc_ref[4:5, :] += dscale

    lat, _ = _split_rows(nxb, nb)(d)
    acc = _full((8, d))
    hbm = pl.BlockSpec(memory_space=pl.ANY)
    ring = pltpu.VMEM((n_slot, ROW_TILE, d), F32)
    return pl.pallas_call(body, name=name, grid=(nb,),
                          in_specs=[hbm, hbm, hbm, hbm, _full((1, d)), acc],
                          out_specs=[lat, acc],
                          out_shape=[jax.ShapeDtypeStruct((n_lat, d), F32), jax.ShapeDtypeStruct((8, d), F32)],
                          scratch_shapes=[ring, ring, ring, pltpu.SemaphoreType.DMA((3, n_slot))],
                          compiler_params=_params(("arbitrary",)))(dhx, x, ctx, dxo, nw, mod)


def _adamw(g, w, m, v):
    m = ADAM_B1 * m + (1.0 - ADAM_B1) * g
    v = ADAM_B2 * v + (1.0 - ADAM_B2) * jnp.square(g)
    m_hat = m / (1.0 - ADAM_B1 ** ADAM_STEP)
    v_hat = v / (1.0 - ADAM_B2 ** ADAM_STEP)
    delta = -ADAM_LR * (m_hat / (jnp.sqrt(v_hat) + ADAM_EPS) + ADAM_WD * w)
    return delta, m, v


def _sum_adamw(parts, w, m, v, name, row0=0, into=None):
    n_p, r, n = parts.shape
    r_all = w.shape[0]
    part_block_bytes = 4 * 1024 * 1024
    br = 8
    for cand in (512, 256, 128, 64, 32, 16):
        if r % cand == 0 and row0 % cand == 0 and n_p * cand * n * parts.dtype.itemsize <= part_block_bytes:
            br = cand
            break
    blk0 = row0 // br

    def body(p_ref, w_ref, m_ref, v_ref, *rest):
        g_out, d_out, m_out, v_out = rest[-4:]
        g = p_ref[0].astype(F32)
        for j in range(1, n_p):
            g = g + p_ref[j].astype(F32)
        g_out[...] = g
        d_out[...], m_out[...], v_out[...] = _adamw(g, w_ref[...], m_ref[...], v_ref[...])

    row = pl.BlockSpec((br, n), lambda i: (i + blk0, 0))
    kept = [] if into is None else list(into)
    return pl.pallas_call(body, name=name, grid=(r // br,),
                          in_specs=[pl.BlockSpec((n_p, br, n), lambda i: (0, i, 0)), row, row, row]
                          + [pl.BlockSpec(memory_space=pl.ANY)] * len(kept),
                          out_specs=[row] * 4, out_shape=[jax.ShapeDtypeStruct((r_all, n), F32)] * 4,
                          input_output_aliases={4 + j: j for j in range(len(kept))},
                          compiler_params=_params(("parallel",)))(parts, w, m, v, *kept)


def _rope_tables(n_lat, n_ctx):
    f = HEAD_DIM // 4
    rows = n_lat // GRID_W
    inv = ROPE_BASE ** (-jnp.arange(f, dtype=F32) / f)
    ang_r = jnp.arange(rows).astype(F32)[:, None] * inv[None, :]
    ang_c = jnp.arange(GRID_W).astype(F32)[:, None] * inv[None, :]

    cr, sr, cc, sc = jnp.cos(ang_r), jnp.sin(ang_r), jnp.cos(ang_c), jnp.sin(ang_c)
    zr, zc = jnp.zeros_like(cr), jnp.zeros_like(cc)

    def table(by_row, by_col):
        both = by_row[:, None, :] + by_col[None, :, :]
        return both.reshape(n_lat, HEAD_DIM)

    cos = table(jnp.concatenate([cr, cr, zr, zr], axis=-1), jnp.concatenate([zc, zc, cc, cc], axis=-1))
    sa = table(jnp.concatenate([-sr, zr, zr, zr], axis=-1), jnp.concatenate([zc, zc, -sc, zc], axis=-1))
    sb = table(jnp.concatenate([zr, sr, zr, zr], axis=-1), jnp.concatenate([zc, zc, zc, sc], axis=-1))
    pad = jnp.zeros((n_ctx, HEAD_DIM), F32)
    return (jnp.concatenate([cos, pad + 1.0], axis=0), jnp.concatenate([sa, pad], axis=0),
            jnp.concatenate([sb, pad], axis=0))


def _pad_rows(a, rows):
    return jnp.pad(a, [(0, rows - a.shape[0])] + [(0, 0)] * (a.ndim - 1))


def _pad_cols(a, cols):
    return jnp.pad(a, [(0, 0), (0, cols - a.shape[1])])


def kernel(x, c, ctx, c_ctx, norm_w, w_mod, b_mod, w_in, conv_w, conv_norm_w, ret_norm_w, ret_decay_f, ret_decay_b, w_out, final_norm_w, loss_target, m_c_ctx, m_norm_w, m_w_mod, m_b_mod, m_w_in, m_conv_w, m_conv_norm_w, m_ret_norm_w, m_ret_decay_f, m_ret_decay_b, m_w_out, m_final_norm_w, v_c_ctx, v_norm_w, v_w_mod, v_b_mod, v_w_in, v_conv_w, v_conv_norm_w, v_ret_norm_w, v_ret_decay_f, v_ret_decay_b, v_w_out, v_final_norm_w):
    depth = norm_w.shape[0]
    n_lat, d = x.shape[1], x.shape[2]
    n_ctx = ctx.shape[1]
    s = d // 2
    n_heads = ret_decay_f.shape[1]
    nx, ncc = n_lat // CHUNK, n_ctx // CHUNK
    n_mod = w_mod.shape[2]
    n_cw = conv_w.shape[2]
    r_out = w_out.shape[1]
    assert s == n_heads * HEAD_DIM and w_in.shape[2] == s and N_DEV * r_out == d
    assert n_lat % ROW_TILE == 0 and n_ctx % ROW_TILE == 0 and 3 * depth * n_cw <= d and N_DEV * n_mod == 3 * d
    me = 4 * lax.axis_index("x") + 2 * lax.axis_index("y") + lax.axis_index("c")

    w_in_bf = [w_in[l].astype(BF16) for l in range(depth)]
    w_out_bf = [w_out[l].astype(BF16) for l in range(depth)]

    first = jnp.concatenate([c.reshape(1, d), _pad_cols(conv_w.reshape(1, -1), d), jnp.zeros((6, d), F32)], axis=0)
    (first_g,) = _all_gather([first], "gather_cond")
    first_g = first_g.reshape(N_DEV, 8, d)
    c_all = first_g[:, 0, :]
    conv_full = first_g[:, 1, :3 * depth * n_cw].reshape(N_DEV, depth, 3, n_cw)
    conv_full = conv_full.transpose(1, 2, 0, 3).reshape(depth, 3, N_DEV * n_cw)
    c9 = jnp.concatenate([c_all, c_ctx.reshape(1, d), jnp.zeros((7, d), F32)], axis=0)

    b_sh = lax.dynamic_slice(b_mod, (0, me * n_mod), (depth, n_mod))
    mod_sh = jnp.concatenate([_mod_rows(c9, w_mod[l], b_sh[l:l + 1], f"mod_rows_l{l}") for l in range(depth)], axis=0)
    (mod_g,) = _all_gather([mod_sh], "gather_mod")
    mod_g = mod_g.reshape(N_DEV, depth, 16, n_mod)
    mods = []
    for l in range(depth):
        mine = lax.dynamic_index_in_dim(mod_g[:, l], me, axis=1, keepdims=False).reshape(3, d)
        cx = mod_g[:, l, 8, :].reshape(3, d)
        mods.append(jnp.concatenate([mine, cx, jnp.zeros((2, d), F32)], axis=0))

    halves = [w_in_bf[0][:, :s // 2], w_in_bf[0][:, s // 2:]]
    near, order = [], [mod_g]
    for j, part in enumerate(halves):
        near.append(_push_start([part], [_landing(part, me)], "near", f"w_in0_start_{j}", after=order))
        order = near[-1][4:]
    pending = []
    for k in range(depth):
        srcs = [w_out_bf[k]] + ([w_in_bf[k]] if k > 0 else [])
        started = _push_start(srcs, [_landing(a, me) for a in srcs], "gather", f"weights_start_l{k}", after=order)
        pending.append(started[:4])
        order = started[4:]
    w_in_g = [None] * depth
    w_out_g = [None] * depth

    cos, sa, sb_tab = _rope_tables(n_lat, n_ctx)
    t_all = n_lat + n_ctx

    saved = []
    xt = hx_next = None
    for l in range(depth):
        tiles = _tiles(l, t_all, d)
        names = ["dc", "dlf", "dlb", "qf", "kf", "qb", "kb", "cdf", "cdb", "lg"]
        dec = jnp.stack([ret_decay_f[l], ret_decay_b[l]], axis=0)
        tabs = dict(zip(names, _decay_tables(dec, n_heads, f"decay_tables_l{l}")))
        if l == 0:
            hx = _prenorm_first(x[0], ctx[0], norm_w[0:1], mods[0], "prenorm_l0", after=order)
            gathered, out, after = [], None, hx
            for j in range(2):
                (landed,) = _push_wait(*near[j][:4], "near", after, f"w_in0_wait_{j}")
                relay = _push_start([], [landed], "relay", f"w_in0_relay_start_{j}")
                (landed,) = _push_wait(*relay[:4], "relay", relay[4], f"w_in0_relay_wait_{j}")
                gathered.append(landed)
                out = _in_proj(hx, landed, cos, sa, sb_tab, s, j, tiles["in_tm_half"], f"in_proj_l0_{j}", into=out)
                after = out[0]
            u, qkv = out
            w_in_g[0] = gathered
        else:
            landed = _push_wait(*pending[l], "gather", xt, f"weights_wait_l{l}")
            w_out_g[l], w_in_g[l] = landed[0].reshape(d, d), [landed[1]]
            hx = hx_next
            u, qkv = _in_proj(hx, w_in_g[l][0], cos, sa, sb_tab, s, 0, tiles["in_tm"], f"in_proj_l{l}")
        sf, sb = _state_sweep(qkv, tabs, n_heads, nx, ncc, f"state_sweep_l{l}")
        ycat, o = _mix_fwd(u, qkv, sf, sb, tabs, conv_full[l], conv_norm_w[l:l + 1], ret_norm_w[l:l + 1],
                           n_heads, nx, ncc, f"mix_fwd_l{l}")
        if l == 0:
            (landed,) = _push_wait(*pending[0], "gather", ycat, "weights_wait_l0")
            w_out_g[0] = landed.reshape(d, d)
        m_res = x_new = None
        if l < depth - 1:
            res = (x[0], ctx[0]) if l == 0 else (xt,)
            m_res, x_new, hx_next = _out_proj_prenorm(ycat, w_out_g[l], res, mods[l], norm_w[l + 1:l + 2], mods[l + 1],
                                                      n_lat, f"out_proj_l{l}")
        else:
            dxt, dm, loss_blk, dfnw, gate_acc = _out_proj_loss(ycat, w_out_g[l], xt, mods[l], loss_target[0],
                                                               final_norm_w.reshape(1, d), n_lat, f"out_proj_loss_l{l}")
        saved.append(dict(tabs=tabs, xt=xt, hx=hx, u=u, qkv=qkv, sf=sf, sb=sb, ycat=ycat, o=o, m=m_res, tiles=tiles))
        xt = x_new

    dmod_x, dmod_c, dnw, dcnw, dgnw, dconv, ddec, dwin, dwout = [], [], [], [], [], [], [], [], []
    started_token = ()
    for l in reversed(range(depth)):
        sv = saved[l]
        tiles = sv["tiles"]
        dycat = _matmul_nt(dm, w_out_g[l], tiles["ob_tn"], f"out_proj_bwd_l{l}", after=started_token)
        dwout.append(_weight_grad(sv["ycat"], dm.reshape(1, *dm.shape), tiles["wo_bm"], tiles["wo_bt"],
                                  f"w_out_grad_l{l}")[0])
        g, dz, db, drz, do, norm_acc = _mix_bwd_a(dycat, sv["u"], sv["o"], conv_full[l], conv_norm_w[l:l + 1],
                                                   ret_norm_w[l:l + 1], n_heads, nx, ncc, f"mix_bwd_a_l{l}")
        gf, gb = _grad_state_sweep(sv["qkv"], do, sv["tabs"], n_heads, nx, ncc, f"grad_state_sweep_l{l}")
        du, conv_acc, dlg = _mix_bwd_b(sv["u"], g, dz, db, drz, sv["qkv"], do, sv["sf"], sv["sb"],
                                       gf, gb, sv["tabs"], cos, sa, sb_tab, conv_full[l], n_heads, nx, ncc,
                                       f"mix_bwd_b_l{l}")
        gate_acc_l = gate_acc
        if l > 0:
            dhx = _in_proj_bwd(du, w_in_g[l], tiles["bwd_tm"], tiles["bwd_gs"], f"in_proj_bwd_l{l}")
            below = (saved[l - 1]["m"], mods[l - 1])
            dwin_l, dxt, pre_acc, dm, gate_acc = _weight_grad_beside_prenorm_bwd(
                sv["hx"], du, dhx, sv["xt"], dxt, norm_w[l:l + 1], mods[l], below, n_lat, f"w_in_grad_l{l}")
        else:
            dwin_l = _weight_grad(sv["hx"], du, tiles["wg_bm"], tiles["wg_bt"], f"w_in_grad_l{l}")
        srcs = [dwin_l, dwout[-1].reshape(N_DEV, r_out, d)]
        lands = [_landing(lax.dynamic_index_in_dim(a, me, axis=0, keepdims=False), me) for a in srcs]
        started = _push_start(srcs, lands, "scatter", f"grads_start_l{l}")
        dwin.append(started[:4])
        started_token = started[4:]
        if l == 0:
            dhx = _in_proj_bwd(du, w_in_g[l], tiles["bwd_tm"], tiles["bwd_gs"], f"in_proj_bwd_l{l}", after=started[4:])
            dxt, pre_acc = _prenorm_bwd_first(dhx, x[0], ctx[0], dxt, norm_w[l:l + 1], mods[l], f"prenorm_bwd_l{l}")
        dmod_x.append(jnp.concatenate([pre_acc[0], pre_acc[1], gate_acc_l[2]]))
        dmod_c.append(jnp.concatenate([pre_acc[3], pre_acc[4], gate_acc_l[5]]))
        dnw.append(pre_acc[6])
        dcnw.append(norm_acc[0])
        dgnw.append(norm_acc[1])
        dconv.append(conv_acc[0:3])
        ddec.append(dlg[0:2, :n_heads])
    for lst in (dmod_x, dmod_c, dnw, dcnw, dgnw, dconv, ddec, dwin, dwout):
        lst.reverse()
    grad_x = dxt.reshape(1, n_lat, d)

    rows = []
    for l in range(depth):
        rows += [dmod_x[l], dmod_c[l]]
    (dmod_g,) = _all_gather([_pad_rows(jnp.stack(rows, axis=0), 8)], "gather_dmod")
    dmod_g = dmod_g.reshape(N_DEV, 8, 3 * d)
    mine_cols = lax.dynamic_slice(dmod_g, (0, 0, me * n_mod), (N_DEV, 8, n_mod))
    g_wmod, dcc = [], jnp.zeros((d,), F32)
    for l in range(depth):
        gw, dc_part = _mod_grads(mine_cols[:, 2 * l], mine_cols[:, 2 * l + 1], c9, w_mod[l], f"mod_grads_l{l}")
        g_wmod.append(gw)
        dcc = dcc + dc_part[0]

    n_small = 16
    small = jnp.concatenate([
        jnp.stack(dnw, axis=0),
        jnp.concatenate(dcnw).reshape(1, -1),
        jnp.concatenate(dgnw).reshape(1, -1),
        dfnw[0:1],
        dcc.reshape(1, d),
        jnp.stack(dconv, axis=0).reshape(-1, d),
        _pad_cols(jnp.stack(ddec, axis=0).reshape(1, -1), d),
    ], axis=0)
    assert depth * s == d and small.shape[0] < n_small
    n_rows = small.shape[0]
    small = jnp.concatenate([small, _pad_cols(loss_blk[0:1], d)], axis=0)
    (small_g,) = _all_gather([_pad_rows(small, n_small)], "gather_small")
    small_g = small_g.reshape(N_DEV, n_small, d)

    def pack_small(nw_, cn_, gn_, fn_, cc_, df_, db_):
        return _pad_rows(jnp.concatenate([
            nw_, cn_.reshape(1, -1), gn_.reshape(1, -1), fn_.reshape(1, d), cc_.reshape(1, d),
            jnp.zeros((n_rows - depth - 5, d), F32),
            _pad_cols(jnp.stack([df_, db_], axis=1).reshape(1, -1), d)], axis=0), n_small)

    w_s = pack_small(norm_w, conv_norm_w, ret_norm_w, final_norm_w, c_ctx, ret_decay_f, ret_decay_b)
    m_s = pack_small(m_norm_w, m_conv_norm_w, m_ret_norm_w, m_final_norm_w, m_c_ctx, m_ret_decay_f, m_ret_decay_b)
    v_s = pack_small(v_norm_w, v_conv_norm_w, v_ret_norm_w, v_final_norm_w, v_c_ctx, v_ret_decay_f, v_ret_decay_b)
    small_out = _sum_adamw(small_g, w_s, m_s, v_s, "adamw_small")
    loss = small_out[0][n_rows, 0]

    def unpack_small(a):
        nw_ = a[0:depth]
        cn_ = a[depth].reshape(depth, s)
        gn_ = a[depth + 1].reshape(depth, s)
        fn_ = a[depth + 2]
        cc_ = a[depth + 3]
        dd = a[n_rows - 1, :depth * 2 * n_heads].reshape(depth, 2, n_heads)
        return dict(c_ctx=cc_, norm_w=nw_, conv_norm_w=cn_, ret_norm_w=gn_, ret_decay_f=dd[:, 0], ret_decay_b=dd[:, 1],
                    final_norm_w=fn_)

    res = {}
    for kind, arr in zip(("grad", "delta", "m", "v"), small_out):
        for k_, val in unpack_small(arr).items():
            res[(kind, k_)] = val

    bm_parts = jnp.concatenate([dmod_g[:, 0:2 * depth:2].reshape(N_DEV, depth, 3 * d),
                                dmod_g[:, 1:2 * depth:2].reshape(N_DEV, depth, 3 * d)], axis=0)
    bm_parts = jnp.concatenate([bm_parts, jnp.zeros((2 * N_DEV, 8 - depth, 3 * d), F32)], axis=1)
    pad8 = lambda a: _pad_rows(a, 8)
    bm_out = _sum_adamw(bm_parts, pad8(b_mod), pad8(m_b_mod), pad8(v_b_mod), "adamw_b_mod")
    for kind, arr in zip(("grad", "delta", "m", "v"), bm_out):
        res[(kind, "b_mod")] = arr[:depth]

    conv_rows = small_g[:, depth + 4:depth + 4 + 3 * depth * s // d].reshape(N_DEV, depth * 3, s)
    conv_mine = lax.dynamic_slice(conv_rows, (0, 0, me * n_cw), (N_DEV, depth * 3, n_cw))
    conv_mine = jnp.concatenate([conv_mine, jnp.zeros((N_DEV, 8 - depth * 3, n_cw), F32)], axis=1)
    cw2 = lambda a: _pad_rows(a.reshape(depth * 3, n_cw), 8)
    cw_out = _sum_adamw(conv_mine, cw2(conv_w), cw2(m_conv_w), cw2(v_conv_w), "adamw_conv_w")
    for kind, arr in zip(("grad", "delta", "m", "v"), cw_out):
        res[(kind, "conv_w")] = arr[:depth * 3].reshape(depth, 3, n_cw)

    wm_out = _sum_adamw(jnp.stack(g_wmod, axis=0).reshape(1, depth * d, n_mod), w_mod.reshape(depth * d, n_mod),
                        m_w_mod.reshape(depth * d, n_mod), v_w_mod.reshape(depth * d, n_mod), "adamw_w_mod")
    for kind, arr in zip(("grad", "delta", "m", "v"), wm_out):
        res[(kind, "w_mod")] = arr.reshape(depth, d, n_mod)

    wi_out = wo_out = None
    after = wm_out[0]
    for l in reversed(range(depth)):
        win_parts, wout_parts = _push_wait(*dwin[l], "scatter", after, f"grads_wait_l{l}")
        wi_out = _sum_adamw(win_parts, w_in.reshape(depth * d, s), m_w_in.reshape(depth * d, s),
                            v_w_in.reshape(depth * d, s), f"adamw_w_in_l{l}", row0=l * d, into=wi_out)
        wo_out = _sum_adamw(wout_parts, w_out.reshape(depth * r_out, d), m_w_out.reshape(depth * r_out, d),
                            v_w_out.reshape(depth * r_out, d), f"adamw_w_out_l{l}", row0=l * r_out, into=wo_out)
        after = wo_out[0]
    for kind, arr in zip(("grad", "delta", "m", "v"), wi_out):
        res[(kind, "w_in")] = arr.reshape(depth, d, s)
    for kind, arr in zip(("grad", "delta", "m", "v"), wo_out):
        res[(kind, "w_out")] = arr.reshape(depth, r_out, d)

    order = ["c_ctx", "norm_w", "w_mod", "b_mod", "w_in", "conv_w", "conv_norm_w", "ret_norm_w", "ret_decay_f",
             "ret_decay_b", "w_out", "final_norm_w"]
    outs = [loss, grad_x]
    for kind in ("grad", "delta", "m", "v"):
        outs += [res[(kind, k_)] for k_ in order]
    return tuple(outs)
```

```python
import jax
import jax.numpy as jnp
from jax import lax
from jax.experimental import pallas as pl
from jax.experimental.pallas import tpu as pltpu

F32 = jnp.float32
BF16 = jnp.bfloat16

EPS = 1e-6
CHUNK = 128
HEAD_DIM = 128
GRID_W = 64
ROPE_BASE = 10000.0
N_DEV = 8
ADAM_LR, ADAM_B1, ADAM_B2, ADAM_EPS, ADAM_WD, ADAM_STEP = 0.001, 0.9, 0.999, 1e-08, 0.01, 10

ROW_TILE = 256
V7X_VMEM_LIMIT = 56 * 1024 * 1024

NN = ((1,), (0,))
NT = ((1,), (1,))
TN = ((0,), (0,))


def _dot(a, b, dims):
    return lax.dot_general(a, b, (dims, ((), ())), preferred_element_type=F32)


def _params(sem=None):
    if sem is None:
        return pltpu.CompilerParams(vmem_limit_bytes=V7X_VMEM_LIMIT)
    return pltpu.CompilerParams(dimension_semantics=sem, vmem_limit_bytes=V7X_VMEM_LIMIT)


def _silu(z):
    return z * jax.nn.sigmoid(z)


def _dsilu(z):
    s = jax.nn.sigmoid(z)
    return s * (1.0 + z * (1.0 - s))


def _silu_and_slope(z):
    s = jax.nn.sigmoid(z)
    return z * s, s * (1.0 + z * (1.0 - s))


def _sum_all(a):
    return jnp.sum(jnp.sum(a, axis=1, keepdims=True), axis=0, keepdims=True)


def _mm_rows(t):
    return 768 if t % 768 == 0 else ROW_TILE


def _rows_or(t, rows):
    return rows if t % rows == 0 else _mm_rows(t)


def _tiles(layer, t, d):
    return dict(in_tm=_rows_or(t, 1408), in_tm_half=_rows_or(t, 2112), bwd_gs=2, wg_bm=d, wg_bt=_mm_rows(t),
                wo_bm=d, wo_bt=_mm_rows(t), ob_tn=d, bwd_tm=_rows_or(t, 1056))


def _full(shape):
    n = len(shape)
    return pl.BlockSpec(shape, lambda *_: (0,) * n)


def _peers(x, y, c):
    return [(x, y, 1 - c), (1 - x, y, c), (x, 1 - y, c), (1 - x, 1 - y, c),
            (1 - x, y, 1 - c), (x, 1 - y, 1 - c), (1 - x, 1 - y, 1 - c)]


def _lin(p):
    return 4 * p[0] + 2 * p[1] + p[2]


def _all_gather(arrays, name):
    n_arr = len(arrays)
    space = pltpu.VMEM

    def body(*refs):
        ins, outs = refs[:n_arr], refs[n_arr:2 * n_arr]
        send_sems, recv_sems, local_sems = refs[2 * n_arr:]
        x, y, c = lax.axis_index("x"), lax.axis_index("y"), lax.axis_index("c")
        me, sibling = (x, y, c), (x, y, 1 - c)
        chips = [(1 - x, y), (x, 1 - y), (1 - x, 1 - y)]
        every = []
        locals_ = []
        for a in range(n_arr):
            m_per = ins[a].shape[0]
            out_ref = outs[a]

            def rows(p, out_ref=out_ref, m_per=m_per):
                return out_ref.at[pl.ds(_lin(p) * m_per, m_per), :]

            def copy(k, block, to, src=None, a=a, rows=rows):
                return pltpu.make_async_remote_copy(
                    src_ref=rows(block) if src is None else src, dst_ref=rows(block),
                    send_sem=send_sems.at[a, k], recv_sem=recv_sems.at[a, k],
                    device_id=to, device_id_type=pl.DeviceIdType.MESH)

            mine = pltpu.make_async_copy(ins[a], rows(me), local_sems.at[a])
            mine.start()
            locals_.append(mine)
            first = [copy(0, me, sibling, src=ins[a])]
            first += [copy(1 + j, me, (*chip, c), src=ins[a]) for j, chip in enumerate(chips)]
            for cp in first:
                cp.start()
            every.append((copy, first))
        sends = []
        for a in range(n_arr):
            copy, first = every[a]
            passed = [copy(4 + j, (*chip, c), sibling) for j, chip in enumerate(chips)]
            for j, chip in enumerate(chips):
                copy(1 + j, (*chip, c), me).wait_recv()
                passed[j].start()
            sends += first + passed
        for a in range(n_arr):
            copy, _ = every[a]
            copy(0, sibling, me).wait_recv()
            for j, chip in enumerate(chips):
                copy(4 + j, (*chip, 1 - c), me).wait_recv()
        for cp in sends:
            cp.wait_send()
        for mine in locals_:
            mine.wait()

    outs = pl.pallas_call(
        body, name=name,
        out_shape=[jax.ShapeDtypeStruct((N_DEV * a.shape[0], a.shape[1]), a.dtype) for a in arrays],
        in_specs=[pl.BlockSpec(memory_space=space)] * n_arr,
        out_specs=[pl.BlockSpec(memory_space=space)] * n_arr,
        scratch_shapes=[pltpu.SemaphoreType.DMA((n_arr, 7)), pltpu.SemaphoreType.DMA((n_arr, 7)),
                        pltpu.SemaphoreType.DMA((n_arr,))],
        compiler_params=_params(),
    )(*arrays)
    return list(outs)


_HBM = pl.BlockSpec(memory_space=pltpu.HBM)
_SEM = pl.BlockSpec(memory_space=pltpu.SEMAPHORE)
_DATAFLOW = pltpu.SideEffectType.DATAFLOW_SIDE_EFFECTING


PUSH_COPIES = {"scatter": 7, "gather": 7, "near": 4, "relay": 3}


def _push_copies(src_refs, land_refs, send_sems, recv_sems, mode):
    x, y, c = lax.axis_index("x"), lax.axis_index("y"), lax.axis_index("c")
    me, sibling = (x, y, c), (x, y, 1 - c)
    n_k = PUSH_COPIES[mode]
    out, back = [], []
    if mode == "relay":
        for k, chip in enumerate([(1 - x, y), (x, 1 - y), (1 - x, 1 - y)]):
            for a, land in enumerate(land_refs):
                sems = dict(send_sem=send_sems.at[n_k * a + k], recv_sem=recv_sems.at[n_k * a + k],
                            device_id=sibling, device_id_type=pl.DeviceIdType.MESH)
                mine = land.at[_lin((*chip, c))]
                out.append(pltpu.make_async_remote_copy(src_ref=mine, dst_ref=mine, **sems))
                back.append(pltpu.make_async_remote_copy(src_ref=mine, dst_ref=land.at[_lin((*chip, 1 - c))], **sems))
        return out, back
    for k, peer in enumerate(_peers(x, y, c)[:n_k]):
        for a, (src, land) in enumerate(zip(src_refs, land_refs)):
            sems = dict(send_sem=send_sems.at[n_k * a + k], recv_sem=recv_sems.at[n_k * a + k],
                        device_id=peer, device_id_type=pl.DeviceIdType.MESH)
            mine = src.at[_lin(peer)] if mode == "scatter" else src
            out.append(pltpu.make_async_remote_copy(src_ref=mine, dst_ref=land.at[_lin(me)], **sems))
            back.append(pltpu.make_async_remote_copy(src_ref=mine, dst_ref=land.at[_lin(peer)], **sems))
    return out, back


def _push_start(srcs, lands, mode, name, after=()):
    n_src, n = len(srcs), len(lands)
    n_buf = n_src + n
    n_in = n_buf + len(after)
    n_sem = PUSH_COPIES[mode] * n

    def body(*refs):
        send_sems, recv_sems = refs[n_in], refs[n_in + 1]
        out, _ = _push_copies(refs[:n_src], refs[n_src:n_buf], send_sems, recv_sems, mode)
        for cp in out:
            cp.start()
        token = refs[-1]
        token[...] = jnp.zeros_like(token)

    both = list(srcs) + list(lands)
    res = pl.pallas_call(
        body, name=name,
        out_shape=[pltpu.SemaphoreType.DMA((n_sem,)), pltpu.SemaphoreType.DMA((n_sem,))]
        + [pltpu.HBM(a.shape, a.dtype) for a in both] + [jax.ShapeDtypeStruct((8, 128), F32)],
        in_specs=[_HBM] * n_buf + [pl.BlockSpec(memory_space=pl.ANY)] * len(after),
        out_specs=[_SEM, _SEM] + [_HBM] * n_buf + [pl.BlockSpec(memory_space=pltpu.VMEM)],
        input_output_aliases={i: 2 + i for i in range(n_buf)},
        compiler_params=pltpu.CompilerParams(has_side_effects=_DATAFLOW),
    )(*[pltpu.with_memory_space_constraint(a, pltpu.HBM) for a in both], *after)
    return res[0], res[1], list(res[2:2 + n_src]), list(res[2 + n_src:2 + n_buf]), res[-1]


def _push_wait(send_sems, recv_sems, srcs, lands, mode, after, name):
    n_src, n = len(srcs), len(lands)
    n_buf = n_src + n

    def body(*refs):
        out, back = _push_copies(refs[:n_src], refs[n_src:n_buf], refs[n_buf], refs[n_buf + 1], mode)
        for cp in out:
            cp.wait_send()
        for cp in back:
            cp.wait_recv()

    both = list(srcs) + list(lands)
    res = pl.pallas_call(
        body, name=name,
        out_shape=[pltpu.HBM(a.shape, a.dtype) for a in both],
        in_specs=[_HBM] * n_buf + [_SEM, _SEM, pl.BlockSpec(memory_space=pl.ANY)],
        out_specs=[_HBM] * n_buf,
        input_output_aliases={i: i for i in range(n_buf)},
        compiler_params=pltpu.CompilerParams(has_side_effects=_DATAFLOW),
    )(*both, send_sems, recv_sems, after)
    return list(res[n_src:])


def _landing(own, me):
    zone = lax.empty((N_DEV,) + own.shape, own.dtype)
    return lax.dynamic_update_slice(zone, own[None], (me,) + (0,) * own.ndim)


def _mod_rows(c9, w_mod, b_sh, name):
    n = w_mod.shape[1]

    def body(c_ref, w_ref, b_ref, o_ref):
        s9 = _silu(c_ref[...]).astype(BF16)
        o_ref[...] = _dot(s9, w_ref[...].astype(BF16), NN) + b_ref[...]

    return pl.pallas_call(body, name=name, out_shape=jax.ShapeDtypeStruct((16, n), F32),
                          compiler_params=_params())(c9, w_mod, b_sh)


def _mod_grads(dm_rows, dc_rows, c9, w_mod, name):
    d, n = w_mod.shape

    def body(dm_ref, dc_ref, c_ref, w_ref, gw_ref, dc_out):
        dc = dc_ref[...]
        tot = dc[0:1]
        for j in range(1, N_DEV):
            tot = tot + dc[j:j + 1]
        row = lax.broadcasted_iota(jnp.int32, (8, n), 0)
        lower = jnp.where(row == 0, tot, 0.0)
        dmod9 = jnp.concatenate([dm_ref[...], lower], axis=0).astype(BF16)
        c9v = c_ref[...]
        s9 = _silu(c9v).astype(BF16)
        gw_ref[...] = _dot(s9, dmod9, TN)
        ds = _dot(lower.astype(BF16), w_ref[...].astype(BF16), NT)
        dc_out[...] = ds * _dsilu(c9v[8:16])

    return pl.pallas_call(body, name=name,
                          out_shape=[jax.ShapeDtypeStruct((d, n), F32), jax.ShapeDtypeStruct((8, d), F32)],
                          compiler_params=_params())(dm_rows, dc_rows, c9, w_mod)


def _decay_tables(dec, n_heads, name):
    c = CHUNK

    def body(dec_ref, dc_ref, dlf_ref, dlb_ref, qf_ref, kf_ref, qb_ref, kb_ref, cdf_ref, cdb_ref, lg_ref):
        h = pl.program_id(0)
        d = dec_ref[...]
        lane = lax.broadcasted_iota(jnp.int32, d.shape, 1)
        lg = -jnp.exp(jnp.sum(jnp.where(lane == h, d, 0.0), axis=1, keepdims=True))
        lgf, lgb = lg[0:1], lg[1:2]
        i = lax.broadcasted_iota(jnp.int32, (c, c), 0).astype(F32)
        j = lax.broadcasted_iota(jnp.int32, (c, c), 1).astype(F32)
        diff = i - j
        d_f = jnp.where(diff >= 0, jnp.exp(lgf * jnp.maximum(diff, 0.0)), 0.0)
        d_b = jnp.where(diff <= 0, jnp.exp(lgb * jnp.maximum(-diff, 0.0)), 0.0)
        dc_ref[...] = d_f + d_b
        dlf_ref[...] = diff * d_f
        dlb_ref[...] = -diff * d_b
        pos = lax.broadcasted_iota(jnp.int32, (c, HEAD_DIM), 0).astype(F32)
        qf_ref[...] = jnp.exp(lgf * (pos + 1.0))
        kf_ref[...] = jnp.exp(lgf * (c - 1.0 - pos))
        qb_ref[...] = jnp.exp(lgb * (c - pos))
        kb_ref[...] = jnp.exp(lgb * pos)
        ones = jnp.ones((8, HEAD_DIM), F32)
        cdf_ref[...] = jnp.exp(lgf * float(c)) * ones
        cdb_ref[...] = jnp.exp(lgb * float(c)) * ones

        @pl.when(h == 0)
        def _():
            lg_ref[...] = jnp.zeros_like(lg_ref)

        row8 = lax.broadcasted_iota(jnp.int32, (8, HEAD_DIM), 0)
        lane8 = lax.broadcasted_iota(jnp.int32, (8, HEAD_DIM), 1)
        lg_ref[...] += (jnp.where((row8 == 0) & (lane8 == h), lgf, 0.0)
                        + jnp.where((row8 == 1) & (lane8 == h), lgb, 0.0))

    def per_head(*tail):
        return pl.BlockSpec((None,) + tail, lambda h: (h,) + (0,) * len(tail))

    shapes = [(c, c)] * 3 + [(c, HEAD_DIM)] * 4 + [(8, HEAD_DIM)] * 2
    return pl.pallas_call(
        body, name=name, grid=(n_heads,),
        in_specs=[_full(dec.shape)],
        out_specs=[per_head(*s) for s in shapes] + [_full((8, HEAD_DIM))],
        out_shape=[jax.ShapeDtypeStruct((n_heads,) + s, F32) for s in shapes]
        + [jax.ShapeDtypeStruct((8, HEAD_DIM), F32)],
        compiler_params=_params(("arbitrary",)),
    )(dec)


def _modulate(x, nw, shift, scale):
    r = lax.rsqrt(jnp.mean(x * x, axis=-1, keepdims=True) + EPS)
    return ((x * r) * nw * (1.0 + scale) + shift).astype(BF16)


def _split_rows(nxb, nb):
    def specs(d, step=lambda i: i):
        lat = pl.BlockSpec((ROW_TILE, d), lambda i: (jnp.minimum(step(i), nxb - 1), 0))
        ctx = pl.BlockSpec((ROW_TILE, d), lambda i: (jnp.clip(step(i) - nxb, 0, nb - nxb - 1), 0))
        return lat, ctx
    return specs


def _prenorm_first(x, ctx, nw, mod, name, after=()):
    n_lat, d = x.shape
    t = n_lat + ctx.shape[0]
    nxb = n_lat // ROW_TILE

    def body(x_ref, c_ref, nw_ref, mod_ref, *rest):
        o_ref = rest[-1]
        m = mod_ref[...]
        nw_v = nw_ref[...]

        @pl.when(pl.program_id(0) < nxb)
        def _():
            o_ref[...] = _modulate(x_ref[...], nw_v, m[0:1], m[1:2])

        @pl.when(pl.program_id(0) >= nxb)
        def _():
            o_ref[...] = _modulate(c_ref[...], nw_v, m[3:4], m[4:5])

    lat, cx = _split_rows(nxb, t // ROW_TILE)(d)
    return pl.pallas_call(
        body, name=name, grid=(t // ROW_TILE,),
        in_specs=[lat, cx, _full((1, d)), _full((8, d))] + [pl.BlockSpec(memory_space=pl.ANY)] * len(after),
        out_specs=pl.BlockSpec((ROW_TILE, d), lambda i: (i, 0)), out_shape=jax.ShapeDtypeStruct((t, d), BF16),
        compiler_params=_params(("parallel",)))(x, ctx, nw, mod, *after)


def _rope_fwd(v, cos, sa, sb):
    return v * cos + pltpu.roll(v, 96, 1) * sa + pltpu.roll(v, 32, 1) * sb


def _rope_bwd(g, cos, sa, sb):
    return g * cos + pltpu.roll(g * sa, 32, 1) + pltpu.roll(g * sb, 96, 1)


N_PLAIN = 5
U_DTYPE = BF16


def _in_proj(hx, wg, cos, sa, sb, s, part, tm, name, after=(), into=None):
    t, d = hx.shape
    n_seg, _, n = wg.shape
    nb = t // tm
    k_scale = HEAD_DIM ** -0.5
    kept = [] if into is None else list(into)

    def body(a_ref, w_ref, cos_ref, sa_ref, sb_ref, *rest):
        u_ref, qkv_ref = rest[-2:]
        g = pl.program_id(1)
        acc = _dot(a_ref[...], w_ref[...], NN)

        @pl.when(g < N_PLAIN)
        def _():
            u_ref[...] = acc.astype(U_DTYPE)

        @pl.when(g == N_PLAIN + 2)
        def _():
            qkv_ref[...] = acc.astype(BF16)

        for which, scale in ((N_PLAIN, 1.0), (N_PLAIN + 1, k_scale)):
            @pl.when(g == which)
            def _(scale=scale):
                co, a, b = cos_ref[...], sa_ref[...], sb_ref[...]
                for h in range(n // HEAD_DIM):
                    sl = slice(h * HEAD_DIM, (h + 1) * HEAD_DIM)
                    qkv_ref[:, sl] = (_rope_fwd(acc[:, sl], co, a, b) * scale).astype(BF16)

    def w_seg(g):
        return jnp.where(g < N_PLAIN - 1, g, jnp.where(g == N_PLAIN - 1, n_seg - 1, g - 1))

    def qkv_at(i, g):
        held = (jnp.where(i == 0, 0, 2), jnp.maximum(i - 1, 0))
        return (jnp.where(g < N_PLAIN, held[0], g - N_PLAIN), jnp.where(g < N_PLAIN, held[1], i), part)

    tab = pl.BlockSpec((tm, HEAD_DIM), lambda i, g: (i, 0))
    hbm = pl.BlockSpec(memory_space=pl.ANY)
    return pl.pallas_call(
        body, name=name, grid=(nb, n_seg),
        in_specs=[pl.BlockSpec((tm, d), lambda i, g: (i, 0)), pl.BlockSpec((None, d, n), lambda i, g: (w_seg(g), 0, 0)),
                  tab, tab, tab] + [hbm] * (len(after) + len(kept)),
        out_specs=[pl.BlockSpec((None, tm, n), lambda i, g: (jnp.minimum(g, N_PLAIN - 1), i, part)),
                   pl.BlockSpec((None, tm, n), qkv_at)],
        out_shape=[jax.ShapeDtypeStruct((N_PLAIN, t, s), U_DTYPE), jax.ShapeDtypeStruct((3, t, s), BF16)],
        input_output_aliases={5 + len(after) + j: j for j in range(len(kept))},
        compiler_params=_params(("arbitrary", "arbitrary")))(hx, wg, cos, sa, sb, *after, *kept)


def _pair_sweep(xs, ys, tab_f, tab_b, cdf, cdb, n_heads, nx, ncc, reverse, name):
    t, s = xs[0].shape[-2:]
    nc = nx + ncc
    c = CHUNK
    n_pair = nc // 2
    assert nx % 2 == 0 and ncc % 2 == 0

    def f_pair(i):
        step = n_pair - 1 - i if reverse else i
        return jnp.where(step < ncc // 2, nx // 2 + step, step - ncc // 2)

    def b_pair(i):
        return i if reverse else n_pair - 1 - i

    f_subs = (1, 0) if reverse else (0, 1)
    b_subs = (0, 1) if reverse else (1, 0)

    n_slot = 3

    def body(x_any, y_any, tf, tb, cdf_ref, cdb_ref, sf_out, sb_out, sf, sb, xf_buf, yf_buf, xb_buf, yb_buf, sems):
        i = pl.program_id(0)

        def fetch(step):
            slot = step % n_slot
            copies = []
            for k, (arr, ref, pair, buf) in enumerate(((xs, x_any, f_pair, xf_buf), (ys, y_any, f_pair, yf_buf),
                                                       (xs, x_any, b_pair, xb_buf), (ys, y_any, b_pair, yb_buf))):
                rows = pl.ds(pl.multiple_of(pair(step) * 2 * c, 2 * c), 2 * c)
                src = ref.at[rows, :] if arr[1] is None else ref.at[arr[1], rows, :]
                copies.append(pltpu.make_async_copy(src, buf.at[slot], sems.at[k, slot]))
            return copies

        @pl.when(i == 0)
        def _():
            sf[...] = jnp.zeros_like(sf)
            sb[...] = jnp.zeros_like(sb)
            for cp in fetch(0) + fetch(1):
                cp.start()

        @pl.when(i + 2 < n_pair)
        def _():
            for cp in fetch(i + 2):
                cp.start()

        for cp in fetch(i):
            cp.wait()
        slot = i % n_slot
        xf_ref, yf_ref, xb_ref, yb_ref = xf_buf.at[slot], yf_buf.at[slot], xb_buf.at[slot], yb_buf.at[slot]

        for step in range(2):
            for x_ref, y_ref, tab, cd, out, st, sub in ((xf_ref, yf_ref, tf, cdf_ref, sf_out, sf, f_subs[step]),
                                                        (xb_ref, yb_ref, tb, cdb_ref, sb_out, sb, b_subs[step])):
                rows = pl.ds(sub * c, c)
                for h in range(n_heads):
                    sl = pl.ds(h * HEAD_DIM, HEAD_DIM)
                    out[sub, h] = st[h].astype(BF16)
                    xd = (x_ref[rows, sl].astype(F32) * tab[h]).astype(BF16)
                    st[h] = cd[h][0:1, :] * st[h] + _dot(xd, y_ref[rows, sl], TN)

    assert n_pair >= 2
    st_blk = (2, n_heads, HEAD_DIM, HEAD_DIM)
    ring = pltpu.VMEM((n_slot, 2 * c, s), BF16)
    return pl.pallas_call(
        body, name=name, grid=(n_pair,),
        in_specs=[pl.BlockSpec(memory_space=pl.ANY), pl.BlockSpec(memory_space=pl.ANY),
                  _full((n_heads, c, HEAD_DIM)), _full((n_heads, c, HEAD_DIM)),
                  _full((n_heads, 8, HEAD_DIM)), _full((n_heads, 8, HEAD_DIM))],
        out_specs=[pl.BlockSpec(st_blk, lambda i: (f_pair(i), 0, 0, 0)), pl.BlockSpec(st_blk, lambda i: (b_pair(i), 0, 0, 0))],
        out_shape=[jax.ShapeDtypeStruct((nc, n_heads, HEAD_DIM, HEAD_DIM), BF16)] * 2,
        scratch_shapes=[pltpu.VMEM((n_heads, HEAD_DIM, HEAD_DIM), F32)] * 2 + [ring] * 4
        + [pltpu.SemaphoreType.DMA((4, n_slot))],
        compiler_params=_params(("arbitrary",)),
    )(xs[0], ys[0], tab_f, tab_b, cdf, cdb)


def _state_sweep(qkv, tabs, n_heads, nx, ncc, name):
    return _pair_sweep((qkv, 1), (qkv, 2), tabs["kf"], tabs["kb"], tabs["cdf"], tabs["cdb"], n_heads, nx, ncc, False, name)


MIX_CHUNKS = 2
MIX_ROWS = MIX_CHUNKS * CHUNK


HALO = 16


def _halo_specs(s, t):
    per = MIX_ROWS // HALO
    n_halo = t // HALO

    def prev(g):
        return pl.BlockSpec((None, HALO, s), lambda i: (g, jnp.maximum(i * per - 1, 0), 0))

    def nxt(g):
        return pl.BlockSpec((None, HALO, s), lambda i: (g, jnp.minimum((i + 1) * per, n_halo - 1), 0))

    return prev, nxt


def _conv_input(h_ref, c_ref, hp_ref, hn_ref, cp_ref, cn_ref):
    a = c_ref[...].astype(F32) * h_ref[...].astype(F32)
    before = cp_ref[HALO - 1:HALO].astype(F32) * hp_ref[HALO - 1:HALO].astype(F32)
    after = cn_ref[0:1].astype(F32) * hn_ref[0:1].astype(F32)
    return a, before, after


def _shifted(a, before, after, has_prev, has_next):
    rows = a.shape[0]
    rowi = lax.broadcasted_iota(jnp.int32, a.shape, 0)
    am = jnp.where(rowi == 0, jnp.where(has_prev, before, 0.0), pltpu.roll(a, 1, 0))
    ap = jnp.where(rowi == rows - 1, jnp.where(has_next, after, 0.0), pltpu.roll(a, rows - 1, 0))
    return am, ap


def _neighbours(i, nx, nc):
    nxb, ncb = nx // MIX_CHUNKS, nc // MIX_CHUNKS
    return (i != 0) & (i != nxb), (i != nxb - 1) & (i != ncb - 1)


def _mix_fwd(u, qkv, sf, sb, tabs, conv_w, cnw, gnw, n_heads, nx, ncc, name):
    _, t, s = u.shape
    nc = nx + ncc
    c = CHUNK
    assert nx % MIX_CHUNKS == 0 and ncc % MIX_CHUNKS == 0

    n_step = nc // MIX_CHUNKS
    n_slot = 3
    assert n_step >= 2

    def body(u_any, hp_ref, hn_ref, cp_ref, cn_ref, q_ref, k_ref, v_ref,
             sf_ref, sb_ref, dc_ref, qft, qbt, w_ref, cnw_ref, gnw_ref, y_ref, o_ref, u_buf, sems):
        i = pl.program_id(0)

        def fetch(step):
            rows = pl.ds(pl.multiple_of(step * MIX_ROWS, MIX_ROWS), MIX_ROWS)
            return pltpu.make_async_copy(u_any.at[:, rows, :], u_buf.at[step % n_slot], sems.at[step % n_slot])

        @pl.when(i == 0)
        def _():
            fetch(0).start()
            fetch(1).start()

        @pl.when(i + 2 < n_step)
        def _():
            fetch(i + 2).start()

        fetch(i).wait()
        mine = u_buf.at[i % n_slot]
        h_ref, b_ref, c_ref, z_ref, rz_ref = mine.at[0], mine.at[1], mine.at[2], mine.at[3], mine.at[4]
        has_prev, has_next = _neighbours(i, nx, nc)
        a, before, after = _conv_input(h_ref, c_ref, hp_ref, hn_ref, cp_ref, cn_ref)
        am, ap = _shifted(a, before, after, has_prev, has_next)
        w = w_ref[...]
        y0 = w[0:1] * am + w[1:2] * a + w[2:3] * ap
        yb = b_ref[...].astype(F32) * y0
        r = lax.rsqrt(jnp.mean(yb * yb, axis=-1, keepdims=True) + EPS)
        y_ref[:, pl.ds(0, s)] = (_silu(z_ref[...].astype(F32)) * ((yb * r) * cnw_ref[...])).astype(BF16)
        for sub in range(MIX_CHUNKS):
            rows = pl.ds(sub * c, c)
            for h in range(n_heads):
                sl = pl.ds(h * HEAD_DIM, HEAD_DIM)
                q, k, v = q_ref[rows, sl], k_ref[rows, sl], v_ref[rows, sl]
                p = (_dot(q, k, NT) * dc_ref[h]).astype(BF16)
                o = _dot(p, v, NN)
                qf = q.astype(F32)
                o += _dot((qf * qft[h]).astype(BF16), sf_ref[sub, h], NN)
                o += _dot((qf * qbt[h]).astype(BF16), sb_ref[sub, h], NN)
                o_ref[rows, sl] = o
                mu = jnp.mean(o, axis=-1, keepdims=True)
                var = jnp.mean(jnp.square(o - mu), axis=-1, keepdims=True)
                on = (o - mu) * lax.rsqrt(var + EPS)
                y_ref[rows, pl.ds(s + h * HEAD_DIM, HEAD_DIM)] = (
                    _silu(rz_ref[rows, sl].astype(F32)) * (on * gnw_ref[:, sl])).astype(BF16)

    def seg(g):
        return pl.BlockSpec((None, MIX_ROWS, s), lambda i: (g, i, 0))

    prev, nxt = _halo_specs(s, t)
    row = pl.BlockSpec((MIX_ROWS, s), lambda i: (i, 0))
    st = pl.BlockSpec((MIX_CHUNKS, n_heads, HEAD_DIM, HEAD_DIM), lambda i: (i, 0, 0, 0))
    return pl.pallas_call(
        body, name=name, grid=(n_step,),
        in_specs=[pl.BlockSpec(memory_space=pl.ANY), prev(0), nxt(0), prev(2), nxt(2), seg(0), seg(1), seg(2),
                  st, st, _full((n_heads, c, c)), _full((n_heads, c, HEAD_DIM)), _full((n_heads, c, HEAD_DIM)),
                  _full((3, s)), _full((1, s)), _full((1, s))],
        out_specs=[pl.BlockSpec((MIX_ROWS, 2 * s), lambda i: (i, 0)), row],
        out_shape=[jax.ShapeDtypeStruct((t, 2 * s), BF16), jax.ShapeDtypeStruct((t, s), F32)],
        scratch_shapes=[pltpu.VMEM((n_slot, N_PLAIN, MIX_ROWS, s), U_DTYPE), pltpu.SemaphoreType.DMA((n_slot,))],
        compiler_params=_params(("arbitrary",)),
    )(u, u, u, u, u, qkv, qkv, qkv, sf, sb, tabs["dc"], tabs["qf"], tabs["qb"], conv_w, cnw, gnw)


def _out_proj_prenorm(ycat, w_out, res, mod, nw_next, mod_next, n_lat, name):
    t, d = ycat.shape
    nb = t // ROW_TILE
    nxb = n_lat // ROW_TILE
    split = len(res) == 2

    def body(a_ref, w_ref, *rest):
        res_refs = rest[:len(res)]
        mod_ref, nw_ref, modn_ref, m_ref, xo_ref, hx_ref, xs = rest[len(res):]
        i = pl.program_id(0)

        @pl.when(i == 0)
        def _():
            xs[...] = jnp.zeros_like(xs)

        cur_ctx = jnp.minimum(i, nb - 1) >= nxb
        prev_ctx = i - 1 >= nxb

        def step(cur, prev):
            mv, mn = mod_ref[...], modn_ref[...]
            shift = jnp.where(prev_ctx, mn[3:4], mn[0:1])
            scale = jnp.where(prev_ctx, mn[4:5], mn[1:2])
            hx_ref[...] = _modulate(xs[prev], nw_ref[...], shift, scale)
            m = _dot(a_ref[...], w_ref[...], NN)
            x_res = jnp.where(cur_ctx, res_refs[1][...], res_refs[0][...]) if split else res_refs[0][...]
            x_new = x_res + jnp.where(cur_ctx, mv[5:6], mv[2:3]) * m
            m_ref[...] = m.astype(BF16)
            xo_ref[...] = x_new
            xs[cur] = x_new

        @pl.when(i % 2 == 0)
        def _():
            step(0, 1)

        @pl.when(i % 2 == 1)
        def _():
            step(1, 0)

    cur = pl.BlockSpec((ROW_TILE, d), lambda i: (jnp.minimum(i, nb - 1), 0))
    prev = pl.BlockSpec((ROW_TILE, d), lambda i: (jnp.maximum(i - 1, 0), 0))
    res_specs = list(_split_rows(nxb, nb)(d, lambda i: jnp.minimum(i, nb - 1))) if split else [cur]
    return pl.pallas_call(
        body, name=name, grid=(nb + 1,),
        in_specs=[cur, _full((d, d))] + res_specs + [_full((8, d)), _full((1, d)), _full((8, d))],
        out_specs=[cur, cur, prev],
        out_shape=[jax.ShapeDtypeStruct((t, d), BF16), jax.ShapeDtypeStruct((t, d), F32),
                   jax.ShapeDtypeStruct((t, d), BF16)],
        scratch_shapes=[pltpu.VMEM((2, ROW_TILE, d), F32)],
        compiler_params=_params(("arbitrary",)))(ycat, w_out, *res, mod, nw_next, mod_next)


def _out_proj_loss(ycat, w_out, xt, mod, tgt, fnw, n_lat, name):
    t, d = xt.shape
    nb = t // ROW_TILE
    nxb = n_lat // ROW_TILE

    def body(a_ref, w_ref, x_ref, mod_ref, t_ref, fw_ref, dx_ref, dm_ref, loss_ref, dw_ref, gacc_ref, xs, ms):
        i = pl.program_id(0)

        @pl.when(i == 0)
        def _():
            xs[...] = jnp.zeros_like(xs)
            ms[...] = jnp.zeros_like(ms)
            loss_ref[...] = jnp.zeros_like(loss_ref)
            dw_ref[...] = jnp.zeros_like(dw_ref)
            gacc_ref[...] = jnp.zeros_like(gacc_ref)

        def step(cur, prev):
            mv = mod_ref[...]
            x_prev, m_prev = xs[prev], ms[prev]
            valid = (i >= 1) & (i - 1 < nxb)
            w = fw_ref[...]
            r = lax.rsqrt(jnp.mean(x_prev * x_prev, axis=-1, keepdims=True) + EPS)
            xn = x_prev * r
            e = xn * w - t_ref[...]
            loss = 0.5 * jnp.sum(jnp.mean(e * e, axis=-1, keepdims=True), axis=0, keepdims=True)
            loss_ref[...] += jnp.where(valid, loss, 0.0)
            dy = e * (1.0 / d)
            dw_ref[0:1, :] += jnp.where(valid, jnp.sum(dy * xn, axis=0, keepdims=True), 0.0)
            dxn = dy * w
            dx = jnp.where(valid, r * (dxn - xn * jnp.mean(dxn * xn, axis=-1, keepdims=True)), 0.0)
            dx_ref[...] = dx
            dm_ref[...] = (dx * mv[2:3]).astype(BF16)
            gacc_ref[2:3, :] += jnp.sum(dx * m_prev, axis=0, keepdims=True)

            m = _dot(a_ref[...], w_ref[...], NN)
            gate = jnp.where(jnp.minimum(i, nb - 1) >= nxb, mv[5:6], mv[2:3])
            xs[cur] = x_ref[...] + gate * m
            ms[cur] = m

        @pl.when(i % 2 == 0)
        def _():
            step(0, 1)

        @pl.when(i % 2 == 1)
        def _():
            step(1, 0)

    cur = pl.BlockSpec((ROW_TILE, d), lambda i: (jnp.minimum(i, nb - 1), 0))
    prev = pl.BlockSpec((ROW_TILE, d), lambda i: (jnp.maximum(i - 1, 0), 0))
    return pl.pallas_call(
        body, name=name, grid=(nb + 1,),
        in_specs=[cur, _full((d, d)), cur, _full((8, d)),
                  pl.BlockSpec((ROW_TILE, d), lambda i: (jnp.clip(i - 1, 0, nxb - 1), 0)), _full((1, d))],
        out_specs=[prev, prev, _full((8, HEAD_DIM)), _full((8, d)), _full((8, d))],
        out_shape=[jax.ShapeDtypeStruct((t, d), F32), jax.ShapeDtypeStruct((t, d), BF16),
                   jax.ShapeDtypeStruct((8, HEAD_DIM), F32), jax.ShapeDtypeStruct((8, d), F32),
                   jax.ShapeDtypeStruct((8, d), F32)],
        scratch_shapes=[pltpu.VMEM((2, ROW_TILE, d), F32), pltpu.VMEM((2, ROW_TILE, d), F32)],
        compiler_params=_params(("arbitrary",)))(ycat, w_out, xt, mod, tgt, fnw)


def _matmul_nt(a, w, tn, name, after=()):
    t, k = a.shape
    n = w.shape[0]
    tm = _mm_rows(t)

    def body(a_ref, w_ref, *rest):
        rest[-1][...] = _dot(a_ref[...], w_ref[...], NT)

    return pl.pallas_call(
        body, name=name, grid=(n // tn, t // tm),
        in_specs=[pl.BlockSpec((tm, k), lambda j, i: (i, 0)), pl.BlockSpec((tn, k), lambda j, i: (j, 0))]
        + [pl.BlockSpec(memory_space=pl.ANY)] * len(after),
        out_specs=pl.BlockSpec((tm, tn), lambda j, i: (i, j)),
        out_shape=jax.ShapeDtypeStruct((t, n), F32),
        compiler_params=_params(("parallel", "parallel")))(a, w, *after)


def _weight_grad(a, b, bm, bt, name):
    t, m = a.shape
    n_g, _, n = b.shape
    nt = t // bt

    def body(a_ref, b_ref, o_ref, acc):
        k = pl.program_id(2)

        @pl.when(k == 0)
        def _():
            acc[...] = jnp.zeros_like(acc)

        acc[...] += _dot(a_ref[...], b_ref[...], TN)

        @pl.when(k == nt - 1)
        def _():
            o_ref[...] = acc[...].astype(o_ref.dtype)

    return pl.pallas_call(
        body, name=name, grid=(n_g, m // bm, nt),
        in_specs=[pl.BlockSpec((bt, bm), lambda g, i, k: (k, i)), pl.BlockSpec((None, bt, n), lambda g, i, k: (g, k, 0))],
        out_specs=pl.BlockSpec((None, bm, n), lambda g, i, k: (g, i, 0)),
        out_shape=jax.ShapeDtypeStruct((n_g, m, n), BF16),
        scratch_shapes=[pltpu.VMEM((bm, n), F32)],
        compiler_params=_params(("parallel", "parallel", "arbitrary")))(a, b)


def _weight_grad_beside_prenorm_bwd(a, b, dhx, xt, dxo, nw, mod, below, n_lat, name):
    t, m = a.shape
    n_g, _, n = b.shape
    d = xt.shape[1]
    bt = _mm_rows(t)
    nt = t // bt
    rows = t // (n_g * nt)
    n_piece = 2 if rows % 32 == 0 and m % 2 == 0 else 1
    rows_p, m_p = rows // n_piece, m // n_piece
    assert rows * n_g * nt == t and rows_p % 8 == 0

    def body(a_ref, b_ref, dh_ref, x_ref, dxo_ref, nw_ref, mod_ref, m_ref, modb_ref,
             o_ref, dx_ref, acc_ref, dm_ref, gacc_ref, acc):
        g, k = pl.program_id(0), pl.program_id(1)
        step = g * nt + k

        @pl.when(step == 0)
        def _():
            acc_ref[...] = jnp.zeros_like(acc_ref)
            gacc_ref[...] = jnp.zeros_like(gacc_ref)

        @pl.when(k == 0)
        def _():
            acc[...] = jnp.zeros_like(acc)

        mv, mb, nw_v = mod_ref[...], modb_ref[...], nw_ref[...]
        for p in range(n_piece):
            rs = pl.ds(p * rows_p, rows_p)
            rowi = step * rows + p * rows_p + lax.broadcasted_iota(jnp.int32, (rows_p, 1), 0)
            ctx = rowi >= n_lat
            w_lat = jnp.where(ctx, 0.0, 1.0)
            w_ctx = 1.0 - w_lat
            scale1 = 1.0 + jnp.where(ctx, mv[4:5], mv[1:2])
            x = x_ref[rs, :]
            r = lax.rsqrt(jnp.mean(x * x, axis=-1, keepdims=True) + EPS)
            xn = x * r
            dh = dh_ref[rs, :]
            dsc = dh * (xn * nw_v)
            acc_ref[0:1, :] += jnp.sum(dh * w_lat, axis=0, keepdims=True)
            acc_ref[1:2, :] += jnp.sum(dsc * w_lat, axis=0, keepdims=True)
            acc_ref[3:4, :] += jnp.sum(dh * w_ctx, axis=0, keepdims=True)
            acc_ref[4:5, :] += jnp.sum(dsc * w_ctx, axis=0, keepdims=True)
            acc_ref[6:7, :] += jnp.sum(dh * scale1 * xn, axis=0, keepdims=True)
            dxn = dh * (nw_v * scale1)
            dx = dxo_ref[rs, :] + r * (dxn - xn * jnp.mean(dxn * xn, axis=-1, keepdims=True))
            dx_ref[rs, :] = dx
            dm_ref[rs, :] = (dx * jnp.where(ctx, mb[5:6], mb[2:3])).astype(BF16)
            dg = dx * m_ref[rs, :].astype(F32)
            gacc_ref[2:3, :] += jnp.sum(dg * w_lat, axis=0, keepdims=True)
            gacc_ref[5:6, :] += jnp.sum(dg * w_ctx, axis=0, keepdims=True)

            ms_ = pl.ds(p * m_p, m_p)
            acc[ms_, :] += _dot(a_ref[:, ms_], b_ref[...], TN)

        @pl.when(k == nt - 1)
        def _():
            o_ref[...] = acc[...].astype(o_ref.dtype)

    side = pl.BlockSpec((rows, d), lambda g, k: (g * nt + k, 0))
    acc8 = _full((8, d))
    return pl.pallas_call(
        body, name=name, grid=(n_g, nt),
        in_specs=[pl.BlockSpec((bt, m), lambda g, k: (k, 0)), pl.BlockSpec((None, bt, n), lambda g, k: (g, k, 0)),
                  side, side, side, _full((1, d)), acc8, side, acc8],
        out_specs=[pl.BlockSpec((None, m, n), lambda g, k: (g, 0, 0)), side, acc8, side, acc8],
        out_shape=[jax.ShapeDtypeStruct((n_g, m, n), BF16), jax.ShapeDtypeStruct((t, d), F32),
                   jax.ShapeDtypeStruct((8, d), F32), jax.ShapeDtypeStruct((t, d), BF16),
                   jax.ShapeDtypeStruct((8, d), F32)],
        scratch_shapes=[pltpu.VMEM((m, n), F32)],
        compiler_params=_params(("arbitrary", "arbitrary")))(a, b, dhx, xt, dxo, nw, mod, *below)


def _mix_bwd_a(dycat, u, o, conv_w, cnw, gnw, n_heads, nx, ncc, name):
    _, t, s = u.shape
    nc = nx + ncc

    def body(dy_ref, h_ref, b_ref, c_ref, z_ref, rz_ref, hp_ref, hn_ref, cp_ref, cn_ref, o_ref, w_ref,
             cnw_ref, gnw_ref, g_ref, dz_ref, db_ref, drz_ref, do_ref, acc_ref):
        i = pl.program_id(0)

        @pl.when(i == 0)
        def _():
            acc_ref[...] = jnp.zeros_like(acc_ref)

        has_prev, has_next = _neighbours(i, nx, nc)
        a, before, after = _conv_input(h_ref, c_ref, hp_ref, hn_ref, cp_ref, cn_ref)
        am, ap = _shifted(a, before, after, has_prev, has_next)
        w = w_ref[...]
        y0 = w[0:1] * am + w[1:2] * a + w[2:3] * ap
        bb = b_ref[...].astype(F32)
        yb = bb * y0
        r = lax.rsqrt(jnp.mean(yb * yb, axis=-1, keepdims=True) + EPS)
        ynn = yb * r
        z = z_ref[...].astype(F32)
        dyc = dy_ref[:, pl.ds(0, s)]
        cw = cnw_ref[...]
        sz, dsz = _silu_and_slope(z)
        dz_ref[...] = (dyc * (ynn * cw) * dsz).astype(BF16)
        dyn = dyc * sz
        acc_ref[0:1, :] += jnp.sum(dyn * ynn, axis=0, keepdims=True)
        dynn = dyn * cw
        dyb = r * (dynn - ynn * jnp.mean(dynn * ynn, axis=-1, keepdims=True))
        db_ref[...] = (dyb * y0).astype(BF16)
        g_ref[...] = dyb * bb
        for h in range(n_heads):
            sl = pl.ds(h * HEAD_DIM, HEAD_DIM)
            ov = o_ref[:, sl]
            mu = jnp.mean(ov, axis=-1, keepdims=True)
            var = jnp.mean(jnp.square(ov - mu), axis=-1, keepdims=True)
            rs = lax.rsqrt(var + EPS)
            on = (ov - mu) * rs
            dyr = dy_ref[:, pl.ds(s + h * HEAD_DIM, HEAD_DIM)]
            rz = rz_ref[:, sl].astype(F32)
            gw = gnw_ref[:, sl]
            srz, dsrz = _silu_and_slope(rz)
            drz_ref[:, sl] = (dyr * (on * gw) * dsrz).astype(BF16)
            dyg = dyr * srz
            acc_ref[1:2, sl] += jnp.sum(dyg * on, axis=0, keepdims=True)
            don = dyg * gw
            do = rs * (don - jnp.mean(don, axis=-1, keepdims=True)
                       - on * jnp.mean(don * on, axis=-1, keepdims=True))
            do_ref[:, sl] = do.astype(BF16)

    def seg(g):
        return pl.BlockSpec((None, MIX_ROWS, s), lambda i: (g, i, 0))

    prev, nxt = _halo_specs(s, t)
    row = pl.BlockSpec((MIX_ROWS, s), lambda i: (i, 0))
    return pl.pallas_call(
        body, name=name, grid=(nc // MIX_CHUNKS,),
        in_specs=[pl.BlockSpec((MIX_ROWS, 2 * s), lambda i: (i, 0)), seg(0), seg(1), seg(2), seg(3), seg(4),
                  prev(0), nxt(0), prev(2), nxt(2), row, _full((3, s)), _full((1, s)), _full((1, s))],
        out_specs=[row, row, row, row, row, _full((8, s))],
        out_shape=[jax.ShapeDtypeStruct((t, s), F32)] + [jax.ShapeDtypeStruct((t, s), BF16)] * 4
        + [jax.ShapeDtypeStruct((8, s), F32)],
        compiler_params=_params(("arbitrary",)),
    )(dycat, u, u, u, u, u, u, u, u, u, o, conv_w, cnw, gnw)


def _grad_state_sweep(qkv, do, tabs, n_heads, nx, ncc, name):
    return _pair_sweep((qkv, 0), (do, None), tabs["qf"], tabs["qb"], tabs["cdf"], tabs["cdb"], n_heads, nx, ncc, True, name)


def _mix_bwd_b(u, g, dz, db, drz, qkv, do, sf, sb, gf, gb, tabs, cos, sa, sb_tab, conv_w,
               n_heads, nx, ncc, name):
    _, t, s = u.shape
    nc = nx + ncc
    c = CHUNK
    k_scale = HEAD_DIM ** -0.5

    def body(h_ref, c_ref, g_ref, gp_ref, gn_ref, dz_ref, db_ref, drz_ref, q_ref, k_ref, v_ref, do_ref,
             sf_ref, sb_ref, gf_ref, gb_ref, dc_t, dlf_t, dlb_t, qft, kft, qbt, kbt, cdf, cdb, lg_ref,
             cos_ref, sa_ref, sb_ref2, w_ref, du_ref, dw_ref, dlg_ref):
        i = pl.program_id(0)

        @pl.when(i == 0)
        def _():
            dw_ref[...] = jnp.zeros_like(dw_ref)
            dlg_ref[...] = jnp.zeros_like(dlg_ref)

        has_prev, has_next = _neighbours(i, nx, nc)
        gv = g_ref[...]
        gm, gp = _shifted(gv, gp_ref[7:8], gn_ref[0:1], has_prev, has_next)
        w = w_ref[...]
        da = w[0:1] * gp + w[1:2] * gv + w[2:3] * gm
        hh, cc = h_ref[...].astype(F32), c_ref[...].astype(F32)
        du_ref[0] = (da * cc).astype(BF16)
        du_ref[2] = (da * hh).astype(BF16)
        a = cc * hh
        dw_ref[0:1, :] += jnp.sum(a * gp, axis=0, keepdims=True)
        dw_ref[1:2, :] += jnp.sum(a * gv, axis=0, keepdims=True)
        dw_ref[2:3, :] += jnp.sum(a * gm, axis=0, keepdims=True)
        du_ref[1] = db_ref[...]
        du_ref[3] = dz_ref[...]
        du_ref[7] = drz_ref[...]

        pos = lax.broadcasted_iota(jnp.int32, (c, HEAD_DIM), 0).astype(F32)
        w_q_f, w_q_b, w_k_f = pos + 1.0, c - pos, c - 1.0 - pos
        row8 = lax.broadcasted_iota(jnp.int32, (8, HEAD_DIM), 0)
        lane8 = lax.broadcasted_iota(jnp.int32, (8, HEAD_DIM), 1)
        dlg = jnp.zeros((8, HEAD_DIM), F32)
        for sub, h in [(sub, h) for sub in range(MIX_CHUNKS) for h in range(n_heads)]:
            rows = pl.ds(sub * c, c)
            co, ra, rb = cos_ref[rows, :], sa_ref[rows, :], sb_ref2[rows, :]
            sl = pl.ds(h * HEAD_DIM, HEAD_DIM)
            q, k, v, do = q_ref[rows, sl], k_ref[rows, sl], v_ref[rows, sl], do_ref[rows, sl]
            qf, kf, dof = q.astype(F32), k.astype(F32), do.astype(F32)
            s_f, s_b, g_f, g_b = sf_ref[sub, h], sb_ref[sub, h], gf_ref[sub, h], gb_ref[sub, h]
            p = _dot(q, k, NT)
            pd = _dot(do, v, NT)
            pdd = (pd * dc_t[h]).astype(BF16)
            dq = _dot(pdd, k, NN)
            dk = _dot(pdd, q, TN)
            dv = _dot((p * dc_t[h]).astype(BF16), do, TN)
            dq_f = _dot((dof * qft[h]).astype(BF16), s_f, NT)
            dq_b = _dot((dof * qbt[h]).astype(BF16), s_b, NT)
            dk_f = _dot(v, g_f, NT) * kft[h]
            dk_b = _dot(v, g_b, NT) * kbt[h]
            dv += _dot((kf * kft[h]).astype(BF16), g_f, NN) + _dot((kf * kbt[h]).astype(BF16), g_b, NN)
            ppd = p * pd
            cd_f, cd_b = cdf[h][0:1, :], cdb[h][0:1, :]
            t_f = _sum_all(dlf_t[h] * ppd + w_q_f * qf * dq_f + w_k_f * kf * dk_f
                           + float(c) * (cd_f * (g_f.astype(F32) * s_f.astype(F32))))
            t_b = _sum_all(dlb_t[h] * ppd + w_q_b * qf * dq_b + pos * kf * dk_b
                           + float(c) * (cd_b * (g_b.astype(F32) * s_b.astype(F32))))
            dlg += jnp.where((row8 == 0) & (lane8 == h), t_f, 0.0) + jnp.where((row8 == 1) & (lane8 == h), t_b, 0.0)
            du_ref[4, rows, sl] = _rope_bwd(dq + dq_f + dq_b, co, ra, rb).astype(BF16)
            du_ref[5, rows, sl] = (_rope_bwd(dk + dk_f + dk_b, co, ra, rb) * k_scale).astype(BF16)
            du_ref[6, rows, sl] = dv.astype(BF16)
        dlg_ref[...] += dlg

        @pl.when(i == nc // MIX_CHUNKS - 1)
        def _():
            dlg_ref[...] = dlg_ref[...] * lg_ref[...]

    def seg(gi):
        return pl.BlockSpec((None, MIX_ROWS, s), lambda i: (gi, i, 0))

    per = MIX_ROWS // 8
    n8 = t // 8
    row = pl.BlockSpec((MIX_ROWS, s), lambda i: (i, 0))
    st = pl.BlockSpec((MIX_CHUNKS, n_heads, HEAD_DIM, HEAD_DIM), lambda i: (i, 0, 0, 0))
    tab = pl.BlockSpec((MIX_ROWS, HEAD_DIM), lambda i: (i, 0))
    hc = _full((n_heads, c, HEAD_DIM))
    cc_ = _full((n_heads, c, c))
    h8 = _full((n_heads, 8, HEAD_DIM))
    return pl.pallas_call(
        body, name=name, grid=(nc // MIX_CHUNKS,),
        in_specs=[seg(0), seg(2), row,
                  pl.BlockSpec((8, s), lambda i: (jnp.maximum(i * per - 1, 0), 0)),
                  pl.BlockSpec((8, s), lambda i: (jnp.minimum((i + 1) * per, n8 - 1), 0)),
                  row, row, row, seg(0), seg(1), seg(2), row, st, st, st, st, cc_, cc_, cc_, hc, hc, hc, hc, h8, h8,
                  _full((8, HEAD_DIM)), tab, tab, tab, _full((3, s))],
        out_specs=[pl.BlockSpec((8, MIX_ROWS, s), lambda i: (0, i, 0)), _full((8, s)), _full((8, HEAD_DIM))],
        out_shape=[jax.ShapeDtypeStruct((8, t, s), BF16), jax.ShapeDtypeStruct((8, s), F32),
                   jax.ShapeDtypeStruct((8, HEAD_DIM), F32)],
        compiler_params=_params(("arbitrary",)),
    )(u, u, g, g, g, dz, db, drz, qkv, qkv, qkv, do, sf, sb, gf, gb, tabs["dc"], tabs["dlf"], tabs["dlb"],
      tabs["qf"], tabs["kf"], tabs["qb"], tabs["kb"], tabs["cdf"], tabs["cdb"], tabs["lg"], cos, sa, sb_tab, conv_w)


def _in_proj_bwd(du, wgs, tm, gs, name, after=()):
    n_seg, t, s = du.shape
    d = wgs[0].shape[1]
    n_w = len(wgs)
    widths = [w.shape[2] for w in wgs]
    assert sum(widths) == s

    def body(a_ref, *rest):
        w_refs, o_ref = rest[:n_w], rest[-1]
        g = pl.program_id(1)
        part = None
        for j in range(gs):
            col = 0
            for w_ref, width in zip(w_refs, widths):
                term = _dot(a_ref[j, :, col:col + width], w_ref[j], NT)
                part = term if part is None else part + term
                col += width

        @pl.when(g == 0)
        def _():
            o_ref[...] = part

        @pl.when(g > 0)
        def _():
            o_ref[...] += part

    return pl.pallas_call(
        body, name=name, grid=(t // tm, n_seg // gs),
        in_specs=[pl.BlockSpec((gs, tm, s), lambda i, g: (g, i, 0))]
        + [pl.BlockSpec((gs, d, width), lambda i, g: (g, 0, 0)) for width in widths]
        + [pl.BlockSpec(memory_space=pl.ANY)] * len(after),
        out_specs=pl.BlockSpec((tm, d), lambda i, g: (i, 0)),
        out_shape=jax.ShapeDtypeStruct((t, d), F32),
        compiler_params=_params(("parallel", "arbitrary")))(du, *wgs, *after)


def _prenorm_bwd_first(dhx, x, ctx, dxo, nw, mod, name):
    n_lat, d = x.shape
    t = n_lat + ctx.shape[0]
    nxb = n_lat // ROW_TILE
    nb = t // ROW_TILE
    n_slot = 3
    assert nxb >= 2

    def body(dh_any, x_any, c_any, dxo_any, nw_ref, mod_ref, dx_ref, acc_ref, dh_buf, x_buf, dxo_buf, sems):
        i = pl.program_id(0)

        def fetch(step, from_ctx):
            slot = step % n_slot
            rows = pl.ds(pl.multiple_of(step * ROW_TILE, ROW_TILE), ROW_TILE)
            own = pl.ds(pl.multiple_of((step - nxb) * ROW_TILE, ROW_TILE), ROW_TILE) if from_ctx else rows
            return [pltpu.make_async_copy(dh_any.at[rows, :], dh_buf.at[slot], sems.at[0, slot]),
                    pltpu.make_async_copy((c_any if from_ctx else x_any).at[own, :], x_buf.at[slot], sems.at[1, slot]),
                    pltpu.make_async_copy(dxo_any.at[rows, :], dxo_buf.at[slot], sems.at[2, slot])]

        @pl.when(i == 0)
        def _():
            acc_ref[...] = jnp.zeros_like(acc_ref)
            for cp in fetch(0, False) + fetch(1, False):
                cp.start()

        @pl.when(i + 2 < nxb)
        def _():
            for cp in fetch(i + 2, False):
                cp.start()

        @pl.when((i + 2 >= nxb) & (i + 2 < nb))
        def _():
            for cp in fetch(i + 2, True):
                cp.start()

        slot = i % n_slot
        head = pl.ds(0, ROW_TILE)
        for k, (src, buf) in enumerate(((dh_any, dh_buf), (x_any, x_buf), (dxo_any, dxo_buf))):
            pltpu.make_async_copy(src.at[head, :], buf.at[slot], sems.at[k, slot]).wait()

        ctx = i >= nxb
        m = mod_ref[...]
        scale1 = 1.0 + jnp.where(ctx, m[4:5], m[1:2])
        x = x_buf[slot]
        nw_v = nw_ref[...]
        r = lax.rsqrt(jnp.mean(x * x, axis=-1, keepdims=True) + EPS)
        xn = x * r
        dh = dh_buf[slot]
        dshift = jnp.sum(dh, axis=0, keepdims=True)
        dscale = jnp.sum(dh * (xn * nw_v), axis=0, keepdims=True)
        acc_ref[6:7, :] += jnp.sum(dh * scale1 * xn, axis=0, keepdims=True)
        dxn = dh * (nw_v * scale1)
        dx = dxo_buf[slot] + r * (dxn - xn * jnp.mean(dxn * xn, axis=-1, keepdims=True))

        @pl.when(i < nxb)
        def _():
            acc_ref[0:1, :] += dshift
            acc_ref[1:2, :] += dscale
            dx_ref[...] = dx

        @pl.when(i >= nxb)
        def _():
            acc_ref[3:4, :] += dshift
            acc_ref[4:5, :] += dscale

    lat, _ = _split_rows(nxb, nb)(d)
    acc = _full((8, d))
    hbm = pl.BlockSpec(memory_space=pl.ANY)
    ring = pltpu.VMEM((n_slot, ROW_TILE, d), F32)
    return pl.pallas_call(body, name=name, grid=(nb,),
                          in_specs=[hbm, hbm, hbm, hbm, _full((1, d)), acc],
                          out_specs=[lat, acc],
                          out_shape=[jax.ShapeDtypeStruct((n_lat, d), F32), jax.ShapeDtypeStruct((8, d), F32)],
                          scratch_shapes=[ring, ring, ring, pltpu.SemaphoreType.DMA((3, n_slot))],
                          compiler_params=_params(("arbitrary",)))(dhx, x, ctx, dxo, nw, mod)


def _adamw(g, w, m, v):
    m = ADAM_B1 * m + (1.0 - ADAM_B1) * g
    v = ADAM_B2 * v + (1.0 - ADAM_B2) * jnp.square(g)
    m_hat = m / (1.0 - ADAM_B1 ** ADAM_STEP)
    v_hat = v / (1.0 - ADAM_B2 ** ADAM_STEP)
    delta = -ADAM_LR * (m_hat / (jnp.sqrt(v_hat) + ADAM_EPS) + ADAM_WD * w)
    return delta, m, v


def _sum_adamw(parts, w, m, v, name, row0=0, into=None):
    n_p, r, n = parts.shape
    r_all = w.shape[0]
    part_block_bytes = 4 * 1024 * 1024
    br = 8
    for cand in (512, 256, 128, 64, 32, 16):
        if r % cand == 0 and row0 % cand == 0 and n_p * cand * n * parts.dtype.itemsize <= part_block_bytes:
            br = cand
            break
    blk0 = row0 // br

    def body(p_ref, w_ref, m_ref, v_ref, *rest):
        g_out, d_out, m_out, v_out = rest[-4:]
        g = p_ref[0].astype(F32)
        for j in range(1, n_p):
            g = g + p_ref[j].astype(F32)
        g_out[...] = g
        d_out[...], m_out[...], v_out[...] = _adamw(g, w_ref[...], m_ref[...], v_ref[...])

    row = pl.BlockSpec((br, n), lambda i: (i + blk0, 0))
    kept = [] if into is None else list(into)
    return pl.pallas_call(body, name=name, grid=(r // br,),
                          in_specs=[pl.BlockSpec((n_p, br, n), lambda i: (0, i, 0)), row, row, row]
                          + [pl.BlockSpec(memory_space=pl.ANY)] * len(kept),
                          out_specs=[row] * 4, out_shape=[jax.ShapeDtypeStruct((r_all, n), F32)] * 4,
                          input_output_aliases={4 + j: j for j in range(len(kept))},
                          compiler_params=_params(("parallel",)))(parts, w, m, v, *kept)


def _rope_tables(n_lat, n_ctx):
    f = HEAD_DIM // 4
    rows = n_lat // GRID_W
    inv = ROPE_BASE ** (-jnp.arange(f, dtype=F32) / f)
    ang_r = jnp.arange(rows).astype(F32)[:, None] * inv[None, :]
    ang_c = jnp.arange(GRID_W).astype(F32)[:, None] * inv[None, :]

    cr, sr, cc, sc = jnp.cos(ang_r), jnp.sin(ang_r), jnp.cos(ang_c), jnp.sin(ang_c)
    zr, zc = jnp.zeros_like(cr), jnp.zeros_like(cc)

    def table(by_row, by_col):
        both = by_row[:, None, :] + by_col[None, :, :]
        return both.reshape(n_lat, HEAD_DIM)

    cos = table(jnp.concatenate([cr, cr, zr, zr], axis=-1), jnp.concatenate([zc, zc, cc, cc], axis=-1))
    sa = table(jnp.concatenate([-sr, zr, zr, zr], axis=-1), jnp.concatenate([zc, zc, -sc, zc], axis=-1))
    sb = table(jnp.concatenate([zr, sr, zr, zr], axis=-1), jnp.concatenate([zc, zc, zc, sc], axis=-1))
    pad = jnp.zeros((n_ctx, HEAD_DIM), F32)
    return (jnp.concatenate([cos, pad + 1.0], axis=0), jnp.concatenate([sa, pad], axis=0),
            jnp.concatenate([sb, pad], axis=0))


def _pad_rows(a, rows):
    return jnp.pad(a, [(0, rows - a.shape[0])] + [(0, 0)] * (a.ndim - 1))


def _pad_cols(a, cols):
    return jnp.pad(a, [(0, 0), (0, cols - a.shape[1])])


def kernel(x, c, ctx, c_ctx, norm_w, w_mod, b_mod, w_in, conv_w, conv_norm_w, ret_norm_w, ret_decay_f, ret_decay_b, w_out, final_norm_w, loss_target, m_c_ctx, m_norm_w, m_w_mod, m_b_mod, m_w_in, m_conv_w, m_conv_norm_w, m_ret_norm_w, m_ret_decay_f, m_ret_decay_b, m_w_out, m_final_norm_w, v_c_ctx, v_norm_w, v_w_mod, v_b_mod, v_w_in, v_conv_w, v_conv_norm_w, v_ret_norm_w, v_ret_decay_f, v_ret_decay_b, v_w_out, v_final_norm_w):
    depth = norm_w.shape[0]
    n_lat, d = x.shape[1], x.shape[2]
    n_ctx = ctx.shape[1]
    s = d // 2
    n_heads = ret_decay_f.shape[1]
    nx, ncc = n_lat // CHUNK, n_ctx // CHUNK
    n_mod = w_mod.shape[2]
    n_cw = conv_w.shape[2]
    r_out = w_out.shape[1]
    assert s == n_heads * HEAD_DIM and w_in.shape[2] == s and N_DEV * r_out == d
    assert n_lat % ROW_TILE == 0 and n_ctx % ROW_TILE == 0 and 3 * depth * n_cw <= d and N_DEV * n_mod == 3 * d
    me = 4 * lax.axis_index("x") + 2 * lax.axis_index("y") + lax.axis_index("c")

    w_in_bf = [w_in[l].astype(BF16) for l in range(depth)]
    w_out_bf = [w_out[l].astype(BF16) for l in range(depth)]

    first = jnp.concatenate([c.reshape(1, d), _pad_cols(conv_w.reshape(1, -1), d), jnp.zeros((6, d), F32)], axis=0)
    (first_g,) = _all_gather([first], "gather_cond")
    first_g = first_g.reshape(N_DEV, 8, d)
    c_all = first_g[:, 0, :]
    conv_full = first_g[:, 1, :3 * depth * n_cw].reshape(N_DEV, depth, 3, n_cw)
    conv_full = conv_full.transpose(1, 2, 0, 3).reshape(depth, 3, N_DEV * n_cw)
    c9 = jnp.concatenate([c_all, c_ctx.reshape(1, d), jnp.zeros((7, d), F32)], axis=0)

    b_sh = lax.dynamic_slice(b_mod, (0, me * n_mod), (depth, n_mod))
    mod_sh = jnp.concatenate([_mod_rows(c9, w_mod[l], b_sh[l:l + 1], f"mod_rows_l{l}") for l in range(depth)], axis=0)
    (mod_g,) = _all_gather([mod_sh], "gather_mod")
    mod_g = mod_g.reshape(N_DEV, depth, 16, n_mod)
    mods = []
    for l in range(depth):
        mine = lax.dynamic_index_in_dim(mod_g[:, l], me, axis=1, keepdims=False).reshape(3, d)
        cx = mod_g[:, l, 8, :].reshape(3, d)
        mods.append(jnp.concatenate([mine, cx, jnp.zeros((2, d), F32)], axis=0))

    halves = [w_in_bf[0][:, :s // 2], w_in_bf[0][:, s // 2:]]
    near, order = [], [mod_g]
    for j, part in enumerate(halves):
        near.append(_push_start([part], [_landing(part, me)], "near", f"w_in0_start_{j}", after=order))
        order = near[-1][4:]
    pending = []
    for k in range(depth):
        srcs = [w_out_bf[k]] + ([w_in_bf[k]] if k > 0 else [])
        started = _push_start(srcs, [_landing(a, me) for a in srcs], "gather", f"weights_start_l{k}", after=order)
        pending.append(started[:4])
        order = started[4:]
    w_in_g = [None] * depth
    w_out_g = [None] * depth

    cos, sa, sb_tab = _rope_tables(n_lat, n_ctx)
    t_all = n_lat + n_ctx

    saved = []
    xt = hx_next = None
    for l in range(depth):
        tiles = _tiles(l, t_all, d)
        names = ["dc", "dlf", "dlb", "qf", "kf", "qb", "kb", "cdf", "cdb", "lg"]
        dec = jnp.stack([ret_decay_f[l], ret_decay_b[l]], axis=0)
        tabs = dict(zip(names, _decay_tables(dec, n_heads, f"decay_tables_l{l}")))
        if l == 0:
            hx = _prenorm_first(x[0], ctx[0], norm_w[0:1], mods[0], "prenorm_l0", after=order)
            gathered, out, after = [], None, hx
            for j in range(2):
                (landed,) = _push_wait(*near[j][:4], "near", after, f"w_in0_wait_{j}")
                relay = _push_start([], [landed], "relay", f"w_in0_relay_start_{j}")
                (landed,) = _push_wait(*relay[:4], "relay", relay[4], f"w_in0_relay_wait_{j}")
                gathered.append(landed)
                out = _in_proj(hx, landed, cos, sa, sb_tab, s, j, tiles["in_tm_half"], f"in_proj_l0_{j}", into=out)
                after = out[0]
            u, qkv = out
            w_in_g[0] = gathered
        else:
            landed = _push_wait(*pending[l], "gather", xt, f"weights_wait_l{l}")
            w_out_g[l], w_in_g[l] = landed[0].reshape(d, d), [landed[1]]
            hx = hx_next
            u, qkv = _in_proj(hx, w_in_g[l][0], cos, sa, sb_tab, s, 0, tiles["in_tm"], f"in_proj_l{l}")
        sf, sb = _state_sweep(qkv, tabs, n_heads, nx, ncc, f"state_sweep_l{l}")
        ycat, o = _mix_fwd(u, qkv, sf, sb, tabs, conv_full[l], conv_norm_w[l:l + 1], ret_norm_w[l:l + 1],
                           n_heads, nx, ncc, f"mix_fwd_l{l}")
        if l == 0:
            (landed,) = _push_wait(*pending[0], "gather", ycat, "weights_wait_l0")
            w_out_g[0] = landed.reshape(d, d)
        m_res = x_new = None
        if l < depth - 1:
            res = (x[0], ctx[0]) if l == 0 else (xt,)
            m_res, x_new, hx_next = _out_proj_prenorm(ycat, w_out_g[l], res, mods[l], norm_w[l + 1:l + 2], mods[l + 1],
                                                      n_lat, f"out_proj_l{l}")
        else:
            dxt, dm, loss_blk, dfnw, gate_acc = _out_proj_loss(ycat, w_out_g[l], xt, mods[l], loss_target[0],
                                                               final_norm_w.reshape(1, d), n_lat, f"out_proj_loss_l{l}")
        saved.append(dict(tabs=tabs, xt=xt, hx=hx, u=u, qkv=qkv, sf=sf, sb=sb, ycat=ycat, o=o, m=m_res, tiles=tiles))
        xt = x_new

    dmod_x, dmod_c, dnw, dcnw, dgnw, dconv, ddec, dwin, dwout = [], [], [], [], [], [], [], [], []
    started_token = ()
    for l in reversed(range(depth)):
        sv = saved[l]
        tiles = sv["tiles"]
        dycat = _matmul_nt(dm, w_out_g[l], tiles["ob_tn"], f"out_proj_bwd_l{l}", after=started_token)
        dwout.append(_weight_grad(sv["ycat"], dm.reshape(1, *dm.shape), tiles["wo_bm"], tiles["wo_bt"],
                                  f"w_out_grad_l{l}")[0])
        g, dz, db, drz, do, norm_acc = _mix_bwd_a(dycat, sv["u"], sv["o"], conv_full[l], conv_norm_w[l:l + 1],
                                                   ret_norm_w[l:l + 1], n_heads, nx, ncc, f"mix_bwd_a_l{l}")
        gf, gb = _grad_state_sweep(sv["qkv"], do, sv["tabs"], n_heads, nx, ncc, f"grad_state_sweep_l{l}")
        du, conv_acc, dlg = _mix_bwd_b(sv["u"], g, dz, db, drz, sv["qkv"], do, sv["sf"], sv["sb"],
                                       gf, gb, sv["tabs"], cos, sa, sb_tab, conv_full[l], n_heads, nx, ncc,
                                       f"mix_bwd_b_l{l}")
        gate_acc_l = gate_acc
        if l > 0:
            dhx = _in_proj_bwd(du, w_in_g[l], tiles["bwd_tm"], tiles["bwd_gs"], f"in_proj_bwd_l{l}")
            below = (saved[l - 1]["m"], mods[l - 1])
            dwin_l, dxt, pre_acc, dm, gate_acc = _weight_grad_beside_prenorm_bwd(
                sv["hx"], du, dhx, sv["xt"], dxt, norm_w[l:l + 1], mods[l], below, n_lat, f"w_in_grad_l{l}")
        else:
            dwin_l = _weight_grad(sv["hx"], du, tiles["wg_bm"], tiles["wg_bt"], f"w_in_grad_l{l}")
        srcs = [dwin_l, dwout[-1].reshape(N_DEV, r_out, d)]
        lands = [_landing(lax.dynamic_index_in_dim(a, me, axis=0, keepdims=False), me) for a in srcs]
        started = _push_start(srcs, lands, "scatter", f"grads_start_l{l}")
        dwin.append(started[:4])
        started_token = started[4:]
        if l == 0:
            dhx = _in_proj_bwd(du, w_in_g[l], tiles["bwd_tm"], tiles["bwd_gs"], f"in_proj_bwd_l{l}", after=started[4:])
            dxt, pre_acc = _prenorm_bwd_first(dhx, x[0], ctx[0], dxt, norm_w[l:l + 1], mods[l], f"prenorm_bwd_l{l}")
        dmod_x.append(jnp.concatenate([pre_acc[0], pre_acc[1], gate_acc_l[2]]))
        dmod_c.append(jnp.concatenate([pre_acc[3], pre_acc[4], gate_acc_l[5]]))
        dnw.append(pre_acc[6])
        dcnw.append(norm_acc[0])
        dgnw.append(norm_acc[1])
        dconv.append(conv_acc[0:3])
        ddec.append(dlg[0:2, :n_heads])
    for lst in (dmod_x, dmod_c, dnw, dcnw, dgnw, dconv, ddec, dwin, dwout):
        lst.reverse()
    grad_x = dxt.reshape(1, n_lat, d)

    rows = []
    for l in range(depth):
        rows += [dmod_x[l], dmod_c[l]]
    (dmod_g,) = _all_gather([_pad_rows(jnp.stack(rows, axis=0), 8)], "gather_dmod")
    dmod_g = dmod_g.reshape(N_DEV, 8, 3 * d)
    mine_cols = lax.dynamic_slice(dmod_g, (0, 0, me * n_mod), (N_DEV, 8, n_mod))
    g_wmod, dcc = [], jnp.zeros((d,), F32)
    for l in range(depth):
        gw, dc_part = _mod_grads(mine_cols[:, 2 * l], mine_cols[:, 2 * l + 1], c9, w_mod[l], f"mod_grads_l{l}")
        g_wmod.append(gw)
        dcc = dcc + dc_part[0]

    n_small = 16
    small = jnp.concatenate([
        jnp.stack(dnw, axis=0),
        jnp.concatenate(dcnw).reshape(1, -1),
        jnp.concatenate(dgnw).reshape(1, -1),
        dfnw[0:1],
        dcc.reshape(1, d),
        jnp.stack(dconv, axis=0).reshape(-1, d),
        _pad_cols(jnp.stack(ddec, axis=0).reshape(1, -1), d),
    ], axis=0)
    assert depth * s == d and small.shape[0] < n_small
    n_rows = small.shape[0]
    small = jnp.concatenate([small, _pad_cols(loss_blk[0:1], d)], axis=0)
    (small_g,) = _all_gather([_pad_rows(small, n_small)], "gather_small")
    small_g = small_g.reshape(N_DEV, n_small, d)

    def pack_small(nw_, cn_, gn_, fn_, cc_, df_, db_):
        return _pad_rows(jnp.concatenate([
            nw_, cn_.reshape(1, -1), gn_.reshape(1, -1), fn_.reshape(1, d), cc_.reshape(1, d),
            jnp.zeros((n_rows - depth - 5, d), F32),
            _pad_cols(jnp.stack([df_, db_], axis=1).reshape(1, -1), d)], axis=0), n_small)

    w_s = pack_small(norm_w, conv_norm_w, ret_norm_w, final_norm_w, c_ctx, ret_decay_f, ret_decay_b)
    m_s = pack_small(m_norm_w, m_conv_norm_w, m_ret_norm_w, m_final_norm_w, m_c_ctx, m_ret_decay_f, m_ret_decay_b)
    v_s = pack_small(v_norm_w, v_conv_norm_w, v_ret_norm_w, v_final_norm_w, v_c_ctx, v_ret_decay_f, v_ret_decay_b)
    small_out = _sum_adamw(small_g, w_s, m_s, v_s, "adamw_small")
    loss = small_out[0][n_rows, 0]

    def unpack_small(a):
        nw_ = a[0:depth]
        cn_ = a[depth].reshape(depth, s)
        gn_ = a[depth + 1].reshape(depth, s)
        fn_ = a[depth + 2]
        cc_ = a[depth + 3]
        dd = a[n_rows - 1, :depth * 2 * n_heads].reshape(depth, 2, n_heads)
        return dict(c_ctx=cc_, norm_w=nw_, conv_norm_w=cn_, ret_norm_w=gn_, ret_decay_f=dd[:, 0], ret_decay_b=dd[:, 1],
                    final_norm_w=fn_)

    res = {}
    for kind, arr in zip(("grad", "delta", "m", "v"), small_out):
        for k_, val in unpack_small(arr).items():
            res[(kind, k_)] = val

    bm_parts = jnp.concatenate([dmod_g[:, 0:2 * depth:2].reshape(N_DEV, depth, 3 * d),
                                dmod_g[:, 1:2 * depth:2].reshape(N_DEV, depth, 3 * d)], axis=0)
    bm_parts = jnp.concatenate([bm_parts, jnp.zeros((2 * N_DEV, 8 - depth, 3 * d), F32)], axis=1)
    pad8 = lambda a: _pad_rows(a, 8)
    bm_out = _sum_adamw(bm_parts, pad8(b_mod), pad8(m_b_mod), pad8(v_b_mod), "adamw_b_mod")
    for kind, arr in zip(("grad", "delta", "m", "v"), bm_out):
        res[(kind, "b_mod")] = arr[:depth]

    conv_rows = small_g[:, depth + 4:depth + 4 + 3 * depth * s // d].reshape(N_DEV, depth * 3, s)
    conv_mine = lax.dynamic_slice(conv_rows, (0, 0, me * n_cw), (N_DEV, depth * 3, n_cw))
    conv_mine = jnp.concatenate([conv_mine, jnp.zeros((N_DEV, 8 - depth * 3, n_cw), F32)], axis=1)
    cw2 = lambda a: _pad_rows(a.reshape(depth * 3, n_cw), 8)
    cw_out = _sum_adamw(conv_mine, cw2(conv_w), cw2(m_conv_w), cw2(v_conv_w), "adamw_conv_w")
    for kind, arr in zip(("grad", "delta", "m", "v"), cw_out):
        res[(kind, "conv_w")] = arr[:depth * 3].reshape(depth, 3, n_cw)

    wm_out = _sum_adamw(jnp.stack(g_wmod, axis=0).reshape(1, depth * d, n_mod), w_mod.reshape(depth * d, n_mod),
                        m_w_mod.reshape(depth * d, n_mod), v_w_mod.reshape(depth * d, n_mod), "adamw_w_mod")
    for kind, arr in zip(("grad", "delta", "m", "v"), wm_out):
        res[(kind, "w_mod")] = arr.reshape(depth, d, n_mod)

    wi_out = wo_out = None
    after = wm_out[0]
    for l in reversed(range(depth)):
        win_parts, wout_parts = _push_wait(*dwin[l], "scatter", after, f"grads_wait_l{l}")
        wi_out = _sum_adamw(win_parts, w_in.reshape(depth * d, s), m_w_in.reshape(depth * d, s),
                            v_w_in.reshape(depth * d, s), f"adamw_w_in_l{l}", row0=l * d, into=wi_out)
        wo_out = _sum_adamw(wout_parts, w_out.reshape(depth * r_out, d), m_w_out.reshape(depth * r_out, d),
                            v_w_out.reshape(depth * r_out, d), f"adamw_w_out_l{l}", row0=l * r_out, into=wo_out)
        after = wo_out[0]
    for kind, arr in zip(("grad", "delta", "m", "v"), wi_out):
        res[(kind, "w_in")] = arr.reshape(depth, d, s)
    for kind, arr in zip(("grad", "delta", "m", "v"), wo_out):
        res[(kind, "w_out")] = arr.reshape(depth, r_out, d)

    order = ["c_ctx", "norm_w", "w_mod", "b_mod", "w_in", "conv_w", "conv_norm_w", "ret_norm_w", "ret_decay_f",
             "ret_decay_b", "w_out", "final_norm_w"]
    outs = [loss, grad_x]
    for kind in ("grad", "delta", "m", "v"):
        outs += [res[(kind, k_)] for k_ in order]
    return tuple(outs)
```
